```python
import math
import jax, jax.numpy as jnp
from jax import lax
import numpy as np

D_MODEL = 1024
BATCH = 8
SEQ = 8192
DEPTH = 1

MEM_LEN = 256
EPS = 1e-6
CONV_WIDTH = 4
SSD_EXPAND = 2
SSD_WIDTH = SSD_EXPAND * D_MODEL
SSD_HEAD_DIM = 64
SSD_HEADS = SSD_WIDTH // SSD_HEAD_DIM
SSD_GROUPS = 4
SSD_STATE = 128
SSD_CHUNK = 128
SSD_CONV_CH = SSD_WIDTH + 2 * SSD_GROUPS * SSD_STATE
LRU_WIDTH = 3 * D_MODEL // 2
LRU_BLOCKS = 16
LRU_BLOCK = LRU_WIDTH // LRU_BLOCKS
LRU_C = 8.0
MEM_HEADS = 4
MEM_HEAD_DIM = D_MODEL // MEM_HEADS
N_BRANCH = 3
SPLIT_POINTS = (
    SSD_WIDTH,
    SSD_WIDTH + SSD_CONV_CH,
    SSD_WIDTH + SSD_CONV_CH + SSD_HEADS,
    SSD_WIDTH + SSD_CONV_CH + SSD_HEADS + LRU_WIDTH,
    SSD_WIDTH + SSD_CONV_CH + SSD_HEADS + 2 * LRU_WIDTH,
    SSD_WIDTH + SSD_CONV_CH + SSD_HEADS + 2 * LRU_WIDTH + D_MODEL,
)
IN_WIDTH = SSD_WIDTH + SSD_CONV_CH + SSD_HEADS + 2 * LRU_WIDTH + D_MODEL + N_BRANCH * D_MODEL

kernel_name = "hybrid_ssd_rglru_memxattn_gated_block"


def rms_norm(x, g):
    xf = x.astype(jnp.float32)
    y = xf * lax.rsqrt(jnp.mean(xf * xf, axis=-1, keepdims=True) + EPS)
    return (y * g.astype(jnp.float32)).astype(x.dtype)


def causal_dwconv(x, w, b):
    k = w.shape[0]
    y = lax.conv_general_dilated(
        x, w[:, None, :].astype(x.dtype), window_strides=(1,), padding=[(k - 1, 0)],
        dimension_numbers=('NWC', 'WIO', 'NWC'), feature_group_count=x.shape[-1])
    return y + b


def ssd_scan(x, dt, a_neg, b_in, c_in):
    f32 = jnp.float32
    bsz, s, h, p = x.shape
    g, n = b_in.shape[2], b_in.shape[3]
    k = h // g
    l = SSD_CHUNK
    nc = s // l
    x = x.astype(f32).reshape(bsz, nc, l, g, k, p)
    dt = dt.astype(f32).reshape(bsz, nc, l, g, k)
    bm = b_in.astype(f32).reshape(bsz, nc, l, g, n)
    cm = c_in.astype(f32).reshape(bsz, nc, l, g, n)
    xdt = x * dt[..., None]
    a_cs = jnp.cumsum(dt * a_neg.astype(f32).reshape(g, k), axis=2)
    causal = jnp.tril(jnp.ones((l, l), dtype=bool))
    seg = a_cs[:, :, :, None] - a_cs[:, :, None, :]
    decay = jnp.exp(jnp.where(causal[:, :, None, None], seg, -jnp.inf))
    cb = jnp.einsum('bclgn,bcsgn->bclsg', cm, bm)
    y_diag = jnp.einsum('bclsgk,bcsgkp->bclgkp', decay * cb[..., None], xdt)
    decay_to_end = jnp.exp(a_cs[:, :, -1:] - a_cs)
    states = jnp.einsum('bclgn,bclgkp->bcgkpn', bm, xdt * decay_to_end[..., None])
    chunk_decay = jnp.exp(a_cs[:, :, -1])

    def step(carry, inp):
        st, dec = inp
        return carry * dec[..., None, None] + st, carry

    init = jnp.zeros((bsz, g, k, p, n), f32)
    _, prev = lax.scan(step, init, (jnp.moveaxis(states, 1, 0), jnp.moveaxis(chunk_decay, 1, 0)))
    prev = jnp.moveaxis(prev, 0, 1)
    y_off = jnp.einsum('bclgn,bcgkpn->bclgkp', cm, prev) * jnp.exp(a_cs)[..., None]
    return (y_diag + y_off).reshape(bsz, s, h, p)


def rg_lru(x, w_a, b_a, w_x, b_x, lam):
    f32 = jnp.float32
    bsz, s, w = x.shape
    xb = x.reshape(bsz, s, LRU_BLOCKS, LRU_BLOCK)
    r = jax.nn.sigmoid(jnp.einsum('bsni,nij->bsnj', xb, w_a) + b_a).reshape(bsz, s, w)
    i = jax.nn.sigmoid(jnp.einsum('bsni,nij->bsnj', xb, w_x) + b_x).reshape(bsz, s, w)
    log_a = (-LRU_C * r.astype(f32)) * jax.nn.softplus(-lam.astype(f32))
    a = jnp.exp(log_a)
    mult = jnp.sqrt(-jnp.expm1(2.0 * log_a))
    u = mult * (i * x).astype(f32)

    def combine(left, right):
        a1, b1 = left
        a2, b2 = right
        return a1 * a2, a2 * b1 + b2

    _, hs = lax.associative_scan(combine, (a, u), axis=1)
    return hs.astype(x.dtype)


def memory_attention(q, mem_n, w_kv):
    bsz, s, _ = q.shape
    m = mem_n.shape[1]
    kv = mem_n @ w_kv
    k, v = jnp.split(kv, 2, axis=-1)
    q = q.reshape(bsz, s, MEM_HEADS, MEM_HEAD_DIM)
    k = k.reshape(bsz, m, MEM_HEADS, MEM_HEAD_DIM)
    v = v.reshape(bsz, m, MEM_HEADS, MEM_HEAD_DIM)
    scores = jnp.einsum('bshd,bmhd->bhsm', q, k).astype(jnp.float32) * (MEM_HEAD_DIM ** -0.5)
    probs = jax.nn.softmax(scores, axis=-1).astype(v.dtype)
    return jnp.einsum('bhsm,bmhd->bshd', probs, v).reshape(bsz, s, D_MODEL)


def _fwd_setup_inputs(seed: int = 0) -> dict:
    key = jax.random.key(seed)
    ks = jax.random.split(key, 24)
    f32 = jnp.float32
    nrm = lambda k, shape, scale: jax.random.normal(k, shape, f32) * scale
    x = jax.random.normal(ks[0], (BATCH, SEQ, D_MODEL), f32)
    mem = jax.random.normal(ks[1], (BATCH, MEM_LEN, D_MODEL), f32)
    norm_g = 1.0 + nrm(ks[2], (DEPTH, D_MODEL), 0.02)
    w_in = nrm(ks[3], (DEPTH, D_MODEL, IN_WIDTH), D_MODEL ** -0.5)
    ssd_conv_w = nrm(ks[4], (DEPTH, CONV_WIDTH, SSD_CONV_CH), CONV_WIDTH ** -0.5)
    ssd_conv_b = nrm(ks[5], (DEPTH, SSD_CONV_CH), 0.02)
    dt0 = jnp.exp(jax.random.uniform(ks[6], (DEPTH, SSD_HEADS), f32, math.log(1e-3), math.log(1e-1)))
    ssd_dt_bias = dt0 + jnp.log(-jnp.expm1(-dt0))
    ssd_a_log = jnp.log(jax.random.uniform(ks[7], (DEPTH, SSD_HEADS), f32, 1.0, 16.0))
    ssd_d = 1.0 + nrm(ks[8], (DEPTH, SSD_HEADS), 0.02)
    ssd_norm_g = 1.0 + nrm(ks[9], (DEPTH, SSD_GROUPS, SSD_WIDTH // SSD_GROUPS), 0.02)
    lru_conv_w = nrm(ks[10], (DEPTH, CONV_WIDTH, LRU_WIDTH), CONV_WIDTH ** -0.5)
    lru_conv_b = nrm(ks[11], (DEPTH, LRU_WIDTH), 0.02)
    lru_w_a = nrm(ks[12], (DEPTH, LRU_BLOCKS, LRU_BLOCK, LRU_BLOCK), LRU_BLOCK ** -0.5)
    lru_b_a = nrm(ks[13], (DEPTH, LRU_BLOCKS, LRU_BLOCK), 0.02)
    lru_w_x = nrm(ks[14], (DEPTH, LRU_BLOCKS, LRU_BLOCK, LRU_BLOCK), LRU_BLOCK ** -0.5)
    lru_b_x = nrm(ks[15], (DEPTH, LRU_BLOCKS, LRU_BLOCK), 0.02)
    a8 = jax.random.uniform(ks[16], (DEPTH, LRU_WIDTH), f32, 0.9, 0.999)
    sig = a8 ** (1.0 / LRU_C)
    lru_lambda = jnp.log(sig) - jnp.log1p(-sig)
    mem_norm_g = 1.0 + nrm(ks[17], (DEPTH, D_MODEL), 0.02)
    w_kv = nrm(ks[18], (DEPTH, D_MODEL, 2 * D_MODEL), D_MODEL ** -0.5)
    w_br_ssd = nrm(ks[19], (DEPTH, SSD_WIDTH, D_MODEL), SSD_WIDTH ** -0.5)
    w_br_lru = nrm(ks[20], (DEPTH, LRU_WIDTH, D_MODEL), LRU_WIDTH ** -0.5)
    w_br_mem = nrm(ks[21], (DEPTH, D_MODEL, D_MODEL), D_MODEL ** -0.5)
    w_out = nrm(ks[22], (DEPTH, D_MODEL, D_MODEL), D_MODEL ** -0.5)
    final_g = 1.0 + nrm(ks[23], (D_MODEL,), 0.02)
    return {"x": x, "mem": mem, "norm_g": norm_g, "w_in": w_in,
            "ssd_conv_w": ssd_conv_w, "ssd_conv_b": ssd_conv_b, "ssd_dt_bias": ssd_dt_bias,
            "ssd_a_log": ssd_a_log, "ssd_d": ssd_d, "ssd_norm_g": ssd_norm_g,
            "lru_conv_w": lru_conv_w, "lru_conv_b": lru_conv_b, "lru_w_a": lru_w_a,
            "lru_b_a": lru_b_a, "lru_w_x": lru_w_x, "lru_b_x": lru_b_x, "lru_lambda": lru_lambda,
            "mem_norm_g": mem_norm_g, "w_kv": w_kv, "w_br_ssd": w_br_ssd, "w_br_lru": w_br_lru,
            "w_br_mem": w_br_mem, "w_out": w_out, "final_g": final_g}


def _fwd_reference(x, mem, norm_g, w_in, ssd_conv_w, ssd_conv_b, ssd_dt_bias, ssd_a_log, ssd_d,
              ssd_norm_g, lru_conv_w, lru_conv_b, lru_w_a, lru_b_a, lru_w_x, lru_b_x, lru_lambda,
              mem_norm_g, w_kv, w_br_ssd, w_br_lru, w_br_mem, w_out, final_g):
    bsz, s, _ = x.shape
    for l in range(DEPTH):
        h = rms_norm(x, norm_g[l])
        proj = h @ w_in[l]
        z, xbc, dt_raw, lru_gate, lru_x, q, gate_logits = jnp.split(proj, SPLIT_POINTS, axis=-1)

        xbc = jax.nn.silu(causal_dwconv(xbc, ssd_conv_w[l], ssd_conv_b[l]))
        xs, bs, cs = jnp.split(xbc, [SSD_WIDTH, SSD_WIDTH + SSD_GROUPS * SSD_STATE], axis=-1)
        xs = xs.reshape(bsz, s, SSD_HEADS, SSD_HEAD_DIM)
        dt = jax.nn.softplus((dt_raw + ssd_dt_bias[l]).astype(jnp.float32))
        y = ssd_scan(xs, dt, -jnp.exp(ssd_a_log[l].astype(jnp.float32)),
                     bs.reshape(bsz, s, SSD_GROUPS, SSD_STATE), cs.reshape(bsz, s, SSD_GROUPS, SSD_STATE))
        y = (y + xs.astype(jnp.float32) * ssd_d[l][:, None].astype(jnp.float32)).astype(x.dtype)
        y = y.reshape(bsz, s, SSD_WIDTH) * jax.nn.silu(z)
        y_ssd = rms_norm(y.reshape(bsz, s, SSD_GROUPS, SSD_WIDTH // SSD_GROUPS),
                         ssd_norm_g[l]).reshape(bsz, s, SSD_WIDTH)

        xl = causal_dwconv(lru_x, lru_conv_w[l], lru_conv_b[l])
        y_lru = rg_lru(xl, lru_w_a[l], lru_b_a[l], lru_w_x[l], lru_b_x[l], lru_lambda[l]) * jax.nn.silu(lru_gate)

        mem_n = rms_norm(mem, mem_norm_g[l])
        y_mem = memory_attention(q, mem_n, w_kv[l])

        gates = jax.nn.sigmoid(gate_logits).reshape(bsz, s, N_BRANCH, D_MODEL)
        merged = (gates[:, :, 0] * (y_ssd @ w_br_ssd[l])
                  + gates[:, :, 1] * (y_lru @ w_br_lru[l])
                  + gates[:, :, 2] * (y_mem @ w_br_mem[l]))
        x = x + merged @ w_out[l]
    return rms_norm(x, final_g)


import jax as _jax
import jax.numpy as _jnp

TWIN_FORMAT = 'train_step'
FWD_PARAMS = ['x', 'mem', 'norm_g', 'w_in', 'ssd_conv_w', 'ssd_conv_b', 'ssd_dt_bias', 'ssd_a_log', 'ssd_d', 'ssd_norm_g', 'lru_conv_w', 'lru_conv_b', 'lru_w_a', 'lru_b_a', 'lru_w_x', 'lru_b_x', 'lru_lambda', 'mem_norm_g', 'w_kv', 'w_br_ssd', 'w_br_lru', 'w_br_mem', 'w_out', 'final_g']
TWIN_WEIGHTS = ['norm_g', 'w_in', 'ssd_conv_w', 'ssd_conv_b', 'ssd_dt_bias', 'ssd_a_log', 'ssd_d', 'ssd_norm_g', 'lru_conv_w', 'lru_conv_b', 'lru_w_a', 'lru_b_a', 'lru_w_x', 'lru_b_x', 'lru_lambda', 'mem_norm_g', 'w_kv', 'w_br_ssd', 'w_br_lru', 'w_br_mem', 'w_out', 'final_g']
TWIN_DIFF_INPUT = 'x'
TWIN_INPUTS = ['x', 'mem', 'norm_g', 'w_in', 'ssd_conv_w', 'ssd_conv_b', 'ssd_dt_bias', 'ssd_a_log', 'ssd_d', 'ssd_norm_g', 'lru_conv_w', 'lru_conv_b', 'lru_w_a', 'lru_b_a', 'lru_w_x', 'lru_b_x', 'lru_lambda', 'mem_norm_g', 'w_kv', 'w_br_ssd', 'w_br_lru', 'w_br_mem', 'w_out', 'final_g', 'loss_target', 'm_norm_g', 'm_w_in', 'm_ssd_conv_w', 'm_ssd_conv_b', 'm_ssd_dt_bias', 'm_ssd_a_log', 'm_ssd_d', 'm_ssd_norm_g', 'm_lru_conv_w', 'm_lru_conv_b', 'm_lru_w_a', 'm_lru_b_a', 'm_lru_w_x', 'm_lru_b_x', 'm_lru_lambda', 'm_mem_norm_g', 'm_w_kv', 'm_w_br_ssd', 'm_w_br_lru', 'm_w_br_mem', 'm_w_out', 'm_final_g', 'v_norm_g', 'v_w_in', 'v_ssd_conv_w', 'v_ssd_conv_b', 'v_ssd_dt_bias', 'v_ssd_a_log', 'v_ssd_d', 'v_ssd_norm_g', 'v_lru_conv_w', 'v_lru_conv_b', 'v_lru_w_a', 'v_lru_b_a', 'v_lru_w_x', 'v_lru_b_x', 'v_lru_lambda', 'v_mem_norm_g', 'v_w_kv', 'v_w_br_ssd', 'v_w_br_lru', 'v_w_br_mem', 'v_w_out', 'v_final_g']
TWIN_OUTPUTS = ['loss', 'grad_x', 'grad_norm_g', 'grad_w_in', 'grad_ssd_conv_w', 'grad_ssd_conv_b', 'grad_ssd_dt_bias', 'grad_ssd_a_log', 'grad_ssd_d', 'grad_ssd_norm_g', 'grad_lru_conv_w', 'grad_lru_conv_b', 'grad_lru_w_a', 'grad_lru_b_a', 'grad_lru_w_x', 'grad_lru_b_x', 'grad_lru_lambda', 'grad_mem_norm_g', 'grad_w_kv', 'grad_w_br_ssd', 'grad_w_br_lru', 'grad_w_br_mem', 'grad_w_out', 'grad_final_g', 'delta_norm_g', 'delta_w_in', 'delta_ssd_conv_w', 'delta_ssd_conv_b', 'delta_ssd_dt_bias', 'delta_ssd_a_log', 'delta_ssd_d', 'delta_ssd_norm_g', 'delta_lru_conv_w', 'delta_lru_conv_b', 'delta_lru_w_a', 'delta_lru_b_a', 'delta_lru_w_x', 'delta_lru_b_x', 'delta_lru_lambda', 'delta_mem_norm_g', 'delta_w_kv', 'delta_w_br_ssd', 'delta_w_br_lru', 'delta_w_br_mem', 'delta_w_out', 'delta_final_g', 'new_m_norm_g', 'new_m_w_in', 'new_m_ssd_conv_w', 'new_m_ssd_conv_b', 'new_m_ssd_dt_bias', 'new_m_ssd_a_log', 'new_m_ssd_d', 'new_m_ssd_norm_g', 'new_m_lru_conv_w', 'new_m_lru_conv_b', 'new_m_lru_w_a', 'new_m_lru_b_a', 'new_m_lru_w_x', 'new_m_lru_b_x', 'new_m_lru_lambda', 'new_m_mem_norm_g', 'new_m_w_kv', 'new_m_w_br_ssd', 'new_m_w_br_lru', 'new_m_w_br_mem', 'new_m_w_out', 'new_m_final_g', 'new_v_norm_g', 'new_v_w_in', 'new_v_ssd_conv_w', 'new_v_ssd_conv_b', 'new_v_ssd_dt_bias', 'new_v_ssd_a_log', 'new_v_ssd_d', 'new_v_ssd_norm_g', 'new_v_lru_conv_w', 'new_v_lru_conv_b', 'new_v_lru_w_a', 'new_v_lru_b_a', 'new_v_lru_w_x', 'new_v_lru_b_x', 'new_v_lru_lambda', 'new_v_mem_norm_g', 'new_v_w_kv', 'new_v_w_br_ssd', 'new_v_w_br_lru', 'new_v_w_br_mem', 'new_v_w_out', 'new_v_final_g']
TWIN_LEAF_KINDS = {'loss': 'loss', 'grad_x': 'grad_x', 'grad_norm_g': 'grad_w', 'grad_w_in': 'grad_w', 'grad_ssd_conv_w': 'grad_w', 'grad_ssd_conv_b': 'grad_w', 'grad_ssd_dt_bias': 'grad_w', 'grad_ssd_a_log': 'grad_w', 'grad_ssd_d': 'grad_w', 'grad_ssd_norm_g': 'grad_w', 'grad_lru_conv_w': 'grad_w', 'grad_lru_conv_b': 'grad_w', 'grad_lru_w_a': 'grad_w', 'grad_lru_b_a': 'grad_w', 'grad_lru_w_x': 'grad_w', 'grad_lru_b_x': 'grad_w', 'grad_lru_lambda': 'grad_w', 'grad_mem_norm_g': 'grad_w', 'grad_w_kv': 'grad_w', 'grad_w_br_ssd': 'grad_w', 'grad_w_br_lru': 'grad_w', 'grad_w_br_mem': 'grad_w', 'grad_w_out': 'grad_w', 'grad_final_g': 'grad_w', 'delta_norm_g': 'delta_w', 'delta_w_in': 'delta_w', 'delta_ssd_conv_w': 'delta_w', 'delta_ssd_conv_b': 'delta_w', 'delta_ssd_dt_bias': 'delta_w', 'delta_ssd_a_log': 'delta_w', 'delta_ssd_d': 'delta_w', 'delta_ssd_norm_g': 'delta_w', 'delta_lru_conv_w': 'delta_w', 'delta_lru_conv_b': 'delta_w', 'delta_lru_w_a': 'delta_w', 'delta_lru_b_a': 'delta_w', 'delta_lru_w_x': 'delta_w', 'delta_lru_b_x': 'delta_w', 'delta_lru_lambda': 'delta_w', 'delta_mem_norm_g': 'delta_w', 'delta_w_kv': 'delta_w', 'delta_w_br_ssd': 'delta_w', 'delta_w_br_lru': 'delta_w', 'delta_w_br_mem': 'delta_w', 'delta_w_out': 'delta_w', 'delta_final_g': 'delta_w', 'new_m_norm_g': 'new_m', 'new_m_w_in': 'new_m', 'new_m_ssd_conv_w': 'new_m', 'new_m_ssd_conv_b': 'new_m', 'new_m_ssd_dt_bias': 'new_m', 'new_m_ssd_a_log': 'new_m', 'new_m_ssd_d': 'new_m', 'new_m_ssd_norm_g': 'new_m', 'new_m_lru_conv_w': 'new_m', 'new_m_lru_conv_b': 'new_m', 'new_m_lru_w_a': 'new_m', 'new_m_lru_b_a': 'new_m', 'new_m_lru_w_x': 'new_m', 'new_m_lru_b_x': 'new_m', 'new_m_lru_lambda': 'new_m', 'new_m_mem_norm_g': 'new_m', 'new_m_w_kv': 'new_m', 'new_m_w_br_ssd': 'new_m', 'new_m_w_br_lru': 'new_m', 'new_m_w_br_mem': 'new_m', 'new_m_w_out': 'new_m', 'new_m_final_g': 'new_m', 'new_v_norm_g': 'new_v', 'new_v_w_in': 'new_v', 'new_v_ssd_conv_w': 'new_v', 'new_v_ssd_conv_b': 'new_v', 'new_v_ssd_dt_bias': 'new_v', 'new_v_ssd_a_log': 'new_v', 'new_v_ssd_d': 'new_v', 'new_v_ssd_norm_g': 'new_v', 'new_v_lru_conv_w': 'new_v', 'new_v_lru_conv_b': 'new_v', 'new_v_lru_w_a': 'new_v', 'new_v_lru_b_a': 'new_v', 'new_v_lru_w_x': 'new_v', 'new_v_lru_b_x': 'new_v', 'new_v_lru_lambda': 'new_v', 'new_v_mem_norm_g': 'new_v', 'new_v_w_kv': 'new_v', 'new_v_w_br_ssd': 'new_v', 'new_v_w_br_lru': 'new_v', 'new_v_w_br_mem': 'new_v', 'new_v_w_out': 'new_v', 'new_v_final_g': 'new_v'}


def _forward(args):
    return _fwd_reference(*[args[k] for k in FWD_PARAMS])


def _output_shape():
    def fwd():
        inp = _fwd_setup_inputs(0)
        return _fwd_reference(*[inp[k] for k in FWD_PARAMS])
    out = _jax.eval_shape(fwd)
    return out.shape, out.dtype

N_MICROBATCH = 1
ADAM_LR = 0.001
ADAM_B1 = 0.9
ADAM_B2 = 0.999
ADAM_EPS = 1e-08
ADAM_WD = 0.01
ADAM_STEP = 10
PER_EXAMPLE_BATCH_AXIS = {'x': 0, 'mem': 0, 'loss_target': 0}
SHARED_INPUTS = []
_WEIGHT_DTYPES = {'norm_g': _jnp.float32, 'w_in': _jnp.float32, 'ssd_conv_w': _jnp.float32, 'ssd_conv_b': _jnp.float32, 'ssd_dt_bias': _jnp.float32, 'ssd_a_log': _jnp.float32, 'ssd_d': _jnp.float32, 'ssd_norm_g': _jnp.float32, 'lru_conv_w': _jnp.float32, 'lru_conv_b': _jnp.float32, 'lru_w_a': _jnp.float32, 'lru_b_a': _jnp.float32, 'lru_w_x': _jnp.float32, 'lru_b_x': _jnp.float32, 'lru_lambda': _jnp.float32, 'mem_norm_g': _jnp.float32, 'w_kv': _jnp.float32, 'w_br_ssd': _jnp.float32, 'w_br_lru': _jnp.float32, 'w_br_mem': _jnp.float32, 'w_out': _jnp.float32, 'final_g': _jnp.float32}
MOMENT_SCALE = {'norm_g': 1.981857e-01, 'w_in': 5.546567e-02, 'ssd_conv_w': 7.170866e-02, 'ssd_conv_b': 9.815045e-02, 'ssd_dt_bias': 1.952226e-01, 'ssd_a_log': 1.811804e-01, 'ssd_d': 4.803757e-01, 'ssd_norm_g': 8.543680e-02, 'lru_conv_w': 3.685614e-02, 'lru_conv_b': 4.328370e-01, 'lru_w_a': 1.406413e-02, 'lru_b_a': 1.175835e-02, 'lru_w_x': 2.530043e-02, 'lru_b_x': 1.237438e-02, 'lru_lambda': 2.133207e-02, 'mem_norm_g': 1.918850e-02, 'w_kv': 1.242564e-02, 'w_br_ssd': 1.175430e-01, 'w_br_lru': 5.054199e-02, 'w_br_mem': 1.260346e-02, 'w_out': 1.283351e-01, 'final_g': 6.397911e+01}


def _to_microbatches(a, axis):
    t = _jnp.moveaxis(a, axis, 0)
    t = t.reshape((N_MICROBATCH, t.shape[0] // N_MICROBATCH) + t.shape[1:])
    return _jnp.moveaxis(t, 1, axis + 1)


def setup_inputs(seed: int = 0) -> dict:
    inp = _fwd_setup_inputs(seed)
    key = _jax.random.fold_in(_jax.random.key(seed), 7919)
    shape, _ = _output_shape()
    out = dict(inp)
    out["loss_target"] = _jax.random.normal(_jax.random.fold_in(key, 0), shape, _jnp.float32)
    for i, name in enumerate(TWIN_WEIGHTS):
        w = inp[name].astype(_jnp.float32)
        if MOMENT_SCALE is None:
            s = _jnp.sqrt(_jnp.mean(_jnp.square(w)) + 1e-30)
        else:
            s = MOMENT_SCALE[name]
        km, kv = _jax.random.split(_jax.random.fold_in(key, i + 1))
        out[name] = w
        out["m_" + name] = s * _jax.random.normal(km, w.shape, _jnp.float32)
        out["v_" + name] = (s * s) * _jax.random.uniform(kv, w.shape, _jnp.float32, 0.5, 1.5)
    if N_MICROBATCH > 1:
        for name, axis in PER_EXAMPLE_BATCH_AXIS.items():
            out[name] = _to_microbatches(out[name], axis)
    return {'x': out['x'], 'mem': out['mem'], 'norm_g': out['norm_g'], 'w_in': out['w_in'], 'ssd_conv_w': out['ssd_conv_w'], 'ssd_conv_b': out['ssd_conv_b'], 'ssd_dt_bias': out['ssd_dt_bias'], 'ssd_a_log': out['ssd_a_log'], 'ssd_d': out['ssd_d'], 'ssd_norm_g': out['ssd_norm_g'], 'lru_conv_w': out['lru_conv_w'], 'lru_conv_b': out['lru_conv_b'], 'lru_w_a': out['lru_w_a'], 'lru_b_a': out['lru_b_a'], 'lru_w_x': out['lru_w_x'], 'lru_b_x': out['lru_b_x'], 'lru_lambda': out['lru_lambda'], 'mem_norm_g': out['mem_norm_g'], 'w_kv': out['w_kv'], 'w_br_ssd': out['w_br_ssd'], 'w_br_lru': out['w_br_lru'], 'w_br_mem': out['w_br_mem'], 'w_out': out['w_out'], 'final_g': out['final_g'], 'loss_target': out['loss_target'], 'm_norm_g': out['m_norm_g'], 'm_w_in': out['m_w_in'], 'm_ssd_conv_w': out['m_ssd_conv_w'], 'm_ssd_conv_b': out['m_ssd_conv_b'], 'm_ssd_dt_bias': out['m_ssd_dt_bias'], 'm_ssd_a_log': out['m_ssd_a_log'], 'm_ssd_d': out['m_ssd_d'], 'm_ssd_norm_g': out['m_ssd_norm_g'], 'm_lru_conv_w': out['m_lru_conv_w'], 'm_lru_conv_b': out['m_lru_conv_b'], 'm_lru_w_a': out['m_lru_w_a'], 'm_lru_b_a': out['m_lru_b_a'], 'm_lru_w_x': out['m_lru_w_x'], 'm_lru_b_x': out['m_lru_b_x'], 'm_lru_lambda': out['m_lru_lambda'], 'm_mem_norm_g': out['m_mem_norm_g'], 'm_w_kv': out['m_w_kv'], 'm_w_br_ssd': out['m_w_br_ssd'], 'm_w_br_lru': out['m_w_br_lru'], 'm_w_br_mem': out['m_w_br_mem'], 'm_w_out': out['m_w_out'], 'm_final_g': out['m_final_g'], 'v_norm_g': out['v_norm_g'], 'v_w_in': out['v_w_in'], 'v_ssd_conv_w': out['v_ssd_conv_w'], 'v_ssd_conv_b': out['v_ssd_conv_b'], 'v_ssd_dt_bias': out['v_ssd_dt_bias'], 'v_ssd_a_log': out['v_ssd_a_log'], 'v_ssd_d': out['v_ssd_d'], 'v_ssd_norm_g': out['v_ssd_norm_g'], 'v_lru_conv_w': out['v_lru_conv_w'], 'v_lru_conv_b': out['v_lru_conv_b'], 'v_lru_w_a': out['v_lru_w_a'], 'v_lru_b_a': out['v_lru_b_a'], 'v_lru_w_x': out['v_lru_w_x'], 'v_lru_b_x': out['v_lru_b_x'], 'v_lru_lambda': out['v_lru_lambda'], 'v_mem_norm_g': out['v_mem_norm_g'], 'v_w_kv': out['v_w_kv'], 'v_w_br_ssd': out['v_w_br_ssd'], 'v_w_br_lru': out['v_w_br_lru'], 'v_w_br_mem': out['v_w_br_mem'], 'v_w_out': out['v_w_out'], 'v_final_g': out['v_final_g']}


def _loss(weights, diff, rest, loss_target):
    with _jax.named_scope("forward"):
        args = {**rest, TWIN_DIFF_INPUT: diff, **{k: w.astype(_WEIGHT_DTYPES[k]) for k, w in weights.items()}}
        y = _forward(args)
    with _jax.named_scope("loss_head"):
        err = _jnp.square(y.astype(_jnp.float32) - loss_target)
        return 0.5 * _jnp.sum(_jnp.mean(err, axis=-1)) if err.ndim else 0.5 * err


def _adamw(w, g, m, v):
    m = ADAM_B1 * m + (1.0 - ADAM_B1) * g
    v = ADAM_B2 * v + (1.0 - ADAM_B2) * _jnp.square(g)
    m_hat = m / (1.0 - ADAM_B1 ** ADAM_STEP)
    v_hat = v / (1.0 - ADAM_B2 ** ADAM_STEP)
    delta = -ADAM_LR * (m_hat / (_jnp.sqrt(v_hat) + ADAM_EPS) + ADAM_WD * w)
    return delta, m, v


def reference(x, mem, norm_g, w_in, ssd_conv_w, ssd_conv_b, ssd_dt_bias, ssd_a_log, ssd_d, ssd_norm_g, lru_conv_w, lru_conv_b, lru_w_a, lru_b_a, lru_w_x, lru_b_x, lru_lambda, mem_norm_g, w_kv, w_br_ssd, w_br_lru, w_br_mem, w_out, final_g, loss_target, m_norm_g, m_w_in, m_ssd_conv_w, m_ssd_conv_b, m_ssd_dt_bias, m_ssd_a_log, m_ssd_d, m_ssd_norm_g, m_lru_conv_w, m_lru_conv_b, m_lru_w_a, m_lru_b_a, m_lru_w_x, m_lru_b_x, m_lru_lambda, m_mem_norm_g, m_w_kv, m_w_br_ssd, m_w_br_lru, m_w_br_mem, m_w_out, m_final_g, v_norm_g, v_w_in, v_ssd_conv_w, v_ssd_conv_b, v_ssd_dt_bias, v_ssd_a_log, v_ssd_d, v_ssd_norm_g, v_lru_conv_w, v_lru_conv_b, v_lru_w_a, v_lru_b_a, v_lru_w_x, v_lru_b_x, v_lru_lambda, v_mem_norm_g, v_w_kv, v_w_br_ssd, v_w_br_lru, v_w_br_mem, v_w_out, v_final_g):
    given = dict(x=x, mem=mem, norm_g=norm_g, w_in=w_in, ssd_conv_w=ssd_conv_w, ssd_conv_b=ssd_conv_b, ssd_dt_bias=ssd_dt_bias, ssd_a_log=ssd_a_log, ssd_d=ssd_d, ssd_norm_g=ssd_norm_g, lru_conv_w=lru_conv_w, lru_conv_b=lru_conv_b, lru_w_a=lru_w_a, lru_b_a=lru_b_a, lru_w_x=lru_w_x, lru_b_x=lru_b_x, lru_lambda=lru_lambda, mem_norm_g=mem_norm_g, w_kv=w_kv, w_br_ssd=w_br_ssd, w_br_lru=w_br_lru, w_br_mem=w_br_mem, w_out=w_out, final_g=final_g, loss_target=loss_target, m_norm_g=m_norm_g, m_w_in=m_w_in, m_ssd_conv_w=m_ssd_conv_w, m_ssd_conv_b=m_ssd_conv_b, m_ssd_dt_bias=m_ssd_dt_bias, m_ssd_a_log=m_ssd_a_log, m_ssd_d=m_ssd_d, m_ssd_norm_g=m_ssd_norm_g, m_lru_conv_w=m_lru_conv_w, m_lru_conv_b=m_lru_conv_b, m_lru_w_a=m_lru_w_a, m_lru_b_a=m_lru_b_a, m_lru_w_x=m_lru_w_x, m_lru_b_x=m_lru_b_x, m_lru_lambda=m_lru_lambda, m_mem_norm_g=m_mem_norm_g, m_w_kv=m_w_kv, m_w_br_ssd=m_w_br_ssd, m_w_br_lru=m_w_br_lru, m_w_br_mem=m_w_br_mem, m_w_out=m_w_out, m_final_g=m_final_g, v_norm_g=v_norm_g, v_w_in=v_w_in, v_ssd_conv_w=v_ssd_conv_w, v_ssd_conv_b=v_ssd_conv_b, v_ssd_dt_bias=v_ssd_dt_bias, v_ssd_a_log=v_ssd_a_log, v_ssd_d=v_ssd_d, v_ssd_norm_g=v_ssd_norm_g, v_lru_conv_w=v_lru_conv_w, v_lru_conv_b=v_lru_conv_b, v_lru_w_a=v_lru_w_a, v_lru_b_a=v_lru_b_a, v_lru_w_x=v_lru_w_x, v_lru_b_x=v_lru_b_x, v_lru_lambda=v_lru_lambda, v_mem_norm_g=v_mem_norm_g, v_w_kv=v_w_kv, v_w_br_ssd=v_w_br_ssd, v_w_br_lru=v_w_br_lru, v_w_br_mem=v_w_br_mem, v_w_out=v_w_out, v_final_g=v_final_g)
    weights = {n: given[n] for n in TWIN_WEIGHTS}
    shared = {n: given[n] for n in SHARED_INPUTS}
    per_example = {n: given[n] for n in ['x', 'mem']}
    grad_fn = _jax.value_and_grad(_loss, argnums=(0, 1))

    def one_microbatch(ex, loss_target):
        ex = dict(ex)
        diff = ex.pop(TWIN_DIFF_INPUT)
        return grad_fn(weights, diff, {**shared, **ex}, loss_target)

    if N_MICROBATCH == 1:
        loss, (grad_w, grad_x) = one_microbatch(per_example, given["loss_target"])
    else:
        def body(carry, xs):
            loss_sum, grad_sum = carry
            l_k, (gw_k, gx_k) = one_microbatch(xs[0], xs[1])
            with _jax.named_scope("update"):
                return (loss_sum + l_k, _jax.tree.map(_jnp.add, grad_sum, gw_k)), gx_k

        init = (_jnp.zeros((), _jnp.float32), _jax.tree.map(_jnp.zeros_like, weights))
        (loss, grad_w), grad_x = _jax.lax.scan(body, init, (per_example, given["loss_target"]))
    with _jax.named_scope("update"):
        delta_w, new_m, new_v = {}, {}, {}
        for n in TWIN_WEIGHTS:
            delta_w[n], new_m[n], new_v[n] = _adamw(weights[n], grad_w[n], given["m_" + n], given["v_" + n])
    return (loss, grad_x, *[grad_w[n] for n in TWIN_WEIGHTS], *[delta_w[n] for n in TWIN_WEIGHTS],
            *[new_m[n] for n in TWIN_WEIGHTS], *[new_v[n] for n in TWIN_WEIGHTS])
```

```python
import jax
import jax.numpy as jnp
from jax import lax
from jax.experimental import pallas as pl
from jax.experimental.pallas import tpu as pltpu

F32 = jnp.float32
_MXU = jnp.bfloat16
_HI = lax.Precision.HIGHEST
MESH = pl.DeviceIdType.MESH

D = 1024
EPS = 1e-6
MEM_HEADS = 4
MEM_HD = 256
LRU_C = 8.0
SSD_L = 128
SSD_W = 2048
LRU_W = 1536
NSHARD = 4

XBC = (0, 3072)
GL = (3072, 3072)
Z = (6144, 2048)
Q = (8192, 1024)
LG = (9216, 1536)
LX = (10752, 1536)
DT = (12288, 512)
NP = 12800

ADAM_LR = 0.001
ADAM_B1 = 0.9
ADAM_B2 = 0.999
ADAM_EPS = 1e-08
ADAM_WD = 0.01
ADAM_STEP = 10

VMEM_LIMIT = 56 * 1024 * 1024

SHARDED = ("w_in", "ssd_conv_w", "ssd_norm_g", "lru_conv_w", "w_kv", "w_br_ssd", "w_br_lru", "w_br_mem", "w_out")
SHARD_AXIS = {"w_in": 1, "ssd_conv_w": 1, "ssd_norm_g": 1, "lru_conv_w": 1, "w_kv": 1,
              "w_br_ssd": 0, "w_br_lru": 0, "w_br_mem": 0, "w_out": 0}
SHARD_SHAPE = {"w_in": (1024, 3080), "ssd_conv_w": (4, 768), "ssd_norm_g": (4, 128), "lru_conv_w": (4, 384),
               "w_kv": (1024, 512), "w_br_ssd": (512, 1024), "w_br_lru": (384, 1024), "w_br_mem": (256, 1024),
               "w_out": (256, 1024)}
REPL = ("norm_g", "ssd_conv_b", "ssd_dt_bias", "ssd_a_log", "ssd_d", "lru_conv_b", "lru_w_a", "lru_b_a",
        "lru_w_x", "lru_b_x", "lru_lambda", "mem_norm_g", "final_g")
REPL_SHAPE = {"norm_g": (1, 1024), "ssd_conv_b": (1, 3072), "ssd_dt_bias": (1, 32), "ssd_a_log": (1, 32),
              "ssd_d": (1, 32), "lru_conv_b": (1, 1536), "lru_w_a": (1, 16, 96, 96), "lru_b_a": (1, 16, 96),
              "lru_w_x": (1, 16, 96, 96), "lru_b_x": (1, 16, 96), "lru_lambda": (1, 1536),
              "mem_norm_g": (1, 1024), "final_g": (1024,)}
WEIGHTS = ("norm_g", "w_in", "ssd_conv_w", "ssd_conv_b", "ssd_dt_bias", "ssd_a_log", "ssd_d", "ssd_norm_g",
           "lru_conv_w", "lru_conv_b", "lru_w_a", "lru_b_a", "lru_w_x", "lru_b_x", "lru_lambda", "mem_norm_g",
           "w_kv", "w_br_ssd", "w_br_lru", "w_br_mem", "w_out", "final_g")

PACK_W = 512
SHARD_ROWS = 10016
SMALL_ROWS = 152
SMALL_Q = SMALL_ROWS * PACK_W
GRAD_ROWS = 10176
GRAD_HALF = GRAD_ROWS // 2


def _size(shape):
    n = 1
    for s in shape:
        n *= s
    return n


def _sigmoid(x):
    return 1.0 / (1.0 + jnp.exp(-x))


def _silu(x):
    return x * _sigmoid(x)


def _dsilu(x):
    s = _sigmoid(x)
    return s * (1.0 + x * (1.0 - s))


def _softplus(x):
    return jnp.maximum(x, 0.0) + jnp.log(1.0 + jnp.exp(-jnp.abs(x)))


def _neg_expm1(x):
    series = -x * (1.0 + x * (0.5 + x * (1.0 / 6.0 + x * (1.0 / 24.0 + x * (1.0 / 120.0)))))
    return jnp.where(x > -0.03, series, 1.0 - jnp.exp(x))


def _dot(a, b, precision=None):
    return jnp.dot(a, b, preferred_element_type=F32, precision=precision)


def _dot_nt(a, b):
    return lax.dot_general(a, b, (((1,), (1,)), ((), ())), preferred_element_type=F32)


def _dot_tn(a, b):
    return lax.dot_general(a, b, (((0,), (0,)), ((), ())), preferred_element_type=F32)


def _mx(a):
    return a.astype(_MXU)


def _cparams(sem):
    return pltpu.CompilerParams(dimension_semantics=sem, vmem_limit_bytes=VMEM_LIMIT)


def _tile(n, want, mult=128):
    if n <= want:
        return n
    for t in range(want - want % mult, 0, -mult):
        if n % t == 0:
            return t
    raise ValueError((n, want, mult))


def _mm(a, b, out_dtype, name, tm=1024, tn=1280, tk=1024):
    m, k = a.shape
    k2, n = b.shape
    assert k == k2
    tm, tn, tk = _tile(m, tm), _tile(n, tn), _tile(k, tk)
    nk = k // tk

    def body(a_ref, b_ref, o_ref, acc_ref):
        kk = pl.program_id(2)

        @pl.when(kk == 0)
        def _():
            acc_ref[...] = jnp.zeros_like(acc_ref)

        acc_ref[...] += _dot(a_ref[...], b_ref[...])

        @pl.when(kk == nk - 1)
        def _():
            o_ref[...] = acc_ref[...].astype(o_ref.dtype)

    return pl.pallas_call(
        body, grid=(m // tm, n // tn, nk),
        in_specs=[pl.BlockSpec((tm, tk), lambda i, j, kk: (i, kk)), pl.BlockSpec((tk, tn), lambda i, j, kk: (kk, j))],
        out_specs=pl.BlockSpec((tm, tn), lambda i, j, kk: (i, j)),
        out_shape=jax.ShapeDtypeStruct((m, n), out_dtype),
        scratch_shapes=[pltpu.VMEM((tm, tn), F32)],
        compiler_params=_cparams(("parallel", "parallel", "arbitrary")), name=name)(a, b)


def _norm_fwd(x, g):
    s = x.shape[0]
    ts = _tile(s, 512)

    def body(x_ref, g_ref, h_ref):
        xv = x_ref[...]
        r = lax.rsqrt(jnp.mean(xv * xv, axis=-1, keepdims=True) + EPS)
        h_ref[...] = (xv * r * g_ref[...]).astype(h_ref.dtype)

    return pl.pallas_call(
        body, grid=(s // ts,),
        in_specs=[pl.BlockSpec((ts, D), lambda i: (i, 0)), pl.BlockSpec((1, D), lambda i: (0, 0))],
        out_specs=pl.BlockSpec((ts, D), lambda i: (i, 0)),
        out_shape=jax.ShapeDtypeStruct((s, D), _MXU),
        compiler_params=_cparams(("parallel",)), name="norm_fwd")(x, g)


def _norm_bwd(x, g, dh, dx2):
    s = x.shape[0]
    ts = _tile(s, 512)

    def body(x_ref, g_ref, dh_ref, dx2_ref, gx_ref, dg_ref):
        @pl.when(pl.program_id(0) == 0)
        def _():
            dg_ref[...] = jnp.zeros_like(dg_ref)

        xv = x_ref[...]
        r = lax.rsqrt(jnp.mean(xv * xv, axis=-1, keepdims=True) + EPS)
        xhat = xv * r
        dh_v = dh_ref[...]
        dg_ref[...] += jnp.sum(dh_v * xhat, axis=0, keepdims=True)
        dxh = dh_v * g_ref[...]
        gx_ref[...] = dx2_ref[...] + r * (dxh - xhat * jnp.mean(dxh * xhat, axis=-1, keepdims=True))

    row = pl.BlockSpec((ts, D), lambda i: (i, 0))
    vec = pl.BlockSpec((1, D), lambda i: (0, 0))
    return pl.pallas_call(
        body, grid=(s // ts,), in_specs=[row, vec, row, row], out_specs=[row, vec],
        out_shape=[jax.ShapeDtypeStruct((s, D), F32), jax.ShapeDtypeStruct((1, D), F32)],
        compiler_params=_cparams(("arbitrary",)), name="norm_bwd")(x, g, dh, dx2)


def _conv_taps(ext_ref, w_ref, ts, base):
    acc = None
    for k in range(4):
        t = w_ref[k:k + 1, :] * ext_ref[base + k:base + k + ts, :]
        acc = t if acc is None else acc + t
    return acc


def _conv_fwd(src, blk, w, b, act, name):
    s = src.shape[0]
    off, width = blk
    cb = off // width
    ts = _tile(s, 256)

    def body(x_ref, w_ref, b_ref, o_ref, ext_ref):
        @pl.when(pl.program_id(0) == 0)
        def _():
            ext_ref[0:8, :] = jnp.zeros((8, width), F32)

        ext_ref[8:8 + ts, :] = x_ref[...]
        pre = _conv_taps(ext_ref, w_ref, ts, 5) + b_ref[...]
        o_ref[...] = _silu(pre) if act else pre
        ext_ref[0:8, :] = x_ref[ts - 8:ts, :]

    return pl.pallas_call(
        body, grid=(s // ts,),
        in_specs=[pl.BlockSpec((ts, width), lambda i: (i, cb)), pl.BlockSpec((4, width), lambda i: (0, 0)),
                  pl.BlockSpec((1, width), lambda i: (0, 0))],
        out_specs=pl.BlockSpec((ts, width), lambda i: (i, 0)),
        out_shape=jax.ShapeDtypeStruct((s, width), F32),
        scratch_shapes=[pltpu.VMEM((ts + 8, width), F32)],
        compiler_params=_cparams(("arbitrary",)), name=name)(src, w, b)


def _conv_bwd_w(src, blk, w, b, dout, act, name):
    s = src.shape[0]
    off, width = blk
    cb = off // width
    ts = _tile(s, 256)

    def body(x_ref, w_ref, b_ref, do_ref, *rest):
        if act:
            dpre_ref, dw_ref, db_ref, ext_ref = rest
        else:
            dw_ref, db_ref, ext_ref = rest

        @pl.when(pl.program_id(0) == 0)
        def _():
            ext_ref[0:8, :] = jnp.zeros((8, width), F32)
            dw_ref[...] = jnp.zeros_like(dw_ref)
            db_ref[...] = jnp.zeros_like(db_ref)

        ext_ref[8:8 + ts, :] = x_ref[...]
        if act:
            pre = _conv_taps(ext_ref, w_ref, ts, 5) + b_ref[...]
            dpre = do_ref[...] * _dsilu(pre)
            dpre_ref[...] = dpre
        else:
            dpre = do_ref[...]
        db_ref[...] += jnp.sum(dpre, axis=0, keepdims=True)
        for k in range(4):
            dw_ref[k:k + 1, :] += jnp.sum(dpre * ext_ref[5 + k:5 + k + ts, :], axis=0, keepdims=True)
        ext_ref[0:8, :] = x_ref[ts - 8:ts, :]

    row = pl.BlockSpec((ts, width), lambda i: (i, 0))
    outs = [pl.BlockSpec((4, width), lambda i: (0, 0)), pl.BlockSpec((1, width), lambda i: (0, 0))]
    shapes = [jax.ShapeDtypeStruct((4, width), F32), jax.ShapeDtypeStruct((1, width), F32)]
    if act:
        outs = [row] + outs
        shapes = [jax.ShapeDtypeStruct((s, width), F32)] + shapes
    return pl.pallas_call(
        body, grid=(s // ts,),
        in_specs=[pl.BlockSpec((ts, width), lambda i: (i, cb)), pl.BlockSpec((4, width), lambda i: (0, 0)),
                  pl.BlockSpec((1, width), lambda i: (0, 0)), row],
        out_specs=outs, out_shape=shapes,
        scratch_shapes=[pltpu.VMEM((ts + 8, width), F32)],
        compiler_params=_cparams(("arbitrary",)), name=name)(src, w, b, dout)


def _conv_bwd_x(dpre, w, dproj, blk, name):
    s = dpre.shape[0]
    off, width = blk
    cb = off // width
    ts = _tile(s, 256)
    nt = s // ts

    def body(dp_ref, w_ref, dproj_hbm, o_ref, ext_ref):
        del dproj_hbm

        @pl.when(pl.program_id(0) == 0)
        def _():
            ext_ref[ts:ts + 8, :] = jnp.zeros((8, width), F32)

        ext_ref[0:ts, :] = dp_ref[...]
        acc = None
        for k in range(4):
            t = w_ref[k:k + 1, :] * ext_ref[3 - k:3 - k + ts, :]
            acc = t if acc is None else acc + t
        o_ref[...] = acc.astype(o_ref.dtype)
        ext_ref[ts:ts + 8, :] = dp_ref[0:8, :]

    return pl.pallas_call(
        body, grid=(nt,),
        in_specs=[pl.BlockSpec((ts, width), lambda i: (nt - 1 - i, 0)), pl.BlockSpec((4, width), lambda i: (0, 0)),
                  pl.BlockSpec(memory_space=pl.ANY)],
        out_specs=pl.BlockSpec((ts, width), lambda i: (nt - 1 - i, cb)),
        out_shape=jax.ShapeDtypeStruct(dproj.shape, dproj.dtype),
        scratch_shapes=[pltpu.VMEM((ts + 8, width), F32)],
        input_output_aliases={2: 0},
        compiler_params=_cparams(("arbitrary",)), name=name)(dpre, w, dproj)


def _ssd_decay(a_cs, acst_ref, h, causal, lane_l):
    col = jnp.sum(jnp.where(lane_l == h, a_cs, 0.0), axis=1, keepdims=True)
    row = acst_ref[h:h + 1, :]
    return jnp.where(causal, jnp.exp(jnp.minimum(col - row, 0.0)), 0.0)


def _ssd_common(dt_ref, dtb_ref, alog_ref, e_ref, acst_ref, dtx_ref, acx_ref):
    ll = SSD_L
    dt = _softplus(dt_ref[:, 0:128] + dtb_ref[...])
    a_neg = -jnp.exp(alog_ref[...])
    ri = lax.broadcasted_iota(jnp.int32, (ll, ll), 0)
    ci = lax.broadcasted_iota(jnp.int32, (ll, ll), 1)
    causal = ri >= ci
    a_cs = _dot(causal.astype(F32), dt * a_neg, _HI)
    acst_ref[...] = a_cs.T
    dtx_ref[...] = _dot(dt, e_ref[...], _HI)
    acx_ref[...] = _dot(a_cs, e_ref[...], _HI)
    return dt, a_neg, a_cs, causal, ri


def _ssd_fwd(xbc, proj, dtb, alog, dexp, ng, expand):
    s = xbc.shape[0]
    ll = SSD_L
    nc = s // ll

    def body(xbc_ref, dt_ref, z_ref, dtb_ref, alog_ref, dexp_ref, ng_ref, e_ref,
             yssd_ref, yraw_ref, hprev_ref, ht_ref, acst_ref, dtx_ref, acx_ref):
        @pl.when(pl.program_id(0) == 0)
        def _():
            ht_ref[...] = jnp.zeros_like(ht_ref)

        hprev_ref[0] = ht_ref[...]
        _, _, a_cs, causal, _ = _ssd_common(dt_ref, dtb_ref, alog_ref, e_ref, acst_ref, dtx_ref, acx_ref)
        lane_l = lax.broadcasted_iota(jnp.int32, (ll, 128), 1)
        lo = lane_l < 64
        for g in range(4):
            bg = _mx(xbc_ref[:, 2048 + 128 * g:2176 + 128 * g])
            cg = _mx(xbc_ref[:, 2560 + 128 * g:2688 + 128 * g])
            cbm = _dot_nt(cg, bg)
            for jj in range(4):
                j = 4 * g + jj
                sl = slice(128 * j, 128 * j + 128)
                xp = xbc_ref[:, sl]
                acx = acx_ref[:, sl]
                a_last = acx_ref[ll - 1:ll, sl]
                xdt = xp * dtx_ref[:, sl]
                acc = None
                for hh in range(2):
                    dec = _ssd_decay(a_cs, acst_ref, 2 * j + hh, causal, lane_l)
                    xm = jnp.where(lo if hh == 0 else jnp.logical_not(lo), xdt, 0.0)
                    t = _dot(_mx(dec * cbm), _mx(xm))
                    acc = t if acc is None else acc + t
                ht = ht_ref[j]
                y = acc + _dot(cg, _mx(ht)) * jnp.exp(acx) + xp * dexp_ref[:, sl]
                yraw_ref[:, sl] = y
                st = _dot_tn(bg, _mx(xdt * jnp.exp(a_last - acx)))
                ht_ref[j] = ht * jnp.exp(a_last) + st
        for g in range(4):
            sl = slice(512 * g, 512 * g + 512)
            yg = yraw_ref[:, sl] * _silu(z_ref[:, sl])
            r = lax.rsqrt(jnp.mean(yg * yg, axis=-1, keepdims=True) + EPS)
            yssd_ref[:, sl] = (yg * r * ng_ref[:, sl]).astype(yssd_ref.dtype)

    vec = lambda w: pl.BlockSpec((1, w), lambda c: (0, 0))
    return pl.pallas_call(
        body, grid=(nc,),
        in_specs=[pl.BlockSpec((ll, 3072), lambda c: (c, 0)),
                  pl.BlockSpec((ll, DT[1]), lambda c: (c, DT[0] // DT[1])),
                  pl.BlockSpec((ll, Z[1]), lambda c: (c, Z[0] // Z[1])),
                  vec(128), vec(128), vec(2048), vec(2048), pl.BlockSpec((128, 2048), lambda c: (0, 0))],
        out_specs=[pl.BlockSpec((ll, 2048), lambda c: (c, 0)), pl.BlockSpec((ll, 2048), lambda c: (c, 0)),
                   pl.BlockSpec((1, 16, 128, 128), lambda c: (c, 0, 0, 0))],
        out_shape=[jax.ShapeDtypeStruct((s, 2048), _MXU), jax.ShapeDtypeStruct((s, 2048), F32),
                   jax.ShapeDtypeStruct((nc, 16, 128, 128), F32)],
        scratch_shapes=[pltpu.VMEM((16, 128, 128), F32), pltpu.VMEM((128, ll), F32),
                        pltpu.VMEM((ll, 2048), F32), pltpu.VMEM((ll, 2048), F32)],
        compiler_params=_cparams(("arbitrary",)), name="ssd_fwd")(xbc, proj, proj, dtb, alog, dexp, ng, expand)


def _ssd_bwd(xbc, proj, yraw, hprev, dyssd, dproj, dtb, alog, dexp, ng, expand, reduce_):
    s = xbc.shape[0]
    ll = SSD_L
    nc = s // ll

    def body(xbc_ref, dt_ref, z_ref, yraw_ref, hprev_ref, dy_ref, dproj_hbm, dtb_ref, alog_ref, dexp_ref, ng_ref,
             e_ref, r_ref,
             dz_ref, ddt_ref, dxbc_ref, dng_ref, dda_ref, ddd_ref, ddtb_ref,
             dht_ref, acst_ref, dtx_ref, acx_ref, dyr_ref, rowt_ref, lane_a_ref, lane_b_ref, dcd_ref):
        del dproj_hbm

        @pl.when(pl.program_id(0) == 0)
        def _():
            dht_ref[...] = jnp.zeros_like(dht_ref)
            dng_ref[...] = jnp.zeros_like(dng_ref)
            dda_ref[...] = jnp.zeros_like(dda_ref)
            ddd_ref[...] = jnp.zeros_like(ddd_ref)
            ddtb_ref[...] = jnp.zeros_like(ddtb_ref)
            rowt_ref[...] = jnp.zeros_like(rowt_ref)

        for g in range(4):
            sl = slice(512 * g, 512 * g + 512)
            zz = z_ref[:, sl]
            yr = yraw_ref[:, sl]
            sz = _silu(zz)
            yg = yr * sz
            r = lax.rsqrt(jnp.mean(yg * yg, axis=-1, keepdims=True) + EPS)
            yhat = yg * r
            dyv = dy_ref[:, sl]
            dng_ref[:, sl] += jnp.sum(dyv * yhat, axis=0, keepdims=True)
            dyh = dyv * ng_ref[:, sl]
            dyg = r * (dyh - yhat * jnp.mean(dyh * yhat, axis=-1, keepdims=True))
            dz_ref[:, sl] = (dyg * yr * _dsilu(zz)).astype(dz_ref.dtype)
            dyr_ref[:, sl] = dyg * sz

        dt, a_neg, a_cs, causal, ri = _ssd_common(dt_ref, dtb_ref, alog_ref, e_ref, acst_ref, dtx_ref, acx_ref)
        lane_l = lax.broadcasted_iota(jnp.int32, (ll, 128), 1)
        lo = lane_l < 64
        da_col = jnp.zeros((ll, 128), F32)
        for g in range(4):
            bg = _mx(xbc_ref[:, 2048 + 128 * g:2176 + 128 * g])
            cg = _mx(xbc_ref[:, 2560 + 128 * g:2688 + 128 * g])
            cbm = _dot_nt(cg, bg)
            dcb = jnp.zeros((ll, ll), F32)
            db_g = jnp.zeros((ll, 128), F32)
            dc_g = jnp.zeros((ll, 128), F32)
            for jj in range(4):
                j = 4 * g + jj
                sl = slice(128 * j, 128 * j + 128)
                xp = xbc_ref[:, sl]
                dtx = dtx_ref[:, sl]
                acx = acx_ref[:, sl]
                a_last = acx_ref[ll - 1:ll, sl]
                ea = jnp.exp(acx)
                dte = jnp.exp(a_last - acx)
                cd = jnp.exp(a_last)
                xdt = xp * dtx
                xdt_m = _mx(xdt)
                dy = dyr_ref[:, sl]
                ht = hprev_ref[0, j]
                dhn = dht_ref[j]
                dhn_m = _mx(dhn)
                gmat = _dot(bg, dhn_m)
                dxdt = gmat * dte
                for hh in range(2):
                    h = 2 * j + hh
                    dec = _ssd_decay(a_cs, acst_ref, h, causal, lane_l)
                    mm = dec * cbm
                    dym = _mx(jnp.where(lo if hh == 0 else jnp.logical_not(lo), dy, 0.0))
                    dxdt = dxdt + _dot_tn(_mx(mm), dym)
                    dm = _dot_nt(dym, xdt_m)
                    dcb = dcb + dm * dec
                    qq = dm * mm
                    da_col = da_col + jnp.where(lane_l == h, jnp.sum(qq, axis=1, keepdims=True), 0.0)
                    rowt_ref[h:h + 1, :] = jnp.sum(qq, axis=0, keepdims=True)
                ch = _dot(cg, _mx(ht))
                dyea = dy * ea
                dyea_m = _mx(dyea)
                xw_m = _mx(xdt * dte)
                dc_g = dc_g + _dot_nt(dyea_m, _mx(ht))
                db_g = db_g + _dot_nt(xw_m, dhn_m)
                wl = xdt * gmat * dte
                dcd_ref[:, sl] = (jnp.sum(dhn * ht, axis=0, keepdims=True) * cd
                                  + jnp.sum(wl, axis=0, keepdims=True))
                dht_ref[j] = dhn * cd + _dot_tn(cg, dyea_m)
                lane_a_ref[:, sl] = dyea * ch - wl
                lane_b_ref[:, sl] = dxdt * xp
                dxbc_ref[:, sl] = dxdt * dtx + dy * dexp_ref[:, sl]
                ddd_ref[:, sl] += jnp.sum(dy * xp, axis=0, keepdims=True)
            dcb_m = _mx(dcb)
            dxbc_ref[:, 2048 + 128 * g:2176 + 128 * g] = db_g + _dot_tn(dcb_m, cg)
            dxbc_ref[:, 2560 + 128 * g:2688 + 128 * g] = dc_g + _dot(dcb_m, bg)

        red = r_ref[...]
        col_terms = _dot(lane_a_ref[...], red, _HI)
        ddt_x = _dot(lane_b_ref[...], red, _HI)
        last = jnp.sum(_dot(jnp.broadcast_to(dcd_ref[...], (8, 2048)), red, _HI), axis=0, keepdims=True) * 0.125
        da_cs = da_col + col_terms - rowt_ref[...].T
        da_cs = da_cs + jnp.where(lax.broadcasted_iota(jnp.int32, (ll, 128), 0) == ll - 1, last, 0.0)
        d_dta = _dot((ri <= lax.broadcasted_iota(jnp.int32, (ll, ll), 1)).astype(F32), da_cs, _HI)
        ddt = d_dta * a_neg + ddt_x
        dda_ref[...] += jnp.sum(d_dta * dt, axis=0, keepdims=True)
        ddt_raw = ddt * _sigmoid(dt_ref[:, 0:128] + dtb_ref[...])
        ddtb_ref[...] += jnp.sum(ddt_raw, axis=0, keepdims=True)
        ddt_ref[:, 0:128] = ddt_raw.astype(ddt_ref.dtype)
        ddt_ref[:, 128:512] = jnp.zeros((ll, 384), ddt_ref.dtype)

    rev = lambda c: nc - 1 - c
    vec = lambda w: pl.BlockSpec((1, w), lambda c: (0, 0))
    row = lambda w: pl.BlockSpec((ll, w), lambda c: (rev(c), 0))
    outs = pl.pallas_call(
        body, grid=(nc,),
        in_specs=[row(3072),
                  pl.BlockSpec((ll, DT[1]), lambda c: (rev(c), DT[0] // DT[1])),
                  pl.BlockSpec((ll, Z[1]), lambda c: (rev(c), Z[0] // Z[1])),
                  row(2048),
                  pl.BlockSpec((1, 16, 128, 128), lambda c: (rev(c), 0, 0, 0)),
                  row(2048),
                  pl.BlockSpec(memory_space=pl.ANY),
                  vec(128), vec(128), vec(2048), vec(2048),
                  pl.BlockSpec((128, 2048), lambda c: (0, 0)), pl.BlockSpec((2048, 128), lambda c: (0, 0))],
        out_specs=[pl.BlockSpec((ll, Z[1]), lambda c: (rev(c), Z[0] // Z[1])),
                   row(DT[1]),
                   row(3072), vec(2048), vec(128), vec(2048), vec(128)],
        out_shape=[jax.ShapeDtypeStruct(dproj.shape, dproj.dtype), jax.ShapeDtypeStruct((s, DT[1]), dproj.dtype),
                   jax.ShapeDtypeStruct((s, 3072), F32), jax.ShapeDtypeStruct((1, 2048), F32),
                   jax.ShapeDtypeStruct((1, 128), F32), jax.ShapeDtypeStruct((1, 2048), F32),
                   jax.ShapeDtypeStruct((1, 128), F32)],
        scratch_shapes=[pltpu.VMEM((16, 128, 128), F32), pltpu.VMEM((128, ll), F32),
                        pltpu.VMEM((ll, 2048), F32), pltpu.VMEM((ll, 2048), F32), pltpu.VMEM((ll, 2048), F32),
                        pltpu.VMEM((128, ll), F32), pltpu.VMEM((ll, 2048), F32), pltpu.VMEM((ll, 2048), F32),
                        pltpu.VMEM((1, 2048), F32)],
        input_output_aliases={6: 0},
        compiler_params=_cparams(("arbitrary",)), name="ssd_bwd")(
            xbc, proj, proj, yraw, hprev, dyssd, dproj, dtb, alog, dexp, ng, expand, reduce_)
    return outs


def _put_block(src, dproj, blk, name):
    s = src.shape[0]
    off, width = blk
    cb = off // width
    ts = _tile(s, 1024)

    def body(s_ref, dproj_hbm, o_ref):
        del dproj_hbm
        o_ref[...] = s_ref[...]

    return pl.pallas_call(
        body, grid=(s // ts,),
        in_specs=[pl.BlockSpec((ts, width), lambda i: (i, 0)), pl.BlockSpec(memory_space=pl.ANY)],
        out_specs=pl.BlockSpec((ts, width), lambda i: (i, cb)),
        out_shape=jax.ShapeDtypeStruct(dproj.shape, dproj.dtype),
        input_output_aliases={1: 0},
        compiler_params=_cparams(("parallel",)), name=name)(src, dproj)


LRU_G = 384


def _lru_gates(xl_ref, wa_ref, wx_ref, ba_ref, bx_ref, lam_ref, g):
    sl = slice(LRU_G * g, LRU_G * g + LRU_G)
    xg = xl_ref[:, sl]
    xm = _mx(xg)
    r = _sigmoid(_dot(xm, wa_ref[g]) + ba_ref[:, sl])
    ig = _sigmoid(_dot(xm, wx_ref[g]) + bx_ref[:, sl])
    sp = _softplus(-lam_ref[:, sl])
    log_a = (-LRU_C * r) * sp
    a = jnp.exp(log_a)
    mult = jnp.sqrt(_neg_expm1(2.0 * log_a))
    return sl, xg, r, ig, sp, a, mult


def _lru_fwd(xl, proj, wa, wx, ba, bx, lam):
    s = xl.shape[0]
    ts = _tile(s, 256)
    w = LRU_W

    def body(xl_ref, lg_ref, wa_ref, wx_ref, ba_ref, bx_ref, lam_ref, y_ref, hs_ref, a_ref, u_ref, carry_ref):
        @pl.when(pl.program_id(0) == 0)
        def _():
            carry_ref[...] = jnp.zeros_like(carry_ref)

        for g in range(4):
            sl, xg, _, ig, _, a, mult = _lru_gates(xl_ref, wa_ref, wx_ref, ba_ref, bx_ref, lam_ref, g)
            a_ref[:, sl] = a
            u_ref[:, sl] = mult * (ig * xg)

        def step(t, h):
            h = a_ref[pl.ds(t, 1), :] * h + u_ref[pl.ds(t, 1), :]
            hs_ref[pl.ds(t, 1), :] = h
            return h

        carry_ref[0:1, :] = lax.fori_loop(0, ts, step, carry_ref[0:1, :], unroll=8)
        y_ref[...] = (hs_ref[...] * _silu(lg_ref[...])).astype(y_ref.dtype)

    row = pl.BlockSpec((ts, w), lambda i: (i, 0))
    vec = pl.BlockSpec((1, w), lambda i: (0, 0))
    wsp = pl.BlockSpec((4, LRU_G, LRU_G), lambda i: (0, 0, 0))
    return pl.pallas_call(
        body, grid=(s // ts,),
        in_specs=[row, pl.BlockSpec((ts, w), lambda i: (i, LG[0] // w)), wsp, wsp, vec, vec, vec],
        out_specs=[row, row],
        out_shape=[jax.ShapeDtypeStruct((s, w), _MXU), jax.ShapeDtypeStruct((s, w), F32)],
        scratch_shapes=[pltpu.VMEM((ts, w), F32), pltpu.VMEM((ts, w), F32), pltpu.VMEM((8, w), F32)],
        compiler_params=_cparams(("arbitrary",)), name="lru_fwd")(xl, proj, wa, wx, ba, bx, lam)


def _lru_bwd(xl, proj, hs, dy, dproj, wa, wx, ba, bx, lam):
    s = xl.shape[0]
    ts = _tile(s, 256)
    nt = s // ts
    w = LRU_W
    hb = ts // 8

    def body(xl_ref, lg_ref, hs_ref, hprev_ref, dy_ref, dproj_hbm, wa_ref, wx_ref, ba_ref, bx_ref, lam_ref,
             dlg_ref, dxl_ref, dwa_ref, dwx_ref, dba_ref, dbx_ref, dlam_ref,
             a_ref, dh_ref, ext_ref, carry_ref):
        del dproj_hbm
        i = pl.program_id(0)

        @pl.when(i == 0)
        def _():
            carry_ref[...] = jnp.zeros_like(carry_ref)
            for ref in (dwa_ref, dwx_ref, dba_ref, dbx_ref, dlam_ref):
                ref[...] = jnp.zeros_like(ref)

        lg = lg_ref[...]
        dyv = dy_ref[...]
        dh_ref[...] = dyv * _silu(lg)
        dlg_ref[...] = (dyv * hs_ref[...] * _dsilu(lg)).astype(dlg_ref.dtype)
        for g in range(4):
            sl, _, _, _, _, a, _ = _lru_gates(xl_ref, wa_ref, wx_ref, ba_ref, bx_ref, lam_ref, g)
            a_ref[:, sl] = a

        def step(k, carry):
            t = ts - 1 - k
            dh = dh_ref[pl.ds(t, 1), :] + carry
            dh_ref[pl.ds(t, 1), :] = dh
            return a_ref[pl.ds(t, 1), :] * dh

        carry_ref[0:1, :] = lax.fori_loop(0, ts, step, carry_ref[0:1, :], unroll=8)

        ext_ref[0:8, :] = jnp.where(i == nt - 1, 0.0, 1.0) * hprev_ref[...]
        ext_ref[8:8 + ts, :] = hs_ref[...]
        for g in range(4):
            sl, xg, r, ig, sp, a, mult = _lru_gates(xl_ref, wa_ref, wx_ref, ba_ref, bx_ref, lam_ref, g)
            dh = dh_ref[:, sl]
            da = dh * ext_ref[7:7 + ts, sl]
            dmult = dh * ig * xg
            di = dh * mult * xg
            dxl = dh * mult * ig
            dlog_a = da * a - dmult * (a * a) / mult
            dlam_ref[:, sl] += jnp.sum(dlog_a * r, axis=0, keepdims=True) * (LRU_C * _sigmoid(-lam_ref[:, sl]))
            dpa = dlog_a * (-LRU_C * sp) * r * (1.0 - r)
            dpx = di * ig * (1.0 - ig)
            dba_ref[:, sl] += jnp.sum(dpa, axis=0, keepdims=True)
            dbx_ref[:, sl] += jnp.sum(dpx, axis=0, keepdims=True)
            dpa_m, dpx_m, xm = _mx(dpa), _mx(dpx), _mx(xg)
            dxl_ref[:, sl] = dxl + _dot_nt(dpa_m, wa_ref[g]) + _dot_nt(dpx_m, wx_ref[g])
            dwa_ref[g] += _dot_tn(xm, dpa_m)
            dwx_ref[g] += _dot_tn(xm, dpx_m)

    rev = lambda i: nt - 1 - i
    row = pl.BlockSpec((ts, w), lambda i: (rev(i), 0))
    vec = pl.BlockSpec((1, w), lambda i: (0, 0))
    wsp = pl.BlockSpec((4, LRU_G, LRU_G), lambda i: (0, 0, 0))
    lgs = pl.BlockSpec((ts, w), lambda i: (rev(i), LG[0] // w))
    return pl.pallas_call(
        body, grid=(nt,),
        in_specs=[row, lgs, row, pl.BlockSpec((8, w), lambda i: (jnp.maximum(rev(i) * hb - 1, 0), 0)), row,
                  pl.BlockSpec(memory_space=pl.ANY), wsp, wsp, vec, vec, vec],
        out_specs=[lgs, row, wsp, wsp, vec, vec, vec],
        out_shape=[jax.ShapeDtypeStruct(dproj.shape, dproj.dtype), jax.ShapeDtypeStruct((s, w), F32),
                   jax.ShapeDtypeStruct((4, LRU_G, LRU_G), F32), jax.ShapeDtypeStruct((4, LRU_G, LRU_G), F32),
                   jax.ShapeDtypeStruct((1, w), F32), jax.ShapeDtypeStruct((1, w), F32),
                   jax.ShapeDtypeStruct((1, w), F32)],
        scratch_shapes=[pltpu.VMEM((ts, w), F32), pltpu.VMEM((ts, w), F32), pltpu.VMEM((ts + 8, w), F32),
                        pltpu.VMEM((8, w), F32)],
        input_output_aliases={5: 0},
        compiler_params=_cparams(("arbitrary",)), name="lru_bwd")(xl, proj, hs, hs, dy, dproj, wa, wx, ba, bx, lam)


def _mem_kv_fwd(mem, g, wkv):
    m = mem.shape[0]

    def body(mem_ref, g_ref, w_ref, k_ref, v_ref, mn_ref):
        mv = mem_ref[...]
        r = lax.rsqrt(jnp.mean(mv * mv, axis=-1, keepdims=True) + EPS)
        mn = _mx(mv * r * g_ref[...])
        mn_ref[...] = mn
        kv = _dot(mn, w_ref[...])
        k_ref[...] = kv[:, 0:D].astype(k_ref.dtype)
        v_ref[...] = kv[:, D:2 * D].astype(v_ref.dtype)

    sh = jax.ShapeDtypeStruct((m, D), _MXU)
    return pl.pallas_call(body, out_shape=[sh, sh, sh], compiler_params=_cparams(None), name="mem_kv_fwd")(mem, g, wkv)


def _mem_kv_bwd(mem, g, mn, wkv, dk, dv):
    m = mem.shape[0]

    def body(mem_ref, g_ref, mn_ref, w_ref, dk_ref, dv_ref, dw_ref, dg_ref):
        dkv = _mx(jnp.concatenate([dk_ref[...], dv_ref[...]], axis=1))
        dw_ref[...] = _dot_tn(mn_ref[...], dkv)
        dmn = _dot_nt(dkv, w_ref[...])
        mv = mem_ref[...]
        r = lax.rsqrt(jnp.mean(mv * mv, axis=-1, keepdims=True) + EPS)
        dg_ref[...] = jnp.sum(dmn * mv * r, axis=0, keepdims=True)

    del m
    return pl.pallas_call(
        body, out_shape=[jax.ShapeDtypeStruct((D, 2 * D), F32), jax.ShapeDtypeStruct((1, D), F32)],
        compiler_params=_cparams(None), name="mem_kv_bwd")(mem, g, mn, wkv, dk, dv)


def _attn_probs(q_ref, k_ref, hd):
    sl = slice(MEM_HD * hd, MEM_HD * hd + MEM_HD)
    qh = _mx(q_ref[:, sl])
    sc = _dot_nt(qh, k_ref[:, sl]) * (MEM_HD ** -0.5)
    e = jnp.exp(sc - jnp.max(sc, axis=-1, keepdims=True))
    return sl, qh, e / jnp.sum(e, axis=-1, keepdims=True)


def _attn_fwd(proj, k, v):
    s = proj.shape[0]
    m = k.shape[0]
    ts = _tile(s, 512)

    def body(q_ref, k_ref, v_ref, y_ref):
        for hd in range(MEM_HEADS):
            sl, _, p = _attn_probs(q_ref, k_ref, hd)
            y_ref[:, sl] = _dot(_mx(p), v_ref[:, sl]).astype(y_ref.dtype)

    kvs = pl.BlockSpec((m, D), lambda i: (0, 0))
    return pl.pallas_call(
        body, grid=(s // ts,),
        in_specs=[pl.BlockSpec((ts, D), lambda i: (i, Q[0] // D)), kvs, kvs],
        out_specs=pl.BlockSpec((ts, D), lambda i: (i, 0)),
        out_shape=jax.ShapeDtypeStruct((s, D), _MXU),
        compiler_params=_cparams(("parallel",)), name="attn_fwd")(proj, k, v)


def _attn_bwd(proj, k, v, dy, dproj):
    s = proj.shape[0]
    m = k.shape[0]
    ts = _tile(s, 512)

    def body(q_ref, k_ref, v_ref, dy_ref, dproj_hbm, dq_ref, dk_ref, dv_ref):
        del dproj_hbm

        @pl.when(pl.program_id(0) == 0)
        def _():
            dk_ref[...] = jnp.zeros_like(dk_ref)
            dv_ref[...] = jnp.zeros_like(dv_ref)

        for hd in range(MEM_HEADS):
            sl, qh, p = _attn_probs(q_ref, k_ref, hd)
            dyh = _mx(dy_ref[:, sl])
            dp = _dot_nt(dyh, v_ref[:, sl])
            ds = _mx(p * (dp - jnp.sum(dp * p, axis=-1, keepdims=True)) * (MEM_HD ** -0.5))
            dq_ref[:, sl] = _dot(ds, k_ref[:, sl]).astype(dq_ref.dtype)
            dk_ref[:, sl] += _dot_tn(ds, qh)
            dv_ref[:, sl] += _dot_tn(_mx(p), dyh)

    kvs = pl.BlockSpec((m, D), lambda i: (0, 0))
    qs = pl.BlockSpec((ts, D), lambda i: (i, Q[0] // D))
    return pl.pallas_call(
        body, grid=(s // ts,),
        in_specs=[qs, kvs, kvs, pl.BlockSpec((ts, D), lambda i: (i, 0)), pl.BlockSpec(memory_space=pl.ANY)],
        out_specs=[qs, kvs, kvs],
        out_shape=[jax.ShapeDtypeStruct(dproj.shape, dproj.dtype), jax.ShapeDtypeStruct((m, D), F32),
                   jax.ShapeDtypeStruct((m, D), F32)],
        input_output_aliases={4: 0},
        compiler_params=_cparams(("arbitrary",)), name="attn_bwd")(proj, k, v, dy, dproj)


def _merge_fb(x, target, yssd, ylru, ymem, proj, wbs, wbl, wbm, wo, fg):
    s = x.shape[0]
    ts = _tile(s, 256)

    def body(x_ref, t_ref, ys_ref, yl_ref, ym_ref, gl_ref, wbs_ref, wbl_ref, wbm_ref, wo_ref, fg_ref,
             dgl_ref, dx2_ref, dx2m_ref, mg_ref, db0_ref, db1_ref, db2_ref, loss_ref, dfg_ref):
        @pl.when(pl.program_id(0) == 0)
        def _():
            loss_ref[...] = jnp.zeros_like(loss_ref)
            dfg_ref[...] = jnp.zeros_like(dfg_ref)

        bs = (_dot(ys_ref[...], wbs_ref[...]), _dot(yl_ref[...], wbl_ref[...]), _dot(ym_ref[...], wbm_ref[...]))
        gates = [_sigmoid(gl_ref[:, D * n:D * n + D]) for n in range(3)]
        merged = gates[0] * bs[0] + gates[1] * bs[1] + gates[2] * bs[2]
        mg = _mx(merged)
        mg_ref[...] = mg
        x2 = x_ref[...] + _dot(mg, wo_ref[...])
        r = lax.rsqrt(jnp.mean(x2 * x2, axis=-1, keepdims=True) + EPS)
        xhat = x2 * r
        err = xhat * fg_ref[...] - t_ref[...]
        loss_ref[...] += jnp.sum(err * err, axis=0, keepdims=True) * (0.5 / D)
        dy = err * (1.0 / D)
        dfg_ref[...] += jnp.sum(dy * xhat, axis=0, keepdims=True)
        dxh = dy * fg_ref[...]
        dx2 = r * (dxh - xhat * jnp.mean(dxh * xhat, axis=-1, keepdims=True))
        dx2_ref[...] = dx2
        dx2m = _mx(dx2)
        dx2m_ref[...] = dx2m
        dmg = _dot_nt(dx2m, wo_ref[...])
        for n, db_ref in enumerate((db0_ref, db1_ref, db2_ref)):
            gt = gates[n]
            dgl_ref[:, D * n:D * n + D] = (dmg * bs[n] * gt * (1.0 - gt)).astype(dgl_ref.dtype)
            db_ref[...] = (dmg * gt).astype(db_ref.dtype)

    row = lambda w: pl.BlockSpec((ts, w), lambda i: (i, 0))
    full = lambda a: pl.BlockSpec(a.shape, lambda i: (0, 0))
    vec = pl.BlockSpec((1, D), lambda i: (0, 0))
    gls = pl.BlockSpec((ts, GL[1]), lambda i: (i, GL[0] // GL[1]))
    act = jax.ShapeDtypeStruct((s, D), _MXU)
    return pl.pallas_call(
        body, grid=(s // ts,),
        in_specs=[row(D), row(D), row(SSD_W), row(LRU_W), row(D), gls, full(wbs), full(wbl), full(wbm), full(wo), vec],
        out_specs=[gls, row(D), row(D), row(D), row(D), row(D), row(D), vec, vec],
        out_shape=[jax.ShapeDtypeStruct((s, NP), _MXU), jax.ShapeDtypeStruct((s, D), F32), act, act, act, act, act,
                   jax.ShapeDtypeStruct((1, D), F32), jax.ShapeDtypeStruct((1, D), F32)],
        compiler_params=_cparams(("arbitrary",)), name="merge_fwd_bwd")(
            x, target, yssd, ylru, ymem, proj, wbs, wbl, wbm, wo, fg)


def _adamw(w, g, m, v, name):
    rows, cols = w.shape
    tr = rows if rows <= 512 else 256
    assert rows % tr == 0

    def body(w_ref, g_ref, m_ref, v_ref, d_ref, mo_ref, vo_ref):
        gv = g_ref[...]
        mn = ADAM_B1 * m_ref[...] + (1.0 - ADAM_B1) * gv
        vn = ADAM_B2 * v_ref[...] + (1.0 - ADAM_B2) * (gv * gv)
        m_hat = mn / (1.0 - ADAM_B1 ** ADAM_STEP)
        v_hat = vn / (1.0 - ADAM_B2 ** ADAM_STEP)
        d_ref[...] = -ADAM_LR * (m_hat / (jnp.sqrt(v_hat) + ADAM_EPS) + ADAM_WD * w_ref[...])
        mo_ref[...] = mn
        vo_ref[...] = vn

    blk = pl.BlockSpec((tr, cols), lambda i: (i, 0))
    sh = jax.ShapeDtypeStruct((rows, cols), F32)
    return pl.pallas_call(
        body, grid=(rows // tr,), in_specs=[blk] * 4, out_specs=[blk] * 3, out_shape=[sh] * 3,
        compiler_params=_cparams(("parallel",)), name=name)(w, g, m, v)


def _mesh_pos():
    x, y, c = lax.axis_index("x"), lax.axis_index("y"), lax.axis_index("c")
    chips = [(1 - x, y), (x, 1 - y), (1 - x, 1 - y)]
    return x, y, c, 2 * x + y, chips


def _hbm():
    return pl.BlockSpec(memory_space=pl.ANY)


def _remote(src, dst, send_sem, recv_sem, dev):
    return pltpu.make_async_remote_copy(src_ref=src, dst_ref=dst, send_sem=send_sem, recv_sem=recv_sem,
                                        device_id=dev, device_id_type=MESH)


def _gather_shards(packed):
    rows, width = packed.shape
    half = rows // 2

    def body(src, out, send_sems, recv_sems, local_sem):
        x, y, c, me, chips = _mesh_pos()
        sib = (x, y, 1 - c)
        mine_rows = pl.ds(c * half, half)
        other_rows = pl.ds((1 - c) * half, half)
        own = pltpu.make_async_copy(src, out.at[me], local_sem)
        own.start()
        sends = [_remote(src.at[mine_rows], out.at[me, mine_rows], send_sems.at[j], recv_sems.at[j], (cx, cy, c))
                 for j, (cx, cy) in enumerate(chips)]
        for cp in sends:
            cp.start()
        passed = []
        for j, (cx, cy) in enumerate(chips):
            slot = out.at[2 * cx + cy, mine_rows]
            _remote(slot, slot, send_sems.at[j], recv_sems.at[j], (cx, cy, c)).wait_recv()
            fwd = _remote(slot, slot, send_sems.at[3 + j], recv_sems.at[3 + j], sib)
            fwd.start()
            passed.append(fwd)
        for j, (cx, cy) in enumerate(chips):
            slot = out.at[2 * cx + cy, other_rows]
            _remote(slot, slot, send_sems.at[3 + j], recv_sems.at[3 + j], sib).wait_recv()
        for cp in sends + passed:
            cp.wait_send()
        own.wait()

    return pl.pallas_call(
        body, in_specs=[_hbm()], out_specs=_hbm(),
        out_shape=jax.ShapeDtypeStruct((NSHARD, rows, width), packed.dtype),
        scratch_shapes=[pltpu.SemaphoreType.DMA((6,)), pltpu.SemaphoreType.DMA((6,)), pltpu.SemaphoreType.DMA],
        name="gather_shards")(packed)


def _swap_halves(g4):
    _, _, half, width = g4.shape

    def body(src, out, send_sems, recv_sems):
        x, y, c, _, _ = _mesh_pos()
        sib = (x, y, 1 - c)
        cps = [_remote(src.at[j, 1 - c], out.at[j], send_sems.at[j], recv_sems.at[j], sib) for j in range(NSHARD)]
        for cp in cps:
            cp.start()
        for cp in cps:
            cp.wait()

    return pl.pallas_call(
        body, in_specs=[_hbm()], out_specs=_hbm(),
        out_shape=jax.ShapeDtypeStruct((NSHARD, half, width), g4.dtype),
        scratch_shapes=[pltpu.SemaphoreType.DMA((NSHARD,)), pltpu.SemaphoreType.DMA((NSHARD,))],
        name="swap_halves")(g4)


def _scatter_chips(part):
    def body(src, out, send_sems, recv_sems, local_sem):
        _, _, c, me, chips = _mesh_pos()
        own = pltpu.make_async_copy(src.at[me], out.at[me], local_sem)
        own.start()
        cps = [_remote(src.at[2 * cx + cy], out.at[me], send_sems.at[j], recv_sems.at[j], (cx, cy, c))
               for j, (cx, cy) in enumerate(chips)]
        for cp in cps:
            cp.start()
        for j, (cx, cy) in enumerate(chips):
            slot = out.at[2 * cx + cy]
            _remote(slot, slot, send_sems.at[j], recv_sems.at[j], (cx, cy, c)).wait_recv()
        for cp in cps:
            cp.wait_send()
        own.wait()

    return pl.pallas_call(
        body, in_specs=[_hbm()], out_specs=_hbm(), out_shape=jax.ShapeDtypeStruct(part.shape, part.dtype),
        scratch_shapes=[pltpu.SemaphoreType.DMA((3,)), pltpu.SemaphoreType.DMA((3,)), pltpu.SemaphoreType.DMA],
        name="scatter_chips")(part)


def _share_half(red):
    half, width = red.shape

    def body(src, out, send_sem, recv_sem, local_sem):
        x, y, c, _, _ = _mesh_pos()
        own = pltpu.make_async_copy(src, out.at[c], local_sem)
        own.start()
        cp = _remote(src, out.at[c], send_sem, recv_sem, (x, y, 1 - c))
        cp.start()
        _remote(src, out.at[1 - c], send_sem, recv_sem, (x, y, 1 - c)).wait_recv()
        cp.wait_send()
        own.wait()

    return pl.pallas_call(
        body, in_specs=[_hbm()], out_specs=_hbm(), out_shape=jax.ShapeDtypeStruct((2, half, width), red.dtype),
        scratch_shapes=[pltpu.SemaphoreType.DMA, pltpu.SemaphoreType.DMA, pltpu.SemaphoreType.DMA],
        name="share_half")(red)


def _gather_small(full):
    _, width = full.shape

    def body(src, out, send_sems, recv_sems, local_sem):
        _, _, c, me, chips = _mesh_pos()
        mine = src.at[pl.ds(SHARD_ROWS, SMALL_ROWS)]
        own = pltpu.make_async_copy(mine, out.at[me], local_sem)
        own.start()
        cps = [_remote(mine, out.at[me], send_sems.at[j], recv_sems.at[j], (cx, cy, c))
               for j, (cx, cy) in enumerate(chips)]
        for cp in cps:
            cp.start()
        for j, (cx, cy) in enumerate(chips):
            slot = out.at[2 * cx + cy]
            _remote(slot, slot, send_sems.at[j], recv_sems.at[j], (cx, cy, c)).wait_recv()
        for cp in cps:
            cp.wait_send()
        own.wait()

    return pl.pallas_call(
        body, in_specs=[_hbm()], out_specs=_hbm(),
        out_shape=jax.ShapeDtypeStruct((NSHARD, SMALL_ROWS, width), full.dtype),
        scratch_shapes=[pltpu.SemaphoreType.DMA((3,)), pltpu.SemaphoreType.DMA((3,)), pltpu.SemaphoreType.DMA],
        name="gather_small")(full)


RED_TR = 848


def _add_sibling(g4, recv, c):
    _, _, half, width = g4.shape
    tr = _tile(half, RED_TR, 8)

    def body(c_ref, a_ref, b_ref, o_ref):
        del c_ref
        o_ref[...] = a_ref[0] + b_ref[...]

    grid_spec = pltpu.PrefetchScalarGridSpec(
        num_scalar_prefetch=1, grid=(NSHARD, half // tr),
        in_specs=[pl.BlockSpec((1, 1, tr, width), lambda j, r, c_ref: (j, c_ref[0], r, 0)),
                  pl.BlockSpec((1, tr, width), lambda j, r, c_ref: (j, r, 0))],
        out_specs=pl.BlockSpec((1, tr, width), lambda j, r, c_ref: (j, r, 0)))
    return pl.pallas_call(
        body, grid_spec=grid_spec, out_shape=jax.ShapeDtypeStruct((NSHARD, half, width), F32),
        compiler_params=_cparams(("parallel", "parallel")), name="add_sibling")(c, g4, recv)


def _sum_chips(parts):
    _, half, width = parts.shape
    tr = _tile(half, RED_TR, 8)

    def body(p_ref, o_ref):
        o_ref[...] = ((p_ref[0] + p_ref[1]) + p_ref[2]) + p_ref[3]

    return pl.pallas_call(
        body, grid=(half // tr,),
        in_specs=[pl.BlockSpec((NSHARD, tr, width), lambda r: (0, r, 0))],
        out_specs=pl.BlockSpec((tr, width), lambda r: (r, 0)),
        out_shape=jax.ShapeDtypeStruct((half, width), F32),
        compiler_params=_cparams(("parallel",)), name="sum_chips")(parts)


def _pack_rows(pieces, rows):
    flat = jnp.concatenate([p.reshape(-1) for p in pieces])
    return jnp.pad(flat, (0, rows * PACK_W - flat.shape[0])).reshape(rows, PACK_W)


def _unpack(flat, names, shapes):
    out, off = {}, 0
    for n in names:
        sz = _size(shapes[n])
        out[n] = flat[off:off + sz].reshape(shapes[n])
        off += sz
    return out


def _reorder_w_in(w):
    return jnp.concatenate([w[:, 2048:5120], w[:, 9248:12320], w[:, 0:2048], w[:, 8224:9248], w[:, 5152:6688],
                            w[:, 6688:8224], w[:, 5120:5152], jnp.zeros((D, NP - 12320), w.dtype)], axis=1)


def _restore_w_in(g):
    return jnp.concatenate([g[:, 6144:8192], g[:, 0:3072], g[:, 12288:12320], g[:, 9216:10752], g[:, 10752:12288],
                            g[:, 8192:9216], g[:, 3072:6144]], axis=1)


def _lru_group_weights(w):
    w4 = w.reshape(4, 4, 96, 96)
    eye = jnp.eye(4, dtype=w.dtype)
    return (w4[:, :, None, :, :] * eye[None, :, :, None, None]).transpose(0, 1, 3, 2, 4).reshape(4, LRU_G, LRU_G)


def _lru_group_blocks(g):
    g5 = g.reshape(4, 4, 96, 4, 96)
    return jnp.stack([g5[:, a, :, a, :] for a in range(4)], axis=1).reshape(16, 96, 96)


def _local_grads(x, mem, target, wts):
    mx = lambda a: a.astype(_MXU)
    pad128 = lambda a: jnp.pad(a, ((0, 0), (0, 128 - a.shape[1])))
    w_in_r = mx(_reorder_w_in(wts["w_in"]))
    w_in_rt = w_in_r.T
    wbs, wbl, wbm, wo = mx(wts["w_br_ssd"]), mx(wts["w_br_lru"]), mx(wts["w_br_mem"]), mx(wts["w_out"])
    wkv = mx(wts["w_kv"])
    wa, wx = mx(_lru_group_weights(wts["lru_w_a"])), mx(_lru_group_weights(wts["lru_w_x"]))
    ba, bx = wts["lru_b_a"].reshape(1, LRU_W), wts["lru_b_x"].reshape(1, LRU_W)
    dtb, alog = pad128(wts["ssd_dt_bias"]), pad128(wts["ssd_a_log"])
    dexp = jnp.repeat(wts["ssd_d"], 64, axis=1)
    ng = wts["ssd_norm_g"].reshape(1, SSD_W)
    head = jnp.arange(SSD_W, dtype=jnp.int32)[None, :] // 64
    expand = (jnp.arange(128, dtype=jnp.int32)[:, None] == head).astype(F32)
    reduce_ = expand.T

    h = _norm_fwd(x, wts["norm_g"])
    proj = _mm(h, w_in_r, F32, "in_proj")
    xbc = _conv_fwd(proj, XBC, wts["ssd_conv_w"], wts["ssd_conv_b"], True, "ssd_conv_fwd")
    yssd, yraw, hprev = _ssd_fwd(xbc, proj, dtb, alog, dexp, ng, expand)
    xl = _conv_fwd(proj, LX, wts["lru_conv_w"], wts["lru_conv_b"], False, "lru_conv_fwd")
    ylru, hs = _lru_fwd(xl, proj, wa, wx, ba, bx, wts["lru_lambda"])
    kk, vv, mn = _mem_kv_fwd(mem, wts["mem_norm_g"], wkv)
    ymem = _attn_fwd(proj, kk, vv)

    dproj, dx2, dx2m, merged, db0, db1, db2, loss_vec, dfg = _merge_fb(
        x, target, yssd, ylru, ymem, proj, wbs, wbl, wbm, wo, wts["final_g"].reshape(1, D))
    grads = {"final_g": dfg.reshape(D)}
    grads["w_out"] = _mm(merged.T, dx2m, F32, "dw_out")
    grads["w_br_ssd"] = _mm(yssd.T, db0, F32, "dw_br_ssd")
    grads["w_br_lru"] = _mm(ylru.T, db1, F32, "dw_br_lru")
    grads["w_br_mem"] = _mm(ymem.T, db2, F32, "dw_br_mem")
    dyssd = _mm(db0, wbs.T, F32, "dy_ssd")
    dylru = _mm(db1, wbl.T, F32, "dy_lru")
    dymem = _mm(db2, wbm.T, F32, "dy_mem")

    dproj, dk, dv = _attn_bwd(proj, kk, vv, dymem, dproj)
    grads["w_kv"], grads["mem_norm_g"] = _mem_kv_bwd(mem, wts["mem_norm_g"], mn, wkv, dk, dv)

    dproj, dxl, dwa, dwx, dba, dbx, dlam = _lru_bwd(xl, proj, hs, dylru, dproj, wa, wx, ba, bx, wts["lru_lambda"])
    grads["lru_w_a"] = _lru_group_blocks(dwa)[None]
    grads["lru_w_x"] = _lru_group_blocks(dwx)[None]
    grads["lru_b_a"], grads["lru_b_x"] = dba.reshape(1, 16, 96), dbx.reshape(1, 16, 96)
    grads["lru_lambda"] = dlam
    grads["lru_conv_w"], grads["lru_conv_b"] = _conv_bwd_w(
        proj, LX, wts["lru_conv_w"], wts["lru_conv_b"], dxl, False, "lru_conv_bwd_w")
    dproj = _conv_bwd_x(dxl, wts["lru_conv_w"], dproj, LX, "lru_conv_bwd_x")

    dproj, ddt, dxbc, dng, dda, ddd, ddtb = _ssd_bwd(
        xbc, proj, yraw, hprev, dyssd, dproj, dtb, alog, dexp, ng, expand, reduce_)
    dproj = _put_block(ddt, dproj, DT, "put_ddt")
    grads["ssd_norm_g"] = dng.reshape(4, 512)
    grads["ssd_dt_bias"] = ddtb[:, 0:32]
    grads["ssd_a_log"] = (dda * -jnp.exp(alog))[:, 0:32]
    grads["ssd_d"] = ddd.reshape(32, 64).sum(axis=1)[None, :]
    dpre, grads["ssd_conv_w"], grads["ssd_conv_b"] = _conv_bwd_w(
        proj, XBC, wts["ssd_conv_w"], wts["ssd_conv_b"], dxbc, True, "ssd_conv_bwd_w")
    dproj = _conv_bwd_x(dpre, wts["ssd_conv_w"], dproj, XBC, "ssd_conv_bwd_x")

    grads["w_in"] = _restore_w_in(_mm(h.T, dproj, F32, "dw_in"))
    dh = _mm(dproj, w_in_rt, F32, "dh", tn=1024, tk=1280)
    grad_x, grads["norm_g"] = _norm_bwd(x, wts["norm_g"], dh, dx2)
    return jnp.sum(loss_vec), grad_x, grads


def kernel(x, mem, norm_g, w_in, ssd_conv_w, ssd_conv_b, ssd_dt_bias, ssd_a_log, ssd_d, ssd_norm_g, lru_conv_w, lru_conv_b, lru_w_a, lru_b_a, lru_w_x, lru_b_x, lru_lambda, mem_norm_g, w_kv, w_br_ssd, w_br_lru, w_br_mem, w_out, final_g, loss_target, m_norm_g, m_w_in, m_ssd_conv_w, m_ssd_conv_b, m_ssd_dt_bias, m_ssd_a_log, m_ssd_d, m_ssd_norm_g, m_lru_conv_w, m_lru_conv_b, m_lru_w_a, m_lru_b_a, m_lru_w_x, m_lru_b_x, m_lru_lambda, m_mem_norm_g, m_w_kv, m_w_br_ssd, m_w_br_lru, m_w_br_mem, m_w_out, m_final_g, v_norm_g, v_w_in, v_ssd_conv_w, v_ssd_conv_b, v_ssd_dt_bias, v_ssd_a_log, v_ssd_d, v_ssd_norm_g, v_lru_conv_w, v_lru_conv_b, v_lru_w_a, v_lru_b_a, v_lru_w_x, v_lru_b_x, v_lru_lambda, v_mem_norm_g, v_w_kv, v_w_br_ssd, v_w_br_lru, v_w_br_mem, v_w_out, v_final_g):
    given = dict(locals())
    shard = {n: given[n][0] for n in SHARDED}
    repl = {n: given[n] for n in REPL}

    gathered = _gather_shards(_pack_rows([shard[n] for n in SHARDED], SHARD_ROWS))
    per_chip = [_unpack(gathered[j].reshape(-1), SHARDED, SHARD_SHAPE) for j in range(NSHARD)]
    wts = {n: jnp.concatenate([pc[n] for pc in per_chip], axis=SHARD_AXIS[n]) for n in SHARDED}
    wts.update(repl)
    wts["lru_w_a"], wts["lru_w_x"] = lru_w_a[0], lru_w_x[0]

    loss_part, grad_x, grads = _local_grads(x[0], mem[0], loss_target[0], wts)
    loss = lax.psum(loss_part, ("x", "y", "c"))

    small = jnp.concatenate([grads[n].reshape(-1) for n in REPL])
    small = jnp.pad(small, (0, NSHARD * SMALL_Q - small.shape[0]))
    slots = []
    for j in range(NSHARD):
        pieces = [lax.slice_in_dim(grads[n], j * SHARD_SHAPE[n][SHARD_AXIS[n]], (j + 1) * SHARD_SHAPE[n][SHARD_AXIS[n]],
                                   axis=SHARD_AXIS[n]) for n in SHARDED]
        body = _pack_rows(pieces, SHARD_ROWS)
        tail = small[j * SMALL_Q:(j + 1) * SMALL_Q].reshape(SMALL_ROWS, PACK_W)
        slots.append(jnp.concatenate([body, tail, jnp.zeros((GRAD_ROWS - SHARD_ROWS - SMALL_ROWS, PACK_W), F32)]))
    g4 = jnp.stack(slots).reshape(NSHARD, 2, GRAD_HALF, PACK_W)
    c_idx = lax.axis_index("c").astype(jnp.int32).reshape(1)
    chip_part = _add_sibling(g4, _swap_halves(g4), c_idx)
    reduced_half = _sum_chips(_scatter_chips(chip_part))
    reduced = _share_half(reduced_half).reshape(GRAD_ROWS, PACK_W)
    small_all = _gather_small(reduced).reshape(-1)

    g_shard = _unpack(reduced.reshape(-1), SHARDED, SHARD_SHAPE)
    g_repl = _unpack(small_all, REPL, REPL_SHAPE)

    out_g, out_d, out_m, out_v = {}, {}, {}, {}
    for n in WEIGHTS:
        w_full = given[n]
        g = (g_shard[n][None] if n in SHARDED else g_repl[n]).reshape(w_full.shape)
        cols = w_full.shape[-1]
        as2d = lambda a: a.reshape(-1, cols)
        d, mo, vo = _adamw(as2d(w_full), as2d(g), as2d(given["m_" + n]), as2d(given["v_" + n]), "adamw_" + n)
        out_g[n] = g
        out_d[n], out_m[n], out_v[n] = d.reshape(w_full.shape), mo.reshape(w_full.shape), vo.reshape(w_full.shape)

    return (loss, grad_x[None], *[out_g[n] for n in WEIGHTS], *[out_d[n] for n in WEIGHTS],
            *[out_m[n] for n in WEIGHTS], *[out_v[n] for n in WEIGHTS])
```

```python
import jax
import jax.numpy as jnp
from jax import lax
from jax.experimental import pallas as pl
from jax.experimental.pallas import tpu as pltpu

F32 = jnp.float32
_MXU = jnp.bfloat16
_HI = lax.Precision.HIGHEST
MESH = pl.DeviceIdType.MESH

D = 1024
EPS = 1e-6
MEM_HEADS = 4
MEM_HD = 256
LRU_C = 8.0
SSD_L = 128
SSD_W = 2048
LRU_W = 1536
NSHARD = 4

XBC = (0, 3072)
GL = (3072, 3072)
Z = (6144, 2048)
Q = (8192, 1024)
LG = (9216, 1536)
LX = (10752, 1536)
DT = (12288, 512)
NP = 12800

ADAM_LR = 0.001
ADAM_B1 = 0.9
ADAM_B2 = 0.999
ADAM_EPS = 1e-08
ADAM_WD = 0.01
ADAM_STEP = 10

VMEM_LIMIT = 56 * 1024 * 1024

SHARDED = ("w_in", "ssd_conv_w", "ssd_norm_g", "lru_conv_w", "w_kv", "w_br_ssd", "w_br_lru", "w_br_mem", "w_out")
SHARD_SHAPE = {"w_in": (1024, 3080), "ssd_conv_w": (4, 768), "ssd_norm_g": (4, 128), "lru_conv_w": (4, 384),
               "w_kv": (1024, 512), "w_br_ssd": (512, 1024), "w_br_lru": (384, 1024), "w_br_mem": (256, 1024),
               "w_out": (256, 1024)}
REPL = ("norm_g", "ssd_conv_b", "ssd_dt_bias", "ssd_a_log", "ssd_d", "lru_conv_b", "lru_w_a", "lru_b_a",
        "lru_w_x", "lru_b_x", "lru_lambda", "mem_norm_g", "final_g")
REPL_SHAPE = {"norm_g": (1, 1024), "ssd_conv_b": (1, 3072), "ssd_dt_bias": (1, 32), "ssd_a_log": (1, 32),
              "ssd_d": (1, 32), "lru_conv_b": (1, 1536), "lru_w_a": (1, 16, 96, 96), "lru_b_a": (1, 16, 96),
              "lru_w_x": (1, 16, 96, 96), "lru_b_x": (1, 16, 96), "lru_lambda": (1, 1536),
              "mem_norm_g": (1, 1024), "final_g": (1024,)}
WEIGHTS = ("norm_g", "w_in", "ssd_conv_w", "ssd_conv_b", "ssd_dt_bias", "ssd_a_log", "ssd_d", "ssd_norm_g",
           "lru_conv_w", "lru_conv_b", "lru_w_a", "lru_b_a", "lru_w_x", "lru_b_x", "lru_lambda", "mem_norm_g",
           "w_kv", "w_br_ssd", "w_br_lru", "w_br_mem", "w_out", "final_g")

ROW_PIECES = (("w_br_ssd", 0, 512), ("w_br_lru", 512, 896), ("w_br_mem", 896, 1152), ("w_out", 1152, 1408))
SMALL_SHARDED = ("ssd_conv_w", "ssd_norm_g", "lru_conv_w")
PACK_W = 512
SMALL_ROWS = 152
SMALL_Q = SMALL_ROWS * PACK_W
SMALL_BUF_ROWS = 176


def _size(shape):
    n = 1
    for s in shape:
        n *= s
    return n


def _sigmoid(x):
    return 0.5 * jnp.tanh(0.5 * x) + 0.5


def _silu(x):
    return x * _sigmoid(x)


def _dsilu(x):
    s = _sigmoid(x)
    return s * (1.0 + x * (1.0 - s))


def _softplus(x):
    return jnp.maximum(x, 0.0) + jnp.log(1.0 + jnp.exp(-jnp.abs(x)))


def _neg_expm1(x):
    series = -x * (1.0 + x * (0.5 + x * (1.0 / 6.0 + x * (1.0 / 24.0 + x * (1.0 / 120.0)))))
    return jnp.where(x > -0.03, series, 1.0 - jnp.exp(x))


def _dot(a, b, precision=None):
    return jnp.dot(a, b, preferred_element_type=F32, precision=precision)


def _dot_nt(a, b):
    return lax.dot_general(a, b, (((1,), (1,)), ((), ())), preferred_element_type=F32)


def _dot_tn(a, b):
    return lax.dot_general(a, b, (((0,), (0,)), ((), ())), preferred_element_type=F32)


def _mx(a):
    return a.astype(_MXU)


def _cparams(sem):
    return pltpu.CompilerParams(dimension_semantics=sem, vmem_limit_bytes=VMEM_LIMIT)


def _tile(n, want, mult=128):
    if n <= want:
        return n
    for t in range(want - want % mult, 0, -mult):
        if n % t == 0:
            return t
    raise ValueError((n, want, mult))


def _mm(a, b, out_dtype, name, ta=False, tb=False, tm=1024, tn=1280, tk=1024):
    k, m = a.shape if ta else a.shape[::-1]
    k2, n = b.shape[::-1] if tb else b.shape
    assert k == k2
    tm, tn, tk = _tile(m, tm), _tile(n, tn), _tile(k, tk)
    nk = k // tk
    contract = (((0 if ta else 1,), (1 if tb else 0,)), ((), ()))

    def body(a_ref, b_ref, o_ref, acc_ref):
        kk = pl.program_id(2)

        @pl.when(kk == 0)
        def _():
            acc_ref[...] = jnp.zeros_like(acc_ref)

        acc_ref[...] += lax.dot_general(a_ref[...], b_ref[...], contract, preferred_element_type=F32)

        @pl.when(kk == nk - 1)
        def _():
            o_ref[...] = acc_ref[...].astype(o_ref.dtype)

    a_spec = pl.BlockSpec((tk, tm), lambda i, j, kk: (kk, i)) if ta else pl.BlockSpec((tm, tk), lambda i, j, kk: (i, kk))
    b_spec = pl.BlockSpec((tn, tk), lambda i, j, kk: (j, kk)) if tb else pl.BlockSpec((tk, tn), lambda i, j, kk: (kk, j))
    return pl.pallas_call(
        body, grid=(m // tm, n // tn, nk),
        in_specs=[a_spec, b_spec],
        out_specs=pl.BlockSpec((tm, tn), lambda i, j, kk: (i, j)),
        out_shape=jax.ShapeDtypeStruct((m, n), out_dtype),
        scratch_shapes=[pltpu.VMEM((tm, tn), F32)],
        compiler_params=_cparams(("parallel", "parallel", "arbitrary")), name=name)(a, b)


def _norm_fwd(x, g):
    s = x.shape[0]
    ts = _tile(s, 512)

    def body(x_ref, g_ref, h_ref):
        xv = x_ref[...]
        r = lax.rsqrt(jnp.mean(xv * xv, axis=-1, keepdims=True) + EPS)
        h_ref[...] = (xv * r * g_ref[...]).astype(h_ref.dtype)

    return pl.pallas_call(
        body, grid=(s // ts,),
        in_specs=[pl.BlockSpec((ts, D), lambda i: (i, 0)), pl.BlockSpec((1, D), lambda i: (0, 0))],
        out_specs=pl.BlockSpec((ts, D), lambda i: (i, 0)),
        out_shape=jax.ShapeDtypeStruct((s, D), _MXU),
        compiler_params=_cparams(("parallel",)), name="norm_fwd")(x, g)


def _norm_bwd(x, g, dh, dx2):
    s = x.shape[0]
    ts = _tile(s, 512)

    def body(x_ref, g_ref, dh_ref, dx2_ref, gx_ref, dg_ref):
        @pl.when(pl.program_id(0) == 0)
        def _():
            dg_ref[...] = jnp.zeros_like(dg_ref)

        xv = x_ref[...]
        r = lax.rsqrt(jnp.mean(xv * xv, axis=-1, keepdims=True) + EPS)
        xhat = xv * r
        dh_v = dh_ref[...]
        dg_ref[...] += jnp.sum(dh_v * xhat, axis=0, keepdims=True)
        dxh = dh_v * g_ref[...]
        gx_ref[...] = dx2_ref[...] + r * (dxh - xhat * jnp.mean(dxh * xhat, axis=-1, keepdims=True))

    row = pl.BlockSpec((ts, D), lambda i: (i, 0))
    vec = pl.BlockSpec((1, D), lambda i: (0, 0))
    return pl.pallas_call(
        body, grid=(s // ts,), in_specs=[row, vec, row, row], out_specs=[row, vec],
        out_shape=[jax.ShapeDtypeStruct((s, D), F32), jax.ShapeDtypeStruct((1, D), F32)],
        compiler_params=_cparams(("arbitrary",)), name="norm_bwd")(x, g, dh, dx2)


def _conv_taps(ext_ref, w_ref, ts, base):
    acc = None
    for k in range(4):
        t = w_ref[k:k + 1, :] * ext_ref[base + k:base + k + ts, :]
        acc = t if acc is None else acc + t
    return acc


def _conv_fwd(src, blk, w, b, act, name):
    s = src.shape[0]
    off, width = blk
    cb = off // width
    ts = _tile(s, 256)

    def body(x_ref, w_ref, b_ref, o_ref, ext_ref):
        @pl.when(pl.program_id(0) == 0)
        def _():
            ext_ref[0:8, :] = jnp.zeros((8, width), F32)

        ext_ref[8:8 + ts, :] = x_ref[...]
        pre = _conv_taps(ext_ref, w_ref, ts, 5) + b_ref[...]
        o_ref[...] = _silu(pre) if act else pre
        ext_ref[0:8, :] = x_ref[ts - 8:ts, :]

    return pl.pallas_call(
        body, grid=(s // ts,),
        in_specs=[pl.BlockSpec((ts, width), lambda i: (i, cb)), pl.BlockSpec((4, width), lambda i: (0, 0)),
                  pl.BlockSpec((1, width), lambda i: (0, 0))],
        out_specs=pl.BlockSpec((ts, width), lambda i: (i, 0)),
        out_shape=jax.ShapeDtypeStruct((s, width), F32),
        scratch_shapes=[pltpu.VMEM((ts + 8, width), F32)],
        compiler_params=_cparams(("arbitrary",)), name=name)(src, w, b)


def _conv_bwd_w(src, blk, w, b, dout, act, name):
    s = src.shape[0]
    off, width = blk
    cb = off // width
    ts = _tile(s, 256)

    def body(x_ref, w_ref, b_ref, do_ref, *rest):
        if act:
            dpre_ref, dw_ref, db_ref, ext_ref = rest
        else:
            dw_ref, db_ref, ext_ref = rest

        @pl.when(pl.program_id(0) == 0)
        def _():
            ext_ref[0:8, :] = jnp.zeros((8, width), F32)
            dw_ref[...] = jnp.zeros_like(dw_ref)
            db_ref[...] = jnp.zeros_like(db_ref)

        ext_ref[8:8 + ts, :] = x_ref[...]
        if act:
            pre = _conv_taps(ext_ref, w_ref, ts, 5) + b_ref[...]
            dpre = do_ref[...] * _dsilu(pre)
            dpre_ref[...] = dpre
        else:
            dpre = do_ref[...]
        db_ref[...] += jnp.sum(dpre, axis=0, keepdims=True)
        for k in range(4):
            dw_ref[k:k + 1, :] += jnp.sum(dpre * ext_ref[5 + k:5 + k + ts, :], axis=0, keepdims=True)
        ext_ref[0:8, :] = x_ref[ts - 8:ts, :]

    row = pl.BlockSpec((ts, width), lambda i: (i, 0))
    outs = [pl.BlockSpec((4, width), lambda i: (0, 0)), pl.BlockSpec((1, width), lambda i: (0, 0))]
    shapes = [jax.ShapeDtypeStruct((4, width), F32), jax.ShapeDtypeStruct((1, width), F32)]
    if act:
        outs = [row] + outs
        shapes = [jax.ShapeDtypeStruct((s, width), F32)] + shapes
    return pl.pallas_call(
        body, grid=(s // ts,),
        in_specs=[pl.BlockSpec((ts, width), lambda i: (i, cb)), pl.BlockSpec((4, width), lambda i: (0, 0)),
                  pl.BlockSpec((1, width), lambda i: (0, 0)), row],
        out_specs=outs, out_shape=shapes,
        scratch_shapes=[pltpu.VMEM((ts + 8, width), F32)],
        compiler_params=_cparams(("arbitrary",)), name=name)(src, w, b, dout)


def _conv_bwd_x(dpre, w, dproj, blk, name):
    s = dpre.shape[0]
    off, width = blk
    cb = off // width
    ts = _tile(s, 256)
    nt = s // ts

    def body(dp_ref, w_ref, dproj_hbm, o_ref, ext_ref):
        del dproj_hbm

        @pl.when(pl.program_id(0) == 0)
        def _():
            ext_ref[ts:ts + 8, :] = jnp.zeros((8, width), F32)

        ext_ref[0:ts, :] = dp_ref[...]
        acc = None
        for k in range(4):
            t = w_ref[k:k + 1, :] * ext_ref[3 - k:3 - k + ts, :]
            acc = t if acc is None else acc + t
        o_ref[...] = acc.astype(o_ref.dtype)
        ext_ref[ts:ts + 8, :] = dp_ref[0:8, :]

    return pl.pallas_call(
        body, grid=(nt,),
        in_specs=[pl.BlockSpec((ts, width), lambda i: (nt - 1 - i, 0)), pl.BlockSpec((4, width), lambda i: (0, 0)),
                  pl.BlockSpec(memory_space=pl.ANY)],
        out_specs=pl.BlockSpec((ts, width), lambda i: (nt - 1 - i, cb)),
        out_shape=jax.ShapeDtypeStruct(dproj.shape, dproj.dtype),
        scratch_shapes=[pltpu.VMEM((ts + 8, width), F32)],
        input_output_aliases={2: 0},
        compiler_params=_cparams(("arbitrary",)), name=name)(dpre, w, dproj)


def _head_col(v, h, lane_l):
    return jnp.sum(jnp.where(lane_l == h, v, 0.0), axis=1, keepdims=True)


def _ssd_decay(a_cs, acst_ref, h, causal, lane_l):
    row = acst_ref[h:h + 1, :]
    return jnp.where(causal, jnp.exp(jnp.minimum(_head_col(a_cs, h, lane_l) - row, 0.0)), 0.0)


def _ssd_common(dt_ref, dtb_ref, alog_ref, acst_ref, dtx_ref, acx_ref):
    ll = SSD_L
    dt = _softplus(dt_ref[:, 0:128] + dtb_ref[...])
    a_neg = -jnp.exp(alog_ref[...])
    ri = lax.broadcasted_iota(jnp.int32, (ll, ll), 0)
    ci = lax.broadcasted_iota(jnp.int32, (ll, ll), 1)
    causal = ri >= ci
    a_cs = _dot(causal.astype(F32), dt * a_neg, _HI)
    acst_ref[...] = a_cs.T
    lane_l = lax.broadcasted_iota(jnp.int32, (ll, 128), 1)
    lo = lane_l < 64
    for j in range(16):
        sl = slice(128 * j, 128 * j + 128)
        dtx_ref[:, sl] = jnp.where(lo, _head_col(dt, 2 * j, lane_l), _head_col(dt, 2 * j + 1, lane_l))
        acx_ref[:, sl] = jnp.where(lo, _head_col(a_cs, 2 * j, lane_l), _head_col(a_cs, 2 * j + 1, lane_l))
    return dt, a_neg, a_cs, causal, ri, lane_l, lo


def _ssd_fwd(xbc, proj, dtb, alog, dexp, ng):
    s = xbc.shape[0]
    ll = SSD_L
    nc = s // ll

    def body(xbc_ref, dt_ref, z_ref, dtb_ref, alog_ref, dexp_ref, ng_ref,
             yssd_ref, yraw_ref, hprev_ref, ht_ref, acst_ref, dtx_ref, acx_ref):
        @pl.when(pl.program_id(0) == 0)
        def _():
            ht_ref[...] = jnp.zeros_like(ht_ref)

        hprev_ref[0] = ht_ref[...]
        _, _, a_cs, causal, _, lane_l, lo = _ssd_common(dt_ref, dtb_ref, alog_ref, acst_ref, dtx_ref, acx_ref)
        for g in range(4):
            bg = _mx(xbc_ref[:, 2048 + 128 * g:2176 + 128 * g])
            cg = _mx(xbc_ref[:, 2560 + 128 * g:2688 + 128 * g])
            cbm = _dot_nt(cg, bg)
            for jj in range(4):
                j = 4 * g + jj
                sl = slice(128 * j, 128 * j + 128)
                xp = xbc_ref[:, sl]
                acx = acx_ref[:, sl]
                a_last = acx_ref[ll - 1:ll, sl]
                xdt = xp * dtx_ref[:, sl]
                acc = None
                for hh in range(2):
                    dec = _ssd_decay(a_cs, acst_ref, 2 * j + hh, causal, lane_l)
                    xm = jnp.where(lo if hh == 0 else jnp.logical_not(lo), xdt, 0.0)
                    t = _dot(_mx(dec * cbm), _mx(xm))
                    acc = t if acc is None else acc + t
                ht = ht_ref[j]
                y = acc + _dot(cg, _mx(ht)) * jnp.exp(acx) + xp * dexp_ref[:, sl]
                yraw_ref[:, sl] = y
                st = _dot_tn(bg, _mx(xdt * jnp.exp(a_last - acx)))
                ht_ref[j] = ht * jnp.exp(a_last) + st
        for g in range(4):
            sl = slice(512 * g, 512 * g + 512)
            yg = yraw_ref[:, sl] * _silu(z_ref[:, sl])
            r = lax.rsqrt(jnp.mean(yg * yg, axis=-1, keepdims=True) + EPS)
            yssd_ref[:, sl] = (yg * r * ng_ref[:, sl]).astype(yssd_ref.dtype)

    vec = lambda w: pl.BlockSpec((1, w), lambda c: (0, 0))
    return pl.pallas_call(
        body, grid=(nc,),
        in_specs=[pl.BlockSpec((ll, 3072), lambda c: (c, 0)),
                  pl.BlockSpec((ll, DT[1]), lambda c: (c, DT[0] // DT[1])),
                  pl.BlockSpec((ll, Z[1]), lambda c: (c, Z[0] // Z[1])),
                  vec(128), vec(128), vec(2048), vec(2048)],
        out_specs=[pl.BlockSpec((ll, 2048), lambda c: (c, 0)), pl.BlockSpec((ll, 2048), lambda c: (c, 0)),
                   pl.BlockSpec((1, 16, 128, 128), lambda c: (c, 0, 0, 0))],
        out_shape=[jax.ShapeDtypeStruct((s, 2048), _MXU), jax.ShapeDtypeStruct((s, 2048), F32),
                   jax.ShapeDtypeStruct((nc, 16, 128, 128), F32)],
        scratch_shapes=[pltpu.VMEM((16, 128, 128), F32), pltpu.VMEM((128, ll), F32),
                        pltpu.VMEM((ll, 2048), F32), pltpu.VMEM((ll, 2048), F32)],
        compiler_params=_cparams(("arbitrary",)), name="ssd_fwd")(xbc, proj, proj, dtb, alog, dexp, ng)


def _ssd_bwd(xbc, proj, yraw, hprev, dyssd, dproj, dtb, alog, dexp, ng):
    s = xbc.shape[0]
    ll = SSD_L
    nc = s // ll

    def body(xbc_ref, dt_ref, z_ref, yraw_ref, hprev_ref, dy_ref, dproj_hbm, dtb_ref, alog_ref, dexp_ref, ng_ref,
             dz_ref, ddt_ref, dxbc_ref, dng_ref, dda_ref, ddd_ref, ddtb_ref,
             dht_ref, acst_ref, dtx_ref, acx_ref, dyr_ref, rowt_ref):
        del dproj_hbm

        @pl.when(pl.program_id(0) == 0)
        def _():
            dht_ref[...] = jnp.zeros_like(dht_ref)
            dng_ref[...] = jnp.zeros_like(dng_ref)
            dda_ref[...] = jnp.zeros_like(dda_ref)
            ddd_ref[...] = jnp.zeros_like(ddd_ref)
            ddtb_ref[...] = jnp.zeros_like(ddtb_ref)
            rowt_ref[...] = jnp.zeros_like(rowt_ref)

        for g in range(4):
            sl = slice(512 * g, 512 * g + 512)
            zz = z_ref[:, sl]
            yr = yraw_ref[:, sl]
            sz = _silu(zz)
            yg = yr * sz
            r = lax.rsqrt(jnp.mean(yg * yg, axis=-1, keepdims=True) + EPS)
            yhat = yg * r
            dyv = dy_ref[:, sl]
            dng_ref[:, sl] += jnp.sum(dyv * yhat, axis=0, keepdims=True)
            dyh = dyv * ng_ref[:, sl]
            dyg = r * (dyh - yhat * jnp.mean(dyh * yhat, axis=-1, keepdims=True))
            dz_ref[:, sl] = (dyg * yr * _dsilu(zz)).astype(dz_ref.dtype)
            dyr_ref[:, sl] = dyg * sz

        dt, a_neg, a_cs, causal, ri, lane_l, lo = _ssd_common(dt_ref, dtb_ref, alog_ref, acst_ref, dtx_ref, acx_ref)
        lane_1 = lax.broadcasted_iota(jnp.int32, (1, 128), 1)
        da_col = jnp.zeros((ll, 128), F32)
        ddt_x = jnp.zeros((ll, 128), F32)
        last = jnp.zeros((1, 128), F32)
        for g in range(4):
            bg = _mx(xbc_ref[:, 2048 + 128 * g:2176 + 128 * g])
            cg = _mx(xbc_ref[:, 2560 + 128 * g:2688 + 128 * g])
            cbm = _dot_nt(cg, bg)
            dcb = jnp.zeros((ll, ll), F32)
            db_g = jnp.zeros((ll, 128), F32)
            dc_g = jnp.zeros((ll, 128), F32)
            for jj in range(4):
                j = 4 * g + jj
                sl = slice(128 * j, 128 * j + 128)
                xp = xbc_ref[:, sl]
                dtx = dtx_ref[:, sl]
                acx = acx_ref[:, sl]
                a_last = acx_ref[ll - 1:ll, sl]
                ea = jnp.exp(acx)
                dte = jnp.exp(a_last - acx)
                cd = jnp.exp(a_last)
                xdt = xp * dtx
                xdt_m = _mx(xdt)
                dy = dyr_ref[:, sl]
                ht = hprev_ref[0, j]
                dhn = dht_ref[j]
                dhn_m = _mx(dhn)
                gmat = _dot(bg, dhn_m)
                dxdt = gmat * dte
                for hh in range(2):
                    h = 2 * j + hh
                    dec = _ssd_decay(a_cs, acst_ref, h, causal, lane_l)
                    mm = dec * cbm
                    dym = _mx(jnp.where(lo if hh == 0 else jnp.logical_not(lo), dy, 0.0))
                    dxdt = dxdt + _dot_tn(_mx(mm), dym)
                    dm = _dot_nt(dym, xdt_m)
                    dcb = dcb + dm * dec
                    qq = dm * mm
                    da_col = da_col + jnp.where(lane_l == h, jnp.sum(qq, axis=1, keepdims=True), 0.0)
                    rowt_ref[h:h + 1, :] = jnp.sum(qq, axis=0, keepdims=True)
                ch = _dot(cg, _mx(ht))
                dyea = dy * ea
                dyea_m = _mx(dyea)
                xw_m = _mx(xdt * dte)
                dc_g = dc_g + _dot_nt(dyea_m, _mx(ht))
                db_g = db_g + _dot_nt(xw_m, dhn_m)
                wl = xdt * gmat * dte
                lane_a = dyea * ch - wl
                lane_b = dxdt * xp
                lane_c = jnp.sum(dhn * ht, axis=0, keepdims=True) * cd + jnp.sum(wl, axis=0, keepdims=True)
                for hh in range(2):
                    h = 2 * j + hh
                    mine = lo if hh == 0 else jnp.logical_not(lo)
                    da_col = da_col + jnp.where(
                        lane_l == h, jnp.sum(jnp.where(mine, lane_a, 0.0), axis=1, keepdims=True), 0.0)
                    ddt_x = ddt_x + jnp.where(
                        lane_l == h, jnp.sum(jnp.where(mine, lane_b, 0.0), axis=1, keepdims=True), 0.0)
                    mine_1 = (lane_1 < 64) if hh == 0 else (lane_1 >= 64)
                    last = last + jnp.where(
                        lane_1 == h, jnp.sum(jnp.where(mine_1, lane_c, 0.0), axis=1, keepdims=True), 0.0)
                dht_ref[j] = dhn * cd + _dot_tn(cg, dyea_m)
                dxbc_ref[:, sl] = dxdt * dtx + dy * dexp_ref[:, sl]
                ddd_ref[:, sl] += jnp.sum(dy * xp, axis=0, keepdims=True)
            dcb_m = _mx(dcb)
            dxbc_ref[:, 2048 + 128 * g:2176 + 128 * g] = db_g + _dot_tn(dcb_m, cg)
            dxbc_ref[:, 2560 + 128 * g:2688 + 128 * g] = dc_g + _dot(dcb_m, bg)

        da_cs = da_col - rowt_ref[...].T
        da_cs = da_cs + jnp.where(lax.broadcasted_iota(jnp.int32, (ll, 128), 0) == ll - 1, last, 0.0)
        d_dta = _dot((ri <= lax.broadcasted_iota(jnp.int32, (ll, ll), 1)).astype(F32), da_cs, _HI)
        ddt = d_dta * a_neg + ddt_x
        dda_ref[...] += jnp.sum(d_dta * dt, axis=0, keepdims=True)
        ddt_raw = ddt * _sigmoid(dt_ref[:, 0:128] + dtb_ref[...])
        ddtb_ref[...] += jnp.sum(ddt_raw, axis=0, keepdims=True)
        ddt_ref[:, 0:128] = ddt_raw.astype(ddt_ref.dtype)
        ddt_ref[:, 128:512] = jnp.zeros((ll, 384), ddt_ref.dtype)

    rev = lambda c: nc - 1 - c
    vec = lambda w: pl.BlockSpec((1, w), lambda c: (0, 0))
    row = lambda w: pl.BlockSpec((ll, w), lambda c: (rev(c), 0))
    outs = pl.pallas_call(
        body, grid=(nc,),
        in_specs=[row(3072),
                  pl.BlockSpec((ll, DT[1]), lambda c: (rev(c), DT[0] // DT[1])),
                  pl.BlockSpec((ll, Z[1]), lambda c: (rev(c), Z[0] // Z[1])),
                  row(2048),
                  pl.BlockSpec((1, 16, 128, 128), lambda c: (rev(c), 0, 0, 0)),
                  row(2048),
                  pl.BlockSpec(memory_space=pl.ANY),
                  vec(128), vec(128), vec(2048), vec(2048)],
        out_specs=[pl.BlockSpec((ll, Z[1]), lambda c: (rev(c), Z[0] // Z[1])),
                   row(DT[1]),
                   row(3072), vec(2048), vec(128), vec(2048), vec(128)],
        out_shape=[jax.ShapeDtypeStruct(dproj.shape, dproj.dtype), jax.ShapeDtypeStruct((s, DT[1]), dproj.dtype),
                   jax.ShapeDtypeStruct((s, 3072), F32), jax.ShapeDtypeStruct((1, 2048), F32),
                   jax.ShapeDtypeStruct((1, 128), F32), jax.ShapeDtypeStruct((1, 2048), F32),
                   jax.ShapeDtypeStruct((1, 128), F32)],
        scratch_shapes=[pltpu.VMEM((16, 128, 128), F32), pltpu.VMEM((128, ll), F32),
                        pltpu.VMEM((ll, 2048), F32), pltpu.VMEM((ll, 2048), F32), pltpu.VMEM((ll, 2048), F32),
                        pltpu.VMEM((128, ll), F32)],
        input_output_aliases={6: 0},
        compiler_params=_cparams(("arbitrary",)), name="ssd_bwd")(
            xbc, proj, proj, yraw, hprev, dyssd, dproj, dtb, alog, dexp, ng)
    return outs


def _put_block(src, dproj, blk, name):
    s = src.shape[0]
    off, width = blk
    cb = off // width
    ts = _tile(s, 1024)

    def body(s_ref, dproj_hbm, o_ref):
        del dproj_hbm
        o_ref[...] = s_ref[...]

    return pl.pallas_call(
        body, grid=(s // ts,),
        in_specs=[pl.BlockSpec((ts, width), lambda i: (i, 0)), pl.BlockSpec(memory_space=pl.ANY)],
        out_specs=pl.BlockSpec((ts, width), lambda i: (i, cb)),
        out_shape=jax.ShapeDtypeStruct(dproj.shape, dproj.dtype),
        input_output_aliases={1: 0},
        compiler_params=_cparams(("parallel",)), name=name)(src, dproj)


LRU_G = 384


def _lru_gates(xl_ref, wa_ref, wx_ref, ba_ref, bx_ref, lam_ref, g):
    sl = slice(LRU_G * g, LRU_G * g + LRU_G)
    xg = xl_ref[:, sl]
    xm = _mx(xg)
    r = _sigmoid(_dot(xm, wa_ref[g]) + ba_ref[:, sl])
    ig = _sigmoid(_dot(xm, wx_ref[g]) + bx_ref[:, sl])
    sp = _softplus(-lam_ref[:, sl])
    log_a = (-LRU_C * r) * sp
    a = jnp.exp(log_a)
    mult = jnp.sqrt(_neg_expm1(2.0 * log_a))
    return sl, xg, r, ig, sp, a, mult


def _lru_fwd(xl, proj, wa, wx, ba, bx, lam):
    s = xl.shape[0]
    ts = _tile(s, 256)
    w = LRU_W

    def body(xl_ref, lg_ref, wa_ref, wx_ref, ba_ref, bx_ref, lam_ref, y_ref, hs_ref, a_ref, u_ref, carry_ref):
        @pl.when(pl.program_id(0) == 0)
        def _():
            carry_ref[...] = jnp.zeros_like(carry_ref)

        for g in range(4):
            sl, xg, _, ig, _, a, mult = _lru_gates(xl_ref, wa_ref, wx_ref, ba_ref, bx_ref, lam_ref, g)
            a_ref[:, sl] = a
            u_ref[:, sl] = mult * (ig * xg)

        def step(t, h):
            h = a_ref[pl.ds(t, 1), :] * h + u_ref[pl.ds(t, 1), :]
            hs_ref[pl.ds(t, 1), :] = h
            return h

        carry_ref[0:1, :] = lax.fori_loop(0, ts, step, carry_ref[0:1, :], unroll=8)
        y_ref[...] = (hs_ref[...] * _silu(lg_ref[...])).astype(y_ref.dtype)

    row = pl.BlockSpec((ts, w), lambda i: (i, 0))
    vec = pl.BlockSpec((1, w), lambda i: (0, 0))
    wsp = pl.BlockSpec((4, LRU_G, LRU_G), lambda i: (0, 0, 0))
    return pl.pallas_call(
        body, grid=(s // ts,),
        in_specs=[row, pl.BlockSpec((ts, w), lambda i: (i, LG[0] // w)), wsp, wsp, vec, vec, vec],
        out_specs=[row, row],
        out_shape=[jax.ShapeDtypeStruct((s, w), _MXU), jax.ShapeDtypeStruct((s, w), F32)],
        scratch_shapes=[pltpu.VMEM((ts, w), F32), pltpu.VMEM((ts, w), F32), pltpu.VMEM((8, w), F32)],
        compiler_params=_cparams(("arbitrary",)), name="lru_fwd")(xl, proj, wa, wx, ba, bx, lam)


def _lru_bwd(xl, proj, hs, dy, dproj, wa, wx, ba, bx, lam):
    s = xl.shape[0]
    ts = _tile(s, 256)
    nt = s // ts
    w = LRU_W
    hb = ts // 8

    def body(xl_ref, lg_ref, hs_ref, hprev_ref, dy_ref, dproj_hbm, wa_ref, wx_ref, ba_ref, bx_ref, lam_ref,
             dlg_ref, dxl_ref, dwa_ref, dwx_ref, dba_ref, dbx_ref, dlam_ref,
             a_ref, dh_ref, ext_ref, carry_ref):
        del dproj_hbm
        i = pl.program_id(0)

        @pl.when(i == 0)
        def _():
            carry_ref[...] = jnp.zeros_like(carry_ref)
            for ref in (dwa_ref, dwx_ref, dba_ref, dbx_ref, dlam_ref):
                ref[...] = jnp.zeros_like(ref)

        lg = lg_ref[...]
        dyv = dy_ref[...]
        dh_ref[...] = dyv * _silu(lg)
        dlg_ref[...] = (dyv * hs_ref[...] * _dsilu(lg)).astype(dlg_ref.dtype)
        for g in range(4):
            sl, _, _, _, _, a, _ = _lru_gates(xl_ref, wa_ref, wx_ref, ba_ref, bx_ref, lam_ref, g)
            a_ref[:, sl] = a

        def step(k, carry):
            t = ts - 1 - k
            dh = dh_ref[pl.ds(t, 1), :] + carry
            dh_ref[pl.ds(t, 1), :] = dh
            return a_ref[pl.ds(t, 1), :] * dh

        carry_ref[0:1, :] = lax.fori_loop(0, ts, step, carry_ref[0:1, :], unroll=8)

        ext_ref[0:8, :] = jnp.where(i == nt - 1, 0.0, 1.0) * hprev_ref[...]
        ext_ref[8:8 + ts, :] = hs_ref[...]
        for g in range(4):
            sl, xg, r, ig, sp, a, mult = _lru_gates(xl_ref, wa_ref, wx_ref, ba_ref, bx_ref, lam_ref, g)
            dh = dh_ref[:, sl]
            da = dh * ext_ref[7:7 + ts, sl]
            dmult = dh * ig * xg
            di = dh * mult * xg
            dxl = dh * mult * ig
            dlog_a = da * a - dmult * (a * a) / mult
            dlam_ref[:, sl] += jnp.sum(dlog_a * r, axis=0, keepdims=True) * (LRU_C * _sigmoid(-lam_ref[:, sl]))
            dpa = dlog_a * (-LRU_C * sp) * r * (1.0 - r)
            dpx = di * ig * (1.0 - ig)
            dba_ref[:, sl] += jnp.sum(dpa, axis=0, keepdims=True)
            dbx_ref[:, sl] += jnp.sum(dpx, axis=0, keepdims=True)
            dpa_m, dpx_m, xm = _mx(dpa), _mx(dpx), _mx(xg)
            dxl_ref[:, sl] = dxl + _dot_nt(dpa_m, wa_ref[g]) + _dot_nt(dpx_m, wx_ref[g])
            dwa_ref[g] += _dot_tn(xm, dpa_m)
            dwx_ref[g] += _dot_tn(xm, dpx_m)

    rev = lambda i: nt - 1 - i
    row = pl.BlockSpec((ts, w), lambda i: (rev(i), 0))
    vec = pl.BlockSpec((1, w), lambda i: (0, 0))
    wsp = pl.BlockSpec((4, LRU_G, LRU_G), lambda i: (0, 0, 0))
    lgs = pl.BlockSpec((ts, w), lambda i: (rev(i), LG[0] // w))
    return pl.pallas_call(
        body, grid=(nt,),
        in_specs=[row, lgs, row, pl.BlockSpec((8, w), lambda i: (jnp.maximum(rev(i) * hb - 1, 0), 0)), row,
                  pl.BlockSpec(memory_space=pl.ANY), wsp, wsp, vec, vec, vec],
        out_specs=[lgs, row, wsp, wsp, vec, vec, vec],
        out_shape=[jax.ShapeDtypeStruct(dproj.shape, dproj.dtype), jax.ShapeDtypeStruct((s, w), F32),
                   jax.ShapeDtypeStruct((4, LRU_G, LRU_G), F32), jax.ShapeDtypeStruct((4, LRU_G, LRU_G), F32),
                   jax.ShapeDtypeStruct((1, w), F32), jax.ShapeDtypeStruct((1, w), F32),
                   jax.ShapeDtypeStruct((1, w), F32)],
        scratch_shapes=[pltpu.VMEM((ts, w), F32), pltpu.VMEM((ts, w), F32), pltpu.VMEM((ts + 8, w), F32),
                        pltpu.VMEM((8, w), F32)],
        input_output_aliases={5: 0},
        compiler_params=_cparams(("arbitrary",)), name="lru_bwd")(xl, proj, hs, hs, dy, dproj, wa, wx, ba, bx, lam)


def _mem_kv_fwd(mem, g, wkv):
    m = mem.shape[0]

    def body(mem_ref, g_ref, w_ref, k_ref, v_ref, mn_ref):
        mv = mem_ref[...]
        r = lax.rsqrt(jnp.mean(mv * mv, axis=-1, keepdims=True) + EPS)
        mn = _mx(mv * r * g_ref[...])
        mn_ref[...] = mn
        kv = _dot(mn, w_ref[...])
        k_ref[...] = kv[:, 0:D].astype(k_ref.dtype)
        v_ref[...] = kv[:, D:2 * D].astype(v_ref.dtype)

    sh = jax.ShapeDtypeStruct((m, D), _MXU)
    return pl.pallas_call(body, out_shape=[sh, sh, sh], compiler_params=_cparams(None), name="mem_kv_fwd")(mem, g, wkv)


def _mem_kv_bwd(mem, g, mn, wkv, dk, dv):
    m = mem.shape[0]

    def body(mem_ref, g_ref, mn_ref, w_ref, dk_ref, dv_ref, dw_ref, dg_ref):
        dkv = _mx(jnp.concatenate([dk_ref[...], dv_ref[...]], axis=1))
        dw_ref[...] = _dot_tn(mn_ref[...], dkv).astype(dw_ref.dtype)
        dmn = _dot_nt(dkv, w_ref[...])
        mv = mem_ref[...]
        r = lax.rsqrt(jnp.mean(mv * mv, axis=-1, keepdims=True) + EPS)
        dg_ref[...] = jnp.sum(dmn * mv * r, axis=0, keepdims=True)

    del m
    return pl.pallas_call(
        body, out_shape=[jax.ShapeDtypeStruct((D, 2 * D), _MXU), jax.ShapeDtypeStruct((1, D), F32)],
        compiler_params=_cparams(None), name="mem_kv_bwd")(mem, g, mn, wkv, dk, dv)


def _attn_probs(q_ref, k_ref, hd):
    sl = slice(MEM_HD * hd, MEM_HD * hd + MEM_HD)
    qh = _mx(q_ref[:, sl])
    sc = _dot_nt(qh, k_ref[:, sl]) * (MEM_HD ** -0.5)
    e = jnp.exp(sc - jnp.max(sc, axis=-1, keepdims=True))
    return sl, qh, e / jnp.sum(e, axis=-1, keepdims=True)


def _attn_fwd(proj, k, v):
    s = proj.shape[0]
    m = k.shape[0]
    ts = _tile(s, 512)

    def body(q_ref, k_ref, v_ref, y_ref):
        for hd in range(MEM_HEADS):
            sl, _, p = _attn_probs(q_ref, k_ref, hd)
            y_ref[:, sl] = _dot(_mx(p), v_ref[:, sl]).astype(y_ref.dtype)

    kvs = pl.BlockSpec((m, D), lambda i: (0, 0))
    return pl.pallas_call(
        body, grid=(s // ts,),
        in_specs=[pl.BlockSpec((ts, D), lambda i: (i, Q[0] // D)), kvs, kvs],
        out_specs=pl.BlockSpec((ts, D), lambda i: (i, 0)),
        out_shape=jax.ShapeDtypeStruct((s, D), _MXU),
        compiler_params=_cparams(("parallel",)), name="attn_fwd")(proj, k, v)


def _attn_bwd(proj, k, v, dy, dproj):
    s = proj.shape[0]
    m = k.shape[0]
    ts = _tile(s, 512)

    def body(q_ref, k_ref, v_ref, dy_ref, dproj_hbm, dq_ref, dk_ref, dv_ref):
        del dproj_hbm

        @pl.when(pl.program_id(0) == 0)
        def _():
            dk_ref[...] = jnp.zeros_like(dk_ref)
            dv_ref[...] = jnp.zeros_like(dv_ref)

        for hd in range(MEM_HEADS):
            sl, qh, p = _attn_probs(q_ref, k_ref, hd)
            dyh = _mx(dy_ref[:, sl])
            dp = _dot_nt(dyh, v_ref[:, sl])
            ds = _mx(p * (dp - jnp.sum(dp * p, axis=-1, keepdims=True)) * (MEM_HD ** -0.5))
            dq_ref[:, sl] = _dot(ds, k_ref[:, sl]).astype(dq_ref.dtype)
            dk_ref[:, sl] += _dot_tn(ds, qh)
            dv_ref[:, sl] += _dot_tn(_mx(p), dyh)

    kvs = pl.BlockSpec((m, D), lambda i: (0, 0))
    qs = pl.BlockSpec((ts, D), lambda i: (i, Q[0] // D))
    return pl.pallas_call(
        body, grid=(s // ts,),
        in_specs=[qs, kvs, kvs, pl.BlockSpec((ts, D), lambda i: (i, 0)), pl.BlockSpec(memory_space=pl.ANY)],
        out_specs=[qs, kvs, kvs],
        out_shape=[jax.ShapeDtypeStruct(dproj.shape, dproj.dtype), jax.ShapeDtypeStruct((m, D), F32),
                   jax.ShapeDtypeStruct((m, D), F32)],
        input_output_aliases={4: 0},
        compiler_params=_cparams(("arbitrary",)), name="attn_bwd")(proj, k, v, dy, dproj)


def _merge_fb(x, target, yssd, ylru, ymem, proj, wbs, wbl, wbm, wo, fg):
    s = x.shape[0]
    ts = _tile(s, 256)

    def body(x_ref, t_ref, ys_ref, yl_ref, ym_ref, gl_ref, wbs_ref, wbl_ref, wbm_ref, wo_ref, fg_ref,
             dgl_ref, dx2_ref, dx2m_ref, mg_ref, db0_ref, db1_ref, db2_ref, loss_ref, dfg_ref):
        @pl.when(pl.program_id(0) == 0)
        def _():
            loss_ref[...] = jnp.zeros_like(loss_ref)
            dfg_ref[...] = jnp.zeros_like(dfg_ref)

        bs = (_dot(ys_ref[...], wbs_ref[...]), _dot(yl_ref[...], wbl_ref[...]), _dot(ym_ref[...], wbm_ref[...]))
        gates = [_sigmoid(gl_ref[:, D * n:D * n + D]) for n in range(3)]
        merged = gates[0] * bs[0] + gates[1] * bs[1] + gates[2] * bs[2]
        mg = _mx(merged)
        mg_ref[...] = mg
        x2 = x_ref[...] + _dot(mg, wo_ref[...])
        r = lax.rsqrt(jnp.mean(x2 * x2, axis=-1, keepdims=True) + EPS)
        xhat = x2 * r
        err = xhat * fg_ref[...] - t_ref[...]
        loss_ref[...] += jnp.sum(err * err, axis=0, keepdims=True) * (0.5 / D)
        dy = err * (1.0 / D)
        dfg_ref[...] += jnp.sum(dy * xhat, axis=0, keepdims=True)
        dxh = dy * fg_ref[...]
        dx2 = r * (dxh - xhat * jnp.mean(dxh * xhat, axis=-1, keepdims=True))
        dx2_ref[...] = dx2
        dx2m = _mx(dx2)
        dx2m_ref[...] = dx2m
        dmg = _dot_nt(dx2m, wo_ref[...])
        for n, db_ref in enumerate((db0_ref, db1_ref, db2_ref)):
            gt = gates[n]
            dgl_ref[:, D * n:D * n + D] = (dmg * bs[n] * gt * (1.0 - gt)).astype(dgl_ref.dtype)
            db_ref[...] = (dmg * gt).astype(db_ref.dtype)

    row = lambda w: pl.BlockSpec((ts, w), lambda i: (i, 0))
    full = lambda a: pl.BlockSpec(a.shape, lambda i: (0, 0))
    vec = pl.BlockSpec((1, D), lambda i: (0, 0))
    gls = pl.BlockSpec((ts, GL[1]), lambda i: (i, GL[0] // GL[1]))
    act = jax.ShapeDtypeStruct((s, D), _MXU)
    return pl.pallas_call(
        body, grid=(s // ts,),
        in_specs=[row(D), row(D), row(SSD_W), row(LRU_W), row(D), gls, full(wbs), full(wbl), full(wbm), full(wo), vec],
        out_specs=[gls, row(D), row(D), row(D), row(D), row(D), row(D), vec, vec],
        out_shape=[jax.ShapeDtypeStruct((s, NP), _MXU), jax.ShapeDtypeStruct((s, D), F32), act, act, act, act, act,
                   jax.ShapeDtypeStruct((1, D), F32), jax.ShapeDtypeStruct((1, D), F32)],
        compiler_params=_cparams(("arbitrary",)), name="merge_fwd_bwd")(
            x, target, yssd, ylru, ymem, proj, wbs, wbl, wbm, wo, fg)


def _adamw(w, g, m, v, name):
    rows, cols = w.shape
    tr = rows if rows <= 512 else 256
    assert rows % tr == 0

    def body(w_ref, g_ref, m_ref, v_ref, d_ref, mo_ref, vo_ref):
        gv = g_ref[...]
        mn = ADAM_B1 * m_ref[...] + (1.0 - ADAM_B1) * gv
        vn = ADAM_B2 * v_ref[...] + (1.0 - ADAM_B2) * (gv * gv)
        m_hat = mn / (1.0 - ADAM_B1 ** ADAM_STEP)
        v_hat = vn / (1.0 - ADAM_B2 ** ADAM_STEP)
        d_ref[...] = -ADAM_LR * (m_hat / (jnp.sqrt(v_hat) + ADAM_EPS) + ADAM_WD * w_ref[...])
        mo_ref[...] = mn
        vo_ref[...] = vn

    blk = pl.BlockSpec((tr, cols), lambda i: (i, 0))
    sh = jax.ShapeDtypeStruct((rows, cols), F32)
    return pl.pallas_call(
        body, grid=(rows // tr,), in_specs=[blk] * 4, out_specs=[blk] * 3, out_shape=[sh] * 3,
        compiler_params=_cparams(("parallel",)), name=name)(w, g, m, v)


def _mesh_pos():
    x, y, c = lax.axis_index("x"), lax.axis_index("y"), lax.axis_index("c")
    chips = [(1 - x, y), (x, 1 - y), (1 - x, 1 - y)]
    return x, y, c, 2 * x + y, chips


def _hbm():
    return pl.BlockSpec(memory_space=pl.ANY)


def _remote(src, dst, send_sem, recv_sem, dev):
    return pltpu.make_async_remote_copy(src_ref=src, dst_ref=dst, send_sem=send_sem, recv_sem=recv_sem,
                                        device_id=dev, device_id_type=MESH)


def _sems(n):
    return [pltpu.SemaphoreType.DMA((n,)), pltpu.SemaphoreType.DMA((n,))]


def _gather_shards(arrs, split):
    n = len(arrs)
    n_sem = sum(6 if sp else 3 for sp in split)

    def body(*refs):
        srcs, outs = refs[:n], refs[n:2 * n]
        send_sems, recv_sems, local_sems = refs[2 * n:]
        x, y, c, me, chips = _mesh_pos()
        sib = (x, y, 1 - c)

        def rows(i, which):
            if not split[i]:
                return pl.ds(0, arrs[i].shape[0])
            half = arrs[i].shape[0] // 2
            return pl.ds(which * half, half)

        own = [pltpu.make_async_copy(srcs[i], outs[i].at[me], local_sems.at[i]) for i in range(n)]
        for cp in own:
            cp.start()
        sends, plan, k = [], [], 0
        for i in range(n):
            for j, (cx, cy) in enumerate(chips):
                cp = _remote(srcs[i].at[rows(i, c)], outs[i].at[me, rows(i, c)], send_sems.at[k], recv_sems.at[k],
                             (cx, cy, c))
                cp.start()
                sends.append(cp)
                plan.append((i, j, k))
                k += 1
        passed = []
        for i, j, k0 in plan:
            cx, cy = chips[j]
            slot = outs[i].at[2 * cx + cy, rows(i, c)]
            _remote(slot, slot, send_sems.at[k0], recv_sems.at[k0], (cx, cy, c)).wait_recv()
            if split[i]:
                fwd = _remote(slot, slot, send_sems.at[k], recv_sems.at[k], sib)
                fwd.start()
                passed.append((fwd, i, j, k))
                k += 1
        for _, i, j, kf in passed:
            cx, cy = chips[j]
            slot = outs[i].at[2 * cx + cy, rows(i, 1 - c)]
            _remote(slot, slot, send_sems.at[kf], recv_sems.at[kf], sib).wait_recv()
        for cp in sends + [p[0] for p in passed]:
            cp.wait_send()
        for cp in own:
            cp.wait()

    return pl.pallas_call(
        body, in_specs=[_hbm()] * n, out_specs=[_hbm()] * n,
        out_shape=[jax.ShapeDtypeStruct((NSHARD,) + a.shape, a.dtype) for a in arrs],
        scratch_shapes=_sems(n_sem) + [pltpu.SemaphoreType.DMA((n,))],
        name="gather_shards")(*arrs)


def _swap_halves(arrs):
    n = len(arrs)

    def body(*refs):
        srcs, outs = refs[:n], refs[n:2 * n]
        send_sems, recv_sems = refs[2 * n:]
        x, y, c, _, _ = _mesh_pos()
        sib = (x, y, 1 - c)
        cps = []
        for i in range(n):
            half = arrs[i].shape[1] // 2
            cps.append(_remote(srcs[i].at[:, pl.ds((1 - c) * half, half)], outs[i], send_sems.at[i], recv_sems.at[i],
                               sib))
        for cp in cps:
            cp.start()
        for cp in cps:
            cp.wait()

    return pl.pallas_call(
        body, in_specs=[_hbm()] * n, out_specs=[_hbm()] * n,
        out_shape=[jax.ShapeDtypeStruct((NSHARD, a.shape[1] // 2, a.shape[2]), a.dtype) for a in arrs],
        scratch_shapes=_sems(n), name="swap_halves")(*arrs)


def _scatter_chips(arrs):
    n = len(arrs)

    def body(*refs):
        srcs, outs = refs[:n], refs[n:2 * n]
        send_sems, recv_sems, local_sems = refs[2 * n:]
        _, _, c, me, chips = _mesh_pos()
        own = [pltpu.make_async_copy(srcs[i].at[me], outs[i].at[me], local_sems.at[i]) for i in range(n)]
        for cp in own:
            cp.start()
        cps = []
        for i in range(n):
            for j, (cx, cy) in enumerate(chips):
                cps.append(_remote(srcs[i].at[2 * cx + cy], outs[i].at[me], send_sems.at[3 * i + j],
                                   recv_sems.at[3 * i + j], (cx, cy, c)))
        for cp in cps:
            cp.start()
        for i in range(n):
            for j, (cx, cy) in enumerate(chips):
                slot = outs[i].at[2 * cx + cy]
                _remote(slot, slot, send_sems.at[3 * i + j], recv_sems.at[3 * i + j], (cx, cy, c)).wait_recv()
        for cp in cps:
            cp.wait_send()
        for cp in own:
            cp.wait()

    return pl.pallas_call(
        body, in_specs=[_hbm()] * n, out_specs=[_hbm()] * n,
        out_shape=[jax.ShapeDtypeStruct(a.shape, a.dtype) for a in arrs],
        scratch_shapes=_sems(3 * n) + [pltpu.SemaphoreType.DMA((n,))], name="scatter_chips")(*arrs)


def _share_halves(arrs):
    n = len(arrs)

    def body(*refs):
        srcs, outs = refs[:n], refs[n:2 * n]
        send_sems, recv_sems, local_sems = refs[2 * n:]
        x, y, c, _, _ = _mesh_pos()
        sib = (x, y, 1 - c)
        own = [pltpu.make_async_copy(srcs[i], outs[i].at[c], local_sems.at[i]) for i in range(n)]
        cps = [_remote(srcs[i], outs[i].at[c], send_sems.at[i], recv_sems.at[i], sib) for i in range(n)]
        for cp in own + cps:
            cp.start()
        for i in range(n):
            _remote(srcs[i], outs[i].at[1 - c], send_sems.at[i], recv_sems.at[i], sib).wait_recv()
        for cp in cps:
            cp.wait_send()
        for cp in own:
            cp.wait()

    return pl.pallas_call(
        body, in_specs=[_hbm()] * n, out_specs=[_hbm()] * n,
        out_shape=[jax.ShapeDtypeStruct((2,) + a.shape, a.dtype) for a in arrs],
        scratch_shapes=_sems(n) + [pltpu.SemaphoreType.DMA((n,))], name="share_halves")(*arrs)


def _gather_small(full):
    _, width = full.shape

    def body(src, out, send_sems, recv_sems, local_sem):
        _, _, c, me, chips = _mesh_pos()
        mine = src.at[pl.ds(0, SMALL_ROWS)]
        own = pltpu.make_async_copy(mine, out.at[me], local_sem)
        own.start()
        cps = [_remote(mine, out.at[me], send_sems.at[j], recv_sems.at[j], (cx, cy, c))
               for j, (cx, cy) in enumerate(chips)]
        for cp in cps:
            cp.start()
        for j, (cx, cy) in enumerate(chips):
            slot = out.at[2 * cx + cy]
            _remote(slot, slot, send_sems.at[j], recv_sems.at[j], (cx, cy, c)).wait_recv()
        for cp in cps:
            cp.wait_send()
        own.wait()

    return pl.pallas_call(
        body, in_specs=[_hbm()], out_specs=_hbm(),
        out_shape=jax.ShapeDtypeStruct((NSHARD, SMALL_ROWS, width), full.dtype),
        scratch_shapes=_sems(3) + [pltpu.SemaphoreType.DMA], name="gather_small")(full)


def _add_sibling(mine, recv, c, name):
    _, half, width = recv.shape
    tr = _tile(half, 256, 8)
    nb = half // tr

    def body(c_ref, a_ref, b_ref, o_ref):
        del c_ref
        o_ref[...] = (a_ref[...].astype(F32) + b_ref[...].astype(F32)).astype(o_ref.dtype)

    grid_spec = pltpu.PrefetchScalarGridSpec(
        num_scalar_prefetch=1, grid=(NSHARD, nb),
        in_specs=[pl.BlockSpec((1, tr, width), lambda j, r, c_ref: (j, c_ref[0] * nb + r, 0)),
                  pl.BlockSpec((1, tr, width), lambda j, r, c_ref: (j, r, 0))],
        out_specs=pl.BlockSpec((1, tr, width), lambda j, r, c_ref: (j, r, 0)))
    return pl.pallas_call(
        body, grid_spec=grid_spec, out_shape=jax.ShapeDtypeStruct(recv.shape, recv.dtype),
        compiler_params=_cparams(("parallel", "parallel")), name=name)(c, mine, recv)


def _sum_chips(parts, name):
    _, half, width = parts.shape
    tr = _tile(half, 256, 8)

    def body(p_ref, o_ref):
        p = [p_ref[j].astype(F32) for j in range(NSHARD)]
        o_ref[...] = ((p[0] + p[1]) + p[2]) + p[3]

    return pl.pallas_call(
        body, grid=(half // tr,),
        in_specs=[pl.BlockSpec((NSHARD, tr, width), lambda r: (0, r, 0))],
        out_specs=pl.BlockSpec((tr, width), lambda r: (r, 0)),
        out_shape=jax.ShapeDtypeStruct((half, width), F32),
        compiler_params=_cparams(("parallel",)), name=name)(parts)


def _unpack(flat, names, shapes):
    out, off = {}, 0
    for n in names:
        sz = _size(shapes[n])
        out[n] = flat[off:off + sz].reshape(shapes[n])
        off += sz
    return out


def _reorder_w_in(w):
    return jnp.concatenate([w[:, 2048:5120], w[:, 9248:12320], w[:, 0:2048], w[:, 8224:9248], w[:, 5152:6688],
                            w[:, 6688:8224], w[:, 5120:5152], jnp.zeros((D, NP - 12320), w.dtype)], axis=1)


def _restore_w_in(g):
    return jnp.concatenate([g[:, 6144:8192], g[:, 0:3072], g[:, 12288:12320], g[:, 9216:10752], g[:, 10752:12288],
                            g[:, 8192:9216], g[:, 3072:6144]], axis=1)


def _lru_group_weights(w):
    w4 = w.reshape(4, 4, 96, 96)
    eye = jnp.eye(4, dtype=w.dtype)
    return (w4[:, :, None, :, :] * eye[None, :, :, None, None]).transpose(0, 1, 3, 2, 4).reshape(4, LRU_G, LRU_G)


def _lru_group_blocks(g):
    g5 = g.reshape(4, 4, 96, 4, 96)
    return jnp.stack([g5[:, a, :, a, :] for a in range(4)], axis=1).reshape(16, 96, 96)


def _local_grads(x, mem, target, wts):
    pad128 = lambda a: jnp.pad(a, ((0, 0), (0, 128 - a.shape[1])))
    w_in_r = wts["w_in_r"]
    wbs, wbl, wbm, wo, wkv = wts["w_br_ssd"], wts["w_br_lru"], wts["w_br_mem"], wts["w_out"], wts["w_kv"]
    wa, wx = _mx(_lru_group_weights(wts["lru_w_a"])), _mx(_lru_group_weights(wts["lru_w_x"]))
    ba, bx = wts["lru_b_a"].reshape(1, LRU_W), wts["lru_b_x"].reshape(1, LRU_W)
    dtb, alog = pad128(wts["ssd_dt_bias"]), pad128(wts["ssd_a_log"])
    dexp = jnp.repeat(wts["ssd_d"], 64, axis=1)
    ng = wts["ssd_norm_g"].reshape(1, SSD_W)

    h = _norm_fwd(x, wts["norm_g"])
    proj = _mm(h, w_in_r, F32, "in_proj")
    xbc = _conv_fwd(proj, XBC, wts["ssd_conv_w"], wts["ssd_conv_b"], True, "ssd_conv_fwd")
    yssd, yraw, hprev = _ssd_fwd(xbc, proj, dtb, alog, dexp, ng)
    xl = _conv_fwd(proj, LX, wts["lru_conv_w"], wts["lru_conv_b"], False, "lru_conv_fwd")
    ylru, hs = _lru_fwd(xl, proj, wa, wx, ba, bx, wts["lru_lambda"])
    kk, vv, mn = _mem_kv_fwd(mem, wts["mem_norm_g"], wkv)
    ymem = _attn_fwd(proj, kk, vv)

    dproj, dx2, dx2m, merged, db0, db1, db2, loss_vec, dfg = _merge_fb(
        x, target, yssd, ylru, ymem, proj, wbs, wbl, wbm, wo, wts["final_g"].reshape(1, D))
    grads = {"final_g": dfg.reshape(D)}
    grads["w_out"] = _mm(merged, dx2m, _MXU, "dw_out", ta=True)
    grads["w_br_ssd"] = _mm(yssd, db0, _MXU, "dw_br_ssd", ta=True)
    grads["w_br_lru"] = _mm(ylru, db1, _MXU, "dw_br_lru", ta=True)
    grads["w_br_mem"] = _mm(ymem, db2, _MXU, "dw_br_mem", ta=True)
    dyssd = _mm(db0, wbs, F32, "dy_ssd", tb=True)
    dylru = _mm(db1, wbl, F32, "dy_lru", tb=True)
    dymem = _mm(db2, wbm, F32, "dy_mem", tb=True)

    dproj, dk, dv = _attn_bwd(proj, kk, vv, dymem, dproj)
    grads["w_kv"], grads["mem_norm_g"] = _mem_kv_bwd(mem, wts["mem_norm_g"], mn, wkv, dk, dv)

    dproj, dxl, dwa, dwx, dba, dbx, dlam = _lru_bwd(xl, proj, hs, dylru, dproj, wa, wx, ba, bx, wts["lru_lambda"])
    grads["lru_w_a"] = _lru_group_blocks(dwa)[None]
    grads["lru_w_x"] = _lru_group_blocks(dwx)[None]
    grads["lru_b_a"], grads["lru_b_x"] = dba.reshape(1, 16, 96), dbx.reshape(1, 16, 96)
    grads["lru_lambda"] = dlam
    grads["lru_conv_w"], grads["lru_conv_b"] = _conv_bwd_w(
        proj, LX, wts["lru_conv_w"], wts["lru_conv_b"], dxl, False, "lru_conv_bwd_w")
    dproj = _conv_bwd_x(dxl, wts["lru_conv_w"], dproj, LX, "lru_conv_bwd_x")

    dproj, ddt, dxbc, dng, dda, ddd, ddtb = _ssd_bwd(xbc, proj, yraw, hprev, dyssd, dproj, dtb, alog, dexp, ng)
    dproj = _put_block(ddt, dproj, DT, "put_ddt")
    grads["ssd_norm_g"] = dng.reshape(4, 512)
    grads["ssd_dt_bias"] = ddtb[:, 0:32]
    grads["ssd_a_log"] = (dda * -jnp.exp(alog))[:, 0:32]
    grads["ssd_d"] = ddd.reshape(32, 64).sum(axis=1)[None, :]
    dpre, grads["ssd_conv_w"], grads["ssd_conv_b"] = _conv_bwd_w(
        proj, XBC, wts["ssd_conv_w"], wts["ssd_conv_b"], dxbc, True, "ssd_conv_bwd_w")
    dproj = _conv_bwd_x(dpre, wts["ssd_conv_w"], dproj, XBC, "ssd_conv_bwd_x")

    grads["w_in_r"] = _mm(h, dproj, _MXU, "dw_in", ta=True)
    dh = _mm(dproj, w_in_r, F32, "dh", tb=True, tn=1024, tk=1280)
    grad_x, grads["norm_g"] = _norm_bwd(x, wts["norm_g"], dh, dx2)
    return jnp.sum(loss_vec), grad_x, grads


def kernel(x, mem, norm_g, w_in, ssd_conv_w, ssd_conv_b, ssd_dt_bias, ssd_a_log, ssd_d, ssd_norm_g, lru_conv_w, lru_conv_b, lru_w_a, lru_b_a, lru_w_x, lru_b_x, lru_lambda, mem_norm_g, w_kv, w_br_ssd, w_br_lru, w_br_mem, w_out, final_g, loss_target, m_norm_g, m_w_in, m_ssd_conv_w, m_ssd_conv_b, m_ssd_dt_bias, m_ssd_a_log, m_ssd_d, m_ssd_norm_g, m_lru_conv_w, m_lru_conv_b, m_lru_w_a, m_lru_b_a, m_lru_w_x, m_lru_b_x, m_lru_lambda, m_mem_norm_g, m_w_kv, m_w_br_ssd, m_w_br_lru, m_w_br_mem, m_w_out, m_final_g, v_norm_g, v_w_in, v_ssd_conv_w, v_ssd_conv_b, v_ssd_dt_bias, v_ssd_a_log, v_ssd_d, v_ssd_norm_g, v_lru_conv_w, v_lru_conv_b, v_lru_w_a, v_lru_b_a, v_lru_w_x, v_lru_b_x, v_lru_lambda, v_mem_norm_g, v_w_kv, v_w_br_ssd, v_w_br_lru, v_w_br_mem, v_w_out, v_final_g):
    given = dict(locals())

    rows_w = jnp.concatenate([w_br_ssd[0], w_br_lru[0], w_br_mem[0], w_out[0]], axis=0)
    small_w = jnp.concatenate([ssd_conv_w[0], ssd_norm_g[0], lru_conv_w[0]], axis=1)
    g_in, g_kv, g_rows, g_small = _gather_shards(
        [_mx(w_in[0]), _mx(w_kv[0]), _mx(rows_w), small_w], [True, True, True, False])
    spread = lambda a: a.transpose(1, 0, 2).reshape(a.shape[1], NSHARD * a.shape[2])
    wts = {n: given[n] for n in REPL}
    wts["lru_w_a"], wts["lru_w_x"] = lru_w_a[0], lru_w_x[0]
    wts["w_in_r"] = _reorder_w_in(spread(g_in))
    wts["w_kv"] = spread(g_kv)
    for n, lo_, hi_ in ROW_PIECES:
        wts[n] = g_rows[:, lo_:hi_].reshape(NSHARD * (hi_ - lo_), D)
    wts["ssd_conv_w"] = spread(g_small[:, :, 0:768])
    wts["ssd_norm_g"] = spread(g_small[:, :, 768:896])
    wts["lru_conv_w"] = spread(g_small[:, :, 896:1280])

    loss_part, grad_x, grads = _local_grads(x[0], mem[0], loss_target[0], wts)
    loss = lax.psum(loss_part, ("x", "y", "c"))

    split = lambda a: a.reshape(a.shape[0], NSHARD, a.shape[1] // NSHARD).transpose(1, 0, 2)
    p_in = split(_restore_w_in(grads["w_in_r"]))
    p_kv = split(grads["w_kv"])
    p_rows = jnp.concatenate([grads[n].reshape(NSHARD, hi_ - lo_, D) for n, lo_, hi_ in ROW_PIECES], axis=1)
    repl_flat = jnp.concatenate([grads[n].reshape(-1) for n in REPL])
    repl_flat = jnp.pad(repl_flat, (0, NSHARD * SMALL_Q - repl_flat.shape[0])).reshape(NSHARD, SMALL_Q)
    shard_small = jnp.concatenate([split(grads[n]).reshape(NSHARD, -1) for n in SMALL_SHARDED], axis=1)
    p_small = jnp.concatenate(
        [repl_flat, shard_small, jnp.zeros((NSHARD, SMALL_BUF_ROWS * PACK_W - SMALL_Q - 5120), F32)], axis=1)
    p_small = p_small.reshape(NSHARD, SMALL_BUF_ROWS, PACK_W)

    parts = [p_in, p_kv, p_rows, p_small]
    names = ["w_in", "w_kv", "rows", "small"]
    c_idx = lax.axis_index("c").astype(jnp.int32).reshape(1)
    recv = _swap_halves(parts)
    chip_parts = [_add_sibling(p, r, c_idx, "add_sibling_" + n) for p, r, n in zip(parts, recv, names)]
    landed = _scatter_chips(chip_parts)
    halves = [_sum_chips(a, "sum_chips_" + n) for a, n in zip(landed, names)]
    r_in, r_kv, r_rows, r_small = [a.reshape(2 * a.shape[1], a.shape[2]) for a in _share_halves(halves)]
    repl_all = _gather_small(r_small).reshape(-1)

    g_shard = {"w_in": r_in, "w_kv": r_kv}
    for n, lo_, hi_ in ROW_PIECES:
        g_shard[n] = r_rows[lo_:hi_]
    g_shard.update(_unpack(r_small.reshape(-1)[SMALL_Q:], SMALL_SHARDED, SHARD_SHAPE))
    g_repl = _unpack(repl_all, REPL, REPL_SHAPE)

    out_g, out_d, out_m, out_v = {}, {}, {}, {}
    for n in WEIGHTS:
        w_full = given[n]
        g = (g_shard[n] if n in SHARDED else g_repl[n]).reshape(w_full.shape)
        cols = w_full.shape[-1]
        as2d = lambda a: a.reshape(-1, cols)
        d, mo, vo = _adamw(as2d(w_full), as2d(g), as2d(given["m_" + n]), as2d(given["v_" + n]), "adamw_" + n)
        out_g[n] = g
        out_d[n], out_m[n], out_v[n] = d.reshape(w_full.shape), mo.reshape(w_full.shape), vo.reshape(w_full.shape)

    return (loss, grad_x[None], *[out_g[n] for n in WEIGHTS], *[out_d[n] for n in WEIGHTS],
            *[out_m[n] for n in WEIGHTS], *[out_v[n] for n in WEIGHTS])
```

```python
import jax
import jax.numpy as jnp
from jax import lax
from jax.experimental import pallas as pl
from jax.experimental.pallas import tpu as pltpu

F32 = jnp.float32
_MXU = jnp.bfloat16
_HI = lax.Precision.HIGHEST
MESH = pl.DeviceIdType.MESH

D = 1024
EPS = 1e-6
MEM_HEADS = 4
MEM_HD = 256
LRU_C = 8.0
SSD_L = 128
SSD_W = 2048
LRU_W = 1536
NSHARD = 4

XBC = (0, 3072)
GL = (3072, 3072)
Z = (6144, 2048)
Q = (8192, 1024)
LG = (9216, 1536)
LX = (10752, 1536)
DT = (12288, 512)
NP = 12800

ADAM_LR = 0.001
ADAM_B1 = 0.9
ADAM_B2 = 0.999
ADAM_EPS = 1e-08
ADAM_WD = 0.01
ADAM_STEP = 10

VMEM_LIMIT = 56 * 1024 * 1024

SHARDED = ("w_in", "ssd_conv_w", "ssd_norm_g", "lru_conv_w", "w_kv", "w_br_ssd", "w_br_lru", "w_br_mem", "w_out")
SHARD_SHAPE = {"w_in": (1024, 3080), "ssd_conv_w": (4, 768), "ssd_norm_g": (4, 128), "lru_conv_w": (4, 384),
               "w_kv": (1024, 512), "w_br_ssd": (512, 1024), "w_br_lru": (384, 1024), "w_br_mem": (256, 1024),
               "w_out": (256, 1024)}
REPL = ("norm_g", "ssd_conv_b", "ssd_dt_bias", "ssd_a_log", "ssd_d", "lru_conv_b", "lru_w_a", "lru_b_a",
        "lru_w_x", "lru_b_x", "lru_lambda", "mem_norm_g", "final_g")
REPL_SHAPE = {"norm_g": (1, 1024), "ssd_conv_b": (1, 3072), "ssd_dt_bias": (1, 32), "ssd_a_log": (1, 32),
              "ssd_d": (1, 32), "lru_conv_b": (1, 1536), "lru_w_a": (1, 16, 96, 96), "lru_b_a": (1, 16, 96),
              "lru_w_x": (1, 16, 96, 96), "lru_b_x": (1, 16, 96), "lru_lambda": (1, 1536),
              "mem_norm_g": (1, 1024), "final_g": (1024,)}
WEIGHTS = ("norm_g", "w_in", "ssd_conv_w", "ssd_conv_b", "ssd_dt_bias", "ssd_a_log", "ssd_d", "ssd_norm_g",
           "lru_conv_w", "lru_conv_b", "lru_w_a", "lru_b_a", "lru_w_x", "lru_b_x", "lru_lambda", "mem_norm_g",
           "w_kv", "w_br_ssd", "w_br_lru", "w_br_mem", "w_out", "final_g")

ROW_PIECES = (("w_br_ssd", 0, 512), ("w_br_lru", 512, 896), ("w_br_mem", 896, 1152), ("w_out", 1152, 1408))
SMALL_SHARDED = ("ssd_conv_w", "ssd_norm_g", "lru_conv_w")
PACK_W = 512
SMALL_ROWS = 152
SMALL_Q = SMALL_ROWS * PACK_W
SMALL_BUF_ROWS = 176


def _size(shape):
    n = 1
    for s in shape:
        n *= s
    return n


def _sigmoid(x):
    return 0.5 * jnp.tanh(0.5 * x) + 0.5


def _silu(x):
    return x * _sigmoid(x)


def _dsilu(x):
    s = _sigmoid(x)
    return s * (1.0 + x * (1.0 - s))


def _softplus(x):
    return jnp.maximum(x, 0.0) + jnp.log(1.0 + jnp.exp(-jnp.abs(x)))


def _one_minus_sq(log_a, a):
    x = 2.0 * log_a
    series = -x * (1.0 + x * (0.5 + x * (1.0 / 6.0 + x * (1.0 / 24.0))))
    return jnp.where(x > -0.03, series, 1.0 - a * a)


def _dot(a, b, precision=None):
    return jnp.dot(a, b, preferred_element_type=F32, precision=precision)


def _dot_nt(a, b):
    return lax.dot_general(a, b, (((1,), (1,)), ((), ())), preferred_element_type=F32)


def _dot_tn(a, b):
    return lax.dot_general(a, b, (((0,), (0,)), ((), ())), preferred_element_type=F32)


def _mx(a):
    return a.astype(_MXU)


def _cparams(sem):
    return pltpu.CompilerParams(dimension_semantics=sem, vmem_limit_bytes=VMEM_LIMIT)


def _tile(n, want, mult=128):
    if n <= want:
        return n
    for t in range(want - want % mult, 0, -mult):
        if n % t == 0:
            return t
    raise ValueError((n, want, mult))


def _mm(a, b, out_dtype, name, ta=False, tb=False, tm=1024, tn=1280, tk=1024):
    k, m = a.shape if ta else a.shape[::-1]
    k2, n = b.shape[::-1] if tb else b.shape
    assert k == k2
    tm, tn, tk = _tile(m, tm), _tile(n, tn), _tile(k, tk)
    nk = k // tk
    contract = (((0 if ta else 1,), (1 if tb else 0,)), ((), ()))

    def body(a_ref, b_ref, o_ref, acc_ref):
        kk = pl.program_id(2)

        @pl.when(kk == 0)
        def _():
            acc_ref[...] = jnp.zeros_like(acc_ref)

        acc_ref[...] += lax.dot_general(a_ref[...], b_ref[...], contract, preferred_element_type=F32)

        @pl.when(kk == nk - 1)
        def _():
            o_ref[...] = acc_ref[...].astype(o_ref.dtype)

    a_spec = pl.BlockSpec((tk, tm), lambda i, j, kk: (kk, i)) if ta else pl.BlockSpec((tm, tk), lambda i, j, kk: (i, kk))
    b_spec = pl.BlockSpec((tn, tk), lambda i, j, kk: (j, kk)) if tb else pl.BlockSpec((tk, tn), lambda i, j, kk: (kk, j))
    return pl.pallas_call(
        body, grid=(m // tm, n // tn, nk),
        in_specs=[a_spec, b_spec],
        out_specs=pl.BlockSpec((tm, tn), lambda i, j, kk: (i, j)),
        out_shape=jax.ShapeDtypeStruct((m, n), out_dtype),
        scratch_shapes=[pltpu.VMEM((tm, tn), F32)],
        compiler_params=_cparams(("parallel", "parallel", "arbitrary")), name=name)(a, b)


def _norm_fwd(x, g):
    s = x.shape[0]
    ts = _tile(s, 512)

    def body(x_ref, g_ref, h_ref):
        xv = x_ref[...]
        r = lax.rsqrt(jnp.mean(xv * xv, axis=-1, keepdims=True) + EPS)
        h_ref[...] = (xv * r * g_ref[...]).astype(h_ref.dtype)

    return pl.pallas_call(
        body, grid=(s // ts,),
        in_specs=[pl.BlockSpec((ts, D), lambda i: (i, 0)), pl.BlockSpec((1, D), lambda i: (0, 0))],
        out_specs=pl.BlockSpec((ts, D), lambda i: (i, 0)),
        out_shape=jax.ShapeDtypeStruct((s, D), _MXU),
        compiler_params=_cparams(("parallel",)), name="norm_fwd")(x, g)


def _norm_bwd(x, g, dh, dx2):
    s = x.shape[0]
    ts = _tile(s, 512)

    def body(x_ref, g_ref, dh_ref, dx2_ref, gx_ref, dg_ref):
        @pl.when(pl.program_id(0) == 0)
        def _():
            dg_ref[...] = jnp.zeros_like(dg_ref)

        xv = x_ref[...]
        r = lax.rsqrt(jnp.mean(xv * xv, axis=-1, keepdims=True) + EPS)
        xhat = xv * r
        dh_v = dh_ref[...]
        dg_ref[...] += jnp.sum(dh_v * xhat, axis=0, keepdims=True)
        dxh = dh_v * g_ref[...]
        gx_ref[...] = dx2_ref[...] + r * (dxh - xhat * jnp.mean(dxh * xhat, axis=-1, keepdims=True))

    row = pl.BlockSpec((ts, D), lambda i: (i, 0))
    vec = pl.BlockSpec((1, D), lambda i: (0, 0))
    return pl.pallas_call(
        body, grid=(s // ts,), in_specs=[row, vec, row, row], out_specs=[row, vec],
        out_shape=[jax.ShapeDtypeStruct((s, D), F32), jax.ShapeDtypeStruct((1, D), F32)],
        compiler_params=_cparams(("arbitrary",)), name="norm_bwd")(x, g, dh, dx2)


CONV_RB = 16
CONV_LC = 256


def _fold8(v):
    acc = v[0:8]
    for r0 in range(8, v.shape[0], 8):
        acc = acc + v[r0:r0 + 8]
    return acc


def _conv_fwd(src, blk, w, b, act, name):
    s = src.shape[0]
    off, width = blk
    cb = off // width
    ts = _tile(s, 256)

    def body(x_ref, w_ref, b_ref, o_ref, ext_ref):
        @pl.when(pl.program_id(0) == 0)
        def _():
            ext_ref[0:8, :] = jnp.zeros((8, width), F32)

        ext_ref[8:8 + ts, :] = x_ref[...]
        for l0 in range(0, width, CONV_LC):
            ls = slice(l0, l0 + CONV_LC)
            taps = [w_ref[k:k + 1, ls] for k in range(4)]
            bias = b_ref[:, ls]
            for r0 in range(0, ts, CONV_RB):
                pre = bias
                for k in range(4):
                    pre = pre + taps[k] * ext_ref[5 + k + r0:5 + k + r0 + CONV_RB, ls]
                o_ref[r0:r0 + CONV_RB, ls] = _silu(pre) if act else pre
        ext_ref[0:8, :] = x_ref[ts - 8:ts, :]

    return pl.pallas_call(
        body, grid=(s // ts,),
        in_specs=[pl.BlockSpec((ts, width), lambda i: (i, cb)), pl.BlockSpec((4, width), lambda i: (0, 0)),
                  pl.BlockSpec((1, width), lambda i: (0, 0))],
        out_specs=pl.BlockSpec((ts, width), lambda i: (i, 0)),
        out_shape=jax.ShapeDtypeStruct((s, width), F32),
        scratch_shapes=[pltpu.VMEM((ts + 8, width), F32)],
        compiler_params=_cparams(("arbitrary",)), name=name)(src, w, b)


def _conv_bwd_w(src, blk, w, b, dout, act, name):
    s = src.shape[0]
    off, width = blk
    cb = off // width
    ts = _tile(s, 256)

    def body(x_ref, w_ref, b_ref, do_ref, *rest):
        if act:
            dpre_ref, dw_ref, db_ref, ext_ref = rest
        else:
            dw_ref, db_ref, ext_ref = rest

        @pl.when(pl.program_id(0) == 0)
        def _():
            ext_ref[0:8, :] = jnp.zeros((8, width), F32)
            dw_ref[...] = jnp.zeros_like(dw_ref)
            db_ref[...] = jnp.zeros_like(db_ref)

        ext_ref[8:8 + ts, :] = x_ref[...]
        for l0 in range(0, width, CONV_LC):
            ls = slice(l0, l0 + CONV_LC)
            taps = [w_ref[k:k + 1, ls] for k in range(4)]
            bias = b_ref[:, ls]
            acc_b = jnp.zeros((8, CONV_LC), F32)
            acc_w = [jnp.zeros((8, CONV_LC), F32) for _ in range(4)]
            for r0 in range(0, ts, CONV_RB):
                xs = [ext_ref[5 + k + r0:5 + k + r0 + CONV_RB, ls] for k in range(4)]
                dpre = do_ref[r0:r0 + CONV_RB, ls]
                if act:
                    pre = bias
                    for k in range(4):
                        pre = pre + taps[k] * xs[k]
                    dpre = dpre * _dsilu(pre)
                    dpre_ref[r0:r0 + CONV_RB, ls] = dpre
                acc_b = acc_b + _fold8(dpre)
                for k in range(4):
                    acc_w[k] = acc_w[k] + _fold8(dpre * xs[k])
            db_ref[:, ls] += jnp.sum(acc_b, axis=0, keepdims=True)
            for k in range(4):
                dw_ref[k:k + 1, ls] += jnp.sum(acc_w[k], axis=0, keepdims=True)
        ext_ref[0:8, :] = x_ref[ts - 8:ts, :]

    row = pl.BlockSpec((ts, width), lambda i: (i, 0))
    outs = [pl.BlockSpec((4, width), lambda i: (0, 0)), pl.BlockSpec((1, width), lambda i: (0, 0))]
    shapes = [jax.ShapeDtypeStruct((4, width), F32), jax.ShapeDtypeStruct((1, width), F32)]
    if act:
        outs = [row] + outs
        shapes = [jax.ShapeDtypeStruct((s, width), F32)] + shapes
    return pl.pallas_call(
        body, grid=(s // ts,),
        in_specs=[pl.BlockSpec((ts, width), lambda i: (i, cb)), pl.BlockSpec((4, width), lambda i: (0, 0)),
                  pl.BlockSpec((1, width), lambda i: (0, 0)), row],
        out_specs=outs, out_shape=shapes,
        scratch_shapes=[pltpu.VMEM((ts + 8, width), F32)],
        compiler_params=_cparams(("arbitrary",)), name=name)(src, w, b, dout)


def _conv_bwd_x(dpre, w, dproj, blk, name):
    s = dpre.shape[0]
    off, width = blk
    cb = off // width
    ts = _tile(s, 256)
    nt = s // ts

    def body(dp_ref, w_ref, dproj_hbm, o_ref, ext_ref):
        del dproj_hbm

        @pl.when(pl.program_id(0) == 0)
        def _():
            ext_ref[ts:ts + 8, :] = jnp.zeros((8, width), F32)

        ext_ref[0:ts, :] = dp_ref[...]
        for l0 in range(0, width, CONV_LC):
            ls = slice(l0, l0 + CONV_LC)
            taps = [w_ref[k:k + 1, ls] for k in range(4)]
            for r0 in range(0, ts, CONV_RB):
                acc = taps[0] * ext_ref[3 + r0:3 + r0 + CONV_RB, ls]
                for k in range(1, 4):
                    acc = acc + taps[k] * ext_ref[3 - k + r0:3 - k + r0 + CONV_RB, ls]
                o_ref[r0:r0 + CONV_RB, ls] = acc.astype(o_ref.dtype)
        ext_ref[ts:ts + 8, :] = dp_ref[0:8, :]

    return pl.pallas_call(
        body, grid=(nt,),
        in_specs=[pl.BlockSpec((ts, width), lambda i: (nt - 1 - i, 0)), pl.BlockSpec((4, width), lambda i: (0, 0)),
                  pl.BlockSpec(memory_space=pl.ANY)],
        out_specs=pl.BlockSpec((ts, width), lambda i: (nt - 1 - i, cb)),
        out_shape=jax.ShapeDtypeStruct(dproj.shape, dproj.dtype),
        scratch_shapes=[pltpu.VMEM((ts + 8, width), F32)],
        input_output_aliases={2: 0},
        compiler_params=_cparams(("arbitrary",)), name=name)(dpre, w, dproj)


def _head_col(v, h, lane_l):
    return jnp.sum(jnp.where(lane_l == h, v, 0.0), axis=1, keepdims=True)


def _ssd_decay(a_cs, acst_ref, h, causal, lane_l):
    row = acst_ref[h:h + 1, :]
    return jnp.where(causal, jnp.exp(jnp.minimum(_head_col(a_cs, h, lane_l) - row, 0.0)), 0.0)


def _ssd_common(dt_ref, dtb_ref, alog_ref, acst_ref, dtx_ref, acx_ref):
    ll = SSD_L
    dt = _softplus(dt_ref[:, 0:128] + dtb_ref[...])
    a_neg = -jnp.exp(alog_ref[...])
    ri = lax.broadcasted_iota(jnp.int32, (ll, ll), 0)
    ci = lax.broadcasted_iota(jnp.int32, (ll, ll), 1)
    causal = ri >= ci
    a_cs = _dot(causal.astype(F32), dt * a_neg, _HI)
    acst_ref[...] = a_cs.T
    lane_l = lax.broadcasted_iota(jnp.int32, (ll, 128), 1)
    lo = lane_l < 64
    for j in range(16):
        sl = slice(128 * j, 128 * j + 128)
        dtx_ref[:, sl] = jnp.where(lo, _head_col(dt, 2 * j, lane_l), _head_col(dt, 2 * j + 1, lane_l))
        acx_ref[:, sl] = jnp.where(lo, _head_col(a_cs, 2 * j, lane_l), _head_col(a_cs, 2 * j + 1, lane_l))
    return dt, a_neg, a_cs, causal, ri, lane_l, lo


def _ssd_fwd(xbc, proj, dtb, alog, dexp, ng):
    s = xbc.shape[0]
    ll = SSD_L
    nc = s // ll

    def body(xbc_ref, dt_ref, z_ref, dtb_ref, alog_ref, dexp_ref, ng_ref,
             yssd_ref, yraw_ref, hprev_ref, ht_ref, acst_ref, dtx_ref, acx_ref):
        @pl.when(pl.program_id(0) == 0)
        def _():
            ht_ref[...] = jnp.zeros_like(ht_ref)

        hprev_ref[0] = ht_ref[...]
        _, _, a_cs, causal, _, lane_l, lo = _ssd_common(dt_ref, dtb_ref, alog_ref, acst_ref, dtx_ref, acx_ref)
        for g in range(4):
            bg = _mx(xbc_ref[:, 2048 + 128 * g:2176 + 128 * g])
            cg = _mx(xbc_ref[:, 2560 + 128 * g:2688 + 128 * g])
            cbm = _dot_nt(cg, bg)
            for jj in range(4):
                j = 4 * g + jj
                sl = slice(128 * j, 128 * j + 128)
                xp = xbc_ref[:, sl]
                acx = acx_ref[:, sl]
                a_last = acx_ref[ll - 1:ll, sl]
                xdt = xp * dtx_ref[:, sl]
                acc = None
                for hh in range(2):
                    dec = _ssd_decay(a_cs, acst_ref, 2 * j + hh, causal, lane_l)
                    xm = jnp.where(lo if hh == 0 else jnp.logical_not(lo), xdt, 0.0)
                    t = _dot(_mx(dec * cbm), _mx(xm))
                    acc = t if acc is None else acc + t
                ht = ht_ref[j]
                y = acc + _dot(cg, _mx(ht)) * jnp.exp(acx) + xp * dexp_ref[:, sl]
                yraw_ref[:, sl] = y
                st = _dot_tn(bg, _mx(xdt * jnp.exp(a_last - acx)))
                ht_ref[j] = ht * jnp.exp(a_last) + st
        for g in range(4):
            sl = slice(512 * g, 512 * g + 512)
            yg = yraw_ref[:, sl] * _silu(z_ref[:, sl])
            r = lax.rsqrt(jnp.mean(yg * yg, axis=-1, keepdims=True) + EPS)
            yssd_ref[:, sl] = (yg * r * ng_ref[:, sl]).astype(yssd_ref.dtype)

    vec = lambda w: pl.BlockSpec((1, w), lambda c: (0, 0))
    return pl.pallas_call(
        body, grid=(nc,),
        in_specs=[pl.BlockSpec((ll, 3072), lambda c: (c, 0)),
                  pl.BlockSpec((ll, DT[1]), lambda c: (c, DT[0] // DT[1])),
                  pl.BlockSpec((ll, Z[1]), lambda c: (c, Z[0] // Z[1])),
                  vec(128), vec(128), vec(2048), vec(2048)],
        out_specs=[pl.BlockSpec((ll, 2048), lambda c: (c, 0)), pl.BlockSpec((ll, 2048), lambda c: (c, 0)),
                   pl.BlockSpec((1, 16, 128, 128), lambda c: (c, 0, 0, 0))],
        out_shape=[jax.ShapeDtypeStruct((s, 2048), _MXU), jax.ShapeDtypeStruct((s, 2048), F32),
                   jax.ShapeDtypeStruct((nc, 16, 128, 128), F32)],
        scratch_shapes=[pltpu.VMEM((16, 128, 128), F32), pltpu.VMEM((128, ll), F32),
                        pltpu.VMEM((ll, 2048), F32), pltpu.VMEM((ll, 2048), F32)],
        compiler_params=_cparams(("arbitrary",)), name="ssd_fwd")(xbc, proj, proj, dtb, alog, dexp, ng)


def _ssd_bwd(xbc, proj, yraw, hprev, dyssd, dproj, dtb, alog, dexp, ng):
    s = xbc.shape[0]
    ll = SSD_L
    nc = s // ll

    def body(xbc_ref, dt_ref, z_ref, yraw_ref, hprev_ref, dy_ref, dproj_hbm, dtb_ref, alog_ref, dexp_ref, ng_ref,
             dz_ref, ddt_ref, dxbc_ref, dng_ref, dda_ref, ddd_ref, ddtb_ref,
             dht_ref, acst_ref, dtx_ref, acx_ref, dyr_ref, rowt_ref):
        del dproj_hbm

        @pl.when(pl.program_id(0) == 0)
        def _():
            dht_ref[...] = jnp.zeros_like(dht_ref)
            dng_ref[...] = jnp.zeros_like(dng_ref)
            dda_ref[...] = jnp.zeros_like(dda_ref)
            ddd_ref[...] = jnp.zeros_like(ddd_ref)
            ddtb_ref[...] = jnp.zeros_like(ddtb_ref)
            rowt_ref[...] = jnp.zeros_like(rowt_ref)

        for g in range(4):
            sl = slice(512 * g, 512 * g + 512)
            zz = z_ref[:, sl]
            yr = yraw_ref[:, sl]
            sz = _silu(zz)
            yg = yr * sz
            r = lax.rsqrt(jnp.mean(yg * yg, axis=-1, keepdims=True) + EPS)
            yhat = yg * r
            dyv = dy_ref[:, sl]
            dng_ref[:, sl] += jnp.sum(dyv * yhat, axis=0, keepdims=True)
            dyh = dyv * ng_ref[:, sl]
            dyg = r * (dyh - yhat * jnp.mean(dyh * yhat, axis=-1, keepdims=True))
            dz_ref[:, sl] = (dyg * yr * _dsilu(zz)).astype(dz_ref.dtype)
            dyr_ref[:, sl] = dyg * sz

        dt, a_neg, a_cs, causal, ri, lane_l, lo = _ssd_common(dt_ref, dtb_ref, alog_ref, acst_ref, dtx_ref, acx_ref)
        lane_1 = lax.broadcasted_iota(jnp.int32, (1, 128), 1)
        da_col = jnp.zeros((ll, 128), F32)
        ddt_x = jnp.zeros((ll, 128), F32)
        last = jnp.zeros((1, 128), F32)
        for g in range(4):
            bg = _mx(xbc_ref[:, 2048 + 128 * g:2176 + 128 * g])
            cg = _mx(xbc_ref[:, 2560 + 128 * g:2688 + 128 * g])
            cbm = _dot_nt(cg, bg)
            dcb = jnp.zeros((ll, ll), F32)
            db_g = jnp.zeros((ll, 128), F32)
            dc_g = jnp.zeros((ll, 128), F32)
            for jj in range(4):
                j = 4 * g + jj
                sl = slice(128 * j, 128 * j + 128)
                xp = xbc_ref[:, sl]
                dtx = dtx_ref[:, sl]
                acx = acx_ref[:, sl]
                a_last = acx_ref[ll - 1:ll, sl]
                ea = jnp.exp(acx)
                dte = jnp.exp(a_last - acx)
                cd = jnp.exp(a_last)
                xdt = xp * dtx
                xdt_m = _mx(xdt)
                dy = dyr_ref[:, sl]
                ht = hprev_ref[0, j]
                dhn = dht_ref[j]
                dhn_m = _mx(dhn)
                gmat = _dot(bg, dhn_m)
                dxdt = gmat * dte
                for hh in range(2):
                    h = 2 * j + hh
                    dec = _ssd_decay(a_cs, acst_ref, h, causal, lane_l)
                    mm = dec * cbm
                    dym = _mx(jnp.where(lo if hh == 0 else jnp.logical_not(lo), dy, 0.0))
                    dxdt = dxdt + _dot_tn(_mx(mm), dym)
                    dm = _dot_nt(dym, xdt_m)
                    dcb = dcb + dm * dec
                    qq = dm * mm
                    da_col = da_col + jnp.where(lane_l == h, jnp.sum(qq, axis=1, keepdims=True), 0.0)
                    rowt_ref[h:h + 1, :] = jnp.sum(qq, axis=0, keepdims=True)
                ch = _dot(cg, _mx(ht))
                dyea = dy * ea
                dyea_m = _mx(dyea)
                xw_m = _mx(xdt * dte)
                dc_g = dc_g + _dot_nt(dyea_m, _mx(ht))
                db_g = db_g + _dot_nt(xw_m, dhn_m)
                wl = xdt * gmat * dte
                lane_a = dyea * ch - wl
                lane_b = dxdt * xp
                lane_c = jnp.sum(dhn * ht, axis=0, keepdims=True) * cd + jnp.sum(wl, axis=0, keepdims=True)
                for hh in range(2):
                    h = 2 * j + hh
                    mine = lo if hh == 0 else jnp.logical_not(lo)
                    da_col = da_col + jnp.where(
                        lane_l == h, jnp.sum(jnp.where(mine, lane_a, 0.0), axis=1, keepdims=True), 0.0)
                    ddt_x = ddt_x + jnp.where(
                        lane_l == h, jnp.sum(jnp.where(mine, lane_b, 0.0), axis=1, keepdims=True), 0.0)
                    mine_1 = (lane_1 < 64) if hh == 0 else (lane_1 >= 64)
                    last = last + jnp.where(
                        lane_1 == h, jnp.sum(jnp.where(mine_1, lane_c, 0.0), axis=1, keepdims=True), 0.0)
                dht_ref[j] = dhn * cd + _dot_tn(cg, dyea_m)
                dxbc_ref[:, sl] = dxdt * dtx + dy * dexp_ref[:, sl]
                ddd_ref[:, sl] += jnp.sum(dy * xp, axis=0, keepdims=True)
            dcb_m = _mx(dcb)
            dxbc_ref[:, 2048 + 128 * g:2176 + 128 * g] = db_g + _dot_tn(dcb_m, cg)
            dxbc_ref[:, 2560 + 128 * g:2688 + 128 * g] = dc_g + _dot(dcb_m, bg)

        da_cs = da_col - rowt_ref[...].T
        da_cs = da_cs + jnp.where(lax.broadcasted_iota(jnp.int32, (ll, 128), 0) == ll - 1, last, 0.0)
        d_dta = _dot((ri <= lax.broadcasted_iota(jnp.int32, (ll, ll), 1)).astype(F32), da_cs, _HI)
        ddt = d_dta * a_neg + ddt_x
        dda_ref[...] += jnp.sum(d_dta * dt, axis=0, keepdims=True)
        ddt_raw = ddt * _sigmoid(dt_ref[:, 0:128] + dtb_ref[...])
        ddtb_ref[...] += jnp.sum(ddt_raw, axis=0, keepdims=True)
        ddt_ref[:, 0:128] = ddt_raw.astype(ddt_ref.dtype)
        ddt_ref[:, 128:512] = jnp.zeros((ll, 384), ddt_ref.dtype)

    rev = lambda c: nc - 1 - c
    vec = lambda w: pl.BlockSpec((1, w), lambda c: (0, 0))
    row = lambda w: pl.BlockSpec((ll, w), lambda c: (rev(c), 0))
    outs = pl.pallas_call(
        body, grid=(nc,),
        in_specs=[row(3072),
                  pl.BlockSpec((ll, DT[1]), lambda c: (rev(c), DT[0] // DT[1])),
                  pl.BlockSpec((ll, Z[1]), lambda c: (rev(c), Z[0] // Z[1])),
                  row(2048),
                  pl.BlockSpec((1, 16, 128, 128), lambda c: (rev(c), 0, 0, 0)),
                  row(2048),
                  pl.BlockSpec(memory_space=pl.ANY),
                  vec(128), vec(128), vec(2048), vec(2048)],
        out_specs=[pl.BlockSpec((ll, Z[1]), lambda c: (rev(c), Z[0] // Z[1])),
                   row(DT[1]),
                   row(3072), vec(2048), vec(128), vec(2048), vec(128)],
        out_shape=[jax.ShapeDtypeStruct(dproj.shape, dproj.dtype), jax.ShapeDtypeStruct((s, DT[1]), dproj.dtype),
                   jax.ShapeDtypeStruct((s, 3072), F32), jax.ShapeDtypeStruct((1, 2048), F32),
                   jax.ShapeDtypeStruct((1, 128), F32), jax.ShapeDtypeStruct((1, 2048), F32),
                   jax.ShapeDtypeStruct((1, 128), F32)],
        scratch_shapes=[pltpu.VMEM((16, 128, 128), F32), pltpu.VMEM((128, ll), F32),
                        pltpu.VMEM((ll, 2048), F32), pltpu.VMEM((ll, 2048), F32), pltpu.VMEM((ll, 2048), F32),
                        pltpu.VMEM((128, ll), F32)],
        input_output_aliases={6: 0},
        compiler_params=_cparams(("arbitrary",)), name="ssd_bwd")(
            xbc, proj, proj, yraw, hprev, dyssd, dproj, dtb, alog, dexp, ng)
    return outs


def _put_block(src, dproj, blk, name):
    s = src.shape[0]
    off, width = blk
    cb = off // width
    ts = _tile(s, 1024)

    def body(s_ref, dproj_hbm, o_ref):
        del dproj_hbm
        o_ref[...] = s_ref[...]

    return pl.pallas_call(
        body, grid=(s // ts,),
        in_specs=[pl.BlockSpec((ts, width), lambda i: (i, 0)), pl.BlockSpec(memory_space=pl.ANY)],
        out_specs=pl.BlockSpec((ts, width), lambda i: (i, cb)),
        out_shape=jax.ShapeDtypeStruct(dproj.shape, dproj.dtype),
        input_output_aliases={1: 0},
        compiler_params=_cparams(("parallel",)), name=name)(src, dproj)


LRU_G = 384


def _lru_gates(xl_ref, wa_ref, wx_ref, ba_ref, bx_ref, lam_ref, g):
    sl = slice(LRU_G * g, LRU_G * g + LRU_G)
    xg = xl_ref[:, sl]
    xm = _mx(xg)
    r = _sigmoid(_dot(xm, wa_ref[g]) + ba_ref[:, sl])
    ig = _sigmoid(_dot(xm, wx_ref[g]) + bx_ref[:, sl])
    sp = _softplus(-lam_ref[:, sl])
    log_a = (-LRU_C * r) * sp
    a = jnp.exp(log_a)
    mult = jnp.sqrt(_one_minus_sq(log_a, a))
    return sl, xg, r, ig, sp, a, mult


def _lru_fwd(xl, proj, wa, wx, ba, bx, lam):
    s = xl.shape[0]
    ts = _tile(s, 256)
    w = LRU_W

    def body(xl_ref, lg_ref, wa_ref, wx_ref, ba_ref, bx_ref, lam_ref, y_ref, hs_ref, a_ref, u_ref, carry_ref):
        @pl.when(pl.program_id(0) == 0)
        def _():
            carry_ref[...] = jnp.zeros_like(carry_ref)

        for g in range(4):
            sl, xg, _, ig, _, a, mult = _lru_gates(xl_ref, wa_ref, wx_ref, ba_ref, bx_ref, lam_ref, g)
            a_ref[:, sl] = a
            u_ref[:, sl] = mult * (ig * xg)

        def step(t, h):
            h = a_ref[pl.ds(t, 1), :] * h + u_ref[pl.ds(t, 1), :]
            hs_ref[pl.ds(t, 1), :] = h
            return h

        carry_ref[0:1, :] = lax.fori_loop(0, ts, step, carry_ref[0:1, :], unroll=8)
        y_ref[...] = (hs_ref[...] * _silu(lg_ref[...])).astype(y_ref.dtype)

    row = pl.BlockSpec((ts, w), lambda i: (i, 0))
    vec = pl.BlockSpec((1, w), lambda i: (0, 0))
    wsp = pl.BlockSpec((4, LRU_G, LRU_G), lambda i: (0, 0, 0))
    return pl.pallas_call(
        body, grid=(s // ts,),
        in_specs=[row, pl.BlockSpec((ts, w), lambda i: (i, LG[0] // w)), wsp, wsp, vec, vec, vec],
        out_specs=[row, row],
        out_shape=[jax.ShapeDtypeStruct((s, w), _MXU), jax.ShapeDtypeStruct((s, w), F32)],
        scratch_shapes=[pltpu.VMEM((ts, w), F32), pltpu.VMEM((ts, w), F32), pltpu.VMEM((8, w), F32)],
        compiler_params=_cparams(("arbitrary",)), name="lru_fwd")(xl, proj, wa, wx, ba, bx, lam)


def _lru_bwd(xl, proj, hs, dy, dproj, wa, wx, ba, bx, lam):
    s = xl.shape[0]
    ts = _tile(s, 256)
    nt = s // ts
    w = LRU_W
    hb = ts // 8

    def body(xl_ref, lg_ref, hs_ref, hprev_ref, dy_ref, dproj_hbm, wa_ref, wx_ref, ba_ref, bx_ref, lam_ref,
             dlg_ref, dxl_ref, dwa_ref, dwx_ref, dba_ref, dbx_ref, dlam_ref,
             a_ref, dh_ref, ext_ref, carry_ref, r_ref, ig_ref, mult_ref):
        del dproj_hbm
        i = pl.program_id(0)

        @pl.when(i == 0)
        def _():
            carry_ref[...] = jnp.zeros_like(carry_ref)
            for ref in (dwa_ref, dwx_ref, dba_ref, dbx_ref, dlam_ref):
                ref[...] = jnp.zeros_like(ref)

        lg = lg_ref[...]
        dyv = dy_ref[...]
        dh_ref[...] = dyv * _silu(lg)
        dlg_ref[...] = (dyv * hs_ref[...] * _dsilu(lg)).astype(dlg_ref.dtype)
        for g in range(4):
            sl, _, r, ig, _, a, mult = _lru_gates(xl_ref, wa_ref, wx_ref, ba_ref, bx_ref, lam_ref, g)
            a_ref[:, sl] = a
            r_ref[:, sl] = r
            ig_ref[:, sl] = ig
            mult_ref[:, sl] = mult

        def step(k, carry):
            t = ts - 1 - k
            dh = dh_ref[pl.ds(t, 1), :] + carry
            dh_ref[pl.ds(t, 1), :] = dh
            return a_ref[pl.ds(t, 1), :] * dh

        carry_ref[0:1, :] = lax.fori_loop(0, ts, step, carry_ref[0:1, :], unroll=8)

        ext_ref[0:8, :] = jnp.where(i == nt - 1, 0.0, 1.0) * hprev_ref[...]
        ext_ref[8:8 + ts, :] = hs_ref[...]
        for g in range(4):
            sl = slice(LRU_G * g, LRU_G * g + LRU_G)
            xg, r, ig, a, mult = xl_ref[:, sl], r_ref[:, sl], ig_ref[:, sl], a_ref[:, sl], mult_ref[:, sl]
            sp = _softplus(-lam_ref[:, sl])
            dh = dh_ref[:, sl]
            da = dh * ext_ref[7:7 + ts, sl]
            dmult = dh * ig * xg
            di = dh * mult * xg
            dxl = dh * mult * ig
            dlog_a = da * a - dmult * (a * a) / mult
            dlam_ref[:, sl] += jnp.sum(dlog_a * r, axis=0, keepdims=True) * (LRU_C * _sigmoid(-lam_ref[:, sl]))
            dpa = dlog_a * (-LRU_C * sp) * r * (1.0 - r)
            dpx = di * ig * (1.0 - ig)
            dba_ref[:, sl] += jnp.sum(dpa, axis=0, keepdims=True)
            dbx_ref[:, sl] += jnp.sum(dpx, axis=0, keepdims=True)
            dpa_m, dpx_m, xm = _mx(dpa), _mx(dpx), _mx(xg)
            dxl_ref[:, sl] = dxl + _dot_nt(dpa_m, wa_ref[g]) + _dot_nt(dpx_m, wx_ref[g])
            dwa_ref[g] += _dot_tn(xm, dpa_m)
            dwx_ref[g] += _dot_tn(xm, dpx_m)

    rev = lambda i: nt - 1 - i
    row = pl.BlockSpec((ts, w), lambda i: (rev(i), 0))
    vec = pl.BlockSpec((1, w), lambda i: (0, 0))
    wsp = pl.BlockSpec((4, LRU_G, LRU_G), lambda i: (0, 0, 0))
    lgs = pl.BlockSpec((ts, w), lambda i: (rev(i), LG[0] // w))
    return pl.pallas_call(
        body, grid=(nt,),
        in_specs=[row, lgs, row, pl.BlockSpec((8, w), lambda i: (jnp.maximum(rev(i) * hb - 1, 0), 0)), row,
                  pl.BlockSpec(memory_space=pl.ANY), wsp, wsp, vec, vec, vec],
        out_specs=[lgs, row, wsp, wsp, vec, vec, vec],
        out_shape=[jax.ShapeDtypeStruct(dproj.shape, dproj.dtype), jax.ShapeDtypeStruct((s, w), F32),
                   jax.ShapeDtypeStruct((4, LRU_G, LRU_G), F32), jax.ShapeDtypeStruct((4, LRU_G, LRU_G), F32),
                   jax.ShapeDtypeStruct((1, w), F32), jax.ShapeDtypeStruct((1, w), F32),
                   jax.ShapeDtypeStruct((1, w), F32)],
        scratch_shapes=[pltpu.VMEM((ts, w), F32), pltpu.VMEM((ts, w), F32), pltpu.VMEM((ts + 8, w), F32),
                        pltpu.VMEM((8, w), F32), pltpu.VMEM((ts, w), F32), pltpu.VMEM((ts, w), F32),
                        pltpu.VMEM((ts, w), F32)],
        input_output_aliases={5: 0},
        compiler_params=_cparams(("arbitrary",)), name="lru_bwd")(xl, proj, hs, hs, dy, dproj, wa, wx, ba, bx, lam)


def _mem_kv_fwd(mem, g, wkv):
    m = mem.shape[0]

    def body(mem_ref, g_ref, w_ref, k_ref, v_ref, mn_ref):
        mv = mem_ref[...]
        r = lax.rsqrt(jnp.mean(mv * mv, axis=-1, keepdims=True) + EPS)
        mn = _mx(mv * r * g_ref[...])
        mn_ref[...] = mn
        kv = _dot(mn, w_ref[...])
        k_ref[...] = kv[:, 0:D].astype(k_ref.dtype)
        v_ref[...] = kv[:, D:2 * D].astype(v_ref.dtype)

    sh = jax.ShapeDtypeStruct((m, D), _MXU)
    return pl.pallas_call(body, out_shape=[sh, sh, sh], compiler_params=_cparams(None), name="mem_kv_fwd")(mem, g, wkv)


def _mem_kv_bwd(mem, g, mn, wkv, dk, dv):
    m = mem.shape[0]

    def body(mem_ref, g_ref, mn_ref, w_ref, dk_ref, dv_ref, dw_ref, dg_ref):
        dkv = _mx(jnp.concatenate([dk_ref[...], dv_ref[...]], axis=1))
        dw_ref[...] = _dot_tn(mn_ref[...], dkv).astype(dw_ref.dtype)
        dmn = _dot_nt(dkv, w_ref[...])
        mv = mem_ref[...]
        r = lax.rsqrt(jnp.mean(mv * mv, axis=-1, keepdims=True) + EPS)
        dg_ref[...] = jnp.sum(dmn * mv * r, axis=0, keepdims=True)

    del m
    return pl.pallas_call(
        body, out_shape=[jax.ShapeDtypeStruct((D, 2 * D), _MXU), jax.ShapeDtypeStruct((1, D), F32)],
        compiler_params=_cparams(None), name="mem_kv_bwd")(mem, g, mn, wkv, dk, dv)


def _attn_probs(q_ref, k_ref, hd):
    sl = slice(MEM_HD * hd, MEM_HD * hd + MEM_HD)
    qh = _mx(q_ref[:, sl])
    sc = _dot_nt(qh, k_ref[:, sl]) * (MEM_HD ** -0.5)
    e = jnp.exp(sc - jnp.max(sc, axis=-1, keepdims=True))
    return sl, qh, e / jnp.sum(e, axis=-1, keepdims=True)


def _attn_fwd(proj, k, v):
    s = proj.shape[0]
    m = k.shape[0]
    ts = _tile(s, 512)

    def body(q_ref, k_ref, v_ref, y_ref):
        for hd in range(MEM_HEADS):
            sl, _, p = _attn_probs(q_ref, k_ref, hd)
            y_ref[:, sl] = _dot(_mx(p), v_ref[:, sl]).astype(y_ref.dtype)

    kvs = pl.BlockSpec((m, D), lambda i: (0, 0))
    return pl.pallas_call(
        body, grid=(s // ts,),
        in_specs=[pl.BlockSpec((ts, D), lambda i: (i, Q[0] // D)), kvs, kvs],
        out_specs=pl.BlockSpec((ts, D), lambda i: (i, 0)),
        out_shape=jax.ShapeDtypeStruct((s, D), _MXU),
        compiler_params=_cparams(("parallel",)), name="attn_fwd")(proj, k, v)


def _attn_bwd(proj, k, v, dy, dproj):
    s = proj.shape[0]
    m = k.shape[0]
    ts = _tile(s, 512)

    def body(q_ref, k_ref, v_ref, dy_ref, dproj_hbm, dq_ref, dk_ref, dv_ref):
        del dproj_hbm

        @pl.when(pl.program_id(0) == 0)
        def _():
            dk_ref[...] = jnp.zeros_like(dk_ref)
            dv_ref[...] = jnp.zeros_like(dv_ref)

        for hd in range(MEM_HEADS):
            sl, qh, p = _attn_probs(q_ref, k_ref, hd)
            dyh = _mx(dy_ref[:, sl])
            dp = _dot_nt(dyh, v_ref[:, sl])
            ds = _mx(p * (dp - jnp.sum(dp * p, axis=-1, keepdims=True)) * (MEM_HD ** -0.5))
            dq_ref[:, sl] = _dot(ds, k_ref[:, sl]).astype(dq_ref.dtype)
            dk_ref[:, sl] += _dot_tn(ds, qh)
            dv_ref[:, sl] += _dot_tn(_mx(p), dyh)

    kvs = pl.BlockSpec((m, D), lambda i: (0, 0))
    qs = pl.BlockSpec((ts, D), lambda i: (i, Q[0] // D))
    return pl.pallas_call(
        body, grid=(s // ts,),
        in_specs=[qs, kvs, kvs, pl.BlockSpec((ts, D), lambda i: (i, 0)), pl.BlockSpec(memory_space=pl.ANY)],
        out_specs=[qs, kvs, kvs],
        out_shape=[jax.ShapeDtypeStruct(dproj.shape, dproj.dtype), jax.ShapeDtypeStruct((m, D), F32),
                   jax.ShapeDtypeStruct((m, D), F32)],
        input_output_aliases={4: 0},
        compiler_params=_cparams(("arbitrary",)), name="attn_bwd")(proj, k, v, dy, dproj)


def _merge_fb(x, target, yssd, ylru, ymem, proj, wbs, wbl, wbm, wo, fg):
    s = x.shape[0]
    ts = _tile(s, 256)

    def body(x_ref, t_ref, ys_ref, yl_ref, ym_ref, gl_ref, wbs_ref, wbl_ref, wbm_ref, wo_ref, fg_ref,
             dgl_ref, dx2_ref, dx2m_ref, mg_ref, db0_ref, db1_ref, db2_ref, loss_ref, dfg_ref):
        @pl.when(pl.program_id(0) == 0)
        def _():
            loss_ref[...] = jnp.zeros_like(loss_ref)
            dfg_ref[...] = jnp.zeros_like(dfg_ref)

        bs = (_dot(ys_ref[...], wbs_ref[...]), _dot(yl_ref[...], wbl_ref[...]), _dot(ym_ref[...], wbm_ref[...]))
        gates = [_sigmoid(gl_ref[:, D * n:D * n + D]) for n in range(3)]
        merged = gates[0] * bs[0] + gates[1] * bs[1] + gates[2] * bs[2]
        mg = _mx(merged)
        mg_ref[...] = mg
        x2 = x_ref[...] + _dot(mg, wo_ref[...])
        r = lax.rsqrt(jnp.mean(x2 * x2, axis=-1, keepdims=True) + EPS)
        xhat = x2 * r
        err = xhat * fg_ref[...] - t_ref[...]
        loss_ref[...] += jnp.sum(err * err, axis=0, keepdims=True) * (0.5 / D)
        dy = err * (1.0 / D)
        dfg_ref[...] += jnp.sum(dy * xhat, axis=0, keepdims=True)
        dxh = dy * fg_ref[...]
        dx2 = r * (dxh - xhat * jnp.mean(dxh * xhat, axis=-1, keepdims=True))
        dx2_ref[...] = dx2
        dx2m = _mx(dx2)
        dx2m_ref[...] = dx2m
        dmg = _dot_nt(dx2m, wo_ref[...])
        for n, db_ref in enumerate((db0_ref, db1_ref, db2_ref)):
            gt = gates[n]
            dgl_ref[:, D * n:D * n + D] = (dmg * bs[n] * gt * (1.0 - gt)).astype(dgl_ref.dtype)
            db_ref[...] = (dmg * gt).astype(db_ref.dtype)

    row = lambda w: pl.BlockSpec((ts, w), lambda i: (i, 0))
    full = lambda a: pl.BlockSpec(a.shape, lambda i: (0, 0))
    vec = pl.BlockSpec((1, D), lambda i: (0, 0))
    gls = pl.BlockSpec((ts, GL[1]), lambda i: (i, GL[0] // GL[1]))
    act = jax.ShapeDtypeStruct((s, D), _MXU)
    return pl.pallas_call(
        body, grid=(s // ts,),
        in_specs=[row(D), row(D), row(SSD_W), row(LRU_W), row(D), gls, full(wbs), full(wbl), full(wbm), full(wo), vec],
        out_specs=[gls, row(D), row(D), row(D), row(D), row(D), row(D), vec, vec],
        out_shape=[jax.ShapeDtypeStruct((s, NP), _MXU), jax.ShapeDtypeStruct((s, D), F32), act, act, act, act, act,
                   jax.ShapeDtypeStruct((1, D), F32), jax.ShapeDtypeStruct((1, D), F32)],
        compiler_params=_cparams(("arbitrary",)), name="merge_fwd_bwd")(
            x, target, yssd, ylru, ymem, proj, wbs, wbl, wbm, wo, fg)


def _adamw(w, g, m, v, name):
    rows, cols = w.shape
    tr = rows if rows <= 512 else 256
    assert rows % tr == 0

    def body(w_ref, g_ref, m_ref, v_ref, d_ref, mo_ref, vo_ref):
        gv = g_ref[...]
        mn = ADAM_B1 * m_ref[...] + (1.0 - ADAM_B1) * gv
        vn = ADAM_B2 * v_ref[...] + (1.0 - ADAM_B2) * (gv * gv)
        m_hat = mn / (1.0 - ADAM_B1 ** ADAM_STEP)
        v_hat = vn / (1.0 - ADAM_B2 ** ADAM_STEP)
        d_ref[...] = -ADAM_LR * (m_hat / (jnp.sqrt(v_hat) + ADAM_EPS) + ADAM_WD * w_ref[...])
        mo_ref[...] = mn
        vo_ref[...] = vn

    blk = pl.BlockSpec((tr, cols), lambda i: (i, 0))
    sh = jax.ShapeDtypeStruct((rows, cols), F32)
    return pl.pallas_call(
        body, grid=(rows // tr,), in_specs=[blk] * 4, out_specs=[blk] * 3, out_shape=[sh] * 3,
        compiler_params=_cparams(("parallel",)), name=name)(w, g, m, v)


def _mesh_pos():
    x, y, c = lax.axis_index("x"), lax.axis_index("y"), lax.axis_index("c")
    chips = [(1 - x, y), (x, 1 - y), (1 - x, 1 - y)]
    return x, y, c, 2 * x + y, chips


def _hbm():
    return pl.BlockSpec(memory_space=pl.ANY)


def _remote(src, dst, send_sem, recv_sem, dev):
    return pltpu.make_async_remote_copy(src_ref=src, dst_ref=dst, send_sem=send_sem, recv_sem=recv_sem,
                                        device_id=dev, device_id_type=MESH)


def _sems(n):
    return [pltpu.SemaphoreType.DMA((n,)), pltpu.SemaphoreType.DMA((n,))]


def _gather_shards(arrs, split):
    n = len(arrs)
    n_sem = sum(6 if sp else 3 for sp in split)

    def body(*refs):
        srcs, outs = refs[:n], refs[n:2 * n]
        send_sems, recv_sems = refs[2 * n:]
        x, y, c, me, chips = _mesh_pos()
        sib = (x, y, 1 - c)

        def rows(i, which):
            if not split[i]:
                return pl.ds(0, arrs[i].shape[0])
            half = arrs[i].shape[0] // 2
            return pl.ds(which * half, half)

        sends, plan, k = [], [], 0
        for i in range(n):
            for j, (cx, cy) in enumerate(chips):
                cp = _remote(srcs[i].at[rows(i, c)], outs[i].at[me, rows(i, c)], send_sems.at[k], recv_sems.at[k],
                             (cx, cy, c))
                cp.start()
                sends.append(cp)
                plan.append((i, j, k))
                k += 1
        passed = []
        for i, j, k0 in plan:
            cx, cy = chips[j]
            slot = outs[i].at[2 * cx + cy, rows(i, c)]
            _remote(slot, slot, send_sems.at[k0], recv_sems.at[k0], (cx, cy, c)).wait_recv()
            if split[i]:
                fwd = _remote(slot, slot, send_sems.at[k], recv_sems.at[k], sib)
                fwd.start()
                passed.append((fwd, i, j, k))
                k += 1
        for _, i, j, kf in passed:
            cx, cy = chips[j]
            slot = outs[i].at[2 * cx + cy, rows(i, 1 - c)]
            _remote(slot, slot, send_sems.at[kf], recv_sems.at[kf], sib).wait_recv()
        for cp in sends + [p[0] for p in passed]:
            cp.wait_send()

    got = pl.pallas_call(
        body, in_specs=[_hbm()] * n, out_specs=[_hbm()] * n,
        out_shape=[jax.ShapeDtypeStruct((NSHARD,) + a.shape, a.dtype) for a in arrs],
        scratch_shapes=_sems(n_sem), name="gather_shards")(*arrs)
    own_slot = jnp.arange(NSHARD, dtype=jnp.int32)[:, None, None] == 2 * lax.axis_index("x") + lax.axis_index("y")
    return [jnp.where(own_slot, a[None], g) for a, g in zip(arrs, got)]


def _swap_halves(arrs):
    n = len(arrs)

    def body(*refs):
        srcs, outs = refs[:n], refs[n:2 * n]
        send_sems, recv_sems = refs[2 * n:]
        x, y, c, _, _ = _mesh_pos()
        sib = (x, y, 1 - c)
        cps = []
        for i in range(n):
            half = arrs[i].shape[1] // 2
            cps.append(_remote(srcs[i].at[:, pl.ds((1 - c) * half, half)], outs[i], send_sems.at[i], recv_sems.at[i],
                               sib))
        for cp in cps:
            cp.start()
        for cp in cps:
            cp.wait()

    return pl.pallas_call(
        body, in_specs=[_hbm()] * n, out_specs=[_hbm()] * n,
        out_shape=[jax.ShapeDtypeStruct((NSHARD, a.shape[1] // 2, a.shape[2]), a.dtype) for a in arrs],
        scratch_shapes=_sems(n), name="swap_halves")(*arrs)


def _scatter_chips(arrs):
    n = len(arrs)

    def body(*refs):
        srcs, outs = refs[:n], refs[n:2 * n]
        send_sems, recv_sems, local_sems = refs[2 * n:]
        _, _, c, me, chips = _mesh_pos()
        own = [pltpu.make_async_copy(srcs[i].at[me], outs[i].at[me], local_sems.at[i]) for i in range(n)]
        for cp in own:
            cp.start()
        cps = []
        for i in range(n):
            for j, (cx, cy) in enumerate(chips):
                cps.append(_remote(srcs[i].at[2 * cx + cy], outs[i].at[me], send_sems.at[3 * i + j],
                                   recv_sems.at[3 * i + j], (cx, cy, c)))
        for cp in cps:
            cp.start()
        for i in range(n):
            for j, (cx, cy) in enumerate(chips):
                slot = outs[i].at[2 * cx + cy]
                _remote(slot, slot, send_sems.at[3 * i + j], recv_sems.at[3 * i + j], (cx, cy, c)).wait_recv()
        for cp in cps:
            cp.wait_send()
        for cp in own:
            cp.wait()

    return pl.pallas_call(
        body, in_specs=[_hbm()] * n, out_specs=[_hbm()] * n,
        out_shape=[jax.ShapeDtypeStruct(a.shape, a.dtype) for a in arrs],
        scratch_shapes=_sems(3 * n) + [pltpu.SemaphoreType.DMA((n,))], name="scatter_chips")(*arrs)


def _share_halves(arrs):
    n = len(arrs)

    def body(*refs):
        outs = refs[n:2 * n]
        send_sems, recv_sems = refs[2 * n:]
        x, y, c, _, _ = _mesh_pos()
        sib = (x, y, 1 - c)
        cps = [_remote(outs[i].at[c], outs[i].at[c], send_sems.at[i], recv_sems.at[i], sib) for i in range(n)]
        for cp in cps:
            cp.start()
        for i in range(n):
            theirs = outs[i].at[1 - c]
            _remote(theirs, theirs, send_sems.at[i], recv_sems.at[i], sib).wait_recv()
        for cp in cps:
            cp.wait_send()

    return pl.pallas_call(
        body, in_specs=[_hbm()] * n, out_specs=[_hbm()] * n,
        out_shape=[jax.ShapeDtypeStruct(a.shape, a.dtype) for a in arrs],
        input_output_aliases={i: i for i in range(n)},
        scratch_shapes=_sems(n), name="share_halves")(*arrs)


def _gather_small(full):
    _, width = full.shape

    def body(src, out, send_sems, recv_sems, local_sem):
        _, _, c, me, chips = _mesh_pos()
        mine = src.at[pl.ds(0, SMALL_ROWS)]
        own = pltpu.make_async_copy(mine, out.at[me], local_sem)
        own.start()
        cps = [_remote(mine, out.at[me], send_sems.at[j], recv_sems.at[j], (cx, cy, c))
               for j, (cx, cy) in enumerate(chips)]
        for cp in cps:
            cp.start()
        for j, (cx, cy) in enumerate(chips):
            slot = out.at[2 * cx + cy]
            _remote(slot, slot, send_sems.at[j], recv_sems.at[j], (cx, cy, c)).wait_recv()
        for cp in cps:
            cp.wait_send()
        own.wait()

    return pl.pallas_call(
        body, in_specs=[_hbm()], out_specs=_hbm(),
        out_shape=jax.ShapeDtypeStruct((NSHARD, SMALL_ROWS, width), full.dtype),
        scratch_shapes=_sems(3) + [pltpu.SemaphoreType.DMA], name="gather_small")(full)


def _add_sibling(mine, recv, c, name):
    _, half, width = recv.shape
    tr = _tile(half, 256, 8)
    nb = half // tr

    def body(c_ref, a_ref, b_ref, o_ref):
        del c_ref
        o_ref[...] = (a_ref[...].astype(F32) + b_ref[...].astype(F32)).astype(o_ref.dtype)

    grid_spec = pltpu.PrefetchScalarGridSpec(
        num_scalar_prefetch=1, grid=(NSHARD, nb),
        in_specs=[pl.BlockSpec((1, tr, width), lambda j, r, c_ref: (j, c_ref[0] * nb + r, 0)),
                  pl.BlockSpec((1, tr, width), lambda j, r, c_ref: (j, r, 0))],
        out_specs=pl.BlockSpec((1, tr, width), lambda j, r, c_ref: (j, r, 0)))
    return pl.pallas_call(
        body, grid_spec=grid_spec, out_shape=jax.ShapeDtypeStruct(recv.shape, recv.dtype),
        compiler_params=_cparams(("parallel", "parallel")), name=name)(c, mine, recv)


def _sum_chips(parts, c, name):
    _, half, width = parts.shape
    tr = _tile(half, 256, 8)

    def body(c_ref, p_ref, o_ref):
        del c_ref
        p = [p_ref[j].astype(F32) for j in range(NSHARD)]
        o_ref[0] = ((p[0] + p[1]) + p[2]) + p[3]

    grid_spec = pltpu.PrefetchScalarGridSpec(
        num_scalar_prefetch=1, grid=(half // tr,),
        in_specs=[pl.BlockSpec((NSHARD, tr, width), lambda r, c_ref: (0, r, 0))],
        out_specs=pl.BlockSpec((1, tr, width), lambda r, c_ref: (c_ref[0], r, 0)))
    return pl.pallas_call(
        body, grid_spec=grid_spec, out_shape=jax.ShapeDtypeStruct((2, half, width), F32),
        compiler_params=_cparams(("parallel",)), name=name)(c, parts)


def _unpack(flat, names, shapes):
    out, off = {}, 0
    for n in names:
        sz = _size(shapes[n])
        out[n] = flat[off:off + sz].reshape(shapes[n])
        off += sz
    return out


def _reorder_w_in(w):
    return jnp.concatenate([w[:, 2048:5120], w[:, 9248:12320], w[:, 0:2048], w[:, 8224:9248], w[:, 5152:6688],
                            w[:, 6688:8224], w[:, 5120:5152], jnp.zeros((D, NP - 12320), w.dtype)], axis=1)


def _restore_w_in(g):
    return jnp.concatenate([g[:, 6144:8192], g[:, 0:3072], g[:, 12288:12320], g[:, 9216:10752], g[:, 10752:12288],
                            g[:, 8192:9216], g[:, 3072:6144]], axis=1)


def _lru_group_weights(w):
    w4 = w.reshape(4, 4, 96, 96)
    eye = jnp.eye(4, dtype=w.dtype)
    return (w4[:, :, None, :, :] * eye[None, :, :, None, None]).transpose(0, 1, 3, 2, 4).reshape(4, LRU_G, LRU_G)


def _lru_group_blocks(g):
    g5 = g.reshape(4, 4, 96, 4, 96)
    return jnp.stack([g5[:, a, :, a, :] for a in range(4)], axis=1).reshape(16, 96, 96)


def _local_grads(x, mem, target, wts):
    pad128 = lambda a: jnp.pad(a, ((0, 0), (0, 128 - a.shape[1])))
    w_in_r = wts["w_in_r"]
    wbs, wbl, wbm, wo, wkv = wts["w_br_ssd"], wts["w_br_lru"], wts["w_br_mem"], wts["w_out"], wts["w_kv"]
    wa, wx = _mx(_lru_group_weights(wts["lru_w_a"])), _mx(_lru_group_weights(wts["lru_w_x"]))
    ba, bx = wts["lru_b_a"].reshape(1, LRU_W), wts["lru_b_x"].reshape(1, LRU_W)
    dtb, alog = pad128(wts["ssd_dt_bias"]), pad128(wts["ssd_a_log"])
    dexp = jnp.repeat(wts["ssd_d"], 64, axis=1)
    ng = wts["ssd_norm_g"].reshape(1, SSD_W)

    h = _norm_fwd(x, wts["norm_g"])
    proj = _mm(h, w_in_r, F32, "in_proj")
    xbc = _conv_fwd(proj, XBC, wts["ssd_conv_w"], wts["ssd_conv_b"], True, "ssd_conv_fwd")
    yssd, yraw, hprev = _ssd_fwd(xbc, proj, dtb, alog, dexp, ng)
    xl = _conv_fwd(proj, LX, wts["lru_conv_w"], wts["lru_conv_b"], False, "lru_conv_fwd")
    ylru, hs = _lru_fwd(xl, proj, wa, wx, ba, bx, wts["lru_lambda"])
    kk, vv, mn = _mem_kv_fwd(mem, wts["mem_norm_g"], wkv)
    ymem = _attn_fwd(proj, kk, vv)

    dproj, dx2, dx2m, merged, db0, db1, db2, loss_vec, dfg = _merge_fb(
        x, target, yssd, ylru, ymem, proj, wbs, wbl, wbm, wo, wts["final_g"].reshape(1, D))
    grads = {"final_g": dfg.reshape(D)}
    grads["w_out"] = _mm(merged, dx2m, _MXU, "dw_out", ta=True)
    grads["w_br_ssd"] = _mm(yssd, db0, _MXU, "dw_br_ssd", ta=True)
    grads["w_br_lru"] = _mm(ylru, db1, _MXU, "dw_br_lru", ta=True)
    grads["w_br_mem"] = _mm(ymem, db2, _MXU, "dw_br_mem", ta=True)
    dyssd = _mm(db0, wbs, F32, "dy_ssd", tb=True)
    dylru = _mm(db1, wbl, F32, "dy_lru", tb=True)
    dymem = _mm(db2, wbm, F32, "dy_mem", tb=True)

    dproj, dk, dv = _attn_bwd(proj, kk, vv, dymem, dproj)
    grads["w_kv"], grads["mem_norm_g"] = _mem_kv_bwd(mem, wts["mem_norm_g"], mn, wkv, dk, dv)

    dproj, dxl, dwa, dwx, dba, dbx, dlam = _lru_bwd(xl, proj, hs, dylru, dproj, wa, wx, ba, bx, wts["lru_lambda"])
    grads["lru_w_a"] = _lru_group_blocks(dwa)[None]
    grads["lru_w_x"] = _lru_group_blocks(dwx)[None]
    grads["lru_b_a"], grads["lru_b_x"] = dba.reshape(1, 16, 96), dbx.reshape(1, 16, 96)
    grads["lru_lambda"] = dlam
    grads["lru_conv_w"], grads["lru_conv_b"] = _conv_bwd_w(
        proj, LX, wts["lru_conv_w"], wts["lru_conv_b"], dxl, False, "lru_conv_bwd_w")
    dproj = _conv_bwd_x(dxl, wts["lru_conv_w"], dproj, LX, "lru_conv_bwd_x")

    dproj, ddt, dxbc, dng, dda, ddd, ddtb = _ssd_bwd(xbc, proj, yraw, hprev, dyssd, dproj, dtb, alog, dexp, ng)
    dproj = _put_block(ddt, dproj, DT, "put_ddt")
    grads["ssd_norm_g"] = dng.reshape(4, 512)
    grads["ssd_dt_bias"] = ddtb[:, 0:32]
    grads["ssd_a_log"] = (dda * -jnp.exp(alog))[:, 0:32]
    grads["ssd_d"] = ddd.reshape(32, 64).sum(axis=1)[None, :]
    dpre, grads["ssd_conv_w"], grads["ssd_conv_b"] = _conv_bwd_w(
        proj, XBC, wts["ssd_conv_w"], wts["ssd_conv_b"], dxbc, True, "ssd_conv_bwd_w")
    dproj = _conv_bwd_x(dpre, wts["ssd_conv_w"], dproj, XBC, "ssd_conv_bwd_x")

    grads["w_in_r"] = _mm(h, dproj, _MXU, "dw_in", ta=True)
    dh = _mm(dproj, w_in_r, F32, "dh", tb=True, tn=1024, tk=1280)
    grad_x, grads["norm_g"] = _norm_bwd(x, wts["norm_g"], dh, dx2)
    return jnp.sum(loss_vec), grad_x, grads


def kernel(x, mem, norm_g, w_in, ssd_conv_w, ssd_conv_b, ssd_dt_bias, ssd_a_log, ssd_d, ssd_norm_g, lru_conv_w, lru_conv_b, lru_w_a, lru_b_a, lru_w_x, lru_b_x, lru_lambda, mem_norm_g, w_kv, w_br_ssd, w_br_lru, w_br_mem, w_out, final_g, loss_target, m_norm_g, m_w_in, m_ssd_conv_w, m_ssd_conv_b, m_ssd_dt_bias, m_ssd_a_log, m_ssd_d, m_ssd_norm_g, m_lru_conv_w, m_lru_conv_b, m_lru_w_a, m_lru_b_a, m_lru_w_x, m_lru_b_x, m_lru_lambda, m_mem_norm_g, m_w_kv, m_w_br_ssd, m_w_br_lru, m_w_br_mem, m_w_out, m_final_g, v_norm_g, v_w_in, v_ssd_conv_w, v_ssd_conv_b, v_ssd_dt_bias, v_ssd_a_log, v_ssd_d, v_ssd_norm_g, v_lru_conv_w, v_lru_conv_b, v_lru_w_a, v_lru_b_a, v_lru_w_x, v_lru_b_x, v_lru_lambda, v_mem_norm_g, v_w_kv, v_w_br_ssd, v_w_br_lru, v_w_br_mem, v_w_out, v_final_g):
    given = dict(locals())

    rows_w = jnp.concatenate([w_br_ssd[0], w_br_lru[0], w_br_mem[0], w_out[0]], axis=0)
    small_w = jnp.concatenate([ssd_conv_w[0], ssd_norm_g[0], lru_conv_w[0]], axis=1)
    g_in, g_kv, g_rows, g_small = _gather_shards(
        [_mx(w_in[0]), _mx(w_kv[0]), _mx(rows_w), small_w], [True, True, True, False])
    spread = lambda a: a.transpose(1, 0, 2).reshape(a.shape[1], NSHARD * a.shape[2])
    wts = {n: given[n] for n in REPL}
    wts["lru_w_a"], wts["lru_w_x"] = lru_w_a[0], lru_w_x[0]
    wts["w_in_r"] = _reorder_w_in(spread(g_in))
    wts["w_kv"] = spread(g_kv)
    for n, lo_, hi_ in ROW_PIECES:
        wts[n] = g_rows[:, lo_:hi_].reshape(NSHARD * (hi_ - lo_), D)
    wts["ssd_conv_w"] = spread(g_small[:, :, 0:768])
    wts["ssd_norm_g"] = spread(g_small[:, :, 768:896])
    wts["lru_conv_w"] = spread(g_small[:, :, 896:1280])

    loss_part, grad_x, grads = _local_grads(x[0], mem[0], loss_target[0], wts)
    loss = lax.psum(loss_part, ("x", "y", "c"))

    split = lambda a: a.reshape(a.shape[0], NSHARD, a.shape[1] // NSHARD).transpose(1, 0, 2)
    p_in = split(_restore_w_in(grads["w_in_r"]))
    p_kv = split(grads["w_kv"])
    p_rows = jnp.concatenate([grads[n].reshape(NSHARD, hi_ - lo_, D) for n, lo_, hi_ in ROW_PIECES], axis=1)
    repl_flat = jnp.concatenate([grads[n].reshape(-1) for n in REPL])
    repl_flat = jnp.pad(repl_flat, (0, NSHARD * SMALL_Q - repl_flat.shape[0])).reshape(NSHARD, SMALL_Q)
    shard_small = jnp.concatenate([split(grads[n]).reshape(NSHARD, -1) for n in SMALL_SHARDED], axis=1)
    p_small = jnp.concatenate(
        [repl_flat, shard_small, jnp.zeros((NSHARD, SMALL_BUF_ROWS * PACK_W - SMALL_Q - 5120), F32)], axis=1)
    p_small = p_small.reshape(NSHARD, SMALL_BUF_ROWS, PACK_W)

    parts = [p_in, p_kv, p_rows, p_small]
    names = ["w_in", "w_kv", "rows", "small"]
    c_idx = lax.axis_index("c").astype(jnp.int32).reshape(1)
    recv = _swap_halves(parts)
    chip_parts = [_add_sibling(p, r, c_idx, "add_sibling_" + n) for p, r, n in zip(parts, recv, names)]
    landed = _scatter_chips(chip_parts)
    halves = [_sum_chips(a, c_idx, "sum_chips_" + n) for a, n in zip(landed, names)]
    r_in, r_kv, r_rows, r_small = [a.reshape(2 * a.shape[1], a.shape[2]) for a in _share_halves(halves)]
    repl_all = _gather_small(r_small).reshape(-1)

    g_shard = {"w_in": r_in, "w_kv": r_kv}
    for n, lo_, hi_ in ROW_PIECES:
        g_shard[n] = r_rows[lo_:hi_]
    g_shard.update(_unpack(r_small.reshape(-1)[SMALL_Q:], SMALL_SHARDED, SHARD_SHAPE))
    g_repl = _unpack(repl_all, REPL, REPL_SHAPE)

    out_g, out_d, out_m, out_v = {}, {}, {}, {}
    for n in WEIGHTS:
        w_full = given[n]
        g = (g_shard[n] if n in SHARDED else g_repl[n]).reshape(w_full.shape)
        cols = w_full.shape[-1]
        as2d = lambda a: a.reshape(-1, cols)
        d, mo, vo = _adamw(as2d(w_full), as2d(g), as2d(given["m_" + n]), as2d(given["v_" + n]), "adamw_" + n)
        out_g[n] = g
        out_d[n], out_m[n], out_v[n] = d.reshape(w_full.shape), mo.reshape(w_full.shape), vo.reshape(w_full.shape)

    return (loss, grad_x[None], *[out_g[n] for n in WEIGHTS], *[out_d[n] for n in WEIGHTS],
            *[out_m[n] for n in WEIGHTS], *[out_v[n] for n in WEIGHTS])
```

```python
import jax
import jax.numpy as jnp
from jax import lax
from jax.experimental import pallas as pl
from jax.experimental.pallas import tpu as pltpu

F32 = jnp.float32
_MXU = jnp.bfloat16
_HI = lax.Precision.HIGHEST
MESH = pl.DeviceIdType.MESH

D = 1024
EPS = 1e-6
MEM_HEADS = 4
MEM_HD = 256
LRU_C = 8.0
SSD_L = 128
SSD_W = 2048
LRU_W = 1536
NSHARD = 4

XBC = (0, 3072)
GL = (3072, 3072)
Z = (6144, 2048)
Q = (8192, 1024)
LG = (9216, 1536)
LX = (10752, 1536)
DT = (12288, 512)
NP = 12800

ADAM_LR = 0.001
ADAM_B1 = 0.9
ADAM_B2 = 0.999
ADAM_EPS = 1e-08
ADAM_WD = 0.01
ADAM_STEP = 10

VMEM_LIMIT = 56 * 1024 * 1024

SHARDED = ("w_in", "ssd_conv_w", "ssd_norm_g", "lru_conv_w", "w_kv", "w_br_ssd", "w_br_lru", "w_br_mem", "w_out")
SHARD_SHAPE = {"w_in": (1024, 3080), "ssd_conv_w": (4, 768), "ssd_norm_g": (4, 128), "lru_conv_w": (4, 384),
               "w_kv": (1024, 512), "w_br_ssd": (512, 1024), "w_br_lru": (384, 1024), "w_br_mem": (256, 1024),
               "w_out": (256, 1024)}
REPL = ("norm_g", "ssd_conv_b", "ssd_dt_bias", "ssd_a_log", "ssd_d", "lru_conv_b", "lru_w_a", "lru_b_a",
        "lru_w_x", "lru_b_x", "lru_lambda", "mem_norm_g", "final_g")
REPL_SHAPE = {"norm_g": (1, 1024), "ssd_conv_b": (1, 3072), "ssd_dt_bias": (1, 32), "ssd_a_log": (1, 32),
              "ssd_d": (1, 32), "lru_conv_b": (1, 1536), "lru_w_a": (1, 16, 96, 96), "lru_b_a": (1, 16, 96),
              "lru_w_x": (1, 16, 96, 96), "lru_b_x": (1, 16, 96), "lru_lambda": (1, 1536),
              "mem_norm_g": (1, 1024), "final_g": (1024,)}
WEIGHTS = ("norm_g", "w_in", "ssd_conv_w", "ssd_conv_b", "ssd_dt_bias", "ssd_a_log", "ssd_d", "ssd_norm_g",
           "lru_conv_w", "lru_conv_b", "lru_w_a", "lru_b_a", "lru_w_x", "lru_b_x", "lru_lambda", "mem_norm_g",
           "w_kv", "w_br_ssd", "w_br_lru", "w_br_mem", "w_out", "final_g")

ROW_PIECES = (("w_br_ssd", 0, 512), ("w_br_lru", 512, 896), ("w_br_mem", 896, 1152), ("w_out", 1152, 1408))
SMALL_SHARDED = ("ssd_conv_w", "ssd_norm_g", "lru_conv_w")
PACK_W = 512
SMALL_ROWS = 152
SMALL_Q = SMALL_ROWS * PACK_W
SMALL_BUF_ROWS = 176


def _size(shape):
    n = 1
    for s in shape:
        n *= s
    return n


def _sigmoid(x):
    return 0.5 * jnp.tanh(0.5 * x) + 0.5


def _silu(x):
    return x * _sigmoid(x)


def _dsilu(x):
    s = _sigmoid(x)
    return s * (1.0 + x * (1.0 - s))


def _softplus(x):
    return jnp.maximum(x, 0.0) + jnp.log(1.0 + jnp.exp(-jnp.abs(x)))


def _one_minus_sq(log_a, a):
    x = 2.0 * log_a
    series = -x * (1.0 + x * (0.5 + x * (1.0 / 6.0 + x * (1.0 / 24.0))))
    return jnp.where(x > -0.03, series, 1.0 - a * a)


def _dot(a, b, precision=None):
    return jnp.dot(a, b, preferred_element_type=F32, precision=precision)


def _dot_nt(a, b):
    return lax.dot_general(a, b, (((1,), (1,)), ((), ())), preferred_element_type=F32)


def _dot_tn(a, b):
    return lax.dot_general(a, b, (((0,), (0,)), ((), ())), preferred_element_type=F32)


def _mx(a):
    return a.astype(_MXU)


def _cparams(sem):
    return pltpu.CompilerParams(dimension_semantics=sem, vmem_limit_bytes=VMEM_LIMIT)


def _tile(n, want, mult=128):
    if n <= want:
        return n
    for t in range(want - want % mult, 0, -mult):
        if n % t == 0:
            return t
    raise ValueError((n, want, mult))


def _mm(a, b, out_dtype, name, ta=False, tb=False, tm=1024, tn=1280, tk=1024, ride=None):
    k, m = a.shape if ta else a.shape[::-1]
    k2, n = b.shape[::-1] if tb else b.shape
    assert k == k2
    tm, tn, tk = _tile(m, tm), _tile(n, tn), _tile(k, tk)
    nk = k // tk
    contract = (((0 if ta else 1,), (1 if tb else 0,)), ((), ()))

    def body(a_ref, b_ref, o_ref, acc_ref):
        kk = pl.program_id(2)

        @pl.when(kk == 0)
        def _():
            acc_ref[...] = jnp.zeros_like(acc_ref)

        acc_ref[...] += lax.dot_general(a_ref[...], b_ref[...], contract, preferred_element_type=F32)

        @pl.when(kk == nk - 1)
        def _():
            o_ref[...] = acc_ref[...].astype(o_ref.dtype)

    a_spec = pl.BlockSpec((tk, tm), lambda i, j, kk: (kk, i)) if ta else pl.BlockSpec((tm, tk), lambda i, j, kk: (i, kk))
    b_spec = pl.BlockSpec((tn, tk), lambda i, j, kk: (j, kk)) if tb else pl.BlockSpec((tk, tn), lambda i, j, kk: (kk, j))
    outs, carried = _pcall(
        body, ride, (a, b), grid=(m // tm, n // tn, nk),
        in_specs=[a_spec, b_spec],
        out_specs=[pl.BlockSpec((tm, tn), lambda i, j, kk: (i, j))],
        out_shape=[jax.ShapeDtypeStruct((m, n), out_dtype)],
        scratch_shapes=[pltpu.VMEM((tm, tn), F32)],
        sem=("parallel", "parallel", "arbitrary"), name=name)
    return outs[0] if ride is None else (outs[0], carried)


def _norm_fwd(x, g):
    s = x.shape[0]
    ts = _tile(s, 512)

    def body(x_ref, g_ref, h_ref):
        xv = x_ref[...]
        r = lax.rsqrt(jnp.mean(xv * xv, axis=-1, keepdims=True) + EPS)
        h_ref[...] = (xv * r * g_ref[...]).astype(h_ref.dtype)

    return pl.pallas_call(
        body, grid=(s // ts,),
        in_specs=[pl.BlockSpec((ts, D), lambda i: (i, 0)), pl.BlockSpec((1, D), lambda i: (0, 0))],
        out_specs=pl.BlockSpec((ts, D), lambda i: (i, 0)),
        out_shape=jax.ShapeDtypeStruct((s, D), _MXU),
        compiler_params=_cparams(("parallel",)), name="norm_fwd")(x, g)


def _norm_bwd(x, g, dh, dx2, ride=None):
    s = x.shape[0]
    ts = _tile(s, 512)

    def body(x_ref, g_ref, dh_ref, dx2_ref, gx_ref, dg_ref):
        @pl.when(pl.program_id(0) == 0)
        def _():
            dg_ref[...] = jnp.zeros_like(dg_ref)

        xv = x_ref[...]
        r = lax.rsqrt(jnp.mean(xv * xv, axis=-1, keepdims=True) + EPS)
        xhat = xv * r
        dh_v = dh_ref[...]
        dg_ref[...] += jnp.sum(dh_v * xhat, axis=0, keepdims=True)
        dxh = dh_v * g_ref[...]
        gx_ref[...] = dx2_ref[...] + r * (dxh - xhat * jnp.mean(dxh * xhat, axis=-1, keepdims=True))

    row = pl.BlockSpec((ts, D), lambda i: (i, 0))
    vec = pl.BlockSpec((1, D), lambda i: (0, 0))
    return _pcall(
        body, ride, (x, g, dh, dx2), grid=(s // ts,), in_specs=[row, vec, row, row], out_specs=[row, vec],
        out_shape=[jax.ShapeDtypeStruct((s, D), F32), jax.ShapeDtypeStruct((1, D), F32)],
        scratch_shapes=[], sem=("arbitrary",), name="norm_bwd")


CONV_RB = 16
CONV_LC = 256


def _fold8(v):
    acc = v[0:8]
    for r0 in range(8, v.shape[0], 8):
        acc = acc + v[r0:r0 + 8]
    return acc


def _conv_fwd(src, blk, w, b, act, name):
    s = src.shape[0]
    off, width = blk
    cb = off // width
    ts = _tile(s, 256)

    def body(x_ref, w_ref, b_ref, o_ref, ext_ref):
        @pl.when(pl.program_id(0) == 0)
        def _():
            ext_ref[0:8, :] = jnp.zeros((8, width), F32)

        ext_ref[8:8 + ts, :] = x_ref[...]
        for l0 in range(0, width, CONV_LC):
            ls = slice(l0, l0 + CONV_LC)
            taps = [w_ref[k:k + 1, ls] for k in range(4)]
            bias = b_ref[:, ls]
            for r0 in range(0, ts, CONV_RB):
                pre = bias
                for k in range(4):
                    pre = pre + taps[k] * ext_ref[5 + k + r0:5 + k + r0 + CONV_RB, ls]
                o_ref[r0:r0 + CONV_RB, ls] = _silu(pre) if act else pre
        ext_ref[0:8, :] = x_ref[ts - 8:ts, :]

    return pl.pallas_call(
        body, grid=(s // ts,),
        in_specs=[pl.BlockSpec((ts, width), lambda i: (i, cb)), pl.BlockSpec((4, width), lambda i: (0, 0)),
                  pl.BlockSpec((1, width), lambda i: (0, 0))],
        out_specs=pl.BlockSpec((ts, width), lambda i: (i, 0)),
        out_shape=jax.ShapeDtypeStruct((s, width), F32),
        scratch_shapes=[pltpu.VMEM((ts + 8, width), F32)],
        compiler_params=_cparams(("arbitrary",)), name=name)(src, w, b)


def _conv_bwd_w(src, blk, w, b, dout, act, name, ride=None):
    s = src.shape[0]
    off, width = blk
    cb = off // width
    ts = _tile(s, 256)

    def body(x_ref, w_ref, b_ref, do_ref, *rest):
        if act:
            dpre_ref, dw_ref, db_ref, ext_ref = rest
        else:
            dw_ref, db_ref, ext_ref = rest

        @pl.when(pl.program_id(0) == 0)
        def _():
            ext_ref[0:8, :] = jnp.zeros((8, width), F32)
            dw_ref[...] = jnp.zeros_like(dw_ref)
            db_ref[...] = jnp.zeros_like(db_ref)

        ext_ref[8:8 + ts, :] = x_ref[...]
        for l0 in range(0, width, CONV_LC):
            ls = slice(l0, l0 + CONV_LC)
            taps = [w_ref[k:k + 1, ls] for k in range(4)]
            bias = b_ref[:, ls]
            acc_b = jnp.zeros((8, CONV_LC), F32)
            acc_w = [jnp.zeros((8, CONV_LC), F32) for _ in range(4)]
            for r0 in range(0, ts, CONV_RB):
                xs = [ext_ref[5 + k + r0:5 + k + r0 + CONV_RB, ls] for k in range(4)]
                dpre = do_ref[r0:r0 + CONV_RB, ls]
                if act:
                    pre = bias
                    for k in range(4):
                        pre = pre + taps[k] * xs[k]
                    dpre = dpre * _dsilu(pre)
                    dpre_ref[r0:r0 + CONV_RB, ls] = dpre
                acc_b = acc_b + _fold8(dpre)
                for k in range(4):
                    acc_w[k] = acc_w[k] + _fold8(dpre * xs[k])
            db_ref[:, ls] += jnp.sum(acc_b, axis=0, keepdims=True)
            for k in range(4):
                dw_ref[k:k + 1, ls] += jnp.sum(acc_w[k], axis=0, keepdims=True)
        ext_ref[0:8, :] = x_ref[ts - 8:ts, :]

    row = pl.BlockSpec((ts, width), lambda i: (i, 0))
    outs = [pl.BlockSpec((4, width), lambda i: (0, 0)), pl.BlockSpec((1, width), lambda i: (0, 0))]
    shapes = [jax.ShapeDtypeStruct((4, width), F32), jax.ShapeDtypeStruct((1, width), F32)]
    if act:
        outs = [row] + outs
        shapes = [jax.ShapeDtypeStruct((s, width), F32)] + shapes
    return _pcall(
        body, ride, (src, w, b, dout), grid=(s // ts,),
        in_specs=[pl.BlockSpec((ts, width), lambda i: (i, cb)), pl.BlockSpec((4, width), lambda i: (0, 0)),
                  pl.BlockSpec((1, width), lambda i: (0, 0)), row],
        out_specs=outs, out_shape=shapes,
        scratch_shapes=[pltpu.VMEM((ts + 8, width), F32)], sem=("arbitrary",), name=name)


def _conv_bwd_x(dpre, w, dproj, blk, name):
    s = dpre.shape[0]
    off, width = blk
    cb = off // width
    ts = _tile(s, 256)
    nt = s // ts

    def body(dp_ref, w_ref, dproj_hbm, o_ref, ext_ref):
        del dproj_hbm

        @pl.when(pl.program_id(0) == 0)
        def _():
            ext_ref[ts:ts + 8, :] = jnp.zeros((8, width), F32)

        ext_ref[0:ts, :] = dp_ref[...]
        for l0 in range(0, width, CONV_LC):
            ls = slice(l0, l0 + CONV_LC)
            taps = [w_ref[k:k + 1, ls] for k in range(4)]
            for r0 in range(0, ts, CONV_RB):
                acc = taps[0] * ext_ref[3 + r0:3 + r0 + CONV_RB, ls]
                for k in range(1, 4):
                    acc = acc + taps[k] * ext_ref[3 - k + r0:3 - k + r0 + CONV_RB, ls]
                o_ref[r0:r0 + CONV_RB, ls] = acc.astype(o_ref.dtype)
        ext_ref[ts:ts + 8, :] = dp_ref[0:8, :]

    return pl.pallas_call(
        body, grid=(nt,),
        in_specs=[pl.BlockSpec((ts, width), lambda i: (nt - 1 - i, 0)), pl.BlockSpec((4, width), lambda i: (0, 0)),
                  pl.BlockSpec(memory_space=pl.ANY)],
        out_specs=pl.BlockSpec((ts, width), lambda i: (nt - 1 - i, cb)),
        out_shape=jax.ShapeDtypeStruct(dproj.shape, dproj.dtype),
        scratch_shapes=[pltpu.VMEM((ts + 8, width), F32)],
        input_output_aliases={2: 0},
        compiler_params=_cparams(("arbitrary",)), name=name)(dpre, w, dproj)


def _head_col(v, h, lane_l):
    return jnp.sum(jnp.where(lane_l == h, v, 0.0), axis=1, keepdims=True)


def _ssd_decay(a_cs, acst_ref, h, causal, lane_l):
    row = acst_ref[h:h + 1, :]
    return jnp.where(causal, jnp.exp(jnp.minimum(_head_col(a_cs, h, lane_l) - row, 0.0)), 0.0)


def _ssd_common(dt_ref, dtb_ref, alog_ref, acst_ref, dtx_ref, acx_ref):
    ll = SSD_L
    dt = _softplus(dt_ref[:, 0:128] + dtb_ref[...])
    a_neg = -jnp.exp(alog_ref[...])
    ri = lax.broadcasted_iota(jnp.int32, (ll, ll), 0)
    ci = lax.broadcasted_iota(jnp.int32, (ll, ll), 1)
    causal = ri >= ci
    a_cs = _dot(causal.astype(F32), dt * a_neg, _HI)
    acst_ref[...] = a_cs.T
    lane_l = lax.broadcasted_iota(jnp.int32, (ll, 128), 1)
    lo = lane_l < 64
    for j in range(16):
        sl = slice(128 * j, 128 * j + 128)
        dtx_ref[:, sl] = jnp.where(lo, _head_col(dt, 2 * j, lane_l), _head_col(dt, 2 * j + 1, lane_l))
        acx_ref[:, sl] = jnp.where(lo, _head_col(a_cs, 2 * j, lane_l), _head_col(a_cs, 2 * j + 1, lane_l))
    return dt, a_neg, a_cs, causal, ri, lane_l, lo


def _ssd_fwd(xbc, proj, dtb, alog, dexp, ng):
    s = xbc.shape[0]
    ll = SSD_L
    nc = s // ll

    def body(xbc_ref, dt_ref, z_ref, dtb_ref, alog_ref, dexp_ref, ng_ref,
             yssd_ref, yraw_ref, hprev_ref, ht_ref, acst_ref, dtx_ref, acx_ref):
        @pl.when(pl.program_id(0) == 0)
        def _():
            ht_ref[...] = jnp.zeros_like(ht_ref)

        hprev_ref[0] = ht_ref[...]
        _, _, a_cs, causal, _, lane_l, lo = _ssd_common(dt_ref, dtb_ref, alog_ref, acst_ref, dtx_ref, acx_ref)
        for g in range(4):
            bg = _mx(xbc_ref[:, 2048 + 128 * g:2176 + 128 * g])
            cg = _mx(xbc_ref[:, 2560 + 128 * g:2688 + 128 * g])
            cbm = _dot_nt(cg, bg)
            for jj in range(4):
                j = 4 * g + jj
                sl = slice(128 * j, 128 * j + 128)
                xp = xbc_ref[:, sl]
                acx = acx_ref[:, sl]
                a_last = acx_ref[ll - 1:ll, sl]
                xdt = xp * dtx_ref[:, sl]
                acc = None
                for hh in range(2):
                    dec = _ssd_decay(a_cs, acst_ref, 2 * j + hh, causal, lane_l)
                    xm = jnp.where(lo if hh == 0 else jnp.logical_not(lo), xdt, 0.0)
                    t = _dot(_mx(dec * cbm), _mx(xm))
                    acc = t if acc is None else acc + t
                ht = ht_ref[j]
                y = acc + _dot(cg, _mx(ht)) * jnp.exp(acx) + xp * dexp_ref[:, sl]
                yraw_ref[:, sl] = y
                st = _dot_tn(bg, _mx(xdt * jnp.exp(a_last - acx)))
                ht_ref[j] = ht * jnp.exp(a_last) + st
        for g in range(4):
            sl = slice(512 * g, 512 * g + 512)
            yg = yraw_ref[:, sl] * _silu(z_ref[:, sl])
            r = lax.rsqrt(jnp.mean(yg * yg, axis=-1, keepdims=True) + EPS)
            yssd_ref[:, sl] = (yg * r * ng_ref[:, sl]).astype(yssd_ref.dtype)

    vec = lambda w: pl.BlockSpec((1, w), lambda c: (0, 0))
    return pl.pallas_call(
        body, grid=(nc,),
        in_specs=[pl.BlockSpec((ll, 3072), lambda c: (c, 0)),
                  pl.BlockSpec((ll, DT[1]), lambda c: (c, DT[0] // DT[1])),
                  pl.BlockSpec((ll, Z[1]), lambda c: (c, Z[0] // Z[1])),
                  vec(128), vec(128), vec(2048), vec(2048)],
        out_specs=[pl.BlockSpec((ll, 2048), lambda c: (c, 0)), pl.BlockSpec((ll, 2048), lambda c: (c, 0)),
                   pl.BlockSpec((1, 16, 128, 128), lambda c: (c, 0, 0, 0))],
        out_shape=[jax.ShapeDtypeStruct((s, 2048), _MXU), jax.ShapeDtypeStruct((s, 2048), F32),
                   jax.ShapeDtypeStruct((nc, 16, 128, 128), F32)],
        scratch_shapes=[pltpu.VMEM((16, 128, 128), F32), pltpu.VMEM((128, ll), F32),
                        pltpu.VMEM((ll, 2048), F32), pltpu.VMEM((ll, 2048), F32)],
        compiler_params=_cparams(("arbitrary",)), name="ssd_fwd")(xbc, proj, proj, dtb, alog, dexp, ng)


def _ssd_bwd(xbc, proj, yraw, hprev, dyssd, dproj, dtb, alog, dexp, ng, ride=None):
    s = xbc.shape[0]
    ll = SSD_L
    nc = s // ll

    def body(xbc_ref, dt_ref, z_ref, yraw_ref, hprev_ref, dy_ref, dproj_hbm, dtb_ref, alog_ref, dexp_ref, ng_ref,
             dz_ref, ddt_ref, dxbc_ref, dng_ref, dda_ref, ddd_ref, ddtb_ref,
             dht_ref, acst_ref, dtx_ref, acx_ref, dyr_ref, rowt_ref):
        del dproj_hbm

        @pl.when(pl.program_id(0) == 0)
        def _():
            dht_ref[...] = jnp.zeros_like(dht_ref)
            dng_ref[...] = jnp.zeros_like(dng_ref)
            dda_ref[...] = jnp.zeros_like(dda_ref)
            ddd_ref[...] = jnp.zeros_like(ddd_ref)
            ddtb_ref[...] = jnp.zeros_like(ddtb_ref)
            rowt_ref[...] = jnp.zeros_like(rowt_ref)

        for g in range(4):
            sl = slice(512 * g, 512 * g + 512)
            zz = z_ref[:, sl]
            yr = yraw_ref[:, sl]
            sz = _silu(zz)
            yg = yr * sz
            r = lax.rsqrt(jnp.mean(yg * yg, axis=-1, keepdims=True) + EPS)
            yhat = yg * r
            dyv = dy_ref[:, sl]
            dng_ref[:, sl] += jnp.sum(dyv * yhat, axis=0, keepdims=True)
            dyh = dyv * ng_ref[:, sl]
            dyg = r * (dyh - yhat * jnp.mean(dyh * yhat, axis=-1, keepdims=True))
            dz_ref[:, sl] = (dyg * yr * _dsilu(zz)).astype(dz_ref.dtype)
            dyr_ref[:, sl] = dyg * sz

        dt, a_neg, a_cs, causal, ri, lane_l, lo = _ssd_common(dt_ref, dtb_ref, alog_ref, acst_ref, dtx_ref, acx_ref)
        lane_1 = lax.broadcasted_iota(jnp.int32, (1, 128), 1)
        da_col = jnp.zeros((ll, 128), F32)
        ddt_x = jnp.zeros((ll, 128), F32)
        last = jnp.zeros((1, 128), F32)
        for g in range(4):
            bg = _mx(xbc_ref[:, 2048 + 128 * g:2176 + 128 * g])
            cg = _mx(xbc_ref[:, 2560 + 128 * g:2688 + 128 * g])
            cbm = _dot_nt(cg, bg)
            dcb = jnp.zeros((ll, ll), F32)
            db_g = jnp.zeros((ll, 128), F32)
            dc_g = jnp.zeros((ll, 128), F32)
            for jj in range(4):
                j = 4 * g + jj
                sl = slice(128 * j, 128 * j + 128)
                xp = xbc_ref[:, sl]
                dtx = dtx_ref[:, sl]
                acx = acx_ref[:, sl]
                a_last = acx_ref[ll - 1:ll, sl]
                ea = jnp.exp(acx)
                dte = jnp.exp(a_last - acx)
                cd = jnp.exp(a_last)
                xdt = xp * dtx
                xdt_m = _mx(xdt)
                dy = dyr_ref[:, sl]
                ht = hprev_ref[0, j]
                dhn = dht_ref[j]
                dhn_m = _mx(dhn)
                gmat = _dot(bg, dhn_m)
                dxdt = gmat * dte
                for hh in range(2):
                    h = 2 * j + hh
                    dec = _ssd_decay(a_cs, acst_ref, h, causal, lane_l)
                    mm = dec * cbm
                    dym = _mx(jnp.where(lo if hh == 0 else jnp.logical_not(lo), dy, 0.0))
                    dxdt = dxdt + _dot_tn(_mx(mm), dym)
                    dm = _dot_nt(dym, xdt_m)
                    dcb = dcb + dm * dec
                    qq = dm * mm
                    da_col = da_col + jnp.where(lane_l == h, jnp.sum(qq, axis=1, keepdims=True), 0.0)
                    rowt_ref[h:h + 1, :] = jnp.sum(qq, axis=0, keepdims=True)
                ch = _dot(cg, _mx(ht))
                dyea = dy * ea
                dyea_m = _mx(dyea)
                xw_m = _mx(xdt * dte)
                dc_g = dc_g + _dot_nt(dyea_m, _mx(ht))
                db_g = db_g + _dot_nt(xw_m, dhn_m)
                wl = xdt * gmat * dte
                lane_a = dyea * ch - wl
                lane_b = dxdt * xp
                lane_c = jnp.sum(dhn * ht, axis=0, keepdims=True) * cd + jnp.sum(wl, axis=0, keepdims=True)
                for hh in range(2):
                    h = 2 * j + hh
                    mine = lo if hh == 0 else jnp.logical_not(lo)
                    da_col = da_col + jnp.where(
                        lane_l == h, jnp.sum(jnp.where(mine, lane_a, 0.0), axis=1, keepdims=True), 0.0)
                    ddt_x = ddt_x + jnp.where(
                        lane_l == h, jnp.sum(jnp.where(mine, lane_b, 0.0), axis=1, keepdims=True), 0.0)
                    mine_1 = (lane_1 < 64) if hh == 0 else (lane_1 >= 64)
                    last = last + jnp.where(
                        lane_1 == h, jnp.sum(jnp.where(mine_1, lane_c, 0.0), axis=1, keepdims=True), 0.0)
                dht_ref[j] = dhn * cd + _dot_tn(cg, dyea_m)
                dxbc_ref[:, sl] = dxdt * dtx + dy * dexp_ref[:, sl]
                ddd_ref[:, sl] += jnp.sum(dy * xp, axis=0, keepdims=True)
            dcb_m = _mx(dcb)
            dxbc_ref[:, 2048 + 128 * g:2176 + 128 * g] = db_g + _dot_tn(dcb_m, cg)
            dxbc_ref[:, 2560 + 128 * g:2688 + 128 * g] = dc_g + _dot(dcb_m, bg)

        da_cs = da_col - rowt_ref[...].T
        da_cs = da_cs + jnp.where(lax.broadcasted_iota(jnp.int32, (ll, 128), 0) == ll - 1, last, 0.0)
        d_dta = _dot((ri <= lax.broadcasted_iota(jnp.int32, (ll, ll), 1)).astype(F32), da_cs, _HI)
        ddt = d_dta * a_neg + ddt_x
        dda_ref[...] += jnp.sum(d_dta * dt, axis=0, keepdims=True)
        ddt_raw = ddt * _sigmoid(dt_ref[:, 0:128] + dtb_ref[...])
        ddtb_ref[...] += jnp.sum(ddt_raw, axis=0, keepdims=True)
        ddt_ref[:, 0:128] = ddt_raw.astype(ddt_ref.dtype)
        ddt_ref[:, 128:512] = jnp.zeros((ll, 384), ddt_ref.dtype)

    rev = lambda c: nc - 1 - c
    vec = lambda w: pl.BlockSpec((1, w), lambda c: (0, 0))
    row = lambda w: pl.BlockSpec((ll, w), lambda c: (rev(c), 0))
    return _pcall(
        body, ride, (xbc, proj, proj, yraw, hprev, dyssd, dproj, dtb, alog, dexp, ng), grid=(nc,),
        in_specs=[row(3072),
                  pl.BlockSpec((ll, DT[1]), lambda c: (rev(c), DT[0] // DT[1])),
                  pl.BlockSpec((ll, Z[1]), lambda c: (rev(c), Z[0] // Z[1])),
                  row(2048),
                  pl.BlockSpec((1, 16, 128, 128), lambda c: (rev(c), 0, 0, 0)),
                  row(2048),
                  pl.BlockSpec(memory_space=pl.ANY),
                  vec(128), vec(128), vec(2048), vec(2048)],
        out_specs=[pl.BlockSpec((ll, Z[1]), lambda c: (rev(c), Z[0] // Z[1])),
                   row(DT[1]),
                   row(3072), vec(2048), vec(128), vec(2048), vec(128)],
        out_shape=[jax.ShapeDtypeStruct(dproj.shape, dproj.dtype), jax.ShapeDtypeStruct((s, DT[1]), dproj.dtype),
                   jax.ShapeDtypeStruct((s, 3072), F32), jax.ShapeDtypeStruct((1, 2048), F32),
                   jax.ShapeDtypeStruct((1, 128), F32), jax.ShapeDtypeStruct((1, 2048), F32),
                   jax.ShapeDtypeStruct((1, 128), F32)],
        scratch_shapes=[pltpu.VMEM((16, 128, 128), F32), pltpu.VMEM((128, ll), F32),
                        pltpu.VMEM((ll, 2048), F32), pltpu.VMEM((ll, 2048), F32), pltpu.VMEM((ll, 2048), F32),
                        pltpu.VMEM((128, ll), F32)],
        aliases={6: 0}, sem=("arbitrary",), name="ssd_bwd")


def _put_block(src, dproj, blk, name):
    s = src.shape[0]
    off, width = blk
    cb = off // width
    ts = _tile(s, 1024)

    def body(s_ref, dproj_hbm, o_ref):
        del dproj_hbm
        o_ref[...] = s_ref[...]

    return pl.pallas_call(
        body, grid=(s // ts,),
        in_specs=[pl.BlockSpec((ts, width), lambda i: (i, 0)), pl.BlockSpec(memory_space=pl.ANY)],
        out_specs=pl.BlockSpec((ts, width), lambda i: (i, cb)),
        out_shape=jax.ShapeDtypeStruct(dproj.shape, dproj.dtype),
        input_output_aliases={1: 0},
        compiler_params=_cparams(("parallel",)), name=name)(src, dproj)


LRU_G = 384


def _lru_gates(xl_ref, wa_ref, wx_ref, ba_ref, bx_ref, lam_ref, g):
    sl = slice(LRU_G * g, LRU_G * g + LRU_G)
    xg = xl_ref[:, sl]
    xm = _mx(xg)
    r = _sigmoid(_dot(xm, wa_ref[g]) + ba_ref[:, sl])
    ig = _sigmoid(_dot(xm, wx_ref[g]) + bx_ref[:, sl])
    sp = _softplus(-lam_ref[:, sl])
    log_a = (-LRU_C * r) * sp
    a = jnp.exp(log_a)
    mult = jnp.sqrt(_one_minus_sq(log_a, a))
    return sl, xg, r, ig, sp, a, mult


def _lru_fwd(xl, proj, wa, wx, ba, bx, lam):
    s = xl.shape[0]
    ts = _tile(s, 256)
    w = LRU_W

    def body(xl_ref, lg_ref, wa_ref, wx_ref, ba_ref, bx_ref, lam_ref, y_ref, hs_ref, a_ref, u_ref, carry_ref):
        @pl.when(pl.program_id(0) == 0)
        def _():
            carry_ref[...] = jnp.zeros_like(carry_ref)

        for g in range(4):
            sl, xg, _, ig, _, a, mult = _lru_gates(xl_ref, wa_ref, wx_ref, ba_ref, bx_ref, lam_ref, g)
            a_ref[:, sl] = a
            u_ref[:, sl] = mult * (ig * xg)

        def step(t, h):
            h = a_ref[pl.ds(t, 1), :] * h + u_ref[pl.ds(t, 1), :]
            hs_ref[pl.ds(t, 1), :] = h
            return h

        carry_ref[0:1, :] = lax.fori_loop(0, ts, step, carry_ref[0:1, :], unroll=8)
        y_ref[...] = (hs_ref[...] * _silu(lg_ref[...])).astype(y_ref.dtype)

    row = pl.BlockSpec((ts, w), lambda i: (i, 0))
    vec = pl.BlockSpec((1, w), lambda i: (0, 0))
    wsp = pl.BlockSpec((4, LRU_G, LRU_G), lambda i: (0, 0, 0))
    return pl.pallas_call(
        body, grid=(s // ts,),
        in_specs=[row, pl.BlockSpec((ts, w), lambda i: (i, LG[0] // w)), wsp, wsp, vec, vec, vec],
        out_specs=[row, row],
        out_shape=[jax.ShapeDtypeStruct((s, w), _MXU), jax.ShapeDtypeStruct((s, w), F32)],
        scratch_shapes=[pltpu.VMEM((ts, w), F32), pltpu.VMEM((ts, w), F32), pltpu.VMEM((8, w), F32)],
        compiler_params=_cparams(("arbitrary",)), name="lru_fwd")(xl, proj, wa, wx, ba, bx, lam)


def _lru_bwd(xl, proj, hs, dy, dproj, wa, wx, ba, bx, lam, ride=None):
    s = xl.shape[0]
    ts = _tile(s, 256)
    nt = s // ts
    w = LRU_W
    hb = ts // 8

    def body(xl_ref, lg_ref, hs_ref, hprev_ref, dy_ref, dproj_hbm, wa_ref, wx_ref, ba_ref, bx_ref, lam_ref,
             dlg_ref, dxl_ref, dwa_ref, dwx_ref, dba_ref, dbx_ref, dlam_ref,
             a_ref, dh_ref, ext_ref, carry_ref, r_ref, ig_ref, mult_ref):
        del dproj_hbm
        i = pl.program_id(0)

        @pl.when(i == 0)
        def _():
            carry_ref[...] = jnp.zeros_like(carry_ref)
            for ref in (dwa_ref, dwx_ref, dba_ref, dbx_ref, dlam_ref):
                ref[...] = jnp.zeros_like(ref)

        lg = lg_ref[...]
        dyv = dy_ref[...]
        dh_ref[...] = dyv * _silu(lg)
        dlg_ref[...] = (dyv * hs_ref[...] * _dsilu(lg)).astype(dlg_ref.dtype)
        for g in range(4):
            sl, _, r, ig, _, a, mult = _lru_gates(xl_ref, wa_ref, wx_ref, ba_ref, bx_ref, lam_ref, g)
            a_ref[:, sl] = a
            r_ref[:, sl] = r
            ig_ref[:, sl] = ig
            mult_ref[:, sl] = mult

        def step(k, carry):
            t = ts - 1 - k
            dh = dh_ref[pl.ds(t, 1), :] + carry
            dh_ref[pl.ds(t, 1), :] = dh
            return a_ref[pl.ds(t, 1), :] * dh

        carry_ref[0:1, :] = lax.fori_loop(0, ts, step, carry_ref[0:1, :], unroll=8)

        ext_ref[0:8, :] = jnp.where(i == nt - 1, 0.0, 1.0) * hprev_ref[...]
        ext_ref[8:8 + ts, :] = hs_ref[...]
        for g in range(4):
            sl = slice(LRU_G * g, LRU_G * g + LRU_G)
            xg, r, ig, a, mult = xl_ref[:, sl], r_ref[:, sl], ig_ref[:, sl], a_ref[:, sl], mult_ref[:, sl]
            sp = _softplus(-lam_ref[:, sl])
            dh = dh_ref[:, sl]
            da = dh * ext_ref[7:7 + ts, sl]
            dmult = dh * ig * xg
            di = dh * mult * xg
            dxl = dh * mult * ig
            dlog_a = da * a - dmult * (a * a) / mult
            dlam_ref[:, sl] += jnp.sum(dlog_a * r, axis=0, keepdims=True) * (LRU_C * _sigmoid(-lam_ref[:, sl]))
            dpa = dlog_a * (-LRU_C * sp) * r * (1.0 - r)
            dpx = di * ig * (1.0 - ig)
            dba_ref[:, sl] += jnp.sum(dpa, axis=0, keepdims=True)
            dbx_ref[:, sl] += jnp.sum(dpx, axis=0, keepdims=True)
            dpa_m, dpx_m, xm = _mx(dpa), _mx(dpx), _mx(xg)
            dxl_ref[:, sl] = dxl + _dot_nt(dpa_m, wa_ref[g]) + _dot_nt(dpx_m, wx_ref[g])
            dwa_ref[g] += _dot_tn(xm, dpa_m)
            dwx_ref[g] += _dot_tn(xm, dpx_m)

    rev = lambda i: nt - 1 - i
    row = pl.BlockSpec((ts, w), lambda i: (rev(i), 0))
    vec = pl.BlockSpec((1, w), lambda i: (0, 0))
    wsp = pl.BlockSpec((4, LRU_G, LRU_G), lambda i: (0, 0, 0))
    lgs = pl.BlockSpec((ts, w), lambda i: (rev(i), LG[0] // w))
    return _pcall(
        body, ride, (xl, proj, hs, hs, dy, dproj, wa, wx, ba, bx, lam), grid=(nt,),
        in_specs=[row, lgs, row, pl.BlockSpec((8, w), lambda i: (jnp.maximum(rev(i) * hb - 1, 0), 0)), row,
                  pl.BlockSpec(memory_space=pl.ANY), wsp, wsp, vec, vec, vec],
        out_specs=[lgs, row, wsp, wsp, vec, vec, vec],
        out_shape=[jax.ShapeDtypeStruct(dproj.shape, dproj.dtype), jax.ShapeDtypeStruct((s, w), F32),
                   jax.ShapeDtypeStruct((4, LRU_G, LRU_G), F32), jax.ShapeDtypeStruct((4, LRU_G, LRU_G), F32),
                   jax.ShapeDtypeStruct((1, w), F32), jax.ShapeDtypeStruct((1, w), F32),
                   jax.ShapeDtypeStruct((1, w), F32)],
        scratch_shapes=[pltpu.VMEM((ts, w), F32), pltpu.VMEM((ts, w), F32), pltpu.VMEM((ts + 8, w), F32),
                        pltpu.VMEM((8, w), F32), pltpu.VMEM((ts, w), F32), pltpu.VMEM((ts, w), F32),
                        pltpu.VMEM((ts, w), F32)],
        aliases={5: 0}, sem=("arbitrary",), name="lru_bwd")


def _mem_kv_fwd(mem, g, wkv):
    m = mem.shape[0]

    def body(mem_ref, g_ref, w_ref, k_ref, v_ref, mn_ref):
        mv = mem_ref[...]
        r = lax.rsqrt(jnp.mean(mv * mv, axis=-1, keepdims=True) + EPS)
        mn = _mx(mv * r * g_ref[...])
        mn_ref[...] = mn
        kv = _dot(mn, w_ref[...])
        k_ref[...] = kv[:, 0:D].astype(k_ref.dtype)
        v_ref[...] = kv[:, D:2 * D].astype(v_ref.dtype)

    sh = jax.ShapeDtypeStruct((m, D), _MXU)
    return pl.pallas_call(body, out_shape=[sh, sh, sh], compiler_params=_cparams(None), name="mem_kv_fwd")(mem, g, wkv)


def _mem_kv_bwd(mem, g, mn, wkv, dk, dv):
    m = mem.shape[0]

    def body(mem_ref, g_ref, mn_ref, w_ref, dk_ref, dv_ref, dw_ref, dg_ref):
        dkv = _mx(jnp.concatenate([dk_ref[...], dv_ref[...]], axis=1))
        dw_ref[...] = _dot_tn(mn_ref[...], dkv).astype(dw_ref.dtype)
        dmn = _dot_nt(dkv, w_ref[...])
        mv = mem_ref[...]
        r = lax.rsqrt(jnp.mean(mv * mv, axis=-1, keepdims=True) + EPS)
        dg_ref[...] = jnp.sum(dmn * mv * r, axis=0, keepdims=True)

    del m
    return pl.pallas_call(
        body, out_shape=[jax.ShapeDtypeStruct((D, 2 * D), _MXU), jax.ShapeDtypeStruct((1, D), F32)],
        compiler_params=_cparams(None), name="mem_kv_bwd")(mem, g, mn, wkv, dk, dv)


def _attn_probs(q_ref, k_ref, hd):
    sl = slice(MEM_HD * hd, MEM_HD * hd + MEM_HD)
    qh = _mx(q_ref[:, sl])
    sc = _dot_nt(qh, k_ref[:, sl]) * (MEM_HD ** -0.5)
    e = jnp.exp(sc - jnp.max(sc, axis=-1, keepdims=True))
    return sl, qh, e / jnp.sum(e, axis=-1, keepdims=True)


def _attn_fwd(proj, k, v):
    s = proj.shape[0]
    m = k.shape[0]
    ts = _tile(s, 512)

    def body(q_ref, k_ref, v_ref, y_ref):
        for hd in range(MEM_HEADS):
            sl, _, p = _attn_probs(q_ref, k_ref, hd)
            y_ref[:, sl] = _dot(_mx(p), v_ref[:, sl]).astype(y_ref.dtype)

    kvs = pl.BlockSpec((m, D), lambda i: (0, 0))
    return pl.pallas_call(
        body, grid=(s // ts,),
        in_specs=[pl.BlockSpec((ts, D), lambda i: (i, Q[0] // D)), kvs, kvs],
        out_specs=pl.BlockSpec((ts, D), lambda i: (i, 0)),
        out_shape=jax.ShapeDtypeStruct((s, D), _MXU),
        compiler_params=_cparams(("parallel",)), name="attn_fwd")(proj, k, v)


def _attn_bwd(proj, k, v, dy, dproj):
    s = proj.shape[0]
    m = k.shape[0]
    ts = _tile(s, 512)

    def body(q_ref, k_ref, v_ref, dy_ref, dproj_hbm, dq_ref, dk_ref, dv_ref):
        del dproj_hbm

        @pl.when(pl.program_id(0) == 0)
        def _():
            dk_ref[...] = jnp.zeros_like(dk_ref)
            dv_ref[...] = jnp.zeros_like(dv_ref)

        for hd in range(MEM_HEADS):
            sl, qh, p = _attn_probs(q_ref, k_ref, hd)
            dyh = _mx(dy_ref[:, sl])
            dp = _dot_nt(dyh, v_ref[:, sl])
            ds = _mx(p * (dp - jnp.sum(dp * p, axis=-1, keepdims=True)) * (MEM_HD ** -0.5))
            dq_ref[:, sl] = _dot(ds, k_ref[:, sl]).astype(dq_ref.dtype)
            dk_ref[:, sl] += _dot_tn(ds, qh)
            dv_ref[:, sl] += _dot_tn(_mx(p), dyh)

    kvs = pl.BlockSpec((m, D), lambda i: (0, 0))
    qs = pl.BlockSpec((ts, D), lambda i: (i, Q[0] // D))
    return pl.pallas_call(
        body, grid=(s // ts,),
        in_specs=[qs, kvs, kvs, pl.BlockSpec((ts, D), lambda i: (i, 0)), pl.BlockSpec(memory_space=pl.ANY)],
        out_specs=[qs, kvs, kvs],
        out_shape=[jax.ShapeDtypeStruct(dproj.shape, dproj.dtype), jax.ShapeDtypeStruct((m, D), F32),
                   jax.ShapeDtypeStruct((m, D), F32)],
        input_output_aliases={4: 0},
        compiler_params=_cparams(("arbitrary",)), name="attn_bwd")(proj, k, v, dy, dproj)


def _merge_fb(x, target, yssd, ylru, ymem, proj, wbs, wbl, wbm, wo, fg):
    s = x.shape[0]
    ts = _tile(s, 256)

    def body(x_ref, t_ref, ys_ref, yl_ref, ym_ref, gl_ref, wbs_ref, wbl_ref, wbm_ref, wo_ref, fg_ref,
             dgl_ref, dx2_ref, dx2m_ref, mg_ref, db0_ref, db1_ref, db2_ref, loss_ref, dfg_ref):
        @pl.when(pl.program_id(0) == 0)
        def _():
            loss_ref[...] = jnp.zeros_like(loss_ref)
            dfg_ref[...] = jnp.zeros_like(dfg_ref)

        bs = (_dot(ys_ref[...], wbs_ref[...]), _dot(yl_ref[...], wbl_ref[...]), _dot(ym_ref[...], wbm_ref[...]))
        gates = [_sigmoid(gl_ref[:, D * n:D * n + D]) for n in range(3)]
        merged = gates[0] * bs[0] + gates[1] * bs[1] + gates[2] * bs[2]
        mg = _mx(merged)
        mg_ref[...] = mg
        x2 = x_ref[...] + _dot(mg, wo_ref[...])
        r = lax.rsqrt(jnp.mean(x2 * x2, axis=-1, keepdims=True) + EPS)
        xhat = x2 * r
        err = xhat * fg_ref[...] - t_ref[...]
        loss_ref[...] += jnp.sum(err * err, axis=0, keepdims=True) * (0.5 / D)
        dy = err * (1.0 / D)
        dfg_ref[...] += jnp.sum(dy * xhat, axis=0, keepdims=True)
        dxh = dy * fg_ref[...]
        dx2 = r * (dxh - xhat * jnp.mean(dxh * xhat, axis=-1, keepdims=True))
        dx2_ref[...] = dx2
        dx2m = _mx(dx2)
        dx2m_ref[...] = dx2m
        dmg = _dot_nt(dx2m, wo_ref[...])
        for n, db_ref in enumerate((db0_ref, db1_ref, db2_ref)):
            gt = gates[n]
            dgl_ref[:, D * n:D * n + D] = (dmg * bs[n] * gt * (1.0 - gt)).astype(dgl_ref.dtype)
            db_ref[...] = (dmg * gt).astype(db_ref.dtype)

    row = lambda w: pl.BlockSpec((ts, w), lambda i: (i, 0))
    full = lambda a: pl.BlockSpec(a.shape, lambda i: (0, 0))
    vec = pl.BlockSpec((1, D), lambda i: (0, 0))
    gls = pl.BlockSpec((ts, GL[1]), lambda i: (i, GL[0] // GL[1]))
    act = jax.ShapeDtypeStruct((s, D), _MXU)
    return pl.pallas_call(
        body, grid=(s // ts,),
        in_specs=[row(D), row(D), row(SSD_W), row(LRU_W), row(D), gls, full(wbs), full(wbl), full(wbm), full(wo), vec],
        out_specs=[gls, row(D), row(D), row(D), row(D), row(D), row(D), vec, vec],
        out_shape=[jax.ShapeDtypeStruct((s, NP), _MXU), jax.ShapeDtypeStruct((s, D), F32), act, act, act, act, act,
                   jax.ShapeDtypeStruct((1, D), F32), jax.ShapeDtypeStruct((1, D), F32)],
        compiler_params=_cparams(("arbitrary",)), name="merge_fwd_bwd")(
            x, target, yssd, ylru, ymem, proj, wbs, wbl, wbm, wo, fg)


def _adamw(w, g, m, v, name):
    rows, cols = w.shape
    tr = rows if rows <= 512 else 256
    assert rows % tr == 0

    def body(w_ref, g_ref, m_ref, v_ref, d_ref, mo_ref, vo_ref):
        gv = g_ref[...]
        mn = ADAM_B1 * m_ref[...] + (1.0 - ADAM_B1) * gv
        vn = ADAM_B2 * v_ref[...] + (1.0 - ADAM_B2) * (gv * gv)
        m_hat = mn / (1.0 - ADAM_B1 ** ADAM_STEP)
        v_hat = vn / (1.0 - ADAM_B2 ** ADAM_STEP)
        d_ref[...] = -ADAM_LR * (m_hat / (jnp.sqrt(v_hat) + ADAM_EPS) + ADAM_WD * w_ref[...])
        mo_ref[...] = mn
        vo_ref[...] = vn

    blk = pl.BlockSpec((tr, cols), lambda i: (i, 0))
    sh = jax.ShapeDtypeStruct((rows, cols), F32)
    return pl.pallas_call(
        body, grid=(rows // tr,), in_specs=[blk] * 4, out_specs=[blk] * 3, out_shape=[sh] * 3,
        compiler_params=_cparams(("parallel",)), name=name)(w, g, m, v)


def _mesh_pos():
    x, y, c = lax.axis_index("x"), lax.axis_index("y"), lax.axis_index("c")
    chips = [(1 - x, y), (x, 1 - y), (1 - x, 1 - y)]
    return x, y, c, 2 * x + y, chips


def _hbm():
    return pl.BlockSpec(memory_space=pl.ANY)


def _remote(src, dst, send_sem, recv_sem, dev):
    return pltpu.make_async_remote_copy(src_ref=src, dst_ref=dst, send_sem=send_sem, recv_sem=recv_sem,
                                        device_id=dev, device_id_type=MESH)


def _sems(n):
    return [pltpu.SemaphoreType.DMA((n,)), pltpu.SemaphoreType.DMA((n,))]


class _Exchange:
    def __init__(self, inputs, out_shape, n_sem, start, finish, aliases=None):
        self.inputs, self.out_shape, self.n_sem = list(inputs), list(out_shape), n_sem
        self.start, self.finish, self.aliases = start, finish, dict(aliases or {})


def _run_exchange(ex, name):
    n_in, n_out = len(ex.inputs), len(ex.out_shape)

    def body(*refs):
        srcs, outs = refs[:n_in], refs[n_in:n_in + n_out]
        send_sems, recv_sems = refs[n_in + n_out:]
        ex.start(srcs, outs, send_sems, recv_sems)
        ex.finish(srcs, outs, send_sems, recv_sems)

    return pl.pallas_call(
        body, in_specs=[_hbm()] * n_in, out_specs=[_hbm()] * n_out, out_shape=ex.out_shape,
        input_output_aliases=ex.aliases, scratch_shapes=_sems(ex.n_sem), name=name)(*ex.inputs)


def _pcall(body, ride, args, *, grid, in_specs, out_specs, out_shape, scratch_shapes, sem, name, aliases=None):
    in_specs, out_specs, out_shape = list(in_specs), list(out_specs), list(out_shape)
    scratch_shapes, aliases = list(scratch_shapes), dict(aliases or {})
    if ride is None:
        outs = pl.pallas_call(
            body, grid=grid, in_specs=in_specs, out_specs=out_specs, out_shape=out_shape, scratch_shapes=scratch_shapes,
            input_output_aliases=aliases, compiler_params=_cparams(sem), name=name)(*args)
        return outs, None
    n_in, n_out, n_scr = len(in_specs), len(out_shape), len(scratch_shapes)
    e_in, e_out = len(ride.inputs), len(ride.out_shape)

    def carried(*refs):
        cut = [n_in, e_in, n_out, e_out, n_scr]
        parts, p = [], 0
        for c in cut:
            parts.append(refs[p:p + c])
            p += c
        ins, e_ins, outs, e_outs, scr = parts
        send_sems, recv_sems = refs[p], refs[p + 1]
        first = last = None
        for d, size in enumerate(grid):
            i = pl.program_id(d)
            first = (i == 0) if first is None else jnp.logical_and(first, i == 0)
            last = (i == size - 1) if last is None else jnp.logical_and(last, i == size - 1)

        @pl.when(first)
        def _():
            ride.start(e_ins, e_outs, send_sems, recv_sems)

        body(*ins, *outs, *scr)

        @pl.when(last)
        def _():
            ride.finish(e_ins, e_outs, send_sems, recv_sems)

    for k, v in ride.aliases.items():
        aliases[n_in + k] = n_out + v
    res = pl.pallas_call(
        carried, grid=grid, in_specs=in_specs + [_hbm()] * e_in, out_specs=out_specs + [_hbm()] * e_out,
        out_shape=out_shape + ride.out_shape, scratch_shapes=scratch_shapes + _sems(ride.n_sem),
        input_output_aliases=aliases, compiler_params=_cparams(("arbitrary",) * len(grid)),
        name=name)(*args, *ride.inputs)
    return res[:n_out], res[n_out:]


def _gather_shards(arrs, split):
    n = len(arrs)
    n_sem = sum(6 if sp else 3 for sp in split)

    def rows(i, which):
        if not split[i]:
            return pl.ds(0, arrs[i].shape[0])
        half = arrs[i].shape[0] // 2
        return pl.ds(which * half, half)

    def sends(srcs, outs, send_sems, recv_sems):
        _, _, c, me, chips = _mesh_pos()
        return [(_remote(srcs[i].at[rows(i, c)], outs[i].at[me, rows(i, c)], send_sems.at[3 * i + j],
                         recv_sems.at[3 * i + j], (cx, cy, c)), i, j)
                for i in range(n) for j, (cx, cy) in enumerate(chips)]

    def start(srcs, outs, send_sems, recv_sems):
        for cp, _, _ in sends(srcs, outs, send_sems, recv_sems):
            cp.start()

    def finish(srcs, outs, send_sems, recv_sems):
        x, y, c, _, chips = _mesh_pos()
        sib = (x, y, 1 - c)
        passed, k = [], 3 * n
        for i, j in [(i, j) for i in range(n) for j in range(3)]:
            cx, cy = chips[j]
            slot = outs[i].at[2 * cx + cy, rows(i, c)]
            _remote(slot, slot, send_sems.at[3 * i + j], recv_sems.at[3 * i + j], (cx, cy, c)).wait_recv()
            if split[i]:
                fwd = _remote(slot, slot, send_sems.at[k], recv_sems.at[k], sib)
                fwd.start()
                passed.append((fwd, i, j, k))
                k += 1
        for _, i, j, kf in passed:
            cx, cy = chips[j]
            slot = outs[i].at[2 * cx + cy, rows(i, 1 - c)]
            _remote(slot, slot, send_sems.at[kf], recv_sems.at[kf], sib).wait_recv()
        for cp in [s[0] for s in sends(srcs, outs, send_sems, recv_sems)] + [p[0] for p in passed]:
            cp.wait_send()

    return _Exchange(arrs, [jax.ShapeDtypeStruct((NSHARD,) + a.shape, a.dtype) for a in arrs], n_sem, start, finish)


def _with_own_slot(arrs, got):
    own_slot = jnp.arange(NSHARD, dtype=jnp.int32)[:, None, None] == 2 * lax.axis_index("x") + lax.axis_index("y")
    return [jnp.where(own_slot, a[None], g) for a, g in zip(arrs, got)]


def _swap_halves(arrs):
    n = len(arrs)

    def copies(srcs, outs, send_sems, recv_sems):
        x, y, c, _, _ = _mesh_pos()
        cps = []
        for i in range(n):
            half = arrs[i].shape[1] // 2
            cps.append(_remote(srcs[i].at[:, pl.ds((1 - c) * half, half)], outs[i], send_sems.at[i], recv_sems.at[i],
                               (x, y, 1 - c)))
        return cps

    def start(*refs):
        for cp in copies(*refs):
            cp.start()

    def finish(*refs):
        for cp in copies(*refs):
            cp.wait()

    shapes = [jax.ShapeDtypeStruct((NSHARD, a.shape[1] // 2, a.shape[2]), a.dtype) for a in arrs]
    return _Exchange(arrs, shapes, n, start, finish)


def _scatter_chips(arrs):
    n = len(arrs)

    def copies(srcs, outs, send_sems, recv_sems):
        _, _, c, me, chips = _mesh_pos()
        own = [pltpu.make_async_copy(srcs[i].at[me], outs[i].at[me], send_sems.at[3 * n + i]) for i in range(n)]
        cps = [_remote(srcs[i].at[2 * cx + cy], outs[i].at[me], send_sems.at[3 * i + j], recv_sems.at[3 * i + j],
                       (cx, cy, c)) for i in range(n) for j, (cx, cy) in enumerate(chips)]
        return own, cps

    def start(*refs):
        own, cps = copies(*refs)
        for cp in own + cps:
            cp.start()

    def finish(srcs, outs, send_sems, recv_sems):
        _, _, c, _, chips = _mesh_pos()
        for i in range(n):
            for j, (cx, cy) in enumerate(chips):
                slot = outs[i].at[2 * cx + cy]
                _remote(slot, slot, send_sems.at[3 * i + j], recv_sems.at[3 * i + j], (cx, cy, c)).wait_recv()
        own, cps = copies(srcs, outs, send_sems, recv_sems)
        for cp in cps:
            cp.wait_send()
        for cp in own:
            cp.wait()

    return _Exchange(arrs, [jax.ShapeDtypeStruct(a.shape, a.dtype) for a in arrs], 4 * n, start, finish)


def _share_halves(arrs):
    n = len(arrs)

    def copies(outs, send_sems, recv_sems):
        x, y, c, _, _ = _mesh_pos()
        return [_remote(outs[i].at[c], outs[i].at[c], send_sems.at[i], recv_sems.at[i], (x, y, 1 - c))
                for i in range(n)]

    def start(srcs, outs, send_sems, recv_sems):
        del srcs
        for cp in copies(outs, send_sems, recv_sems):
            cp.start()

    def finish(srcs, outs, send_sems, recv_sems):
        del srcs
        x, y, c, _, _ = _mesh_pos()
        for i in range(n):
            theirs = outs[i].at[1 - c]
            _remote(theirs, theirs, send_sems.at[i], recv_sems.at[i], (x, y, 1 - c)).wait_recv()
        for cp in copies(outs, send_sems, recv_sems):
            cp.wait_send()

    return _Exchange(arrs, [jax.ShapeDtypeStruct(a.shape, a.dtype) for a in arrs], n, start, finish,
                     aliases={i: i for i in range(n)})


def _gather_small(full):
    _, width = full.shape

    def copies(srcs, outs, send_sems, recv_sems):
        _, _, c, me, chips = _mesh_pos()
        mine = srcs[0].at[pl.ds(0, SMALL_ROWS)]
        own = pltpu.make_async_copy(mine, outs[0].at[me], send_sems.at[3])
        return own, [_remote(mine, outs[0].at[me], send_sems.at[j], recv_sems.at[j], (cx, cy, c))
                     for j, (cx, cy) in enumerate(chips)]

    def start(*refs):
        own, cps = copies(*refs)
        for cp in [own] + cps:
            cp.start()

    def finish(srcs, outs, send_sems, recv_sems):
        _, _, c, _, chips = _mesh_pos()
        for j, (cx, cy) in enumerate(chips):
            slot = outs[0].at[2 * cx + cy]
            _remote(slot, slot, send_sems.at[j], recv_sems.at[j], (cx, cy, c)).wait_recv()
        own, cps = copies(srcs, outs, send_sems, recv_sems)
        for cp in cps:
            cp.wait_send()
        own.wait()

    return _Exchange([full], [jax.ShapeDtypeStruct((NSHARD, SMALL_ROWS, width), full.dtype)], 4, start, finish)


def _add_sibling(mine, recv, c, name):
    _, half, width = recv.shape
    tr = _tile(half, 256, 8)
    nb = half // tr

    def body(c_ref, a_ref, b_ref, o_ref):
        del c_ref
        o_ref[...] = (a_ref[...].astype(F32) + b_ref[...].astype(F32)).astype(o_ref.dtype)

    grid_spec = pltpu.PrefetchScalarGridSpec(
        num_scalar_prefetch=1, grid=(NSHARD, nb),
        in_specs=[pl.BlockSpec((1, tr, width), lambda j, r, c_ref: (j, c_ref[0] * nb + r, 0)),
                  pl.BlockSpec((1, tr, width), lambda j, r, c_ref: (j, r, 0))],
        out_specs=pl.BlockSpec((1, tr, width), lambda j, r, c_ref: (j, r, 0)))
    return pl.pallas_call(
        body, grid_spec=grid_spec, out_shape=jax.ShapeDtypeStruct(recv.shape, recv.dtype),
        compiler_params=_cparams(("parallel", "parallel")), name=name)(c, mine, recv)


def _sum_chips(parts, c, name):
    _, half, width = parts.shape
    tr = _tile(half, 256, 8)

    def body(c_ref, p_ref, o_ref):
        del c_ref
        p = [p_ref[j].astype(F32) for j in range(NSHARD)]
        o_ref[0] = ((p[0] + p[1]) + p[2]) + p[3]

    grid_spec = pltpu.PrefetchScalarGridSpec(
        num_scalar_prefetch=1, grid=(half // tr,),
        in_specs=[pl.BlockSpec((NSHARD, tr, width), lambda r, c_ref: (0, r, 0))],
        out_specs=pl.BlockSpec((1, tr, width), lambda r, c_ref: (c_ref[0], r, 0)))
    return pl.pallas_call(
        body, grid_spec=grid_spec, out_shape=jax.ShapeDtypeStruct((2, half, width), F32),
        compiler_params=_cparams(("parallel",)), name=name)(c, parts)


def _unpack(flat, names, shapes):
    out, off = {}, 0
    for n in names:
        sz = _size(shapes[n])
        out[n] = flat[off:off + sz].reshape(shapes[n])
        off += sz
    return out


def _reorder_w_in(w):
    return jnp.concatenate([w[:, 2048:5120], w[:, 9248:12320], w[:, 0:2048], w[:, 8224:9248], w[:, 5152:6688],
                            w[:, 6688:8224], w[:, 5120:5152], jnp.zeros((D, NP - 12320), w.dtype)], axis=1)


def _restore_w_in(g):
    return jnp.concatenate([g[:, 6144:8192], g[:, 0:3072], g[:, 12288:12320], g[:, 9216:10752], g[:, 10752:12288],
                            g[:, 8192:9216], g[:, 3072:6144]], axis=1)


def _lru_group_weights(w):
    w4 = w.reshape(4, 4, 96, 96)
    eye = jnp.eye(4, dtype=w.dtype)
    return (w4[:, :, None, :, :] * eye[None, :, :, None, None]).transpose(0, 1, 3, 2, 4).reshape(4, LRU_G, LRU_G)


def _lru_group_blocks(g):
    g5 = g.reshape(4, 4, 96, 4, 96)
    return jnp.stack([g5[:, a, :, a, :] for a in range(4)], axis=1).reshape(16, 96, 96)


def _spread(a):
    return a.transpose(1, 0, 2).reshape(a.shape[1], NSHARD * a.shape[2])


def _split(a):
    return a.reshape(a.shape[0], NSHARD, a.shape[1] // NSHARD).transpose(1, 0, 2)


class _Reduction:
    def __init__(self, dist, parts, names):
        self.c, self.parts, self.names = dist.c, parts, names

    def swap(self):
        return _swap_halves(self.parts)

    def scatter(self, recv):
        return _scatter_chips([_add_sibling(p, r, self.c, "add_sibling_" + n)
                               for p, r, n in zip(self.parts, recv, self.names)])

    def share(self, landed):
        return _share_halves([_sum_chips(a, self.c, "sum_chips_" + n) for a, n in zip(landed, self.names)])

    def done(self, shared):
        return [a.reshape(2 * a.shape[1], a.shape[2]) for a in shared]


class _Dist:
    def __init__(self, late_shards):
        self.c = lax.axis_index("c").astype(jnp.int32).reshape(1)
        self.late_shards = late_shards

    def weights_ride(self):
        return _gather_shards(self.late_shards, [True, True, False])

    def weights_arrived(self, got):
        g_kv, g_rows, g_small = _with_own_slot(self.late_shards, got)
        out = {"w_kv": _spread(g_kv)}
        for n, lo_, hi_ in ROW_PIECES:
            out[n] = g_rows[:, lo_:hi_].reshape(NSHARD * (hi_ - lo_), D)
        out["ssd_conv_w"] = _spread(g_small[:, :, 0:768])
        out["ssd_norm_g"] = _spread(g_small[:, :, 768:896])
        out["lru_conv_w"] = _spread(g_small[:, :, 896:1280])
        return out

    def early_parts(self, grads):
        rows = jnp.concatenate([grads[n].reshape(NSHARD, hi_ - lo_, D) for n, lo_, hi_ in ROW_PIECES], axis=1)
        return [_split(grads["w_kv"]), rows]

    def late_parts(self, grads):
        return [_split(_restore_w_in(grads["w_in_r"]))]


def _local_grads(x, mem, target, wts, dist=None):
    pad128 = lambda a: jnp.pad(a, ((0, 0), (0, 128 - a.shape[1])))
    w_in_r = wts["w_in_r"]

    h = _norm_fwd(x, wts["norm_g"])
    if dist is None:
        proj = _mm(h, w_in_r, F32, "in_proj")
    else:
        proj, arrived = _mm(h, w_in_r, F32, "in_proj", ride=dist.weights_ride())
        wts = dict(wts, **dist.weights_arrived(arrived))
    wbs, wbl, wbm, wo, wkv = wts["w_br_ssd"], wts["w_br_lru"], wts["w_br_mem"], wts["w_out"], wts["w_kv"]
    wa, wx = _mx(_lru_group_weights(wts["lru_w_a"])), _mx(_lru_group_weights(wts["lru_w_x"]))
    ba, bx = wts["lru_b_a"].reshape(1, LRU_W), wts["lru_b_x"].reshape(1, LRU_W)
    dtb, alog = pad128(wts["ssd_dt_bias"]), pad128(wts["ssd_a_log"])
    dexp = jnp.repeat(wts["ssd_d"], 64, axis=1)
    ng = wts["ssd_norm_g"].reshape(1, SSD_W)
    xbc = _conv_fwd(proj, XBC, wts["ssd_conv_w"], wts["ssd_conv_b"], True, "ssd_conv_fwd")
    yssd, yraw, hprev = _ssd_fwd(xbc, proj, dtb, alog, dexp, ng)
    xl = _conv_fwd(proj, LX, wts["lru_conv_w"], wts["lru_conv_b"], False, "lru_conv_fwd")
    ylru, hs = _lru_fwd(xl, proj, wa, wx, ba, bx, wts["lru_lambda"])
    kk, vv, mn = _mem_kv_fwd(mem, wts["mem_norm_g"], wkv)
    ymem = _attn_fwd(proj, kk, vv)

    dproj, dx2, dx2m, merged, db0, db1, db2, loss_vec, dfg = _merge_fb(
        x, target, yssd, ylru, ymem, proj, wbs, wbl, wbm, wo, wts["final_g"].reshape(1, D))
    grads = {"final_g": dfg.reshape(D)}
    grads["w_out"] = _mm(merged, dx2m, _MXU, "dw_out", ta=True)
    grads["w_br_ssd"] = _mm(yssd, db0, _MXU, "dw_br_ssd", ta=True)
    grads["w_br_lru"] = _mm(ylru, db1, _MXU, "dw_br_lru", ta=True)
    grads["w_br_mem"] = _mm(ymem, db2, _MXU, "dw_br_mem", ta=True)
    dyssd = _mm(db0, wbs, F32, "dy_ssd", tb=True)
    dylru = _mm(db1, wbl, F32, "dy_lru", tb=True)
    dymem = _mm(db2, wbm, F32, "dy_mem", tb=True)

    dproj, dk, dv = _attn_bwd(proj, kk, vv, dymem, dproj)
    grads["w_kv"], grads["mem_norm_g"] = _mem_kv_bwd(mem, wts["mem_norm_g"], mn, wkv, dk, dv)

    early = None if dist is None else _Reduction(dist, dist.early_parts(grads), ["w_kv", "rows"])

    (dproj, dxl, dwa, dwx, dba, dbx, dlam), got = _lru_bwd(
        xl, proj, hs, dylru, dproj, wa, wx, ba, bx, wts["lru_lambda"], ride=early and early.swap())
    grads["lru_w_a"] = _lru_group_blocks(dwa)[None]
    grads["lru_w_x"] = _lru_group_blocks(dwx)[None]
    grads["lru_b_a"], grads["lru_b_x"] = dba.reshape(1, 16, 96), dbx.reshape(1, 16, 96)
    grads["lru_lambda"] = dlam
    (grads["lru_conv_w"], grads["lru_conv_b"]), _ = _conv_bwd_w(
        proj, LX, wts["lru_conv_w"], wts["lru_conv_b"], dxl, False, "lru_conv_bwd_w")
    dproj = _conv_bwd_x(dxl, wts["lru_conv_w"], dproj, LX, "lru_conv_bwd_x")

    (dproj, ddt, dxbc, dng, dda, ddd, ddtb), got = _ssd_bwd(
        xbc, proj, yraw, hprev, dyssd, dproj, dtb, alog, dexp, ng, ride=early and early.scatter(got))
    dproj = _put_block(ddt, dproj, DT, "put_ddt")
    grads["ssd_norm_g"] = dng.reshape(4, 512)
    grads["ssd_dt_bias"] = ddtb[:, 0:32]
    grads["ssd_a_log"] = (dda * -jnp.exp(alog))[:, 0:32]
    grads["ssd_d"] = ddd.reshape(32, 64).sum(axis=1)[None, :]
    (dpre, grads["ssd_conv_w"], grads["ssd_conv_b"]), got = _conv_bwd_w(
        proj, XBC, wts["ssd_conv_w"], wts["ssd_conv_b"], dxbc, True, "ssd_conv_bwd_w", ride=early and early.share(got))
    reduced = {} if dist is None else dict(zip(["w_kv", "rows"], early.done(got)))
    dproj = _conv_bwd_x(dpre, wts["ssd_conv_w"], dproj, XBC, "ssd_conv_bwd_x")

    grads["w_in_r"] = _mm(h, dproj, _MXU, "dw_in", ta=True)
    if dist is None:
        dh = _mm(dproj, w_in_r, F32, "dh", tb=True, tn=1024, tk=1280)
        (grad_x, grads["norm_g"]), _ = _norm_bwd(x, wts["norm_g"], dh, dx2)
    else:
        late = _Reduction(dist, dist.late_parts(grads), ["w_in"])
        got = _run_exchange(late.swap(), "swap_halves_w_in")
        dh, got = _mm(dproj, w_in_r, F32, "dh", tb=True, tn=1024, tk=1280, ride=late.scatter(got))
        (grad_x, grads["norm_g"]), got = _norm_bwd(x, wts["norm_g"], dh, dx2, ride=late.share(got))
        reduced["w_in"] = late.done(got)[0]
    return jnp.sum(loss_vec), grad_x, grads, reduced


def kernel(x, mem, norm_g, w_in, ssd_conv_w, ssd_conv_b, ssd_dt_bias, ssd_a_log, ssd_d, ssd_norm_g, lru_conv_w, lru_conv_b, lru_w_a, lru_b_a, lru_w_x, lru_b_x, lru_lambda, mem_norm_g, w_kv, w_br_ssd, w_br_lru, w_br_mem, w_out, final_g, loss_target, m_norm_g, m_w_in, m_ssd_conv_w, m_ssd_conv_b, m_ssd_dt_bias, m_ssd_a_log, m_ssd_d, m_ssd_norm_g, m_lru_conv_w, m_lru_conv_b, m_lru_w_a, m_lru_b_a, m_lru_w_x, m_lru_b_x, m_lru_lambda, m_mem_norm_g, m_w_kv, m_w_br_ssd, m_w_br_lru, m_w_br_mem, m_w_out, m_final_g, v_norm_g, v_w_in, v_ssd_conv_w, v_ssd_conv_b, v_ssd_dt_bias, v_ssd_a_log, v_ssd_d, v_ssd_norm_g, v_lru_conv_w, v_lru_conv_b, v_lru_w_a, v_lru_b_a, v_lru_w_x, v_lru_b_x, v_lru_lambda, v_mem_norm_g, v_w_kv, v_w_br_ssd, v_w_br_lru, v_w_br_mem, v_w_out, v_final_g):
    given = dict(locals())

    rows_w = jnp.concatenate([w_br_ssd[0], w_br_lru[0], w_br_mem[0], w_out[0]], axis=0)
    small_w = jnp.concatenate([ssd_conv_w[0], ssd_norm_g[0], lru_conv_w[0]], axis=1)
    dist = _Dist([_mx(w_kv[0]), _mx(rows_w), small_w])
    own_in = [_mx(w_in[0])]
    (g_in,) = _with_own_slot(own_in, _run_exchange(_gather_shards(own_in, [True]), "gather_w_in"))
    wts = {n: given[n] for n in REPL}
    wts["lru_w_a"], wts["lru_w_x"] = lru_w_a[0], lru_w_x[0]
    wts["w_in_r"] = _reorder_w_in(_spread(g_in))

    loss_part, grad_x, grads, reduced = _local_grads(x[0], mem[0], loss_target[0], wts, dist)
    loss = lax.psum(loss_part, ("x", "y", "c"))

    repl_flat = jnp.concatenate([grads[n].reshape(-1) for n in REPL])
    repl_flat = jnp.pad(repl_flat, (0, NSHARD * SMALL_Q - repl_flat.shape[0])).reshape(NSHARD, SMALL_Q)
    shard_small = jnp.concatenate([_split(grads[n]).reshape(NSHARD, -1) for n in SMALL_SHARDED], axis=1)
    p_small = jnp.concatenate(
        [repl_flat, shard_small, jnp.zeros((NSHARD, SMALL_BUF_ROWS * PACK_W - SMALL_Q - 5120), F32)], axis=1)
    small = _Reduction(dist, [p_small.reshape(NSHARD, SMALL_BUF_ROWS, PACK_W)], ["small"])
    got = _run_exchange(small.swap(), "swap_halves_small")
    got = _run_exchange(small.scatter(got), "scatter_chips_small")
    got = _run_exchange(small.share(got), "share_halves_small")
    r_small = small.done(got)[0]
    repl_all = _run_exchange(_gather_small(r_small), "gather_small")[0].reshape(-1)

    g_shard = {"w_in": reduced["w_in"], "w_kv": reduced["w_kv"]}
    for n, lo_, hi_ in ROW_PIECES:
        g_shard[n] = reduced["rows"][lo_:hi_]
    g_shard.update(_unpack(r_small.reshape(-1)[SMALL_Q:], SMALL_SHARDED, SHARD_SHAPE))
    g_repl = _unpack(repl_all, REPL, REPL_SHAPE)

    out_g, out_d, out_m, out_v = {}, {}, {}, {}
    for n in WEIGHTS:
        w_full = given[n]
        g = (g_shard[n] if n in SHARDED else g_repl[n]).reshape(w_full.shape)
        cols = w_full.shape[-1]
        as2d = lambda a: a.reshape(-1, cols)
        d, mo, vo = _adamw(as2d(w_full), as2d(g), as2d(given["m_" + n]), as2d(given["v_" + n]), "adamw_" + n)
        out_g[n] = g
        out_d[n], out_m[n], out_v[n] = d.reshape(w_full.shape), mo.reshape(w_full.shape), vo.reshape(w_full.shape)

    return (loss, grad_x[None], *[out_g[n] for n in WEIGHTS], *[out_d[n] for n in WEIGHTS],
            *[out_m[n] for n in WEIGHTS], *[out_v[n] for n in WEIGHTS])
```

```python
import jax
import jax.numpy as jnp
from jax import lax
from jax.experimental import pallas as pl
from jax.experimental.pallas import tpu as pltpu

F32 = jnp.float32
_MXU = jnp.bfloat16
_HI = lax.Precision.HIGHEST
MESH = pl.DeviceIdType.MESH

D = 1024
EPS = 1e-6
MEM_HEADS = 4
MEM_HD = 256
LRU_C = 8.0
SSD_L = 128
SSD_W = 2048
LRU_W = 1536
NSHARD = 4

XBC = (0, 3072)
GL = (3072, 3072)
Z = (6144, 2048)
Q = (8192, 1024)
LG = (9216, 1536)
LX = (10752, 1536)
DT = (12288, 256)
NP = 12544
NP_TILE = 1792

ADAM_LR = 0.001
ADAM_B1 = 0.9
ADAM_B2 = 0.999
ADAM_EPS = 1e-08
ADAM_WD = 0.01
ADAM_STEP = 10

VMEM_LIMIT = 56 * 1024 * 1024

SHARDED = ("w_in", "ssd_conv_w", "ssd_norm_g", "lru_conv_w", "w_kv", "w_br_ssd", "w_br_lru", "w_br_mem", "w_out")
SHARD_SHAPE = {"w_in": (1024, 3080), "ssd_conv_w": (4, 768), "ssd_norm_g": (4, 128), "lru_conv_w": (4, 384),
               "w_kv": (1024, 512), "w_br_ssd": (512, 1024), "w_br_lru": (384, 1024), "w_br_mem": (256, 1024),
               "w_out": (256, 1024)}
REPL = ("norm_g", "ssd_conv_b", "ssd_dt_bias", "ssd_a_log", "ssd_d", "lru_conv_b", "lru_w_a", "lru_b_a",
        "lru_w_x", "lru_b_x", "lru_lambda", "mem_norm_g", "final_g")
REPL_SHAPE = {"norm_g": (1, 1024), "ssd_conv_b": (1, 3072), "ssd_dt_bias": (1, 32), "ssd_a_log": (1, 32),
              "ssd_d": (1, 32), "lru_conv_b": (1, 1536), "lru_w_a": (1, 16, 96, 96), "lru_b_a": (1, 16, 96),
              "lru_w_x": (1, 16, 96, 96), "lru_b_x": (1, 16, 96), "lru_lambda": (1, 1536),
              "mem_norm_g": (1, 1024), "final_g": (1024,)}
WEIGHTS = ("norm_g", "w_in", "ssd_conv_w", "ssd_conv_b", "ssd_dt_bias", "ssd_a_log", "ssd_d", "ssd_norm_g",
           "lru_conv_w", "lru_conv_b", "lru_w_a", "lru_b_a", "lru_w_x", "lru_b_x", "lru_lambda", "mem_norm_g",
           "w_kv", "w_br_ssd", "w_br_lru", "w_br_mem", "w_out", "final_g")

ROW_PIECES = (("w_br_ssd", 0, 512), ("w_br_lru", 512, 896), ("w_br_mem", 896, 1152), ("w_out", 1152, 1408))
SMALL_SHARDED = ("ssd_conv_w", "ssd_norm_g", "lru_conv_w")
PACK_W = 512
SMALL_ROWS = 152
SMALL_Q = SMALL_ROWS * PACK_W
SMALL_BUF_ROWS = 176


def _size(shape):
    n = 1
    for s in shape:
        n *= s
    return n


def _sigmoid(x):
    return 0.5 * jnp.tanh(0.5 * x) + 0.5


def _silu(x):
    return x * _sigmoid(x)


def _dsilu(x):
    s = _sigmoid(x)
    return s * (1.0 + x * (1.0 - s))


def _softplus(x):
    return jnp.maximum(x, 0.0) + jnp.log(1.0 + jnp.exp(-jnp.abs(x)))


def _one_minus_sq(log_a, a):
    x = 2.0 * log_a
    series = -x * (1.0 + x * (0.5 + x * (1.0 / 6.0 + x * (1.0 / 24.0))))
    return jnp.where(x > -0.03, series, 1.0 - a * a)


def _dot(a, b, precision=None):
    return jnp.dot(a, b, preferred_element_type=F32, precision=precision)


def _dot_nt(a, b):
    return lax.dot_general(a, b, (((1,), (1,)), ((), ())), preferred_element_type=F32)


def _dot_tn(a, b):
    return lax.dot_general(a, b, (((0,), (0,)), ((), ())), preferred_element_type=F32)


def _mx(a):
    return a.astype(_MXU)


def _cparams(sem):
    return pltpu.CompilerParams(dimension_semantics=sem, vmem_limit_bytes=VMEM_LIMIT)


def _tile(n, want, mult=128):
    if n <= want:
        return n
    for t in range(want - want % mult, 0, -mult):
        if n % t == 0:
            return t
    raise ValueError((n, want, mult))


def _mm(a, b, out_dtype, name, ta=False, tb=False, tm=1024, tn=1280, tk=1024, ride=None):
    k, m = a.shape if ta else a.shape[::-1]
    k2, n = b.shape[::-1] if tb else b.shape
    assert k == k2
    tm, tn, tk = _tile(m, tm), _tile(n, tn), _tile(k, tk)
    nk = k // tk
    contract = (((0 if ta else 1,), (1 if tb else 0,)), ((), ()))

    def body(a_ref, b_ref, o_ref, acc_ref):
        kk = pl.program_id(2)

        @pl.when(kk == 0)
        def _():
            acc_ref[...] = jnp.zeros_like(acc_ref)

        acc_ref[...] += lax.dot_general(a_ref[...], b_ref[...], contract, preferred_element_type=F32)

        @pl.when(kk == nk - 1)
        def _():
            o_ref[...] = acc_ref[...].astype(o_ref.dtype)

    a_spec = pl.BlockSpec((tk, tm), lambda i, j, kk: (kk, i)) if ta else pl.BlockSpec((tm, tk), lambda i, j, kk: (i, kk))
    b_spec = pl.BlockSpec((tn, tk), lambda i, j, kk: (j, kk)) if tb else pl.BlockSpec((tk, tn), lambda i, j, kk: (kk, j))
    outs, carried = _pcall(
        body, ride, (a, b), grid=(m // tm, n // tn, nk),
        in_specs=[a_spec, b_spec],
        out_specs=[pl.BlockSpec((tm, tn), lambda i, j, kk: (i, j))],
        out_shape=[jax.ShapeDtypeStruct((m, n), out_dtype)],
        scratch_shapes=[pltpu.VMEM((tm, tn), F32)],
        sem=("parallel", "parallel", "arbitrary"), name=name)
    return outs[0] if ride is None else (outs[0], carried)


def _norm_fwd(x, g, ride=None):
    s = x.shape[0]
    ts = _tile(s, 512)

    def body(x_ref, g_ref, h_ref):
        xv = x_ref[...]
        r = lax.rsqrt(jnp.mean(xv * xv, axis=-1, keepdims=True) + EPS)
        h_ref[...] = (xv * r * g_ref[...]).astype(h_ref.dtype)

    return _pcall(
        body, ride, (x, g), grid=(s // ts,),
        in_specs=[pl.BlockSpec((ts, D), lambda i: (i, 0)), pl.BlockSpec((1, D), lambda i: (0, 0))],
        out_specs=[pl.BlockSpec((ts, D), lambda i: (i, 0))],
        out_shape=[jax.ShapeDtypeStruct((s, D), _MXU)], scratch_shapes=[], sem=("parallel",), name="norm_fwd")


def _norm_bwd(x, g, dh, dx2, ride=None):
    s = x.shape[0]
    ts = _tile(s, 512)

    def body(x_ref, g_ref, dh_ref, dx2_ref, gx_ref, dg_ref):
        @pl.when(pl.program_id(0) == 0)
        def _():
            dg_ref[...] = jnp.zeros_like(dg_ref)

        xv = x_ref[...]
        r = lax.rsqrt(jnp.mean(xv * xv, axis=-1, keepdims=True) + EPS)
        xhat = xv * r
        dh_v = dh_ref[...]
        dg_ref[...] += jnp.sum(dh_v * xhat, axis=0, keepdims=True)
        dxh = dh_v * g_ref[...]
        gx_ref[...] = dx2_ref[...] + r * (dxh - xhat * jnp.mean(dxh * xhat, axis=-1, keepdims=True))

    row = pl.BlockSpec((ts, D), lambda i: (i, 0))
    vec = pl.BlockSpec((1, D), lambda i: (0, 0))
    return _pcall(
        body, ride, (x, g, dh, dx2), grid=(s // ts,), in_specs=[row, vec, row, row], out_specs=[row, vec],
        out_shape=[jax.ShapeDtypeStruct((s, D), F32), jax.ShapeDtypeStruct((1, D), F32)],
        scratch_shapes=[], sem=("arbitrary",), name="norm_bwd")


CONV_RB = 16
CONV_LC = 256


def _fold8(v):
    acc = v[0:8]
    for r0 in range(8, v.shape[0], 8):
        acc = acc + v[r0:r0 + 8]
    return acc


def _conv_fwd(src, blk, w, b, act, name):
    s = src.shape[0]
    off, width = blk
    cb = off // width
    ts = _tile(s, 256)

    def body(x_ref, w_ref, b_ref, o_ref, ext_ref):
        @pl.when(pl.program_id(0) == 0)
        def _():
            ext_ref[0:8, :] = jnp.zeros((8, width), F32)

        ext_ref[8:8 + ts, :] = x_ref[...]
        for l0 in range(0, width, CONV_LC):
            ls = slice(l0, l0 + CONV_LC)
            taps = [w_ref[k:k + 1, ls] for k in range(4)]
            bias = b_ref[:, ls]
            for r0 in range(0, ts, CONV_RB):
                pre = bias
                for k in range(4):
                    pre = pre + taps[k] * ext_ref[5 + k + r0:5 + k + r0 + CONV_RB, ls]
                o_ref[r0:r0 + CONV_RB, ls] = _silu(pre) if act else pre
        ext_ref[0:8, :] = x_ref[ts - 8:ts, :]

    return pl.pallas_call(
        body, grid=(s // ts,),
        in_specs=[pl.BlockSpec((ts, width), lambda i: (i, cb)), pl.BlockSpec((4, width), lambda i: (0, 0)),
                  pl.BlockSpec((1, width), lambda i: (0, 0))],
        out_specs=pl.BlockSpec((ts, width), lambda i: (i, 0)),
        out_shape=jax.ShapeDtypeStruct((s, width), F32),
        scratch_shapes=[pltpu.VMEM((ts + 8, width), F32)],
        compiler_params=_cparams(("arbitrary",)), name=name)(src, w, b)


def _conv_bwd_w(src, blk, w, b, dout, act, name, ride=None):
    s = src.shape[0]
    off, width = blk
    cb = off // width
    ts = _tile(s, 256)

    def body(x_ref, w_ref, b_ref, do_ref, *rest):
        if act:
            dpre_ref, dw_ref, db_ref, ext_ref = rest
        else:
            dw_ref, db_ref, ext_ref = rest

        @pl.when(pl.program_id(0) == 0)
        def _():
            ext_ref[0:8, :] = jnp.zeros((8, width), F32)
            dw_ref[...] = jnp.zeros_like(dw_ref)
            db_ref[...] = jnp.zeros_like(db_ref)

        ext_ref[8:8 + ts, :] = x_ref[...]
        for l0 in range(0, width, CONV_LC):
            ls = slice(l0, l0 + CONV_LC)
            taps = [w_ref[k:k + 1, ls] for k in range(4)]
            bias = b_ref[:, ls]
            acc_b = jnp.zeros((8, CONV_LC), F32)
            acc_w = [jnp.zeros((8, CONV_LC), F32) for _ in range(4)]
            for r0 in range(0, ts, CONV_RB):
                xs = [ext_ref[5 + k + r0:5 + k + r0 + CONV_RB, ls] for k in range(4)]
                dpre = do_ref[r0:r0 + CONV_RB, ls]
                if act:
                    pre = bias
                    for k in range(4):
                        pre = pre + taps[k] * xs[k]
                    dpre = dpre * _dsilu(pre)
                    dpre_ref[r0:r0 + CONV_RB, ls] = dpre
                acc_b = acc_b + _fold8(dpre)
                for k in range(4):
                    acc_w[k] = acc_w[k] + _fold8(dpre * xs[k])
            db_ref[:, ls] += jnp.sum(acc_b, axis=0, keepdims=True)
            for k in range(4):
                dw_ref[k:k + 1, ls] += jnp.sum(acc_w[k], axis=0, keepdims=True)
        ext_ref[0:8, :] = x_ref[ts - 8:ts, :]

    row = pl.BlockSpec((ts, width), lambda i: (i, 0))
    outs = [pl.BlockSpec((4, width), lambda i: (0, 0)), pl.BlockSpec((1, width), lambda i: (0, 0))]
    shapes = [jax.ShapeDtypeStruct((4, width), F32), jax.ShapeDtypeStruct((1, width), F32)]
    if act:
        outs = [row] + outs
        shapes = [jax.ShapeDtypeStruct((s, width), F32)] + shapes
    return _pcall(
        body, ride, (src, w, b, dout), grid=(s // ts,),
        in_specs=[pl.BlockSpec((ts, width), lambda i: (i, cb)), pl.BlockSpec((4, width), lambda i: (0, 0)),
                  pl.BlockSpec((1, width), lambda i: (0, 0)), row],
        out_specs=outs, out_shape=shapes,
        scratch_shapes=[pltpu.VMEM((ts + 8, width), F32)], sem=("arbitrary",), name=name)


def _conv_bwd_x(dpre, w, dproj, blk, name):
    s = dpre.shape[0]
    off, width = blk
    cb = off // width
    ts = _tile(s, 256)
    nt = s // ts

    def body(dp_ref, w_ref, dproj_hbm, o_ref, ext_ref):
        del dproj_hbm

        @pl.when(pl.program_id(0) == 0)
        def _():
            ext_ref[ts:ts + 8, :] = jnp.zeros((8, width), F32)

        ext_ref[0:ts, :] = dp_ref[...]
        for l0 in range(0, width, CONV_LC):
            ls = slice(l0, l0 + CONV_LC)
            taps = [w_ref[k:k + 1, ls] for k in range(4)]
            for r0 in range(0, ts, CONV_RB):
                acc = taps[0] * ext_ref[3 + r0:3 + r0 + CONV_RB, ls]
                for k in range(1, 4):
                    acc = acc + taps[k] * ext_ref[3 - k + r0:3 - k + r0 + CONV_RB, ls]
                o_ref[r0:r0 + CONV_RB, ls] = acc.astype(o_ref.dtype)
        ext_ref[ts:ts + 8, :] = dp_ref[0:8, :]

    return pl.pallas_call(
        body, grid=(nt,),
        in_specs=[pl.BlockSpec((ts, width), lambda i: (nt - 1 - i, 0)), pl.BlockSpec((4, width), lambda i: (0, 0)),
                  pl.BlockSpec(memory_space=pl.ANY)],
        out_specs=pl.BlockSpec((ts, width), lambda i: (nt - 1 - i, cb)),
        out_shape=jax.ShapeDtypeStruct(dproj.shape, dproj.dtype),
        scratch_shapes=[pltpu.VMEM((ts + 8, width), F32)],
        input_output_aliases={2: 0},
        compiler_params=_cparams(("arbitrary",)), name=name)(dpre, w, dproj)


def _ssd_decay(a_cs, acst_ref, h, causal, lane_l):
    col = jnp.sum(jnp.where(lane_l == h, a_cs, 0.0), axis=1, keepdims=True)
    row = acst_ref[h:h + 1, :]
    return jnp.where(causal, jnp.exp(jnp.minimum(col - row, 0.0)), 0.0)


def _split3(x):
    hi = x.astype(jnp.bfloat16)
    rest = x - hi.astype(F32)
    mid = rest.astype(jnp.bfloat16)
    return jnp.concatenate([hi, mid, (rest - mid.astype(F32)).astype(jnp.bfloat16)], axis=1)


def _spread_matrix():
    col = jnp.arange(128, dtype=jnp.int32)[:, None]
    e64 = (col == jnp.arange(SSD_W, dtype=jnp.int32)[None, :] // 64).astype(jnp.bfloat16)
    return jnp.tile(e64, (3, 1))


def _ssd_common(dt_ref, dtb_ref, alog_ref, e64_ref, acst_ref, dtx_ref, acx_ref):
    ll = SSD_L
    dt = _softplus(dt_ref[:, 0:128] + dtb_ref[...])
    a_neg = -jnp.exp(alog_ref[...])
    ri = lax.broadcasted_iota(jnp.int32, (ll, ll), 0)
    ci = lax.broadcasted_iota(jnp.int32, (ll, ll), 1)
    causal = ri >= ci
    a_cs = _dot(causal.astype(F32), dt * a_neg, _HI)
    acst_ref[...] = a_cs.T
    both = _dot(jnp.concatenate([_split3(dt), _split3(a_cs)], axis=0), e64_ref[...])
    dtx_ref[...] = both[0:ll]
    acx_ref[...] = both[ll:2 * ll]
    lane_l = lax.broadcasted_iota(jnp.int32, (ll, 128), 1)
    return dt, a_neg, a_cs, causal, ri, lane_l, lane_l < 64


def _ssd_fwd(xbc, proj, dtb, alog, dexp, ng):
    s = xbc.shape[0]
    ll = SSD_L
    nc = s // ll
    e64 = _spread_matrix()

    def body(xbc_ref, dt_ref, z_ref, dtb_ref, alog_ref, dexp_ref, ng_ref, e64_ref,
             yssd_ref, yraw_ref, hprev_ref, ht_ref, acst_ref, dtx_ref, acx_ref):
        @pl.when(pl.program_id(0) == 0)
        def _():
            ht_ref[...] = jnp.zeros_like(ht_ref)

        hprev_ref[0] = ht_ref[...]
        _, _, a_cs, causal, _, lane_l, lo = _ssd_common(dt_ref, dtb_ref, alog_ref, e64_ref, acst_ref, dtx_ref, acx_ref)
        for g in range(4):
            bg = _mx(xbc_ref[:, 2048 + 128 * g:2176 + 128 * g])
            cg = _mx(xbc_ref[:, 2560 + 128 * g:2688 + 128 * g])
            cbm = _dot_nt(cg, bg)
            for jj in range(4):
                j = 4 * g + jj
                sl = slice(128 * j, 128 * j + 128)
                xp = xbc_ref[:, sl]
                acx = acx_ref[:, sl]
                a_last = acx_ref[ll - 1:ll, sl]
                xdt = xp * dtx_ref[:, sl]
                acc = None
                for hh in range(2):
                    dec = _ssd_decay(a_cs, acst_ref, 2 * j + hh, causal, lane_l)
                    xm = jnp.where(lo if hh == 0 else jnp.logical_not(lo), xdt, 0.0)
                    t = _dot(_mx(dec * cbm), _mx(xm))
                    acc = t if acc is None else acc + t
                ht = ht_ref[j]
                y = acc + _dot(cg, _mx(ht)) * jnp.exp(acx) + xp * dexp_ref[:, sl]
                yraw_ref[:, sl] = y
                st = _dot_tn(bg, _mx(xdt * jnp.exp(a_last - acx)))
                ht_ref[j] = ht * jnp.exp(a_last) + st
        for g in range(4):
            sl = slice(512 * g, 512 * g + 512)
            yg = yraw_ref[:, sl] * _silu(z_ref[:, sl])
            r = lax.rsqrt(jnp.mean(yg * yg, axis=-1, keepdims=True) + EPS)
            yssd_ref[:, sl] = (yg * r * ng_ref[:, sl]).astype(yssd_ref.dtype)

    vec = lambda w: pl.BlockSpec((1, w), lambda c: (0, 0))
    return pl.pallas_call(
        body, grid=(nc,),
        in_specs=[pl.BlockSpec((ll, 3072), lambda c: (c, 0)),
                  pl.BlockSpec((ll, DT[1]), lambda c: (c, DT[0] // DT[1])),
                  pl.BlockSpec((ll, Z[1]), lambda c: (c, Z[0] // Z[1])),
                  vec(128), vec(128), vec(2048), vec(2048),
                  pl.BlockSpec(e64.shape, lambda c: (0, 0))],
        out_specs=[pl.BlockSpec((ll, 2048), lambda c: (c, 0)), pl.BlockSpec((ll, 2048), lambda c: (c, 0)),
                   pl.BlockSpec((1, 16, 128, 128), lambda c: (c, 0, 0, 0))],
        out_shape=[jax.ShapeDtypeStruct((s, 2048), _MXU), jax.ShapeDtypeStruct((s, 2048), F32),
                   jax.ShapeDtypeStruct((nc, 16, 128, 128), F32)],
        scratch_shapes=[pltpu.VMEM((16, 128, 128), F32), pltpu.VMEM((128, ll), F32),
                        pltpu.VMEM((ll, 2048), F32), pltpu.VMEM((ll, 2048), F32)],
        compiler_params=_cparams(("arbitrary",)), name="ssd_fwd")(xbc, proj, proj, dtb, alog, dexp, ng, e64)


def _ssd_bwd(xbc, proj, yraw, hprev, dyssd, dproj, dtb, alog, dexp, ng, ride=None):
    s = xbc.shape[0]
    ll = SSD_L
    nc = s // ll
    e64 = _spread_matrix()

    def body(xbc_ref, dt_ref, z_ref, yraw_ref, hprev_ref, dy_ref, dproj_hbm, dtb_ref, alog_ref, dexp_ref, ng_ref,
             e64_ref,
             dz_ref, ddt_ref, dxbc_ref, dng_ref, dda_ref, ddd_ref, ddtb_ref,
             dht_ref, acst_ref, dtx_ref, acx_ref, dyr_ref, rowt_ref):
        del dproj_hbm

        @pl.when(pl.program_id(0) == 0)
        def _():
            dht_ref[...] = jnp.zeros_like(dht_ref)
            dng_ref[...] = jnp.zeros_like(dng_ref)
            dda_ref[...] = jnp.zeros_like(dda_ref)
            ddd_ref[...] = jnp.zeros_like(ddd_ref)
            ddtb_ref[...] = jnp.zeros_like(ddtb_ref)
            rowt_ref[...] = jnp.zeros_like(rowt_ref)

        for g in range(4):
            sl = slice(512 * g, 512 * g + 512)
            zz = z_ref[:, sl]
            yr = yraw_ref[:, sl]
            sz = _silu(zz)
            yg = yr * sz
            r = lax.rsqrt(jnp.mean(yg * yg, axis=-1, keepdims=True) + EPS)
            yhat = yg * r
            dyv = dy_ref[:, sl]
            dng_ref[:, sl] += jnp.sum(dyv * yhat, axis=0, keepdims=True)
            dyh = dyv * ng_ref[:, sl]
            dyg = r * (dyh - yhat * jnp.mean(dyh * yhat, axis=-1, keepdims=True))
            dz_ref[:, sl] = (dyg * yr * _dsilu(zz)).astype(dz_ref.dtype)
            dyr_ref[:, sl] = dyg * sz

        dt, a_neg, a_cs, causal, ri, lane_l, lo = _ssd_common(dt_ref, dtb_ref, alog_ref, e64_ref,
                                                              acst_ref, dtx_ref, acx_ref)
        lane_1 = lax.broadcasted_iota(jnp.int32, (1, 128), 1)
        da_col = jnp.zeros((ll, 128), F32)
        ddt_x = jnp.zeros((ll, 128), F32)
        last = jnp.zeros((1, 128), F32)
        for g in range(4):
            bg = _mx(xbc_ref[:, 2048 + 128 * g:2176 + 128 * g])
            cg = _mx(xbc_ref[:, 2560 + 128 * g:2688 + 128 * g])
            cbm = _dot_nt(cg, bg)
            dcb = jnp.zeros((ll, ll), F32)
            db_g = jnp.zeros((ll, 128), F32)
            dc_g = jnp.zeros((ll, 128), F32)
            for jj in range(4):
                j = 4 * g + jj
                sl = slice(128 * j, 128 * j + 128)
                xp = xbc_ref[:, sl]
                dtx = dtx_ref[:, sl]
                acx = acx_ref[:, sl]
                a_last = acx_ref[ll - 1:ll, sl]
                ea = jnp.exp(acx)
                dte = jnp.exp(a_last - acx)
                cd = jnp.exp(a_last)
                xdt = xp * dtx
                xdt_m = _mx(xdt)
                dy = dyr_ref[:, sl]
                ht = hprev_ref[0, j]
                dhn = dht_ref[j]
                dhn_m = _mx(dhn)
                gmat = _dot(bg, dhn_m)
                dxdt = gmat * dte
                for hh in range(2):
                    h = 2 * j + hh
                    dec = _ssd_decay(a_cs, acst_ref, h, causal, lane_l)
                    mm = dec * cbm
                    dym = _mx(jnp.where(lo if hh == 0 else jnp.logical_not(lo), dy, 0.0))
                    dxdt = dxdt + _dot_tn(_mx(mm), dym)
                    dm = _dot_nt(dym, xdt_m)
                    dcb = dcb + dm * dec
                    qq = dm * mm
                    da_col = da_col + jnp.where(lane_l == h, jnp.sum(qq, axis=1, keepdims=True), 0.0)
                    rowt_ref[h:h + 1, :] = jnp.sum(qq, axis=0, keepdims=True)
                ch = _dot(cg, _mx(ht))
                dyea = dy * ea
                dyea_m = _mx(dyea)
                xw_m = _mx(xdt * dte)
                dc_g = dc_g + _dot_nt(dyea_m, _mx(ht))
                db_g = db_g + _dot_nt(xw_m, dhn_m)
                wl = xdt * gmat * dte
                lane_a = dyea * ch - wl
                lane_b = dxdt * xp
                lane_c = jnp.sum(dhn * ht, axis=0, keepdims=True) * cd + jnp.sum(wl, axis=0, keepdims=True)
                for hh in range(2):
                    h = 2 * j + hh
                    mine = lo if hh == 0 else jnp.logical_not(lo)
                    da_col = da_col + jnp.where(
                        lane_l == h, jnp.sum(jnp.where(mine, lane_a, 0.0), axis=1, keepdims=True), 0.0)
                    ddt_x = ddt_x + jnp.where(
                        lane_l == h, jnp.sum(jnp.where(mine, lane_b, 0.0), axis=1, keepdims=True), 0.0)
                    mine_1 = (lane_1 < 64) if hh == 0 else (lane_1 >= 64)
                    last = last + jnp.where(
                        lane_1 == h, jnp.sum(jnp.where(mine_1, lane_c, 0.0), axis=1, keepdims=True), 0.0)
                dht_ref[j] = dhn * cd + _dot_tn(cg, dyea_m)
                dxbc_ref[:, sl] = dxdt * dtx + dy * dexp_ref[:, sl]
                ddd_ref[:, sl] += jnp.sum(dy * xp, axis=0, keepdims=True)
            dcb_m = _mx(dcb)
            dxbc_ref[:, 2048 + 128 * g:2176 + 128 * g] = db_g + _dot_tn(dcb_m, cg)
            dxbc_ref[:, 2560 + 128 * g:2688 + 128 * g] = dc_g + _dot(dcb_m, bg)

        da_cs = da_col - rowt_ref[...].T
        da_cs = da_cs + jnp.where(lax.broadcasted_iota(jnp.int32, (ll, 128), 0) == ll - 1, last, 0.0)
        d_dta = _dot((ri <= lax.broadcasted_iota(jnp.int32, (ll, ll), 1)).astype(F32), da_cs, _HI)
        ddt = d_dta * a_neg + ddt_x
        dda_ref[...] += jnp.sum(d_dta * dt, axis=0, keepdims=True)
        ddt_raw = ddt * _sigmoid(dt_ref[:, 0:128] + dtb_ref[...])
        ddtb_ref[...] += jnp.sum(ddt_raw, axis=0, keepdims=True)
        ddt_ref[:, 0:128] = ddt_raw.astype(ddt_ref.dtype)
        ddt_ref[:, 128:DT[1]] = jnp.zeros((ll, DT[1] - 128), ddt_ref.dtype)

    rev = lambda c: nc - 1 - c
    vec = lambda w: pl.BlockSpec((1, w), lambda c: (0, 0))
    row = lambda w: pl.BlockSpec((ll, w), lambda c: (rev(c), 0))
    return _pcall(
        body, ride, (xbc, proj, proj, yraw, hprev, dyssd, dproj, dtb, alog, dexp, ng, e64), grid=(nc,),
        in_specs=[row(3072),
                  pl.BlockSpec((ll, DT[1]), lambda c: (rev(c), DT[0] // DT[1])),
                  pl.BlockSpec((ll, Z[1]), lambda c: (rev(c), Z[0] // Z[1])),
                  row(2048),
                  pl.BlockSpec((1, 16, 128, 128), lambda c: (rev(c), 0, 0, 0)),
                  row(2048),
                  pl.BlockSpec(memory_space=pl.ANY),
                  vec(128), vec(128), vec(2048), vec(2048),
                  pl.BlockSpec(e64.shape, lambda c: (0, 0))],
        out_specs=[pl.BlockSpec((ll, Z[1]), lambda c: (rev(c), Z[0] // Z[1])),
                   row(DT[1]),
                   row(3072), vec(2048), vec(128), vec(2048), vec(128)],
        out_shape=[jax.ShapeDtypeStruct(dproj.shape, dproj.dtype), jax.ShapeDtypeStruct((s, DT[1]), dproj.dtype),
                   jax.ShapeDtypeStruct((s, 3072), F32), jax.ShapeDtypeStruct((1, 2048), F32),
                   jax.ShapeDtypeStruct((1, 128), F32), jax.ShapeDtypeStruct((1, 2048), F32),
                   jax.ShapeDtypeStruct((1, 128), F32)],
        scratch_shapes=[pltpu.VMEM((16, 128, 128), F32), pltpu.VMEM((128, ll), F32),
                        pltpu.VMEM((ll, 2048), F32), pltpu.VMEM((ll, 2048), F32), pltpu.VMEM((ll, 2048), F32),
                        pltpu.VMEM((128, ll), F32)],
        aliases={6: 0}, sem=("arbitrary",), name="ssd_bwd")


def _put_block(src, dproj, blk, name):
    s = src.shape[0]
    off, width = blk
    cb = off // width
    ts = _tile(s, 1024)

    def body(s_ref, dproj_hbm, o_ref):
        del dproj_hbm
        o_ref[...] = s_ref[...]

    return pl.pallas_call(
        body, grid=(s // ts,),
        in_specs=[pl.BlockSpec((ts, width), lambda i: (i, 0)), pl.BlockSpec(memory_space=pl.ANY)],
        out_specs=pl.BlockSpec((ts, width), lambda i: (i, cb)),
        out_shape=jax.ShapeDtypeStruct(dproj.shape, dproj.dtype),
        input_output_aliases={1: 0},
        compiler_params=_cparams(("parallel",)), name=name)(src, dproj)


LRU_G = 384


def _lru_gates(xl_ref, wa_ref, wx_ref, ba_ref, bx_ref, lam_ref, g):
    sl = slice(LRU_G * g, LRU_G * g + LRU_G)
    xg = xl_ref[:, sl]
    xm = _mx(xg)
    r = _sigmoid(_dot(xm, wa_ref[g]) + ba_ref[:, sl])
    ig = _sigmoid(_dot(xm, wx_ref[g]) + bx_ref[:, sl])
    sp = _softplus(-lam_ref[:, sl])
    log_a = (-LRU_C * r) * sp
    a = jnp.exp(log_a)
    mult = jnp.sqrt(_one_minus_sq(log_a, a))
    return sl, xg, r, ig, sp, a, mult


def _lru_fwd(xl, proj, wa, wx, ba, bx, lam):
    s = xl.shape[0]
    ts = _tile(s, 256)
    w = LRU_W

    def body(xl_ref, lg_ref, wa_ref, wx_ref, ba_ref, bx_ref, lam_ref, y_ref, hs_ref, a_ref, u_ref, carry_ref):
        @pl.when(pl.program_id(0) == 0)
        def _():
            carry_ref[...] = jnp.zeros_like(carry_ref)

        for g in range(4):
            sl, xg, _, ig, _, a, mult = _lru_gates(xl_ref, wa_ref, wx_ref, ba_ref, bx_ref, lam_ref, g)
            a_ref[:, sl] = a
            u_ref[:, sl] = mult * (ig * xg)

        def step(t, h):
            h = a_ref[pl.ds(t, 1), :] * h + u_ref[pl.ds(t, 1), :]
            hs_ref[pl.ds(t, 1), :] = h
            return h

        carry_ref[0:1, :] = lax.fori_loop(0, ts, step, carry_ref[0:1, :], unroll=8)
        y_ref[...] = (hs_ref[...] * _silu(lg_ref[...])).astype(y_ref.dtype)

    row = pl.BlockSpec((ts, w), lambda i: (i, 0))
    vec = pl.BlockSpec((1, w), lambda i: (0, 0))
    wsp = pl.BlockSpec((4, LRU_G, LRU_G), lambda i: (0, 0, 0))
    return pl.pallas_call(
        body, grid=(s // ts,),
        in_specs=[row, pl.BlockSpec((ts, w), lambda i: (i, LG[0] // w)), wsp, wsp, vec, vec, vec],
        out_specs=[row, row],
        out_shape=[jax.ShapeDtypeStruct((s, w), _MXU), jax.ShapeDtypeStruct((s, w), F32)],
        scratch_shapes=[pltpu.VMEM((ts, w), F32), pltpu.VMEM((ts, w), F32), pltpu.VMEM((8, w), F32)],
        compiler_params=_cparams(("arbitrary",)), name="lru_fwd")(xl, proj, wa, wx, ba, bx, lam)


def _lru_bwd(xl, proj, hs, dy, dproj, wa, wx, ba, bx, lam, ride=None):
    s = xl.shape[0]
    ts = _tile(s, 256)
    nt = s // ts
    w = LRU_W
    hb = ts // 8

    def body(xl_ref, lg_ref, hs_ref, hprev_ref, dy_ref, dproj_hbm, wa_ref, wx_ref, ba_ref, bx_ref, lam_ref,
             dlg_ref, dxl_ref, dwa_ref, dwx_ref, dba_ref, dbx_ref, dlam_ref,
             a_ref, dh_ref, ext_ref, carry_ref, r_ref, ig_ref, mult_ref):
        del dproj_hbm
        i = pl.program_id(0)

        @pl.when(i == 0)
        def _():
            carry_ref[...] = jnp.zeros_like(carry_ref)
            for ref in (dwa_ref, dwx_ref, dba_ref, dbx_ref, dlam_ref):
                ref[...] = jnp.zeros_like(ref)

        lg = lg_ref[...]
        dyv = dy_ref[...]
        dh_ref[...] = dyv * _silu(lg)
        dlg_ref[...] = (dyv * hs_ref[...] * _dsilu(lg)).astype(dlg_ref.dtype)
        for g in range(4):
            sl, _, r, ig, _, a, mult = _lru_gates(xl_ref, wa_ref, wx_ref, ba_ref, bx_ref, lam_ref, g)
            a_ref[:, sl] = a
            r_ref[:, sl] = r
            ig_ref[:, sl] = ig
            mult_ref[:, sl] = mult

        def step(k, carry):
            t = ts - 1 - k
            dh = dh_ref[pl.ds(t, 1), :] + carry
            dh_ref[pl.ds(t, 1), :] = dh
            return a_ref[pl.ds(t, 1), :] * dh

        carry_ref[0:1, :] = lax.fori_loop(0, ts, step, carry_ref[0:1, :], unroll=8)

        ext_ref[0:8, :] = jnp.where(i == nt - 1, 0.0, 1.0) * hprev_ref[...]
        ext_ref[8:8 + ts, :] = hs_ref[...]
        for g in range(4):
            sl = slice(LRU_G * g, LRU_G * g + LRU_G)
            xg, r, ig, a, mult = xl_ref[:, sl], r_ref[:, sl], ig_ref[:, sl], a_ref[:, sl], mult_ref[:, sl]
            sp = _softplus(-lam_ref[:, sl])
            dh = dh_ref[:, sl]
            da = dh * ext_ref[7:7 + ts, sl]
            dmult = dh * ig * xg
            di = dh * mult * xg
            dxl = dh * mult * ig
            dlog_a = da * a - dmult * (a * a) / mult
            dlam_ref[:, sl] += jnp.sum(dlog_a * r, axis=0, keepdims=True) * (LRU_C * _sigmoid(-lam_ref[:, sl]))
            dpa = dlog_a * (-LRU_C * sp) * r * (1.0 - r)
            dpx = di * ig * (1.0 - ig)
            dba_ref[:, sl] += jnp.sum(dpa, axis=0, keepdims=True)
            dbx_ref[:, sl] += jnp.sum(dpx, axis=0, keepdims=True)
            dpa_m, dpx_m, xm = _mx(dpa), _mx(dpx), _mx(xg)
            dxl_ref[:, sl] = dxl + _dot_nt(dpa_m, wa_ref[g]) + _dot_nt(dpx_m, wx_ref[g])
            dwa_ref[g] += _dot_tn(xm, dpa_m)
            dwx_ref[g] += _dot_tn(xm, dpx_m)

    rev = lambda i: nt - 1 - i
    row = pl.BlockSpec((ts, w), lambda i: (rev(i), 0))
    vec = pl.BlockSpec((1, w), lambda i: (0, 0))
    wsp = pl.BlockSpec((4, LRU_G, LRU_G), lambda i: (0, 0, 0))
    lgs = pl.BlockSpec((ts, w), lambda i: (rev(i), LG[0] // w))
    return _pcall(
        body, ride, (xl, proj, hs, hs, dy, dproj, wa, wx, ba, bx, lam), grid=(nt,),
        in_specs=[row, lgs, row, pl.BlockSpec((8, w), lambda i: (jnp.maximum(rev(i) * hb - 1, 0), 0)), row,
                  pl.BlockSpec(memory_space=pl.ANY), wsp, wsp, vec, vec, vec],
        out_specs=[lgs, row, wsp, wsp, vec, vec, vec],
        out_shape=[jax.ShapeDtypeStruct(dproj.shape, dproj.dtype), jax.ShapeDtypeStruct((s, w), F32),
                   jax.ShapeDtypeStruct((4, LRU_G, LRU_G), F32), jax.ShapeDtypeStruct((4, LRU_G, LRU_G), F32),
                   jax.ShapeDtypeStruct((1, w), F32), jax.ShapeDtypeStruct((1, w), F32),
                   jax.ShapeDtypeStruct((1, w), F32)],
        scratch_shapes=[pltpu.VMEM((ts, w), F32), pltpu.VMEM((ts, w), F32), pltpu.VMEM((ts + 8, w), F32),
                        pltpu.VMEM((8, w), F32), pltpu.VMEM((ts, w), F32), pltpu.VMEM((ts, w), F32),
                        pltpu.VMEM((ts, w), F32)],
        aliases={5: 0}, sem=("arbitrary",), name="lru_bwd")


def _mem_kv_fwd(mem, g, wkv):
    m = mem.shape[0]

    def body(mem_ref, g_ref, w_ref, k_ref, v_ref, mn_ref):
        mv = mem_ref[...]
        r = lax.rsqrt(jnp.mean(mv * mv, axis=-1, keepdims=True) + EPS)
        mn = _mx(mv * r * g_ref[...])
        mn_ref[...] = mn
        kv = _dot(mn, w_ref[...])
        k_ref[...] = kv[:, 0:D].astype(k_ref.dtype)
        v_ref[...] = kv[:, D:2 * D].astype(v_ref.dtype)

    sh = jax.ShapeDtypeStruct((m, D), _MXU)
    return pl.pallas_call(body, out_shape=[sh, sh, sh], compiler_params=_cparams(None), name="mem_kv_fwd")(mem, g, wkv)


def _mem_kv_bwd(mem, g, mn, wkv, dk, dv):
    m = mem.shape[0]

    def body(mem_ref, g_ref, mn_ref, w_ref, dk_ref, dv_ref, dw_ref, dg_ref):
        dkv = _mx(jnp.concatenate([dk_ref[...], dv_ref[...]], axis=1))
        dw_ref[...] = _dot_tn(mn_ref[...], dkv).astype(dw_ref.dtype)
        dmn = _dot_nt(dkv, w_ref[...])
        mv = mem_ref[...]
        r = lax.rsqrt(jnp.mean(mv * mv, axis=-1, keepdims=True) + EPS)
        dg_ref[...] = jnp.sum(dmn * mv * r, axis=0, keepdims=True)

    del m
    return pl.pallas_call(
        body, out_shape=[jax.ShapeDtypeStruct((D, 2 * D), _MXU), jax.ShapeDtypeStruct((1, D), F32)],
        compiler_params=_cparams(None), name="mem_kv_bwd")(mem, g, mn, wkv, dk, dv)


def _attn_probs(q_ref, k_ref, hd):
    sl = slice(MEM_HD * hd, MEM_HD * hd + MEM_HD)
    qh = _mx(q_ref[:, sl])
    sc = _dot_nt(qh, k_ref[:, sl]) * (MEM_HD ** -0.5)
    e = jnp.exp(sc - jnp.max(sc, axis=-1, keepdims=True))
    return sl, qh, e / jnp.sum(e, axis=-1, keepdims=True)


def _attn_fwd(proj, k, v):
    s = proj.shape[0]
    m = k.shape[0]
    ts = _tile(s, 512)

    def body(q_ref, k_ref, v_ref, y_ref):
        for hd in range(MEM_HEADS):
            sl, _, p = _attn_probs(q_ref, k_ref, hd)
            y_ref[:, sl] = _dot(_mx(p), v_ref[:, sl]).astype(y_ref.dtype)

    kvs = pl.BlockSpec((m, D), lambda i: (0, 0))
    return pl.pallas_call(
        body, grid=(s // ts,),
        in_specs=[pl.BlockSpec((ts, D), lambda i: (i, Q[0] // D)), kvs, kvs],
        out_specs=pl.BlockSpec((ts, D), lambda i: (i, 0)),
        out_shape=jax.ShapeDtypeStruct((s, D), _MXU),
        compiler_params=_cparams(("parallel",)), name="attn_fwd")(proj, k, v)


def _attn_bwd(proj, k, v, dy, dproj):
    s = proj.shape[0]
    m = k.shape[0]
    ts = _tile(s, 512)

    def body(q_ref, k_ref, v_ref, dy_ref, dproj_hbm, dq_ref, dk_ref, dv_ref):
        del dproj_hbm

        @pl.when(pl.program_id(0) == 0)
        def _():
            dk_ref[...] = jnp.zeros_like(dk_ref)
            dv_ref[...] = jnp.zeros_like(dv_ref)

        for hd in range(MEM_HEADS):
            sl, qh, p = _attn_probs(q_ref, k_ref, hd)
            dyh = _mx(dy_ref[:, sl])
            dp = _dot_nt(dyh, v_ref[:, sl])
            ds = _mx(p * (dp - jnp.sum(dp * p, axis=-1, keepdims=True)) * (MEM_HD ** -0.5))
            dq_ref[:, sl] = _dot(ds, k_ref[:, sl]).astype(dq_ref.dtype)
            dk_ref[:, sl] += _dot_tn(ds, qh)
            dv_ref[:, sl] += _dot_tn(_mx(p), dyh)

    kvs = pl.BlockSpec((m, D), lambda i: (0, 0))
    qs = pl.BlockSpec((ts, D), lambda i: (i, Q[0] // D))
    return pl.pallas_call(
        body, grid=(s // ts,),
        in_specs=[qs, kvs, kvs, pl.BlockSpec((ts, D), lambda i: (i, 0)), pl.BlockSpec(memory_space=pl.ANY)],
        out_specs=[qs, kvs, kvs],
        out_shape=[jax.ShapeDtypeStruct(dproj.shape, dproj.dtype), jax.ShapeDtypeStruct((m, D), F32),
                   jax.ShapeDtypeStruct((m, D), F32)],
        input_output_aliases={4: 0},
        compiler_params=_cparams(("arbitrary",)), name="attn_bwd")(proj, k, v, dy, dproj)


def _merge_fb(x, target, yssd, ylru, ymem, proj, wbs, wbl, wbm, wo, fg):
    s = x.shape[0]
    ts = _tile(s, 256)

    def body(x_ref, t_ref, ys_ref, yl_ref, ym_ref, gl_ref, wbs_ref, wbl_ref, wbm_ref, wo_ref, fg_ref,
             dgl_ref, dx2_ref, dx2m_ref, mg_ref, db0_ref, db1_ref, db2_ref, loss_ref, dfg_ref):
        @pl.when(pl.program_id(0) == 0)
        def _():
            loss_ref[...] = jnp.zeros_like(loss_ref)
            dfg_ref[...] = jnp.zeros_like(dfg_ref)

        bs = (_dot(ys_ref[...], wbs_ref[...]), _dot(yl_ref[...], wbl_ref[...]), _dot(ym_ref[...], wbm_ref[...]))
        gates = [_sigmoid(gl_ref[:, D * n:D * n + D]) for n in range(3)]
        merged = gates[0] * bs[0] + gates[1] * bs[1] + gates[2] * bs[2]
        mg = _mx(merged)
        mg_ref[...] = mg
        x2 = x_ref[...] + _dot(mg, wo_ref[...])
        r = lax.rsqrt(jnp.mean(x2 * x2, axis=-1, keepdims=True) + EPS)
        xhat = x2 * r
        err = xhat * fg_ref[...] - t_ref[...]
        loss_ref[...] += jnp.sum(err * err, axis=0, keepdims=True) * (0.5 / D)
        dy = err * (1.0 / D)
        dfg_ref[...] += jnp.sum(dy * xhat, axis=0, keepdims=True)
        dxh = dy * fg_ref[...]
        dx2 = r * (dxh - xhat * jnp.mean(dxh * xhat, axis=-1, keepdims=True))
        dx2_ref[...] = dx2
        dx2m = _mx(dx2)
        dx2m_ref[...] = dx2m
        dmg = _dot_nt(dx2m, wo_ref[...])
        for n, db_ref in enumerate((db0_ref, db1_ref, db2_ref)):
            gt = gates[n]
            dgl_ref[:, D * n:D * n + D] = (dmg * bs[n] * gt * (1.0 - gt)).astype(dgl_ref.dtype)
            db_ref[...] = (dmg * gt).astype(db_ref.dtype)

    row = lambda w: pl.BlockSpec((ts, w), lambda i: (i, 0))
    full = lambda a: pl.BlockSpec(a.shape, lambda i: (0, 0))
    vec = pl.BlockSpec((1, D), lambda i: (0, 0))
    gls = pl.BlockSpec((ts, GL[1]), lambda i: (i, GL[0] // GL[1]))
    act = jax.ShapeDtypeStruct((s, D), _MXU)
    return pl.pallas_call(
        body, grid=(s // ts,),
        in_specs=[row(D), row(D), row(SSD_W), row(LRU_W), row(D), gls, full(wbs), full(wbl), full(wbm), full(wo), vec],
        out_specs=[gls, row(D), row(D), row(D), row(D), row(D), row(D), vec, vec],
        out_shape=[jax.ShapeDtypeStruct((s, NP), _MXU), jax.ShapeDtypeStruct((s, D), F32), act, act, act, act, act,
                   jax.ShapeDtypeStruct((1, D), F32), jax.ShapeDtypeStruct((1, D), F32)],
        compiler_params=_cparams(("arbitrary",)), name="merge_fwd_bwd")(
            x, target, yssd, ylru, ymem, proj, wbs, wbl, wbm, wo, fg)


def _adamw(w, g, m, v, name):
    rows, cols = w.shape
    tr = rows if rows <= 512 else 256
    assert rows % tr == 0

    def body(w_ref, g_ref, m_ref, v_ref, d_ref, mo_ref, vo_ref):
        gv = g_ref[...]
        mn = ADAM_B1 * m_ref[...] + (1.0 - ADAM_B1) * gv
        vn = ADAM_B2 * v_ref[...] + (1.0 - ADAM_B2) * (gv * gv)
        m_hat = mn / (1.0 - ADAM_B1 ** ADAM_STEP)
        v_hat = vn / (1.0 - ADAM_B2 ** ADAM_STEP)
        d_ref[...] = -ADAM_LR * (m_hat / (jnp.sqrt(v_hat) + ADAM_EPS) + ADAM_WD * w_ref[...])
        mo_ref[...] = mn
        vo_ref[...] = vn

    blk = pl.BlockSpec((tr, cols), lambda i: (i, 0))
    sh = jax.ShapeDtypeStruct((rows, cols), F32)
    return pl.pallas_call(
        body, grid=(rows // tr,), in_specs=[blk] * 4, out_specs=[blk] * 3, out_shape=[sh] * 3,
        compiler_params=_cparams(("parallel",)), name=name)(w, g, m, v)


def _mesh_pos():
    x, y, c = lax.axis_index("x"), lax.axis_index("y"), lax.axis_index("c")
    chips = [(1 - x, y), (x, 1 - y), (1 - x, 1 - y)]
    return x, y, c, 2 * x + y, chips


def _hbm():
    return pl.BlockSpec(memory_space=pl.ANY)


def _remote(src, dst, send_sem, recv_sem, dev):
    return pltpu.make_async_remote_copy(src_ref=src, dst_ref=dst, send_sem=send_sem, recv_sem=recv_sem,
                                        device_id=dev, device_id_type=MESH)


def _sems(n):
    return [pltpu.SemaphoreType.DMA((n,)), pltpu.SemaphoreType.DMA((n,))]


class _Exchange:
    def __init__(self, inputs, out_shape, n_sem, start, finish, aliases=None):
        self.inputs, self.out_shape, self.n_sem = list(inputs), list(out_shape), n_sem
        self.start, self.finish, self.aliases = start, finish, dict(aliases or {})


def _run_exchange(ex, name):
    n_in, n_out = len(ex.inputs), len(ex.out_shape)

    def body(*refs):
        srcs, outs = refs[:n_in], refs[n_in:n_in + n_out]
        send_sems, recv_sems = refs[n_in + n_out:]
        ex.start(srcs, outs, send_sems, recv_sems)
        ex.finish(srcs, outs, send_sems, recv_sems)

    return pl.pallas_call(
        body, in_specs=[_hbm()] * n_in, out_specs=[_hbm()] * n_out, out_shape=ex.out_shape,
        input_output_aliases=ex.aliases, scratch_shapes=_sems(ex.n_sem), name=name)(*ex.inputs)


def _pcall(body, ride, args, *, grid, in_specs, out_specs, out_shape, scratch_shapes, sem, name, aliases=None):
    in_specs, out_specs, out_shape = list(in_specs), list(out_specs), list(out_shape)
    scratch_shapes, aliases = list(scratch_shapes), dict(aliases or {})
    if ride is None:
        outs = pl.pallas_call(
            body, grid=grid, in_specs=in_specs, out_specs=out_specs, out_shape=out_shape, scratch_shapes=scratch_shapes,
            input_output_aliases=aliases, compiler_params=_cparams(sem), name=name)(*args)
        return outs, None
    n_in, n_out, n_scr = len(in_specs), len(out_shape), len(scratch_shapes)
    e_in, e_out = len(ride.inputs), len(ride.out_shape)

    def carried(*refs):
        cut = [n_in, e_in, n_out, e_out, n_scr]
        parts, p = [], 0
        for c in cut:
            parts.append(refs[p:p + c])
            p += c
        ins, e_ins, outs, e_outs, scr = parts
        send_sems, recv_sems = refs[p], refs[p + 1]
        first = last = None
        for d, size in enumerate(grid):
            i = pl.program_id(d)
            first = (i == 0) if first is None else jnp.logical_and(first, i == 0)
            last = (i == size - 1) if last is None else jnp.logical_and(last, i == size - 1)

        @pl.when(first)
        def _():
            ride.start(e_ins, e_outs, send_sems, recv_sems)

        body(*ins, *outs, *scr)

        @pl.when(last)
        def _():
            ride.finish(e_ins, e_outs, send_sems, recv_sems)

    for k, v in ride.aliases.items():
        aliases[n_in + k] = n_out + v
    res = pl.pallas_call(
        carried, grid=grid, in_specs=in_specs + [_hbm()] * e_in, out_specs=out_specs + [_hbm()] * e_out,
        out_shape=out_shape + ride.out_shape, scratch_shapes=scratch_shapes + _sems(ride.n_sem),
        input_output_aliases=aliases, compiler_params=_cparams(("arbitrary",) * len(grid)),
        name=name)(*args, *ride.inputs)
    return res[:n_out], res[n_out:]


def _gather_shards(arrs, split):
    n = len(arrs)
    n_sem = sum(6 if sp else 3 for sp in split)

    def rows(i, which):
        if not split[i]:
            return pl.ds(0, arrs[i].shape[0])
        half = arrs[i].shape[0] // 2
        return pl.ds(which * half, half)

    def sends(srcs, outs, send_sems, recv_sems):
        _, _, c, me, chips = _mesh_pos()
        return [(_remote(srcs[i].at[rows(i, c)], outs[i].at[me, rows(i, c)], send_sems.at[3 * i + j],
                         recv_sems.at[3 * i + j], (cx, cy, c)), i, j)
                for i in range(n) for j, (cx, cy) in enumerate(chips)]

    def start(srcs, outs, send_sems, recv_sems):
        for cp, _, _ in sends(srcs, outs, send_sems, recv_sems):
            cp.start()

    def finish(srcs, outs, send_sems, recv_sems):
        x, y, c, _, chips = _mesh_pos()
        sib = (x, y, 1 - c)
        passed, k = [], 3 * n
        for i, j in [(i, j) for i in range(n) for j in range(3)]:
            cx, cy = chips[j]
            slot = outs[i].at[2 * cx + cy, rows(i, c)]
            _remote(slot, slot, send_sems.at[3 * i + j], recv_sems.at[3 * i + j], (cx, cy, c)).wait_recv()
            if split[i]:
                fwd = _remote(slot, slot, send_sems.at[k], recv_sems.at[k], sib)
                fwd.start()
                passed.append((fwd, i, j, k))
                k += 1
        for _, i, j, kf in passed:
            cx, cy = chips[j]
            slot = outs[i].at[2 * cx + cy, rows(i, 1 - c)]
            _remote(slot, slot, send_sems.at[kf], recv_sems.at[kf], sib).wait_recv()
        for cp in [s[0] for s in sends(srcs, outs, send_sems, recv_sems)] + [p[0] for p in passed]:
            cp.wait_send()

    return _Exchange(arrs, [jax.ShapeDtypeStruct((NSHARD,) + a.shape, a.dtype) for a in arrs], n_sem, start, finish)


def _with_own_slot(arrs, got):
    own_slot = jnp.arange(NSHARD, dtype=jnp.int32)[:, None, None] == 2 * lax.axis_index("x") + lax.axis_index("y")
    return [jnp.where(own_slot, a[None], g) for a, g in zip(arrs, got)]


def _swap_halves(arrs):
    n = len(arrs)

    def copies(srcs, outs, send_sems, recv_sems):
        x, y, c, _, _ = _mesh_pos()
        cps = []
        for i in range(n):
            half = arrs[i].shape[1] // 2
            cps.append(_remote(srcs[i].at[:, pl.ds((1 - c) * half, half)], outs[i], send_sems.at[i], recv_sems.at[i],
                               (x, y, 1 - c)))
        return cps

    def start(*refs):
        for cp in copies(*refs):
            cp.start()

    def finish(*refs):
        for cp in copies(*refs):
            cp.wait()

    shapes = [jax.ShapeDtypeStruct((NSHARD, a.shape[1] // 2, a.shape[2]), a.dtype) for a in arrs]
    return _Exchange(arrs, shapes, n, start, finish)


def _scatter_chips(arrs):
    n = len(arrs)

    def copies(srcs, outs, send_sems, recv_sems):
        _, _, c, me, chips = _mesh_pos()
        own = [pltpu.make_async_copy(srcs[i].at[me], outs[i].at[me], send_sems.at[3 * n + i]) for i in range(n)]
        cps = [_remote(srcs[i].at[2 * cx + cy], outs[i].at[me], send_sems.at[3 * i + j], recv_sems.at[3 * i + j],
                       (cx, cy, c)) for i in range(n) for j, (cx, cy) in enumerate(chips)]
        return own, cps

    def start(*refs):
        own, cps = copies(*refs)
        for cp in own + cps:
            cp.start()

    def finish(srcs, outs, send_sems, recv_sems):
        _, _, c, _, chips = _mesh_pos()
        for i in range(n):
            for j, (cx, cy) in enumerate(chips):
                slot = outs[i].at[2 * cx + cy]
                _remote(slot, slot, send_sems.at[3 * i + j], recv_sems.at[3 * i + j], (cx, cy, c)).wait_recv()
        own, cps = copies(srcs, outs, send_sems, recv_sems)
        for cp in cps:
            cp.wait_send()
        for cp in own:
            cp.wait()

    return _Exchange(arrs, [jax.ShapeDtypeStruct(a.shape, a.dtype) for a in arrs], 4 * n, start, finish)


def _share_halves(arrs):
    n = len(arrs)

    def copies(outs, send_sems, recv_sems):
        x, y, c, _, _ = _mesh_pos()
        return [_remote(outs[i].at[c], outs[i].at[c], send_sems.at[i], recv_sems.at[i], (x, y, 1 - c))
                for i in range(n)]

    def start(srcs, outs, send_sems, recv_sems):
        del srcs
        for cp in copies(outs, send_sems, recv_sems):
            cp.start()

    def finish(srcs, outs, send_sems, recv_sems):
        del srcs
        x, y, c, _, _ = _mesh_pos()
        for i in range(n):
            theirs = outs[i].at[1 - c]
            _remote(theirs, theirs, send_sems.at[i], recv_sems.at[i], (x, y, 1 - c)).wait_recv()
        for cp in copies(outs, send_sems, recv_sems):
            cp.wait_send()

    return _Exchange(arrs, [jax.ShapeDtypeStruct(a.shape, a.dtype) for a in arrs], n, start, finish,
                     aliases={i: i for i in range(n)})


def _gather_small(full):
    _, width = full.shape

    def copies(srcs, outs, send_sems, recv_sems):
        _, _, c, me, chips = _mesh_pos()
        mine = srcs[0].at[pl.ds(0, SMALL_ROWS)]
        own = pltpu.make_async_copy(mine, outs[0].at[me], send_sems.at[3])
        return own, [_remote(mine, outs[0].at[me], send_sems.at[j], recv_sems.at[j], (cx, cy, c))
                     for j, (cx, cy) in enumerate(chips)]

    def start(*refs):
        own, cps = copies(*refs)
        for cp in [own] + cps:
            cp.start()

    def finish(srcs, outs, send_sems, recv_sems):
        _, _, c, _, chips = _mesh_pos()
        for j, (cx, cy) in enumerate(chips):
            slot = outs[0].at[2 * cx + cy]
            _remote(slot, slot, send_sems.at[j], recv_sems.at[j], (cx, cy, c)).wait_recv()
        own, cps = copies(srcs, outs, send_sems, recv_sems)
        for cp in cps:
            cp.wait_send()
        own.wait()

    return _Exchange([full], [jax.ShapeDtypeStruct((NSHARD, SMALL_ROWS, width), full.dtype)], 4, start, finish)


def _add_sibling(mine, recv, c, name):
    _, half, width = recv.shape
    tr = _tile(half, 256, 8)
    nb = half // tr

    def body(c_ref, a_ref, b_ref, o_ref):
        del c_ref
        o_ref[...] = (a_ref[...].astype(F32) + b_ref[...].astype(F32)).astype(o_ref.dtype)

    grid_spec = pltpu.PrefetchScalarGridSpec(
        num_scalar_prefetch=1, grid=(NSHARD, nb),
        in_specs=[pl.BlockSpec((1, tr, width), lambda j, r, c_ref: (j, c_ref[0] * nb + r, 0)),
                  pl.BlockSpec((1, tr, width), lambda j, r, c_ref: (j, r, 0))],
        out_specs=pl.BlockSpec((1, tr, width), lambda j, r, c_ref: (j, r, 0)))
    return pl.pallas_call(
        body, grid_spec=grid_spec, out_shape=jax.ShapeDtypeStruct(recv.shape, recv.dtype),
        compiler_params=_cparams(("parallel", "parallel")), name=name)(c, mine, recv)


def _sum_chips(parts, c, name):
    _, half, width = parts.shape
    tr = _tile(half, 256, 8)

    def body(c_ref, p_ref, o_ref):
        del c_ref
        p = [p_ref[j].astype(F32) for j in range(NSHARD)]
        o_ref[0] = ((p[0] + p[1]) + p[2]) + p[3]

    grid_spec = pltpu.PrefetchScalarGridSpec(
        num_scalar_prefetch=1, grid=(half // tr,),
        in_specs=[pl.BlockSpec((NSHARD, tr, width), lambda r, c_ref: (0, r, 0))],
        out_specs=pl.BlockSpec((1, tr, width), lambda r, c_ref: (c_ref[0], r, 0)))
    return pl.pallas_call(
        body, grid_spec=grid_spec, out_shape=jax.ShapeDtypeStruct((2, half, width), F32),
        compiler_params=_cparams(("parallel",)), name=name)(c, parts)


def _unpack(flat, names, shapes):
    out, off = {}, 0
    for n in names:
        sz = _size(shapes[n])
        out[n] = flat[off:off + sz].reshape(shapes[n])
        off += sz
    return out


def _reorder_w_in(w):
    return jnp.concatenate([w[:, 2048:5120], w[:, 9248:12320], w[:, 0:2048], w[:, 8224:9248], w[:, 5152:6688],
                            w[:, 6688:8224], w[:, 5120:5152], jnp.zeros((D, NP - 12320), w.dtype)], axis=1)


def _restore_w_in(g):
    return jnp.concatenate([g[:, 6144:8192], g[:, 0:3072], g[:, 12288:12320], g[:, 9216:10752], g[:, 10752:12288],
                            g[:, 8192:9216], g[:, 3072:6144]], axis=1)


def _lru_group_weights(w):
    w4 = w.reshape(4, 4, 96, 96)
    eye = jnp.eye(4, dtype=w.dtype)
    return (w4[:, :, None, :, :] * eye[None, :, :, None, None]).transpose(0, 1, 3, 2, 4).reshape(4, LRU_G, LRU_G)


def _lru_group_blocks(g):
    g5 = g.reshape(4, 4, 96, 4, 96)
    return jnp.stack([g5[:, a, :, a, :] for a in range(4)], axis=1).reshape(16, 96, 96)


def _spread(a):
    return a.transpose(1, 0, 2).reshape(a.shape[1], NSHARD * a.shape[2])


def _split(a):
    return a.reshape(a.shape[0], NSHARD, a.shape[1] // NSHARD).transpose(1, 0, 2)


class _Reduction:
    def __init__(self, dist, parts, names):
        self.c, self.parts, self.names = dist.c, parts, names

    def swap(self):
        return _swap_halves(self.parts)

    def scatter(self, recv):
        return _scatter_chips([_add_sibling(p, r, self.c, "add_sibling_" + n)
                               for p, r, n in zip(self.parts, recv, self.names)])

    def share(self, landed):
        return _share_halves([_sum_chips(a, self.c, "sum_chips_" + n) for a, n in zip(landed, self.names)])

    def done(self, shared):
        return [a.reshape(2 * a.shape[1], a.shape[2]) for a in shared]


class _Dist:
    def __init__(self, w_in_shard, late_shards):
        self.c = lax.axis_index("c").astype(jnp.int32).reshape(1)
        self.w_in_shard = [w_in_shard]
        self.late_shards = late_shards

    def w_in_ride(self):
        return _gather_shards(self.w_in_shard, [True])

    def w_in_arrived(self, got):
        (g_in,) = _with_own_slot(self.w_in_shard, got)
        return _reorder_w_in(_spread(g_in))

    def weights_ride(self):
        return _gather_shards(self.late_shards, [True, True, False])

    def weights_arrived(self, got):
        g_kv, g_rows, g_small = _with_own_slot(self.late_shards, got)
        out = {"w_kv": _spread(g_kv)}
        for n, lo_, hi_ in ROW_PIECES:
            out[n] = g_rows[:, lo_:hi_].reshape(NSHARD * (hi_ - lo_), D)
        out["ssd_conv_w"] = _spread(g_small[:, :, 0:768])
        out["ssd_norm_g"] = _spread(g_small[:, :, 768:896])
        out["lru_conv_w"] = _spread(g_small[:, :, 896:1280])
        return out

    def early_parts(self, grads):
        rows = jnp.concatenate([grads[n].reshape(NSHARD, hi_ - lo_, D) for n, lo_, hi_ in ROW_PIECES], axis=1)
        return [_split(grads["w_kv"]), rows]

    def late_parts(self, grads):
        return [_split(_restore_w_in(grads["w_in_r"]))]


def _local_grads(x, mem, target, wts, dist=None):
    pad128 = lambda a: jnp.pad(a, ((0, 0), (0, 128 - a.shape[1])))

    if dist is None:
        (h,), _ = _norm_fwd(x, wts["norm_g"])
        w_in_r = wts["w_in_r"]
        proj = _mm(h, w_in_r, F32, "in_proj", tn=NP_TILE)
    else:
        (h,), arrived = _norm_fwd(x, wts["norm_g"], ride=dist.w_in_ride())
        w_in_r = dist.w_in_arrived(arrived)
        proj, arrived = _mm(h, w_in_r, F32, "in_proj", tn=NP_TILE, ride=dist.weights_ride())
        wts = dict(wts, **dist.weights_arrived(arrived))
    wbs, wbl, wbm, wo, wkv = wts["w_br_ssd"], wts["w_br_lru"], wts["w_br_mem"], wts["w_out"], wts["w_kv"]
    wa, wx = _mx(_lru_group_weights(wts["lru_w_a"])), _mx(_lru_group_weights(wts["lru_w_x"]))
    ba, bx = wts["lru_b_a"].reshape(1, LRU_W), wts["lru_b_x"].reshape(1, LRU_W)
    dtb, alog = pad128(wts["ssd_dt_bias"]), pad128(wts["ssd_a_log"])
    dexp = jnp.repeat(wts["ssd_d"], 64, axis=1)
    ng = wts["ssd_norm_g"].reshape(1, SSD_W)
    xbc = _conv_fwd(proj, XBC, wts["ssd_conv_w"], wts["ssd_conv_b"], True, "ssd_conv_fwd")
    yssd, yraw, hprev = _ssd_fwd(xbc, proj, dtb, alog, dexp, ng)
    xl = _conv_fwd(proj, LX, wts["lru_conv_w"], wts["lru_conv_b"], False, "lru_conv_fwd")
    ylru, hs = _lru_fwd(xl, proj, wa, wx, ba, bx, wts["lru_lambda"])
    kk, vv, mn = _mem_kv_fwd(mem, wts["mem_norm_g"], wkv)
    ymem = _attn_fwd(proj, kk, vv)

    dproj, dx2, dx2m, merged, db0, db1, db2, loss_vec, dfg = _merge_fb(
        x, target, yssd, ylru, ymem, proj, wbs, wbl, wbm, wo, wts["final_g"].reshape(1, D))
    grads = {"final_g": dfg.reshape(D)}
    grads["w_out"] = _mm(merged, dx2m, _MXU, "dw_out", ta=True)
    grads["w_br_ssd"] = _mm(yssd, db0, _MXU, "dw_br_ssd", ta=True)
    grads["w_br_lru"] = _mm(ylru, db1, _MXU, "dw_br_lru", ta=True)
    grads["w_br_mem"] = _mm(ymem, db2, _MXU, "dw_br_mem", ta=True)
    dyssd = _mm(db0, wbs, F32, "dy_ssd", tb=True)
    dylru = _mm(db1, wbl, F32, "dy_lru", tb=True)
    dymem = _mm(db2, wbm, F32, "dy_mem", tb=True)

    dproj, dk, dv = _attn_bwd(proj, kk, vv, dymem, dproj)
    grads["w_kv"], grads["mem_norm_g"] = _mem_kv_bwd(mem, wts["mem_norm_g"], mn, wkv, dk, dv)

    early = None if dist is None else _Reduction(dist, dist.early_parts(grads), ["w_kv", "rows"])

    (dproj, dxl, dwa, dwx, dba, dbx, dlam), got = _lru_bwd(
        xl, proj, hs, dylru, dproj, wa, wx, ba, bx, wts["lru_lambda"], ride=early and early.swap())
    grads["lru_w_a"] = _lru_group_blocks(dwa)[None]
    grads["lru_w_x"] = _lru_group_blocks(dwx)[None]
    grads["lru_b_a"], grads["lru_b_x"] = dba.reshape(1, 16, 96), dbx.reshape(1, 16, 96)
    grads["lru_lambda"] = dlam
    (grads["lru_conv_w"], grads["lru_conv_b"]), _ = _conv_bwd_w(
        proj, LX, wts["lru_conv_w"], wts["lru_conv_b"], dxl, False, "lru_conv_bwd_w")
    dproj = _conv_bwd_x(dxl, wts["lru_conv_w"], dproj, LX, "lru_conv_bwd_x")

    (dproj, ddt, dxbc, dng, dda, ddd, ddtb), got = _ssd_bwd(
        xbc, proj, yraw, hprev, dyssd, dproj, dtb, alog, dexp, ng, ride=early and early.scatter(got))
    dproj = _put_block(ddt, dproj, DT, "put_ddt")
    grads["ssd_norm_g"] = dng.reshape(4, 512)
    grads["ssd_dt_bias"] = ddtb[:, 0:32]
    grads["ssd_a_log"] = (dda * -jnp.exp(alog))[:, 0:32]
    grads["ssd_d"] = ddd.reshape(32, 64).sum(axis=1)[None, :]
    (dpre, grads["ssd_conv_w"], grads["ssd_conv_b"]), got = _conv_bwd_w(
        proj, XBC, wts["ssd_conv_w"], wts["ssd_conv_b"], dxbc, True, "ssd_conv_bwd_w", ride=early and early.share(got))
    reduced = {} if dist is None else dict(zip(["w_kv", "rows"], early.done(got)))
    dproj = _conv_bwd_x(dpre, wts["ssd_conv_w"], dproj, XBC, "ssd_conv_bwd_x")

    grads["w_in_r"] = _mm(h, dproj, _MXU, "dw_in", ta=True, tn=NP_TILE)
    if dist is None:
        dh = _mm(dproj, w_in_r, F32, "dh", tb=True, tn=1024, tk=NP_TILE)
        (grad_x, grads["norm_g"]), _ = _norm_bwd(x, wts["norm_g"], dh, dx2)
    else:
        late = _Reduction(dist, dist.late_parts(grads), ["w_in"])
        got = _run_exchange(late.swap(), "swap_halves_w_in")
        dh, got = _mm(dproj, w_in_r, F32, "dh", tb=True, tn=1024, tk=NP_TILE, ride=late.scatter(got))
        (grad_x, grads["norm_g"]), got = _norm_bwd(x, wts["norm_g"], dh, dx2, ride=late.share(got))
        reduced["w_in"] = late.done(got)[0]
    return jnp.sum(loss_vec), grad_x, grads, reduced


def kernel(x, mem, norm_g, w_in, ssd_conv_w, ssd_conv_b, ssd_dt_bias, ssd_a_log, ssd_d, ssd_norm_g, lru_conv_w, lru_conv_b, lru_w_a, lru_b_a, lru_w_x, lru_b_x, lru_lambda, mem_norm_g, w_kv, w_br_ssd, w_br_lru, w_br_mem, w_out, final_g, loss_target, m_norm_g, m_w_in, m_ssd_conv_w, m_ssd_conv_b, m_ssd_dt_bias, m_ssd_a_log, m_ssd_d, m_ssd_norm_g, m_lru_conv_w, m_lru_conv_b, m_lru_w_a, m_lru_b_a, m_lru_w_x, m_lru_b_x, m_lru_lambda, m_mem_norm_g, m_w_kv, m_w_br_ssd, m_w_br_lru, m_w_br_mem, m_w_out, m_final_g, v_norm_g, v_w_in, v_ssd_conv_w, v_ssd_conv_b, v_ssd_dt_bias, v_ssd_a_log, v_ssd_d, v_ssd_norm_g, v_lru_conv_w, v_lru_conv_b, v_lru_w_a, v_lru_b_a, v_lru_w_x, v_lru_b_x, v_lru_lambda, v_mem_norm_g, v_w_kv, v_w_br_ssd, v_w_br_lru, v_w_br_mem, v_w_out, v_final_g):
    given = dict(locals())

    rows_w = jnp.concatenate([w_br_ssd[0], w_br_lru[0], w_br_mem[0], w_out[0]], axis=0)
    small_w = jnp.concatenate([ssd_conv_w[0], ssd_norm_g[0], lru_conv_w[0]], axis=1)
    dist = _Dist(_mx(w_in[0]), [_mx(w_kv[0]), _mx(rows_w), small_w])
    wts = {n: given[n] for n in REPL}
    wts["lru_w_a"], wts["lru_w_x"] = lru_w_a[0], lru_w_x[0]

    loss_part, grad_x, grads, reduced = _local_grads(x[0], mem[0], loss_target[0], wts, dist)
    loss = lax.psum(loss_part, ("x", "y", "c"))

    repl_flat = jnp.concatenate([grads[n].reshape(-1) for n in REPL])
    repl_flat = jnp.pad(repl_flat, (0, NSHARD * SMALL_Q - repl_flat.shape[0])).reshape(NSHARD, SMALL_Q)
    shard_small = jnp.concatenate([_split(grads[n]).reshape(NSHARD, -1) for n in SMALL_SHARDED], axis=1)
    p_small = jnp.concatenate(
        [repl_flat, shard_small, jnp.zeros((NSHARD, SMALL_BUF_ROWS * PACK_W - SMALL_Q - 5120), F32)], axis=1)
    small = _Reduction(dist, [p_small.reshape(NSHARD, SMALL_BUF_ROWS, PACK_W)], ["small"])
    got = _run_exchange(small.swap(), "swap_halves_small")
    got = _run_exchange(small.scatter(got), "scatter_chips_small")
    got = _run_exchange(small.share(got), "share_halves_small")
    r_small = small.done(got)[0]
    repl_all = _run_exchange(_gather_small(r_small), "gather_small")[0].reshape(-1)

    g_shard = {"w_in": reduced["w_in"], "w_kv": reduced["w_kv"]}
    for n, lo_, hi_ in ROW_PIECES:
        g_shard[n] = reduced["rows"][lo_:hi_]
    g_shard.update(_unpack(r_small.reshape(-1)[SMALL_Q:], SMALL_SHARDED, SHARD_SHAPE))
    g_repl = _unpack(repl_all, REPL, REPL_SHAPE)

    out_g, out_d, out_m, out_v = {}, {}, {}, {}
    for n in WEIGHTS:
        w_full = given[n]
        g = (g_shard[n] if n in SHARDED else g_repl[n]).reshape(w_full.shape)
        cols = w_full.shape[-1]
        as2d = lambda a: a.reshape(-1, cols)
        d, mo, vo = _adamw(as2d(w_full), as2d(g), as2d(given["m_" + n]), as2d(given["v_" + n]), "adamw_" + n)
        out_g[n] = g
        out_d[n], out_m[n], out_v[n] = d.reshape(w_full.shape), mo.reshape(w_full.shape), vo.reshape(w_full.shape)

    return (loss, grad_x[None], *[out_g[n] for n in WEIGHTS], *[out_d[n] for n in WEIGHTS],
            *[out_m[n] for n in WEIGHTS], *[out_v[n] for n in WEIGHTS])
```

```python
import jax
import jax.numpy as jnp
from jax import lax
from jax.experimental import pallas as pl
from jax.experimental.pallas import tpu as pltpu

F32 = jnp.float32
_MXU = jnp.bfloat16
_HI = lax.Precision.HIGHEST
MESH = pl.DeviceIdType.MESH

D = 1024
EPS = 1e-6
MEM_HEADS = 4
MEM_HD = 256
LRU_C = 8.0
SSD_L = 128
SSD_W = 2048
LRU_W = 1536
NSHARD = 4

XBC = (0, 3072)
GL = (3072, 3072)
Z = (6144, 2048)
Q = (8192, 1024)
LG = (9216, 1536)
LX = (10752, 1536)
DT = (12288, 256)
NP = 12544
NP_TILE = 1792

ADAM_LR = 0.001
ADAM_B1 = 0.9
ADAM_B2 = 0.999
ADAM_EPS = 1e-08
ADAM_WD = 0.01
ADAM_STEP = 10

VMEM_LIMIT = 56 * 1024 * 1024

SHARDED = ("w_in", "ssd_conv_w", "ssd_norm_g", "lru_conv_w", "w_kv", "w_br_ssd", "w_br_lru", "w_br_mem", "w_out")
SHARD_SHAPE = {"w_in": (1024, 3080), "ssd_conv_w": (4, 768), "ssd_norm_g": (4, 128), "lru_conv_w": (4, 384),
               "w_kv": (1024, 512), "w_br_ssd": (512, 1024), "w_br_lru": (384, 1024), "w_br_mem": (256, 1024),
               "w_out": (256, 1024)}
REPL = ("norm_g", "ssd_conv_b", "ssd_dt_bias", "ssd_a_log", "ssd_d", "lru_conv_b", "lru_w_a", "lru_b_a",
        "lru_w_x", "lru_b_x", "lru_lambda", "mem_norm_g", "final_g")
REPL_SHAPE = {"norm_g": (1, 1024), "ssd_conv_b": (1, 3072), "ssd_dt_bias": (1, 32), "ssd_a_log": (1, 32),
              "ssd_d": (1, 32), "lru_conv_b": (1, 1536), "lru_w_a": (1, 16, 96, 96), "lru_b_a": (1, 16, 96),
              "lru_w_x": (1, 16, 96, 96), "lru_b_x": (1, 16, 96), "lru_lambda": (1, 1536),
              "mem_norm_g": (1, 1024), "final_g": (1024,)}
WEIGHTS = ("norm_g", "w_in", "ssd_conv_w", "ssd_conv_b", "ssd_dt_bias", "ssd_a_log", "ssd_d", "ssd_norm_g",
           "lru_conv_w", "lru_conv_b", "lru_w_a", "lru_b_a", "lru_w_x", "lru_b_x", "lru_lambda", "mem_norm_g",
           "w_kv", "w_br_ssd", "w_br_lru", "w_br_mem", "w_out", "final_g")

ROW_PIECES = (("w_br_ssd", 0, 512), ("w_br_lru", 512, 896), ("w_br_mem", 896, 1152), ("w_out", 1152, 1408))
SMALL_SHARDED = ("ssd_conv_w", "ssd_norm_g", "lru_conv_w")
PACK_W = 512
SMALL_ROWS = 152
SMALL_Q = SMALL_ROWS * PACK_W
SMALL_BUF_ROWS = 176


def _size(shape):
    n = 1
    for s in shape:
        n *= s
    return n


def _sigmoid(x):
    return 0.5 * jnp.tanh(0.5 * x) + 0.5


def _silu(x):
    return x * _sigmoid(x)


def _dsilu(x):
    s = _sigmoid(x)
    return s * (1.0 + x * (1.0 - s))


def _softplus(x):
    return jnp.maximum(x, 0.0) + jnp.log(1.0 + jnp.exp(-jnp.abs(x)))


def _one_minus_sq(log_a, a):
    x = 2.0 * log_a
    series = -x * (1.0 + x * (0.5 + x * (1.0 / 6.0 + x * (1.0 / 24.0))))
    return jnp.where(x > -0.03, series, 1.0 - a * a)


def _dot(a, b, precision=None):
    return jnp.dot(a, b, preferred_element_type=F32, precision=precision)


def _dot_nt(a, b):
    return lax.dot_general(a, b, (((1,), (1,)), ((), ())), preferred_element_type=F32)


def _dot_tn(a, b):
    return lax.dot_general(a, b, (((0,), (0,)), ((), ())), preferred_element_type=F32)


def _mx(a):
    return a.astype(_MXU)


def _cparams(sem):
    return pltpu.CompilerParams(dimension_semantics=sem, vmem_limit_bytes=VMEM_LIMIT)


def _tile(n, want, mult=128):
    if n <= want:
        return n
    for t in range(want - want % mult, 0, -mult):
        if n % t == 0:
            return t
    raise ValueError((n, want, mult))


def _mm(a, b, out_dtype, name, ta=False, tb=False, tm=1024, tn=1280, tk=1024, ride=None):
    k, m = a.shape if ta else a.shape[::-1]
    k2, n = b.shape[::-1] if tb else b.shape
    assert k == k2
    tm, tn, tk = _tile(m, tm), _tile(n, tn), _tile(k, tk)
    nk = k // tk
    contract = (((0 if ta else 1,), (1 if tb else 0,)), ((), ()))

    def body(a_ref, b_ref, o_ref, acc_ref):
        kk = pl.program_id(2)

        @pl.when(kk == 0)
        def _():
            acc_ref[...] = jnp.zeros_like(acc_ref)

        acc_ref[...] += lax.dot_general(a_ref[...], b_ref[...], contract, preferred_element_type=F32)

        @pl.when(kk == nk - 1)
        def _():
            o_ref[...] = acc_ref[...].astype(o_ref.dtype)

    a_spec = pl.BlockSpec((tk, tm), lambda i, j, kk: (kk, i)) if ta else pl.BlockSpec((tm, tk), lambda i, j, kk: (i, kk))
    b_spec = pl.BlockSpec((tn, tk), lambda i, j, kk: (j, kk)) if tb else pl.BlockSpec((tk, tn), lambda i, j, kk: (kk, j))
    outs, carried = _pcall(
        body, ride, (a, b), grid=(m // tm, n // tn, nk),
        in_specs=[a_spec, b_spec],
        out_specs=[pl.BlockSpec((tm, tn), lambda i, j, kk: (i, j))],
        out_shape=[jax.ShapeDtypeStruct((m, n), out_dtype)],
        scratch_shapes=[pltpu.VMEM((tm, tn), F32)],
        sem=("parallel", "parallel", "arbitrary"), name=name)
    return outs[0] if ride is None else (outs[0], carried)


def _norm_fwd(x, g, ride=None):
    s = x.shape[0]
    ts = _tile(s, 512)

    def body(x_ref, g_ref, h_ref):
        xv = x_ref[...]
        r = lax.rsqrt(jnp.mean(xv * xv, axis=-1, keepdims=True) + EPS)
        h_ref[...] = (xv * r * g_ref[...]).astype(h_ref.dtype)

    return _pcall(
        body, ride, (x, g), grid=(s // ts,),
        in_specs=[pl.BlockSpec((ts, D), lambda i: (i, 0)), pl.BlockSpec((1, D), lambda i: (0, 0))],
        out_specs=[pl.BlockSpec((ts, D), lambda i: (i, 0))],
        out_shape=[jax.ShapeDtypeStruct((s, D), _MXU)], scratch_shapes=[], sem=("parallel",), name="norm_fwd")


def _norm_bwd(x, g, dh, dx2, ride=None):
    s = x.shape[0]
    ts = _tile(s, 512)

    def body(x_ref, g_ref, dh_ref, dx2_ref, gx_ref, dg_ref):
        @pl.when(pl.program_id(0) == 0)
        def _():
            dg_ref[...] = jnp.zeros_like(dg_ref)

        xv = x_ref[...]
        r = lax.rsqrt(jnp.mean(xv * xv, axis=-1, keepdims=True) + EPS)
        xhat = xv * r
        dh_v = dh_ref[...]
        dg_ref[...] += jnp.sum(dh_v * xhat, axis=0, keepdims=True)
        dxh = dh_v * g_ref[...]
        gx_ref[...] = dx2_ref[...] + r * (dxh - xhat * jnp.mean(dxh * xhat, axis=-1, keepdims=True))

    row = pl.BlockSpec((ts, D), lambda i: (i, 0))
    vec = pl.BlockSpec((1, D), lambda i: (0, 0))
    return _pcall(
        body, ride, (x, g, dh, dx2), grid=(s // ts,), in_specs=[row, vec, row, row], out_specs=[row, vec],
        out_shape=[jax.ShapeDtypeStruct((s, D), F32), jax.ShapeDtypeStruct((1, D), F32)],
        scratch_shapes=[], sem=("arbitrary",), name="norm_bwd")


CONV_RB = 16
CONV_LC = 256


def _fold8(v):
    acc = v[0:8]
    for r0 in range(8, v.shape[0], 8):
        acc = acc + v[r0:r0 + 8]
    return acc


def _conv_fwd(src, blk, w, b, act, name):
    s = src.shape[0]
    off, width = blk
    cb = off // width
    ts = _tile(s, 256)

    def body(x_ref, w_ref, b_ref, o_ref, ext_ref):
        @pl.when(pl.program_id(0) == 0)
        def _():
            ext_ref[0:8, :] = jnp.zeros((8, width), F32)

        ext_ref[8:8 + ts, :] = x_ref[...]
        for l0 in range(0, width, CONV_LC):
            ls = slice(l0, l0 + CONV_LC)
            taps = [w_ref[k:k + 1, ls] for k in range(4)]
            bias = b_ref[:, ls]
            for r0 in range(0, ts, CONV_RB):
                pre = bias
                for k in range(4):
                    pre = pre + taps[k] * ext_ref[5 + k + r0:5 + k + r0 + CONV_RB, ls]
                o_ref[r0:r0 + CONV_RB, ls] = _silu(pre) if act else pre
        ext_ref[0:8, :] = x_ref[ts - 8:ts, :]

    return pl.pallas_call(
        body, grid=(s // ts,),
        in_specs=[pl.BlockSpec((ts, width), lambda i: (i, cb)), pl.BlockSpec((4, width), lambda i: (0, 0)),
                  pl.BlockSpec((1, width), lambda i: (0, 0))],
        out_specs=pl.BlockSpec((ts, width), lambda i: (i, 0)),
        out_shape=jax.ShapeDtypeStruct((s, width), F32),
        scratch_shapes=[pltpu.VMEM((ts + 8, width), F32)],
        compiler_params=_cparams(("arbitrary",)), name=name)(src, w, b)


def _conv_bwd_w(src, blk, w, b, dout, act, name, ride=None):
    s = src.shape[0]
    off, width = blk
    cb = off // width
    ts = _tile(s, 256)

    def body(x_ref, w_ref, b_ref, do_ref, *rest):
        if act:
            dpre_ref, dw_ref, db_ref, ext_ref = rest
        else:
            dw_ref, db_ref, ext_ref = rest

        @pl.when(pl.program_id(0) == 0)
        def _():
            ext_ref[0:8, :] = jnp.zeros((8, width), F32)
            dw_ref[...] = jnp.zeros_like(dw_ref)
            db_ref[...] = jnp.zeros_like(db_ref)

        ext_ref[8:8 + ts, :] = x_ref[...]
        for l0 in range(0, width, CONV_LC):
            ls = slice(l0, l0 + CONV_LC)
            taps = [w_ref[k:k + 1, ls] for k in range(4)]
            bias = b_ref[:, ls]
            acc_b = jnp.zeros((8, CONV_LC), F32)
            acc_w = [jnp.zeros((8, CONV_LC), F32) for _ in range(4)]
            for r0 in range(0, ts, CONV_RB):
                xs = [ext_ref[5 + k + r0:5 + k + r0 + CONV_RB, ls] for k in range(4)]
                dpre = do_ref[r0:r0 + CONV_RB, ls]
                if act:
                    pre = bias
                    for k in range(4):
                        pre = pre + taps[k] * xs[k]
                    dpre = dpre * _dsilu(pre)
                    dpre_ref[r0:r0 + CONV_RB, ls] = dpre
                acc_b = acc_b + _fold8(dpre)
                for k in range(4):
                    acc_w[k] = acc_w[k] + _fold8(dpre * xs[k])
            db_ref[:, ls] += jnp.sum(acc_b, axis=0, keepdims=True)
            for k in range(4):
                dw_ref[k:k + 1, ls] += jnp.sum(acc_w[k], axis=0, keepdims=True)
        ext_ref[0:8, :] = x_ref[ts - 8:ts, :]

    row = pl.BlockSpec((ts, width), lambda i: (i, 0))
    outs = [pl.BlockSpec((4, width), lambda i: (0, 0)), pl.BlockSpec((1, width), lambda i: (0, 0))]
    shapes = [jax.ShapeDtypeStruct((4, width), F32), jax.ShapeDtypeStruct((1, width), F32)]
    if act:
        outs = [row] + outs
        shapes = [jax.ShapeDtypeStruct((s, width), F32)] + shapes
    return _pcall(
        body, ride, (src, w, b, dout), grid=(s // ts,),
        in_specs=[pl.BlockSpec((ts, width), lambda i: (i, cb)), pl.BlockSpec((4, width), lambda i: (0, 0)),
                  pl.BlockSpec((1, width), lambda i: (0, 0)), row],
        out_specs=outs, out_shape=shapes,
        scratch_shapes=[pltpu.VMEM((ts + 8, width), F32)], sem=("arbitrary",), name=name)


def _conv_bwd_x(dpre, w, dproj, blk, name):
    s = dpre.shape[0]
    off, width = blk
    cb = off // width
    ts = _tile(s, 256)
    nt = s // ts

    def body(dp_ref, w_ref, dproj_hbm, o_ref, ext_ref):
        del dproj_hbm

        @pl.when(pl.program_id(0) == 0)
        def _():
            ext_ref[ts:ts + 8, :] = jnp.zeros((8, width), F32)

        ext_ref[0:ts, :] = dp_ref[...]
        for l0 in range(0, width, CONV_LC):
            ls = slice(l0, l0 + CONV_LC)
            taps = [w_ref[k:k + 1, ls] for k in range(4)]
            for r0 in range(0, ts, CONV_RB):
                acc = taps[0] * ext_ref[3 + r0:3 + r0 + CONV_RB, ls]
                for k in range(1, 4):
                    acc = acc + taps[k] * ext_ref[3 - k + r0:3 - k + r0 + CONV_RB, ls]
                o_ref[r0:r0 + CONV_RB, ls] = acc.astype(o_ref.dtype)
        ext_ref[ts:ts + 8, :] = dp_ref[0:8, :]

    return pl.pallas_call(
        body, grid=(nt,),
        in_specs=[pl.BlockSpec((ts, width), lambda i: (nt - 1 - i, 0)), pl.BlockSpec((4, width), lambda i: (0, 0)),
                  pl.BlockSpec(memory_space=pl.ANY)],
        out_specs=pl.BlockSpec((ts, width), lambda i: (nt - 1 - i, cb)),
        out_shape=jax.ShapeDtypeStruct(dproj.shape, dproj.dtype),
        scratch_shapes=[pltpu.VMEM((ts + 8, width), F32)],
        input_output_aliases={2: 0},
        compiler_params=_cparams(("arbitrary",)), name=name)(dpre, w, dproj)


def _ssd_decay(a_cs, acst_ref, h, causal, lane_l):
    col = jnp.sum(jnp.where(lane_l == h, a_cs, 0.0), axis=1, keepdims=True)
    row = acst_ref[h:h + 1, :]
    return jnp.where(causal, jnp.exp(jnp.minimum(col - row, 0.0)), 0.0)


def _split3(x):
    hi = x.astype(jnp.bfloat16)
    rest = x - hi.astype(F32)
    mid = rest.astype(jnp.bfloat16)
    return jnp.concatenate([hi, mid, (rest - mid.astype(F32)).astype(jnp.bfloat16)], axis=1)


def _spread_matrix():
    col = jnp.arange(128, dtype=jnp.int32)[:, None]
    e64 = (col == jnp.arange(SSD_W, dtype=jnp.int32)[None, :] // 64).astype(jnp.bfloat16)
    return jnp.tile(e64, (3, 1))


def _ssd_common(dt_ref, dtb_ref, alog_ref, e64_ref, acst_ref, dtx_ref, acx_ref):
    ll = SSD_L
    dt = _softplus(dt_ref[:, 0:128] + dtb_ref[...])
    a_neg = -jnp.exp(alog_ref[...])
    ri = lax.broadcasted_iota(jnp.int32, (ll, ll), 0)
    ci = lax.broadcasted_iota(jnp.int32, (ll, ll), 1)
    causal = ri >= ci
    a_cs = _dot(causal.astype(F32), dt * a_neg, _HI)
    acst_ref[...] = a_cs.T
    both = _dot(jnp.concatenate([_split3(dt), _split3(a_cs)], axis=0), e64_ref[...])
    dtx_ref[...] = both[0:ll]
    acx_ref[...] = both[ll:2 * ll]
    lane_l = lax.broadcasted_iota(jnp.int32, (ll, 128), 1)
    return dt, a_neg, a_cs, causal, ri, lane_l, lane_l < 64


def _ssd_fwd(xbc, proj, dtb, alog, dexp, ng):
    s = xbc.shape[0]
    ll = SSD_L
    nc = s // ll
    e64 = _spread_matrix()

    def body(xbc_ref, dt_ref, z_ref, dtb_ref, alog_ref, dexp_ref, ng_ref, e64_ref,
             yssd_ref, yraw_ref, hprev_ref, ht_ref, acst_ref, dtx_ref, acx_ref):
        @pl.when(pl.program_id(0) == 0)
        def _():
            ht_ref[...] = jnp.zeros_like(ht_ref)

        hprev_ref[0] = ht_ref[...]
        _, _, a_cs, causal, _, lane_l, lo = _ssd_common(dt_ref, dtb_ref, alog_ref, e64_ref, acst_ref, dtx_ref, acx_ref)
        for g in range(4):
            bg = _mx(xbc_ref[:, 2048 + 128 * g:2176 + 128 * g])
            cg = _mx(xbc_ref[:, 2560 + 128 * g:2688 + 128 * g])
            cbm = _dot_nt(cg, bg)
            for jj in range(4):
                j = 4 * g + jj
                sl = slice(128 * j, 128 * j + 128)
                xp = xbc_ref[:, sl]
                acx = acx_ref[:, sl]
                a_last = acx_ref[ll - 1:ll, sl]
                xdt = xp * dtx_ref[:, sl]
                acc = None
                for hh in range(2):
                    dec = _ssd_decay(a_cs, acst_ref, 2 * j + hh, causal, lane_l)
                    xm = jnp.where(lo if hh == 0 else jnp.logical_not(lo), xdt, 0.0)
                    t = _dot(_mx(dec * cbm), _mx(xm))
                    acc = t if acc is None else acc + t
                ht = ht_ref[j]
                y = acc + _dot(cg, _mx(ht)) * jnp.exp(acx) + xp * dexp_ref[:, sl]
                yraw_ref[:, sl] = y
                st = _dot_tn(bg, _mx(xdt * jnp.exp(a_last - acx)))
                ht_ref[j] = ht * jnp.exp(a_last) + st
        for g in range(4):
            sl = slice(512 * g, 512 * g + 512)
            yg = yraw_ref[:, sl] * _silu(z_ref[:, sl])
            r = lax.rsqrt(jnp.mean(yg * yg, axis=-1, keepdims=True) + EPS)
            yssd_ref[:, sl] = (yg * r * ng_ref[:, sl]).astype(yssd_ref.dtype)

    vec = lambda w: pl.BlockSpec((1, w), lambda c: (0, 0))
    return pl.pallas_call(
        body, grid=(nc,),
        in_specs=[pl.BlockSpec((ll, 3072), lambda c: (c, 0)),
                  pl.BlockSpec((ll, DT[1]), lambda c: (c, DT[0] // DT[1])),
                  pl.BlockSpec((ll, Z[1]), lambda c: (c, Z[0] // Z[1])),
                  vec(128), vec(128), vec(2048), vec(2048),
                  pl.BlockSpec(e64.shape, lambda c: (0, 0))],
        out_specs=[pl.BlockSpec((ll, 2048), lambda c: (c, 0)), pl.BlockSpec((ll, 2048), lambda c: (c, 0)),
                   pl.BlockSpec((1, 16, 128, 128), lambda c: (c, 0, 0, 0))],
        out_shape=[jax.ShapeDtypeStruct((s, 2048), _MXU), jax.ShapeDtypeStruct((s, 2048), F32),
                   jax.ShapeDtypeStruct((nc, 16, 128, 128), F32)],
        scratch_shapes=[pltpu.VMEM((16, 128, 128), F32), pltpu.VMEM((128, ll), F32),
                        pltpu.VMEM((ll, 2048), F32), pltpu.VMEM((ll, 2048), F32)],
        compiler_params=_cparams(("arbitrary",)), name="ssd_fwd")(xbc, proj, proj, dtb, alog, dexp, ng, e64)


def _ssd_bwd(xbc, proj, yraw, hprev, dyssd, dproj, dtb, alog, dexp, ng, ride=None):
    s = xbc.shape[0]
    ll = SSD_L
    nc = s // ll
    e64 = _spread_matrix()

    def body(xbc_ref, dt_ref, z_ref, yraw_ref, hprev_ref, dy_ref, dproj_hbm, dtb_ref, alog_ref, dexp_ref, ng_ref,
             e64_ref,
             dz_ref, ddt_ref, dxbc_ref, dng_ref, dda_ref, ddd_ref, ddtb_ref,
             dht_ref, acst_ref, dtx_ref, acx_ref, dyr_ref, rowt_ref):
        del dproj_hbm

        @pl.when(pl.program_id(0) == 0)
        def _():
            dht_ref[...] = jnp.zeros_like(dht_ref)
            dng_ref[...] = jnp.zeros_like(dng_ref)
            dda_ref[...] = jnp.zeros_like(dda_ref)
            ddd_ref[...] = jnp.zeros_like(ddd_ref)
            ddtb_ref[...] = jnp.zeros_like(ddtb_ref)
            rowt_ref[...] = jnp.zeros_like(rowt_ref)

        for g in range(4):
            sl = slice(512 * g, 512 * g + 512)
            zz = z_ref[:, sl]
            yr = yraw_ref[:, sl]
            sz = _silu(zz)
            yg = yr * sz
            r = lax.rsqrt(jnp.mean(yg * yg, axis=-1, keepdims=True) + EPS)
            yhat = yg * r
            dyv = dy_ref[:, sl]
            dng_ref[:, sl] += jnp.sum(dyv * yhat, axis=0, keepdims=True)
            dyh = dyv * ng_ref[:, sl]
            dyg = r * (dyh - yhat * jnp.mean(dyh * yhat, axis=-1, keepdims=True))
            dz_ref[:, sl] = (dyg * yr * _dsilu(zz)).astype(dz_ref.dtype)
            dyr_ref[:, sl] = dyg * sz

        dt, a_neg, a_cs, causal, ri, lane_l, lo = _ssd_common(dt_ref, dtb_ref, alog_ref, e64_ref,
                                                              acst_ref, dtx_ref, acx_ref)
        lane_1 = lax.broadcasted_iota(jnp.int32, (1, 128), 1)
        da_col = jnp.zeros((ll, 128), F32)
        ddt_x = jnp.zeros((ll, 128), F32)
        last = jnp.zeros((1, 128), F32)
        for g in range(4):
            bg = _mx(xbc_ref[:, 2048 + 128 * g:2176 + 128 * g])
            cg = _mx(xbc_ref[:, 2560 + 128 * g:2688 + 128 * g])
            cbm = _dot_nt(cg, bg)
            dcb = jnp.zeros((ll, ll), F32)
            db_g = jnp.zeros((ll, 128), F32)
            dc_g = jnp.zeros((ll, 128), F32)
            for jj in range(4):
                j = 4 * g + jj
                sl = slice(128 * j, 128 * j + 128)
                xp = xbc_ref[:, sl]
                dtx = dtx_ref[:, sl]
                acx = acx_ref[:, sl]
                a_last = acx_ref[ll - 1:ll, sl]
                ea = jnp.exp(acx)
                dte = jnp.exp(a_last - acx)
                cd = jnp.exp(a_last)
                xdt = xp * dtx
                xdt_m = _mx(xdt)
                dy = dyr_ref[:, sl]
                ht = hprev_ref[0, j]
                dhn = dht_ref[j]
                dhn_m = _mx(dhn)
                gmat = _dot(bg, dhn_m)
                dxdt = gmat * dte
                for hh in range(2):
                    h = 2 * j + hh
                    dec = _ssd_decay(a_cs, acst_ref, h, causal, lane_l)
                    mm = dec * cbm
                    dym = _mx(jnp.where(lo if hh == 0 else jnp.logical_not(lo), dy, 0.0))
                    dxdt = dxdt + _dot_tn(_mx(mm), dym)
                    dm = _dot_nt(dym, xdt_m)
                    dcb = dcb + dm * dec
                    qq = dm * mm
                    da_col = da_col + jnp.where(lane_l == h, jnp.sum(qq, axis=1, keepdims=True), 0.0)
                    rowt_ref[h:h + 1, :] = jnp.sum(qq, axis=0, keepdims=True)
                ch = _dot(cg, _mx(ht))
                dyea = dy * ea
                dyea_m = _mx(dyea)
                xw_m = _mx(xdt * dte)
                dc_g = dc_g + _dot_nt(dyea_m, _mx(ht))
                db_g = db_g + _dot_nt(xw_m, dhn_m)
                wl = xdt * gmat * dte
                lane_a = dyea * ch - wl
                lane_b = dxdt * xp
                lane_c = jnp.sum(dhn * ht, axis=0, keepdims=True) * cd + jnp.sum(wl, axis=0, keepdims=True)
                for hh in range(2):
                    h = 2 * j + hh
                    mine = lo if hh == 0 else jnp.logical_not(lo)
                    da_col = da_col + jnp.where(
                        lane_l == h, jnp.sum(jnp.where(mine, lane_a, 0.0), axis=1, keepdims=True), 0.0)
                    ddt_x = ddt_x + jnp.where(
                        lane_l == h, jnp.sum(jnp.where(mine, lane_b, 0.0), axis=1, keepdims=True), 0.0)
                    mine_1 = (lane_1 < 64) if hh == 0 else (lane_1 >= 64)
                    last = last + jnp.where(
                        lane_1 == h, jnp.sum(jnp.where(mine_1, lane_c, 0.0), axis=1, keepdims=True), 0.0)
                dht_ref[j] = dhn * cd + _dot_tn(cg, dyea_m)
                dxbc_ref[:, sl] = dxdt * dtx + dy * dexp_ref[:, sl]
                ddd_ref[:, sl] += jnp.sum(dy * xp, axis=0, keepdims=True)
            dcb_m = _mx(dcb)
            dxbc_ref[:, 2048 + 128 * g:2176 + 128 * g] = db_g + _dot_tn(dcb_m, cg)
            dxbc_ref[:, 2560 + 128 * g:2688 + 128 * g] = dc_g + _dot(dcb_m, bg)

        da_cs = da_col - rowt_ref[...].T
        da_cs = da_cs + jnp.where(lax.broadcasted_iota(jnp.int32, (ll, 128), 0) == ll - 1, last, 0.0)
        d_dta = _dot((ri <= lax.broadcasted_iota(jnp.int32, (ll, ll), 1)).astype(F32), da_cs, _HI)
        ddt = d_dta * a_neg + ddt_x
        dda_ref[...] += jnp.sum(d_dta * dt, axis=0, keepdims=True)
        ddt_raw = ddt * _sigmoid(dt_ref[:, 0:128] + dtb_ref[...])
        ddtb_ref[...] += jnp.sum(ddt_raw, axis=0, keepdims=True)
        ddt_ref[:, 0:128] = ddt_raw.astype(ddt_ref.dtype)
        ddt_ref[:, 128:DT[1]] = jnp.zeros((ll, DT[1] - 128), ddt_ref.dtype)

    rev = lambda c: nc - 1 - c
    vec = lambda w: pl.BlockSpec((1, w), lambda c: (0, 0))
    row = lambda w: pl.BlockSpec((ll, w), lambda c: (rev(c), 0))
    return _pcall(
        body, ride, (xbc, proj, proj, yraw, hprev, dyssd, dproj, dtb, alog, dexp, ng, e64), grid=(nc,),
        in_specs=[row(3072),
                  pl.BlockSpec((ll, DT[1]), lambda c: (rev(c), DT[0] // DT[1])),
                  pl.BlockSpec((ll, Z[1]), lambda c: (rev(c), Z[0] // Z[1])),
                  row(2048),
                  pl.BlockSpec((1, 16, 128, 128), lambda c: (rev(c), 0, 0, 0)),
                  row(2048),
                  pl.BlockSpec(memory_space=pl.ANY),
                  vec(128), vec(128), vec(2048), vec(2048),
                  pl.BlockSpec(e64.shape, lambda c: (0, 0))],
        out_specs=[pl.BlockSpec((ll, Z[1]), lambda c: (rev(c), Z[0] // Z[1])),
                   row(DT[1]),
                   row(3072), vec(2048), vec(128), vec(2048), vec(128)],
        out_shape=[jax.ShapeDtypeStruct(dproj.shape, dproj.dtype), jax.ShapeDtypeStruct((s, DT[1]), dproj.dtype),
                   jax.ShapeDtypeStruct((s, 3072), F32), jax.ShapeDtypeStruct((1, 2048), F32),
                   jax.ShapeDtypeStruct((1, 128), F32), jax.ShapeDtypeStruct((1, 2048), F32),
                   jax.ShapeDtypeStruct((1, 128), F32)],
        scratch_shapes=[pltpu.VMEM((16, 128, 128), F32), pltpu.VMEM((128, ll), F32),
                        pltpu.VMEM((ll, 2048), F32), pltpu.VMEM((ll, 2048), F32), pltpu.VMEM((ll, 2048), F32),
                        pltpu.VMEM((128, ll), F32)],
        aliases={6: 0}, sem=("arbitrary",), name="ssd_bwd")


def _put_block(src, dproj, blk, name):
    s = src.shape[0]
    off, width = blk
    cb = off // width
    ts = _tile(s, 1024)

    def body(s_ref, dproj_hbm, o_ref):
        del dproj_hbm
        o_ref[...] = s_ref[...]

    return pl.pallas_call(
        body, grid=(s // ts,),
        in_specs=[pl.BlockSpec((ts, width), lambda i: (i, 0)), pl.BlockSpec(memory_space=pl.ANY)],
        out_specs=pl.BlockSpec((ts, width), lambda i: (i, cb)),
        out_shape=jax.ShapeDtypeStruct(dproj.shape, dproj.dtype),
        input_output_aliases={1: 0},
        compiler_params=_cparams(("parallel",)), name=name)(src, dproj)


LRU_G = 384


def _lru_gates(xl_ref, wa_ref, wx_ref, ba_ref, bx_ref, lam_ref, g):
    sl = slice(LRU_G * g, LRU_G * g + LRU_G)
    xg = xl_ref[:, sl]
    xm = _mx(xg)
    r = _sigmoid(_dot(xm, wa_ref[g]) + ba_ref[:, sl])
    ig = _sigmoid(_dot(xm, wx_ref[g]) + bx_ref[:, sl])
    sp = _softplus(-lam_ref[:, sl])
    log_a = (-LRU_C * r) * sp
    a = jnp.exp(log_a)
    mult = jnp.sqrt(_one_minus_sq(log_a, a))
    return sl, xg, r, ig, sp, a, mult


def _lru_fwd(xl, proj, wa, wx, ba, bx, lam):
    s = xl.shape[0]
    ts = _tile(s, 256)
    w = LRU_W

    def body(xl_ref, lg_ref, wa_ref, wx_ref, ba_ref, bx_ref, lam_ref, y_ref, hs_ref, a_ref, u_ref, carry_ref):
        @pl.when(pl.program_id(0) == 0)
        def _():
            carry_ref[...] = jnp.zeros_like(carry_ref)

        for g in range(4):
            sl, xg, _, ig, _, a, mult = _lru_gates(xl_ref, wa_ref, wx_ref, ba_ref, bx_ref, lam_ref, g)
            a_ref[:, sl] = a
            u_ref[:, sl] = mult * (ig * xg)

        def step(t, h):
            h = a_ref[pl.ds(t, 1), :] * h + u_ref[pl.ds(t, 1), :]
            hs_ref[pl.ds(t, 1), :] = h
            return h

        carry_ref[0:1, :] = lax.fori_loop(0, ts, step, carry_ref[0:1, :], unroll=8)
        y_ref[...] = (hs_ref[...] * _silu(lg_ref[...])).astype(y_ref.dtype)

    row = pl.BlockSpec((ts, w), lambda i: (i, 0))
    vec = pl.BlockSpec((1, w), lambda i: (0, 0))
    wsp = pl.BlockSpec((4, LRU_G, LRU_G), lambda i: (0, 0, 0))
    return pl.pallas_call(
        body, grid=(s // ts,),
        in_specs=[row, pl.BlockSpec((ts, w), lambda i: (i, LG[0] // w)), wsp, wsp, vec, vec, vec],
        out_specs=[row, row],
        out_shape=[jax.ShapeDtypeStruct((s, w), _MXU), jax.ShapeDtypeStruct((s, w), F32)],
        scratch_shapes=[pltpu.VMEM((ts, w), F32), pltpu.VMEM((ts, w), F32), pltpu.VMEM((8, w), F32)],
        compiler_params=_cparams(("arbitrary",)), name="lru_fwd")(xl, proj, wa, wx, ba, bx, lam)


def _lru_bwd(xl, proj, hs, dy, dproj, wa, wx, ba, bx, lam, ride=None):
    s = xl.shape[0]
    ts = _tile(s, 256)
    nt = s // ts
    w = LRU_W
    hb = ts // 8

    def body(xl_ref, lg_ref, hs_ref, hprev_ref, dy_ref, dproj_hbm, wa_ref, wx_ref, ba_ref, bx_ref, lam_ref,
             dlg_ref, dxl_ref, dwa_ref, dwx_ref, dba_ref, dbx_ref, dlam_ref,
             a_ref, dh_ref, ext_ref, carry_ref, r_ref, ig_ref, mult_ref):
        del dproj_hbm
        i = pl.program_id(0)

        @pl.when(i == 0)
        def _():
            carry_ref[...] = jnp.zeros_like(carry_ref)
            for ref in (dwa_ref, dwx_ref, dba_ref, dbx_ref, dlam_ref):
                ref[...] = jnp.zeros_like(ref)

        lg = lg_ref[...]
        dyv = dy_ref[...]
        dh_ref[...] = dyv * _silu(lg)
        dlg_ref[...] = (dyv * hs_ref[...] * _dsilu(lg)).astype(dlg_ref.dtype)
        for g in range(4):
            sl, _, r, ig, _, a, mult = _lru_gates(xl_ref, wa_ref, wx_ref, ba_ref, bx_ref, lam_ref, g)
            a_ref[:, sl] = a
            r_ref[:, sl] = r
            ig_ref[:, sl] = ig
            mult_ref[:, sl] = mult

        def step(k, carry):
            t = ts - 1 - k
            dh = dh_ref[pl.ds(t, 1), :] + carry
            dh_ref[pl.ds(t, 1), :] = dh
            return a_ref[pl.ds(t, 1), :] * dh

        carry_ref[0:1, :] = lax.fori_loop(0, ts, step, carry_ref[0:1, :], unroll=8)

        ext_ref[0:8, :] = jnp.where(i == nt - 1, 0.0, 1.0) * hprev_ref[...]
        ext_ref[8:8 + ts, :] = hs_ref[...]
        for g in range(4):
            sl = slice(LRU_G * g, LRU_G * g + LRU_G)
            xg, r, ig, a, mult = xl_ref[:, sl], r_ref[:, sl], ig_ref[:, sl], a_ref[:, sl], mult_ref[:, sl]
            sp = _softplus(-lam_ref[:, sl])
            dh = dh_ref[:, sl]
            da = dh * ext_ref[7:7 + ts, sl]
            dmult = dh * ig * xg
            di = dh * mult * xg
            dxl = dh * mult * ig
            dlog_a = da * a - dmult * (a * a) / mult
            dlam_ref[:, sl] += jnp.sum(dlog_a * r, axis=0, keepdims=True) * (LRU_C * _sigmoid(-lam_ref[:, sl]))
            dpa = dlog_a * (-LRU_C * sp) * r * (1.0 - r)
            dpx = di * ig * (1.0 - ig)
            dba_ref[:, sl] += jnp.sum(dpa, axis=0, keepdims=True)
            dbx_ref[:, sl] += jnp.sum(dpx, axis=0, keepdims=True)
            dpa_m, dpx_m, xm = _mx(dpa), _mx(dpx), _mx(xg)
            dxl_ref[:, sl] = dxl + _dot_nt(dpa_m, wa_ref[g]) + _dot_nt(dpx_m, wx_ref[g])
            dwa_ref[g] += _dot_tn(xm, dpa_m)
            dwx_ref[g] += _dot_tn(xm, dpx_m)

    rev = lambda i: nt - 1 - i
    row = pl.BlockSpec((ts, w), lambda i: (rev(i), 0))
    vec = pl.BlockSpec((1, w), lambda i: (0, 0))
    wsp = pl.BlockSpec((4, LRU_G, LRU_G), lambda i: (0, 0, 0))
    lgs = pl.BlockSpec((ts, w), lambda i: (rev(i), LG[0] // w))
    return _pcall(
        body, ride, (xl, proj, hs, hs, dy, dproj, wa, wx, ba, bx, lam), grid=(nt,),
        in_specs=[row, lgs, row, pl.BlockSpec((8, w), lambda i: (jnp.maximum(rev(i) * hb - 1, 0), 0)), row,
                  pl.BlockSpec(memory_space=pl.ANY), wsp, wsp, vec, vec, vec],
        out_specs=[lgs, row, wsp, wsp, vec, vec, vec],
        out_shape=[jax.ShapeDtypeStruct(dproj.shape, dproj.dtype), jax.ShapeDtypeStruct((s, w), F32),
                   jax.ShapeDtypeStruct((4, LRU_G, LRU_G), F32), jax.ShapeDtypeStruct((4, LRU_G, LRU_G), F32),
                   jax.ShapeDtypeStruct((1, w), F32), jax.ShapeDtypeStruct((1, w), F32),
                   jax.ShapeDtypeStruct((1, w), F32)],
        scratch_shapes=[pltpu.VMEM((ts, w), F32), pltpu.VMEM((ts, w), F32), pltpu.VMEM((ts + 8, w), F32),
                        pltpu.VMEM((8, w), F32), pltpu.VMEM((ts, w), F32), pltpu.VMEM((ts, w), F32),
                        pltpu.VMEM((ts, w), F32)],
        aliases={5: 0}, sem=("arbitrary",), name="lru_bwd")


def _mem_kv_fwd(mem, g, wkv):
    m = mem.shape[0]

    def body(mem_ref, g_ref, w_ref, k_ref, v_ref, mn_ref):
        mv = mem_ref[...]
        r = lax.rsqrt(jnp.mean(mv * mv, axis=-1, keepdims=True) + EPS)
        mn = _mx(mv * r * g_ref[...])
        mn_ref[...] = mn
        kv = _dot(mn, w_ref[...])
        k_ref[...] = kv[:, 0:D].astype(k_ref.dtype)
        v_ref[...] = kv[:, D:2 * D].astype(v_ref.dtype)

    sh = jax.ShapeDtypeStruct((m, D), _MXU)
    return pl.pallas_call(body, out_shape=[sh, sh, sh], compiler_params=_cparams(None), name="mem_kv_fwd")(mem, g, wkv)


def _mem_kv_bwd(mem, g, mn, wkv, dk, dv):
    m = mem.shape[0]

    def body(mem_ref, g_ref, mn_ref, w_ref, dk_ref, dv_ref, dw_ref, dg_ref):
        dkv = _mx(jnp.concatenate([dk_ref[...], dv_ref[...]], axis=1))
        dw_ref[...] = _dot_tn(mn_ref[...], dkv).astype(dw_ref.dtype)
        dmn = _dot_nt(dkv, w_ref[...])
        mv = mem_ref[...]
        r = lax.rsqrt(jnp.mean(mv * mv, axis=-1, keepdims=True) + EPS)
        dg_ref[...] = jnp.sum(dmn * mv * r, axis=0, keepdims=True)

    del m
    return pl.pallas_call(
        body, out_shape=[jax.ShapeDtypeStruct((D, 2 * D), _MXU), jax.ShapeDtypeStruct((1, D), F32)],
        compiler_params=_cparams(None), name="mem_kv_bwd")(mem, g, mn, wkv, dk, dv)


def _attn_probs(q_ref, k_ref, hd):
    sl = slice(MEM_HD * hd, MEM_HD * hd + MEM_HD)
    qh = _mx(q_ref[:, sl])
    sc = _dot_nt(qh, k_ref[:, sl]) * (MEM_HD ** -0.5)
    e = jnp.exp(sc - jnp.max(sc, axis=-1, keepdims=True))
    return sl, qh, e / jnp.sum(e, axis=-1, keepdims=True)


def _attn_fwd(proj, k, v):
    s = proj.shape[0]
    m = k.shape[0]
    ts = _tile(s, 512)

    def body(q_ref, k_ref, v_ref, y_ref):
        for hd in range(MEM_HEADS):
            sl, _, p = _attn_probs(q_ref, k_ref, hd)
            y_ref[:, sl] = _dot(_mx(p), v_ref[:, sl]).astype(y_ref.dtype)

    kvs = pl.BlockSpec((m, D), lambda i: (0, 0))
    return pl.pallas_call(
        body, grid=(s // ts,),
        in_specs=[pl.BlockSpec((ts, D), lambda i: (i, Q[0] // D)), kvs, kvs],
        out_specs=pl.BlockSpec((ts, D), lambda i: (i, 0)),
        out_shape=jax.ShapeDtypeStruct((s, D), _MXU),
        compiler_params=_cparams(("parallel",)), name="attn_fwd")(proj, k, v)


def _attn_bwd(proj, k, v, dy, dproj):
    s = proj.shape[0]
    m = k.shape[0]
    ts = _tile(s, 512)

    def body(q_ref, k_ref, v_ref, dy_ref, dproj_hbm, dq_ref, dk_ref, dv_ref):
        del dproj_hbm

        @pl.when(pl.program_id(0) == 0)
        def _():
            dk_ref[...] = jnp.zeros_like(dk_ref)
            dv_ref[...] = jnp.zeros_like(dv_ref)

        for hd in range(MEM_HEADS):
            sl, qh, p = _attn_probs(q_ref, k_ref, hd)
            dyh = _mx(dy_ref[:, sl])
            dp = _dot_nt(dyh, v_ref[:, sl])
            ds = _mx(p * (dp - jnp.sum(dp * p, axis=-1, keepdims=True)) * (MEM_HD ** -0.5))
            dq_ref[:, sl] = _dot(ds, k_ref[:, sl]).astype(dq_ref.dtype)
            dk_ref[:, sl] += _dot_tn(ds, qh)
            dv_ref[:, sl] += _dot_tn(_mx(p), dyh)

    kvs = pl.BlockSpec((m, D), lambda i: (0, 0))
    qs = pl.BlockSpec((ts, D), lambda i: (i, Q[0] // D))
    return pl.pallas_call(
        body, grid=(s // ts,),
        in_specs=[qs, kvs, kvs, pl.BlockSpec((ts, D), lambda i: (i, 0)), pl.BlockSpec(memory_space=pl.ANY)],
        out_specs=[qs, kvs, kvs],
        out_shape=[jax.ShapeDtypeStruct(dproj.shape, dproj.dtype), jax.ShapeDtypeStruct((m, D), F32),
                   jax.ShapeDtypeStruct((m, D), F32)],
        input_output_aliases={4: 0},
        compiler_params=_cparams(("arbitrary",)), name="attn_bwd")(proj, k, v, dy, dproj)


def _merge_fb(x, target, yssd, ylru, ymem, proj, wbs, wbl, wbm, wo, fg):
    s = x.shape[0]
    ts = _tile(s, 256)

    def body(x_ref, t_ref, ys_ref, yl_ref, ym_ref, gl_ref, wbs_ref, wbl_ref, wbm_ref, wo_ref, fg_ref,
             dgl_ref, dx2_ref, dx2m_ref, mg_ref, db0_ref, db1_ref, db2_ref, loss_ref, dfg_ref):
        @pl.when(pl.program_id(0) == 0)
        def _():
            loss_ref[...] = jnp.zeros_like(loss_ref)
            dfg_ref[...] = jnp.zeros_like(dfg_ref)

        bs = (_dot(ys_ref[...], wbs_ref[...]), _dot(yl_ref[...], wbl_ref[...]), _dot(ym_ref[...], wbm_ref[...]))
        gates = [_sigmoid(gl_ref[:, D * n:D * n + D]) for n in range(3)]
        merged = gates[0] * bs[0] + gates[1] * bs[1] + gates[2] * bs[2]
        mg = _mx(merged)
        mg_ref[...] = mg
        x2 = x_ref[...] + _dot(mg, wo_ref[...])
        r = lax.rsqrt(jnp.mean(x2 * x2, axis=-1, keepdims=True) + EPS)
        xhat = x2 * r
        err = xhat * fg_ref[...] - t_ref[...]
        loss_ref[...] += jnp.sum(err * err, axis=0, keepdims=True) * (0.5 / D)
        dy = err * (1.0 / D)
        dfg_ref[...] += jnp.sum(dy * xhat, axis=0, keepdims=True)
        dxh = dy * fg_ref[...]
        dx2 = r * (dxh - xhat * jnp.mean(dxh * xhat, axis=-1, keepdims=True))
        dx2_ref[...] = dx2
        dx2m = _mx(dx2)
        dx2m_ref[...] = dx2m
        dmg = _dot_nt(dx2m, wo_ref[...])
        for n, db_ref in enumerate((db0_ref, db1_ref, db2_ref)):
            gt = gates[n]
            dgl_ref[:, D * n:D * n + D] = (dmg * bs[n] * gt * (1.0 - gt)).astype(dgl_ref.dtype)
            db_ref[...] = (dmg * gt).astype(db_ref.dtype)

    row = lambda w: pl.BlockSpec((ts, w), lambda i: (i, 0))
    full = lambda a: pl.BlockSpec(a.shape, lambda i: (0, 0))
    vec = pl.BlockSpec((1, D), lambda i: (0, 0))
    gls = pl.BlockSpec((ts, GL[1]), lambda i: (i, GL[0] // GL[1]))
    act = jax.ShapeDtypeStruct((s, D), _MXU)
    return pl.pallas_call(
        body, grid=(s // ts,),
        in_specs=[row(D), row(D), row(SSD_W), row(LRU_W), row(D), gls, full(wbs), full(wbl), full(wbm), full(wo), vec],
        out_specs=[gls, row(D), row(D), row(D), row(D), row(D), row(D), vec, vec],
        out_shape=[jax.ShapeDtypeStruct((s, NP), _MXU), jax.ShapeDtypeStruct((s, D), F32), act, act, act, act, act,
                   jax.ShapeDtypeStruct((1, D), F32), jax.ShapeDtypeStruct((1, D), F32)],
        compiler_params=_cparams(("arbitrary",)), name="merge_fwd_bwd")(
            x, target, yssd, ylru, ymem, proj, wbs, wbl, wbm, wo, fg)


def _adamw(w, g, m, v, name):
    rows, cols = w.shape
    tr = _tile(rows, 512, 8)

    def body(w_ref, g_ref, m_ref, v_ref, d_ref, mo_ref, vo_ref):
        gv = g_ref[...]
        mn = ADAM_B1 * m_ref[...] + (1.0 - ADAM_B1) * gv
        vn = ADAM_B2 * v_ref[...] + (1.0 - ADAM_B2) * (gv * gv)
        m_hat = mn / (1.0 - ADAM_B1 ** ADAM_STEP)
        v_hat = vn / (1.0 - ADAM_B2 ** ADAM_STEP)
        d_ref[...] = -ADAM_LR * (m_hat / (jnp.sqrt(v_hat) + ADAM_EPS) + ADAM_WD * w_ref[...])
        mo_ref[...] = mn
        vo_ref[...] = vn

    blk = pl.BlockSpec((tr, cols), lambda i: (i, 0))
    sh = jax.ShapeDtypeStruct((rows, cols), F32)
    return pl.pallas_call(
        body, grid=(rows // tr,), in_specs=[blk] * 4, out_specs=[blk] * 3, out_shape=[sh] * 3,
        compiler_params=_cparams(("parallel",)), name=name)(w, g, m, v)


def _mesh_pos():
    x, y, c = lax.axis_index("x"), lax.axis_index("y"), lax.axis_index("c")
    chips = [(1 - x, y), (x, 1 - y), (1 - x, 1 - y)]
    return x, y, c, 2 * x + y, chips


def _hbm():
    return pl.BlockSpec(memory_space=pl.ANY)


def _remote(src, dst, send_sem, recv_sem, dev):
    return pltpu.make_async_remote_copy(src_ref=src, dst_ref=dst, send_sem=send_sem, recv_sem=recv_sem,
                                        device_id=dev, device_id_type=MESH)


def _sems(n):
    return [pltpu.SemaphoreType.DMA((n,)), pltpu.SemaphoreType.DMA((n,))]


class _Exchange:
    def __init__(self, inputs, out_shape, n_sem, start, finish, aliases=None):
        self.inputs, self.out_shape, self.n_sem = list(inputs), list(out_shape), n_sem
        self.start, self.finish, self.aliases = start, finish, dict(aliases or {})


def _run_exchange(ex, name):
    n_in, n_out = len(ex.inputs), len(ex.out_shape)

    def body(*refs):
        srcs, outs = refs[:n_in], refs[n_in:n_in + n_out]
        send_sems, recv_sems = refs[n_in + n_out:]
        ex.start(srcs, outs, send_sems, recv_sems)
        ex.finish(srcs, outs, send_sems, recv_sems)

    return pl.pallas_call(
        body, in_specs=[_hbm()] * n_in, out_specs=[_hbm()] * n_out, out_shape=ex.out_shape,
        input_output_aliases=ex.aliases, scratch_shapes=_sems(ex.n_sem), name=name)(*ex.inputs)


def _pcall(body, ride, args, *, grid, in_specs, out_specs, out_shape, scratch_shapes, sem, name, aliases=None):
    in_specs, out_specs, out_shape = list(in_specs), list(out_specs), list(out_shape)
    scratch_shapes, aliases = list(scratch_shapes), dict(aliases or {})
    if ride is None:
        outs = pl.pallas_call(
            body, grid=grid, in_specs=in_specs, out_specs=out_specs, out_shape=out_shape, scratch_shapes=scratch_shapes,
            input_output_aliases=aliases, compiler_params=_cparams(sem), name=name)(*args)
        return outs, None
    n_in, n_out, n_scr = len(in_specs), len(out_shape), len(scratch_shapes)
    e_in, e_out = len(ride.inputs), len(ride.out_shape)

    def carried(*refs):
        cut = [n_in, e_in, n_out, e_out, n_scr]
        parts, p = [], 0
        for c in cut:
            parts.append(refs[p:p + c])
            p += c
        ins, e_ins, outs, e_outs, scr = parts
        send_sems, recv_sems = refs[p], refs[p + 1]
        first = last = None
        for d, size in enumerate(grid):
            i = pl.program_id(d)
            first = (i == 0) if first is None else jnp.logical_and(first, i == 0)
            last = (i == size - 1) if last is None else jnp.logical_and(last, i == size - 1)

        @pl.when(first)
        def _():
            ride.start(e_ins, e_outs, send_sems, recv_sems)

        body(*ins, *outs, *scr)

        @pl.when(last)
        def _():
            ride.finish(e_ins, e_outs, send_sems, recv_sems)

    for k, v in ride.aliases.items():
        aliases[n_in + k] = n_out + v
    res = pl.pallas_call(
        carried, grid=grid, in_specs=in_specs + [_hbm()] * e_in, out_specs=out_specs + [_hbm()] * e_out,
        out_shape=out_shape + ride.out_shape, scratch_shapes=scratch_shapes + _sems(ride.n_sem),
        input_output_aliases=aliases, compiler_params=_cparams(("arbitrary",) * len(grid)),
        name=name)(*args, *ride.inputs)
    return res[:n_out], res[n_out:]


def _gather_shards(arrs, split):
    n = len(arrs)
    n_sem = sum(6 if sp else 3 for sp in split)

    def rows(i, which):
        if not split[i]:
            return pl.ds(0, arrs[i].shape[0])
        half = arrs[i].shape[0] // 2
        return pl.ds(which * half, half)

    def sends(srcs, outs, send_sems, recv_sems):
        _, _, c, me, chips = _mesh_pos()
        return [(_remote(srcs[i].at[rows(i, c)], outs[i].at[me, rows(i, c)], send_sems.at[3 * i + j],
                         recv_sems.at[3 * i + j], (cx, cy, c)), i, j)
                for i in range(n) for j, (cx, cy) in enumerate(chips)]

    def start(srcs, outs, send_sems, recv_sems):
        for cp, _, _ in sends(srcs, outs, send_sems, recv_sems):
            cp.start()

    def finish(srcs, outs, send_sems, recv_sems):
        x, y, c, _, chips = _mesh_pos()
        sib = (x, y, 1 - c)
        passed, k = [], 3 * n
        for i, j in [(i, j) for i in range(n) for j in range(3)]:
            cx, cy = chips[j]
            slot = outs[i].at[2 * cx + cy, rows(i, c)]
            _remote(slot, slot, send_sems.at[3 * i + j], recv_sems.at[3 * i + j], (cx, cy, c)).wait_recv()
            if split[i]:
                fwd = _remote(slot, slot, send_sems.at[k], recv_sems.at[k], sib)
                fwd.start()
                passed.append((fwd, i, j, k))
                k += 1
        for _, i, j, kf in passed:
            cx, cy = chips[j]
            slot = outs[i].at[2 * cx + cy, rows(i, 1 - c)]
            _remote(slot, slot, send_sems.at[kf], recv_sems.at[kf], sib).wait_recv()
        for cp in [s[0] for s in sends(srcs, outs, send_sems, recv_sems)] + [p[0] for p in passed]:
            cp.wait_send()

    return _Exchange(arrs, [jax.ShapeDtypeStruct((NSHARD,) + a.shape, a.dtype) for a in arrs], n_sem, start, finish)


def _with_own_slot(arrs, got):
    own_slot = jnp.arange(NSHARD, dtype=jnp.int32)[:, None, None] == 2 * lax.axis_index("x") + lax.axis_index("y")
    return [jnp.where(own_slot, a[None], g) for a, g in zip(arrs, got)]


def _swap_halves(arrs):
    n = len(arrs)

    def copies(srcs, outs, send_sems, recv_sems):
        x, y, c, _, _ = _mesh_pos()
        cps = []
        for i in range(n):
            half = arrs[i].shape[1] // 2
            cps.append(_remote(srcs[i].at[:, pl.ds((1 - c) * half, half)], outs[i], send_sems.at[i], recv_sems.at[i],
                               (x, y, 1 - c)))
        return cps

    def start(*refs):
        for cp in copies(*refs):
            cp.start()

    def finish(*refs):
        for cp in copies(*refs):
            cp.wait()

    shapes = [jax.ShapeDtypeStruct((NSHARD, a.shape[1] // 2, a.shape[2]), a.dtype) for a in arrs]
    return _Exchange(arrs, shapes, n, start, finish)


def _scatter_chips(arrs):
    n = len(arrs)

    def copies(srcs, outs, send_sems, recv_sems):
        _, _, c, me, chips = _mesh_pos()
        own = [pltpu.make_async_copy(srcs[i].at[me], outs[i].at[me], send_sems.at[3 * n + i]) for i in range(n)]
        cps = [_remote(srcs[i].at[2 * cx + cy], outs[i].at[me], send_sems.at[3 * i + j], recv_sems.at[3 * i + j],
                       (cx, cy, c)) for i in range(n) for j, (cx, cy) in enumerate(chips)]
        return own, cps

    def start(*refs):
        own, cps = copies(*refs)
        for cp in own + cps:
            cp.start()

    def finish(srcs, outs, send_sems, recv_sems):
        _, _, c, _, chips = _mesh_pos()
        for i in range(n):
            for j, (cx, cy) in enumerate(chips):
                slot = outs[i].at[2 * cx + cy]
                _remote(slot, slot, send_sems.at[3 * i + j], recv_sems.at[3 * i + j], (cx, cy, c)).wait_recv()
        own, cps = copies(srcs, outs, send_sems, recv_sems)
        for cp in cps:
            cp.wait_send()
        for cp in own:
            cp.wait()

    return _Exchange(arrs, [jax.ShapeDtypeStruct(a.shape, a.dtype) for a in arrs], 4 * n, start, finish)


def _share_halves(arrs):
    n = len(arrs)

    def copies(outs, send_sems, recv_sems):
        x, y, c, _, _ = _mesh_pos()
        return [_remote(outs[i].at[c], outs[i].at[c], send_sems.at[i], recv_sems.at[i], (x, y, 1 - c))
                for i in range(n)]

    def start(srcs, outs, send_sems, recv_sems):
        del srcs
        for cp in copies(outs, send_sems, recv_sems):
            cp.start()

    def finish(srcs, outs, send_sems, recv_sems):
        del srcs
        x, y, c, _, _ = _mesh_pos()
        for i in range(n):
            theirs = outs[i].at[1 - c]
            _remote(theirs, theirs, send_sems.at[i], recv_sems.at[i], (x, y, 1 - c)).wait_recv()
        for cp in copies(outs, send_sems, recv_sems):
            cp.wait_send()

    return _Exchange(arrs, [jax.ShapeDtypeStruct(a.shape, a.dtype) for a in arrs], n, start, finish,
                     aliases={i: i for i in range(n)})


def _gather_small(full):
    _, width = full.shape

    def copies(srcs, outs, send_sems, recv_sems):
        _, _, c, me, chips = _mesh_pos()
        mine = srcs[0].at[pl.ds(0, SMALL_ROWS)]
        own = pltpu.make_async_copy(mine, outs[0].at[me], send_sems.at[3])
        return own, [_remote(mine, outs[0].at[me], send_sems.at[j], recv_sems.at[j], (cx, cy, c))
                     for j, (cx, cy) in enumerate(chips)]

    def start(*refs):
        own, cps = copies(*refs)
        for cp in [own] + cps:
            cp.start()

    def finish(srcs, outs, send_sems, recv_sems):
        _, _, c, _, chips = _mesh_pos()
        for j, (cx, cy) in enumerate(chips):
            slot = outs[0].at[2 * cx + cy]
            _remote(slot, slot, send_sems.at[j], recv_sems.at[j], (cx, cy, c)).wait_recv()
        own, cps = copies(srcs, outs, send_sems, recv_sems)
        for cp in cps:
            cp.wait_send()
        own.wait()

    return _Exchange([full], [jax.ShapeDtypeStruct((NSHARD, SMALL_ROWS, width), full.dtype)], 4, start, finish)


def _add_sibling(mine, recv, c, name):
    _, half, width = recv.shape
    tr = _tile(half, 256, 8)
    nb = half // tr

    def body(c_ref, a_ref, b_ref, o_ref):
        del c_ref
        o_ref[...] = (a_ref[...].astype(F32) + b_ref[...].astype(F32)).astype(o_ref.dtype)

    grid_spec = pltpu.PrefetchScalarGridSpec(
        num_scalar_prefetch=1, grid=(NSHARD, nb),
        in_specs=[pl.BlockSpec((1, tr, width), lambda j, r, c_ref: (j, c_ref[0] * nb + r, 0)),
                  pl.BlockSpec((1, tr, width), lambda j, r, c_ref: (j, r, 0))],
        out_specs=pl.BlockSpec((1, tr, width), lambda j, r, c_ref: (j, r, 0)))
    return pl.pallas_call(
        body, grid_spec=grid_spec, out_shape=jax.ShapeDtypeStruct(recv.shape, recv.dtype),
        compiler_params=_cparams(("parallel", "parallel")), name=name)(c, mine, recv)


def _sum_chips(parts, c, name):
    _, half, width = parts.shape
    tr = _tile(half, 256, 8)

    def body(c_ref, p_ref, o_ref):
        del c_ref
        p = [p_ref[j].astype(F32) for j in range(NSHARD)]
        o_ref[0] = ((p[0] + p[1]) + p[2]) + p[3]

    grid_spec = pltpu.PrefetchScalarGridSpec(
        num_scalar_prefetch=1, grid=(half // tr,),
        in_specs=[pl.BlockSpec((NSHARD, tr, width), lambda r, c_ref: (0, r, 0))],
        out_specs=pl.BlockSpec((1, tr, width), lambda r, c_ref: (c_ref[0], r, 0)))
    return pl.pallas_call(
        body, grid_spec=grid_spec, out_shape=jax.ShapeDtypeStruct((2, half, width), F32),
        compiler_params=_cparams(("parallel",)), name=name)(c, parts)


def _unpack(flat, names, shapes):
    out, off = {}, 0
    for n in names:
        sz = _size(shapes[n])
        out[n] = flat[off:off + sz].reshape(shapes[n])
        off += sz
    return out


W_IN_COLS = 3080
W_IN_PAD = 3136


def _reorder_w_in_t(w):
    return jnp.concatenate([w[2048:5120], w[9248:12320], w[0:2048], w[8224:9248], w[5152:6688], w[6688:8224],
                            w[5120:5152], jnp.zeros((NP - 12320, D), w.dtype)], axis=0)


def _restore_w_in_t(g):
    return jnp.concatenate([g[6144:8192], g[0:3072], g[12288:12320], g[9216:10752], g[10752:12288], g[8192:9216],
                            g[3072:6144]], axis=0)


def _lru_group_weights(w):
    w4 = w.reshape(4, 4, 96, 96)
    eye = jnp.eye(4, dtype=w.dtype)
    return (w4[:, :, None, :, :] * eye[None, :, :, None, None]).transpose(0, 1, 3, 2, 4).reshape(4, LRU_G, LRU_G)


def _lru_group_blocks(g):
    g5 = g.reshape(4, 4, 96, 4, 96)
    return jnp.stack([g5[:, a, :, a, :] for a in range(4)], axis=1).reshape(16, 96, 96)


def _spread(a):
    return a.transpose(1, 0, 2).reshape(a.shape[1], NSHARD * a.shape[2])


def _split(a):
    return a.reshape(a.shape[0], NSHARD, a.shape[1] // NSHARD).transpose(1, 0, 2)


class _Reduction:
    def __init__(self, dist, parts, names):
        self.c, self.parts, self.names = dist.c, parts, names

    def swap(self):
        return _swap_halves(self.parts)

    def scatter(self, recv):
        return _scatter_chips([_add_sibling(p, r, self.c, "add_sibling_" + n)
                               for p, r, n in zip(self.parts, recv, self.names)])

    def share(self, landed):
        return _share_halves([_sum_chips(a, self.c, "sum_chips_" + n) for a, n in zip(landed, self.names)])

    def done(self, shared):
        return [a.reshape(2 * a.shape[1], a.shape[2]) for a in shared]


class _Dist:
    def __init__(self, w_in_shard, late_shards):
        self.c = lax.axis_index("c").astype(jnp.int32).reshape(1)
        self.w_in_shard = [w_in_shard]
        self.late_shards = late_shards

    def w_in_ride(self):
        return _gather_shards(self.w_in_shard, [True])

    def w_in_arrived(self, got):
        (g_in,) = _with_own_slot(self.w_in_shard, got)
        return _reorder_w_in_t(g_in[:, 0:W_IN_COLS].reshape(NSHARD * W_IN_COLS, D))

    def weights_ride(self):
        return _gather_shards(self.late_shards, [True, True, False])

    def weights_arrived(self, got):
        g_kv, g_rows, g_small = _with_own_slot(self.late_shards, got)
        out = {"w_kv": _spread(g_kv)}
        for n, lo_, hi_ in ROW_PIECES:
            out[n] = g_rows[:, lo_:hi_].reshape(NSHARD * (hi_ - lo_), D)
        out["ssd_conv_w"] = _spread(g_small[:, :, 0:768])
        out["ssd_norm_g"] = _spread(g_small[:, :, 768:896])
        out["lru_conv_w"] = _spread(g_small[:, :, 896:1280])
        return out

    def early_parts(self, grads):
        rows = jnp.concatenate([grads[n].reshape(NSHARD, hi_ - lo_, D) for n, lo_, hi_ in ROW_PIECES], axis=1)
        return [_split(grads["w_kv"]), rows]

    def late_parts(self, grads):
        rows = _restore_w_in_t(grads["w_in_rt"]).reshape(NSHARD, W_IN_COLS, D)
        return [jnp.pad(rows, ((0, 0), (0, W_IN_PAD - W_IN_COLS), (0, 0)))]


def _local_grads(x, mem, target, wts, dist=None):
    pad128 = lambda a: jnp.pad(a, ((0, 0), (0, 128 - a.shape[1])))

    if dist is None:
        (h,), _ = _norm_fwd(x, wts["norm_g"])
        w_in_rt = wts["w_in_rt"]
        proj = _mm(h, w_in_rt, F32, "in_proj", tb=True, tn=NP_TILE)
    else:
        (h,), arrived = _norm_fwd(x, wts["norm_g"], ride=dist.w_in_ride())
        w_in_rt = dist.w_in_arrived(arrived)
        proj, arrived = _mm(h, w_in_rt, F32, "in_proj", tb=True, tn=NP_TILE, ride=dist.weights_ride())
        wts = dict(wts, **dist.weights_arrived(arrived))
    wbs, wbl, wbm, wo, wkv = wts["w_br_ssd"], wts["w_br_lru"], wts["w_br_mem"], wts["w_out"], wts["w_kv"]
    wa, wx = _mx(_lru_group_weights(wts["lru_w_a"])), _mx(_lru_group_weights(wts["lru_w_x"]))
    ba, bx = wts["lru_b_a"].reshape(1, LRU_W), wts["lru_b_x"].reshape(1, LRU_W)
    dtb, alog = pad128(wts["ssd_dt_bias"]), pad128(wts["ssd_a_log"])
    dexp = jnp.repeat(wts["ssd_d"], 64, axis=1)
    ng = wts["ssd_norm_g"].reshape(1, SSD_W)
    xbc = _conv_fwd(proj, XBC, wts["ssd_conv_w"], wts["ssd_conv_b"], True, "ssd_conv_fwd")
    yssd, yraw, hprev = _ssd_fwd(xbc, proj, dtb, alog, dexp, ng)
    xl = _conv_fwd(proj, LX, wts["lru_conv_w"], wts["lru_conv_b"], False, "lru_conv_fwd")
    ylru, hs = _lru_fwd(xl, proj, wa, wx, ba, bx, wts["lru_lambda"])
    kk, vv, mn = _mem_kv_fwd(mem, wts["mem_norm_g"], wkv)
    ymem = _attn_fwd(proj, kk, vv)

    dproj, dx2, dx2m, merged, db0, db1, db2, loss_vec, dfg = _merge_fb(
        x, target, yssd, ylru, ymem, proj, wbs, wbl, wbm, wo, wts["final_g"].reshape(1, D))
    grads = {"final_g": dfg.reshape(D)}
    grads["w_out"] = _mm(merged, dx2m, _MXU, "dw_out", ta=True)
    grads["w_br_ssd"] = _mm(yssd, db0, _MXU, "dw_br_ssd", ta=True)
    grads["w_br_lru"] = _mm(ylru, db1, _MXU, "dw_br_lru", ta=True)
    grads["w_br_mem"] = _mm(ymem, db2, _MXU, "dw_br_mem", ta=True)
    dyssd = _mm(db0, wbs, F32, "dy_ssd", tb=True)
    dylru = _mm(db1, wbl, F32, "dy_lru", tb=True)
    dymem = _mm(db2, wbm, F32, "dy_mem", tb=True)

    dproj, dk, dv = _attn_bwd(proj, kk, vv, dymem, dproj)
    grads["w_kv"], grads["mem_norm_g"] = _mem_kv_bwd(mem, wts["mem_norm_g"], mn, wkv, dk, dv)

    early = None if dist is None else _Reduction(dist, dist.early_parts(grads), ["w_kv", "rows"])

    (dproj, dxl, dwa, dwx, dba, dbx, dlam), got = _lru_bwd(
        xl, proj, hs, dylru, dproj, wa, wx, ba, bx, wts["lru_lambda"], ride=early and early.swap())
    grads["lru_w_a"] = _lru_group_blocks(dwa)[None]
    grads["lru_w_x"] = _lru_group_blocks(dwx)[None]
    grads["lru_b_a"], grads["lru_b_x"] = dba.reshape(1, 16, 96), dbx.reshape(1, 16, 96)
    grads["lru_lambda"] = dlam
    (grads["lru_conv_w"], grads["lru_conv_b"]), _ = _conv_bwd_w(
        proj, LX, wts["lru_conv_w"], wts["lru_conv_b"], dxl, False, "lru_conv_bwd_w")
    dproj = _conv_bwd_x(dxl, wts["lru_conv_w"], dproj, LX, "lru_conv_bwd_x")

    (dproj, ddt, dxbc, dng, dda, ddd, ddtb), got = _ssd_bwd(
        xbc, proj, yraw, hprev, dyssd, dproj, dtb, alog, dexp, ng, ride=early and early.scatter(got))
    dproj = _put_block(ddt, dproj, DT, "put_ddt")
    grads["ssd_norm_g"] = dng.reshape(4, 512)
    grads["ssd_dt_bias"] = ddtb[:, 0:32]
    grads["ssd_a_log"] = (dda * -jnp.exp(alog))[:, 0:32]
    grads["ssd_d"] = ddd.reshape(32, 64).sum(axis=1)[None, :]
    (dpre, grads["ssd_conv_w"], grads["ssd_conv_b"]), got = _conv_bwd_w(
        proj, XBC, wts["ssd_conv_w"], wts["ssd_conv_b"], dxbc, True, "ssd_conv_bwd_w", ride=early and early.share(got))
    reduced = {} if dist is None else dict(zip(["w_kv", "rows"], early.done(got)))
    dproj = _conv_bwd_x(dpre, wts["ssd_conv_w"], dproj, XBC, "ssd_conv_bwd_x")

    grads["w_in_rt"] = _mm(dproj, h, _MXU, "dw_in", ta=True, tm=NP_TILE, tn=1024)
    if dist is None:
        dh = _mm(dproj, w_in_rt, F32, "dh", tn=1024, tk=NP_TILE)
        (grad_x, grads["norm_g"]), _ = _norm_bwd(x, wts["norm_g"], dh, dx2)
    else:
        late = _Reduction(dist, dist.late_parts(grads), ["w_in"])
        got = _run_exchange(late.swap(), "swap_halves_w_in")
        dh, got = _mm(dproj, w_in_rt, F32, "dh", tn=1024, tk=NP_TILE, ride=late.scatter(got))
        (grad_x, grads["norm_g"]), _ = _norm_bwd(x, wts["norm_g"], dh, dx2)
        reduced["w_in"] = late.done(_run_exchange(late.share(got), "share_halves_w_in"))[0]
    return jnp.sum(loss_vec), grad_x, grads, reduced


def kernel(x, mem, norm_g, w_in, ssd_conv_w, ssd_conv_b, ssd_dt_bias, ssd_a_log, ssd_d, ssd_norm_g, lru_conv_w, lru_conv_b, lru_w_a, lru_b_a, lru_w_x, lru_b_x, lru_lambda, mem_norm_g, w_kv, w_br_ssd, w_br_lru, w_br_mem, w_out, final_g, loss_target, m_norm_g, m_w_in, m_ssd_conv_w, m_ssd_conv_b, m_ssd_dt_bias, m_ssd_a_log, m_ssd_d, m_ssd_norm_g, m_lru_conv_w, m_lru_conv_b, m_lru_w_a, m_lru_b_a, m_lru_w_x, m_lru_b_x, m_lru_lambda, m_mem_norm_g, m_w_kv, m_w_br_ssd, m_w_br_lru, m_w_br_mem, m_w_out, m_final_g, v_norm_g, v_w_in, v_ssd_conv_w, v_ssd_conv_b, v_ssd_dt_bias, v_ssd_a_log, v_ssd_d, v_ssd_norm_g, v_lru_conv_w, v_lru_conv_b, v_lru_w_a, v_lru_b_a, v_lru_w_x, v_lru_b_x, v_lru_lambda, v_mem_norm_g, v_w_kv, v_w_br_ssd, v_w_br_lru, v_w_br_mem, v_w_out, v_final_g):
    given = dict(locals())

    rows_w = jnp.concatenate([w_br_ssd[0], w_br_lru[0], w_br_mem[0], w_out[0]], axis=0)
    small_w = jnp.concatenate([ssd_conv_w[0], ssd_norm_g[0], lru_conv_w[0]], axis=1)
    w_in_t = jnp.pad(_mx(w_in[0].T), ((0, W_IN_PAD - W_IN_COLS), (0, 0)))
    dist = _Dist(w_in_t, [_mx(w_kv[0]), _mx(rows_w), small_w])
    wts = {n: given[n] for n in REPL}
    wts["lru_w_a"], wts["lru_w_x"] = lru_w_a[0], lru_w_x[0]

    loss_part, grad_x, grads, reduced = _local_grads(x[0], mem[0], loss_target[0], wts, dist)
    loss = lax.psum(loss_part, ("x", "y", "c"))

    repl_flat = jnp.concatenate([grads[n].reshape(-1) for n in REPL])
    repl_flat = jnp.pad(repl_flat, (0, NSHARD * SMALL_Q - repl_flat.shape[0])).reshape(NSHARD, SMALL_Q)
    shard_small = jnp.concatenate([_split(grads[n]).reshape(NSHARD, -1) for n in SMALL_SHARDED], axis=1)
    p_small = jnp.concatenate(
        [repl_flat, shard_small, jnp.zeros((NSHARD, SMALL_BUF_ROWS * PACK_W - SMALL_Q - 5120), F32)], axis=1)
    small = _Reduction(dist, [p_small.reshape(NSHARD, SMALL_BUF_ROWS, PACK_W)], ["small"])
    got = _run_exchange(small.swap(), "swap_halves_small")
    got = _run_exchange(small.scatter(got), "scatter_chips_small")
    got = _run_exchange(small.share(got), "share_halves_small")
    r_small = small.done(got)[0]
    repl_all = _run_exchange(_gather_small(r_small), "gather_small")[0].reshape(-1)

    g_shard = {"w_kv": reduced["w_kv"]}
    for n, lo_, hi_ in ROW_PIECES:
        g_shard[n] = reduced["rows"][lo_:hi_]
    g_shard.update(_unpack(r_small.reshape(-1)[SMALL_Q:], SMALL_SHARDED, SHARD_SHAPE))
    g_repl = _unpack(repl_all, REPL, REPL_SHAPE)

    out_g, out_d, out_m, out_v = {}, {}, {}, {}
    for n in WEIGHTS:
        w_full = given[n]
        if n == "w_in":
            g2 = reduced["w_in"][0:W_IN_COLS]
            d, mo, vo = _adamw(w_in[0].T, g2, m_w_in[0].T, v_w_in[0].T, "adamw_w_in")
            out_g[n], out_d[n], out_m[n], out_v[n] = [a.T[None] for a in (g2, d, mo, vo)]
            continue
        g = (g_shard[n] if n in SHARDED else g_repl[n]).reshape(w_full.shape)
        cols = w_full.shape[-1]
        as2d = lambda a: a.reshape(-1, cols)
        d, mo, vo = _adamw(as2d(w_full), as2d(g), as2d(given["m_" + n]), as2d(given["v_" + n]), "adamw_" + n)
        out_g[n] = g
        out_d[n], out_m[n], out_v[n] = d.reshape(w_full.shape), mo.reshape(w_full.shape), vo.reshape(w_full.shape)

    return (loss, grad_x[None], *[out_g[n] for n in WEIGHTS], *[out_d[n] for n in WEIGHTS],
            *[out_m[n] for n in WEIGHTS], *[out_v[n] for n in WEIGHTS])
```

```python
import jax
import jax.numpy as jnp
from jax import lax
from jax.experimental import pallas as pl
from jax.experimental.pallas import tpu as pltpu

F32 = jnp.float32
_MXU = jnp.bfloat16
_HI = lax.Precision.HIGHEST
MESH = pl.DeviceIdType.MESH

D = 1024
EPS = 1e-6
MEM_HEADS = 4
MEM_HD = 256
LRU_C = 8.0
SSD_L = 128
SSD_W = 2048
LRU_W = 1536
NSHARD = 4

XBC = (0, 3072)
GL = (3072, 3072)
Z = (6144, 2048)
Q = (8192, 1024)
LG = (9216, 1536)
LX = (10752, 1536)
DT = (12288, 256)
NP = 12544
NP_TILE = 1792

ADAM_LR = 0.001
ADAM_B1 = 0.9
ADAM_B2 = 0.999
ADAM_EPS = 1e-08
ADAM_WD = 0.01
ADAM_STEP = 10

VMEM_LIMIT = 56 * 1024 * 1024

SHARDED = ("w_in", "ssd_conv_w", "ssd_norm_g", "lru_conv_w", "w_kv", "w_br_ssd", "w_br_lru", "w_br_mem", "w_out")
SHARD_SHAPE = {"w_in": (1024, 3080), "ssd_conv_w": (4, 768), "ssd_norm_g": (4, 128), "lru_conv_w": (4, 384),
               "w_kv": (1024, 512), "w_br_ssd": (512, 1024), "w_br_lru": (384, 1024), "w_br_mem": (256, 1024),
               "w_out": (256, 1024)}
REPL = ("norm_g", "ssd_conv_b", "ssd_dt_bias", "ssd_a_log", "ssd_d", "lru_conv_b", "lru_w_a", "lru_b_a",
        "lru_w_x", "lru_b_x", "lru_lambda", "mem_norm_g", "final_g")
REPL_SHAPE = {"norm_g": (1, 1024), "ssd_conv_b": (1, 3072), "ssd_dt_bias": (1, 32), "ssd_a_log": (1, 32),
              "ssd_d": (1, 32), "lru_conv_b": (1, 1536), "lru_w_a": (1, 16, 96, 96), "lru_b_a": (1, 16, 96),
              "lru_w_x": (1, 16, 96, 96), "lru_b_x": (1, 16, 96), "lru_lambda": (1, 1536),
              "mem_norm_g": (1, 1024), "final_g": (1024,)}
WEIGHTS = ("norm_g", "w_in", "ssd_conv_w", "ssd_conv_b", "ssd_dt_bias", "ssd_a_log", "ssd_d", "ssd_norm_g",
           "lru_conv_w", "lru_conv_b", "lru_w_a", "lru_b_a", "lru_w_x", "lru_b_x", "lru_lambda", "mem_norm_g",
           "w_kv", "w_br_ssd", "w_br_lru", "w_br_mem", "w_out", "final_g")

ROW_PIECES = (("w_br_ssd", 0, 512), ("w_br_lru", 512, 896), ("w_br_mem", 896, 1152), ("w_out", 1152, 1408))
SMALL_SHARDED = ("ssd_conv_w", "ssd_norm_g", "lru_conv_w")
PACK_W = 512
SMALL_ROWS = 152
SMALL_Q = SMALL_ROWS * PACK_W
SMALL_BUF_ROWS = 176


def _size(shape):
    n = 1
    for s in shape:
        n *= s
    return n


def _sigmoid(x):
    return 0.5 * jnp.tanh(0.5 * x) + 0.5


def _silu(x):
    return x * _sigmoid(x)


def _dsilu(x):
    s = _sigmoid(x)
    return s * (1.0 + x * (1.0 - s))


def _softplus(x):
    return jnp.maximum(x, 0.0) + jnp.log(1.0 + jnp.exp(-jnp.abs(x)))


def _one_minus_sq(log_a, a):
    x = 2.0 * log_a
    series = -x * (1.0 + x * (0.5 + x * (1.0 / 6.0 + x * (1.0 / 24.0))))
    return jnp.where(x > -0.03, series, 1.0 - a * a)


def _dot(a, b, precision=None):
    return jnp.dot(a, b, preferred_element_type=F32, precision=precision)


def _dot_nt(a, b):
    return lax.dot_general(a, b, (((1,), (1,)), ((), ())), preferred_element_type=F32)


def _dot_tn(a, b):
    return lax.dot_general(a, b, (((0,), (0,)), ((), ())), preferred_element_type=F32)


def _mx(a):
    return a.astype(_MXU)


def _cparams(sem):
    return pltpu.CompilerParams(dimension_semantics=sem, vmem_limit_bytes=VMEM_LIMIT)


def _tile(n, want, mult=128):
    if n <= want:
        return n
    for t in range(want - want % mult, 0, -mult):
        if n % t == 0:
            return t
    raise ValueError((n, want, mult))


def _mm(a, b, out_dtype, name, ta=False, tb=False, tm=1024, tn=1280, tk=1024, ride=None):
    k, m = a.shape if ta else a.shape[::-1]
    k2, n = b.shape[::-1] if tb else b.shape
    assert k == k2
    tm, tn, tk = _tile(m, tm), _tile(n, tn), _tile(k, tk)
    nk = k // tk
    contract = (((0 if ta else 1,), (1 if tb else 0,)), ((), ()))

    def body(a_ref, b_ref, o_ref, *acc):
        prod = lax.dot_general(a_ref[...], b_ref[...], contract, preferred_element_type=F32)
        if nk == 1:
            o_ref[...] = prod.astype(o_ref.dtype)
            return
        acc_ref, = acc
        kk = pl.program_id(2)

        @pl.when(kk == 0)
        def _():
            acc_ref[...] = prod

        @pl.when(kk > 0)
        def _():
            acc_ref[...] += prod

        @pl.when(kk == nk - 1)
        def _():
            o_ref[...] = acc_ref[...].astype(o_ref.dtype)

    a_spec = pl.BlockSpec((tk, tm), lambda i, j, kk: (kk, i)) if ta else pl.BlockSpec((tm, tk), lambda i, j, kk: (i, kk))
    b_spec = pl.BlockSpec((tn, tk), lambda i, j, kk: (j, kk)) if tb else pl.BlockSpec((tk, tn), lambda i, j, kk: (kk, j))
    outs, carried = _pcall(
        body, ride, (a, b), grid=(m // tm, n // tn, nk),
        in_specs=[a_spec, b_spec],
        out_specs=[pl.BlockSpec((tm, tn), lambda i, j, kk: (i, j))],
        out_shape=[jax.ShapeDtypeStruct((m, n), out_dtype)],
        scratch_shapes=[pltpu.VMEM((tm, tn), F32)] if nk > 1 else [],
        sem=("parallel", "parallel", "arbitrary"), name=name)
    return outs[0] if ride is None else (outs[0], carried)


def _norm_fwd(x, g, ride=None):
    s = x.shape[0]
    ts = _tile(s, 512)

    def body(x_ref, g_ref, h_ref):
        xv = x_ref[...]
        r = lax.rsqrt(jnp.mean(xv * xv, axis=-1, keepdims=True) + EPS)
        h_ref[...] = (xv * r * g_ref[...]).astype(h_ref.dtype)

    return _pcall(
        body, ride, (x, g), grid=(s // ts,),
        in_specs=[pl.BlockSpec((ts, D), lambda i: (i, 0)), pl.BlockSpec((1, D), lambda i: (0, 0))],
        out_specs=[pl.BlockSpec((ts, D), lambda i: (i, 0))],
        out_shape=[jax.ShapeDtypeStruct((s, D), _MXU)], scratch_shapes=[], sem=("parallel",), name="norm_fwd")


def _norm_bwd(x, g, dh, dx2, ride=None):
    s = x.shape[0]
    ts = _tile(s, 512)

    def body(x_ref, g_ref, dh_ref, dx2_ref, gx_ref, dg_ref):
        @pl.when(pl.program_id(0) == 0)
        def _():
            dg_ref[...] = jnp.zeros_like(dg_ref)

        xv = x_ref[...]
        r = lax.rsqrt(jnp.mean(xv * xv, axis=-1, keepdims=True) + EPS)
        xhat = xv * r
        dh_v = dh_ref[...]
        dg_ref[...] += jnp.sum(dh_v * xhat, axis=0, keepdims=True)
        dxh = dh_v * g_ref[...]
        gx_ref[...] = dx2_ref[...] + r * (dxh - xhat * jnp.mean(dxh * xhat, axis=-1, keepdims=True))

    row = pl.BlockSpec((ts, D), lambda i: (i, 0))
    vec = pl.BlockSpec((1, D), lambda i: (0, 0))
    return _pcall(
        body, ride, (x, g, dh, dx2), grid=(s // ts,), in_specs=[row, vec, row, row], out_specs=[row, vec],
        out_shape=[jax.ShapeDtypeStruct((s, D), F32), jax.ShapeDtypeStruct((1, D), F32)],
        scratch_shapes=[], sem=("arbitrary",), name="norm_bwd")


CONV_RB = 16
CONV_LC = 256


def _fold8(v):
    acc = v[0:8]
    for r0 in range(8, v.shape[0], 8):
        acc = acc + v[r0:r0 + 8]
    return acc


def _conv_fwd(src, blk, w, b, act, name):
    s = src.shape[0]
    off, width = blk
    cb = off // width
    ts = _tile(s, 256)

    def body(x_ref, w_ref, b_ref, o_ref, ext_ref):
        @pl.when(pl.program_id(0) == 0)
        def _():
            ext_ref[0:8, :] = jnp.zeros((8, width), F32)

        ext_ref[8:8 + ts, :] = x_ref[...]
        for l0 in range(0, width, CONV_LC):
            ls = slice(l0, l0 + CONV_LC)
            taps = [w_ref[k:k + 1, ls] for k in range(4)]
            bias = b_ref[:, ls]
            for r0 in range(0, ts, CONV_RB):
                pre = bias
                for k in range(4):
                    pre = pre + taps[k] * ext_ref[5 + k + r0:5 + k + r0 + CONV_RB, ls]
                o_ref[r0:r0 + CONV_RB, ls] = _silu(pre) if act else pre
        ext_ref[0:8, :] = x_ref[ts - 8:ts, :]

    return pl.pallas_call(
        body, grid=(s // ts,),
        in_specs=[pl.BlockSpec((ts, width), lambda i: (i, cb)), pl.BlockSpec((4, width), lambda i: (0, 0)),
                  pl.BlockSpec((1, width), lambda i: (0, 0))],
        out_specs=pl.BlockSpec((ts, width), lambda i: (i, 0)),
        out_shape=jax.ShapeDtypeStruct((s, width), F32),
        scratch_shapes=[pltpu.VMEM((ts + 8, width), F32)],
        compiler_params=_cparams(("arbitrary",)), name=name)(src, w, b)


def _conv_bwd_w(src, blk, w, b, dout, act, name, ride=None):
    s = src.shape[0]
    off, width = blk
    cb = off // width
    ts = _tile(s, 256)

    def body(x_ref, w_ref, b_ref, do_ref, *rest):
        if act:
            dpre_ref, dw_ref, db_ref, ext_ref = rest
        else:
            dw_ref, db_ref, ext_ref = rest

        @pl.when(pl.program_id(0) == 0)
        def _():
            ext_ref[0:8, :] = jnp.zeros((8, width), F32)
            dw_ref[...] = jnp.zeros_like(dw_ref)
            db_ref[...] = jnp.zeros_like(db_ref)

        ext_ref[8:8 + ts, :] = x_ref[...]
        for l0 in range(0, width, CONV_LC):
            ls = slice(l0, l0 + CONV_LC)
            taps = [w_ref[k:k + 1, ls] for k in range(4)]
            bias = b_ref[:, ls]
            acc_b = jnp.zeros((8, CONV_LC), F32)
            acc_w = [jnp.zeros((8, CONV_LC), F32) for _ in range(4)]
            for r0 in range(0, ts, CONV_RB):
                xs = [ext_ref[5 + k + r0:5 + k + r0 + CONV_RB, ls] for k in range(4)]
                dpre = do_ref[r0:r0 + CONV_RB, ls]
                if act:
                    pre = bias
                    for k in range(4):
                        pre = pre + taps[k] * xs[k]
                    dpre = dpre * _dsilu(pre)
                    dpre_ref[r0:r0 + CONV_RB, ls] = dpre
                acc_b = acc_b + _fold8(dpre)
                for k in range(4):
                    acc_w[k] = acc_w[k] + _fold8(dpre * xs[k])
            db_ref[:, ls] += jnp.sum(acc_b, axis=0, keepdims=True)
            for k in range(4):
                dw_ref[k:k + 1, ls] += jnp.sum(acc_w[k], axis=0, keepdims=True)
        ext_ref[0:8, :] = x_ref[ts - 8:ts, :]

    row = pl.BlockSpec((ts, width), lambda i: (i, 0))
    outs = [pl.BlockSpec((4, width), lambda i: (0, 0)), pl.BlockSpec((1, width), lambda i: (0, 0))]
    shapes = [jax.ShapeDtypeStruct((4, width), F32), jax.ShapeDtypeStruct((1, width), F32)]
    if act:
        outs = [row] + outs
        shapes = [jax.ShapeDtypeStruct((s, width), F32)] + shapes
    return _pcall(
        body, ride, (src, w, b, dout), grid=(s // ts,),
        in_specs=[pl.BlockSpec((ts, width), lambda i: (i, cb)), pl.BlockSpec((4, width), lambda i: (0, 0)),
                  pl.BlockSpec((1, width), lambda i: (0, 0)), row],
        out_specs=outs, out_shape=shapes,
        scratch_shapes=[pltpu.VMEM((ts + 8, width), F32)], sem=("arbitrary",), name=name)


def _conv_bwd_x(dpre, w, dproj, blk, name):
    s = dpre.shape[0]
    off, width = blk
    cb = off // width
    ts = _tile(s, 256)
    nt = s // ts

    def body(dp_ref, w_ref, dproj_hbm, o_ref, ext_ref):
        del dproj_hbm

        @pl.when(pl.program_id(0) == 0)
        def _():
            ext_ref[ts:ts + 8, :] = jnp.zeros((8, width), F32)

        ext_ref[0:ts, :] = dp_ref[...]
        for l0 in range(0, width, CONV_LC):
            ls = slice(l0, l0 + CONV_LC)
            taps = [w_ref[k:k + 1, ls] for k in range(4)]
            for r0 in range(0, ts, CONV_RB):
                acc = taps[0] * ext_ref[3 + r0:3 + r0 + CONV_RB, ls]
                for k in range(1, 4):
                    acc = acc + taps[k] * ext_ref[3 - k + r0:3 - k + r0 + CONV_RB, ls]
                o_ref[r0:r0 + CONV_RB, ls] = acc.astype(o_ref.dtype)
        ext_ref[ts:ts + 8, :] = dp_ref[0:8, :]

    return pl.pallas_call(
        body, grid=(nt,),
        in_specs=[pl.BlockSpec((ts, width), lambda i: (nt - 1 - i, 0)), pl.BlockSpec((4, width), lambda i: (0, 0)),
                  pl.BlockSpec(memory_space=pl.ANY)],
        out_specs=pl.BlockSpec((ts, width), lambda i: (nt - 1 - i, cb)),
        out_shape=jax.ShapeDtypeStruct(dproj.shape, dproj.dtype),
        scratch_shapes=[pltpu.VMEM((ts + 8, width), F32)],
        input_output_aliases={2: 0},
        compiler_params=_cparams(("arbitrary",)), name=name)(dpre, w, dproj)


def _ssd_decay(a_cs, acst_ref, h, causal, lane_l):
    col = jnp.sum(jnp.where(lane_l == h, a_cs, 0.0), axis=1, keepdims=True)
    row = acst_ref[h:h + 1, :]
    return jnp.where(causal, jnp.exp(jnp.minimum(col - row, 0.0)), 0.0)


def _split3(x):
    hi = x.astype(jnp.bfloat16)
    rest = x - hi.astype(F32)
    mid = rest.astype(jnp.bfloat16)
    return jnp.concatenate([hi, mid, (rest - mid.astype(F32)).astype(jnp.bfloat16)], axis=1)


def _spread_matrix():
    col = jnp.arange(128, dtype=jnp.int32)[:, None]
    e64 = (col == jnp.arange(SSD_W, dtype=jnp.int32)[None, :] // 64).astype(jnp.bfloat16)
    return jnp.tile(e64, (3, 1))


def _ssd_common(dt_ref, dtb_ref, alog_ref, e64_ref, acst_ref, dtx_ref, acx_ref):
    ll = SSD_L
    dt = _softplus(dt_ref[:, 0:128] + dtb_ref[...])
    a_neg = -jnp.exp(alog_ref[...])
    ri = lax.broadcasted_iota(jnp.int32, (ll, ll), 0)
    ci = lax.broadcasted_iota(jnp.int32, (ll, ll), 1)
    causal = ri >= ci
    a_cs = _dot(causal.astype(F32), dt * a_neg, _HI)
    acst_ref[...] = a_cs.T
    both = _dot(jnp.concatenate([_split3(dt), _split3(a_cs)], axis=0), e64_ref[...])
    dtx_ref[...] = both[0:ll]
    acx_ref[...] = both[ll:2 * ll]
    lane_l = lax.broadcasted_iota(jnp.int32, (ll, 128), 1)
    return dt, a_neg, a_cs, causal, ri, lane_l, lane_l < 64


def _ssd_fwd(xbc, proj, dtb, alog, dexp, ng):
    s = xbc.shape[0]
    ll = SSD_L
    nc = s // ll
    e64 = _spread_matrix()

    def body(xbc_ref, dt_ref, z_ref, dtb_ref, alog_ref, dexp_ref, ng_ref, e64_ref,
             yssd_ref, yraw_ref, hprev_ref, ht_ref, acst_ref, dtx_ref, acx_ref):
        @pl.when(pl.program_id(0) == 0)
        def _():
            ht_ref[...] = jnp.zeros_like(ht_ref)

        hprev_ref[0] = ht_ref[...]
        _, _, a_cs, causal, _, lane_l, lo = _ssd_common(dt_ref, dtb_ref, alog_ref, e64_ref, acst_ref, dtx_ref, acx_ref)
        for g in range(4):
            bg = _mx(xbc_ref[:, 2048 + 128 * g:2176 + 128 * g])
            cg = _mx(xbc_ref[:, 2560 + 128 * g:2688 + 128 * g])
            cbm = _dot_nt(cg, bg)
            for jj in range(4):
                j = 4 * g + jj
                sl = slice(128 * j, 128 * j + 128)
                xp = xbc_ref[:, sl]
                acx = acx_ref[:, sl]
                a_last = acx_ref[ll - 1:ll, sl]
                xdt = xp * dtx_ref[:, sl]
                acc = None
                for hh in range(2):
                    dec = _ssd_decay(a_cs, acst_ref, 2 * j + hh, causal, lane_l)
                    xm = jnp.where(lo if hh == 0 else jnp.logical_not(lo), xdt, 0.0)
                    t = _dot(_mx(dec * cbm), _mx(xm))
                    acc = t if acc is None else acc + t
                ht = ht_ref[j]
                y = acc + _dot(cg, _mx(ht)) * jnp.exp(acx) + xp * dexp_ref[:, sl]
                yraw_ref[:, sl] = y
                st = _dot_tn(bg, _mx(xdt * jnp.exp(a_last - acx)))
                ht_ref[j] = ht * jnp.exp(a_last) + st
        for g in range(4):
            sl = slice(512 * g, 512 * g + 512)
            yg = yraw_ref[:, sl] * _silu(z_ref[:, sl])
            r = lax.rsqrt(jnp.mean(yg * yg, axis=-1, keepdims=True) + EPS)
            yssd_ref[:, sl] = (yg * r * ng_ref[:, sl]).astype(yssd_ref.dtype)

    vec = lambda w: pl.BlockSpec((1, w), lambda c: (0, 0))
    return pl.pallas_call(
        body, grid=(nc,),
        in_specs=[pl.BlockSpec((ll, 3072), lambda c: (c, 0)),
                  pl.BlockSpec((ll, DT[1]), lambda c: (c, DT[0] // DT[1])),
                  pl.BlockSpec((ll, Z[1]), lambda c: (c, Z[0] // Z[1])),
                  vec(128), vec(128), vec(2048), vec(2048),
                  pl.BlockSpec(e64.shape, lambda c: (0, 0))],
        out_specs=[pl.BlockSpec((ll, 2048), lambda c: (c, 0)), pl.BlockSpec((ll, 2048), lambda c: (c, 0)),
                   pl.BlockSpec((1, 16, 128, 128), lambda c: (c, 0, 0, 0))],
        out_shape=[jax.ShapeDtypeStruct((s, 2048), _MXU), jax.ShapeDtypeStruct((s, 2048), F32),
                   jax.ShapeDtypeStruct((nc, 16, 128, 128), F32)],
        scratch_shapes=[pltpu.VMEM((16, 128, 128), F32), pltpu.VMEM((128, ll), F32),
                        pltpu.VMEM((ll, 2048), F32), pltpu.VMEM((ll, 2048), F32)],
        compiler_params=_cparams(("arbitrary",)), name="ssd_fwd")(xbc, proj, proj, dtb, alog, dexp, ng, e64)


def _ssd_bwd(xbc, proj, yraw, hprev, dyssd, dproj, dtb, alog, dexp, ng, ride=None):
    s = xbc.shape[0]
    ll = SSD_L
    nc = s // ll
    e64 = _spread_matrix()

    def body(xbc_ref, dt_ref, z_ref, yraw_ref, hprev_ref, dy_ref, dproj_hbm, dtb_ref, alog_ref, dexp_ref, ng_ref,
             e64_ref,
             dz_ref, ddt_ref, dxbc_ref, dng_ref, dda_ref, ddd_ref, ddtb_ref,
             dht_ref, acst_ref, dtx_ref, acx_ref, dyr_ref, rowt_ref):
        del dproj_hbm

        @pl.when(pl.program_id(0) == 0)
        def _():
            dht_ref[...] = jnp.zeros_like(dht_ref)
            dng_ref[...] = jnp.zeros_like(dng_ref)
            dda_ref[...] = jnp.zeros_like(dda_ref)
            ddd_ref[...] = jnp.zeros_like(ddd_ref)
            ddtb_ref[...] = jnp.zeros_like(ddtb_ref)
            rowt_ref[...] = jnp.zeros_like(rowt_ref)

        for g in range(4):
            sl = slice(512 * g, 512 * g + 512)
            zz = z_ref[:, sl]
            yr = yraw_ref[:, sl]
            sz = _silu(zz)
            yg = yr * sz
            r = lax.rsqrt(jnp.mean(yg * yg, axis=-1, keepdims=True) + EPS)
            yhat = yg * r
            dyv = dy_ref[:, sl]
            dng_ref[:, sl] += jnp.sum(dyv * yhat, axis=0, keepdims=True)
            dyh = dyv * ng_ref[:, sl]
            dyg = r * (dyh - yhat * jnp.mean(dyh * yhat, axis=-1, keepdims=True))
            dz_ref[:, sl] = (dyg * yr * _dsilu(zz)).astype(dz_ref.dtype)
            dyr_ref[:, sl] = dyg * sz

        dt, a_neg, a_cs, causal, ri, lane_l, lo = _ssd_common(dt_ref, dtb_ref, alog_ref, e64_ref,
                                                              acst_ref, dtx_ref, acx_ref)
        lane_1 = lax.broadcasted_iota(jnp.int32, (1, 128), 1)
        da_col = jnp.zeros((ll, 128), F32)
        ddt_x = jnp.zeros((ll, 128), F32)
        last = jnp.zeros((1, 128), F32)
        for g in range(4):
            bg = _mx(xbc_ref[:, 2048 + 128 * g:2176 + 128 * g])
            cg = _mx(xbc_ref[:, 2560 + 128 * g:2688 + 128 * g])
            cbm = _dot_nt(cg, bg)
            dcb = jnp.zeros((ll, ll), F32)
            db_g = jnp.zeros((ll, 128), F32)
            dc_g = jnp.zeros((ll, 128), F32)
            for jj in range(4):
                j = 4 * g + jj
                sl = slice(128 * j, 128 * j + 128)
                xp = xbc_ref[:, sl]
                dtx = dtx_ref[:, sl]
                acx = acx_ref[:, sl]
                a_last = acx_ref[ll - 1:ll, sl]
                ea = jnp.exp(acx)
                dte = jnp.exp(a_last - acx)
                cd = jnp.exp(a_last)
                xdt = xp * dtx
                xdt_m = _mx(xdt)
                dy = dyr_ref[:, sl]
                ht = hprev_ref[0, j]
                dhn = dht_ref[j]
                dhn_m = _mx(dhn)
                gmat = _dot(bg, dhn_m)
                dxdt = gmat * dte
                for hh in range(2):
                    h = 2 * j + hh
                    dec = _ssd_decay(a_cs, acst_ref, h, causal, lane_l)
                    mm = dec * cbm
                    dym = _mx(jnp.where(lo if hh == 0 else jnp.logical_not(lo), dy, 0.0))
                    dxdt = dxdt + _dot_tn(_mx(mm), dym)
                    dm = _dot_nt(dym, xdt_m)
                    dcb = dcb + dm * dec
                    qq = dm * mm
                    da_col = da_col + jnp.where(lane_l == h, jnp.sum(qq, axis=1, keepdims=True), 0.0)
                    rowt_ref[h:h + 1, :] = jnp.sum(qq, axis=0, keepdims=True)
                ch = _dot(cg, _mx(ht))
                dyea = dy * ea
                dyea_m = _mx(dyea)
                xw_m = _mx(xdt * dte)
                dc_g = dc_g + _dot_nt(dyea_m, _mx(ht))
                db_g = db_g + _dot_nt(xw_m, dhn_m)
                wl = xdt * gmat * dte
                lane_a = dyea * ch - wl
                lane_b = dxdt * xp
                lane_c = jnp.sum(dhn * ht, axis=0, keepdims=True) * cd + jnp.sum(wl, axis=0, keepdims=True)
                for hh in range(2):
                    h = 2 * j + hh
                    mine = lo if hh == 0 else jnp.logical_not(lo)
                    da_col = da_col + jnp.where(
                        lane_l == h, jnp.sum(jnp.where(mine, lane_a, 0.0), axis=1, keepdims=True), 0.0)
                    ddt_x = ddt_x + jnp.where(
                        lane_l == h, jnp.sum(jnp.where(mine, lane_b, 0.0), axis=1, keepdims=True), 0.0)
                    mine_1 = (lane_1 < 64) if hh == 0 else (lane_1 >= 64)
                    last = last + jnp.where(
                        lane_1 == h, jnp.sum(jnp.where(mine_1, lane_c, 0.0), axis=1, keepdims=True), 0.0)
                dht_ref[j] = dhn * cd + _dot_tn(cg, dyea_m)
                dxbc_ref[:, sl] = dxdt * dtx + dy * dexp_ref[:, sl]
                ddd_ref[:, sl] += jnp.sum(dy * xp, axis=0, keepdims=True)
            dcb_m = _mx(dcb)
            dxbc_ref[:, 2048 + 128 * g:2176 + 128 * g] = db_g + _dot_tn(dcb_m, cg)
            dxbc_ref[:, 2560 + 128 * g:2688 + 128 * g] = dc_g + _dot(dcb_m, bg)

        da_cs = da_col - rowt_ref[...].T
        da_cs = da_cs + jnp.where(lax.broadcasted_iota(jnp.int32, (ll, 128), 0) == ll - 1, last, 0.0)
        d_dta = _dot((ri <= lax.broadcasted_iota(jnp.int32, (ll, ll), 1)).astype(F32), da_cs, _HI)
        ddt = d_dta * a_neg + ddt_x
        dda_ref[...] += jnp.sum(d_dta * dt, axis=0, keepdims=True)
        ddt_raw = ddt * _sigmoid(dt_ref[:, 0:128] + dtb_ref[...])
        ddtb_ref[...] += jnp.sum(ddt_raw, axis=0, keepdims=True)
        ddt_ref[:, 0:128] = ddt_raw.astype(ddt_ref.dtype)
        ddt_ref[:, 128:DT[1]] = jnp.zeros((ll, DT[1] - 128), ddt_ref.dtype)

    rev = lambda c: nc - 1 - c
    vec = lambda w: pl.BlockSpec((1, w), lambda c: (0, 0))
    row = lambda w: pl.BlockSpec((ll, w), lambda c: (rev(c), 0))
    return _pcall(
        body, ride, (xbc, proj, proj, yraw, hprev, dyssd, dproj, dtb, alog, dexp, ng, e64), grid=(nc,),
        in_specs=[row(3072),
                  pl.BlockSpec((ll, DT[1]), lambda c: (rev(c), DT[0] // DT[1])),
                  pl.BlockSpec((ll, Z[1]), lambda c: (rev(c), Z[0] // Z[1])),
                  row(2048),
                  pl.BlockSpec((1, 16, 128, 128), lambda c: (rev(c), 0, 0, 0)),
                  row(2048),
                  pl.BlockSpec(memory_space=pl.ANY),
                  vec(128), vec(128), vec(2048), vec(2048),
                  pl.BlockSpec(e64.shape, lambda c: (0, 0))],
        out_specs=[pl.BlockSpec((ll, Z[1]), lambda c: (rev(c), Z[0] // Z[1])),
                   row(DT[1]),
                   row(3072), vec(2048), vec(128), vec(2048), vec(128)],
        out_shape=[jax.ShapeDtypeStruct(dproj.shape, dproj.dtype), jax.ShapeDtypeStruct((s, DT[1]), dproj.dtype),
                   jax.ShapeDtypeStruct((s, 3072), F32), jax.ShapeDtypeStruct((1, 2048), F32),
                   jax.ShapeDtypeStruct((1, 128), F32), jax.ShapeDtypeStruct((1, 2048), F32),
                   jax.ShapeDtypeStruct((1, 128), F32)],
        scratch_shapes=[pltpu.VMEM((16, 128, 128), F32), pltpu.VMEM((128, ll), F32),
                        pltpu.VMEM((ll, 2048), F32), pltpu.VMEM((ll, 2048), F32), pltpu.VMEM((ll, 2048), F32),
                        pltpu.VMEM((128, ll), F32)],
        aliases={6: 0}, sem=("arbitrary",), name="ssd_bwd")


def _put_block(src, dproj, blk, name):
    s = src.shape[0]
    off, width = blk
    cb = off // width
    ts = _tile(s, 1024)

    def body(s_ref, dproj_hbm, o_ref):
        del dproj_hbm
        o_ref[...] = s_ref[...]

    return pl.pallas_call(
        body, grid=(s // ts,),
        in_specs=[pl.BlockSpec((ts, width), lambda i: (i, 0)), pl.BlockSpec(memory_space=pl.ANY)],
        out_specs=pl.BlockSpec((ts, width), lambda i: (i, cb)),
        out_shape=jax.ShapeDtypeStruct(dproj.shape, dproj.dtype),
        input_output_aliases={1: 0},
        compiler_params=_cparams(("parallel",)), name=name)(src, dproj)


LRU_G = 384


def _lru_gates(xl_ref, wa_ref, wx_ref, ba_ref, bx_ref, lam_ref, g):
    sl = slice(LRU_G * g, LRU_G * g + LRU_G)
    xg = xl_ref[:, sl]
    xm = _mx(xg)
    pa = _dot(xm, wa_ref[g]) + ba_ref[:, sl]
    r = jnp.where(pa < -12.0, jnp.exp(pa), _sigmoid(pa))
    ig = _sigmoid(_dot(xm, wx_ref[g]) + bx_ref[:, sl])
    sp = _softplus(-lam_ref[:, sl])
    log_a = (-LRU_C * r) * sp
    a = jnp.exp(log_a)
    mult = jnp.sqrt(_one_minus_sq(log_a, a))
    return sl, xg, r, ig, sp, a, mult


def _lru_fwd(xl, proj, wa, wx, ba, bx, lam):
    s = xl.shape[0]
    ts = _tile(s, 256)
    w = LRU_W

    def body(xl_ref, lg_ref, wa_ref, wx_ref, ba_ref, bx_ref, lam_ref, y_ref, hs_ref, a_ref, u_ref, carry_ref):
        @pl.when(pl.program_id(0) == 0)
        def _():
            carry_ref[...] = jnp.zeros_like(carry_ref)

        for g in range(4):
            sl, xg, _, ig, _, a, mult = _lru_gates(xl_ref, wa_ref, wx_ref, ba_ref, bx_ref, lam_ref, g)
            a_ref[:, sl] = a
            u_ref[:, sl] = mult * (ig * xg)

        def step(t, h):
            h = a_ref[pl.ds(t, 1), :] * h + u_ref[pl.ds(t, 1), :]
            hs_ref[pl.ds(t, 1), :] = h
            return h

        carry_ref[0:1, :] = lax.fori_loop(0, ts, step, carry_ref[0:1, :], unroll=8)
        y_ref[...] = (hs_ref[...] * _silu(lg_ref[...])).astype(y_ref.dtype)

    row = pl.BlockSpec((ts, w), lambda i: (i, 0))
    vec = pl.BlockSpec((1, w), lambda i: (0, 0))
    wsp = pl.BlockSpec((4, LRU_G, LRU_G), lambda i: (0, 0, 0))
    return pl.pallas_call(
        body, grid=(s // ts,),
        in_specs=[row, pl.BlockSpec((ts, w), lambda i: (i, LG[0] // w)), wsp, wsp, vec, vec, vec],
        out_specs=[row, row],
        out_shape=[jax.ShapeDtypeStruct((s, w), _MXU), jax.ShapeDtypeStruct((s, w), F32)],
        scratch_shapes=[pltpu.VMEM((ts, w), F32), pltpu.VMEM((ts, w), F32), pltpu.VMEM((8, w), F32)],
        compiler_params=_cparams(("arbitrary",)), name="lru_fwd")(xl, proj, wa, wx, ba, bx, lam)


def _lru_bwd(xl, proj, hs, dy, dproj, wa, wx, ba, bx, lam, ride=None):
    s = xl.shape[0]
    ts = _tile(s, 256)
    nt = s // ts
    w = LRU_W
    hb = ts // 8

    def body(xl_ref, lg_ref, hs_ref, hprev_ref, dy_ref, dproj_hbm, wa_ref, wx_ref, ba_ref, bx_ref, lam_ref,
             dlg_ref, dxl_ref, dwa_ref, dwx_ref, dba_ref, dbx_ref, dlam_ref,
             a_ref, dh_ref, ext_ref, carry_ref, r_ref, ig_ref, mult_ref):
        del dproj_hbm
        i = pl.program_id(0)

        @pl.when(i == 0)
        def _():
            carry_ref[...] = jnp.zeros_like(carry_ref)
            for ref in (dwa_ref, dwx_ref, dba_ref, dbx_ref, dlam_ref):
                ref[...] = jnp.zeros_like(ref)

        lg = lg_ref[...]
        dyv = dy_ref[...]
        dh_ref[...] = dyv * _silu(lg)
        dlg_ref[...] = (dyv * hs_ref[...] * _dsilu(lg)).astype(dlg_ref.dtype)
        for g in range(4):
            sl, _, r, ig, _, a, mult = _lru_gates(xl_ref, wa_ref, wx_ref, ba_ref, bx_ref, lam_ref, g)
            a_ref[:, sl] = a
            r_ref[:, sl] = r
            ig_ref[:, sl] = ig
            mult_ref[:, sl] = mult

        def step(k, carry):
            t = ts - 1 - k
            dh = dh_ref[pl.ds(t, 1), :] + carry
            dh_ref[pl.ds(t, 1), :] = dh
            return a_ref[pl.ds(t, 1), :] * dh

        carry_ref[0:1, :] = lax.fori_loop(0, ts, step, carry_ref[0:1, :], unroll=8)

        ext_ref[0:8, :] = jnp.where(i == nt - 1, 0.0, 1.0) * hprev_ref[...]
        ext_ref[8:8 + ts, :] = hs_ref[...]
        for g in range(4):
            sl = slice(LRU_G * g, LRU_G * g + LRU_G)
            xg, r, ig, a, mult = xl_ref[:, sl], r_ref[:, sl], ig_ref[:, sl], a_ref[:, sl], mult_ref[:, sl]
            sp = _softplus(-lam_ref[:, sl])
            dh = dh_ref[:, sl]
            da = dh * ext_ref[7:7 + ts, sl]
            dmult = dh * ig * xg
            di = dh * mult * xg
            dxl = dh * mult * ig
            dlog_a = da * a - dmult * (a * a) / mult
            dlam_ref[:, sl] += jnp.sum(dlog_a * r, axis=0, keepdims=True) * (LRU_C * _sigmoid(-lam_ref[:, sl]))
            dpa = dlog_a * (-LRU_C * sp) * r * (1.0 - r)
            dpx = di * ig * (1.0 - ig)
            dba_ref[:, sl] += jnp.sum(dpa, axis=0, keepdims=True)
            dbx_ref[:, sl] += jnp.sum(dpx, axis=0, keepdims=True)
            dpa_m, dpx_m, xm = _mx(dpa), _mx(dpx), _mx(xg)
            dxl_ref[:, sl] = dxl + _dot_nt(dpa_m, wa_ref[g]) + _dot_nt(dpx_m, wx_ref[g])
            dwa_ref[g] += _dot_tn(xm, dpa_m)
            dwx_ref[g] += _dot_tn(xm, dpx_m)

    rev = lambda i: nt - 1 - i
    row = pl.BlockSpec((ts, w), lambda i: (rev(i), 0))
    vec = pl.BlockSpec((1, w), lambda i: (0, 0))
    wsp = pl.BlockSpec((4, LRU_G, LRU_G), lambda i: (0, 0, 0))
    lgs = pl.BlockSpec((ts, w), lambda i: (rev(i), LG[0] // w))
    return _pcall(
        body, ride, (xl, proj, hs, hs, dy, dproj, wa, wx, ba, bx, lam), grid=(nt,),
        in_specs=[row, lgs, row, pl.BlockSpec((8, w), lambda i: (jnp.maximum(rev(i) * hb - 1, 0), 0)), row,
                  pl.BlockSpec(memory_space=pl.ANY), wsp, wsp, vec, vec, vec],
        out_specs=[lgs, row, wsp, wsp, vec, vec, vec],
        out_shape=[jax.ShapeDtypeStruct(dproj.shape, dproj.dtype), jax.ShapeDtypeStruct((s, w), F32),
                   jax.ShapeDtypeStruct((4, LRU_G, LRU_G), F32), jax.ShapeDtypeStruct((4, LRU_G, LRU_G), F32),
                   jax.ShapeDtypeStruct((1, w), F32), jax.ShapeDtypeStruct((1, w), F32),
                   jax.ShapeDtypeStruct((1, w), F32)],
        scratch_shapes=[pltpu.VMEM((ts, w), F32), pltpu.VMEM((ts, w), F32), pltpu.VMEM((ts + 8, w), F32),
                        pltpu.VMEM((8, w), F32), pltpu.VMEM((ts, w), F32), pltpu.VMEM((ts, w), F32),
                        pltpu.VMEM((ts, w), F32)],
        aliases={5: 0}, sem=("arbitrary",), name="lru_bwd")


def _mem_kv_fwd(mem, g, wkv):
    m = mem.shape[0]

    def body(mem_ref, g_ref, w_ref, k_ref, v_ref, mn_ref):
        mv = mem_ref[...]
        r = lax.rsqrt(jnp.mean(mv * mv, axis=-1, keepdims=True) + EPS)
        mn = _mx(mv * r * g_ref[...])
        mn_ref[...] = mn
        kv = _dot(mn, w_ref[...])
        k_ref[...] = kv[:, 0:D].astype(k_ref.dtype)
        v_ref[...] = kv[:, D:2 * D].astype(v_ref.dtype)

    sh = jax.ShapeDtypeStruct((m, D), _MXU)
    return pl.pallas_call(body, out_shape=[sh, sh, sh], compiler_params=_cparams(None), name="mem_kv_fwd")(mem, g, wkv)


def _mem_kv_bwd(mem, g, mn, wkv, dk, dv):
    m = mem.shape[0]

    def body(mem_ref, g_ref, mn_ref, w_ref, dk_ref, dv_ref, dw_ref, dg_ref):
        dkv = _mx(jnp.concatenate([dk_ref[...], dv_ref[...]], axis=1))
        dw_ref[...] = _dot_tn(mn_ref[...], dkv).astype(dw_ref.dtype)
        dmn = _dot_nt(dkv, w_ref[...])
        mv = mem_ref[...]
        r = lax.rsqrt(jnp.mean(mv * mv, axis=-1, keepdims=True) + EPS)
        dg_ref[...] = jnp.sum(dmn * mv * r, axis=0, keepdims=True)

    del m
    return pl.pallas_call(
        body, out_shape=[jax.ShapeDtypeStruct((D, 2 * D), _MXU), jax.ShapeDtypeStruct((1, D), F32)],
        compiler_params=_cparams(None), name="mem_kv_bwd")(mem, g, mn, wkv, dk, dv)


def _attn_probs(q_ref, k_ref, hd):
    sl = slice(MEM_HD * hd, MEM_HD * hd + MEM_HD)
    qh = _mx(q_ref[:, sl])
    sc = _dot_nt(qh, k_ref[:, sl]) * (MEM_HD ** -0.5)
    e = jnp.exp(sc - jnp.max(sc, axis=-1, keepdims=True))
    return sl, qh, e / jnp.sum(e, axis=-1, keepdims=True)


def _attn_fwd(proj, k, v):
    s = proj.shape[0]
    m = k.shape[0]
    ts = _tile(s, 512)

    def body(q_ref, k_ref, v_ref, y_ref):
        for hd in range(MEM_HEADS):
            sl, _, p = _attn_probs(q_ref, k_ref, hd)
            y_ref[:, sl] = _dot(_mx(p), v_ref[:, sl]).astype(y_ref.dtype)

    kvs = pl.BlockSpec((m, D), lambda i: (0, 0))
    return pl.pallas_call(
        body, grid=(s // ts,),
        in_specs=[pl.BlockSpec((ts, D), lambda i: (i, Q[0] // D)), kvs, kvs],
        out_specs=pl.BlockSpec((ts, D), lambda i: (i, 0)),
        out_shape=jax.ShapeDtypeStruct((s, D), _MXU),
        compiler_params=_cparams(("parallel",)), name="attn_fwd")(proj, k, v)


def _attn_bwd(proj, k, v, dy, dproj):
    s = proj.shape[0]
    m = k.shape[0]
    ts = _tile(s, 512)

    def body(q_ref, k_ref, v_ref, dy_ref, dproj_hbm, dq_ref, dk_ref, dv_ref):
        del dproj_hbm

        @pl.when(pl.program_id(0) == 0)
        def _():
            dk_ref[...] = jnp.zeros_like(dk_ref)
            dv_ref[...] = jnp.zeros_like(dv_ref)

        for hd in range(MEM_HEADS):
            sl, qh, p = _attn_probs(q_ref, k_ref, hd)
            dyh = _mx(dy_ref[:, sl])
            dp = _dot_nt(dyh, v_ref[:, sl])
            ds = _mx(p * (dp - jnp.sum(dp * p, axis=-1, keepdims=True)) * (MEM_HD ** -0.5))
            dq_ref[:, sl] = _dot(ds, k_ref[:, sl]).astype(dq_ref.dtype)
            dk_ref[:, sl] += _dot_tn(ds, qh)
            dv_ref[:, sl] += _dot_tn(_mx(p), dyh)

    kvs = pl.BlockSpec((m, D), lambda i: (0, 0))
    qs = pl.BlockSpec((ts, D), lambda i: (i, Q[0] // D))
    return pl.pallas_call(
        body, grid=(s // ts,),
        in_specs=[qs, kvs, kvs, pl.BlockSpec((ts, D), lambda i: (i, 0)), pl.BlockSpec(memory_space=pl.ANY)],
        out_specs=[qs, kvs, kvs],
        out_shape=[jax.ShapeDtypeStruct(dproj.shape, dproj.dtype), jax.ShapeDtypeStruct((m, D), F32),
                   jax.ShapeDtypeStruct((m, D), F32)],
        input_output_aliases={4: 0},
        compiler_params=_cparams(("arbitrary",)), name="attn_bwd")(proj, k, v, dy, dproj)


def _merge_fb(x, target, yssd, ylru, ymem, proj, wbs, wbl, wbm, wo, fg):
    s = x.shape[0]
    ts = _tile(s, 256)

    def body(x_ref, t_ref, ys_ref, yl_ref, ym_ref, gl_ref, wbs_ref, wbl_ref, wbm_ref, wo_ref, fg_ref,
             dgl_ref, dx2_ref, dx2m_ref, mg_ref, db0_ref, db1_ref, db2_ref, loss_ref, dfg_ref):
        @pl.when(pl.program_id(0) == 0)
        def _():
            loss_ref[...] = jnp.zeros_like(loss_ref)
            dfg_ref[...] = jnp.zeros_like(dfg_ref)

        bs = (_dot(ys_ref[...], wbs_ref[...]), _dot(yl_ref[...], wbl_ref[...]), _dot(ym_ref[...], wbm_ref[...]))
        gates = [_sigmoid(gl_ref[:, D * n:D * n + D]) for n in range(3)]
        merged = gates[0] * bs[0] + gates[1] * bs[1] + gates[2] * bs[2]
        mg = _mx(merged)
        mg_ref[...] = mg
        x2 = x_ref[...] + _dot(mg, wo_ref[...])
        r = lax.rsqrt(jnp.mean(x2 * x2, axis=-1, keepdims=True) + EPS)
        xhat = x2 * r
        err = xhat * fg_ref[...] - t_ref[...]
        loss_ref[...] += jnp.sum(err * err, axis=0, keepdims=True) * (0.5 / D)
        dy = err * (1.0 / D)
        dfg_ref[...] += jnp.sum(dy * xhat, axis=0, keepdims=True)
        dxh = dy * fg_ref[...]
        dx2 = r * (dxh - xhat * jnp.mean(dxh * xhat, axis=-1, keepdims=True))
        dx2_ref[...] = dx2
        dx2m = _mx(dx2)
        dx2m_ref[...] = dx2m
        dmg = _dot_nt(dx2m, wo_ref[...])
        for n, db_ref in enumerate((db0_ref, db1_ref, db2_ref)):
            gt = gates[n]
            dgl_ref[:, D * n:D * n + D] = (dmg * bs[n] * gt * (1.0 - gt)).astype(dgl_ref.dtype)
            db_ref[...] = (dmg * gt).astype(db_ref.dtype)

    row = lambda w: pl.BlockSpec((ts, w), lambda i: (i, 0))
    full = lambda a: pl.BlockSpec(a.shape, lambda i: (0, 0))
    vec = pl.BlockSpec((1, D), lambda i: (0, 0))
    gls = pl.BlockSpec((ts, GL[1]), lambda i: (i, GL[0] // GL[1]))
    act = jax.ShapeDtypeStruct((s, D), _MXU)
    return pl.pallas_call(
        body, grid=(s // ts,),
        in_specs=[row(D), row(D), row(SSD_W), row(LRU_W), row(D), gls, full(wbs), full(wbl), full(wbm), full(wo), vec],
        out_specs=[gls, row(D), row(D), row(D), row(D), row(D), row(D), vec, vec],
        out_shape=[jax.ShapeDtypeStruct((s, NP), _MXU), jax.ShapeDtypeStruct((s, D), F32), act, act, act, act, act,
                   jax.ShapeDtypeStruct((1, D), F32), jax.ShapeDtypeStruct((1, D), F32)],
        compiler_params=_cparams(("arbitrary",)), name="merge_fwd_bwd")(
            x, target, yssd, ylru, ymem, proj, wbs, wbl, wbm, wo, fg)


def _adamw(w, g, m, v, name):
    rows, cols = w.shape
    tr = _tile(rows, 512, 8)

    def body(w_ref, g_ref, m_ref, v_ref, d_ref, mo_ref, vo_ref):
        gv = g_ref[...]
        mn = ADAM_B1 * m_ref[...] + (1.0 - ADAM_B1) * gv
        vn = ADAM_B2 * v_ref[...] + (1.0 - ADAM_B2) * (gv * gv)
        m_hat = mn / (1.0 - ADAM_B1 ** ADAM_STEP)
        v_hat = vn / (1.0 - ADAM_B2 ** ADAM_STEP)
        d_ref[...] = -ADAM_LR * (m_hat / (jnp.sqrt(v_hat) + ADAM_EPS) + ADAM_WD * w_ref[...])
        mo_ref[...] = mn
        vo_ref[...] = vn

    blk = pl.BlockSpec((tr, cols), lambda i: (i, 0))
    sh = jax.ShapeDtypeStruct((rows, cols), F32)
    return pl.pallas_call(
        body, grid=(rows // tr,), in_specs=[blk] * 4, out_specs=[blk] * 3, out_shape=[sh] * 3,
        compiler_params=_cparams(("parallel",)), name=name)(w, g, m, v)


def _mesh_pos():
    x, y, c = lax.axis_index("x"), lax.axis_index("y"), lax.axis_index("c")
    chips = [(1 - x, y), (x, 1 - y), (1 - x, 1 - y)]
    return x, y, c, 2 * x + y, chips


def _hbm():
    return pl.BlockSpec(memory_space=pl.ANY)


def _remote(src, dst, send_sem, recv_sem, dev):
    return pltpu.make_async_remote_copy(src_ref=src, dst_ref=dst, send_sem=send_sem, recv_sem=recv_sem,
                                        device_id=dev, device_id_type=MESH)


def _sems(n):
    return [pltpu.SemaphoreType.DMA((n,)), pltpu.SemaphoreType.DMA((n,))]


class _Exchange:
    def __init__(self, inputs, out_shape, n_sem, start, finish, aliases=None):
        self.inputs, self.out_shape, self.n_sem = list(inputs), list(out_shape), n_sem
        self.start, self.finish, self.aliases = start, finish, dict(aliases or {})


def _run_exchange(ex, name):
    n_in, n_out = len(ex.inputs), len(ex.out_shape)

    def body(*refs):
        srcs, outs = refs[:n_in], refs[n_in:n_in + n_out]
        send_sems, recv_sems = refs[n_in + n_out:]
        ex.start(srcs, outs, send_sems, recv_sems)
        ex.finish(srcs, outs, send_sems, recv_sems)

    return pl.pallas_call(
        body, in_specs=[_hbm()] * n_in, out_specs=[_hbm()] * n_out, out_shape=ex.out_shape,
        input_output_aliases=ex.aliases, scratch_shapes=_sems(ex.n_sem), name=name)(*ex.inputs)


def _pcall(body, ride, args, *, grid, in_specs, out_specs, out_shape, scratch_shapes, sem, name, aliases=None):
    in_specs, out_specs, out_shape = list(in_specs), list(out_specs), list(out_shape)
    scratch_shapes, aliases = list(scratch_shapes), dict(aliases or {})
    if ride is None:
        outs = pl.pallas_call(
            body, grid=grid, in_specs=in_specs, out_specs=out_specs, out_shape=out_shape, scratch_shapes=scratch_shapes,
            input_output_aliases=aliases, compiler_params=_cparams(sem), name=name)(*args)
        return outs, None
    n_in, n_out, n_scr = len(in_specs), len(out_shape), len(scratch_shapes)
    e_in, e_out = len(ride.inputs), len(ride.out_shape)

    def carried(*refs):
        cut = [n_in, e_in, n_out, e_out, n_scr]
        parts, p = [], 0
        for c in cut:
            parts.append(refs[p:p + c])
            p += c
        ins, e_ins, outs, e_outs, scr = parts
        send_sems, recv_sems = refs[p], refs[p + 1]
        first = last = None
        for d, size in enumerate(grid):
            i = pl.program_id(d)
            first = (i == 0) if first is None else jnp.logical_and(first, i == 0)
            last = (i == size - 1) if last is None else jnp.logical_and(last, i == size - 1)

        @pl.when(first)
        def _():
            ride.start(e_ins, e_outs, send_sems, recv_sems)

        body(*ins, *outs, *scr)

        @pl.when(last)
        def _():
            ride.finish(e_ins, e_outs, send_sems, recv_sems)

    for k, v in ride.aliases.items():
        aliases[n_in + k] = n_out + v
    res = pl.pallas_call(
        carried, grid=grid, in_specs=in_specs + [_hbm()] * e_in, out_specs=out_specs + [_hbm()] * e_out,
        out_shape=out_shape + ride.out_shape, scratch_shapes=scratch_shapes + _sems(ride.n_sem),
        input_output_aliases=aliases, compiler_params=_cparams(("arbitrary",) * len(grid)),
        name=name)(*args, *ride.inputs)
    return res[:n_out], res[n_out:]


def _gather_shards(arrs, split):
    n = len(arrs)
    n_sem = sum(6 if sp else 3 for sp in split)

    def rows(i, which):
        if not split[i]:
            return pl.ds(0, arrs[i].shape[0])
        half = arrs[i].shape[0] // 2
        return pl.ds(which * half, half)

    def sends(srcs, outs, send_sems, recv_sems):
        _, _, c, me, chips = _mesh_pos()
        return [(_remote(srcs[i].at[rows(i, c)], outs[i].at[me, rows(i, c)], send_sems.at[3 * i + j],
                         recv_sems.at[3 * i + j], (cx, cy, c)), i, j)
                for i in range(n) for j, (cx, cy) in enumerate(chips)]

    def start(srcs, outs, send_sems, recv_sems):
        for cp, _, _ in sends(srcs, outs, send_sems, recv_sems):
            cp.start()

    def finish(srcs, outs, send_sems, recv_sems):
        x, y, c, _, chips = _mesh_pos()
        sib = (x, y, 1 - c)
        passed, k = [], 3 * n
        for i, j in [(i, j) for i in range(n) for j in range(3)]:
            cx, cy = chips[j]
            slot = outs[i].at[2 * cx + cy, rows(i, c)]
            _remote(slot, slot, send_sems.at[3 * i + j], recv_sems.at[3 * i + j], (cx, cy, c)).wait_recv()
            if split[i]:
                fwd = _remote(slot, slot, send_sems.at[k], recv_sems.at[k], sib)
                fwd.start()
                passed.append((fwd, i, j, k))
                k += 1
        for _, i, j, kf in passed:
            cx, cy = chips[j]
            slot = outs[i].at[2 * cx + cy, rows(i, 1 - c)]
            _remote(slot, slot, send_sems.at[kf], recv_sems.at[kf], sib).wait_recv()
        for cp in [s[0] for s in sends(srcs, outs, send_sems, recv_sems)] + [p[0] for p in passed]:
            cp.wait_send()

    return _Exchange(arrs, [jax.ShapeDtypeStruct((NSHARD,) + a.shape, a.dtype) for a in arrs], n_sem, start, finish)


def _with_own_slot(arrs, got):
    own_slot = jnp.arange(NSHARD, dtype=jnp.int32)[:, None, None] == 2 * lax.axis_index("x") + lax.axis_index("y")
    return [jnp.where(own_slot, a[None], g) for a, g in zip(arrs, got)]


def _swap_halves(arrs):
    n = len(arrs)

    def copies(srcs, outs, send_sems, recv_sems):
        x, y, c, _, _ = _mesh_pos()
        cps = []
        for i in range(n):
            half = arrs[i].shape[1] // 2
            cps.append(_remote(srcs[i].at[:, pl.ds((1 - c) * half, half)], outs[i], send_sems.at[i], recv_sems.at[i],
                               (x, y, 1 - c)))
        return cps

    def start(*refs):
        for cp in copies(*refs):
            cp.start()

    def finish(*refs):
        for cp in copies(*refs):
            cp.wait()

    shapes = [jax.ShapeDtypeStruct((NSHARD, a.shape[1] // 2, a.shape[2]), a.dtype) for a in arrs]
    return _Exchange(arrs, shapes, n, start, finish)


def _scatter_chips(arrs):
    n = len(arrs)

    def copies(srcs, outs, send_sems, recv_sems):
        _, _, c, me, chips = _mesh_pos()
        own = [pltpu.make_async_copy(srcs[i].at[me], outs[i].at[me], send_sems.at[3 * n + i]) for i in range(n)]
        cps = [_remote(srcs[i].at[2 * cx + cy], outs[i].at[me], send_sems.at[3 * i + j], recv_sems.at[3 * i + j],
                       (cx, cy, c)) for i in range(n) for j, (cx, cy) in enumerate(chips)]
        return own, cps

    def start(*refs):
        own, cps = copies(*refs)
        for cp in own + cps:
            cp.start()

    def finish(srcs, outs, send_sems, recv_sems):
        _, _, c, _, chips = _mesh_pos()
        for i in range(n):
            for j, (cx, cy) in enumerate(chips):
                slot = outs[i].at[2 * cx + cy]
                _remote(slot, slot, send_sems.at[3 * i + j], recv_sems.at[3 * i + j], (cx, cy, c)).wait_recv()
        own, cps = copies(srcs, outs, send_sems, recv_sems)
        for cp in cps:
            cp.wait_send()
        for cp in own:
            cp.wait()

    return _Exchange(arrs, [jax.ShapeDtypeStruct(a.shape, a.dtype) for a in arrs], 4 * n, start, finish)


def _share_halves(arrs):
    n = len(arrs)

    def copies(outs, send_sems, recv_sems):
        x, y, c, _, _ = _mesh_pos()
        return [_remote(outs[i].at[c], outs[i].at[c], send_sems.at[i], recv_sems.at[i], (x, y, 1 - c))
                for i in range(n)]

    def start(srcs, outs, send_sems, recv_sems):
        del srcs
        for cp in copies(outs, send_sems, recv_sems):
            cp.start()

    def finish(srcs, outs, send_sems, recv_sems):
        del srcs
        x, y, c, _, _ = _mesh_pos()
        for i in range(n):
            theirs = outs[i].at[1 - c]
            _remote(theirs, theirs, send_sems.at[i], recv_sems.at[i], (x, y, 1 - c)).wait_recv()
        for cp in copies(outs, send_sems, recv_sems):
            cp.wait_send()

    return _Exchange(arrs, [jax.ShapeDtypeStruct(a.shape, a.dtype) for a in arrs], n, start, finish,
                     aliases={i: i for i in range(n)})


def _gather_small(full):
    _, width = full.shape

    def copies(srcs, outs, send_sems, recv_sems):
        _, _, c, me, chips = _mesh_pos()
        mine = srcs[0].at[pl.ds(0, SMALL_ROWS)]
        own = pltpu.make_async_copy(mine, outs[0].at[me], send_sems.at[3])
        return own, [_remote(mine, outs[0].at[me], send_sems.at[j], recv_sems.at[j], (cx, cy, c))
                     for j, (cx, cy) in enumerate(chips)]

    def start(*refs):
        own, cps = copies(*refs)
        for cp in [own] + cps:
            cp.start()

    def finish(srcs, outs, send_sems, recv_sems):
        _, _, c, _, chips = _mesh_pos()
        for j, (cx, cy) in enumerate(chips):
            slot = outs[0].at[2 * cx + cy]
            _remote(slot, slot, send_sems.at[j], recv_sems.at[j], (cx, cy, c)).wait_recv()
        own, cps = copies(srcs, outs, send_sems, recv_sems)
        for cp in cps:
            cp.wait_send()
        own.wait()

    return _Exchange([full], [jax.ShapeDtypeStruct((NSHARD, SMALL_ROWS, width), full.dtype)], 4, start, finish)


def _add_sibling(mine, recv, c, name):
    _, half, width = recv.shape
    tr = _tile(half, 256, 8)
    nb = half // tr

    def body(c_ref, a_ref, b_ref, o_ref):
        del c_ref
        o_ref[...] = (a_ref[...].astype(F32) + b_ref[...].astype(F32)).astype(o_ref.dtype)

    grid_spec = pltpu.PrefetchScalarGridSpec(
        num_scalar_prefetch=1, grid=(NSHARD, nb),
        in_specs=[pl.BlockSpec((1, tr, width), lambda j, r, c_ref: (j, c_ref[0] * nb + r, 0)),
                  pl.BlockSpec((1, tr, width), lambda j, r, c_ref: (j, r, 0))],
        out_specs=pl.BlockSpec((1, tr, width), lambda j, r, c_ref: (j, r, 0)))
    return pl.pallas_call(
        body, grid_spec=grid_spec, out_shape=jax.ShapeDtypeStruct(recv.shape, recv.dtype),
        compiler_params=_cparams(("parallel", "parallel")), name=name)(c, mine, recv)


def _sum_chips(parts, c, name):
    _, half, width = parts.shape
    tr = _tile(half, 256, 8)

    def body(c_ref, p_ref, o_ref):
        del c_ref
        p = [p_ref[j].astype(F32) for j in range(NSHARD)]
        o_ref[0] = ((p[0] + p[1]) + p[2]) + p[3]

    grid_spec = pltpu.PrefetchScalarGridSpec(
        num_scalar_prefetch=1, grid=(half // tr,),
        in_specs=[pl.BlockSpec((NSHARD, tr, width), lambda r, c_ref: (0, r, 0))],
        out_specs=pl.BlockSpec((1, tr, width), lambda r, c_ref: (c_ref[0], r, 0)))
    return pl.pallas_call(
        body, grid_spec=grid_spec, out_shape=jax.ShapeDtypeStruct((2, half, width), F32),
        compiler_params=_cparams(("parallel",)), name=name)(c, parts)


def _unpack(flat, names, shapes):
    out, off = {}, 0
    for n in names:
        sz = _size(shapes[n])
        out[n] = flat[off:off + sz].reshape(shapes[n])
        off += sz
    return out


W_IN_COLS = 3080
W_IN_PAD = 3136


def _reorder_w_in_t(w):
    return jnp.concatenate([w[2048:5120], w[9248:12320], w[0:2048], w[8224:9248], w[5152:6688], w[6688:8224],
                            w[5120:5152], jnp.zeros((NP - 12320, D), w.dtype)], axis=0)


def _restore_w_in_t(g):
    return jnp.concatenate([g[6144:8192], g[0:3072], g[12288:12320], g[9216:10752], g[10752:12288], g[8192:9216],
                            g[3072:6144]], axis=0)


def _lru_group_weights(w):
    w4 = w.reshape(4, 4, 96, 96)
    eye = jnp.eye(4, dtype=w.dtype)
    return (w4[:, :, None, :, :] * eye[None, :, :, None, None]).transpose(0, 1, 3, 2, 4).reshape(4, LRU_G, LRU_G)


def _lru_group_blocks(g):
    g5 = g.reshape(4, 4, 96, 4, 96)
    return jnp.stack([g5[:, a, :, a, :] for a in range(4)], axis=1).reshape(16, 96, 96)


def _spread(a):
    return a.transpose(1, 0, 2).reshape(a.shape[1], NSHARD * a.shape[2])


def _split(a):
    return a.reshape(a.shape[0], NSHARD, a.shape[1] // NSHARD).transpose(1, 0, 2)


class _Reduction:
    def __init__(self, dist, parts, names):
        self.c, self.parts, self.names = dist.c, parts, names

    def swap(self):
        return _swap_halves(self.parts)

    def scatter(self, recv):
        return _scatter_chips([_add_sibling(p, r, self.c, "add_sibling_" + n)
                               for p, r, n in zip(self.parts, recv, self.names)])

    def share(self, landed):
        return _share_halves([_sum_chips(a, self.c, "sum_chips_" + n) for a, n in zip(landed, self.names)])

    def done(self, shared):
        return [a.reshape(2 * a.shape[1], a.shape[2]) for a in shared]


class _Dist:
    def __init__(self, w_in_shard, late_shards):
        self.c = lax.axis_index("c").astype(jnp.int32).reshape(1)
        self.w_in_shard = [w_in_shard]
        self.late_shards = late_shards

    def w_in_ride(self):
        return _gather_shards(self.w_in_shard, [True])

    def w_in_arrived(self, got):
        (g_in,) = _with_own_slot(self.w_in_shard, got)
        return _reorder_w_in_t(g_in[:, 0:W_IN_COLS].reshape(NSHARD * W_IN_COLS, D))

    def weights_ride(self):
        return _gather_shards(self.late_shards, [True, True, False])

    def weights_arrived(self, got):
        g_kv, g_rows, g_small = _with_own_slot(self.late_shards, got)
        out = {"w_kv": _spread(g_kv)}
        for n, lo_, hi_ in ROW_PIECES:
            out[n] = g_rows[:, lo_:hi_].reshape(NSHARD * (hi_ - lo_), D)
        out["ssd_conv_w"] = _spread(g_small[:, :, 0:768])
        out["ssd_norm_g"] = _spread(g_small[:, :, 768:896])
        out["lru_conv_w"] = _spread(g_small[:, :, 896:1280])
        return out

    def early_parts(self, grads):
        rows = jnp.concatenate([grads[n].reshape(NSHARD, hi_ - lo_, D) for n, lo_, hi_ in ROW_PIECES], axis=1)
        return [_split(grads["w_kv"]), rows]

    def late_parts(self, grads):
        rows = _restore_w_in_t(grads["w_in_rt"]).reshape(NSHARD, W_IN_COLS, D)
        return [jnp.pad(rows, ((0, 0), (0, W_IN_PAD - W_IN_COLS), (0, 0)))]


def _local_grads(x, mem, target, wts, dist=None):
    pad128 = lambda a: jnp.pad(a, ((0, 0), (0, 128 - a.shape[1])))

    if dist is None:
        (h,), _ = _norm_fwd(x, wts["norm_g"])
        w_in_rt = wts["w_in_rt"]
        proj = _mm(h, w_in_rt, F32, "in_proj", tb=True, tn=NP_TILE)
    else:
        (h,), arrived = _norm_fwd(x, wts["norm_g"], ride=dist.w_in_ride())
        w_in_rt = dist.w_in_arrived(arrived)
        proj, arrived = _mm(h, w_in_rt, F32, "in_proj", tb=True, tn=NP_TILE, ride=dist.weights_ride())
        wts = dict(wts, **dist.weights_arrived(arrived))
    wbs, wbl, wbm, wo, wkv = wts["w_br_ssd"], wts["w_br_lru"], wts["w_br_mem"], wts["w_out"], wts["w_kv"]
    wa, wx = _mx(_lru_group_weights(wts["lru_w_a"])), _mx(_lru_group_weights(wts["lru_w_x"]))
    ba, bx = wts["lru_b_a"].reshape(1, LRU_W), wts["lru_b_x"].reshape(1, LRU_W)
    dtb, alog = pad128(wts["ssd_dt_bias"]), pad128(wts["ssd_a_log"])
    dexp = jnp.repeat(wts["ssd_d"], 64, axis=1)
    ng = wts["ssd_norm_g"].reshape(1, SSD_W)
    xbc = _conv_fwd(proj, XBC, wts["ssd_conv_w"], wts["ssd_conv_b"], True, "ssd_conv_fwd")
    yssd, yraw, hprev = _ssd_fwd(xbc, proj, dtb, alog, dexp, ng)
    xl = _conv_fwd(proj, LX, wts["lru_conv_w"], wts["lru_conv_b"], False, "lru_conv_fwd")
    ylru, hs = _lru_fwd(xl, proj, wa, wx, ba, bx, wts["lru_lambda"])
    kk, vv, mn = _mem_kv_fwd(mem, wts["mem_norm_g"], wkv)
    ymem = _attn_fwd(proj, kk, vv)

    dproj, dx2, dx2m, merged, db0, db1, db2, loss_vec, dfg = _merge_fb(
        x, target, yssd, ylru, ymem, proj, wbs, wbl, wbm, wo, wts["final_g"].reshape(1, D))
    grads = {"final_g": dfg.reshape(D)}
    grads["w_out"] = _mm(merged, dx2m, _MXU, "dw_out", ta=True)
    grads["w_br_ssd"] = _mm(yssd, db0, _MXU, "dw_br_ssd", ta=True)
    grads["w_br_lru"] = _mm(ylru, db1, _MXU, "dw_br_lru", ta=True)
    grads["w_br_mem"] = _mm(ymem, db2, _MXU, "dw_br_mem", ta=True)
    dyssd = _mm(db0, wbs, F32, "dy_ssd", tb=True)
    dylru = _mm(db1, wbl, F32, "dy_lru", tb=True)
    dymem = _mm(db2, wbm, F32, "dy_mem", tb=True)

    dproj, dk, dv = _attn_bwd(proj, kk, vv, dymem, dproj)
    grads["w_kv"], grads["mem_norm_g"] = _mem_kv_bwd(mem, wts["mem_norm_g"], mn, wkv, dk, dv)

    early = None if dist is None else _Reduction(dist, dist.early_parts(grads), ["w_kv", "rows"])

    (dproj, dxl, dwa, dwx, dba, dbx, dlam), got = _lru_bwd(
        xl, proj, hs, dylru, dproj, wa, wx, ba, bx, wts["lru_lambda"], ride=early and early.swap())
    grads["lru_w_a"] = _lru_group_blocks(dwa)[None]
    grads["lru_w_x"] = _lru_group_blocks(dwx)[None]
    grads["lru_b_a"], grads["lru_b_x"] = dba.reshape(1, 16, 96), dbx.reshape(1, 16, 96)
    grads["lru_lambda"] = dlam
    (grads["lru_conv_w"], grads["lru_conv_b"]), _ = _conv_bwd_w(
        proj, LX, wts["lru_conv_w"], wts["lru_conv_b"], dxl, False, "lru_conv_bwd_w")
    dproj = _conv_bwd_x(dxl, wts["lru_conv_w"], dproj, LX, "lru_conv_bwd_x")

    (dproj, ddt, dxbc, dng, dda, ddd, ddtb), got = _ssd_bwd(
        xbc, proj, yraw, hprev, dyssd, dproj, dtb, alog, dexp, ng, ride=early and early.scatter(got))
    dproj = _put_block(ddt, dproj, DT, "put_ddt")
    grads["ssd_norm_g"] = dng.reshape(4, 512)
    grads["ssd_dt_bias"] = ddtb[:, 0:32]
    grads["ssd_a_log"] = (dda * -jnp.exp(alog))[:, 0:32]
    grads["ssd_d"] = ddd.reshape(32, 64).sum(axis=1)[None, :]
    (dpre, grads["ssd_conv_w"], grads["ssd_conv_b"]), got = _conv_bwd_w(
        proj, XBC, wts["ssd_conv_w"], wts["ssd_conv_b"], dxbc, True, "ssd_conv_bwd_w", ride=early and early.share(got))
    reduced = {} if dist is None else dict(zip(["w_kv", "rows"], early.done(got)))
    dproj = _conv_bwd_x(dpre, wts["ssd_conv_w"], dproj, XBC, "ssd_conv_bwd_x")

    grads["w_in_rt"] = _mm(dproj, h, _MXU, "dw_in", ta=True, tm=NP_TILE, tn=1024)
    if dist is None:
        dh = _mm(dproj, w_in_rt, F32, "dh", tn=1024, tk=NP_TILE)
        (grad_x, grads["norm_g"]), _ = _norm_bwd(x, wts["norm_g"], dh, dx2)
    else:
        late = _Reduction(dist, dist.late_parts(grads), ["w_in"])
        got = _run_exchange(late.swap(), "swap_halves_w_in")
        dh, got = _mm(dproj, w_in_rt, F32, "dh", tn=1024, tk=NP_TILE, ride=late.scatter(got))
        (grad_x, grads["norm_g"]), _ = _norm_bwd(x, wts["norm_g"], dh, dx2)
        reduced["w_in"] = late.done(_run_exchange(late.share(got), "share_halves_w_in"))[0]
    return jnp.sum(loss_vec), grad_x, grads, reduced


def kernel(x, mem, norm_g, w_in, ssd_conv_w, ssd_conv_b, ssd_dt_bias, ssd_a_log, ssd_d, ssd_norm_g, lru_conv_w, lru_conv_b, lru_w_a, lru_b_a, lru_w_x, lru_b_x, lru_lambda, mem_norm_g, w_kv, w_br_ssd, w_br_lru, w_br_mem, w_out, final_g, loss_target, m_norm_g, m_w_in, m_ssd_conv_w, m_ssd_conv_b, m_ssd_dt_bias, m_ssd_a_log, m_ssd_d, m_ssd_norm_g, m_lru_conv_w, m_lru_conv_b, m_lru_w_a, m_lru_b_a, m_lru_w_x, m_lru_b_x, m_lru_lambda, m_mem_norm_g, m_w_kv, m_w_br_ssd, m_w_br_lru, m_w_br_mem, m_w_out, m_final_g, v_norm_g, v_w_in, v_ssd_conv_w, v_ssd_conv_b, v_ssd_dt_bias, v_ssd_a_log, v_ssd_d, v_ssd_norm_g, v_lru_conv_w, v_lru_conv_b, v_lru_w_a, v_lru_b_a, v_lru_w_x, v_lru_b_x, v_lru_lambda, v_mem_norm_g, v_w_kv, v_w_br_ssd, v_w_br_lru, v_w_br_mem, v_w_out, v_final_g):
    given = dict(locals())

    rows_w = jnp.concatenate([w_br_ssd[0], w_br_lru[0], w_br_mem[0], w_out[0]], axis=0)
    small_w = jnp.concatenate([ssd_conv_w[0], ssd_norm_g[0], lru_conv_w[0]], axis=1)
    w_in_t = jnp.pad(_mx(w_in[0].T), ((0, W_IN_PAD - W_IN_COLS), (0, 0)))
    dist = _Dist(w_in_t, [_mx(w_kv[0]), _mx(rows_w), small_w])
    wts = {n: given[n] for n in REPL}
    wts["lru_w_a"], wts["lru_w_x"] = lru_w_a[0], lru_w_x[0]

    loss_part, grad_x, grads, reduced = _local_grads(x[0], mem[0], loss_target[0], wts, dist)
    loss = lax.psum(loss_part, ("x", "y", "c"))

    repl_flat = jnp.concatenate([grads[n].reshape(-1) for n in REPL])
    repl_flat = jnp.pad(repl_flat, (0, NSHARD * SMALL_Q - repl_flat.shape[0])).reshape(NSHARD, SMALL_Q)
    shard_small = jnp.concatenate([_split(grads[n]).reshape(NSHARD, -1) for n in SMALL_SHARDED], axis=1)
    p_small = jnp.concatenate(
        [repl_flat, shard_small, jnp.zeros((NSHARD, SMALL_BUF_ROWS * PACK_W - SMALL_Q - 5120), F32)], axis=1)
    small = _Reduction(dist, [p_small.reshape(NSHARD, SMALL_BUF_ROWS, PACK_W)], ["small"])
    got = _run_exchange(small.swap(), "swap_halves_small")
    got = _run_exchange(small.scatter(got), "scatter_chips_small")
    got = _run_exchange(small.share(got), "share_halves_small")
    r_small = small.done(got)[0]
    repl_all = _run_exchange(_gather_small(r_small), "gather_small")[0].reshape(-1)

    g_shard = {"w_kv": reduced["w_kv"]}
    for n, lo_, hi_ in ROW_PIECES:
        g_shard[n] = reduced["rows"][lo_:hi_]
    g_shard.update(_unpack(r_small.reshape(-1)[SMALL_Q:], SMALL_SHARDED, SHARD_SHAPE))
    g_repl = _unpack(repl_all, REPL, REPL_SHAPE)

    out_g, out_d, out_m, out_v = {}, {}, {}, {}
    for n in WEIGHTS:
        w_full = given[n]
        if n == "w_in":
            g2 = reduced["w_in"][0:W_IN_COLS]
            d, mo, vo = _adamw(w_in[0].T, g2, m_w_in[0].T, v_w_in[0].T, "adamw_w_in")
            out_g[n], out_d[n], out_m[n], out_v[n] = [a.T[None] for a in (g2, d, mo, vo)]
            continue
        g = (g_shard[n] if n in SHARDED else g_repl[n]).reshape(w_full.shape)
        cols = w_full.shape[-1]
        as2d = lambda a: a.reshape(-1, cols)
        d, mo, vo = _adamw(as2d(w_full), as2d(g), as2d(given["m_" + n]), as2d(given["v_" + n]), "adamw_" + n)
        out_g[n] = g
        out_d[n], out_m[n], out_v[n] = d.reshape(w_full.shape), mo.reshape(w_full.shape), vo.reshape(w_full.shape)

    return (loss, grad_x[None], *[out_g[n] for n in WEIGHTS], *[out_d[n] for n in WEIGHTS],
            *[out_m[n] for n in WEIGHTS], *[out_v[n] for n in WEIGHTS])
```

```python
import jax
import jax.numpy as jnp
from jax import lax
from jax.experimental import pallas as pl
from jax.experimental.pallas import tpu as pltpu

F32 = jnp.float32
_MXU = jnp.bfloat16
_HI = lax.Precision.HIGHEST
MESH = pl.DeviceIdType.MESH

D = 1024
EPS = 1e-6
MEM_HEADS = 4
MEM_HD = 256
LRU_C = 8.0
SSD_L = 128
SSD_W = 2048
LRU_W = 1536
NSHARD = 4

XBC = (0, 3072)
GL = (3072, 3072)
Z = (6144, 2048)
Q = (8192, 1024)
LG = (9216, 1536)
LX = (10752, 1536)
DT = (12288, 256)
NP = 12544
NP_TILE = 1792

ADAM_LR = 0.001
ADAM_B1 = 0.9
ADAM_B2 = 0.999
ADAM_EPS = 1e-08
ADAM_WD = 0.01
ADAM_STEP = 10

VMEM_LIMIT = 56 * 1024 * 1024

SHARDED = ("w_in", "ssd_conv_w", "ssd_norm_g", "lru_conv_w", "w_kv", "w_br_ssd", "w_br_lru", "w_br_mem", "w_out")
SHARD_SHAPE = {"w_in": (1024, 3080), "ssd_conv_w": (4, 768), "ssd_norm_g": (4, 128), "lru_conv_w": (4, 384),
               "w_kv": (1024, 512), "w_br_ssd": (512, 1024), "w_br_lru": (384, 1024), "w_br_mem": (256, 1024),
               "w_out": (256, 1024)}
REPL = ("norm_g", "ssd_conv_b", "ssd_dt_bias", "ssd_a_log", "ssd_d", "lru_conv_b", "lru_w_a", "lru_b_a",
        "lru_w_x", "lru_b_x", "lru_lambda", "mem_norm_g", "final_g")
REPL_SHAPE = {"norm_g": (1, 1024), "ssd_conv_b": (1, 3072), "ssd_dt_bias": (1, 32), "ssd_a_log": (1, 32),
              "ssd_d": (1, 32), "lru_conv_b": (1, 1536), "lru_w_a": (1, 16, 96, 96), "lru_b_a": (1, 16, 96),
              "lru_w_x": (1, 16, 96, 96), "lru_b_x": (1, 16, 96), "lru_lambda": (1, 1536),
              "mem_norm_g": (1, 1024), "final_g": (1024,)}
WEIGHTS = ("norm_g", "w_in", "ssd_conv_w", "ssd_conv_b", "ssd_dt_bias", "ssd_a_log", "ssd_d", "ssd_norm_g",
           "lru_conv_w", "lru_conv_b", "lru_w_a", "lru_b_a", "lru_w_x", "lru_b_x", "lru_lambda", "mem_norm_g",
           "w_kv", "w_br_ssd", "w_br_lru", "w_br_mem", "w_out", "final_g")

ROW_PIECES = (("w_br_ssd", 0, 512), ("w_br_lru", 512, 896), ("w_br_mem", 896, 1152), ("w_out", 1152, 1408))
SMALL_SHARDED = ("ssd_conv_w", "ssd_norm_g", "lru_conv_w")
PACK_W = 512
SMALL_ROWS = 152
SMALL_Q = SMALL_ROWS * PACK_W
SMALL_BUF_ROWS = 176


def _size(shape):
    n = 1
    for s in shape:
        n *= s
    return n


def _sigmoid(x):
    return 0.5 * jnp.tanh(0.5 * x) + 0.5


def _silu(x):
    return x * _sigmoid(x)


def _dsilu(x):
    s = _sigmoid(x)
    return s * (1.0 + x * (1.0 - s))


def _softplus(x):
    return jnp.maximum(x, 0.0) + jnp.log(1.0 + jnp.exp(-jnp.abs(x)))


def _one_minus_sq(log_a, a):
    x = 2.0 * log_a
    series = -x * (1.0 + x * (0.5 + x * (1.0 / 6.0 + x * (1.0 / 24.0))))
    return jnp.where(x > -0.03, series, 1.0 - a * a)


def _dot(a, b, precision=None):
    return jnp.dot(a, b, preferred_element_type=F32, precision=precision)


def _dot_nt(a, b):
    return lax.dot_general(a, b, (((1,), (1,)), ((), ())), preferred_element_type=F32)


def _dot_tn(a, b):
    return lax.dot_general(a, b, (((0,), (0,)), ((), ())), preferred_element_type=F32)


def _mx(a):
    return a.astype(_MXU)


def _cparams(sem):
    return pltpu.CompilerParams(dimension_semantics=sem, vmem_limit_bytes=VMEM_LIMIT)


def _tile(n, want, mult=128):
    if n <= want:
        return n
    for t in range(want - want % mult, 0, -mult):
        if n % t == 0:
            return t
    raise ValueError((n, want, mult))


def _mm(a, b, out_dtype, name, ta=False, tb=False, tm=1024, tn=1280, tk=1024, ride=None):
    k, m = a.shape if ta else a.shape[::-1]
    k2, n = b.shape[::-1] if tb else b.shape
    assert k == k2
    tm, tn, tk = _tile(m, tm), _tile(n, tn), _tile(k, tk)
    nk = k // tk
    contract = (((0 if ta else 1,), (1 if tb else 0,)), ((), ()))

    def body(a_ref, b_ref, o_ref, acc_ref):
        kk = pl.program_id(2)

        @pl.when(kk == 0)
        def _():
            acc_ref[...] = jnp.zeros_like(acc_ref)

        acc_ref[...] += lax.dot_general(a_ref[...], b_ref[...], contract, preferred_element_type=F32)

        @pl.when(kk == nk - 1)
        def _():
            o_ref[...] = acc_ref[...].astype(o_ref.dtype)

    a_spec = pl.BlockSpec((tk, tm), lambda i, j, kk: (kk, i)) if ta else pl.BlockSpec((tm, tk), lambda i, j, kk: (i, kk))
    b_spec = pl.BlockSpec((tn, tk), lambda i, j, kk: (j, kk)) if tb else pl.BlockSpec((tk, tn), lambda i, j, kk: (kk, j))
    outs, carried = _pcall(
        body, ride, (a, b), grid=(m // tm, n // tn, nk),
        in_specs=[a_spec, b_spec],
        out_specs=[pl.BlockSpec((tm, tn), lambda i, j, kk: (i, j))],
        out_shape=[jax.ShapeDtypeStruct((m, n), out_dtype)],
        scratch_shapes=[pltpu.VMEM((tm, tn), F32)],
        sem=("parallel", "parallel", "arbitrary"), name=name)
    return outs[0] if ride is None else (outs[0], carried)


def _norm_fwd(x, g, ride=None):
    s = x.shape[0]
    ts = _tile(s, 512)

    def body(x_ref, g_ref, h_ref):
        xv = x_ref[...]
        r = lax.rsqrt(jnp.mean(xv * xv, axis=-1, keepdims=True) + EPS)
        h_ref[...] = (xv * r * g_ref[...]).astype(h_ref.dtype)

    return _pcall(
        body, ride, (x, g), grid=(s // ts,),
        in_specs=[pl.BlockSpec((ts, D), lambda i: (i, 0)), pl.BlockSpec((1, D), lambda i: (0, 0))],
        out_specs=[pl.BlockSpec((ts, D), lambda i: (i, 0))],
        out_shape=[jax.ShapeDtypeStruct((s, D), _MXU)], scratch_shapes=[], sem=("parallel",), name="norm_fwd")


def _norm_bwd(x, g, dh, dx2, ride=None):
    s = x.shape[0]
    ts = _tile(s, 512)

    def body(x_ref, g_ref, dh_ref, dx2_ref, gx_ref, dg_ref):
        @pl.when(pl.program_id(0) == 0)
        def _():
            dg_ref[...] = jnp.zeros_like(dg_ref)

        xv = x_ref[...]
        r = lax.rsqrt(jnp.mean(xv * xv, axis=-1, keepdims=True) + EPS)
        xhat = xv * r
        dh_v = dh_ref[...]
        dg_ref[...] += jnp.sum(dh_v * xhat, axis=0, keepdims=True)
        dxh = dh_v * g_ref[...]
        gx_ref[...] = dx2_ref[...] + r * (dxh - xhat * jnp.mean(dxh * xhat, axis=-1, keepdims=True))

    row = pl.BlockSpec((ts, D), lambda i: (i, 0))
    vec = pl.BlockSpec((1, D), lambda i: (0, 0))
    return _pcall(
        body, ride, (x, g, dh, dx2), grid=(s // ts,), in_specs=[row, vec, row, row], out_specs=[row, vec],
        out_shape=[jax.ShapeDtypeStruct((s, D), F32), jax.ShapeDtypeStruct((1, D), F32)],
        scratch_shapes=[], sem=("arbitrary",), name="norm_bwd")


CONV_RB = 16
CONV_LC = 256


def _fold8(v):
    acc = v[0:8]
    for r0 in range(8, v.shape[0], 8):
        acc = acc + v[r0:r0 + 8]
    return acc


def _conv_fwd(src, blk, w, b, act, name):
    s = src.shape[0]
    off, width = blk
    cb = off // width
    ts = _tile(s, 256)

    def body(x_ref, w_ref, b_ref, o_ref, ext_ref):
        @pl.when(pl.program_id(0) == 0)
        def _():
            ext_ref[0:8, :] = jnp.zeros((8, width), F32)

        ext_ref[8:8 + ts, :] = x_ref[...]
        for l0 in range(0, width, CONV_LC):
            ls = slice(l0, l0 + CONV_LC)
            taps = [w_ref[k:k + 1, ls] for k in range(4)]
            bias = b_ref[:, ls]
            for r0 in range(0, ts, CONV_RB):
                pre = bias
                for k in range(4):
                    pre = pre + taps[k] * ext_ref[5 + k + r0:5 + k + r0 + CONV_RB, ls]
                o_ref[r0:r0 + CONV_RB, ls] = _silu(pre) if act else pre
        ext_ref[0:8, :] = x_ref[ts - 8:ts, :]

    return pl.pallas_call(
        body, grid=(s // ts,),
        in_specs=[pl.BlockSpec((ts, width), lambda i: (i, cb)), pl.BlockSpec((4, width), lambda i: (0, 0)),
                  pl.BlockSpec((1, width), lambda i: (0, 0))],
        out_specs=pl.BlockSpec((ts, width), lambda i: (i, 0)),
        out_shape=jax.ShapeDtypeStruct((s, width), F32),
        scratch_shapes=[pltpu.VMEM((ts + 8, width), F32)],
        compiler_params=_cparams(("arbitrary",)), name=name)(src, w, b)


def _conv_bwd_w(src, blk, w, b, dout, act, name, ride=None):
    s = src.shape[0]
    off, width = blk
    cb = off // width
    ts = _tile(s, 256)

    def body(x_ref, w_ref, b_ref, do_ref, *rest):
        if act:
            dpre_ref, dw_ref, db_ref, ext_ref = rest
        else:
            dw_ref, db_ref, ext_ref = rest

        @pl.when(pl.program_id(0) == 0)
        def _():
            ext_ref[0:8, :] = jnp.zeros((8, width), F32)
            dw_ref[...] = jnp.zeros_like(dw_ref)
            db_ref[...] = jnp.zeros_like(db_ref)

        ext_ref[8:8 + ts, :] = x_ref[...]
        for l0 in range(0, width, CONV_LC):
            ls = slice(l0, l0 + CONV_LC)
            taps = [w_ref[k:k + 1, ls] for k in range(4)]
            bias = b_ref[:, ls]
            acc_b = jnp.zeros((8, CONV_LC), F32)
            acc_w = [jnp.zeros((8, CONV_LC), F32) for _ in range(4)]
            for r0 in range(0, ts, CONV_RB):
                xs = [ext_ref[5 + k + r0:5 + k + r0 + CONV_RB, ls] for k in range(4)]
                dpre = do_ref[r0:r0 + CONV_RB, ls]
                if act:
                    pre = bias
                    for k in range(4):
                        pre = pre + taps[k] * xs[k]
                    dpre = dpre * _dsilu(pre)
                    dpre_ref[r0:r0 + CONV_RB, ls] = dpre
                acc_b = acc_b + _fold8(dpre)
                for k in range(4):
                    acc_w[k] = acc_w[k] + _fold8(dpre * xs[k])
            db_ref[:, ls] += jnp.sum(acc_b, axis=0, keepdims=True)
            for k in range(4):
                dw_ref[k:k + 1, ls] += jnp.sum(acc_w[k], axis=0, keepdims=True)
        ext_ref[0:8, :] = x_ref[ts - 8:ts, :]

    row = pl.BlockSpec((ts, width), lambda i: (i, 0))
    outs = [pl.BlockSpec((4, width), lambda i: (0, 0)), pl.BlockSpec((1, width), lambda i: (0, 0))]
    shapes = [jax.ShapeDtypeStruct((4, width), F32), jax.ShapeDtypeStruct((1, width), F32)]
    if act:
        outs = [row] + outs
        shapes = [jax.ShapeDtypeStruct((s, width), F32)] + shapes
    return _pcall(
        body, ride, (src, w, b, dout), grid=(s // ts,),
        in_specs=[pl.BlockSpec((ts, width), lambda i: (i, cb)), pl.BlockSpec((4, width), lambda i: (0, 0)),
                  pl.BlockSpec((1, width), lambda i: (0, 0)), row],
        out_specs=outs, out_shape=shapes,
        scratch_shapes=[pltpu.VMEM((ts + 8, width), F32)], sem=("arbitrary",), name=name)


def _conv_bwd_x(dpre, w, dproj, blk, name):
    s = dpre.shape[0]
    off, width = blk
    cb = off // width
    ts = _tile(s, 256)
    nt = s // ts

    def body(dp_ref, w_ref, dproj_hbm, o_ref, ext_ref):
        del dproj_hbm

        @pl.when(pl.program_id(0) == 0)
        def _():
            ext_ref[ts:ts + 8, :] = jnp.zeros((8, width), F32)

        ext_ref[0:ts, :] = dp_ref[...]
        for l0 in range(0, width, CONV_LC):
            ls = slice(l0, l0 + CONV_LC)
            taps = [w_ref[k:k + 1, ls] for k in range(4)]
            for r0 in range(0, ts, CONV_RB):
                acc = taps[0] * ext_ref[3 + r0:3 + r0 + CONV_RB, ls]
                for k in range(1, 4):
                    acc = acc + taps[k] * ext_ref[3 - k + r0:3 - k + r0 + CONV_RB, ls]
                o_ref[r0:r0 + CONV_RB, ls] = acc.astype(o_ref.dtype)
        ext_ref[ts:ts + 8, :] = dp_ref[0:8, :]

    return pl.pallas_call(
        body, grid=(nt,),
        in_specs=[pl.BlockSpec((ts, width), lambda i: (nt - 1 - i, 0)), pl.BlockSpec((4, width), lambda i: (0, 0)),
                  pl.BlockSpec(memory_space=pl.ANY)],
        out_specs=pl.BlockSpec((ts, width), lambda i: (nt - 1 - i, cb)),
        out_shape=jax.ShapeDtypeStruct(dproj.shape, dproj.dtype),
        scratch_shapes=[pltpu.VMEM((ts + 8, width), F32)],
        input_output_aliases={2: 0},
        compiler_params=_cparams(("arbitrary",)), name=name)(dpre, w, dproj)


def _ssd_decay(a_cs, acst_ref, h, causal, lane_l):
    col = jnp.sum(jnp.where(lane_l == h, a_cs, 0.0), axis=1, keepdims=True)
    row = acst_ref[h:h + 1, :]
    return jnp.where(causal, jnp.exp(jnp.minimum(col - row, 0.0)), 0.0)


def _split3(x):
    hi = x.astype(jnp.bfloat16)
    rest = x - hi.astype(F32)
    mid = rest.astype(jnp.bfloat16)
    return jnp.concatenate([hi, mid, (rest - mid.astype(F32)).astype(jnp.bfloat16)], axis=1)


def _spread_matrix():
    col = jnp.arange(128, dtype=jnp.int32)[:, None]
    e64 = (col == jnp.arange(SSD_W, dtype=jnp.int32)[None, :] // 64).astype(jnp.bfloat16)
    return jnp.tile(e64, (3, 1))


def _ssd_common(dt_ref, dtb_ref, alog_ref, e64_ref, acst_ref, dtx_ref, acx_ref):
    ll = SSD_L
    dt = _softplus(dt_ref[:, 0:128] + dtb_ref[...])
    a_neg = -jnp.exp(alog_ref[...])
    ri = lax.broadcasted_iota(jnp.int32, (ll, ll), 0)
    ci = lax.broadcasted_iota(jnp.int32, (ll, ll), 1)
    causal = ri >= ci
    a_cs = _dot(causal.astype(F32), dt * a_neg, _HI)
    acst_ref[...] = a_cs.T
    both = _dot(jnp.concatenate([_split3(dt), _split3(a_cs)], axis=0), e64_ref[...])
    dtx_ref[...] = both[0:ll]
    acx_ref[...] = both[ll:2 * ll]
    lane_l = lax.broadcasted_iota(jnp.int32, (ll, 128), 1)
    return dt, a_neg, a_cs, causal, ri, lane_l, lane_l < 64


def _ssd_fwd(xbc, proj, dtb, alog, dexp, ng):
    s = xbc.shape[0]
    ll = SSD_L
    nc = s // ll
    e64 = _spread_matrix()

    def body(xbc_ref, dt_ref, z_ref, dtb_ref, alog_ref, dexp_ref, ng_ref, e64_ref,
             yssd_ref, yraw_ref, hprev_ref, ht_ref, acst_ref, dtx_ref, acx_ref):
        @pl.when(pl.program_id(0) == 0)
        def _():
            ht_ref[...] = jnp.zeros_like(ht_ref)

        hprev_ref[0] = ht_ref[...]
        _, _, a_cs, causal, _, lane_l, lo = _ssd_common(dt_ref, dtb_ref, alog_ref, e64_ref, acst_ref, dtx_ref, acx_ref)
        for g in range(4):
            bg = _mx(xbc_ref[:, 2048 + 128 * g:2176 + 128 * g])
            cg = _mx(xbc_ref[:, 2560 + 128 * g:2688 + 128 * g])
            cbm = _dot_nt(cg, bg)
            for jj in range(4):
                j = 4 * g + jj
                sl = slice(128 * j, 128 * j + 128)
                xp = xbc_ref[:, sl]
                acx = acx_ref[:, sl]
                a_last = acx_ref[ll - 1:ll, sl]
                xdt = xp * dtx_ref[:, sl]
                acc = None
                for hh in range(2):
                    dec = _ssd_decay(a_cs, acst_ref, 2 * j + hh, causal, lane_l)
                    xm = jnp.where(lo if hh == 0 else jnp.logical_not(lo), xdt, 0.0)
                    t = _dot(_mx(dec * cbm), _mx(xm))
                    acc = t if acc is None else acc + t
                ht = ht_ref[j]
                y = acc + _dot(cg, _mx(ht)) * jnp.exp(acx) + xp * dexp_ref[:, sl]
                yraw_ref[:, sl] = y
                st = _dot_tn(bg, _mx(xdt * jnp.exp(a_last - acx)))
                ht_ref[j] = ht * jnp.exp(a_last) + st
        for g in range(4):
            sl = slice(512 * g, 512 * g + 512)
            yg = yraw_ref[:, sl] * _silu(z_ref[:, sl])
            r = lax.rsqrt(jnp.mean(yg * yg, axis=-1, keepdims=True) + EPS)
            yssd_ref[:, sl] = (yg * r * ng_ref[:, sl]).astype(yssd_ref.dtype)

    vec = lambda w: pl.BlockSpec((1, w), lambda c: (0, 0))
    return pl.pallas_call(
        body, grid=(nc,),
        in_specs=[pl.BlockSpec((ll, 3072), lambda c: (c, 0)),
                  pl.BlockSpec((ll, DT[1]), lambda c: (c, DT[0] // DT[1])),
                  pl.BlockSpec((ll, Z[1]), lambda c: (c, Z[0] // Z[1])),
                  vec(128), vec(128), vec(2048), vec(2048),
                  pl.BlockSpec(e64.shape, lambda c: (0, 0))],
        out_specs=[pl.BlockSpec((ll, 2048), lambda c: (c, 0)), pl.BlockSpec((ll, 2048), lambda c: (c, 0)),
                   pl.BlockSpec((1, 16, 128, 128), lambda c: (c, 0, 0, 0))],
        out_shape=[jax.ShapeDtypeStruct((s, 2048), _MXU), jax.ShapeDtypeStruct((s, 2048), F32),
                   jax.ShapeDtypeStruct((nc, 16, 128, 128), F32)],
        scratch_shapes=[pltpu.VMEM((16, 128, 128), F32), pltpu.VMEM((128, ll), F32),
                        pltpu.VMEM((ll, 2048), F32), pltpu.VMEM((ll, 2048), F32)],
        compiler_params=_cparams(("arbitrary",)), name="ssd_fwd")(xbc, proj, proj, dtb, alog, dexp, ng, e64)


def _ssd_bwd(xbc, proj, yraw, hprev, dyssd, dproj, dtb, alog, dexp, ng, ride=None):
    s = xbc.shape[0]
    ll = SSD_L
    nc = s // ll
    e64 = _spread_matrix()

    def body(xbc_ref, dt_ref, z_ref, yraw_ref, hprev_ref, dy_ref, dproj_hbm, dtb_ref, alog_ref, dexp_ref, ng_ref,
             e64_ref,
             dz_ref, ddt_ref, dxbc_ref, dng_ref, dda_ref, ddd_ref, ddtb_ref,
             dht_ref, acst_ref, dtx_ref, acx_ref, dyr_ref, rowt_ref):
        del dproj_hbm

        @pl.when(pl.program_id(0) == 0)
        def _():
            dht_ref[...] = jnp.zeros_like(dht_ref)
            dng_ref[...] = jnp.zeros_like(dng_ref)
            dda_ref[...] = jnp.zeros_like(dda_ref)
            ddd_ref[...] = jnp.zeros_like(ddd_ref)
            ddtb_ref[...] = jnp.zeros_like(ddtb_ref)
            rowt_ref[...] = jnp.zeros_like(rowt_ref)

        for g in range(4):
            sl = slice(512 * g, 512 * g + 512)
            zz = z_ref[:, sl]
            yr = yraw_ref[:, sl]
            sz = _silu(zz)
            yg = yr * sz
            r = lax.rsqrt(jnp.mean(yg * yg, axis=-1, keepdims=True) + EPS)
            yhat = yg * r
            dyv = dy_ref[:, sl]
            dng_ref[:, sl] += jnp.sum(dyv * yhat, axis=0, keepdims=True)
            dyh = dyv * ng_ref[:, sl]
            dyg = r * (dyh - yhat * jnp.mean(dyh * yhat, axis=-1, keepdims=True))
            dz_ref[:, sl] = (dyg * yr * _dsilu(zz)).astype(dz_ref.dtype)
            dyr_ref[:, sl] = dyg * sz

        dt, a_neg, a_cs, causal, ri, lane_l, lo = _ssd_common(dt_ref, dtb_ref, alog_ref, e64_ref,
                                                              acst_ref, dtx_ref, acx_ref)
        lane_1 = lax.broadcasted_iota(jnp.int32, (1, 128), 1)
        da_col = jnp.zeros((ll, 128), F32)
        ddt_x = jnp.zeros((ll, 128), F32)
        last = jnp.zeros((1, 128), F32)
        for g in range(4):
            bg = _mx(xbc_ref[:, 2048 + 128 * g:2176 + 128 * g])
            cg = _mx(xbc_ref[:, 2560 + 128 * g:2688 + 128 * g])
            cbm = _dot_nt(cg, bg)
            dcb = jnp.zeros((ll, ll), F32)
            db_g = jnp.zeros((ll, 128), F32)
            dc_g = jnp.zeros((ll, 128), F32)
            for jj in range(4):
                j = 4 * g + jj
                sl = slice(128 * j, 128 * j + 128)
                xp = xbc_ref[:, sl]
                dtx = dtx_ref[:, sl]
                acx = acx_ref[:, sl]
                a_last = acx_ref[ll - 1:ll, sl]
                ea = jnp.exp(acx)
                dte = jnp.exp(a_last - acx)
                cd = jnp.exp(a_last)
                xdt = xp * dtx
                xdt_m = _mx(xdt)
                dy = dyr_ref[:, sl]
                ht = hprev_ref[0, j]
                dhn = dht_ref[j]
                dhn_m = _mx(dhn)
                gmat = _dot(bg, dhn_m)
                dxdt = gmat * dte
                for hh in range(2):
                    h = 2 * j + hh
                    dec = _ssd_decay(a_cs, acst_ref, h, causal, lane_l)
                    mm = dec * cbm
                    dym = _mx(jnp.where(lo if hh == 0 else jnp.logical_not(lo), dy, 0.0))
                    dxdt = dxdt + _dot_tn(_mx(mm), dym)
                    dm = _dot_nt(dym, xdt_m)
                    dcb = dcb + dm * dec
                    qq = dm * mm
                    da_col = da_col + jnp.where(lane_l == h, jnp.sum(qq, axis=1, keepdims=True), 0.0)
                    rowt_ref[h:h + 1, :] = jnp.sum(qq, axis=0, keepdims=True)
                ch = _dot(cg, _mx(ht))
                dyea = dy * ea
                dyea_m = _mx(dyea)
                xw_m = _mx(xdt * dte)
                dc_g = dc_g + _dot_nt(dyea_m, _mx(ht))
                db_g = db_g + _dot_nt(xw_m, dhn_m)
                wl = xdt * gmat * dte
                lane_a = dyea * ch - wl
                lane_b = dxdt * xp
                lane_c = jnp.sum(dhn * ht, axis=0, keepdims=True) * cd + jnp.sum(wl, axis=0, keepdims=True)
                for hh in range(2):
                    h = 2 * j + hh
                    mine = lo if hh == 0 else jnp.logical_not(lo)
                    da_col = da_col + jnp.where(
                        lane_l == h, jnp.sum(jnp.where(mine, lane_a, 0.0), axis=1, keepdims=True), 0.0)
                    ddt_x = ddt_x + jnp.where(
                        lane_l == h, jnp.sum(jnp.where(mine, lane_b, 0.0), axis=1, keepdims=True), 0.0)
                    mine_1 = (lane_1 < 64) if hh == 0 else (lane_1 >= 64)
                    last = last + jnp.where(
                        lane_1 == h, jnp.sum(jnp.where(mine_1, lane_c, 0.0), axis=1, keepdims=True), 0.0)
                dht_ref[j] = dhn * cd + _dot_tn(cg, dyea_m)
                dxbc_ref[:, sl] = dxdt * dtx + dy * dexp_ref[:, sl]
                ddd_ref[:, sl] += jnp.sum(dy * xp, axis=0, keepdims=True)
            dcb_m = _mx(dcb)
            dxbc_ref[:, 2048 + 128 * g:2176 + 128 * g] = db_g + _dot_tn(dcb_m, cg)
            dxbc_ref[:, 2560 + 128 * g:2688 + 128 * g] = dc_g + _dot(dcb_m, bg)

        da_cs = da_col - rowt_ref[...].T
        da_cs = da_cs + jnp.where(lax.broadcasted_iota(jnp.int32, (ll, 128), 0) == ll - 1, last, 0.0)
        d_dta = _dot((ri <= lax.broadcasted_iota(jnp.int32, (ll, ll), 1)).astype(F32), da_cs, _HI)
        ddt = d_dta * a_neg + ddt_x
        dda_ref[...] += jnp.sum(d_dta * dt, axis=0, keepdims=True)
        ddt_raw = ddt * _sigmoid(dt_ref[:, 0:128] + dtb_ref[...])
        ddtb_ref[...] += jnp.sum(ddt_raw, axis=0, keepdims=True)
        ddt_ref[:, 0:128] = ddt_raw.astype(ddt_ref.dtype)
        ddt_ref[:, 128:DT[1]] = jnp.zeros((ll, DT[1] - 128), ddt_ref.dtype)

    rev = lambda c: nc - 1 - c
    vec = lambda w: pl.BlockSpec((1, w), lambda c: (0, 0))
    row = lambda w: pl.BlockSpec((ll, w), lambda c: (rev(c), 0))
    return _pcall(
        body, ride, (xbc, proj, proj, yraw, hprev, dyssd, dproj, dtb, alog, dexp, ng, e64), grid=(nc,),
        in_specs=[row(3072),
                  pl.BlockSpec((ll, DT[1]), lambda c: (rev(c), DT[0] // DT[1])),
                  pl.BlockSpec((ll, Z[1]), lambda c: (rev(c), Z[0] // Z[1])),
                  row(2048),
                  pl.BlockSpec((1, 16, 128, 128), lambda c: (rev(c), 0, 0, 0)),
                  row(2048),
                  pl.BlockSpec(memory_space=pl.ANY),
                  vec(128), vec(128), vec(2048), vec(2048),
                  pl.BlockSpec(e64.shape, lambda c: (0, 0))],
        out_specs=[pl.BlockSpec((ll, Z[1]), lambda c: (rev(c), Z[0] // Z[1])),
                   row(DT[1]),
                   row(3072), vec(2048), vec(128), vec(2048), vec(128)],
        out_shape=[jax.ShapeDtypeStruct(dproj.shape, dproj.dtype), jax.ShapeDtypeStruct((s, DT[1]), dproj.dtype),
                   jax.ShapeDtypeStruct((s, 3072), F32), jax.ShapeDtypeStruct((1, 2048), F32),
                   jax.ShapeDtypeStruct((1, 128), F32), jax.ShapeDtypeStruct((1, 2048), F32),
                   jax.ShapeDtypeStruct((1, 128), F32)],
        scratch_shapes=[pltpu.VMEM((16, 128, 128), F32), pltpu.VMEM((128, ll), F32),
                        pltpu.VMEM((ll, 2048), F32), pltpu.VMEM((ll, 2048), F32), pltpu.VMEM((ll, 2048), F32),
                        pltpu.VMEM((128, ll), F32)],
        aliases={6: 0}, sem=("arbitrary",), name="ssd_bwd")


def _put_block(src, dproj, blk, name):
    s = src.shape[0]
    off, width = blk
    cb = off // width
    ts = _tile(s, 1024)

    def body(s_ref, dproj_hbm, o_ref):
        del dproj_hbm
        o_ref[...] = s_ref[...]

    return pl.pallas_call(
        body, grid=(s // ts,),
        in_specs=[pl.BlockSpec((ts, width), lambda i: (i, 0)), pl.BlockSpec(memory_space=pl.ANY)],
        out_specs=pl.BlockSpec((ts, width), lambda i: (i, cb)),
        out_shape=jax.ShapeDtypeStruct(dproj.shape, dproj.dtype),
        input_output_aliases={1: 0},
        compiler_params=_cparams(("parallel",)), name=name)(src, dproj)


LRU_G = 384


def _lru_gates(xl_ref, wa_ref, wx_ref, ba_ref, bx_ref, lam_ref, g):
    sl = slice(LRU_G * g, LRU_G * g + LRU_G)
    xg = xl_ref[:, sl]
    xm = _mx(xg)
    pa = _dot(xm, wa_ref[g]) + ba_ref[:, sl]
    r = jnp.where(pa < -12.0, jnp.exp(pa), _sigmoid(pa))
    ig = _sigmoid(_dot(xm, wx_ref[g]) + bx_ref[:, sl])
    sp = _softplus(-lam_ref[:, sl])
    log_a = (-LRU_C * r) * sp
    a = jnp.exp(log_a)
    mult = jnp.sqrt(_one_minus_sq(log_a, a))
    return sl, xg, r, ig, sp, a, mult


def _lru_fwd(xl, proj, wa, wx, ba, bx, lam):
    s = xl.shape[0]
    ts = _tile(s, 256)
    w = LRU_W

    def body(xl_ref, lg_ref, wa_ref, wx_ref, ba_ref, bx_ref, lam_ref, y_ref, hs_ref, a_ref, u_ref, carry_ref):
        @pl.when(pl.program_id(0) == 0)
        def _():
            carry_ref[...] = jnp.zeros_like(carry_ref)

        for g in range(4):
            sl, xg, _, ig, _, a, mult = _lru_gates(xl_ref, wa_ref, wx_ref, ba_ref, bx_ref, lam_ref, g)
            a_ref[:, sl] = a
            u_ref[:, sl] = mult * (ig * xg)

        def step(t, h):
            h = a_ref[pl.ds(t, 1), :] * h + u_ref[pl.ds(t, 1), :]
            hs_ref[pl.ds(t, 1), :] = h
            return h

        carry_ref[0:1, :] = lax.fori_loop(0, ts, step, carry_ref[0:1, :], unroll=8)
        y_ref[...] = (hs_ref[...] * _silu(lg_ref[...])).astype(y_ref.dtype)

    row = pl.BlockSpec((ts, w), lambda i: (i, 0))
    vec = pl.BlockSpec((1, w), lambda i: (0, 0))
    wsp = pl.BlockSpec((4, LRU_G, LRU_G), lambda i: (0, 0, 0))
    return pl.pallas_call(
        body, grid=(s // ts,),
        in_specs=[row, pl.BlockSpec((ts, w), lambda i: (i, LG[0] // w)), wsp, wsp, vec, vec, vec],
        out_specs=[row, row],
        out_shape=[jax.ShapeDtypeStruct((s, w), _MXU), jax.ShapeDtypeStruct((s, w), F32)],
        scratch_shapes=[pltpu.VMEM((ts, w), F32), pltpu.VMEM((ts, w), F32), pltpu.VMEM((8, w), F32)],
        compiler_params=_cparams(("arbitrary",)), name="lru_fwd")(xl, proj, wa, wx, ba, bx, lam)


def _lru_bwd(xl, proj, hs, dy, dproj, wa, wx, ba, bx, lam, ride=None):
    s = xl.shape[0]
    ts = _tile(s, 256)
    nt = s // ts
    w = LRU_W
    hb = ts // 8

    def body(xl_ref, lg_ref, hs_ref, hprev_ref, dy_ref, dproj_hbm, wa_ref, wx_ref, ba_ref, bx_ref, lam_ref,
             dlg_ref, dxl_ref, dwa_ref, dwx_ref, dba_ref, dbx_ref, dlam_ref,
             a_ref, dh_ref, ext_ref, carry_ref, r_ref, ig_ref, mult_ref):
        del dproj_hbm
        i = pl.program_id(0)

        @pl.when(i == 0)
        def _():
            carry_ref[...] = jnp.zeros_like(carry_ref)
            for ref in (dwa_ref, dwx_ref, dba_ref, dbx_ref, dlam_ref):
                ref[...] = jnp.zeros_like(ref)

        lg = lg_ref[...]
        dyv = dy_ref[...]
        dh_ref[...] = dyv * _silu(lg)
        dlg_ref[...] = (dyv * hs_ref[...] * _dsilu(lg)).astype(dlg_ref.dtype)
        for g in range(4):
            sl, _, r, ig, _, a, mult = _lru_gates(xl_ref, wa_ref, wx_ref, ba_ref, bx_ref, lam_ref, g)
            a_ref[:, sl] = a
            r_ref[:, sl] = r
            ig_ref[:, sl] = ig
            mult_ref[:, sl] = mult

        def step(k, carry):
            t = ts - 1 - k
            dh = dh_ref[pl.ds(t, 1), :] + carry
            dh_ref[pl.ds(t, 1), :] = dh
            return a_ref[pl.ds(t, 1), :] * dh

        carry_ref[0:1, :] = lax.fori_loop(0, ts, step, carry_ref[0:1, :], unroll=8)

        ext_ref[0:8, :] = jnp.where(i == nt - 1, 0.0, 1.0) * hprev_ref[...]
        ext_ref[8:8 + ts, :] = hs_ref[...]
        for g in range(4):
            sl = slice(LRU_G * g, LRU_G * g + LRU_G)
            xg, r, ig, a, mult = xl_ref[:, sl], r_ref[:, sl], ig_ref[:, sl], a_ref[:, sl], mult_ref[:, sl]
            sp = _softplus(-lam_ref[:, sl])
            dh = dh_ref[:, sl]
            da = dh * ext_ref[7:7 + ts, sl]
            dmult = dh * ig * xg
            di = dh * mult * xg
            dxl = dh * mult * ig
            dlog_a = da * a - dmult * (a * a) / mult
            dlam_ref[:, sl] += jnp.sum(dlog_a * r, axis=0, keepdims=True) * (LRU_C * _sigmoid(-lam_ref[:, sl]))
            dpa = dlog_a * (-LRU_C * sp) * r * (1.0 - r)
            dpx = di * ig * (1.0 - ig)
            dba_ref[:, sl] += jnp.sum(dpa, axis=0, keepdims=True)
            dbx_ref[:, sl] += jnp.sum(dpx, axis=0, keepdims=True)
            dpa_m, dpx_m, xm = _mx(dpa), _mx(dpx), _mx(xg)
            dxl_ref[:, sl] = dxl + _dot_nt(dpa_m, wa_ref[g]) + _dot_nt(dpx_m, wx_ref[g])
            dwa_ref[g] += _dot_tn(xm, dpa_m)
            dwx_ref[g] += _dot_tn(xm, dpx_m)

    rev = lambda i: nt - 1 - i
    row = pl.BlockSpec((ts, w), lambda i: (rev(i), 0))
    vec = pl.BlockSpec((1, w), lambda i: (0, 0))
    wsp = pl.BlockSpec((4, LRU_G, LRU_G), lambda i: (0, 0, 0))
    lgs = pl.BlockSpec((ts, w), lambda i: (rev(i), LG[0] // w))
    return _pcall(
        body, ride, (xl, proj, hs, hs, dy, dproj, wa, wx, ba, bx, lam), grid=(nt,),
        in_specs=[row, lgs, row, pl.BlockSpec((8, w), lambda i: (jnp.maximum(rev(i) * hb - 1, 0), 0)), row,
                  pl.BlockSpec(memory_space=pl.ANY), wsp, wsp, vec, vec, vec],
        out_specs=[lgs, row, wsp, wsp, vec, vec, vec],
        out_shape=[jax.ShapeDtypeStruct(dproj.shape, dproj.dtype), jax.ShapeDtypeStruct((s, w), F32),
                   jax.ShapeDtypeStruct((4, LRU_G, LRU_G), F32), jax.ShapeDtypeStruct((4, LRU_G, LRU_G), F32),
                   jax.ShapeDtypeStruct((1, w), F32), jax.ShapeDtypeStruct((1, w), F32),
                   jax.ShapeDtypeStruct((1, w), F32)],
        scratch_shapes=[pltpu.VMEM((ts, w), F32), pltpu.VMEM((ts, w), F32), pltpu.VMEM((ts + 8, w), F32),
                        pltpu.VMEM((8, w), F32), pltpu.VMEM((ts, w), F32), pltpu.VMEM((ts, w), F32),
                        pltpu.VMEM((ts, w), F32)],
        aliases={5: 0}, sem=("arbitrary",), name="lru_bwd")


def _mem_kv_fwd(mem, g, wkv):
    m = mem.shape[0]

    def body(mem_ref, g_ref, w_ref, k_ref, v_ref, mn_ref):
        mv = mem_ref[...]
        r = lax.rsqrt(jnp.mean(mv * mv, axis=-1, keepdims=True) + EPS)
        mn = _mx(mv * r * g_ref[...])
        mn_ref[...] = mn
        kv = _dot(mn, w_ref[...])
        k_ref[...] = kv[:, 0:D].astype(k_ref.dtype)
        v_ref[...] = kv[:, D:2 * D].astype(v_ref.dtype)

    sh = jax.ShapeDtypeStruct((m, D), _MXU)
    return pl.pallas_call(body, out_shape=[sh, sh, sh], compiler_params=_cparams(None), name="mem_kv_fwd")(mem, g, wkv)


def _mem_kv_bwd(mem, g, mn, wkv, dk, dv):
    m = mem.shape[0]

    def body(mem_ref, g_ref, mn_ref, w_ref, dk_ref, dv_ref, dw_ref, dg_ref):
        dkv = _mx(jnp.concatenate([dk_ref[...], dv_ref[...]], axis=1))
        dw_ref[...] = _dot_tn(mn_ref[...], dkv).astype(dw_ref.dtype)
        dmn = _dot_nt(dkv, w_ref[...])
        mv = mem_ref[...]
        r = lax.rsqrt(jnp.mean(mv * mv, axis=-1, keepdims=True) + EPS)
        dg_ref[...] = jnp.sum(dmn * mv * r, axis=0, keepdims=True)

    del m
    return pl.pallas_call(
        body, out_shape=[jax.ShapeDtypeStruct((D, 2 * D), _MXU), jax.ShapeDtypeStruct((1, D), F32)],
        compiler_params=_cparams(None), name="mem_kv_bwd")(mem, g, mn, wkv, dk, dv)


def _attn_probs(q_ref, k_ref, hd):
    sl = slice(MEM_HD * hd, MEM_HD * hd + MEM_HD)
    qh = _mx(q_ref[:, sl])
    sc = _dot_nt(qh, k_ref[:, sl]) * (MEM_HD ** -0.5)
    e = jnp.exp(sc - jnp.max(sc, axis=-1, keepdims=True))
    return sl, qh, e / jnp.sum(e, axis=-1, keepdims=True)


def _attn_fwd(proj, k, v):
    s = proj.shape[0]
    m = k.shape[0]
    ts = _tile(s, 512)

    def body(q_ref, k_ref, v_ref, y_ref):
        for hd in range(MEM_HEADS):
            sl, _, p = _attn_probs(q_ref, k_ref, hd)
            y_ref[:, sl] = _dot(_mx(p), v_ref[:, sl]).astype(y_ref.dtype)

    kvs = pl.BlockSpec((m, D), lambda i: (0, 0))
    return pl.pallas_call(
        body, grid=(s // ts,),
        in_specs=[pl.BlockSpec((ts, D), lambda i: (i, Q[0] // D)), kvs, kvs],
        out_specs=pl.BlockSpec((ts, D), lambda i: (i, 0)),
        out_shape=jax.ShapeDtypeStruct((s, D), _MXU),
        compiler_params=_cparams(("parallel",)), name="attn_fwd")(proj, k, v)


def _attn_bwd(proj, k, v, dy, dproj):
    s = proj.shape[0]
    m = k.shape[0]
    ts = _tile(s, 512)

    def body(q_ref, k_ref, v_ref, dy_ref, dproj_hbm, dq_ref, dk_ref, dv_ref):
        del dproj_hbm

        @pl.when(pl.program_id(0) == 0)
        def _():
            dk_ref[...] = jnp.zeros_like(dk_ref)
            dv_ref[...] = jnp.zeros_like(dv_ref)

        for hd in range(MEM_HEADS):
            sl, qh, p = _attn_probs(q_ref, k_ref, hd)
            dyh = _mx(dy_ref[:, sl])
            dp = _dot_nt(dyh, v_ref[:, sl])
            ds = _mx(p * (dp - jnp.sum(dp * p, axis=-1, keepdims=True)) * (MEM_HD ** -0.5))
            dq_ref[:, sl] = _dot(ds, k_ref[:, sl]).astype(dq_ref.dtype)
            dk_ref[:, sl] += _dot_tn(ds, qh)
            dv_ref[:, sl] += _dot_tn(_mx(p), dyh)

    kvs = pl.BlockSpec((m, D), lambda i: (0, 0))
    qs = pl.BlockSpec((ts, D), lambda i: (i, Q[0] // D))
    return pl.pallas_call(
        body, grid=(s // ts,),
        in_specs=[qs, kvs, kvs, pl.BlockSpec((ts, D), lambda i: (i, 0)), pl.BlockSpec(memory_space=pl.ANY)],
        out_specs=[qs, kvs, kvs],
        out_shape=[jax.ShapeDtypeStruct(dproj.shape, dproj.dtype), jax.ShapeDtypeStruct((m, D), F32),
                   jax.ShapeDtypeStruct((m, D), F32)],
        input_output_aliases={4: 0},
        compiler_params=_cparams(("arbitrary",)), name="attn_bwd")(proj, k, v, dy, dproj)


def _merge_fb(x, target, yssd, ylru, ymem, proj, wbs, wbl, wbm, wo, fg):
    s = x.shape[0]
    ts = _tile(s, 256)

    def body(x_ref, t_ref, ys_ref, yl_ref, ym_ref, gl_ref, wbs_ref, wbl_ref, wbm_ref, wo_ref, fg_ref,
             dgl_ref, dx2_ref, dx2m_ref, mg_ref, db0_ref, db1_ref, db2_ref, loss_ref, dfg_ref):
        @pl.when(pl.program_id(0) == 0)
        def _():
            loss_ref[...] = jnp.zeros_like(loss_ref)
            dfg_ref[...] = jnp.zeros_like(dfg_ref)

        bs = (_dot(ys_ref[...], wbs_ref[...]), _dot(yl_ref[...], wbl_ref[...]), _dot(ym_ref[...], wbm_ref[...]))
        gates = [_sigmoid(gl_ref[:, D * n:D * n + D]) for n in range(3)]
        merged = gates[0] * bs[0] + gates[1] * bs[1] + gates[2] * bs[2]
        mg = _mx(merged)
        mg_ref[...] = mg
        x2 = x_ref[...] + _dot(mg, wo_ref[...])
        r = lax.rsqrt(jnp.mean(x2 * x2, axis=-1, keepdims=True) + EPS)
        xhat = x2 * r
        err = xhat * fg_ref[...] - t_ref[...]
        loss_ref[...] += jnp.sum(err * err, axis=0, keepdims=True) * (0.5 / D)
        dy = err * (1.0 / D)
        dfg_ref[...] += jnp.sum(dy * xhat, axis=0, keepdims=True)
        dxh = dy * fg_ref[...]
        dx2 = r * (dxh - xhat * jnp.mean(dxh * xhat, axis=-1, keepdims=True))
        dx2_ref[...] = dx2
        dx2m = _mx(dx2)
        dx2m_ref[...] = dx2m
        dmg = _dot_nt(dx2m, wo_ref[...])
        for n, db_ref in enumerate((db0_ref, db1_ref, db2_ref)):
            gt = gates[n]
            dgl_ref[:, D * n:D * n + D] = (dmg * bs[n] * gt * (1.0 - gt)).astype(dgl_ref.dtype)
            db_ref[...] = (dmg * gt).astype(db_ref.dtype)

    row = lambda w: pl.BlockSpec((ts, w), lambda i: (i, 0))
    full = lambda a: pl.BlockSpec(a.shape, lambda i: (0, 0))
    vec = pl.BlockSpec((1, D), lambda i: (0, 0))
    gls = pl.BlockSpec((ts, GL[1]), lambda i: (i, GL[0] // GL[1]))
    act = jax.ShapeDtypeStruct((s, D), _MXU)
    return pl.pallas_call(
        body, grid=(s // ts,),
        in_specs=[row(D), row(D), row(SSD_W), row(LRU_W), row(D), gls, full(wbs), full(wbl), full(wbm), full(wo), vec],
        out_specs=[gls, row(D), row(D), row(D), row(D), row(D), row(D), vec, vec],
        out_shape=[jax.ShapeDtypeStruct((s, NP), _MXU), jax.ShapeDtypeStruct((s, D), F32), act, act, act, act, act,
                   jax.ShapeDtypeStruct((1, D), F32), jax.ShapeDtypeStruct((1, D), F32)],
        compiler_params=_cparams(("arbitrary",)), name="merge_fwd_bwd")(
            x, target, yssd, ylru, ymem, proj, wbs, wbl, wbm, wo, fg)


def _adamw(w, g, m, v, name):
    rows, cols = w.shape
    tr = _tile(rows, 512, 8)

    def body(w_ref, g_ref, m_ref, v_ref, d_ref, mo_ref, vo_ref):
        gv = g_ref[...]
        mn = ADAM_B1 * m_ref[...] + (1.0 - ADAM_B1) * gv
        vn = ADAM_B2 * v_ref[...] + (1.0 - ADAM_B2) * (gv * gv)
        m_hat = mn / (1.0 - ADAM_B1 ** ADAM_STEP)
        v_hat = vn / (1.0 - ADAM_B2 ** ADAM_STEP)
        d_ref[...] = -ADAM_LR * (m_hat / (jnp.sqrt(v_hat) + ADAM_EPS) + ADAM_WD * w_ref[...])
        mo_ref[...] = mn
        vo_ref[...] = vn

    blk = pl.BlockSpec((tr, cols), lambda i: (i, 0))
    sh = jax.ShapeDtypeStruct((rows, cols), F32)
    return pl.pallas_call(
        body, grid=(rows // tr,), in_specs=[blk] * 4, out_specs=[blk] * 3, out_shape=[sh] * 3,
        compiler_params=_cparams(("parallel",)), name=name)(w, g, m, v)


def _mesh_pos():
    x, y, c = lax.axis_index("x"), lax.axis_index("y"), lax.axis_index("c")
    chips = [(1 - x, y), (x, 1 - y), (1 - x, 1 - y)]
    return x, y, c, 2 * x + y, chips


def _hbm():
    return pl.BlockSpec(memory_space=pl.ANY)


def _remote(src, dst, send_sem, recv_sem, dev):
    return pltpu.make_async_remote_copy(src_ref=src, dst_ref=dst, send_sem=send_sem, recv_sem=recv_sem,
                                        device_id=dev, device_id_type=MESH)


def _sems(n):
    return [pltpu.SemaphoreType.DMA((n,)), pltpu.SemaphoreType.DMA((n,))]


class _Exchange:
    def __init__(self, inputs, out_shape, n_sem, start, finish, aliases=None):
        self.inputs, self.out_shape, self.n_sem = list(inputs), list(out_shape), n_sem
        self.start, self.finish, self.aliases = start, finish, dict(aliases or {})


def _run_exchange(ex, name):
    n_in, n_out = len(ex.inputs), len(ex.out_shape)

    def body(*refs):
        srcs, outs = refs[:n_in], refs[n_in:n_in + n_out]
        send_sems, recv_sems = refs[n_in + n_out:]
        ex.start(srcs, outs, send_sems, recv_sems)
        ex.finish(srcs, outs, send_sems, recv_sems)

    return pl.pallas_call(
        body, in_specs=[_hbm()] * n_in, out_specs=[_hbm()] * n_out, out_shape=ex.out_shape,
        input_output_aliases=ex.aliases, scratch_shapes=_sems(ex.n_sem), name=name)(*ex.inputs)


def _pcall(body, ride, args, *, grid, in_specs, out_specs, out_shape, scratch_shapes, sem, name, aliases=None):
    in_specs, out_specs, out_shape = list(in_specs), list(out_specs), list(out_shape)
    scratch_shapes, aliases = list(scratch_shapes), dict(aliases or {})
    if ride is None:
        outs = pl.pallas_call(
            body, grid=grid, in_specs=in_specs, out_specs=out_specs, out_shape=out_shape, scratch_shapes=scratch_shapes,
            input_output_aliases=aliases, compiler_params=_cparams(sem), name=name)(*args)
        return outs, None
    n_in, n_out, n_scr = len(in_specs), len(out_shape), len(scratch_shapes)
    e_in, e_out = len(ride.inputs), len(ride.out_shape)

    def carried(*refs):
        cut = [n_in, e_in, n_out, e_out, n_scr]
        parts, p = [], 0
        for c in cut:
            parts.append(refs[p:p + c])
            p += c
        ins, e_ins, outs, e_outs, scr = parts
        send_sems, recv_sems = refs[p], refs[p + 1]
        first = last = None
        for d, size in enumerate(grid):
            i = pl.program_id(d)
            first = (i == 0) if first is None else jnp.logical_and(first, i == 0)
            last = (i == size - 1) if last is None else jnp.logical_and(last, i == size - 1)

        @pl.when(first)
        def _():
            ride.start(e_ins, e_outs, send_sems, recv_sems)

        body(*ins, *outs, *scr)

        @pl.when(last)
        def _():
            ride.finish(e_ins, e_outs, send_sems, recv_sems)

    for k, v in ride.aliases.items():
        aliases[n_in + k] = n_out + v
    res = pl.pallas_call(
        carried, grid=grid, in_specs=in_specs + [_hbm()] * e_in, out_specs=out_specs + [_hbm()] * e_out,
        out_shape=out_shape + ride.out_shape, scratch_shapes=scratch_shapes + _sems(ride.n_sem),
        input_output_aliases=aliases, compiler_params=_cparams(("arbitrary",) * len(grid)),
        name=name)(*args, *ride.inputs)
    return res[:n_out], res[n_out:]


def _gather_shards(arrs, split):
    n = len(arrs)
    n_sem = sum(6 if sp else 3 for sp in split)

    def rows(i, which):
        if not split[i]:
            return pl.ds(0, arrs[i].shape[0])
        half = arrs[i].shape[0] // 2
        return pl.ds(which * half, half)

    def sends(srcs, outs, send_sems, recv_sems):
        _, _, c, me, chips = _mesh_pos()
        return [_remote(srcs[i].at[rows(i, c)], outs[i].at[me, rows(i, c)], send_sems.at[3 * i + j],
                        recv_sems.at[3 * i + j], (cx, cy, c))
                for i in range(n) for j, (cx, cy) in enumerate(chips) if not (split[i] and j == 2)]

    def start(srcs, outs, send_sems, recv_sems):
        for cp in sends(srcs, outs, send_sems, recv_sems):
            cp.start()

    def finish(srcs, outs, send_sems, recv_sems):
        x, y, c, _, chips = _mesh_pos()
        sib = (x, y, 1 - c)
        first = ((x + 1 - c) % 2, (y + c) % 2)
        other = ((x + c) % 2, (y + 1 - c) % 2)
        started, k = [], 3 * n
        for i in range(n):
            sem = lambda j, i=i: (send_sems.at[3 * i + j], recv_sems.at[3 * i + j])
            if not split[i]:
                for j, (cx, cy) in enumerate(chips):
                    slot = outs[i].at[2 * cx + cy]
                    _remote(slot, slot, *sem(j), (cx, cy, c)).wait_recv()
                continue
            to_sib = lambda j, k=k: (send_sems.at[k + j], recv_sems.at[k + j])
            slot = lambda chip, which, i=i: outs[i].at[2 * chip[0] + chip[1], rows(i, which)]
            got = slot(first, c)
            _remote(got, got, *sem(c), (*first, c)).wait_recv()
            started += [_remote(got, got, *sem(2), (*other, c)), _remote(got, got, *to_sib(c), sib)]
            started[-2].start()
            started[-1].start()
            for chip, j_in, j_sib in ((other, 1 - c, 1 - c), (chips[2], 2, 2)):
                got = slot(chip, c)
                _remote(got, got, *sem(j_in), (*chip, c)).wait_recv()
                started.append(_remote(got, got, *to_sib(j_sib), sib))
                started[-1].start()
            for chip, j_sib in ((first, c), (other, 1 - c), (chips[2], 2)):
                theirs = slot(chip, 1 - c)
                _remote(theirs, theirs, *to_sib(j_sib), sib).wait_recv()
            k += 3
        for cp in sends(srcs, outs, send_sems, recv_sems) + started:
            cp.wait_send()

    return _Exchange(arrs, [jax.ShapeDtypeStruct((NSHARD,) + a.shape, a.dtype) for a in arrs], n_sem, start, finish)


def _with_own_slot(arrs, got):
    own_slot = jnp.arange(NSHARD, dtype=jnp.int32)[:, None, None] == 2 * lax.axis_index("x") + lax.axis_index("y")
    return [jnp.where(own_slot, a[None], g) for a, g in zip(arrs, got)]


def _swap_halves(arrs):
    n = len(arrs)

    def copies(srcs, outs, send_sems, recv_sems):
        x, y, c, _, _ = _mesh_pos()
        cps = []
        for i in range(n):
            half = arrs[i].shape[1] // 2
            cps.append(_remote(srcs[i].at[:, pl.ds((1 - c) * half, half)], outs[i], send_sems.at[i], recv_sems.at[i],
                               (x, y, 1 - c)))
        return cps

    def start(*refs):
        for cp in copies(*refs):
            cp.start()

    def finish(*refs):
        for cp in copies(*refs):
            cp.wait()

    shapes = [jax.ShapeDtypeStruct((NSHARD, a.shape[1] // 2, a.shape[2]), a.dtype) for a in arrs]
    return _Exchange(arrs, shapes, n, start, finish)


def _scatter_chips(arrs):
    n = len(arrs)

    def copies(srcs, outs, send_sems, recv_sems):
        _, _, c, me, chips = _mesh_pos()
        own = [pltpu.make_async_copy(srcs[i].at[me], outs[i].at[me], send_sems.at[3 * n + i]) for i in range(n)]
        cps = [_remote(srcs[i].at[2 * cx + cy], outs[i].at[me], send_sems.at[3 * i + j], recv_sems.at[3 * i + j],
                       (cx, cy, c)) for i in range(n) for j, (cx, cy) in enumerate(chips)]
        return own, cps

    def start(*refs):
        own, cps = copies(*refs)
        for cp in own + cps:
            cp.start()

    def finish(srcs, outs, send_sems, recv_sems):
        _, _, c, _, chips = _mesh_pos()
        for i in range(n):
            for j, (cx, cy) in enumerate(chips):
                slot = outs[i].at[2 * cx + cy]
                _remote(slot, slot, send_sems.at[3 * i + j], recv_sems.at[3 * i + j], (cx, cy, c)).wait_recv()
        own, cps = copies(srcs, outs, send_sems, recv_sems)
        for cp in cps:
            cp.wait_send()
        for cp in own:
            cp.wait()

    return _Exchange(arrs, [jax.ShapeDtypeStruct(a.shape, a.dtype) for a in arrs], 4 * n, start, finish)


def _share_halves(arrs):
    n = len(arrs)

    def copies(outs, send_sems, recv_sems):
        x, y, c, _, _ = _mesh_pos()
        return [_remote(outs[i].at[c], outs[i].at[c], send_sems.at[i], recv_sems.at[i], (x, y, 1 - c))
                for i in range(n)]

    def start(srcs, outs, send_sems, recv_sems):
        del srcs
        for cp in copies(outs, send_sems, recv_sems):
            cp.start()

    def finish(srcs, outs, send_sems, recv_sems):
        del srcs
        x, y, c, _, _ = _mesh_pos()
        for i in range(n):
            theirs = outs[i].at[1 - c]
            _remote(theirs, theirs, send_sems.at[i], recv_sems.at[i], (x, y, 1 - c)).wait_recv()
        for cp in copies(outs, send_sems, recv_sems):
            cp.wait_send()

    return _Exchange(arrs, [jax.ShapeDtypeStruct(a.shape, a.dtype) for a in arrs], n, start, finish,
                     aliases={i: i for i in range(n)})


def _gather_small(full):
    _, width = full.shape

    def copies(srcs, outs, send_sems, recv_sems):
        _, _, c, me, chips = _mesh_pos()
        mine = srcs[0].at[pl.ds(0, SMALL_ROWS)]
        own = pltpu.make_async_copy(mine, outs[0].at[me], send_sems.at[3])
        return own, [_remote(mine, outs[0].at[me], send_sems.at[j], recv_sems.at[j], (cx, cy, c))
                     for j, (cx, cy) in enumerate(chips)]

    def start(*refs):
        own, cps = copies(*refs)
        for cp in [own] + cps:
            cp.start()

    def finish(srcs, outs, send_sems, recv_sems):
        _, _, c, _, chips = _mesh_pos()
        for j, (cx, cy) in enumerate(chips):
            slot = outs[0].at[2 * cx + cy]
            _remote(slot, slot, send_sems.at[j], recv_sems.at[j], (cx, cy, c)).wait_recv()
        own, cps = copies(srcs, outs, send_sems, recv_sems)
        for cp in cps:
            cp.wait_send()
        own.wait()

    return _Exchange([full], [jax.ShapeDtypeStruct((NSHARD, SMALL_ROWS, width), full.dtype)], 4, start, finish)


def _add_sibling(mine, recv, c, name):
    _, half, width = recv.shape
    tr = _tile(half, 256, 8)
    nb = half // tr

    def body(c_ref, a_ref, b_ref, o_ref):
        del c_ref
        o_ref[...] = (a_ref[...].astype(F32) + b_ref[...].astype(F32)).astype(o_ref.dtype)

    grid_spec = pltpu.PrefetchScalarGridSpec(
        num_scalar_prefetch=1, grid=(NSHARD, nb),
        in_specs=[pl.BlockSpec((1, tr, width), lambda j, r, c_ref: (j, c_ref[0] * nb + r, 0)),
                  pl.BlockSpec((1, tr, width), lambda j, r, c_ref: (j, r, 0))],
        out_specs=pl.BlockSpec((1, tr, width), lambda j, r, c_ref: (j, r, 0)))
    return pl.pallas_call(
        body, grid_spec=grid_spec, out_shape=jax.ShapeDtypeStruct(recv.shape, recv.dtype),
        compiler_params=_cparams(("parallel", "parallel")), name=name)(c, mine, recv)


def _sum_chips(parts, c, name):
    _, half, width = parts.shape
    tr = _tile(half, 256, 8)

    def body(c_ref, p_ref, o_ref):
        del c_ref
        p = [p_ref[j].astype(F32) for j in range(NSHARD)]
        o_ref[0] = ((p[0] + p[1]) + p[2]) + p[3]

    grid_spec = pltpu.PrefetchScalarGridSpec(
        num_scalar_prefetch=1, grid=(half // tr,),
        in_specs=[pl.BlockSpec((NSHARD, tr, width), lambda r, c_ref: (0, r, 0))],
        out_specs=pl.BlockSpec((1, tr, width), lambda r, c_ref: (c_ref[0], r, 0)))
    return pl.pallas_call(
        body, grid_spec=grid_spec, out_shape=jax.ShapeDtypeStruct((2, half, width), F32),
        compiler_params=_cparams(("parallel",)), name=name)(c, parts)


def _unpack(flat, names, shapes):
    out, off = {}, 0
    for n in names:
        sz = _size(shapes[n])
        out[n] = flat[off:off + sz].reshape(shapes[n])
        off += sz
    return out


W_IN_COLS = 3080
W_IN_PAD = 3136


def _reorder_w_in_t(w):
    return jnp.concatenate([w[2048:5120], w[9248:12320], w[0:2048], w[8224:9248], w[5152:6688], w[6688:8224],
                            w[5120:5152], jnp.zeros((NP - 12320, D), w.dtype)], axis=0)


def _restore_w_in_t(g):
    return jnp.concatenate([g[6144:8192], g[0:3072], g[12288:12320], g[9216:10752], g[10752:12288], g[8192:9216],
                            g[3072:6144]], axis=0)


def _lru_group_weights(w):
    w4 = w.reshape(4, 4, 96, 96)
    eye = jnp.eye(4, dtype=w.dtype)
    return (w4[:, :, None, :, :] * eye[None, :, :, None, None]).transpose(0, 1, 3, 2, 4).reshape(4, LRU_G, LRU_G)


def _lru_group_blocks(g):
    g5 = g.reshape(4, 4, 96, 4, 96)
    return jnp.stack([g5[:, a, :, a, :] for a in range(4)], axis=1).reshape(16, 96, 96)


def _spread(a):
    return a.transpose(1, 0, 2).reshape(a.shape[1], NSHARD * a.shape[2])


def _split(a):
    return a.reshape(a.shape[0], NSHARD, a.shape[1] // NSHARD).transpose(1, 0, 2)


class _Reduction:
    def __init__(self, dist, parts, names):
        self.c, self.parts, self.names = dist.c, parts, names

    def swap(self):
        return _swap_halves(self.parts)

    def scatter(self, recv):
        return _scatter_chips([_add_sibling(p, r, self.c, "add_sibling_" + n)
                               for p, r, n in zip(self.parts, recv, self.names)])

    def share(self, landed):
        return _share_halves([_sum_chips(a, self.c, "sum_chips_" + n) for a, n in zip(landed, self.names)])

    def done(self, shared):
        return [a.reshape(2 * a.shape[1], a.shape[2]) for a in shared]


class _Dist:
    def __init__(self, w_in_shard, late_shards):
        self.c = lax.axis_index("c").astype(jnp.int32).reshape(1)
        self.w_in_shard = [w_in_shard]
        self.late_shards = late_shards

    def w_in_ride(self):
        return _gather_shards(self.w_in_shard, [True])

    def w_in_arrived(self, got):
        (g_in,) = _with_own_slot(self.w_in_shard, got)
        return _reorder_w_in_t(g_in[:, 0:W_IN_COLS].reshape(NSHARD * W_IN_COLS, D))

    def weights_ride(self):
        return _gather_shards(self.late_shards, [True, True, False])

    def weights_arrived(self, got):
        g_kv, g_rows, g_small = _with_own_slot(self.late_shards, got)
        out = {"w_kv": _spread(g_kv)}
        for n, lo_, hi_ in ROW_PIECES:
            out[n] = g_rows[:, lo_:hi_].reshape(NSHARD * (hi_ - lo_), D)
        out["ssd_conv_w"] = _spread(g_small[:, :, 0:768])
        out["ssd_norm_g"] = _spread(g_small[:, :, 768:896])
        out["lru_conv_w"] = _spread(g_small[:, :, 896:1280])
        return out

    def early_parts(self, grads):
        rows = jnp.concatenate([grads[n].reshape(NSHARD, hi_ - lo_, D) for n, lo_, hi_ in ROW_PIECES], axis=1)
        return [_split(grads["w_kv"]), rows]

    def late_parts(self, grads):
        rows = _restore_w_in_t(grads["w_in_rt"]).reshape(NSHARD, W_IN_COLS, D)
        return [jnp.pad(rows, ((0, 0), (0, W_IN_PAD - W_IN_COLS), (0, 0)))]


def _local_grads(x, mem, target, wts, dist=None):
    pad128 = lambda a: jnp.pad(a, ((0, 0), (0, 128 - a.shape[1])))

    if dist is None:
        (h,), _ = _norm_fwd(x, wts["norm_g"])
        w_in_rt = wts["w_in_rt"]
        proj = _mm(h, w_in_rt, F32, "in_proj", tb=True, tn=NP_TILE)
    else:
        (h,), arrived = _norm_fwd(x, wts["norm_g"], ride=dist.w_in_ride())
        w_in_rt = dist.w_in_arrived(arrived)
        proj, arrived = _mm(h, w_in_rt, F32, "in_proj", tb=True, tn=NP_TILE, ride=dist.weights_ride())
        wts = dict(wts, **dist.weights_arrived(arrived))
    wbs, wbl, wbm, wo, wkv = wts["w_br_ssd"], wts["w_br_lru"], wts["w_br_mem"], wts["w_out"], wts["w_kv"]
    wa, wx = _mx(_lru_group_weights(wts["lru_w_a"])), _mx(_lru_group_weights(wts["lru_w_x"]))
    ba, bx = wts["lru_b_a"].reshape(1, LRU_W), wts["lru_b_x"].reshape(1, LRU_W)
    dtb, alog = pad128(wts["ssd_dt_bias"]), pad128(wts["ssd_a_log"])
    dexp = jnp.repeat(wts["ssd_d"], 64, axis=1)
    ng = wts["ssd_norm_g"].reshape(1, SSD_W)
    xbc = _conv_fwd(proj, XBC, wts["ssd_conv_w"], wts["ssd_conv_b"], True, "ssd_conv_fwd")
    yssd, yraw, hprev = _ssd_fwd(xbc, proj, dtb, alog, dexp, ng)
    xl = _conv_fwd(proj, LX, wts["lru_conv_w"], wts["lru_conv_b"], False, "lru_conv_fwd")
    ylru, hs = _lru_fwd(xl, proj, wa, wx, ba, bx, wts["lru_lambda"])
    kk, vv, mn = _mem_kv_fwd(mem, wts["mem_norm_g"], wkv)
    ymem = _attn_fwd(proj, kk, vv)

    dproj, dx2, dx2m, merged, db0, db1, db2, loss_vec, dfg = _merge_fb(
        x, target, yssd, ylru, ymem, proj, wbs, wbl, wbm, wo, wts["final_g"].reshape(1, D))
    grads = {"final_g": dfg.reshape(D)}
    grads["w_out"] = _mm(merged, dx2m, _MXU, "dw_out", ta=True)
    grads["w_br_ssd"] = _mm(yssd, db0, _MXU, "dw_br_ssd", ta=True)
    grads["w_br_lru"] = _mm(ylru, db1, _MXU, "dw_br_lru", ta=True)
    grads["w_br_mem"] = _mm(ymem, db2, _MXU, "dw_br_mem", ta=True)
    dyssd = _mm(db0, wbs, F32, "dy_ssd", tb=True)
    dylru = _mm(db1, wbl, F32, "dy_lru", tb=True)
    dymem = _mm(db2, wbm, F32, "dy_mem", tb=True)

    dproj, dk, dv = _attn_bwd(proj, kk, vv, dymem, dproj)
    grads["w_kv"], grads["mem_norm_g"] = _mem_kv_bwd(mem, wts["mem_norm_g"], mn, wkv, dk, dv)

    early = None if dist is None else _Reduction(dist, dist.early_parts(grads), ["w_kv", "rows"])

    (dproj, dxl, dwa, dwx, dba, dbx, dlam), got = _lru_bwd(
        xl, proj, hs, dylru, dproj, wa, wx, ba, bx, wts["lru_lambda"], ride=early and early.swap())
    grads["lru_w_a"] = _lru_group_blocks(dwa)[None]
    grads["lru_w_x"] = _lru_group_blocks(dwx)[None]
    grads["lru_b_a"], grads["lru_b_x"] = dba.reshape(1, 16, 96), dbx.reshape(1, 16, 96)
    grads["lru_lambda"] = dlam
    (grads["lru_conv_w"], grads["lru_conv_b"]), _ = _conv_bwd_w(
        proj, LX, wts["lru_conv_w"], wts["lru_conv_b"], dxl, False, "lru_conv_bwd_w")
    dproj = _conv_bwd_x(dxl, wts["lru_conv_w"], dproj, LX, "lru_conv_bwd_x")

    (dproj, ddt, dxbc, dng, dda, ddd, ddtb), got = _ssd_bwd(
        xbc, proj, yraw, hprev, dyssd, dproj, dtb, alog, dexp, ng, ride=early and early.scatter(got))
    dproj = _put_block(ddt, dproj, DT, "put_ddt")
    grads["ssd_norm_g"] = dng.reshape(4, 512)
    grads["ssd_dt_bias"] = ddtb[:, 0:32]
    grads["ssd_a_log"] = (dda * -jnp.exp(alog))[:, 0:32]
    grads["ssd_d"] = ddd.reshape(32, 64).sum(axis=1)[None, :]
    (dpre, grads["ssd_conv_w"], grads["ssd_conv_b"]), got = _conv_bwd_w(
        proj, XBC, wts["ssd_conv_w"], wts["ssd_conv_b"], dxbc, True, "ssd_conv_bwd_w", ride=early and early.share(got))
    reduced = {} if dist is None else dict(zip(["w_kv", "rows"], early.done(got)))
    dproj = _conv_bwd_x(dpre, wts["ssd_conv_w"], dproj, XBC, "ssd_conv_bwd_x")

    grads["w_in_rt"] = _mm(dproj, h, _MXU, "dw_in", ta=True, tm=NP_TILE, tn=1024)
    if dist is None:
        dh = _mm(dproj, w_in_rt, F32, "dh", tn=1024, tk=NP_TILE)
        (grad_x, grads["norm_g"]), _ = _norm_bwd(x, wts["norm_g"], dh, dx2)
    else:
        late = _Reduction(dist, dist.late_parts(grads), ["w_in"])
        got = _run_exchange(late.swap(), "swap_halves_w_in")
        dh, got = _mm(dproj, w_in_rt, F32, "dh", tn=1024, tk=NP_TILE, ride=late.scatter(got))
        (grad_x, grads["norm_g"]), _ = _norm_bwd(x, wts["norm_g"], dh, dx2)
        reduced["w_in"] = late.done(_run_exchange(late.share(got), "share_halves_w_in"))[0]
    return jnp.sum(loss_vec), grad_x, grads, reduced


def kernel(x, mem, norm_g, w_in, ssd_conv_w, ssd_conv_b, ssd_dt_bias, ssd_a_log, ssd_d, ssd_norm_g, lru_conv_w, lru_conv_b, lru_w_a, lru_b_a, lru_w_x, lru_b_x, lru_lambda, mem_norm_g, w_kv, w_br_ssd, w_br_lru, w_br_mem, w_out, final_g, loss_target, m_norm_g, m_w_in, m_ssd_conv_w, m_ssd_conv_b, m_ssd_dt_bias, m_ssd_a_log, m_ssd_d, m_ssd_norm_g, m_lru_conv_w, m_lru_conv_b, m_lru_w_a, m_lru_b_a, m_lru_w_x, m_lru_b_x, m_lru_lambda, m_mem_norm_g, m_w_kv, m_w_br_ssd, m_w_br_lru, m_w_br_mem, m_w_out, m_final_g, v_norm_g, v_w_in, v_ssd_conv_w, v_ssd_conv_b, v_ssd_dt_bias, v_ssd_a_log, v_ssd_d, v_ssd_norm_g, v_lru_conv_w, v_lru_conv_b, v_lru_w_a, v_lru_b_a, v_lru_w_x, v_lru_b_x, v_lru_lambda, v_mem_norm_g, v_w_kv, v_w_br_ssd, v_w_br_lru, v_w_br_mem, v_w_out, v_final_g):
    given = dict(locals())

    rows_w = jnp.concatenate([w_br_ssd[0], w_br_lru[0], w_br_mem[0], w_out[0]], axis=0)
    small_w = jnp.concatenate([ssd_conv_w[0], ssd_norm_g[0], lru_conv_w[0]], axis=1)
    w_in_t = jnp.pad(_mx(w_in[0].T), ((0, W_IN_PAD - W_IN_COLS), (0, 0)))
    dist = _Dist(w_in_t, [_mx(w_kv[0]), _mx(rows_w), small_w])
    wts = {n: given[n] for n in REPL}
    wts["lru_w_a"], wts["lru_w_x"] = lru_w_a[0], lru_w_x[0]

    loss_part, grad_x, grads, reduced = _local_grads(x[0], mem[0], loss_target[0], wts, dist)
    loss = lax.psum(loss_part, ("x", "y", "c"))

    repl_flat = jnp.concatenate([grads[n].reshape(-1) for n in REPL])
    repl_flat = jnp.pad(repl_flat, (0, NSHARD * SMALL_Q - repl_flat.shape[0])).reshape(NSHARD, SMALL_Q)
    shard_small = jnp.concatenate([_split(grads[n]).reshape(NSHARD, -1) for n in SMALL_SHARDED], axis=1)
    p_small = jnp.concatenate(
        [repl_flat, shard_small, jnp.zeros((NSHARD, SMALL_BUF_ROWS * PACK_W - SMALL_Q - 5120), F32)], axis=1)
    small = _Reduction(dist, [p_small.reshape(NSHARD, SMALL_BUF_ROWS, PACK_W)], ["small"])
    got = _run_exchange(small.swap(), "swap_halves_small")
    got = _run_exchange(small.scatter(got), "scatter_chips_small")
    got = _run_exchange(small.share(got), "share_halves_small")
    r_small = small.done(got)[0]
    repl_all = _run_exchange(_gather_small(r_small), "gather_small")[0].reshape(-1)

    g_shard = {"w_kv": reduced["w_kv"]}
    for n, lo_, hi_ in ROW_PIECES:
        g_shard[n] = reduced["rows"][lo_:hi_]
    g_shard.update(_unpack(r_small.reshape(-1)[SMALL_Q:], SMALL_SHARDED, SHARD_SHAPE))
    g_repl = _unpack(repl_all, REPL, REPL_SHAPE)

    out_g, out_d, out_m, out_v = {}, {}, {}, {}
    for n in WEIGHTS:
        w_full = given[n]
        if n == "w_in":
            g2 = reduced["w_in"][0:W_IN_COLS]
            d, mo, vo = _adamw(w_in[0].T, g2, m_w_in[0].T, v_w_in[0].T, "adamw_w_in")
            out_g[n], out_d[n], out_m[n], out_v[n] = [a.T[None] for a in (g2, d, mo, vo)]
            continue
        g = (g_shard[n] if n in SHARDED else g_repl[n]).reshape(w_full.shape)
        cols = w_full.shape[-1]
        as2d = lambda a: a.reshape(-1, cols)
        d, mo, vo = _adamw(as2d(w_full), as2d(g), as2d(given["m_" + n]), as2d(given["v_" + n]), "adamw_" + n)
        out_g[n] = g
        out_d[n], out_m[n], out_v[n] = d.reshape(w_full.shape), mo.reshape(w_full.shape), vo.reshape(w_full.shape)

    return (loss, grad_x[None], *[out_g[n] for n in WEIGHTS], *[out_d[n] for n in WEIGHTS],
            *[out_m[n] for n in WEIGHTS], *[out_v[n] for n in WEIGHTS])
```

```python
import jax
import jax.numpy as jnp
from jax import lax
from jax.experimental import pallas as pl
from jax.experimental.pallas import tpu as pltpu

F32 = jnp.float32
_MXU = jnp.bfloat16
_HI = lax.Precision.HIGHEST
MESH = pl.DeviceIdType.MESH

D = 1024
EPS = 1e-6
MEM_HEADS = 4
MEM_HD = 256
LRU_C = 8.0
SSD_L = 128
SSD_W = 2048
LRU_W = 1536
NSHARD = 4

XBC = (0, 3072)
GL = (3072, 3072)
Z = (6144, 2048)
Q = (8192, 1024)
LG = (9216, 1536)
LX = (10752, 1536)
DT = (12288, 256)
NP = 12544
NP_TILE = 1792

ADAM_LR = 0.001
ADAM_B1 = 0.9
ADAM_B2 = 0.999
ADAM_EPS = 1e-08
ADAM_WD = 0.01
ADAM_STEP = 10

VMEM_LIMIT = 56 * 1024 * 1024

SHARDED = ("w_in", "ssd_conv_w", "ssd_norm_g", "lru_conv_w", "w_kv", "w_br_ssd", "w_br_lru", "w_br_mem", "w_out")
SHARD_SHAPE = {"w_in": (1024, 3080), "ssd_conv_w": (4, 768), "ssd_norm_g": (4, 128), "lru_conv_w": (4, 384),
               "w_kv": (1024, 512), "w_br_ssd": (512, 1024), "w_br_lru": (384, 1024), "w_br_mem": (256, 1024),
               "w_out": (256, 1024)}
REPL = ("norm_g", "ssd_conv_b", "ssd_dt_bias", "ssd_a_log", "ssd_d", "lru_conv_b", "lru_w_a", "lru_b_a",
        "lru_w_x", "lru_b_x", "lru_lambda", "mem_norm_g", "final_g")
REPL_SHAPE = {"norm_g": (1, 1024), "ssd_conv_b": (1, 3072), "ssd_dt_bias": (1, 32), "ssd_a_log": (1, 32),
              "ssd_d": (1, 32), "lru_conv_b": (1, 1536), "lru_w_a": (1, 16, 96, 96), "lru_b_a": (1, 16, 96),
              "lru_w_x": (1, 16, 96, 96), "lru_b_x": (1, 16, 96), "lru_lambda": (1, 1536),
              "mem_norm_g": (1, 1024), "final_g": (1024,)}
WEIGHTS = ("norm_g", "w_in", "ssd_conv_w", "ssd_conv_b", "ssd_dt_bias", "ssd_a_log", "ssd_d", "ssd_norm_g",
           "lru_conv_w", "lru_conv_b", "lru_w_a", "lru_b_a", "lru_w_x", "lru_b_x", "lru_lambda", "mem_norm_g",
           "w_kv", "w_br_ssd", "w_br_lru", "w_br_mem", "w_out", "final_g")

ROW_PIECES = (("w_br_ssd", 0, 512), ("w_br_lru", 512, 896), ("w_br_mem", 896, 1152), ("w_out", 1152, 1408))
SMALL_SHARDED = ("ssd_conv_w", "ssd_norm_g", "lru_conv_w")
PACK_W = 512
SMALL_ROWS = 152
SMALL_Q = SMALL_ROWS * PACK_W
SMALL_BUF_ROWS = 176


def _size(shape):
    n = 1
    for s in shape:
        n *= s
    return n


def _sigmoid(x):
    return 0.5 * jnp.tanh(0.5 * x) + 0.5


def _silu(x):
    return x * _sigmoid(x)


def _dsilu(x):
    s = _sigmoid(x)
    return s * (1.0 + x * (1.0 - s))


def _softplus(x):
    return jnp.maximum(x, 0.0) + jnp.log(1.0 + jnp.exp(-jnp.abs(x)))


def _one_minus_sq(log_a, a):
    x = 2.0 * log_a
    series = -x * (1.0 + x * (0.5 + x * (1.0 / 6.0 + x * (1.0 / 24.0))))
    return jnp.where(x > -0.03, series, 1.0 - a * a)


def _dot(a, b, precision=None):
    return jnp.dot(a, b, preferred_element_type=F32, precision=precision)


def _dot_nt(a, b):
    return lax.dot_general(a, b, (((1,), (1,)), ((), ())), preferred_element_type=F32)


def _dot_tn(a, b):
    return lax.dot_general(a, b, (((0,), (0,)), ((), ())), preferred_element_type=F32)


def _mx(a):
    return a.astype(_MXU)


def _cparams(sem):
    return pltpu.CompilerParams(dimension_semantics=sem, vmem_limit_bytes=VMEM_LIMIT)


def _tile(n, want, mult=128):
    if n <= want:
        return n
    for t in range(want - want % mult, 0, -mult):
        if n % t == 0:
            return t
    raise ValueError((n, want, mult))


def _mm(a, b, out_dtype, name, ta=False, tb=False, tm=1024, tn=1280, tk=1024, ride=None):
    k, m = a.shape if ta else a.shape[::-1]
    k2, n = b.shape[::-1] if tb else b.shape
    assert k == k2
    tm, tn, tk = _tile(m, tm), _tile(n, tn), _tile(k, tk)
    nk = k // tk
    contract = (((0 if ta else 1,), (1 if tb else 0,)), ((), ()))

    def body(a_ref, b_ref, o_ref, acc_ref):
        kk = pl.program_id(2)

        @pl.when(kk == 0)
        def _():
            acc_ref[...] = jnp.zeros_like(acc_ref)

        acc_ref[...] += lax.dot_general(a_ref[...], b_ref[...], contract, preferred_element_type=F32)

        @pl.when(kk == nk - 1)
        def _():
            o_ref[...] = acc_ref[...].astype(o_ref.dtype)

    a_spec = pl.BlockSpec((tk, tm), lambda i, j, kk: (kk, i)) if ta else pl.BlockSpec((tm, tk), lambda i, j, kk: (i, kk))
    b_spec = pl.BlockSpec((tn, tk), lambda i, j, kk: (j, kk)) if tb else pl.BlockSpec((tk, tn), lambda i, j, kk: (kk, j))
    outs, carried = _pcall(
        body, ride, (a, b), grid=(m // tm, n // tn, nk),
        in_specs=[a_spec, b_spec],
        out_specs=[pl.BlockSpec((tm, tn), lambda i, j, kk: (i, j))],
        out_shape=[jax.ShapeDtypeStruct((m, n), out_dtype)],
        scratch_shapes=[pltpu.VMEM((tm, tn), F32)],
        sem=("parallel", "parallel", "arbitrary"), name=name)
    return outs[0] if ride is None else (outs[0], carried)


def _norm_fwd(x, g, ride=None):
    s = x.shape[0]
    ts = _tile(s, 512)

    def body(x_ref, g_ref, h_ref):
        xv = x_ref[...]
        r = lax.rsqrt(jnp.mean(xv * xv, axis=-1, keepdims=True) + EPS)
        h_ref[...] = (xv * r * g_ref[...]).astype(h_ref.dtype)

    return _pcall(
        body, ride, (x, g), grid=(s // ts,),
        in_specs=[pl.BlockSpec((ts, D), lambda i: (i, 0)), pl.BlockSpec((1, D), lambda i: (0, 0))],
        out_specs=[pl.BlockSpec((ts, D), lambda i: (i, 0))],
        out_shape=[jax.ShapeDtypeStruct((s, D), _MXU)], scratch_shapes=[], sem=("parallel",), name="norm_fwd")


def _norm_bwd(x, g, dh, dx2, ride=None):
    s = x.shape[0]
    ts = _tile(s, 512)

    def body(x_ref, g_ref, dh_ref, dx2_ref, gx_ref, dg_ref):
        @pl.when(pl.program_id(0) == 0)
        def _():
            dg_ref[...] = jnp.zeros_like(dg_ref)

        xv = x_ref[...]
        r = lax.rsqrt(jnp.mean(xv * xv, axis=-1, keepdims=True) + EPS)
        xhat = xv * r
        dh_v = dh_ref[...]
        dg_ref[...] += jnp.sum(dh_v * xhat, axis=0, keepdims=True)
        dxh = dh_v * g_ref[...]
        gx_ref[...] = dx2_ref[...] + r * (dxh - xhat * jnp.mean(dxh * xhat, axis=-1, keepdims=True))

    row = pl.BlockSpec((ts, D), lambda i: (i, 0))
    vec = pl.BlockSpec((1, D), lambda i: (0, 0))
    return _pcall(
        body, ride, (x, g, dh, dx2), grid=(s // ts,), in_specs=[row, vec, row, row], out_specs=[row, vec],
        out_shape=[jax.ShapeDtypeStruct((s, D), F32), jax.ShapeDtypeStruct((1, D), F32)],
        scratch_shapes=[], sem=("arbitrary",), name="norm_bwd")


CONV_RB = 16
CONV_LC = 256


def _fold8(v):
    acc = v[0:8]
    for r0 in range(8, v.shape[0], 8):
        acc = acc + v[r0:r0 + 8]
    return acc


def _conv_fwd(src, blk, w, b, act, name):
    s = src.shape[0]
    off, width = blk
    cb = off // width
    ts = _tile(s, 256)

    def body(x_ref, w_ref, b_ref, o_ref, ext_ref):
        @pl.when(pl.program_id(0) == 0)
        def _():
            ext_ref[0:8, :] = jnp.zeros((8, width), F32)

        ext_ref[8:8 + ts, :] = x_ref[...]
        for l0 in range(0, width, CONV_LC):
            ls = slice(l0, l0 + CONV_LC)
            taps = [w_ref[k:k + 1, ls] for k in range(4)]
            bias = b_ref[:, ls]
            for r0 in range(0, ts, CONV_RB):
                pre = bias
                for k in range(4):
                    pre = pre + taps[k] * ext_ref[5 + k + r0:5 + k + r0 + CONV_RB, ls]
                o_ref[r0:r0 + CONV_RB, ls] = _silu(pre) if act else pre
        ext_ref[0:8, :] = x_ref[ts - 8:ts, :]

    return pl.pallas_call(
        body, grid=(s // ts,),
        in_specs=[pl.BlockSpec((ts, width), lambda i: (i, cb)), pl.BlockSpec((4, width), lambda i: (0, 0)),
                  pl.BlockSpec((1, width), lambda i: (0, 0))],
        out_specs=pl.BlockSpec((ts, width), lambda i: (i, 0)),
        out_shape=jax.ShapeDtypeStruct((s, width), F32),
        scratch_shapes=[pltpu.VMEM((ts + 8, width), F32)],
        compiler_params=_cparams(("arbitrary",)), name=name)(src, w, b)


def _conv_bwd_w(src, blk, w, b, dout, act, name, ride=None):
    s = src.shape[0]
    off, width = blk
    cb = off // width
    ts = _tile(s, 256)

    def body(x_ref, w_ref, b_ref, do_ref, *rest):
        if act:
            dpre_ref, dw_ref, db_ref, ext_ref = rest
        else:
            dw_ref, db_ref, ext_ref = rest

        @pl.when(pl.program_id(0) == 0)
        def _():
            ext_ref[0:8, :] = jnp.zeros((8, width), F32)
            dw_ref[...] = jnp.zeros_like(dw_ref)
            db_ref[...] = jnp.zeros_like(db_ref)

        ext_ref[8:8 + ts, :] = x_ref[...]
        for l0 in range(0, width, CONV_LC):
            ls = slice(l0, l0 + CONV_LC)
            taps = [w_ref[k:k + 1, ls] for k in range(4)]
            bias = b_ref[:, ls]
            acc_b = jnp.zeros((8, CONV_LC), F32)
            acc_w = [jnp.zeros((8, CONV_LC), F32) for _ in range(4)]
            for r0 in range(0, ts, CONV_RB):
                xs = [ext_ref[5 + k + r0:5 + k + r0 + CONV_RB, ls] for k in range(4)]
                dpre = do_ref[r0:r0 + CONV_RB, ls]
                if act:
                    pre = bias
                    for k in range(4):
                        pre = pre + taps[k] * xs[k]
                    dpre = dpre * _dsilu(pre)
                    dpre_ref[r0:r0 + CONV_RB, ls] = dpre
                acc_b = acc_b + _fold8(dpre)
                for k in range(4):
                    acc_w[k] = acc_w[k] + _fold8(dpre * xs[k])
            db_ref[:, ls] += jnp.sum(acc_b, axis=0, keepdims=True)
            for k in range(4):
                dw_ref[k:k + 1, ls] += jnp.sum(acc_w[k], axis=0, keepdims=True)
        ext_ref[0:8, :] = x_ref[ts - 8:ts, :]

    row = pl.BlockSpec((ts, width), lambda i: (i, 0))
    outs = [pl.BlockSpec((4, width), lambda i: (0, 0)), pl.BlockSpec((1, width), lambda i: (0, 0))]
    shapes = [jax.ShapeDtypeStruct((4, width), F32), jax.ShapeDtypeStruct((1, width), F32)]
    if act:
        outs = [row] + outs
        shapes = [jax.ShapeDtypeStruct((s, width), F32)] + shapes
    return _pcall(
        body, ride, (src, w, b, dout), grid=(s // ts,),
        in_specs=[pl.BlockSpec((ts, width), lambda i: (i, cb)), pl.BlockSpec((4, width), lambda i: (0, 0)),
                  pl.BlockSpec((1, width), lambda i: (0, 0)), row],
        out_specs=outs, out_shape=shapes,
        scratch_shapes=[pltpu.VMEM((ts + 8, width), F32)], sem=("arbitrary",), name=name)


def _conv_bwd_x(dpre, w, dproj, blk, name):
    s = dpre.shape[0]
    off, width = blk
    cb = off // width
    ts = _tile(s, 256)
    nt = s // ts

    def body(dp_ref, w_ref, dproj_hbm, o_ref, ext_ref):
        del dproj_hbm

        @pl.when(pl.program_id(0) == 0)
        def _():
            ext_ref[ts:ts + 8, :] = jnp.zeros((8, width), F32)

        ext_ref[0:ts, :] = dp_ref[...]
        for l0 in range(0, width, CONV_LC):
            ls = slice(l0, l0 + CONV_LC)
            taps = [w_ref[k:k + 1, ls] for k in range(4)]
            for r0 in range(0, ts, CONV_RB):
                acc = taps[0] * ext_ref[3 + r0:3 + r0 + CONV_RB, ls]
                for k in range(1, 4):
                    acc = acc + taps[k] * ext_ref[3 - k + r0:3 - k + r0 + CONV_RB, ls]
                o_ref[r0:r0 + CONV_RB, ls] = acc.astype(o_ref.dtype)
        ext_ref[ts:ts + 8, :] = dp_ref[0:8, :]

    return pl.pallas_call(
        body, grid=(nt,),
        in_specs=[pl.BlockSpec((ts, width), lambda i: (nt - 1 - i, 0)), pl.BlockSpec((4, width), lambda i: (0, 0)),
                  pl.BlockSpec(memory_space=pl.ANY)],
        out_specs=pl.BlockSpec((ts, width), lambda i: (nt - 1 - i, cb)),
        out_shape=jax.ShapeDtypeStruct(dproj.shape, dproj.dtype),
        scratch_shapes=[pltpu.VMEM((ts + 8, width), F32)],
        input_output_aliases={2: 0},
        compiler_params=_cparams(("arbitrary",)), name=name)(dpre, w, dproj)


def _ssd_decay(a_cs, acst_ref, h, causal, lane_l):
    col = jnp.sum(jnp.where(lane_l == h, a_cs, 0.0), axis=1, keepdims=True)
    row = acst_ref[h:h + 1, :]
    return jnp.where(causal, jnp.exp(jnp.minimum(col - row, 0.0)), 0.0)


def _split3(x):
    hi = x.astype(jnp.bfloat16)
    rest = x - hi.astype(F32)
    mid = rest.astype(jnp.bfloat16)
    return jnp.concatenate([hi, mid, (rest - mid.astype(F32)).astype(jnp.bfloat16)], axis=1)


def _spread_matrix():
    col = jnp.arange(128, dtype=jnp.int32)[:, None]
    e64 = (col == jnp.arange(SSD_W, dtype=jnp.int32)[None, :] // 64).astype(jnp.bfloat16)
    return jnp.tile(e64, (3, 1))


def _ssd_common(dt_ref, dtb_ref, alog_ref, e64_ref, acst_ref, dtx_ref, acx_ref):
    ll = SSD_L
    dt = _softplus(dt_ref[:, 0:128] + dtb_ref[...])
    a_neg = -jnp.exp(alog_ref[...])
    ri = lax.broadcasted_iota(jnp.int32, (ll, ll), 0)
    ci = lax.broadcasted_iota(jnp.int32, (ll, ll), 1)
    causal = ri >= ci
    a_cs = _dot(causal.astype(F32), dt * a_neg, _HI)
    acst_ref[...] = a_cs.T
    both = _dot(jnp.concatenate([_split3(dt), _split3(a_cs)], axis=0), e64_ref[...])
    dtx_ref[...] = both[0:ll]
    acx_ref[...] = both[ll:2 * ll]
    lane_l = lax.broadcasted_iota(jnp.int32, (ll, 128), 1)
    return dt, a_neg, a_cs, causal, ri, lane_l, lane_l < 64


def _ssd_fwd(xbc, proj, dtb, alog, dexp, ng):
    s = xbc.shape[0]
    ll = SSD_L
    nc = s // ll
    e64 = _spread_matrix()

    def body(xbc_ref, dt_ref, z_ref, dtb_ref, alog_ref, dexp_ref, ng_ref, e64_ref,
             yssd_ref, yraw_ref, hprev_ref, ht_ref, acst_ref, dtx_ref, acx_ref):
        @pl.when(pl.program_id(0) == 0)
        def _():
            ht_ref[...] = jnp.zeros_like(ht_ref)

        hprev_ref[0] = ht_ref[...]
        _, _, a_cs, causal, _, lane_l, lo = _ssd_common(dt_ref, dtb_ref, alog_ref, e64_ref, acst_ref, dtx_ref, acx_ref)
        for g in range(4):
            bg = _mx(xbc_ref[:, 2048 + 128 * g:2176 + 128 * g])
            cg = _mx(xbc_ref[:, 2560 + 128 * g:2688 + 128 * g])
            cbm = _dot_nt(cg, bg)
            for jj in range(4):
                j = 4 * g + jj
                sl = slice(128 * j, 128 * j + 128)
                xp = xbc_ref[:, sl]
                acx = acx_ref[:, sl]
                a_last = acx_ref[ll - 1:ll, sl]
                xdt = xp * dtx_ref[:, sl]
                acc = None
                for hh in range(2):
                    dec = _ssd_decay(a_cs, acst_ref, 2 * j + hh, causal, lane_l)
                    xm = jnp.where(lo if hh == 0 else jnp.logical_not(lo), xdt, 0.0)
                    t = _dot(_mx(dec * cbm), _mx(xm))
                    acc = t if acc is None else acc + t
                ht = ht_ref[j]
                y = acc + _dot(cg, _mx(ht)) * jnp.exp(acx) + xp * dexp_ref[:, sl]
                yraw_ref[:, sl] = y
                st = _dot_tn(bg, _mx(xdt * jnp.exp(a_last - acx)))
                ht_ref[j] = ht * jnp.exp(a_last) + st
        for g in range(4):
            sl = slice(512 * g, 512 * g + 512)
            yg = yraw_ref[:, sl] * _silu(z_ref[:, sl])
            r = lax.rsqrt(jnp.mean(yg * yg, axis=-1, keepdims=True) + EPS)
            yssd_ref[:, sl] = (yg * r * ng_ref[:, sl]).astype(yssd_ref.dtype)

    vec = lambda w: pl.BlockSpec((1, w), lambda c: (0, 0))
    return pl.pallas_call(
        body, grid=(nc,),
        in_specs=[pl.BlockSpec((ll, 3072), lambda c: (c, 0)),
                  pl.BlockSpec((ll, DT[1]), lambda c: (c, DT[0] // DT[1])),
                  pl.BlockSpec((ll, Z[1]), lambda c: (c, Z[0] // Z[1])),
                  vec(128), vec(128), vec(2048), vec(2048),
                  pl.BlockSpec(e64.shape, lambda c: (0, 0))],
        out_specs=[pl.BlockSpec((ll, 2048), lambda c: (c, 0)), pl.BlockSpec((ll, 2048), lambda c: (c, 0)),
                   pl.BlockSpec((1, 16, 128, 128), lambda c: (c, 0, 0, 0))],
        out_shape=[jax.ShapeDtypeStruct((s, 2048), _MXU), jax.ShapeDtypeStruct((s, 2048), F32),
                   jax.ShapeDtypeStruct((nc, 16, 128, 128), F32)],
        scratch_shapes=[pltpu.VMEM((16, 128, 128), F32), pltpu.VMEM((128, ll), F32),
                        pltpu.VMEM((ll, 2048), F32), pltpu.VMEM((ll, 2048), F32)],
        compiler_params=_cparams(("arbitrary",)), name="ssd_fwd")(xbc, proj, proj, dtb, alog, dexp, ng, e64)


def _ssd_bwd(xbc, proj, yraw, hprev, dyssd, dproj, dtb, alog, dexp, ng, ride=None):
    s = xbc.shape[0]
    ll = SSD_L
    nc = s // ll
    e64 = _spread_matrix()

    def body(xbc_ref, dt_ref, z_ref, yraw_ref, hprev_ref, dy_ref, dproj_hbm, dtb_ref, alog_ref, dexp_ref, ng_ref,
             e64_ref,
             dz_ref, ddt_ref, dxbc_ref, dng_ref, dda_ref, ddd_ref, ddtb_ref,
             dht_ref, acst_ref, dtx_ref, acx_ref, dyr_ref, rowt_ref):
        del dproj_hbm

        @pl.when(pl.program_id(0) == 0)
        def _():
            dht_ref[...] = jnp.zeros_like(dht_ref)
            dng_ref[...] = jnp.zeros_like(dng_ref)
            dda_ref[...] = jnp.zeros_like(dda_ref)
            ddd_ref[...] = jnp.zeros_like(ddd_ref)
            ddtb_ref[...] = jnp.zeros_like(ddtb_ref)
            rowt_ref[...] = jnp.zeros_like(rowt_ref)

        for g in range(4):
            sl = slice(512 * g, 512 * g + 512)
            zz = z_ref[:, sl]
            yr = yraw_ref[:, sl]
            sz = _silu(zz)
            yg = yr * sz
            r = lax.rsqrt(jnp.mean(yg * yg, axis=-1, keepdims=True) + EPS)
            yhat = yg * r
            dyv = dy_ref[:, sl]
            dng_ref[:, sl] += jnp.sum(dyv * yhat, axis=0, keepdims=True)
            dyh = dyv * ng_ref[:, sl]
            dyg = r * (dyh - yhat * jnp.mean(dyh * yhat, axis=-1, keepdims=True))
            dz_ref[:, sl] = (dyg * yr * _dsilu(zz)).astype(dz_ref.dtype)
            dyr_ref[:, sl] = dyg * sz

        dt, a_neg, a_cs, causal, ri, lane_l, lo = _ssd_common(dt_ref, dtb_ref, alog_ref, e64_ref,
                                                              acst_ref, dtx_ref, acx_ref)
        lane_1 = lax.broadcasted_iota(jnp.int32, (1, 128), 1)
        da_col = jnp.zeros((ll, 128), F32)
        ddt_x = jnp.zeros((ll, 128), F32)
        last = jnp.zeros((1, 128), F32)
        for g in range(4):
            bg = _mx(xbc_ref[:, 2048 + 128 * g:2176 + 128 * g])
            cg = _mx(xbc_ref[:, 2560 + 128 * g:2688 + 128 * g])
            cbm = _dot_nt(cg, bg)
            dcb = jnp.zeros((ll, ll), F32)
            db_g = jnp.zeros((ll, 128), F32)
            dc_g = jnp.zeros((ll, 128), F32)
            for jj in range(4):
                j = 4 * g + jj
                sl = slice(128 * j, 128 * j + 128)
                xp = xbc_ref[:, sl]
                dtx = dtx_ref[:, sl]
                acx = acx_ref[:, sl]
                a_last = acx_ref[ll - 1:ll, sl]
                ea = jnp.exp(acx)
                dte = jnp.exp(a_last - acx)
                cd = jnp.exp(a_last)
                xdt = xp * dtx
                xdt_m = _mx(xdt)
                dy = dyr_ref[:, sl]
                ht = hprev_ref[0, j]
                dhn = dht_ref[j]
                dhn_m = _mx(dhn)
                gmat = _dot(bg, dhn_m)
                dxdt = gmat * dte
                for hh in range(2):
                    h = 2 * j + hh
                    dec = _ssd_decay(a_cs, acst_ref, h, causal, lane_l)
                    mm = dec * cbm
                    dym = _mx(jnp.where(lo if hh == 0 else jnp.logical_not(lo), dy, 0.0))
                    dxdt = dxdt + _dot_tn(_mx(mm), dym)
                    dm = _dot_nt(dym, xdt_m)
                    dcb = dcb + dm * dec
                    qq = dm * mm
                    da_col = da_col + jnp.where(lane_l == h, jnp.sum(qq, axis=1, keepdims=True), 0.0)
                    rowt_ref[h:h + 1, :] = jnp.sum(qq, axis=0, keepdims=True)
                ch = _dot(cg, _mx(ht))
                dyea = dy * ea
                dyea_m = _mx(dyea)
                xw_m = _mx(xdt * dte)
                dc_g = dc_g + _dot_nt(dyea_m, _mx(ht))
                db_g = db_g + _dot_nt(xw_m, dhn_m)
                wl = xdt * gmat * dte
                lane_a = dyea * ch - wl
                lane_b = dxdt * xp
                lane_c = jnp.sum(dhn * ht, axis=0, keepdims=True) * cd + jnp.sum(wl, axis=0, keepdims=True)
                for hh in range(2):
                    h = 2 * j + hh
                    mine = lo if hh == 0 else jnp.logical_not(lo)
                    da_col = da_col + jnp.where(
                        lane_l == h, jnp.sum(jnp.where(mine, lane_a, 0.0), axis=1, keepdims=True), 0.0)
                    ddt_x = ddt_x + jnp.where(
                        lane_l == h, jnp.sum(jnp.where(mine, lane_b, 0.0), axis=1, keepdims=True), 0.0)
                    mine_1 = (lane_1 < 64) if hh == 0 else (lane_1 >= 64)
                    last = last + jnp.where(
                        lane_1 == h, jnp.sum(jnp.where(mine_1, lane_c, 0.0), axis=1, keepdims=True), 0.0)
                dht_ref[j] = dhn * cd + _dot_tn(cg, dyea_m)
                dxbc_ref[:, sl] = dxdt * dtx + dy * dexp_ref[:, sl]
                ddd_ref[:, sl] += jnp.sum(dy * xp, axis=0, keepdims=True)
            dcb_m = _mx(dcb)
            dxbc_ref[:, 2048 + 128 * g:2176 + 128 * g] = db_g + _dot_tn(dcb_m, cg)
            dxbc_ref[:, 2560 + 128 * g:2688 + 128 * g] = dc_g + _dot(dcb_m, bg)

        da_cs = da_col - rowt_ref[...].T
        da_cs = da_cs + jnp.where(lax.broadcasted_iota(jnp.int32, (ll, 128), 0) == ll - 1, last, 0.0)
        d_dta = _dot((ri <= lax.broadcasted_iota(jnp.int32, (ll, ll), 1)).astype(F32), da_cs, _HI)
        ddt = d_dta * a_neg + ddt_x
        dda_ref[...] += jnp.sum(d_dta * dt, axis=0, keepdims=True)
        ddt_raw = ddt * _sigmoid(dt_ref[:, 0:128] + dtb_ref[...])
        ddtb_ref[...] += jnp.sum(ddt_raw, axis=0, keepdims=True)
        ddt_ref[:, 0:128] = ddt_raw.astype(ddt_ref.dtype)
        ddt_ref[:, 128:DT[1]] = jnp.zeros((ll, DT[1] - 128), ddt_ref.dtype)

    rev = lambda c: nc - 1 - c
    vec = lambda w: pl.BlockSpec((1, w), lambda c: (0, 0))
    row = lambda w: pl.BlockSpec((ll, w), lambda c: (rev(c), 0))
    return _pcall(
        body, ride, (xbc, proj, proj, yraw, hprev, dyssd, dproj, dtb, alog, dexp, ng, e64), grid=(nc,),
        in_specs=[row(3072),
                  pl.BlockSpec((ll, DT[1]), lambda c: (rev(c), DT[0] // DT[1])),
                  pl.BlockSpec((ll, Z[1]), lambda c: (rev(c), Z[0] // Z[1])),
                  row(2048),
                  pl.BlockSpec((1, 16, 128, 128), lambda c: (rev(c), 0, 0, 0)),
                  row(2048),
                  pl.BlockSpec(memory_space=pl.ANY),
                  vec(128), vec(128), vec(2048), vec(2048),
                  pl.BlockSpec(e64.shape, lambda c: (0, 0))],
        out_specs=[pl.BlockSpec((ll, Z[1]), lambda c: (rev(c), Z[0] // Z[1])),
                   row(DT[1]),
                   row(3072), vec(2048), vec(128), vec(2048), vec(128)],
        out_shape=[jax.ShapeDtypeStruct(dproj.shape, dproj.dtype), jax.ShapeDtypeStruct((s, DT[1]), dproj.dtype),
                   jax.ShapeDtypeStruct((s, 3072), F32), jax.ShapeDtypeStruct((1, 2048), F32),
                   jax.ShapeDtypeStruct((1, 128), F32), jax.ShapeDtypeStruct((1, 2048), F32),
                   jax.ShapeDtypeStruct((1, 128), F32)],
        scratch_shapes=[pltpu.VMEM((16, 128, 128), F32), pltpu.VMEM((128, ll), F32),
                        pltpu.VMEM((ll, 2048), F32), pltpu.VMEM((ll, 2048), F32), pltpu.VMEM((ll, 2048), F32),
                        pltpu.VMEM((128, ll), F32)],
        aliases={6: 0}, sem=("arbitrary",), name="ssd_bwd")


def _put_block(src, dproj, blk, name):
    s = src.shape[0]
    off, width = blk
    cb = off // width
    ts = _tile(s, 1024)

    def body(s_ref, dproj_hbm, o_ref):
        del dproj_hbm
        o_ref[...] = s_ref[...]

    return pl.pallas_call(
        body, grid=(s // ts,),
        in_specs=[pl.BlockSpec((ts, width), lambda i: (i, 0)), pl.BlockSpec(memory_space=pl.ANY)],
        out_specs=pl.BlockSpec((ts, width), lambda i: (i, cb)),
        out_shape=jax.ShapeDtypeStruct(dproj.shape, dproj.dtype),
        input_output_aliases={1: 0},
        compiler_params=_cparams(("parallel",)), name=name)(src, dproj)


LRU_G = 384


def _lru_gates(xl_ref, wa_ref, wx_ref, ba_ref, bx_ref, lam_ref, g):
    sl = slice(LRU_G * g, LRU_G * g + LRU_G)
    xg = xl_ref[:, sl]
    xm = _mx(xg)
    pa = _dot(xm, wa_ref[g]) + ba_ref[:, sl]
    r = jnp.where(pa < -12.0, jnp.exp(pa), _sigmoid(pa))
    ig = _sigmoid(_dot(xm, wx_ref[g]) + bx_ref[:, sl])
    sp = _softplus(-lam_ref[:, sl])
    log_a = (-LRU_C * r) * sp
    a = jnp.exp(log_a)
    mult = jnp.sqrt(_one_minus_sq(log_a, a))
    return sl, xg, r, ig, sp, a, mult


def _lru_fwd(xl, proj, wa, wx, ba, bx, lam):
    s = xl.shape[0]
    ts = _tile(s, 256)
    w = LRU_W

    def body(xl_ref, lg_ref, wa_ref, wx_ref, ba_ref, bx_ref, lam_ref, y_ref, hs_ref, a_ref, u_ref, carry_ref):
        @pl.when(pl.program_id(0) == 0)
        def _():
            carry_ref[...] = jnp.zeros_like(carry_ref)

        for g in range(4):
            sl, xg, _, ig, _, a, mult = _lru_gates(xl_ref, wa_ref, wx_ref, ba_ref, bx_ref, lam_ref, g)
            a_ref[:, sl] = a
            u_ref[:, sl] = mult * (ig * xg)

        def step(t, h):
            h = a_ref[pl.ds(t, 1), :] * h + u_ref[pl.ds(t, 1), :]
            hs_ref[pl.ds(t, 1), :] = h
            return h

        carry_ref[0:1, :] = lax.fori_loop(0, ts, step, carry_ref[0:1, :], unroll=8)
        y_ref[...] = (hs_ref[...] * _silu(lg_ref[...])).astype(y_ref.dtype)

    row = pl.BlockSpec((ts, w), lambda i: (i, 0))
    vec = pl.BlockSpec((1, w), lambda i: (0, 0))
    wsp = pl.BlockSpec((4, LRU_G, LRU_G), lambda i: (0, 0, 0))
    return pl.pallas_call(
        body, grid=(s // ts,),
        in_specs=[row, pl.BlockSpec((ts, w), lambda i: (i, LG[0] // w)), wsp, wsp, vec, vec, vec],
        out_specs=[row, row],
        out_shape=[jax.ShapeDtypeStruct((s, w), _MXU), jax.ShapeDtypeStruct((s, w), F32)],
        scratch_shapes=[pltpu.VMEM((ts, w), F32), pltpu.VMEM((ts, w), F32), pltpu.VMEM((8, w), F32)],
        compiler_params=_cparams(("arbitrary",)), name="lru_fwd")(xl, proj, wa, wx, ba, bx, lam)


def _lru_bwd(xl, proj, hs, dy, dproj, wa, wx, ba, bx, lam, ride=None):
    s = xl.shape[0]
    ts = _tile(s, 256)
    nt = s // ts
    w = LRU_W
    hb = ts // 8

    def body(xl_ref, lg_ref, hs_ref, hprev_ref, dy_ref, dproj_hbm, wa_ref, wx_ref, ba_ref, bx_ref, lam_ref,
             dlg_ref, dxl_ref, dwa_ref, dwx_ref, dba_ref, dbx_ref, dlam_ref,
             a_ref, dh_ref, ext_ref, carry_ref, r_ref, ig_ref, mult_ref):
        del dproj_hbm
        i = pl.program_id(0)

        @pl.when(i == 0)
        def _():
            carry_ref[...] = jnp.zeros_like(carry_ref)
            for ref in (dwa_ref, dwx_ref, dba_ref, dbx_ref, dlam_ref):
                ref[...] = jnp.zeros_like(ref)

        lg = lg_ref[...]
        dyv = dy_ref[...]
        dh_ref[...] = dyv * _silu(lg)
        dlg_ref[...] = (dyv * hs_ref[...] * _dsilu(lg)).astype(dlg_ref.dtype)
        for g in range(4):
            sl, _, r, ig, _, a, mult = _lru_gates(xl_ref, wa_ref, wx_ref, ba_ref, bx_ref, lam_ref, g)
            a_ref[:, sl] = a
            r_ref[:, sl] = r
            ig_ref[:, sl] = ig
            mult_ref[:, sl] = mult

        def step(k, carry):
            t = ts - 1 - k
            dh = dh_ref[pl.ds(t, 1), :] + carry
            dh_ref[pl.ds(t, 1), :] = dh
            return a_ref[pl.ds(t, 1), :] * dh

        carry_ref[0:1, :] = lax.fori_loop(0, ts, step, carry_ref[0:1, :], unroll=8)

        ext_ref[0:8, :] = jnp.where(i == nt - 1, 0.0, 1.0) * hprev_ref[...]
        ext_ref[8:8 + ts, :] = hs_ref[...]
        for g in range(4):
            sl = slice(LRU_G * g, LRU_G * g + LRU_G)
            xg, r, ig, a, mult = xl_ref[:, sl], r_ref[:, sl], ig_ref[:, sl], a_ref[:, sl], mult_ref[:, sl]
            sp = _softplus(-lam_ref[:, sl])
            dh = dh_ref[:, sl]
            da = dh * ext_ref[7:7 + ts, sl]
            dmult = dh * ig * xg
            di = dh * mult * xg
            dxl = dh * mult * ig
            dlog_a = da * a - dmult * (a * a) / mult
            dlam_ref[:, sl] += jnp.sum(dlog_a * r, axis=0, keepdims=True) * (LRU_C * _sigmoid(-lam_ref[:, sl]))
            dpa = dlog_a * (-LRU_C * sp) * r * (1.0 - r)
            dpx = di * ig * (1.0 - ig)
            dba_ref[:, sl] += jnp.sum(dpa, axis=0, keepdims=True)
            dbx_ref[:, sl] += jnp.sum(dpx, axis=0, keepdims=True)
            dpa_m, dpx_m, xm = _mx(dpa), _mx(dpx), _mx(xg)
            dxl_ref[:, sl] = dxl + _dot_nt(dpa_m, wa_ref[g]) + _dot_nt(dpx_m, wx_ref[g])
            dwa_ref[g] += _dot_tn(xm, dpa_m)
            dwx_ref[g] += _dot_tn(xm, dpx_m)

    rev = lambda i: nt - 1 - i
    row = pl.BlockSpec((ts, w), lambda i: (rev(i), 0))
    vec = pl.BlockSpec((1, w), lambda i: (0, 0))
    wsp = pl.BlockSpec((4, LRU_G, LRU_G), lambda i: (0, 0, 0))
    lgs = pl.BlockSpec((ts, w), lambda i: (rev(i), LG[0] // w))
    return _pcall(
        body, ride, (xl, proj, hs, hs, dy, dproj, wa, wx, ba, bx, lam), grid=(nt,),
        in_specs=[row, lgs, row, pl.BlockSpec((8, w), lambda i: (jnp.maximum(rev(i) * hb - 1, 0), 0)), row,
                  pl.BlockSpec(memory_space=pl.ANY), wsp, wsp, vec, vec, vec],
        out_specs=[lgs, row, wsp, wsp, vec, vec, vec],
        out_shape=[jax.ShapeDtypeStruct(dproj.shape, dproj.dtype), jax.ShapeDtypeStruct((s, w), F32),
                   jax.ShapeDtypeStruct((4, LRU_G, LRU_G), F32), jax.ShapeDtypeStruct((4, LRU_G, LRU_G), F32),
                   jax.ShapeDtypeStruct((1, w), F32), jax.ShapeDtypeStruct((1, w), F32),
                   jax.ShapeDtypeStruct((1, w), F32)],
        scratch_shapes=[pltpu.VMEM((ts, w), F32), pltpu.VMEM((ts, w), F32), pltpu.VMEM((ts + 8, w), F32),
                        pltpu.VMEM((8, w), F32), pltpu.VMEM((ts, w), F32), pltpu.VMEM((ts, w), F32),
                        pltpu.VMEM((ts, w), F32)],
        aliases={5: 0}, sem=("arbitrary",), name="lru_bwd")


def _mem_kv_fwd(mem, g, wkv):
    m = mem.shape[0]

    def body(mem_ref, g_ref, w_ref, k_ref, v_ref, mn_ref):
        mv = mem_ref[...]
        r = lax.rsqrt(jnp.mean(mv * mv, axis=-1, keepdims=True) + EPS)
        mn = _mx(mv * r * g_ref[...])
        mn_ref[...] = mn
        kv = _dot(mn, w_ref[...])
        k_ref[...] = kv[:, 0:D].astype(k_ref.dtype)
        v_ref[...] = kv[:, D:2 * D].astype(v_ref.dtype)

    sh = jax.ShapeDtypeStruct((m, D), _MXU)
    return pl.pallas_call(body, out_shape=[sh, sh, sh], compiler_params=_cparams(None), name="mem_kv_fwd")(mem, g, wkv)


def _mem_kv_bwd(mem, g, mn, wkv, dk, dv):
    m = mem.shape[0]

    def body(mem_ref, g_ref, mn_ref, w_ref, dk_ref, dv_ref, dw_ref, dg_ref):
        dkv = _mx(jnp.concatenate([dk_ref[...], dv_ref[...]], axis=1))
        dw_ref[...] = _dot_tn(mn_ref[...], dkv).astype(dw_ref.dtype)
        dmn = _dot_nt(dkv, w_ref[...])
        mv = mem_ref[...]
        r = lax.rsqrt(jnp.mean(mv * mv, axis=-1, keepdims=True) + EPS)
        dg_ref[...] = jnp.sum(dmn * mv * r, axis=0, keepdims=True)

    del m
    return pl.pallas_call(
        body, out_shape=[jax.ShapeDtypeStruct((D, 2 * D), _MXU), jax.ShapeDtypeStruct((1, D), F32)],
        compiler_params=_cparams(None), name="mem_kv_bwd")(mem, g, mn, wkv, dk, dv)


def _attn_probs(q_ref, k_ref, hd):
    sl = slice(MEM_HD * hd, MEM_HD * hd + MEM_HD)
    qh = _mx(q_ref[:, sl])
    sc = _dot_nt(qh, k_ref[:, sl]) * (MEM_HD ** -0.5)
    e = jnp.exp(sc - jnp.max(sc, axis=-1, keepdims=True))
    return sl, qh, e / jnp.sum(e, axis=-1, keepdims=True)


def _attn_fwd(proj, k, v):
    s = proj.shape[0]
    m = k.shape[0]
    ts = _tile(s, 512)

    def body(q_ref, k_ref, v_ref, y_ref):
        for hd in range(MEM_HEADS):
            sl, _, p = _attn_probs(q_ref, k_ref, hd)
            y_ref[:, sl] = _dot(_mx(p), v_ref[:, sl]).astype(y_ref.dtype)

    kvs = pl.BlockSpec((m, D), lambda i: (0, 0))
    return pl.pallas_call(
        body, grid=(s // ts,),
        in_specs=[pl.BlockSpec((ts, D), lambda i: (i, Q[0] // D)), kvs, kvs],
        out_specs=pl.BlockSpec((ts, D), lambda i: (i, 0)),
        out_shape=jax.ShapeDtypeStruct((s, D), _MXU),
        compiler_params=_cparams(("parallel",)), name="attn_fwd")(proj, k, v)


def _attn_bwd(proj, k, v, dy, dproj):
    s = proj.shape[0]
    m = k.shape[0]
    ts = _tile(s, 512)

    def body(q_ref, k_ref, v_ref, dy_ref, dproj_hbm, dq_ref, dk_ref, dv_ref):
        del dproj_hbm

        @pl.when(pl.program_id(0) == 0)
        def _():
            dk_ref[...] = jnp.zeros_like(dk_ref)
            dv_ref[...] = jnp.zeros_like(dv_ref)

        for hd in range(MEM_HEADS):
            sl, qh, p = _attn_probs(q_ref, k_ref, hd)
            dyh = _mx(dy_ref[:, sl])
            dp = _dot_nt(dyh, v_ref[:, sl])
            ds = _mx(p * (dp - jnp.sum(dp * p, axis=-1, keepdims=True)) * (MEM_HD ** -0.5))
            dq_ref[:, sl] = _dot(ds, k_ref[:, sl]).astype(dq_ref.dtype)
            dk_ref[:, sl] += _dot_tn(ds, qh)
            dv_ref[:, sl] += _dot_tn(_mx(p), dyh)

    kvs = pl.BlockSpec((m, D), lambda i: (0, 0))
    qs = pl.BlockSpec((ts, D), lambda i: (i, Q[0] // D))
    return pl.pallas_call(
        body, grid=(s // ts,),
        in_specs=[qs, kvs, kvs, pl.BlockSpec((ts, D), lambda i: (i, 0)), pl.BlockSpec(memory_space=pl.ANY)],
        out_specs=[qs, kvs, kvs],
        out_shape=[jax.ShapeDtypeStruct(dproj.shape, dproj.dtype), jax.ShapeDtypeStruct((m, D), F32),
                   jax.ShapeDtypeStruct((m, D), F32)],
        input_output_aliases={4: 0},
        compiler_params=_cparams(("arbitrary",)), name="attn_bwd")(proj, k, v, dy, dproj)


def _merge_fb(x, target, yssd, ylru, ymem, proj, wbs, wbl, wbm, wo, fg):
    s = x.shape[0]
    ts = _tile(s, 256)

    def body(x_ref, t_ref, ys_ref, yl_ref, ym_ref, gl_ref, wbs_ref, wbl_ref, wbm_ref, wo_ref, fg_ref,
             dgl_ref, dx2_ref, dx2m_ref, mg_ref, db0_ref, db1_ref, db2_ref, loss_ref, dfg_ref):
        @pl.when(pl.program_id(0) == 0)
        def _():
            loss_ref[...] = jnp.zeros_like(loss_ref)
            dfg_ref[...] = jnp.zeros_like(dfg_ref)

        bs = (_dot(ys_ref[...], wbs_ref[...]), _dot(yl_ref[...], wbl_ref[...]), _dot(ym_ref[...], wbm_ref[...]))
        gates = [_sigmoid(gl_ref[:, D * n:D * n + D]) for n in range(3)]
        merged = gates[0] * bs[0] + gates[1] * bs[1] + gates[2] * bs[2]
        mg = _mx(merged)
        mg_ref[...] = mg
        x2 = x_ref[...] + _dot(mg, wo_ref[...])
        r = lax.rsqrt(jnp.mean(x2 * x2, axis=-1, keepdims=True) + EPS)
        xhat = x2 * r
        err = xhat * fg_ref[...] - t_ref[...]
        loss_ref[...] += jnp.sum(err * err, axis=0, keepdims=True) * (0.5 / D)
        dy = err * (1.0 / D)
        dfg_ref[...] += jnp.sum(dy * xhat, axis=0, keepdims=True)
        dxh = dy * fg_ref[...]
        dx2 = r * (dxh - xhat * jnp.mean(dxh * xhat, axis=-1, keepdims=True))
        dx2_ref[...] = dx2
        dx2m = _mx(dx2)
        dx2m_ref[...] = dx2m
        dmg = _dot_nt(dx2m, wo_ref[...])
        for n, db_ref in enumerate((db0_ref, db1_ref, db2_ref)):
            gt = gates[n]
            dgl_ref[:, D * n:D * n + D] = (dmg * bs[n] * gt * (1.0 - gt)).astype(dgl_ref.dtype)
            db_ref[...] = (dmg * gt).astype(db_ref.dtype)

    row = lambda w: pl.BlockSpec((ts, w), lambda i: (i, 0))
    full = lambda a: pl.BlockSpec(a.shape, lambda i: (0, 0))
    vec = pl.BlockSpec((1, D), lambda i: (0, 0))
    gls = pl.BlockSpec((ts, GL[1]), lambda i: (i, GL[0] // GL[1]))
    act = jax.ShapeDtypeStruct((s, D), _MXU)
    return pl.pallas_call(
        body, grid=(s // ts,),
        in_specs=[row(D), row(D), row(SSD_W), row(LRU_W), row(D), gls, full(wbs), full(wbl), full(wbm), full(wo), vec],
        out_specs=[gls, row(D), row(D), row(D), row(D), row(D), row(D), vec, vec],
        out_shape=[jax.ShapeDtypeStruct((s, NP), _MXU), jax.ShapeDtypeStruct((s, D), F32), act, act, act, act, act,
                   jax.ShapeDtypeStruct((1, D), F32), jax.ShapeDtypeStruct((1, D), F32)],
        compiler_params=_cparams(("arbitrary",)), name="merge_fwd_bwd")(
            x, target, yssd, ylru, ymem, proj, wbs, wbl, wbm, wo, fg)


def _adamw(w, g, m, v, name):
    rows, cols = w.shape
    tr = _tile(rows, 512, 8)

    def body(w_ref, g_ref, m_ref, v_ref, d_ref, mo_ref, vo_ref):
        gv = g_ref[...]
        mn = ADAM_B1 * m_ref[...] + (1.0 - ADAM_B1) * gv
        vn = ADAM_B2 * v_ref[...] + (1.0 - ADAM_B2) * (gv * gv)
        m_hat = mn / (1.0 - ADAM_B1 ** ADAM_STEP)
        v_hat = vn / (1.0 - ADAM_B2 ** ADAM_STEP)
        d_ref[...] = -ADAM_LR * (m_hat / (jnp.sqrt(v_hat) + ADAM_EPS) + ADAM_WD * w_ref[...])
        mo_ref[...] = mn
        vo_ref[...] = vn

    blk = pl.BlockSpec((tr, cols), lambda i: (i, 0))
    sh = jax.ShapeDtypeStruct((rows, cols), F32)
    return pl.pallas_call(
        body, grid=(rows // tr,), in_specs=[blk] * 4, out_specs=[blk] * 3, out_shape=[sh] * 3,
        compiler_params=_cparams(("parallel",)), name=name)(w, g, m, v)


def _mesh_pos():
    x, y, c = lax.axis_index("x"), lax.axis_index("y"), lax.axis_index("c")
    chips = [(1 - x, y), (x, 1 - y), (1 - x, 1 - y)]
    return x, y, c, 2 * x + y, chips


def _hbm():
    return pl.BlockSpec(memory_space=pl.ANY)


def _remote(src, dst, send_sem, recv_sem, dev):
    return pltpu.make_async_remote_copy(src_ref=src, dst_ref=dst, send_sem=send_sem, recv_sem=recv_sem,
                                        device_id=dev, device_id_type=MESH)


def _sems(n):
    return [pltpu.SemaphoreType.DMA((n,)), pltpu.SemaphoreType.DMA((n,))]


class _Exchange:
    def __init__(self, inputs, out_shape, n_sem, start, finish, aliases=None):
        self.inputs, self.out_shape, self.n_sem = list(inputs), list(out_shape), n_sem
        self.start, self.finish, self.aliases = start, finish, dict(aliases or {})


def _run_exchange(ex, name):
    n_in, n_out = len(ex.inputs), len(ex.out_shape)

    def body(*refs):
        srcs, outs = refs[:n_in], refs[n_in:n_in + n_out]
        send_sems, recv_sems = refs[n_in + n_out:]
        ex.start(srcs, outs, send_sems, recv_sems)
        ex.finish(srcs, outs, send_sems, recv_sems)

    return pl.pallas_call(
        body, in_specs=[_hbm()] * n_in, out_specs=[_hbm()] * n_out, out_shape=ex.out_shape,
        input_output_aliases=ex.aliases, scratch_shapes=_sems(ex.n_sem), name=name)(*ex.inputs)


def _pcall(body, ride, args, *, grid, in_specs, out_specs, out_shape, scratch_shapes, sem, name, aliases=None):
    in_specs, out_specs, out_shape = list(in_specs), list(out_specs), list(out_shape)
    scratch_shapes, aliases = list(scratch_shapes), dict(aliases or {})
    if ride is None:
        outs = pl.pallas_call(
            body, grid=grid, in_specs=in_specs, out_specs=out_specs, out_shape=out_shape, scratch_shapes=scratch_shapes,
            input_output_aliases=aliases, compiler_params=_cparams(sem), name=name)(*args)
        return outs, None
    n_in, n_out, n_scr = len(in_specs), len(out_shape), len(scratch_shapes)
    e_in, e_out = len(ride.inputs), len(ride.out_shape)

    def carried(*refs):
        cut = [n_in, e_in, n_out, e_out, n_scr]
        parts, p = [], 0
        for c in cut:
            parts.append(refs[p:p + c])
            p += c
        ins, e_ins, outs, e_outs, scr = parts
        send_sems, recv_sems = refs[p], refs[p + 1]
        first = last = None
        for d, size in enumerate(grid):
            i = pl.program_id(d)
            first = (i == 0) if first is None else jnp.logical_and(first, i == 0)
            last = (i == size - 1) if last is None else jnp.logical_and(last, i == size - 1)

        @pl.when(first)
        def _():
            ride.start(e_ins, e_outs, send_sems, recv_sems)

        body(*ins, *outs, *scr)

        @pl.when(last)
        def _():
            ride.finish(e_ins, e_outs, send_sems, recv_sems)

    for k, v in ride.aliases.items():
        aliases[n_in + k] = n_out + v
    res = pl.pallas_call(
        carried, grid=grid, in_specs=in_specs + [_hbm()] * e_in, out_specs=out_specs + [_hbm()] * e_out,
        out_shape=out_shape + ride.out_shape, scratch_shapes=scratch_shapes + _sems(ride.n_sem),
        input_output_aliases=aliases, compiler_params=_cparams(("arbitrary",) * len(grid)),
        name=name)(*args, *ride.inputs)
    return res[:n_out], res[n_out:]


def _gather_shards(arrs, split, relay):
    n = len(arrs)
    n_sem = sum(6 if sp else 3 for sp in split)

    def rows(i, which):
        if not split[i]:
            return pl.ds(0, arrs[i].shape[0])
        half = arrs[i].shape[0] // 2
        return pl.ds(which * half, half)

    def sends(srcs, outs, send_sems, recv_sems):
        _, _, c, me, chips = _mesh_pos()
        return [_remote(srcs[i].at[rows(i, c)], outs[i].at[me, rows(i, c)], send_sems.at[3 * i + j],
                        recv_sems.at[3 * i + j], (cx, cy, c))
                for i in range(n) for j, (cx, cy) in enumerate(chips) if not (relay and split[i] and j == 2)]

    def start(srcs, outs, send_sems, recv_sems):
        for cp in sends(srcs, outs, send_sems, recv_sems):
            cp.start()

    def finish(srcs, outs, send_sems, recv_sems):
        x, y, c, _, chips = _mesh_pos()
        sib = (x, y, 1 - c)
        first = ((x + 1 - c) % 2, (y + c) % 2)
        other = ((x + c) % 2, (y + 1 - c) % 2)
        started, k = [], 3 * n
        for i in range(n):
            sem = lambda j, i=i: (send_sems.at[3 * i + j], recv_sems.at[3 * i + j])
            if not split[i]:
                for j, (cx, cy) in enumerate(chips):
                    slot = outs[i].at[2 * cx + cy]
                    _remote(slot, slot, *sem(j), (cx, cy, c)).wait_recv()
                continue
            to_sib = lambda j, k=k: (send_sems.at[k + j], recv_sems.at[k + j])
            slot = lambda chip, which, i=i: outs[i].at[2 * chip[0] + chip[1], rows(i, which)]
            got = slot(first, c)
            _remote(got, got, *sem(c), (*first, c)).wait_recv()
            if relay:
                started.append(_remote(got, got, *sem(2), (*other, c)))
                started[-1].start()
            started.append(_remote(got, got, *to_sib(c), sib))
            started[-1].start()
            for chip, j_in, j_sib in ((other, 1 - c, 1 - c), (chips[2], 2, 2)):
                got = slot(chip, c)
                _remote(got, got, *sem(j_in), (*chip, c)).wait_recv()
                started.append(_remote(got, got, *to_sib(j_sib), sib))
                started[-1].start()
            for chip, j_sib in ((first, c), (other, 1 - c), (chips[2], 2)):
                theirs = slot(chip, 1 - c)
                _remote(theirs, theirs, *to_sib(j_sib), sib).wait_recv()
            k += 3
        for cp in sends(srcs, outs, send_sems, recv_sems) + started:
            cp.wait_send()

    return _Exchange(arrs, [jax.ShapeDtypeStruct((NSHARD,) + a.shape, a.dtype) for a in arrs], n_sem, start, finish)


def _with_own_slot(arrs, got):
    own_slot = jnp.arange(NSHARD, dtype=jnp.int32)[:, None, None] == 2 * lax.axis_index("x") + lax.axis_index("y")
    return [jnp.where(own_slot, a[None], g) for a, g in zip(arrs, got)]


def _swap_halves(arrs):
    n = len(arrs)

    def copies(srcs, outs, send_sems, recv_sems):
        x, y, c, _, _ = _mesh_pos()
        cps = []
        for i in range(n):
            half = arrs[i].shape[1] // 2
            cps.append(_remote(srcs[i].at[:, pl.ds((1 - c) * half, half)], outs[i], send_sems.at[i], recv_sems.at[i],
                               (x, y, 1 - c)))
        return cps

    def start(*refs):
        for cp in copies(*refs):
            cp.start()

    def finish(*refs):
        for cp in copies(*refs):
            cp.wait()

    shapes = [jax.ShapeDtypeStruct((NSHARD, a.shape[1] // 2, a.shape[2]), a.dtype) for a in arrs]
    return _Exchange(arrs, shapes, n, start, finish)


def _scatter_chips(arrs):
    n = len(arrs)

    def copies(srcs, outs, send_sems, recv_sems):
        _, _, c, me, chips = _mesh_pos()
        own = [pltpu.make_async_copy(srcs[i].at[me], outs[i].at[me], send_sems.at[3 * n + i]) for i in range(n)]
        cps = [_remote(srcs[i].at[2 * cx + cy], outs[i].at[me], send_sems.at[3 * i + j], recv_sems.at[3 * i + j],
                       (cx, cy, c)) for i in range(n) for j, (cx, cy) in enumerate(chips)]
        return own, cps

    def start(*refs):
        own, cps = copies(*refs)
        for cp in own + cps:
            cp.start()

    def finish(srcs, outs, send_sems, recv_sems):
        _, _, c, _, chips = _mesh_pos()
        for i in range(n):
            for j, (cx, cy) in enumerate(chips):
                slot = outs[i].at[2 * cx + cy]
                _remote(slot, slot, send_sems.at[3 * i + j], recv_sems.at[3 * i + j], (cx, cy, c)).wait_recv()
        own, cps = copies(srcs, outs, send_sems, recv_sems)
        for cp in cps:
            cp.wait_send()
        for cp in own:
            cp.wait()

    return _Exchange(arrs, [jax.ShapeDtypeStruct(a.shape, a.dtype) for a in arrs], 4 * n, start, finish)


def _share_halves(arrs):
    n = len(arrs)

    def copies(outs, send_sems, recv_sems):
        x, y, c, _, _ = _mesh_pos()
        return [_remote(outs[i].at[c], outs[i].at[c], send_sems.at[i], recv_sems.at[i], (x, y, 1 - c))
                for i in range(n)]

    def start(srcs, outs, send_sems, recv_sems):
        del srcs
        for cp in copies(outs, send_sems, recv_sems):
            cp.start()

    def finish(srcs, outs, send_sems, recv_sems):
        del srcs
        x, y, c, _, _ = _mesh_pos()
        for i in range(n):
            theirs = outs[i].at[1 - c]
            _remote(theirs, theirs, send_sems.at[i], recv_sems.at[i], (x, y, 1 - c)).wait_recv()
        for cp in copies(outs, send_sems, recv_sems):
            cp.wait_send()

    return _Exchange(arrs, [jax.ShapeDtypeStruct(a.shape, a.dtype) for a in arrs], n, start, finish,
                     aliases={i: i for i in range(n)})


def _gather_small(full):
    _, width = full.shape

    def copies(srcs, outs, send_sems, recv_sems):
        _, _, c, me, chips = _mesh_pos()
        mine = srcs[0].at[pl.ds(0, SMALL_ROWS)]
        own = pltpu.make_async_copy(mine, outs[0].at[me], send_sems.at[3])
        return own, [_remote(mine, outs[0].at[me], send_sems.at[j], recv_sems.at[j], (cx, cy, c))
                     for j, (cx, cy) in enumerate(chips)]

    def start(*refs):
        own, cps = copies(*refs)
        for cp in [own] + cps:
            cp.start()

    def finish(srcs, outs, send_sems, recv_sems):
        _, _, c, _, chips = _mesh_pos()
        for j, (cx, cy) in enumerate(chips):
            slot = outs[0].at[2 * cx + cy]
            _remote(slot, slot, send_sems.at[j], recv_sems.at[j], (cx, cy, c)).wait_recv()
        own, cps = copies(srcs, outs, send_sems, recv_sems)
        for cp in cps:
            cp.wait_send()
        own.wait()

    return _Exchange([full], [jax.ShapeDtypeStruct((NSHARD, SMALL_ROWS, width), full.dtype)], 4, start, finish)


def _add_sibling(mine, recv, c, name):
    _, half, width = recv.shape
    tr = _tile(half, 256, 8)
    nb = half // tr

    def body(c_ref, a_ref, b_ref, o_ref):
        del c_ref
        o_ref[...] = (a_ref[...].astype(F32) + b_ref[...].astype(F32)).astype(o_ref.dtype)

    grid_spec = pltpu.PrefetchScalarGridSpec(
        num_scalar_prefetch=1, grid=(NSHARD, nb),
        in_specs=[pl.BlockSpec((1, tr, width), lambda j, r, c_ref: (j, c_ref[0] * nb + r, 0)),
                  pl.BlockSpec((1, tr, width), lambda j, r, c_ref: (j, r, 0))],
        out_specs=pl.BlockSpec((1, tr, width), lambda j, r, c_ref: (j, r, 0)))
    return pl.pallas_call(
        body, grid_spec=grid_spec, out_shape=jax.ShapeDtypeStruct(recv.shape, recv.dtype),
        compiler_params=_cparams(("parallel", "parallel")), name=name)(c, mine, recv)


def _sum_chips(parts, c, name):
    _, half, width = parts.shape
    tr = _tile(half, 256, 8)

    def body(c_ref, p_ref, o_ref):
        del c_ref
        p = [p_ref[j].astype(F32) for j in range(NSHARD)]
        o_ref[0] = ((p[0] + p[1]) + p[2]) + p[3]

    grid_spec = pltpu.PrefetchScalarGridSpec(
        num_scalar_prefetch=1, grid=(half // tr,),
        in_specs=[pl.BlockSpec((NSHARD, tr, width), lambda r, c_ref: (0, r, 0))],
        out_specs=pl.BlockSpec((1, tr, width), lambda r, c_ref: (c_ref[0], r, 0)))
    return pl.pallas_call(
        body, grid_spec=grid_spec, out_shape=jax.ShapeDtypeStruct((2, half, width), F32),
        compiler_params=_cparams(("parallel",)), name=name)(c, parts)


def _unpack(flat, names, shapes):
    out, off = {}, 0
    for n in names:
        sz = _size(shapes[n])
        out[n] = flat[off:off + sz].reshape(shapes[n])
        off += sz
    return out


W_IN_COLS = 3080
W_IN_PAD = 3136


def _reorder_w_in_t(w):
    return jnp.concatenate([w[2048:5120], w[9248:12320], w[0:2048], w[8224:9248], w[5152:6688], w[6688:8224],
                            w[5120:5152], jnp.zeros((NP - 12320, D), w.dtype)], axis=0)


def _restore_w_in_t(g):
    return jnp.concatenate([g[6144:8192], g[0:3072], g[12288:12320], g[9216:10752], g[10752:12288], g[8192:9216],
                            g[3072:6144]], axis=0)


def _lru_group_weights(w):
    w4 = w.reshape(4, 4, 96, 96)
    eye = jnp.eye(4, dtype=w.dtype)
    return (w4[:, :, None, :, :] * eye[None, :, :, None, None]).transpose(0, 1, 3, 2, 4).reshape(4, LRU_G, LRU_G)


def _lru_group_blocks(g):
    g5 = g.reshape(4, 4, 96, 4, 96)
    return jnp.stack([g5[:, a, :, a, :] for a in range(4)], axis=1).reshape(16, 96, 96)


def _spread(a):
    return a.transpose(1, 0, 2).reshape(a.shape[1], NSHARD * a.shape[2])


def _split(a):
    return a.reshape(a.shape[0], NSHARD, a.shape[1] // NSHARD).transpose(1, 0, 2)


class _Reduction:
    def __init__(self, dist, parts, names):
        self.c, self.parts, self.names = dist.c, parts, names

    def swap(self):
        return _swap_halves(self.parts)

    def scatter(self, recv):
        return _scatter_chips([_add_sibling(p, r, self.c, "add_sibling_" + n)
                               for p, r, n in zip(self.parts, recv, self.names)])

    def share(self, landed):
        return _share_halves([_sum_chips(a, self.c, "sum_chips_" + n) for a, n in zip(landed, self.names)])

    def done(self, shared):
        return [a.reshape(2 * a.shape[1], a.shape[2]) for a in shared]


class _Dist:
    def __init__(self, w_in_shard, late_shards):
        self.c = lax.axis_index("c").astype(jnp.int32).reshape(1)
        self.w_in_shard = [w_in_shard]
        self.late_shards = late_shards

    def w_in_ride(self):
        return _gather_shards(self.w_in_shard, [True], relay=True)

    def w_in_arrived(self, got):
        (g_in,) = _with_own_slot(self.w_in_shard, got)
        return _reorder_w_in_t(g_in[:, 0:W_IN_COLS].reshape(NSHARD * W_IN_COLS, D))

    def weights_ride(self):
        return _gather_shards(self.late_shards, [True, True, False], relay=False)

    def weights_arrived(self, got):
        g_kv, g_rows, g_small = _with_own_slot(self.late_shards, got)
        out = {"w_kv": _spread(g_kv)}
        for n, lo_, hi_ in ROW_PIECES:
            out[n] = g_rows[:, lo_:hi_].reshape(NSHARD * (hi_ - lo_), D)
        out["ssd_conv_w"] = _spread(g_small[:, :, 0:768])
        out["ssd_norm_g"] = _spread(g_small[:, :, 768:896])
        out["lru_conv_w"] = _spread(g_small[:, :, 896:1280])
        return out

    def early_parts(self, grads):
        rows = jnp.concatenate([grads[n].reshape(NSHARD, hi_ - lo_, D) for n, lo_, hi_ in ROW_PIECES], axis=1)
        return [_split(grads["w_kv"]), rows]

    def late_parts(self, grads):
        rows = _restore_w_in_t(grads["w_in_rt"]).reshape(NSHARD, W_IN_COLS, D)
        return [jnp.pad(rows, ((0, 0), (0, W_IN_PAD - W_IN_COLS), (0, 0)))]


def _local_grads(x, mem, target, wts, dist=None):
    pad128 = lambda a: jnp.pad(a, ((0, 0), (0, 128 - a.shape[1])))

    if dist is None:
        (h,), _ = _norm_fwd(x, wts["norm_g"])
        w_in_rt = wts["w_in_rt"]
        proj = _mm(h, w_in_rt, F32, "in_proj", tb=True, tn=NP_TILE)
    else:
        (h,), arrived = _norm_fwd(x, wts["norm_g"], ride=dist.w_in_ride())
        w_in_rt = dist.w_in_arrived(arrived)
        proj, arrived = _mm(h, w_in_rt, F32, "in_proj", tb=True, tn=NP_TILE, ride=dist.weights_ride())
        wts = dict(wts, **dist.weights_arrived(arrived))
    wbs, wbl, wbm, wo, wkv = wts["w_br_ssd"], wts["w_br_lru"], wts["w_br_mem"], wts["w_out"], wts["w_kv"]
    wa, wx = _mx(_lru_group_weights(wts["lru_w_a"])), _mx(_lru_group_weights(wts["lru_w_x"]))
    ba, bx = wts["lru_b_a"].reshape(1, LRU_W), wts["lru_b_x"].reshape(1, LRU_W)
    dtb, alog = pad128(wts["ssd_dt_bias"]), pad128(wts["ssd_a_log"])
    dexp = jnp.repeat(wts["ssd_d"], 64, axis=1)
    ng = wts["ssd_norm_g"].reshape(1, SSD_W)
    xbc = _conv_fwd(proj, XBC, wts["ssd_conv_w"], wts["ssd_conv_b"], True, "ssd_conv_fwd")
    yssd, yraw, hprev = _ssd_fwd(xbc, proj, dtb, alog, dexp, ng)
    xl = _conv_fwd(proj, LX, wts["lru_conv_w"], wts["lru_conv_b"], False, "lru_conv_fwd")
    ylru, hs = _lru_fwd(xl, proj, wa, wx, ba, bx, wts["lru_lambda"])
    kk, vv, mn = _mem_kv_fwd(mem, wts["mem_norm_g"], wkv)
    ymem = _attn_fwd(proj, kk, vv)

    dproj, dx2, dx2m, merged, db0, db1, db2, loss_vec, dfg = _merge_fb(
        x, target, yssd, ylru, ymem, proj, wbs, wbl, wbm, wo, wts["final_g"].reshape(1, D))
    grads = {"final_g": dfg.reshape(D)}
    grads["w_out"] = _mm(merged, dx2m, _MXU, "dw_out", ta=True)
    grads["w_br_ssd"] = _mm(yssd, db0, _MXU, "dw_br_ssd", ta=True)
    grads["w_br_lru"] = _mm(ylru, db1, _MXU, "dw_br_lru", ta=True)
    grads["w_br_mem"] = _mm(ymem, db2, _MXU, "dw_br_mem", ta=True)
    dyssd = _mm(db0, wbs, F32, "dy_ssd", tb=True)
    dylru = _mm(db1, wbl, F32, "dy_lru", tb=True)
    dymem = _mm(db2, wbm, F32, "dy_mem", tb=True)

    dproj, dk, dv = _attn_bwd(proj, kk, vv, dymem, dproj)
    grads["w_kv"], grads["mem_norm_g"] = _mem_kv_bwd(mem, wts["mem_norm_g"], mn, wkv, dk, dv)

    early = None if dist is None else _Reduction(dist, dist.early_parts(grads), ["w_kv", "rows"])

    (dproj, dxl, dwa, dwx, dba, dbx, dlam), got = _lru_bwd(
        xl, proj, hs, dylru, dproj, wa, wx, ba, bx, wts["lru_lambda"], ride=early and early.swap())
    grads["lru_w_a"] = _lru_group_blocks(dwa)[None]
    grads["lru_w_x"] = _lru_group_blocks(dwx)[None]
    grads["lru_b_a"], grads["lru_b_x"] = dba.reshape(1, 16, 96), dbx.reshape(1, 16, 96)
    grads["lru_lambda"] = dlam
    (grads["lru_conv_w"], grads["lru_conv_b"]), _ = _conv_bwd_w(
        proj, LX, wts["lru_conv_w"], wts["lru_conv_b"], dxl, False, "lru_conv_bwd_w")
    dproj = _conv_bwd_x(dxl, wts["lru_conv_w"], dproj, LX, "lru_conv_bwd_x")

    (dproj, ddt, dxbc, dng, dda, ddd, ddtb), got = _ssd_bwd(
        xbc, proj, yraw, hprev, dyssd, dproj, dtb, alog, dexp, ng, ride=early and early.scatter(got))
    dproj = _put_block(ddt, dproj, DT, "put_ddt")
    grads["ssd_norm_g"] = dng.reshape(4, 512)
    grads["ssd_dt_bias"] = ddtb[:, 0:32]
    grads["ssd_a_log"] = (dda * -jnp.exp(alog))[:, 0:32]
    grads["ssd_d"] = ddd.reshape(32, 64).sum(axis=1)[None, :]
    (dpre, grads["ssd_conv_w"], grads["ssd_conv_b"]), got = _conv_bwd_w(
        proj, XBC, wts["ssd_conv_w"], wts["ssd_conv_b"], dxbc, True, "ssd_conv_bwd_w", ride=early and early.share(got))
    reduced = {} if dist is None else dict(zip(["w_kv", "rows"], early.done(got)))
    dproj = _conv_bwd_x(dpre, wts["ssd_conv_w"], dproj, XBC, "ssd_conv_bwd_x")

    grads["w_in_rt"] = _mm(dproj, h, _MXU, "dw_in", ta=True, tm=NP_TILE, tn=1024)
    if dist is None:
        dh = _mm(dproj, w_in_rt, F32, "dh", tn=1024, tk=NP_TILE)
        (grad_x, grads["norm_g"]), _ = _norm_bwd(x, wts["norm_g"], dh, dx2)
    else:
        late = _Reduction(dist, dist.late_parts(grads), ["w_in"])
        got = _run_exchange(late.swap(), "swap_halves_w_in")
        dh, got = _mm(dproj, w_in_rt, F32, "dh", tn=1024, tk=NP_TILE, ride=late.scatter(got))
        (grad_x, grads["norm_g"]), _ = _norm_bwd(x, wts["norm_g"], dh, dx2)
        reduced["w_in"] = late.done(_run_exchange(late.share(got), "share_halves_w_in"))[0]
    return jnp.sum(loss_vec), grad_x, grads, reduced


def kernel(x, mem, norm_g, w_in, ssd_conv_w, ssd_conv_b, ssd_dt_bias, ssd_a_log, ssd_d, ssd_norm_g, lru_conv_w, lru_conv_b, lru_w_a, lru_b_a, lru_w_x, lru_b_x, lru_lambda, mem_norm_g, w_kv, w_br_ssd, w_br_lru, w_br_mem, w_out, final_g, loss_target, m_norm_g, m_w_in, m_ssd_conv_w, m_ssd_conv_b, m_ssd_dt_bias, m_ssd_a_log, m_ssd_d, m_ssd_norm_g, m_lru_conv_w, m_lru_conv_b, m_lru_w_a, m_lru_b_a, m_lru_w_x, m_lru_b_x, m_lru_lambda, m_mem_norm_g, m_w_kv, m_w_br_ssd, m_w_br_lru, m_w_br_mem, m_w_out, m_final_g, v_norm_g, v_w_in, v_ssd_conv_w, v_ssd_conv_b, v_ssd_dt_bias, v_ssd_a_log, v_ssd_d, v_ssd_norm_g, v_lru_conv_w, v_lru_conv_b, v_lru_w_a, v_lru_b_a, v_lru_w_x, v_lru_b_x, v_lru_lambda, v_mem_norm_g, v_w_kv, v_w_br_ssd, v_w_br_lru, v_w_br_mem, v_w_out, v_final_g):
    given = dict(locals())

    rows_w = jnp.concatenate([w_br_ssd[0], w_br_lru[0], w_br_mem[0], w_out[0]], axis=0)
    small_w = jnp.concatenate([ssd_conv_w[0], ssd_norm_g[0], lru_conv_w[0]], axis=1)
    w_in_t = jnp.pad(_mx(w_in[0].T), ((0, W_IN_PAD - W_IN_COLS), (0, 0)))
    dist = _Dist(w_in_t, [_mx(w_kv[0]), _mx(rows_w), small_w])
    wts = {n: given[n] for n in REPL}
    wts["lru_w_a"], wts["lru_w_x"] = lru_w_a[0], lru_w_x[0]

    loss_part, grad_x, grads, reduced = _local_grads(x[0], mem[0], loss_target[0], wts, dist)
    loss = lax.psum(loss_part, ("x", "y", "c"))

    repl_flat = jnp.concatenate([grads[n].reshape(-1) for n in REPL])
    repl_flat = jnp.pad(repl_flat, (0, NSHARD * SMALL_Q - repl_flat.shape[0])).reshape(NSHARD, SMALL_Q)
    shard_small = jnp.concatenate([_split(grads[n]).reshape(NSHARD, -1) for n in SMALL_SHARDED], axis=1)
    p_small = jnp.concatenate(
        [repl_flat, shard_small, jnp.zeros((NSHARD, SMALL_BUF_ROWS * PACK_W - SMALL_Q - 5120), F32)], axis=1)
    small = _Reduction(dist, [p_small.reshape(NSHARD, SMALL_BUF_ROWS, PACK_W)], ["small"])
    got = _run_exchange(small.swap(), "swap_halves_small")
    got = _run_exchange(small.scatter(got), "scatter_chips_small")
    got = _run_exchange(small.share(got), "share_halves_small")
    r_small = small.done(got)[0]
    repl_all = _run_exchange(_gather_small(r_small), "gather_small")[0].reshape(-1)

    g_shard = {"w_kv": reduced["w_kv"]}
    for n, lo_, hi_ in ROW_PIECES:
        g_shard[n] = reduced["rows"][lo_:hi_]
    g_shard.update(_unpack(r_small.reshape(-1)[SMALL_Q:], SMALL_SHARDED, SHARD_SHAPE))
    g_repl = _unpack(repl_all, REPL, REPL_SHAPE)

    out_g, out_d, out_m, out_v = {}, {}, {}, {}
    for n in WEIGHTS:
        w_full = given[n]
        if n == "w_in":
            g2 = reduced["w_in"][0:W_IN_COLS]
            d, mo, vo = _adamw(w_in[0].T, g2, m_w_in[0].T, v_w_in[0].T, "adamw_w_in")
            out_g[n], out_d[n], out_m[n], out_v[n] = [a.T[None] for a in (g2, d, mo, vo)]
            continue
        g = (g_shard[n] if n in SHARDED else g_repl[n]).reshape(w_full.shape)
        cols = w_full.shape[-1]
        as2d = lambda a: a.reshape(-1, cols)
        d, mo, vo = _adamw(as2d(w_full), as2d(g), as2d(given["m_" + n]), as2d(given["v_" + n]), "adamw_" + n)
        out_g[n] = g
        out_d[n], out_m[n], out_v[n] = d.reshape(w_full.shape), mo.reshape(w_full.shape), vo.reshape(w_full.shape)

    return (loss, grad_x[None], *[out_g[n] for n in WEIGHTS], *[out_d[n] for n in WEIGHTS],
            *[out_m[n] for n in WEIGHTS], *[out_v[n] for n in WEIGHTS])
```

```python
import jax
import jax.numpy as jnp
from jax import lax
from jax.experimental import pallas as pl
from jax.experimental.pallas import tpu as pltpu

F32 = jnp.float32
_MXU = jnp.bfloat16
_HI = lax.Precision.HIGHEST
MESH = pl.DeviceIdType.MESH

D = 1024
EPS = 1e-6
MEM_HEADS = 4
MEM_HD = 256
LRU_C = 8.0
SSD_L = 128
SSD_W = 2048
LRU_W = 1536
NSHARD = 4

XBC = (0, 3072)
GL = (3072, 3072)
Z = (6144, 2048)
Q = (8192, 1024)
LG = (9216, 1536)
LX = (10752, 1536)
DT = (12288, 256)
NP = 12544
NP_TILE = 1792

ADAM_LR = 0.001
ADAM_B1 = 0.9
ADAM_B2 = 0.999
ADAM_EPS = 1e-08
ADAM_WD = 0.01
ADAM_STEP = 10

VMEM_LIMIT = 56 * 1024 * 1024

SHARDED = ("w_in", "ssd_conv_w", "ssd_norm_g", "lru_conv_w", "w_kv", "w_br_ssd", "w_br_lru", "w_br_mem", "w_out")
SHARD_SHAPE = {"w_in": (1024, 3080), "ssd_conv_w": (4, 768), "ssd_norm_g": (4, 128), "lru_conv_w": (4, 384),
               "w_kv": (1024, 512), "w_br_ssd": (512, 1024), "w_br_lru": (384, 1024), "w_br_mem": (256, 1024),
               "w_out": (256, 1024)}
REPL = ("norm_g", "ssd_conv_b", "ssd_dt_bias", "ssd_a_log", "ssd_d", "lru_conv_b", "lru_w_a", "lru_b_a",
        "lru_w_x", "lru_b_x", "lru_lambda", "mem_norm_g", "final_g")
REPL_SHAPE = {"norm_g": (1, 1024), "ssd_conv_b": (1, 3072), "ssd_dt_bias": (1, 32), "ssd_a_log": (1, 32),
              "ssd_d": (1, 32), "lru_conv_b": (1, 1536), "lru_w_a": (1, 16, 96, 96), "lru_b_a": (1, 16, 96),
              "lru_w_x": (1, 16, 96, 96), "lru_b_x": (1, 16, 96), "lru_lambda": (1, 1536),
              "mem_norm_g": (1, 1024), "final_g": (1024,)}
WEIGHTS = ("norm_g", "w_in", "ssd_conv_w", "ssd_conv_b", "ssd_dt_bias", "ssd_a_log", "ssd_d", "ssd_norm_g",
           "lru_conv_w", "lru_conv_b", "lru_w_a", "lru_b_a", "lru_w_x", "lru_b_x", "lru_lambda", "mem_norm_g",
           "w_kv", "w_br_ssd", "w_br_lru", "w_br_mem", "w_out", "final_g")

ROW_PIECES = (("w_br_ssd", 0, 512), ("w_br_lru", 512, 896), ("w_br_mem", 896, 1152), ("w_out", 1152, 1408))
SMALL_SHARDED = ("ssd_conv_w", "ssd_norm_g", "lru_conv_w")
PACK_W = 512
SMALL_ROWS = 152
SMALL_Q = SMALL_ROWS * PACK_W
SMALL_BUF_ROWS = 176


def _size(shape):
    n = 1
    for s in shape:
        n *= s
    return n


def _sigmoid(x):
    return 0.5 * jnp.tanh(0.5 * x) + 0.5


def _silu(x):
    return x * _sigmoid(x)


def _dsilu(x):
    s = _sigmoid(x)
    return s * (1.0 + x * (1.0 - s))


def _softplus(x):
    return jnp.maximum(x, 0.0) + jnp.log(1.0 + jnp.exp(-jnp.abs(x)))


def _one_minus_sq(log_a, a):
    x = 2.0 * log_a
    series = -x * (1.0 + x * (0.5 + x * (1.0 / 6.0 + x * (1.0 / 24.0))))
    return jnp.where(x > -0.03, series, 1.0 - a * a)


def _dot(a, b, precision=None):
    return jnp.dot(a, b, preferred_element_type=F32, precision=precision)


def _dot_nt(a, b):
    return lax.dot_general(a, b, (((1,), (1,)), ((), ())), preferred_element_type=F32)


def _dot_tn(a, b):
    return lax.dot_general(a, b, (((0,), (0,)), ((), ())), preferred_element_type=F32)


def _mx(a):
    return a.astype(_MXU)


def _cparams(sem):
    return pltpu.CompilerParams(dimension_semantics=sem, vmem_limit_bytes=VMEM_LIMIT)


def _tile(n, want, mult=128):
    if n <= want:
        return n
    for t in range(want - want % mult, 0, -mult):
        if n % t == 0:
            return t
    raise ValueError((n, want, mult))


def _mm(a, b, out_dtype, name, ta=False, tb=False, tm=1024, tn=1280, tk=1024, ride=None):
    k, m = a.shape if ta else a.shape[::-1]
    k2, n = b.shape[::-1] if tb else b.shape
    assert k == k2
    tm, tn, tk = _tile(m, tm), _tile(n, tn), _tile(k, tk)
    nk = k // tk
    contract = (((0 if ta else 1,), (1 if tb else 0,)), ((), ()))

    def body(a_ref, b_ref, o_ref, acc_ref):
        kk = pl.program_id(2)

        @pl.when(kk == 0)
        def _():
            acc_ref[...] = jnp.zeros_like(acc_ref)

        acc_ref[...] += lax.dot_general(a_ref[...], b_ref[...], contract, preferred_element_type=F32)

        @pl.when(kk == nk - 1)
        def _():
            o_ref[...] = acc_ref[...].astype(o_ref.dtype)

    a_spec = pl.BlockSpec((tk, tm), lambda i, j, kk: (kk, i)) if ta else pl.BlockSpec((tm, tk), lambda i, j, kk: (i, kk))
    b_spec = pl.BlockSpec((tn, tk), lambda i, j, kk: (j, kk)) if tb else pl.BlockSpec((tk, tn), lambda i, j, kk: (kk, j))
    outs, carried = _pcall(
        body, ride, (a, b), grid=(m // tm, n // tn, nk),
        in_specs=[a_spec, b_spec],
        out_specs=[pl.BlockSpec((tm, tn), lambda i, j, kk: (i, j))],
        out_shape=[jax.ShapeDtypeStruct((m, n), out_dtype)],
        scratch_shapes=[pltpu.VMEM((tm, tn), F32)],
        sem=("parallel", "parallel", "arbitrary"), name=name)
    return outs[0] if ride is None else (outs[0], carried)


def _norm_fwd(x, g, ride=None):
    s = x.shape[0]
    ts = _tile(s, 512)

    def body(x_ref, g_ref, h_ref):
        xv = x_ref[...]
        r = lax.rsqrt(jnp.mean(xv * xv, axis=-1, keepdims=True) + EPS)
        h_ref[...] = (xv * r * g_ref[...]).astype(h_ref.dtype)

    return _pcall(
        body, ride, (x, g), grid=(s // ts,),
        in_specs=[pl.BlockSpec((ts, D), lambda i: (i, 0)), pl.BlockSpec((1, D), lambda i: (0, 0))],
        out_specs=[pl.BlockSpec((ts, D), lambda i: (i, 0))],
        out_shape=[jax.ShapeDtypeStruct((s, D), _MXU)], scratch_shapes=[], sem=("parallel",), name="norm_fwd")


def _norm_bwd(x, g, dh, dx2, ride=None):
    s = x.shape[0]
    ts = _tile(s, 512)

    def body(x_ref, g_ref, dh_ref, dx2_ref, gx_ref, dg_ref):
        @pl.when(pl.program_id(0) == 0)
        def _():
            dg_ref[...] = jnp.zeros_like(dg_ref)

        xv = x_ref[...]
        r = lax.rsqrt(jnp.mean(xv * xv, axis=-1, keepdims=True) + EPS)
        xhat = xv * r
        dh_v = dh_ref[...]
        dg_ref[...] += jnp.sum(dh_v * xhat, axis=0, keepdims=True)
        dxh = dh_v * g_ref[...]
        gx_ref[...] = dx2_ref[...] + r * (dxh - xhat * jnp.mean(dxh * xhat, axis=-1, keepdims=True))

    row = pl.BlockSpec((ts, D), lambda i: (i, 0))
    vec = pl.BlockSpec((1, D), lambda i: (0, 0))
    return _pcall(
        body, ride, (x, g, dh, dx2), grid=(s // ts,), in_specs=[row, vec, row, row], out_specs=[row, vec],
        out_shape=[jax.ShapeDtypeStruct((s, D), F32), jax.ShapeDtypeStruct((1, D), F32)],
        scratch_shapes=[], sem=("arbitrary",), name="norm_bwd")


CONV_TS = 512
CONV_RB = 16
CONV_LC = 256


def _fold8(v):
    acc = v[0:8]
    for r0 in range(8, v.shape[0], 8):
        acc = acc + v[r0:r0 + 8]
    return acc


def _conv_fwd(src, blk, w, b, act, name):
    s = src.shape[0]
    off, width = blk
    cb = off // width
    ts = _tile(s, CONV_TS)

    def body(x_ref, w_ref, b_ref, o_ref, ext_ref):
        @pl.when(pl.program_id(0) == 0)
        def _():
            ext_ref[0:8, :] = jnp.zeros((8, width), F32)

        ext_ref[8:8 + ts, :] = x_ref[...]
        for l0 in range(0, width, CONV_LC):
            ls = slice(l0, l0 + CONV_LC)
            taps = [w_ref[k:k + 1, ls] for k in range(4)]
            bias = b_ref[:, ls]
            for r0 in range(0, ts, CONV_RB):
                pre = bias
                for k in range(4):
                    pre = pre + taps[k] * ext_ref[5 + k + r0:5 + k + r0 + CONV_RB, ls]
                o_ref[r0:r0 + CONV_RB, ls] = _silu(pre) if act else pre
        ext_ref[0:8, :] = x_ref[ts - 8:ts, :]

    return pl.pallas_call(
        body, grid=(s // ts,),
        in_specs=[pl.BlockSpec((ts, width), lambda i: (i, cb)), pl.BlockSpec((4, width), lambda i: (0, 0)),
                  pl.BlockSpec((1, width), lambda i: (0, 0))],
        out_specs=pl.BlockSpec((ts, width), lambda i: (i, 0)),
        out_shape=jax.ShapeDtypeStruct((s, width), F32),
        scratch_shapes=[pltpu.VMEM((ts + 8, width), F32)],
        compiler_params=_cparams(("arbitrary",)), name=name)(src, w, b)


def _conv_bwd_w(src, blk, w, b, dout, act, name, ride=None):
    s = src.shape[0]
    off, width = blk
    cb = off // width
    ts = _tile(s, CONV_TS)

    def body(x_ref, w_ref, b_ref, do_ref, *rest):
        if act:
            dpre_ref, dw_ref, db_ref, ext_ref = rest
        else:
            dw_ref, db_ref, ext_ref = rest

        @pl.when(pl.program_id(0) == 0)
        def _():
            ext_ref[0:8, :] = jnp.zeros((8, width), F32)
            dw_ref[...] = jnp.zeros_like(dw_ref)
            db_ref[...] = jnp.zeros_like(db_ref)

        ext_ref[8:8 + ts, :] = x_ref[...]
        for l0 in range(0, width, CONV_LC):
            ls = slice(l0, l0 + CONV_LC)
            taps = [w_ref[k:k + 1, ls] for k in range(4)]
            bias = b_ref[:, ls]
            acc_b = jnp.zeros((8, CONV_LC), F32)
            acc_w = [jnp.zeros((8, CONV_LC), F32) for _ in range(4)]
            for r0 in range(0, ts, CONV_RB):
                xs = [ext_ref[5 + k + r0:5 + k + r0 + CONV_RB, ls] for k in range(4)]
                dpre = do_ref[r0:r0 + CONV_RB, ls]
                if act:
                    pre = bias
                    for k in range(4):
                        pre = pre + taps[k] * xs[k]
                    dpre = dpre * _dsilu(pre)
                    dpre_ref[r0:r0 + CONV_RB, ls] = dpre
                acc_b = acc_b + _fold8(dpre)
                for k in range(4):
                    acc_w[k] = acc_w[k] + _fold8(dpre * xs[k])
            db_ref[:, ls] += jnp.sum(acc_b, axis=0, keepdims=True)
            for k in range(4):
                dw_ref[k:k + 1, ls] += jnp.sum(acc_w[k], axis=0, keepdims=True)
        ext_ref[0:8, :] = x_ref[ts - 8:ts, :]

    row = pl.BlockSpec((ts, width), lambda i: (i, 0))
    outs = [pl.BlockSpec((4, width), lambda i: (0, 0)), pl.BlockSpec((1, width), lambda i: (0, 0))]
    shapes = [jax.ShapeDtypeStruct((4, width), F32), jax.ShapeDtypeStruct((1, width), F32)]
    if act:
        outs = [row] + outs
        shapes = [jax.ShapeDtypeStruct((s, width), F32)] + shapes
    return _pcall(
        body, ride, (src, w, b, dout), grid=(s // ts,),
        in_specs=[pl.BlockSpec((ts, width), lambda i: (i, cb)), pl.BlockSpec((4, width), lambda i: (0, 0)),
                  pl.BlockSpec((1, width), lambda i: (0, 0)), row],
        out_specs=outs, out_shape=shapes,
        scratch_shapes=[pltpu.VMEM((ts + 8, width), F32)], sem=("arbitrary",), name=name)


def _conv_bwd_x(dpre, w, dproj, blk, name):
    s = dpre.shape[0]
    off, width = blk
    cb = off // width
    ts = _tile(s, CONV_TS)
    nt = s // ts

    def body(dp_ref, w_ref, dproj_hbm, o_ref, ext_ref):
        del dproj_hbm

        @pl.when(pl.program_id(0) == 0)
        def _():
            ext_ref[ts:ts + 8, :] = jnp.zeros((8, width), F32)

        ext_ref[0:ts, :] = dp_ref[...]
        for l0 in range(0, width, CONV_LC):
            ls = slice(l0, l0 + CONV_LC)
            taps = [w_ref[k:k + 1, ls] for k in range(4)]
            for r0 in range(0, ts, CONV_RB):
                acc = taps[0] * ext_ref[3 + r0:3 + r0 + CONV_RB, ls]
                for k in range(1, 4):
                    acc = acc + taps[k] * ext_ref[3 - k + r0:3 - k + r0 + CONV_RB, ls]
                o_ref[r0:r0 + CONV_RB, ls] = acc.astype(o_ref.dtype)
        ext_ref[ts:ts + 8, :] = dp_ref[0:8, :]

    return pl.pallas_call(
        body, grid=(nt,),
        in_specs=[pl.BlockSpec((ts, width), lambda i: (nt - 1 - i, 0)), pl.BlockSpec((4, width), lambda i: (0, 0)),
                  pl.BlockSpec(memory_space=pl.ANY)],
        out_specs=pl.BlockSpec((ts, width), lambda i: (nt - 1 - i, cb)),
        out_shape=jax.ShapeDtypeStruct(dproj.shape, dproj.dtype),
        scratch_shapes=[pltpu.VMEM((ts + 8, width), F32)],
        input_output_aliases={2: 0},
        compiler_params=_cparams(("arbitrary",)), name=name)(dpre, w, dproj)


def _ssd_decay(a_cs, acst_ref, h, causal, lane_l):
    col = jnp.sum(jnp.where(lane_l == h, a_cs, 0.0), axis=1, keepdims=True)
    row = acst_ref[h:h + 1, :]
    return jnp.where(causal, jnp.exp(jnp.minimum(col - row, 0.0)), 0.0)


def _split3(x):
    hi = x.astype(jnp.bfloat16)
    rest = x - hi.astype(F32)
    mid = rest.astype(jnp.bfloat16)
    return jnp.concatenate([hi, mid, (rest - mid.astype(F32)).astype(jnp.bfloat16)], axis=1)


def _spread_matrix():
    col = jnp.arange(128, dtype=jnp.int32)[:, None]
    e64 = (col == jnp.arange(SSD_W, dtype=jnp.int32)[None, :] // 64).astype(jnp.bfloat16)
    return jnp.tile(e64, (3, 1))


def _ssd_common(dt_ref, dtb_ref, alog_ref, e64_ref, acst_ref, dtx_ref, acx_ref):
    ll = SSD_L
    dt = _softplus(dt_ref[:, 0:128] + dtb_ref[...])
    a_neg = -jnp.exp(alog_ref[...])
    ri = lax.broadcasted_iota(jnp.int32, (ll, ll), 0)
    ci = lax.broadcasted_iota(jnp.int32, (ll, ll), 1)
    causal = ri >= ci
    a_cs = _dot(causal.astype(F32), dt * a_neg, _HI)
    acst_ref[...] = a_cs.T
    both = _dot(jnp.concatenate([_split3(dt), _split3(a_cs)], axis=0), e64_ref[...])
    dtx_ref[...] = both[0:ll]
    acx_ref[...] = both[ll:2 * ll]
    lane_l = lax.broadcasted_iota(jnp.int32, (ll, 128), 1)
    return dt, a_neg, a_cs, causal, ri, lane_l, lane_l < 64


def _ssd_fwd(xbc, proj, dtb, alog, dexp, ng):
    s = xbc.shape[0]
    ll = SSD_L
    nc = s // ll
    e64 = _spread_matrix()

    def body(xbc_ref, dt_ref, z_ref, dtb_ref, alog_ref, dexp_ref, ng_ref, e64_ref,
             yssd_ref, yraw_ref, hprev_ref, ht_ref, acst_ref, dtx_ref, acx_ref):
        @pl.when(pl.program_id(0) == 0)
        def _():
            ht_ref[...] = jnp.zeros_like(ht_ref)

        hprev_ref[0] = ht_ref[...]
        _, _, a_cs, causal, _, lane_l, lo = _ssd_common(dt_ref, dtb_ref, alog_ref, e64_ref, acst_ref, dtx_ref, acx_ref)
        for g in range(4):
            bg = _mx(xbc_ref[:, 2048 + 128 * g:2176 + 128 * g])
            cg = _mx(xbc_ref[:, 2560 + 128 * g:2688 + 128 * g])
            cbm = _dot_nt(cg, bg)
            for jj in range(4):
                j = 4 * g + jj
                sl = slice(128 * j, 128 * j + 128)
                xp = xbc_ref[:, sl]
                acx = acx_ref[:, sl]
                a_last = acx_ref[ll - 1:ll, sl]
                xdt = xp * dtx_ref[:, sl]
                acc = None
                for hh in range(2):
                    dec = _ssd_decay(a_cs, acst_ref, 2 * j + hh, causal, lane_l)
                    xm = jnp.where(lo if hh == 0 else jnp.logical_not(lo), xdt, 0.0)
                    t = _dot(_mx(dec * cbm), _mx(xm))
                    acc = t if acc is None else acc + t
                ht = ht_ref[j]
                y = acc + _dot(cg, _mx(ht)) * jnp.exp(acx) + xp * dexp_ref[:, sl]
                yraw_ref[:, sl] = y
                st = _dot_tn(bg, _mx(xdt * jnp.exp(a_last - acx)))
                ht_ref[j] = ht * jnp.exp(a_last) + st
        for g in range(4):
            sl = slice(512 * g, 512 * g + 512)
            yg = yraw_ref[:, sl] * _silu(z_ref[:, sl])
            r = lax.rsqrt(jnp.mean(yg * yg, axis=-1, keepdims=True) + EPS)
            yssd_ref[:, sl] = (yg * r * ng_ref[:, sl]).astype(yssd_ref.dtype)

    vec = lambda w: pl.BlockSpec((1, w), lambda c: (0, 0))
    return pl.pallas_call(
        body, grid=(nc,),
        in_specs=[pl.BlockSpec((ll, 3072), lambda c: (c, 0)),
                  pl.BlockSpec((ll, DT[1]), lambda c: (c, DT[0] // DT[1])),
                  pl.BlockSpec((ll, Z[1]), lambda c: (c, Z[0] // Z[1])),
                  vec(128), vec(128), vec(2048), vec(2048),
                  pl.BlockSpec(e64.shape, lambda c: (0, 0))],
        out_specs=[pl.BlockSpec((ll, 2048), lambda c: (c, 0)), pl.BlockSpec((ll, 2048), lambda c: (c, 0)),
                   pl.BlockSpec((1, 16, 128, 128), lambda c: (c, 0, 0, 0))],
        out_shape=[jax.ShapeDtypeStruct((s, 2048), _MXU), jax.ShapeDtypeStruct((s, 2048), F32),
                   jax.ShapeDtypeStruct((nc, 16, 128, 128), F32)],
        scratch_shapes=[pltpu.VMEM((16, 128, 128), F32), pltpu.VMEM((128, ll), F32),
                        pltpu.VMEM((ll, 2048), F32), pltpu.VMEM((ll, 2048), F32)],
        compiler_params=_cparams(("arbitrary",)), name="ssd_fwd")(xbc, proj, proj, dtb, alog, dexp, ng, e64)


def _ssd_bwd(xbc, proj, yraw, hprev, dyssd, dproj, dtb, alog, dexp, ng, ride=None):
    s = xbc.shape[0]
    ll = SSD_L
    nc = s // ll
    e64 = _spread_matrix()

    def body(xbc_ref, dt_ref, z_ref, yraw_ref, hprev_ref, dy_ref, dproj_hbm, dtb_ref, alog_ref, dexp_ref, ng_ref,
             e64_ref,
             dz_ref, ddt_ref, dxbc_ref, dng_ref, dda_ref, ddd_ref, ddtb_ref,
             dht_ref, acst_ref, dtx_ref, acx_ref, dyr_ref, rowt_ref):
        del dproj_hbm

        @pl.when(pl.program_id(0) == 0)
        def _():
            dht_ref[...] = jnp.zeros_like(dht_ref)
            dng_ref[...] = jnp.zeros_like(dng_ref)
            dda_ref[...] = jnp.zeros_like(dda_ref)
            ddd_ref[...] = jnp.zeros_like(ddd_ref)
            ddtb_ref[...] = jnp.zeros_like(ddtb_ref)
            rowt_ref[...] = jnp.zeros_like(rowt_ref)

        for g in range(4):
            sl = slice(512 * g, 512 * g + 512)
            zz = z_ref[:, sl]
            yr = yraw_ref[:, sl]
            sz = _silu(zz)
            yg = yr * sz
            r = lax.rsqrt(jnp.mean(yg * yg, axis=-1, keepdims=True) + EPS)
            yhat = yg * r
            dyv = dy_ref[:, sl]
            dng_ref[:, sl] += jnp.sum(dyv * yhat, axis=0, keepdims=True)
            dyh = dyv * ng_ref[:, sl]
            dyg = r * (dyh - yhat * jnp.mean(dyh * yhat, axis=-1, keepdims=True))
            dz_ref[:, sl] = (dyg * yr * _dsilu(zz)).astype(dz_ref.dtype)
            dyr_ref[:, sl] = dyg * sz

        dt, a_neg, a_cs, causal, ri, lane_l, lo = _ssd_common(dt_ref, dtb_ref, alog_ref, e64_ref,
                                                              acst_ref, dtx_ref, acx_ref)
        lane_1 = lax.broadcasted_iota(jnp.int32, (1, 128), 1)
        da_col = jnp.zeros((ll, 128), F32)
        ddt_x = jnp.zeros((ll, 128), F32)
        last = jnp.zeros((1, 128), F32)
        for g in range(4):
            bg = _mx(xbc_ref[:, 2048 + 128 * g:2176 + 128 * g])
            cg = _mx(xbc_ref[:, 2560 + 128 * g:2688 + 128 * g])
            cbm = _dot_nt(cg, bg)
            dcb = jnp.zeros((ll, ll), F32)
            db_g = jnp.zeros((ll, 128), F32)
            dc_g = jnp.zeros((ll, 128), F32)
            for jj in range(4):
                j = 4 * g + jj
                sl = slice(128 * j, 128 * j + 128)
                xp = xbc_ref[:, sl]
                dtx = dtx_ref[:, sl]
                acx = acx_ref[:, sl]
                a_last = acx_ref[ll - 1:ll, sl]
                ea = jnp.exp(acx)
                dte = jnp.exp(a_last - acx)
                cd = jnp.exp(a_last)
                xdt = xp * dtx
                xdt_m = _mx(xdt)
                dy = dyr_ref[:, sl]
                ht = hprev_ref[0, j]
                dhn = dht_ref[j]
                dhn_m = _mx(dhn)
                gmat = _dot(bg, dhn_m)
                dxdt = gmat * dte
                for hh in range(2):
                    h = 2 * j + hh
                    dec = _ssd_decay(a_cs, acst_ref, h, causal, lane_l)
                    mm = dec * cbm
                    dym = _mx(jnp.where(lo if hh == 0 else jnp.logical_not(lo), dy, 0.0))
                    dxdt = dxdt + _dot_tn(_mx(mm), dym)
                    dm = _dot_nt(dym, xdt_m)
                    dcb = dcb + dm * dec
                    qq = dm * mm
                    da_col = da_col + jnp.where(lane_l == h, jnp.sum(qq, axis=1, keepdims=True), 0.0)
                    rowt_ref[h:h + 1, :] = jnp.sum(qq, axis=0, keepdims=True)
                ch = _dot(cg, _mx(ht))
                dyea = dy * ea
                dyea_m = _mx(dyea)
                xw_m = _mx(xdt * dte)
                dc_g = dc_g + _dot_nt(dyea_m, _mx(ht))
                db_g = db_g + _dot_nt(xw_m, dhn_m)
                wl = xdt * gmat * dte
                lane_a = dyea * ch - wl
                lane_b = dxdt * xp
                lane_c = jnp.sum(dhn * ht, axis=0, keepdims=True) * cd + jnp.sum(wl, axis=0, keepdims=True)
                for hh in range(2):
                    h = 2 * j + hh
                    mine = lo if hh == 0 else jnp.logical_not(lo)
                    da_col = da_col + jnp.where(
                        lane_l == h, jnp.sum(jnp.where(mine, lane_a, 0.0), axis=1, keepdims=True), 0.0)
                    ddt_x = ddt_x + jnp.where(
                        lane_l == h, jnp.sum(jnp.where(mine, lane_b, 0.0), axis=1, keepdims=True), 0.0)
                    mine_1 = (lane_1 < 64) if hh == 0 else (lane_1 >= 64)
                    last = last + jnp.where(
                        lane_1 == h, jnp.sum(jnp.where(mine_1, lane_c, 0.0), axis=1, keepdims=True), 0.0)
                dht_ref[j] = dhn * cd + _dot_tn(cg, dyea_m)
                dxbc_ref[:, sl] = dxdt * dtx + dy * dexp_ref[:, sl]
                ddd_ref[:, sl] += jnp.sum(dy * xp, axis=0, keepdims=True)
            dcb_m = _mx(dcb)
            dxbc_ref[:, 2048 + 128 * g:2176 + 128 * g] = db_g + _dot_tn(dcb_m, cg)
            dxbc_ref[:, 2560 + 128 * g:2688 + 128 * g] = dc_g + _dot(dcb_m, bg)

        da_cs = da_col - rowt_ref[...].T
        da_cs = da_cs + jnp.where(lax.broadcasted_iota(jnp.int32, (ll, 128), 0) == ll - 1, last, 0.0)
        d_dta = _dot((ri <= lax.broadcasted_iota(jnp.int32, (ll, ll), 1)).astype(F32), da_cs, _HI)
        ddt = d_dta * a_neg + ddt_x
        dda_ref[...] += jnp.sum(d_dta * dt, axis=0, keepdims=True)
        ddt_raw = ddt * _sigmoid(dt_ref[:, 0:128] + dtb_ref[...])
        ddtb_ref[...] += jnp.sum(ddt_raw, axis=0, keepdims=True)
        ddt_ref[:, 0:128] = ddt_raw.astype(ddt_ref.dtype)
        ddt_ref[:, 128:DT[1]] = jnp.zeros((ll, DT[1] - 128), ddt_ref.dtype)

    rev = lambda c: nc - 1 - c
    vec = lambda w: pl.BlockSpec((1, w), lambda c: (0, 0))
    row = lambda w: pl.BlockSpec((ll, w), lambda c: (rev(c), 0))
    return _pcall(
        body, ride, (xbc, proj, proj, yraw, hprev, dyssd, dproj, dtb, alog, dexp, ng, e64), grid=(nc,),
        in_specs=[row(3072),
                  pl.BlockSpec((ll, DT[1]), lambda c: (rev(c), DT[0] // DT[1])),
                  pl.BlockSpec((ll, Z[1]), lambda c: (rev(c), Z[0] // Z[1])),
                  row(2048),
                  pl.BlockSpec((1, 16, 128, 128), lambda c: (rev(c), 0, 0, 0)),
                  row(2048),
                  pl.BlockSpec(memory_space=pl.ANY),
                  vec(128), vec(128), vec(2048), vec(2048),
                  pl.BlockSpec(e64.shape, lambda c: (0, 0))],
        out_specs=[pl.BlockSpec((ll, Z[1]), lambda c: (rev(c), Z[0] // Z[1])),
                   row(DT[1]),
                   row(3072), vec(2048), vec(128), vec(2048), vec(128)],
        out_shape=[jax.ShapeDtypeStruct(dproj.shape, dproj.dtype), jax.ShapeDtypeStruct((s, DT[1]), dproj.dtype),
                   jax.ShapeDtypeStruct((s, 3072), F32), jax.ShapeDtypeStruct((1, 2048), F32),
                   jax.ShapeDtypeStruct((1, 128), F32), jax.ShapeDtypeStruct((1, 2048), F32),
                   jax.ShapeDtypeStruct((1, 128), F32)],
        scratch_shapes=[pltpu.VMEM((16, 128, 128), F32), pltpu.VMEM((128, ll), F32),
                        pltpu.VMEM((ll, 2048), F32), pltpu.VMEM((ll, 2048), F32), pltpu.VMEM((ll, 2048), F32),
                        pltpu.VMEM((128, ll), F32)],
        aliases={6: 0}, sem=("arbitrary",), name="ssd_bwd")


def _put_block(src, dproj, blk, name):
    s = src.shape[0]
    off, width = blk
    cb = off // width
    ts = _tile(s, 1024)

    def body(s_ref, dproj_hbm, o_ref):
        del dproj_hbm
        o_ref[...] = s_ref[...]

    return pl.pallas_call(
        body, grid=(s // ts,),
        in_specs=[pl.BlockSpec((ts, width), lambda i: (i, 0)), pl.BlockSpec(memory_space=pl.ANY)],
        out_specs=pl.BlockSpec((ts, width), lambda i: (i, cb)),
        out_shape=jax.ShapeDtypeStruct(dproj.shape, dproj.dtype),
        input_output_aliases={1: 0},
        compiler_params=_cparams(("parallel",)), name=name)(src, dproj)


LRU_G = 384


def _lru_gates(xl_ref, wa_ref, wx_ref, ba_ref, bx_ref, lam_ref, g):
    sl = slice(LRU_G * g, LRU_G * g + LRU_G)
    xg = xl_ref[:, sl]
    xm = _mx(xg)
    pa = _dot(xm, wa_ref[g]) + ba_ref[:, sl]
    r = jnp.where(pa < -12.0, jnp.exp(pa), _sigmoid(pa))
    ig = _sigmoid(_dot(xm, wx_ref[g]) + bx_ref[:, sl])
    sp = _softplus(-lam_ref[:, sl])
    log_a = (-LRU_C * r) * sp
    a = jnp.exp(log_a)
    mult = jnp.sqrt(_one_minus_sq(log_a, a))
    return sl, xg, r, ig, sp, a, mult


def _lru_fwd(xl, proj, wa, wx, ba, bx, lam):
    s = xl.shape[0]
    ts = _tile(s, 256)
    w = LRU_W

    def body(xl_ref, lg_ref, wa_ref, wx_ref, ba_ref, bx_ref, lam_ref, y_ref, hs_ref, a_ref, u_ref, carry_ref):
        @pl.when(pl.program_id(0) == 0)
        def _():
            carry_ref[...] = jnp.zeros_like(carry_ref)

        for g in range(4):
            sl, xg, _, ig, _, a, mult = _lru_gates(xl_ref, wa_ref, wx_ref, ba_ref, bx_ref, lam_ref, g)
            a_ref[:, sl] = a
            u_ref[:, sl] = mult * (ig * xg)

        def step(t, h):
            h = a_ref[pl.ds(t, 1), :] * h + u_ref[pl.ds(t, 1), :]
            hs_ref[pl.ds(t, 1), :] = h
            return h

        carry_ref[0:1, :] = lax.fori_loop(0, ts, step, carry_ref[0:1, :], unroll=8)
        y_ref[...] = (hs_ref[...] * _silu(lg_ref[...])).astype(y_ref.dtype)

    row = pl.BlockSpec((ts, w), lambda i: (i, 0))
    vec = pl.BlockSpec((1, w), lambda i: (0, 0))
    wsp = pl.BlockSpec((4, LRU_G, LRU_G), lambda i: (0, 0, 0))
    return pl.pallas_call(
        body, grid=(s // ts,),
        in_specs=[row, pl.BlockSpec((ts, w), lambda i: (i, LG[0] // w)), wsp, wsp, vec, vec, vec],
        out_specs=[row, row],
        out_shape=[jax.ShapeDtypeStruct((s, w), _MXU), jax.ShapeDtypeStruct((s, w), F32)],
        scratch_shapes=[pltpu.VMEM((ts, w), F32), pltpu.VMEM((ts, w), F32), pltpu.VMEM((8, w), F32)],
        compiler_params=_cparams(("arbitrary",)), name="lru_fwd")(xl, proj, wa, wx, ba, bx, lam)


def _lru_bwd(xl, proj, hs, dy, dproj, wa, wx, ba, bx, lam, ride=None):
    s = xl.shape[0]
    ts = _tile(s, 256)
    nt = s // ts
    w = LRU_W
    hb = ts // 8

    def body(xl_ref, lg_ref, hs_ref, hprev_ref, dy_ref, dproj_hbm, wa_ref, wx_ref, ba_ref, bx_ref, lam_ref,
             dlg_ref, dxl_ref, dwa_ref, dwx_ref, dba_ref, dbx_ref, dlam_ref,
             a_ref, dh_ref, ext_ref, carry_ref, r_ref, ig_ref, mult_ref):
        del dproj_hbm
        i = pl.program_id(0)

        @pl.when(i == 0)
        def _():
            carry_ref[...] = jnp.zeros_like(carry_ref)
            for ref in (dwa_ref, dwx_ref, dba_ref, dbx_ref, dlam_ref):
                ref[...] = jnp.zeros_like(ref)

        lg = lg_ref[...]
        dyv = dy_ref[...]
        dh_ref[...] = dyv * _silu(lg)
        dlg_ref[...] = (dyv * hs_ref[...] * _dsilu(lg)).astype(dlg_ref.dtype)
        for g in range(4):
            sl, _, r, ig, _, a, mult = _lru_gates(xl_ref, wa_ref, wx_ref, ba_ref, bx_ref, lam_ref, g)
            a_ref[:, sl] = a
            r_ref[:, sl] = r
            ig_ref[:, sl] = ig
            mult_ref[:, sl] = mult

        def step(k, carry):
            t = ts - 1 - k
            dh = dh_ref[pl.ds(t, 1), :] + carry
            dh_ref[pl.ds(t, 1), :] = dh
            return a_ref[pl.ds(t, 1), :] * dh

        carry_ref[0:1, :] = lax.fori_loop(0, ts, step, carry_ref[0:1, :], unroll=8)

        ext_ref[0:8, :] = jnp.where(i == nt - 1, 0.0, 1.0) * hprev_ref[...]
        ext_ref[8:8 + ts, :] = hs_ref[...]
        for g in range(4):
            sl = slice(LRU_G * g, LRU_G * g + LRU_G)
            xg, r, ig, a, mult = xl_ref[:, sl], r_ref[:, sl], ig_ref[:, sl], a_ref[:, sl], mult_ref[:, sl]
            sp = _softplus(-lam_ref[:, sl])
            dh = dh_ref[:, sl]
            da = dh * ext_ref[7:7 + ts, sl]
            dmult = dh * ig * xg
            di = dh * mult * xg
            dxl = dh * mult * ig
            dlog_a = da * a - dmult * (a * a) / mult
            dlam_ref[:, sl] += jnp.sum(dlog_a * r, axis=0, keepdims=True) * (LRU_C * _sigmoid(-lam_ref[:, sl]))
            dpa = dlog_a * (-LRU_C * sp) * r * (1.0 - r)
            dpx = di * ig * (1.0 - ig)
            dba_ref[:, sl] += jnp.sum(dpa, axis=0, keepdims=True)
            dbx_ref[:, sl] += jnp.sum(dpx, axis=0, keepdims=True)
            dpa_m, dpx_m, xm = _mx(dpa), _mx(dpx), _mx(xg)
            dxl_ref[:, sl] = dxl + _dot_nt(dpa_m, wa_ref[g]) + _dot_nt(dpx_m, wx_ref[g])
            dwa_ref[g] += _dot_tn(xm, dpa_m)
            dwx_ref[g] += _dot_tn(xm, dpx_m)

    rev = lambda i: nt - 1 - i
    row = pl.BlockSpec((ts, w), lambda i: (rev(i), 0))
    vec = pl.BlockSpec((1, w), lambda i: (0, 0))
    wsp = pl.BlockSpec((4, LRU_G, LRU_G), lambda i: (0, 0, 0))
    lgs = pl.BlockSpec((ts, w), lambda i: (rev(i), LG[0] // w))
    return _pcall(
        body, ride, (xl, proj, hs, hs, dy, dproj, wa, wx, ba, bx, lam), grid=(nt,),
        in_specs=[row, lgs, row, pl.BlockSpec((8, w), lambda i: (jnp.maximum(rev(i) * hb - 1, 0), 0)), row,
                  pl.BlockSpec(memory_space=pl.ANY), wsp, wsp, vec, vec, vec],
        out_specs=[lgs, row, wsp, wsp, vec, vec, vec],
        out_shape=[jax.ShapeDtypeStruct(dproj.shape, dproj.dtype), jax.ShapeDtypeStruct((s, w), F32),
                   jax.ShapeDtypeStruct((4, LRU_G, LRU_G), F32), jax.ShapeDtypeStruct((4, LRU_G, LRU_G), F32),
                   jax.ShapeDtypeStruct((1, w), F32), jax.ShapeDtypeStruct((1, w), F32),
                   jax.ShapeDtypeStruct((1, w), F32)],
        scratch_shapes=[pltpu.VMEM((ts, w), F32), pltpu.VMEM((ts, w), F32), pltpu.VMEM((ts + 8, w), F32),
                        pltpu.VMEM((8, w), F32), pltpu.VMEM((ts, w), F32), pltpu.VMEM((ts, w), F32),
                        pltpu.VMEM((ts, w), F32)],
        aliases={5: 0}, sem=("arbitrary",), name="lru_bwd")


def _mem_kv_fwd(mem, g, wkv):
    m = mem.shape[0]

    def body(mem_ref, g_ref, w_ref, k_ref, v_ref, mn_ref):
        mv = mem_ref[...]
        r = lax.rsqrt(jnp.mean(mv * mv, axis=-1, keepdims=True) + EPS)
        mn = _mx(mv * r * g_ref[...])
        mn_ref[...] = mn
        kv = _dot(mn, w_ref[...])
        k_ref[...] = kv[:, 0:D].astype(k_ref.dtype)
        v_ref[...] = kv[:, D:2 * D].astype(v_ref.dtype)

    sh = jax.ShapeDtypeStruct((m, D), _MXU)
    return pl.pallas_call(body, out_shape=[sh, sh, sh], compiler_params=_cparams(None), name="mem_kv_fwd")(mem, g, wkv)


def _mem_kv_bwd(mem, g, mn, wkv, dk, dv):
    m = mem.shape[0]

    def body(mem_ref, g_ref, mn_ref, w_ref, dk_ref, dv_ref, dw_ref, dg_ref):
        dkv = _mx(jnp.concatenate([dk_ref[...], dv_ref[...]], axis=1))
        dw_ref[...] = _dot_tn(mn_ref[...], dkv).astype(dw_ref.dtype)
        dmn = _dot_nt(dkv, w_ref[...])
        mv = mem_ref[...]
        r = lax.rsqrt(jnp.mean(mv * mv, axis=-1, keepdims=True) + EPS)
        dg_ref[...] = jnp.sum(dmn * mv * r, axis=0, keepdims=True)

    del m
    return pl.pallas_call(
        body, out_shape=[jax.ShapeDtypeStruct((D, 2 * D), _MXU), jax.ShapeDtypeStruct((1, D), F32)],
        compiler_params=_cparams(None), name="mem_kv_bwd")(mem, g, mn, wkv, dk, dv)


def _attn_probs(q_ref, k_ref, hd):
    sl = slice(MEM_HD * hd, MEM_HD * hd + MEM_HD)
    qh = _mx(q_ref[:, sl])
    sc = _dot_nt(qh, k_ref[:, sl]) * (MEM_HD ** -0.5)
    e = jnp.exp(sc - jnp.max(sc, axis=-1, keepdims=True))
    return sl, qh, e / jnp.sum(e, axis=-1, keepdims=True)


def _attn_fwd(proj, k, v):
    s = proj.shape[0]
    m = k.shape[0]
    ts = _tile(s, 512)

    def body(q_ref, k_ref, v_ref, y_ref):
        for hd in range(MEM_HEADS):
            sl, _, p = _attn_probs(q_ref, k_ref, hd)
            y_ref[:, sl] = _dot(_mx(p), v_ref[:, sl]).astype(y_ref.dtype)

    kvs = pl.BlockSpec((m, D), lambda i: (0, 0))
    return pl.pallas_call(
        body, grid=(s // ts,),
        in_specs=[pl.BlockSpec((ts, D), lambda i: (i, Q[0] // D)), kvs, kvs],
        out_specs=pl.BlockSpec((ts, D), lambda i: (i, 0)),
        out_shape=jax.ShapeDtypeStruct((s, D), _MXU),
        compiler_params=_cparams(("parallel",)), name="attn_fwd")(proj, k, v)


def _attn_bwd(proj, k, v, dy, dproj):
    s = proj.shape[0]
    m = k.shape[0]
    ts = _tile(s, 512)

    def body(q_ref, k_ref, v_ref, dy_ref, dproj_hbm, dq_ref, dk_ref, dv_ref):
        del dproj_hbm

        @pl.when(pl.program_id(0) == 0)
        def _():
            dk_ref[...] = jnp.zeros_like(dk_ref)
            dv_ref[...] = jnp.zeros_like(dv_ref)

        for hd in range(MEM_HEADS):
            sl, qh, p = _attn_probs(q_ref, k_ref, hd)
            dyh = _mx(dy_ref[:, sl])
            dp = _dot_nt(dyh, v_ref[:, sl])
            ds = _mx(p * (dp - jnp.sum(dp * p, axis=-1, keepdims=True)) * (MEM_HD ** -0.5))
            dq_ref[:, sl] = _dot(ds, k_ref[:, sl]).astype(dq_ref.dtype)
            dk_ref[:, sl] += _dot_tn(ds, qh)
            dv_ref[:, sl] += _dot_tn(_mx(p), dyh)

    kvs = pl.BlockSpec((m, D), lambda i: (0, 0))
    qs = pl.BlockSpec((ts, D), lambda i: (i, Q[0] // D))
    return pl.pallas_call(
        body, grid=(s // ts,),
        in_specs=[qs, kvs, kvs, pl.BlockSpec((ts, D), lambda i: (i, 0)), pl.BlockSpec(memory_space=pl.ANY)],
        out_specs=[qs, kvs, kvs],
        out_shape=[jax.ShapeDtypeStruct(dproj.shape, dproj.dtype), jax.ShapeDtypeStruct((m, D), F32),
                   jax.ShapeDtypeStruct((m, D), F32)],
        input_output_aliases={4: 0},
        compiler_params=_cparams(("arbitrary",)), name="attn_bwd")(proj, k, v, dy, dproj)


def _merge_fb(x, target, yssd, ylru, ymem, proj, wbs, wbl, wbm, wo, fg):
    s = x.shape[0]
    ts = _tile(s, 256)

    def body(x_ref, t_ref, ys_ref, yl_ref, ym_ref, gl_ref, wbs_ref, wbl_ref, wbm_ref, wo_ref, fg_ref,
             dgl_ref, dx2_ref, dx2m_ref, mg_ref, db0_ref, db1_ref, db2_ref, loss_ref, dfg_ref):
        @pl.when(pl.program_id(0) == 0)
        def _():
            loss_ref[...] = jnp.zeros_like(loss_ref)
            dfg_ref[...] = jnp.zeros_like(dfg_ref)

        bs = (_dot(ys_ref[...], wbs_ref[...]), _dot(yl_ref[...], wbl_ref[...]), _dot(ym_ref[...], wbm_ref[...]))
        gates = [_sigmoid(gl_ref[:, D * n:D * n + D]) for n in range(3)]
        merged = gates[0] * bs[0] + gates[1] * bs[1] + gates[2] * bs[2]
        mg = _mx(merged)
        mg_ref[...] = mg
        x2 = x_ref[...] + _dot(mg, wo_ref[...])
        r = lax.rsqrt(jnp.mean(x2 * x2, axis=-1, keepdims=True) + EPS)
        xhat = x2 * r
        err = xhat * fg_ref[...] - t_ref[...]
        loss_ref[...] += jnp.sum(err * err, axis=0, keepdims=True) * (0.5 / D)
        dy = err * (1.0 / D)
        dfg_ref[...] += jnp.sum(dy * xhat, axis=0, keepdims=True)
        dxh = dy * fg_ref[...]
        dx2 = r * (dxh - xhat * jnp.mean(dxh * xhat, axis=-1, keepdims=True))
        dx2_ref[...] = dx2
        dx2m = _mx(dx2)
        dx2m_ref[...] = dx2m
        dmg = _dot_nt(dx2m, wo_ref[...])
        for n, db_ref in enumerate((db0_ref, db1_ref, db2_ref)):
            gt = gates[n]
            dgl_ref[:, D * n:D * n + D] = (dmg * bs[n] * gt * (1.0 - gt)).astype(dgl_ref.dtype)
            db_ref[...] = (dmg * gt).astype(db_ref.dtype)

    row = lambda w: pl.BlockSpec((ts, w), lambda i: (i, 0))
    full = lambda a: pl.BlockSpec(a.shape, lambda i: (0, 0))
    vec = pl.BlockSpec((1, D), lambda i: (0, 0))
    gls = pl.BlockSpec((ts, GL[1]), lambda i: (i, GL[0] // GL[1]))
    act = jax.ShapeDtypeStruct((s, D), _MXU)
    return pl.pallas_call(
        body, grid=(s // ts,),
        in_specs=[row(D), row(D), row(SSD_W), row(LRU_W), row(D), gls, full(wbs), full(wbl), full(wbm), full(wo), vec],
        out_specs=[gls, row(D), row(D), row(D), row(D), row(D), row(D), vec, vec],
        out_shape=[jax.ShapeDtypeStruct((s, NP), _MXU), jax.ShapeDtypeStruct((s, D), F32), act, act, act, act, act,
                   jax.ShapeDtypeStruct((1, D), F32), jax.ShapeDtypeStruct((1, D), F32)],
        compiler_params=_cparams(("arbitrary",)), name="merge_fwd_bwd")(
            x, target, yssd, ylru, ymem, proj, wbs, wbl, wbm, wo, fg)


def _adamw(w, g, m, v, name):
    rows, cols = w.shape
    tr = _tile(rows, 512, 8)

    def body(w_ref, g_ref, m_ref, v_ref, d_ref, mo_ref, vo_ref):
        gv = g_ref[...]
        mn = ADAM_B1 * m_ref[...] + (1.0 - ADAM_B1) * gv
        vn = ADAM_B2 * v_ref[...] + (1.0 - ADAM_B2) * (gv * gv)
        m_hat = mn / (1.0 - ADAM_B1 ** ADAM_STEP)
        v_hat = vn / (1.0 - ADAM_B2 ** ADAM_STEP)
        d_ref[...] = -ADAM_LR * (m_hat / (jnp.sqrt(v_hat) + ADAM_EPS) + ADAM_WD * w_ref[...])
        mo_ref[...] = mn
        vo_ref[...] = vn

    blk = pl.BlockSpec((tr, cols), lambda i: (i, 0))
    sh = jax.ShapeDtypeStruct((rows, cols), F32)
    return pl.pallas_call(
        body, grid=(rows // tr,), in_specs=[blk] * 4, out_specs=[blk] * 3, out_shape=[sh] * 3,
        compiler_params=_cparams(("parallel",)), name=name)(w, g, m, v)


def _mesh_pos():
    x, y, c = lax.axis_index("x"), lax.axis_index("y"), lax.axis_index("c")
    chips = [(1 - x, y), (x, 1 - y), (1 - x, 1 - y)]
    return x, y, c, 2 * x + y, chips


def _hbm():
    return pl.BlockSpec(memory_space=pl.ANY)


def _remote(src, dst, send_sem, recv_sem, dev):
    return pltpu.make_async_remote_copy(src_ref=src, dst_ref=dst, send_sem=send_sem, recv_sem=recv_sem,
                                        device_id=dev, device_id_type=MESH)


def _sems(n):
    return [pltpu.SemaphoreType.DMA((n,)), pltpu.SemaphoreType.DMA((n,))]


class _Exchange:
    def __init__(self, inputs, out_shape, n_sem, start, finish, aliases=None):
        self.inputs, self.out_shape, self.n_sem = list(inputs), list(out_shape), n_sem
        self.start, self.finish, self.aliases = start, finish, dict(aliases or {})


def _run_exchange(ex, name):
    n_in, n_out = len(ex.inputs), len(ex.out_shape)

    def body(*refs):
        srcs, outs = refs[:n_in], refs[n_in:n_in + n_out]
        send_sems, recv_sems = refs[n_in + n_out:]
        ex.start(srcs, outs, send_sems, recv_sems)
        ex.finish(srcs, outs, send_sems, recv_sems)

    return pl.pallas_call(
        body, in_specs=[_hbm()] * n_in, out_specs=[_hbm()] * n_out, out_shape=ex.out_shape,
        input_output_aliases=ex.aliases, scratch_shapes=_sems(ex.n_sem), name=name)(*ex.inputs)


def _pcall(body, ride, args, *, grid, in_specs, out_specs, out_shape, scratch_shapes, sem, name, aliases=None):
    in_specs, out_specs, out_shape = list(in_specs), list(out_specs), list(out_shape)
    scratch_shapes, aliases = list(scratch_shapes), dict(aliases or {})
    if ride is None:
        outs = pl.pallas_call(
            body, grid=grid, in_specs=in_specs, out_specs=out_specs, out_shape=out_shape, scratch_shapes=scratch_shapes,
            input_output_aliases=aliases, compiler_params=_cparams(sem), name=name)(*args)
        return outs, None
    n_in, n_out, n_scr = len(in_specs), len(out_shape), len(scratch_shapes)
    e_in, e_out = len(ride.inputs), len(ride.out_shape)

    def carried(*refs):
        cut = [n_in, e_in, n_out, e_out, n_scr]
        parts, p = [], 0
        for c in cut:
            parts.append(refs[p:p + c])
            p += c
        ins, e_ins, outs, e_outs, scr = parts
        send_sems, recv_sems = refs[p], refs[p + 1]
        first = last = None
        for d, size in enumerate(grid):
            i = pl.program_id(d)
            first = (i == 0) if first is None else jnp.logical_and(first, i == 0)
            last = (i == size - 1) if last is None else jnp.logical_and(last, i == size - 1)

        @pl.when(first)
        def _():
            ride.start(e_ins, e_outs, send_sems, recv_sems)

        body(*ins, *outs, *scr)

        @pl.when(last)
        def _():
            ride.finish(e_ins, e_outs, send_sems, recv_sems)

    for k, v in ride.aliases.items():
        aliases[n_in + k] = n_out + v
    res = pl.pallas_call(
        carried, grid=grid, in_specs=in_specs + [_hbm()] * e_in, out_specs=out_specs + [_hbm()] * e_out,
        out_shape=out_shape + ride.out_shape, scratch_shapes=scratch_shapes + _sems(ride.n_sem),
        input_output_aliases=aliases, compiler_params=_cparams(("arbitrary",) * len(grid)),
        name=name)(*args, *ride.inputs)
    return res[:n_out], res[n_out:]


def _gather_shards(arrs, split, relay):
    n = len(arrs)
    n_sem = sum(6 if sp else 3 for sp in split)

    def rows(i, which):
        if not split[i]:
            return pl.ds(0, arrs[i].shape[0])
        half = arrs[i].shape[0] // 2
        return pl.ds(which * half, half)

    def sends(srcs, outs, send_sems, recv_sems):
        _, _, c, me, chips = _mesh_pos()
        return [_remote(srcs[i].at[rows(i, c)], outs[i].at[me, rows(i, c)], send_sems.at[3 * i + j],
                        recv_sems.at[3 * i + j], (cx, cy, c))
                for i in range(n) for j, (cx, cy) in enumerate(chips) if not (relay and split[i] and j == 2)]

    def start(srcs, outs, send_sems, recv_sems):
        for cp in sends(srcs, outs, send_sems, recv_sems):
            cp.start()

    def finish(srcs, outs, send_sems, recv_sems):
        x, y, c, _, chips = _mesh_pos()
        sib = (x, y, 1 - c)
        first = ((x + 1 - c) % 2, (y + c) % 2)
        other = ((x + c) % 2, (y + 1 - c) % 2)
        started, k = [], 3 * n
        for i in range(n):
            sem = lambda j, i=i: (send_sems.at[3 * i + j], recv_sems.at[3 * i + j])
            if not split[i]:
                for j, (cx, cy) in enumerate(chips):
                    slot = outs[i].at[2 * cx + cy]
                    _remote(slot, slot, *sem(j), (cx, cy, c)).wait_recv()
                continue
            to_sib = lambda j, k=k: (send_sems.at[k + j], recv_sems.at[k + j])
            slot = lambda chip, which, i=i: outs[i].at[2 * chip[0] + chip[1], rows(i, which)]
            got = slot(first, c)
            _remote(got, got, *sem(c), (*first, c)).wait_recv()
            if relay:
                started.append(_remote(got, got, *sem(2), (*other, c)))
                started[-1].start()
            started.append(_remote(got, got, *to_sib(c), sib))
            started[-1].start()
            for chip, j_in, j_sib in ((other, 1 - c, 1 - c), (chips[2], 2, 2)):
                got = slot(chip, c)
                _remote(got, got, *sem(j_in), (*chip, c)).wait_recv()
                started.append(_remote(got, got, *to_sib(j_sib), sib))
                started[-1].start()
            for chip, j_sib in ((first, c), (other, 1 - c), (chips[2], 2)):
                theirs = slot(chip, 1 - c)
                _remote(theirs, theirs, *to_sib(j_sib), sib).wait_recv()
            k += 3
        for cp in sends(srcs, outs, send_sems, recv_sems) + started:
            cp.wait_send()

    return _Exchange(arrs, [jax.ShapeDtypeStruct((NSHARD,) + a.shape, a.dtype) for a in arrs], n_sem, start, finish)


def _with_own_slot(arrs, got):
    own_slot = jnp.arange(NSHARD, dtype=jnp.int32)[:, None, None] == 2 * lax.axis_index("x") + lax.axis_index("y")
    return [jnp.where(own_slot, a[None], g) for a, g in zip(arrs, got)]


def _swap_halves(arrs):
    n = len(arrs)

    def copies(srcs, outs, send_sems, recv_sems):
        x, y, c, _, _ = _mesh_pos()
        cps = []
        for i in range(n):
            half = arrs[i].shape[1] // 2
            cps.append(_remote(srcs[i].at[:, pl.ds((1 - c) * half, half)], outs[i], send_sems.at[i], recv_sems.at[i],
                               (x, y, 1 - c)))
        return cps

    def start(*refs):
        for cp in copies(*refs):
            cp.start()

    def finish(*refs):
        for cp in copies(*refs):
            cp.wait()

    shapes = [jax.ShapeDtypeStruct((NSHARD, a.shape[1] // 2, a.shape[2]), a.dtype) for a in arrs]
    return _Exchange(arrs, shapes, n, start, finish)


def _scatter_chips(arrs):
    n = len(arrs)

    def copies(srcs, outs, send_sems, recv_sems):
        _, _, c, me, chips = _mesh_pos()
        own = [pltpu.make_async_copy(srcs[i].at[me], outs[i].at[me], send_sems.at[3 * n + i]) for i in range(n)]
        cps = [_remote(srcs[i].at[2 * cx + cy], outs[i].at[me], send_sems.at[3 * i + j], recv_sems.at[3 * i + j],
                       (cx, cy, c)) for i in range(n) for j, (cx, cy) in enumerate(chips)]
        return own, cps

    def start(*refs):
        own, cps = copies(*refs)
        for cp in own + cps:
            cp.start()

    def finish(srcs, outs, send_sems, recv_sems):
        _, _, c, _, chips = _mesh_pos()
        for i in range(n):
            for j, (cx, cy) in enumerate(chips):
                slot = outs[i].at[2 * cx + cy]
                _remote(slot, slot, send_sems.at[3 * i + j], recv_sems.at[3 * i + j], (cx, cy, c)).wait_recv()
        own, cps = copies(srcs, outs, send_sems, recv_sems)
        for cp in cps:
            cp.wait_send()
        for cp in own:
            cp.wait()

    return _Exchange(arrs, [jax.ShapeDtypeStruct(a.shape, a.dtype) for a in arrs], 4 * n, start, finish)


def _share_halves(arrs):
    n = len(arrs)

    def copies(outs, send_sems, recv_sems):
        x, y, c, _, _ = _mesh_pos()
        return [_remote(outs[i].at[c], outs[i].at[c], send_sems.at[i], recv_sems.at[i], (x, y, 1 - c))
                for i in range(n)]

    def start(srcs, outs, send_sems, recv_sems):
        del srcs
        for cp in copies(outs, send_sems, recv_sems):
            cp.start()

    def finish(srcs, outs, send_sems, recv_sems):
        del srcs
        x, y, c, _, _ = _mesh_pos()
        for i in range(n):
            theirs = outs[i].at[1 - c]
            _remote(theirs, theirs, send_sems.at[i], recv_sems.at[i], (x, y, 1 - c)).wait_recv()
        for cp in copies(outs, send_sems, recv_sems):
            cp.wait_send()

    return _Exchange(arrs, [jax.ShapeDtypeStruct(a.shape, a.dtype) for a in arrs], n, start, finish,
                     aliases={i: i for i in range(n)})


def _gather_small(full):
    _, width = full.shape

    def copies(srcs, outs, send_sems, recv_sems):
        _, _, c, me, chips = _mesh_pos()
        mine = srcs[0].at[pl.ds(0, SMALL_ROWS)]
        own = pltpu.make_async_copy(mine, outs[0].at[me], send_sems.at[3])
        return own, [_remote(mine, outs[0].at[me], send_sems.at[j], recv_sems.at[j], (cx, cy, c))
                     for j, (cx, cy) in enumerate(chips)]

    def start(*refs):
        own, cps = copies(*refs)
        for cp in [own] + cps:
            cp.start()

    def finish(srcs, outs, send_sems, recv_sems):
        _, _, c, _, chips = _mesh_pos()
        for j, (cx, cy) in enumerate(chips):
            slot = outs[0].at[2 * cx + cy]
            _remote(slot, slot, send_sems.at[j], recv_sems.at[j], (cx, cy, c)).wait_recv()
        own, cps = copies(srcs, outs, send_sems, recv_sems)
        for cp in cps:
            cp.wait_send()
        own.wait()

    return _Exchange([full], [jax.ShapeDtypeStruct((NSHARD, SMALL_ROWS, width), full.dtype)], 4, start, finish)


def _add_sibling(mine, recv, c, name):
    _, half, width = recv.shape
    tr = _tile(half, 256, 8)
    nb = half // tr

    def body(c_ref, a_ref, b_ref, o_ref):
        del c_ref
        o_ref[...] = (a_ref[...].astype(F32) + b_ref[...].astype(F32)).astype(o_ref.dtype)

    grid_spec = pltpu.PrefetchScalarGridSpec(
        num_scalar_prefetch=1, grid=(NSHARD, nb),
        in_specs=[pl.BlockSpec((1, tr, width), lambda j, r, c_ref: (j, c_ref[0] * nb + r, 0)),
                  pl.BlockSpec((1, tr, width), lambda j, r, c_ref: (j, r, 0))],
        out_specs=pl.BlockSpec((1, tr, width), lambda j, r, c_ref: (j, r, 0)))
    return pl.pallas_call(
        body, grid_spec=grid_spec, out_shape=jax.ShapeDtypeStruct(recv.shape, recv.dtype),
        compiler_params=_cparams(("parallel", "parallel")), name=name)(c, mine, recv)


def _sum_chips(parts, c, name):
    _, half, width = parts.shape
    tr = _tile(half, 256, 8)

    def body(c_ref, p_ref, o_ref):
        del c_ref
        p = [p_ref[j].astype(F32) for j in range(NSHARD)]
        o_ref[0] = ((p[0] + p[1]) + p[2]) + p[3]

    grid_spec = pltpu.PrefetchScalarGridSpec(
        num_scalar_prefetch=1, grid=(half // tr,),
        in_specs=[pl.BlockSpec((NSHARD, tr, width), lambda r, c_ref: (0, r, 0))],
        out_specs=pl.BlockSpec((1, tr, width), lambda r, c_ref: (c_ref[0], r, 0)))
    return pl.pallas_call(
        body, grid_spec=grid_spec, out_shape=jax.ShapeDtypeStruct((2, half, width), F32),
        compiler_params=_cparams(("parallel",)), name=name)(c, parts)


def _unpack(flat, names, shapes):
    out, off = {}, 0
    for n in names:
        sz = _size(shapes[n])
        out[n] = flat[off:off + sz].reshape(shapes[n])
        off += sz
    return out


W_IN_COLS = 3080
W_IN_PAD = 3136


def _reorder_w_in_t(w):
    return jnp.concatenate([w[2048:5120], w[9248:12320], w[0:2048], w[8224:9248], w[5152:6688], w[6688:8224],
                            w[5120:5152], jnp.zeros((NP - 12320, D), w.dtype)], axis=0)


def _restore_w_in_t(g):
    return jnp.concatenate([g[6144:8192], g[0:3072], g[12288:12320], g[9216:10752], g[10752:12288], g[8192:9216],
                            g[3072:6144]], axis=0)


def _lru_group_weights(w):
    w4 = w.reshape(4, 4, 96, 96)
    eye = jnp.eye(4, dtype=w.dtype)
    return (w4[:, :, None, :, :] * eye[None, :, :, None, None]).transpose(0, 1, 3, 2, 4).reshape(4, LRU_G, LRU_G)


def _lru_group_blocks(g):
    g5 = g.reshape(4, 4, 96, 4, 96)
    return jnp.stack([g5[:, a, :, a, :] for a in range(4)], axis=1).reshape(16, 96, 96)


def _spread(a):
    return a.transpose(1, 0, 2).reshape(a.shape[1], NSHARD * a.shape[2])


def _split(a):
    return a.reshape(a.shape[0], NSHARD, a.shape[1] // NSHARD).transpose(1, 0, 2)


class _Reduction:
    def __init__(self, dist, parts, names):
        self.c, self.parts, self.names = dist.c, parts, names

    def swap(self):
        return _swap_halves(self.parts)

    def scatter(self, recv):
        return _scatter_chips([_add_sibling(p, r, self.c, "add_sibling_" + n)
                               for p, r, n in zip(self.parts, recv, self.names)])

    def share(self, landed):
        return _share_halves([_sum_chips(a, self.c, "sum_chips_" + n) for a, n in zip(landed, self.names)])

    def done(self, shared):
        return [a.reshape(2 * a.shape[1], a.shape[2]) for a in shared]


class _Dist:
    def __init__(self, w_in_shard, late_shards):
        self.c = lax.axis_index("c").astype(jnp.int32).reshape(1)
        self.w_in_shard = [w_in_shard]
        self.late_shards = late_shards

    def w_in_ride(self):
        return _gather_shards(self.w_in_shard, [True], relay=True)

    def w_in_arrived(self, got):
        (g_in,) = _with_own_slot(self.w_in_shard, got)
        return _reorder_w_in_t(g_in[:, 0:W_IN_COLS].reshape(NSHARD * W_IN_COLS, D))

    def weights_ride(self):
        return _gather_shards(self.late_shards, [True, True, False], relay=False)

    def weights_arrived(self, got):
        g_kv, g_rows, g_small = _with_own_slot(self.late_shards, got)
        out = {"w_kv": _spread(g_kv)}
        for n, lo_, hi_ in ROW_PIECES:
            out[n] = g_rows[:, lo_:hi_].reshape(NSHARD * (hi_ - lo_), D)
        out["ssd_conv_w"] = _spread(g_small[:, :, 0:768])
        out["ssd_norm_g"] = _spread(g_small[:, :, 768:896])
        out["lru_conv_w"] = _spread(g_small[:, :, 896:1280])
        return out

    def early_parts(self, grads):
        rows = jnp.concatenate([grads[n].reshape(NSHARD, hi_ - lo_, D) for n, lo_, hi_ in ROW_PIECES], axis=1)
        return [_split(grads["w_kv"]), rows]

    def late_parts(self, grads):
        rows = _restore_w_in_t(grads["w_in_rt"]).reshape(NSHARD, W_IN_COLS, D)
        return [jnp.pad(rows, ((0, 0), (0, W_IN_PAD - W_IN_COLS), (0, 0)))]


def _local_grads(x, mem, target, wts, dist=None):
    pad128 = lambda a: jnp.pad(a, ((0, 0), (0, 128 - a.shape[1])))

    if dist is None:
        (h,), _ = _norm_fwd(x, wts["norm_g"])
        w_in_rt = wts["w_in_rt"]
        proj = _mm(h, w_in_rt, F32, "in_proj", tb=True, tn=NP_TILE)
    else:
        (h,), arrived = _norm_fwd(x, wts["norm_g"], ride=dist.w_in_ride())
        w_in_rt = dist.w_in_arrived(arrived)
        proj, arrived = _mm(h, w_in_rt, F32, "in_proj", tb=True, tn=NP_TILE, ride=dist.weights_ride())
        wts = dict(wts, **dist.weights_arrived(arrived))
    wbs, wbl, wbm, wo, wkv = wts["w_br_ssd"], wts["w_br_lru"], wts["w_br_mem"], wts["w_out"], wts["w_kv"]
    wa, wx = _mx(_lru_group_weights(wts["lru_w_a"])), _mx(_lru_group_weights(wts["lru_w_x"]))
    ba, bx = wts["lru_b_a"].reshape(1, LRU_W), wts["lru_b_x"].reshape(1, LRU_W)
    dtb, alog = pad128(wts["ssd_dt_bias"]), pad128(wts["ssd_a_log"])
    dexp = jnp.repeat(wts["ssd_d"], 64, axis=1)
    ng = wts["ssd_norm_g"].reshape(1, SSD_W)
    xbc = _conv_fwd(proj, XBC, wts["ssd_conv_w"], wts["ssd_conv_b"], True, "ssd_conv_fwd")
    yssd, yraw, hprev = _ssd_fwd(xbc, proj, dtb, alog, dexp, ng)
    xl = _conv_fwd(proj, LX, wts["lru_conv_w"], wts["lru_conv_b"], False, "lru_conv_fwd")
    ylru, hs = _lru_fwd(xl, proj, wa, wx, ba, bx, wts["lru_lambda"])
    kk, vv, mn = _mem_kv_fwd(mem, wts["mem_norm_g"], wkv)
    ymem = _attn_fwd(proj, kk, vv)

    dproj, dx2, dx2m, merged, db0, db1, db2, loss_vec, dfg = _merge_fb(
        x, target, yssd, ylru, ymem, proj, wbs, wbl, wbm, wo, wts["final_g"].reshape(1, D))
    grads = {"final_g": dfg.reshape(D)}
    grads["w_out"] = _mm(merged, dx2m, _MXU, "dw_out", ta=True)
    grads["w_br_ssd"] = _mm(yssd, db0, _MXU, "dw_br_ssd", ta=True)
    grads["w_br_lru"] = _mm(ylru, db1, _MXU, "dw_br_lru", ta=True)
    grads["w_br_mem"] = _mm(ymem, db2, _MXU, "dw_br_mem", ta=True)
    dyssd = _mm(db0, wbs, F32, "dy_ssd", tb=True)
    dylru = _mm(db1, wbl, F32, "dy_lru", tb=True)
    dymem = _mm(db2, wbm, F32, "dy_mem", tb=True)

    dproj, dk, dv = _attn_bwd(proj, kk, vv, dymem, dproj)
    grads["w_kv"], grads["mem_norm_g"] = _mem_kv_bwd(mem, wts["mem_norm_g"], mn, wkv, dk, dv)

    early = None if dist is None else _Reduction(dist, dist.early_parts(grads), ["w_kv", "rows"])

    (dproj, dxl, dwa, dwx, dba, dbx, dlam), got = _lru_bwd(
        xl, proj, hs, dylru, dproj, wa, wx, ba, bx, wts["lru_lambda"], ride=early and early.swap())
    grads["lru_w_a"] = _lru_group_blocks(dwa)[None]
    grads["lru_w_x"] = _lru_group_blocks(dwx)[None]
    grads["lru_b_a"], grads["lru_b_x"] = dba.reshape(1, 16, 96), dbx.reshape(1, 16, 96)
    grads["lru_lambda"] = dlam
    (grads["lru_conv_w"], grads["lru_conv_b"]), _ = _conv_bwd_w(
        proj, LX, wts["lru_conv_w"], wts["lru_conv_b"], dxl, False, "lru_conv_bwd_w")
    dproj = _conv_bwd_x(dxl, wts["lru_conv_w"], dproj, LX, "lru_conv_bwd_x")

    (dproj, ddt, dxbc, dng, dda, ddd, ddtb), got = _ssd_bwd(
        xbc, proj, yraw, hprev, dyssd, dproj, dtb, alog, dexp, ng, ride=early and early.scatter(got))
    dproj = _put_block(ddt, dproj, DT, "put_ddt")
    grads["ssd_norm_g"] = dng.reshape(4, 512)
    grads["ssd_dt_bias"] = ddtb[:, 0:32]
    grads["ssd_a_log"] = (dda * -jnp.exp(alog))[:, 0:32]
    grads["ssd_d"] = ddd.reshape(32, 64).sum(axis=1)[None, :]
    (dpre, grads["ssd_conv_w"], grads["ssd_conv_b"]), got = _conv_bwd_w(
        proj, XBC, wts["ssd_conv_w"], wts["ssd_conv_b"], dxbc, True, "ssd_conv_bwd_w", ride=early and early.share(got))
    reduced = {} if dist is None else dict(zip(["w_kv", "rows"], early.done(got)))
    dproj = _conv_bwd_x(dpre, wts["ssd_conv_w"], dproj, XBC, "ssd_conv_bwd_x")

    grads["w_in_rt"] = _mm(dproj, h, _MXU, "dw_in", ta=True, tm=NP_TILE, tn=1024)
    if dist is None:
        dh = _mm(dproj, w_in_rt, F32, "dh", tn=1024, tk=NP_TILE)
        (grad_x, grads["norm_g"]), _ = _norm_bwd(x, wts["norm_g"], dh, dx2)
    else:
        late = _Reduction(dist, dist.late_parts(grads), ["w_in"])
        got = _run_exchange(late.swap(), "swap_halves_w_in")
        dh, got = _mm(dproj, w_in_rt, F32, "dh", tn=1024, tk=NP_TILE, ride=late.scatter(got))
        (grad_x, grads["norm_g"]), _ = _norm_bwd(x, wts["norm_g"], dh, dx2)
        reduced["w_in"] = late.done(_run_exchange(late.share(got), "share_halves_w_in"))[0]
    return jnp.sum(loss_vec), grad_x, grads, reduced


def kernel(x, mem, norm_g, w_in, ssd_conv_w, ssd_conv_b, ssd_dt_bias, ssd_a_log, ssd_d, ssd_norm_g, lru_conv_w, lru_conv_b, lru_w_a, lru_b_a, lru_w_x, lru_b_x, lru_lambda, mem_norm_g, w_kv, w_br_ssd, w_br_lru, w_br_mem, w_out, final_g, loss_target, m_norm_g, m_w_in, m_ssd_conv_w, m_ssd_conv_b, m_ssd_dt_bias, m_ssd_a_log, m_ssd_d, m_ssd_norm_g, m_lru_conv_w, m_lru_conv_b, m_lru_w_a, m_lru_b_a, m_lru_w_x, m_lru_b_x, m_lru_lambda, m_mem_norm_g, m_w_kv, m_w_br_ssd, m_w_br_lru, m_w_br_mem, m_w_out, m_final_g, v_norm_g, v_w_in, v_ssd_conv_w, v_ssd_conv_b, v_ssd_dt_bias, v_ssd_a_log, v_ssd_d, v_ssd_norm_g, v_lru_conv_w, v_lru_conv_b, v_lru_w_a, v_lru_b_a, v_lru_w_x, v_lru_b_x, v_lru_lambda, v_mem_norm_g, v_w_kv, v_w_br_ssd, v_w_br_lru, v_w_br_mem, v_w_out, v_final_g):
    given = dict(locals())

    rows_w = jnp.concatenate([w_br_ssd[0], w_br_lru[0], w_br_mem[0], w_out[0]], axis=0)
    small_w = jnp.concatenate([ssd_conv_w[0], ssd_norm_g[0], lru_conv_w[0]], axis=1)
    w_in_t = jnp.pad(_mx(w_in[0].T), ((0, W_IN_PAD - W_IN_COLS), (0, 0)))
    dist = _Dist(w_in_t, [_mx(w_kv[0]), _mx(rows_w), small_w])
    wts = {n: given[n] for n in REPL}
    wts["lru_w_a"], wts["lru_w_x"] = lru_w_a[0], lru_w_x[0]

    loss_part, grad_x, grads, reduced = _local_grads(x[0], mem[0], loss_target[0], wts, dist)
    loss = lax.psum(loss_part, ("x", "y", "c"))

    repl_flat = jnp.concatenate([grads[n].reshape(-1) for n in REPL])
    repl_flat = jnp.pad(repl_flat, (0, NSHARD * SMALL_Q - repl_flat.shape[0])).reshape(NSHARD, SMALL_Q)
    shard_small = jnp.concatenate([_split(grads[n]).reshape(NSHARD, -1) for n in SMALL_SHARDED], axis=1)
    p_small = jnp.concatenate(
        [repl_flat, shard_small, jnp.zeros((NSHARD, SMALL_BUF_ROWS * PACK_W - SMALL_Q - 5120), F32)], axis=1)
    small = _Reduction(dist, [p_small.reshape(NSHARD, SMALL_BUF_ROWS, PACK_W)], ["small"])
    got = _run_exchange(small.swap(), "swap_halves_small")
    got = _run_exchange(small.scatter(got), "scatter_chips_small")
    got = _run_exchange(small.share(got), "share_halves_small")
    r_small = small.done(got)[0]
    repl_all = _run_exchange(_gather_small(r_small), "gather_small")[0].reshape(-1)

    g_shard = {"w_kv": reduced["w_kv"]}
    for n, lo_, hi_ in ROW_PIECES:
        g_shard[n] = reduced["rows"][lo_:hi_]
    g_shard.update(_unpack(r_small.reshape(-1)[SMALL_Q:], SMALL_SHARDED, SHARD_SHAPE))
    g_repl = _unpack(repl_all, REPL, REPL_SHAPE)

    out_g, out_d, out_m, out_v = {}, {}, {}, {}
    for n in WEIGHTS:
        w_full = given[n]
        if n == "w_in":
            g2 = reduced["w_in"][0:W_IN_COLS]
            d, mo, vo = _adamw(w_in[0].T, g2, m_w_in[0].T, v_w_in[0].T, "adamw_w_in")
            out_g[n], out_d[n], out_m[n], out_v[n] = [a.T[None] for a in (g2, d, mo, vo)]
            continue
        g = (g_shard[n] if n in SHARDED else g_repl[n]).reshape(w_full.shape)
        cols = w_full.shape[-1]
        as2d = lambda a: a.reshape(-1, cols)
        d, mo, vo = _adamw(as2d(w_full), as2d(g), as2d(given["m_" + n]), as2d(given["v_" + n]), "adamw_" + n)
        out_g[n] = g
        out_d[n], out_m[n], out_v[n] = d.reshape(w_full.shape), mo.reshape(w_full.shape), vo.reshape(w_full.shape)

    return (loss, grad_x[None], *[out_g[n] for n in WEIGHTS], *[out_d[n] for n in WEIGHTS],
            *[out_m[n] for n in WEIGHTS], *[out_v[n] for n in WEIGHTS])
```

```python
import jax
import jax.numpy as jnp
from jax import lax
from jax.experimental import pallas as pl
from jax.experimental.pallas import tpu as pltpu

F32 = jnp.float32
_MXU = jnp.bfloat16
_HI = lax.Precision.HIGHEST
MESH = pl.DeviceIdType.MESH

D = 1024
EPS = 1e-6
MEM_HEADS = 4
MEM_HD = 256
LRU_C = 8.0
SSD_L = 128
SSD_W = 2048
LRU_W = 1536
NSHARD = 4

XBC = (0, 3072)
GL = (3072, 3072)
Z = (6144, 2048)
Q = (8192, 1024)
LG = (9216, 1536)
LX = (10752, 1536)
DT = (12288, 256)
NP = 12544
NP_TILE = 1792

ADAM_LR = 0.001
ADAM_B1 = 0.9
ADAM_B2 = 0.999
ADAM_EPS = 1e-08
ADAM_WD = 0.01
ADAM_STEP = 10

VMEM_LIMIT = 56 * 1024 * 1024

SHARDED = ("w_in", "ssd_conv_w", "ssd_norm_g", "lru_conv_w", "w_kv", "w_br_ssd", "w_br_lru", "w_br_mem", "w_out")
SHARD_SHAPE = {"w_in": (1024, 3080), "ssd_conv_w": (4, 768), "ssd_norm_g": (4, 128), "lru_conv_w": (4, 384),
               "w_kv": (1024, 512), "w_br_ssd": (512, 1024), "w_br_lru": (384, 1024), "w_br_mem": (256, 1024),
               "w_out": (256, 1024)}
REPL = ("norm_g", "ssd_conv_b", "ssd_dt_bias", "ssd_a_log", "ssd_d", "lru_conv_b", "lru_w_a", "lru_b_a",
        "lru_w_x", "lru_b_x", "lru_lambda", "mem_norm_g", "final_g")
REPL_SHAPE = {"norm_g": (1, 1024), "ssd_conv_b": (1, 3072), "ssd_dt_bias": (1, 32), "ssd_a_log": (1, 32),
              "ssd_d": (1, 32), "lru_conv_b": (1, 1536), "lru_w_a": (1, 16, 96, 96), "lru_b_a": (1, 16, 96),
              "lru_w_x": (1, 16, 96, 96), "lru_b_x": (1, 16, 96), "lru_lambda": (1, 1536),
              "mem_norm_g": (1, 1024), "final_g": (1024,)}
WEIGHTS = ("norm_g", "w_in", "ssd_conv_w", "ssd_conv_b", "ssd_dt_bias", "ssd_a_log", "ssd_d", "ssd_norm_g",
           "lru_conv_w", "lru_conv_b", "lru_w_a", "lru_b_a", "lru_w_x", "lru_b_x", "lru_lambda", "mem_norm_g",
           "w_kv", "w_br_ssd", "w_br_lru", "w_br_mem", "w_out", "final_g")

ROW_PIECES = (("w_br_ssd", 0, 512), ("w_br_lru", 512, 896), ("w_br_mem", 896, 1152), ("w_out", 1152, 1408))
SMALL_SHARDED = ("ssd_conv_w", "ssd_norm_g", "lru_conv_w")
PACK_W = 512
SMALL_ROWS = 152
SMALL_Q = SMALL_ROWS * PACK_W
SMALL_BUF_ROWS = 176


def _size(shape):
    n = 1
    for s in shape:
        n *= s
    return n


def _sigmoid(x):
    return 0.5 * jnp.tanh(0.5 * x) + 0.5


def _silu(x):
    return x * _sigmoid(x)


def _dsilu(x):
    s = _sigmoid(x)
    return s * (1.0 + x * (1.0 - s))


def _softplus(x):
    return jnp.maximum(x, 0.0) + jnp.log(1.0 + jnp.exp(-jnp.abs(x)))


def _one_minus_sq(log_a, a):
    x = 2.0 * log_a
    series = -x * (1.0 + x * (0.5 + x * (1.0 / 6.0 + x * (1.0 / 24.0))))
    return jnp.where(x > -0.03, series, 1.0 - a * a)


def _dot(a, b, precision=None):
    return jnp.dot(a, b, preferred_element_type=F32, precision=precision)


def _dot_nt(a, b):
    return lax.dot_general(a, b, (((1,), (1,)), ((), ())), preferred_element_type=F32)


def _dot_tn(a, b):
    return lax.dot_general(a, b, (((0,), (0,)), ((), ())), preferred_element_type=F32)


def _mx(a):
    return a.astype(_MXU)


def _cparams(sem):
    return pltpu.CompilerParams(dimension_semantics=sem, vmem_limit_bytes=VMEM_LIMIT)


def _tile(n, want, mult=128):
    if n <= want:
        return n
    for t in range(want - want % mult, 0, -mult):
        if n % t == 0:
            return t
    raise ValueError((n, want, mult))


def _mm(a, b, out_dtype, name, ta=False, tb=False, tm=1024, tn=1280, tk=1024, ride=None):
    k, m = a.shape if ta else a.shape[::-1]
    k2, n = b.shape[::-1] if tb else b.shape
    assert k == k2
    tm, tn, tk = _tile(m, tm), _tile(n, tn), _tile(k, tk)
    nk = k // tk
    contract = (((0 if ta else 1,), (1 if tb else 0,)), ((), ()))

    def body(a_ref, b_ref, o_ref, acc_ref):
        kk = pl.program_id(2)

        @pl.when(kk == 0)
        def _():
            acc_ref[...] = jnp.zeros_like(acc_ref)

        acc_ref[...] += lax.dot_general(a_ref[...], b_ref[...], contract, preferred_element_type=F32)

        @pl.when(kk == nk - 1)
        def _():
            o_ref[...] = acc_ref[...].astype(o_ref.dtype)

    a_spec = pl.BlockSpec((tk, tm), lambda i, j, kk: (kk, i)) if ta else pl.BlockSpec((tm, tk), lambda i, j, kk: (i, kk))
    b_spec = pl.BlockSpec((tn, tk), lambda i, j, kk: (j, kk)) if tb else pl.BlockSpec((tk, tn), lambda i, j, kk: (kk, j))
    outs, carried = _pcall(
        body, ride, (a, b), grid=(m // tm, n // tn, nk),
        in_specs=[a_spec, b_spec],
        out_specs=[pl.BlockSpec((tm, tn), lambda i, j, kk: (i, j))],
        out_shape=[jax.ShapeDtypeStruct((m, n), out_dtype)],
        scratch_shapes=[pltpu.VMEM((tm, tn), F32)],
        sem=("parallel", "parallel", "arbitrary"), name=name)
    return outs[0] if ride is None else (outs[0], carried)


def _norm_fwd(x, g, ride=None):
    s = x.shape[0]
    ts = _tile(s, 512)

    def body(x_ref, g_ref, h_ref):
        xv = x_ref[...]
        r = lax.rsqrt(jnp.mean(xv * xv, axis=-1, keepdims=True) + EPS)
        h_ref[...] = (xv * r * g_ref[...]).astype(h_ref.dtype)

    return _pcall(
        body, ride, (x, g), grid=(s // ts,),
        in_specs=[pl.BlockSpec((ts, D), lambda i: (i, 0)), pl.BlockSpec((1, D), lambda i: (0, 0))],
        out_specs=[pl.BlockSpec((ts, D), lambda i: (i, 0))],
        out_shape=[jax.ShapeDtypeStruct((s, D), _MXU)], scratch_shapes=[], sem=("parallel",), name="norm_fwd")


def _norm_bwd(x, g, dh, dx2, ride=None):
    s = x.shape[0]
    ts = _tile(s, 1024)

    def body(x_ref, g_ref, dh_ref, dx2_ref, gx_ref, dg_ref):
        @pl.when(pl.program_id(0) == 0)
        def _():
            dg_ref[...] = jnp.zeros_like(dg_ref)

        xv = x_ref[...]
        r = lax.rsqrt(jnp.mean(xv * xv, axis=-1, keepdims=True) + EPS)
        xhat = xv * r
        dh_v = dh_ref[...]
        dg_ref[...] += jnp.sum(dh_v * xhat, axis=0, keepdims=True)
        dxh = dh_v * g_ref[...]
        gx_ref[...] = dx2_ref[...] + r * (dxh - xhat * jnp.mean(dxh * xhat, axis=-1, keepdims=True))

    row = pl.BlockSpec((ts, D), lambda i: (i, 0))
    vec = pl.BlockSpec((1, D), lambda i: (0, 0))
    return _pcall(
        body, ride, (x, g, dh, dx2), grid=(s // ts,), in_specs=[row, vec, row, row], out_specs=[row, vec],
        out_shape=[jax.ShapeDtypeStruct((s, D), F32), jax.ShapeDtypeStruct((1, D), F32)],
        scratch_shapes=[], sem=("arbitrary",), name="norm_bwd")


CONV_TS = 512
CONV_RB = 16
CONV_LC = 256


def _fold8(v):
    acc = v[0:8]
    for r0 in range(8, v.shape[0], 8):
        acc = acc + v[r0:r0 + 8]
    return acc


def _conv_fwd(src, blk, w, b, act, name):
    s = src.shape[0]
    off, width = blk
    cb = off // width
    ts = _tile(s, CONV_TS)

    def body(x_ref, w_ref, b_ref, o_ref, ext_ref):
        @pl.when(pl.program_id(0) == 0)
        def _():
            ext_ref[0:8, :] = jnp.zeros((8, width), F32)

        ext_ref[8:8 + ts, :] = x_ref[...]
        for l0 in range(0, width, CONV_LC):
            ls = slice(l0, l0 + CONV_LC)
            taps = [w_ref[k:k + 1, ls] for k in range(4)]
            bias = b_ref[:, ls]
            for r0 in range(0, ts, CONV_RB):
                pre = bias
                for k in range(4):
                    pre = pre + taps[k] * ext_ref[5 + k + r0:5 + k + r0 + CONV_RB, ls]
                o_ref[r0:r0 + CONV_RB, ls] = _silu(pre) if act else pre
        ext_ref[0:8, :] = x_ref[ts - 8:ts, :]

    return pl.pallas_call(
        body, grid=(s // ts,),
        in_specs=[pl.BlockSpec((ts, width), lambda i: (i, cb)), pl.BlockSpec((4, width), lambda i: (0, 0)),
                  pl.BlockSpec((1, width), lambda i: (0, 0))],
        out_specs=pl.BlockSpec((ts, width), lambda i: (i, 0)),
        out_shape=jax.ShapeDtypeStruct((s, width), F32),
        scratch_shapes=[pltpu.VMEM((ts + 8, width), F32)],
        compiler_params=_cparams(("arbitrary",)), name=name)(src, w, b)


def _conv_bwd_w(src, blk, w, b, dout, act, name, ride=None):
    s = src.shape[0]
    off, width = blk
    cb = off // width
    ts = _tile(s, CONV_TS)

    def body(x_ref, w_ref, b_ref, do_ref, *rest):
        if act:
            dpre_ref, dw_ref, db_ref, ext_ref = rest
        else:
            dw_ref, db_ref, ext_ref = rest

        @pl.when(pl.program_id(0) == 0)
        def _():
            ext_ref[0:8, :] = jnp.zeros((8, width), F32)
            dw_ref[...] = jnp.zeros_like(dw_ref)
            db_ref[...] = jnp.zeros_like(db_ref)

        ext_ref[8:8 + ts, :] = x_ref[...]
        for l0 in range(0, width, CONV_LC):
            ls = slice(l0, l0 + CONV_LC)
            taps = [w_ref[k:k + 1, ls] for k in range(4)]
            bias = b_ref[:, ls]
            acc_b = jnp.zeros((8, CONV_LC), F32)
            acc_w = [jnp.zeros((8, CONV_LC), F32) for _ in range(4)]
            for r0 in range(0, ts, CONV_RB):
                xs = [ext_ref[5 + k + r0:5 + k + r0 + CONV_RB, ls] for k in range(4)]
                dpre = do_ref[r0:r0 + CONV_RB, ls]
                if act:
                    pre = bias
                    for k in range(4):
                        pre = pre + taps[k] * xs[k]
                    dpre = dpre * _dsilu(pre)
                    dpre_ref[r0:r0 + CONV_RB, ls] = dpre
                acc_b = acc_b + _fold8(dpre)
                for k in range(4):
                    acc_w[k] = acc_w[k] + _fold8(dpre * xs[k])
            db_ref[:, ls] += jnp.sum(acc_b, axis=0, keepdims=True)
            for k in range(4):
                dw_ref[k:k + 1, ls] += jnp.sum(acc_w[k], axis=0, keepdims=True)
        ext_ref[0:8, :] = x_ref[ts - 8:ts, :]

    row = pl.BlockSpec((ts, width), lambda i: (i, 0))
    outs = [pl.BlockSpec((4, width), lambda i: (0, 0)), pl.BlockSpec((1, width), lambda i: (0, 0))]
    shapes = [jax.ShapeDtypeStruct((4, width), F32), jax.ShapeDtypeStruct((1, width), F32)]
    if act:
        outs = [row] + outs
        shapes = [jax.ShapeDtypeStruct((s, width), F32)] + shapes
    return _pcall(
        body, ride, (src, w, b, dout), grid=(s // ts,),
        in_specs=[pl.BlockSpec((ts, width), lambda i: (i, cb)), pl.BlockSpec((4, width), lambda i: (0, 0)),
                  pl.BlockSpec((1, width), lambda i: (0, 0)), row],
        out_specs=outs, out_shape=shapes,
        scratch_shapes=[pltpu.VMEM((ts + 8, width), F32)], sem=("arbitrary",), name=name)


def _conv_bwd_x(dpre, w, dproj, blk, name):
    s = dpre.shape[0]
    off, width = blk
    cb = off // width
    ts = _tile(s, CONV_TS)
    nt = s // ts

    def body(dp_ref, w_ref, dproj_hbm, o_ref, ext_ref):
        del dproj_hbm

        @pl.when(pl.program_id(0) == 0)
        def _():
            ext_ref[ts:ts + 8, :] = jnp.zeros((8, width), F32)

        ext_ref[0:ts, :] = dp_ref[...]
        for l0 in range(0, width, CONV_LC):
            ls = slice(l0, l0 + CONV_LC)
            taps = [w_ref[k:k + 1, ls] for k in range(4)]
            for r0 in range(0, ts, CONV_RB):
                acc = taps[0] * ext_ref[3 + r0:3 + r0 + CONV_RB, ls]
                for k in range(1, 4):
                    acc = acc + taps[k] * ext_ref[3 - k + r0:3 - k + r0 + CONV_RB, ls]
                o_ref[r0:r0 + CONV_RB, ls] = acc.astype(o_ref.dtype)
        ext_ref[ts:ts + 8, :] = dp_ref[0:8, :]

    return pl.pallas_call(
        body, grid=(nt,),
        in_specs=[pl.BlockSpec((ts, width), lambda i: (nt - 1 - i, 0)), pl.BlockSpec((4, width), lambda i: (0, 0)),
                  pl.BlockSpec(memory_space=pl.ANY)],
        out_specs=pl.BlockSpec((ts, width), lambda i: (nt - 1 - i, cb)),
        out_shape=jax.ShapeDtypeStruct(dproj.shape, dproj.dtype),
        scratch_shapes=[pltpu.VMEM((ts + 8, width), F32)],
        input_output_aliases={2: 0},
        compiler_params=_cparams(("arbitrary",)), name=name)(dpre, w, dproj)


def _ssd_decay(a_cs, acst_ref, h, causal, lane_l):
    col = jnp.sum(jnp.where(lane_l == h, a_cs, 0.0), axis=1, keepdims=True)
    row = acst_ref[h:h + 1, :]
    return jnp.where(causal, jnp.exp(jnp.minimum(col - row, 0.0)), 0.0)


def _split3(x):
    hi = x.astype(jnp.bfloat16)
    rest = x - hi.astype(F32)
    mid = rest.astype(jnp.bfloat16)
    return jnp.concatenate([hi, mid, (rest - mid.astype(F32)).astype(jnp.bfloat16)], axis=1)


def _spread_matrix():
    col = jnp.arange(128, dtype=jnp.int32)[:, None]
    e64 = (col == jnp.arange(SSD_W, dtype=jnp.int32)[None, :] // 64).astype(jnp.bfloat16)
    return jnp.tile(e64, (3, 1))


def _ssd_common(dt_ref, dtb_ref, alog_ref, e64_ref, acst_ref, dtx_ref, acx_ref):
    ll = SSD_L
    dt = _softplus(dt_ref[:, 0:128] + dtb_ref[...])
    a_neg = -jnp.exp(alog_ref[...])
    ri = lax.broadcasted_iota(jnp.int32, (ll, ll), 0)
    ci = lax.broadcasted_iota(jnp.int32, (ll, ll), 1)
    causal = ri >= ci
    a_cs = _dot(causal.astype(F32), dt * a_neg, _HI)
    acst_ref[...] = a_cs.T
    both = _dot(jnp.concatenate([_split3(dt), _split3(a_cs)], axis=0), e64_ref[...])
    dtx_ref[...] = both[0:ll]
    acx_ref[...] = both[ll:2 * ll]
    lane_l = lax.broadcasted_iota(jnp.int32, (ll, 128), 1)
    return dt, a_neg, a_cs, causal, ri, lane_l, lane_l < 64


def _ssd_fwd(xbc, proj, dtb, alog, dexp, ng):
    s = xbc.shape[0]
    ll = SSD_L
    nc = s // ll
    e64 = _spread_matrix()

    def body(xbc_ref, dt_ref, z_ref, dtb_ref, alog_ref, dexp_ref, ng_ref, e64_ref,
             yssd_ref, yraw_ref, hprev_ref, ht_ref, acst_ref, dtx_ref, acx_ref):
        @pl.when(pl.program_id(0) == 0)
        def _():
            ht_ref[...] = jnp.zeros_like(ht_ref)

        hprev_ref[0] = ht_ref[...]
        _, _, a_cs, causal, _, lane_l, lo = _ssd_common(dt_ref, dtb_ref, alog_ref, e64_ref, acst_ref, dtx_ref, acx_ref)
        for g in range(4):
            bg = _mx(xbc_ref[:, 2048 + 128 * g:2176 + 128 * g])
            cg = _mx(xbc_ref[:, 2560 + 128 * g:2688 + 128 * g])
            cbm = _dot_nt(cg, bg)
            for jj in range(4):
                j = 4 * g + jj
                sl = slice(128 * j, 128 * j + 128)
                xp = xbc_ref[:, sl]
                acx = acx_ref[:, sl]
                a_last = acx_ref[ll - 1:ll, sl]
                xdt = xp * dtx_ref[:, sl]
                acc = None
                for hh in range(2):
                    dec = _ssd_decay(a_cs, acst_ref, 2 * j + hh, causal, lane_l)
                    xm = jnp.where(lo if hh == 0 else jnp.logical_not(lo), xdt, 0.0)
                    t = _dot(_mx(dec * cbm), _mx(xm))
                    acc = t if acc is None else acc + t
                ht = ht_ref[j]
                y = acc + _dot(cg, _mx(ht)) * jnp.exp(acx) + xp * dexp_ref[:, sl]
                yraw_ref[:, sl] = y
                st = _dot_tn(bg, _mx(xdt * jnp.exp(a_last - acx)))
                ht_ref[j] = ht * jnp.exp(a_last) + st
        for g in range(4):
            sl = slice(512 * g, 512 * g + 512)
            yg = yraw_ref[:, sl] * _silu(z_ref[:, sl])
            r = lax.rsqrt(jnp.mean(yg * yg, axis=-1, keepdims=True) + EPS)
            yssd_ref[:, sl] = (yg * r * ng_ref[:, sl]).astype(yssd_ref.dtype)

    vec = lambda w: pl.BlockSpec((1, w), lambda c: (0, 0))
    return pl.pallas_call(
        body, grid=(nc,),
        in_specs=[pl.BlockSpec((ll, 3072), lambda c: (c, 0)),
                  pl.BlockSpec((ll, DT[1]), lambda c: (c, DT[0] // DT[1])),
                  pl.BlockSpec((ll, Z[1]), lambda c: (c, Z[0] // Z[1])),
                  vec(128), vec(128), vec(2048), vec(2048),
                  pl.BlockSpec(e64.shape, lambda c: (0, 0))],
        out_specs=[pl.BlockSpec((ll, 2048), lambda c: (c, 0)), pl.BlockSpec((ll, 2048), lambda c: (c, 0)),
                   pl.BlockSpec((1, 16, 128, 128), lambda c: (c, 0, 0, 0))],
        out_shape=[jax.ShapeDtypeStruct((s, 2048), _MXU), jax.ShapeDtypeStruct((s, 2048), F32),
                   jax.ShapeDtypeStruct((nc, 16, 128, 128), F32)],
        scratch_shapes=[pltpu.VMEM((16, 128, 128), F32), pltpu.VMEM((128, ll), F32),
                        pltpu.VMEM((ll, 2048), F32), pltpu.VMEM((ll, 2048), F32)],
        compiler_params=_cparams(("arbitrary",)), name="ssd_fwd")(xbc, proj, proj, dtb, alog, dexp, ng, e64)


def _ssd_bwd(xbc, proj, yraw, hprev, dyssd, dproj, dtb, alog, dexp, ng, ride=None):
    s = xbc.shape[0]
    ll = SSD_L
    nc = s // ll
    e64 = _spread_matrix()

    def body(xbc_ref, dt_ref, z_ref, yraw_ref, hprev_ref, dy_ref, dproj_hbm, dtb_ref, alog_ref, dexp_ref, ng_ref,
             e64_ref,
             dz_ref, ddt_ref, dxbc_ref, dng_ref, dda_ref, ddd_ref, ddtb_ref,
             dht_ref, acst_ref, dtx_ref, acx_ref, dyr_ref, rowt_ref):
        del dproj_hbm

        @pl.when(pl.program_id(0) == 0)
        def _():
            dht_ref[...] = jnp.zeros_like(dht_ref)
            dng_ref[...] = jnp.zeros_like(dng_ref)
            dda_ref[...] = jnp.zeros_like(dda_ref)
            ddd_ref[...] = jnp.zeros_like(ddd_ref)
            ddtb_ref[...] = jnp.zeros_like(ddtb_ref)
            rowt_ref[...] = jnp.zeros_like(rowt_ref)

        for g in range(4):
            sl = slice(512 * g, 512 * g + 512)
            zz = z_ref[:, sl]
            yr = yraw_ref[:, sl]
            sz = _silu(zz)
            yg = yr * sz
            r = lax.rsqrt(jnp.mean(yg * yg, axis=-1, keepdims=True) + EPS)
            yhat = yg * r
            dyv = dy_ref[:, sl]
            dng_ref[:, sl] += jnp.sum(dyv * yhat, axis=0, keepdims=True)
            dyh = dyv * ng_ref[:, sl]
            dyg = r * (dyh - yhat * jnp.mean(dyh * yhat, axis=-1, keepdims=True))
            dz_ref[:, sl] = (dyg * yr * _dsilu(zz)).astype(dz_ref.dtype)
            dyr_ref[:, sl] = dyg * sz

        dt, a_neg, a_cs, causal, ri, lane_l, lo = _ssd_common(dt_ref, dtb_ref, alog_ref, e64_ref,
                                                              acst_ref, dtx_ref, acx_ref)
        lane_1 = lax.broadcasted_iota(jnp.int32, (1, 128), 1)
        da_col = jnp.zeros((ll, 128), F32)
        ddt_x = jnp.zeros((ll, 128), F32)
        last = jnp.zeros((1, 128), F32)
        for g in range(4):
            bg = _mx(xbc_ref[:, 2048 + 128 * g:2176 + 128 * g])
            cg = _mx(xbc_ref[:, 2560 + 128 * g:2688 + 128 * g])
            cbm = _dot_nt(cg, bg)
            dcb = jnp.zeros((ll, ll), F32)
            db_g = jnp.zeros((ll, 128), F32)
            dc_g = jnp.zeros((ll, 128), F32)
            for jj in range(4):
                j = 4 * g + jj
                sl = slice(128 * j, 128 * j + 128)
                xp = xbc_ref[:, sl]
                dtx = dtx_ref[:, sl]
                acx = acx_ref[:, sl]
                a_last = acx_ref[ll - 1:ll, sl]
                ea = jnp.exp(acx)
                dte = jnp.exp(a_last - acx)
                cd = jnp.exp(a_last)
                xdt = xp * dtx
                xdt_m = _mx(xdt)
                dy = dyr_ref[:, sl]
                ht = hprev_ref[0, j]
                dhn = dht_ref[j]
                dhn_m = _mx(dhn)
                gmat = _dot(bg, dhn_m)
                dxdt = gmat * dte
                for hh in range(2):
                    h = 2 * j + hh
                    dec = _ssd_decay(a_cs, acst_ref, h, causal, lane_l)
                    mm = dec * cbm
                    dym = _mx(jnp.where(lo if hh == 0 else jnp.logical_not(lo), dy, 0.0))
                    dxdt = dxdt + _dot_tn(_mx(mm), dym)
                    dm = _dot_nt(dym, xdt_m)
                    dcb = dcb + dm * dec
                    qq = dm * mm
                    da_col = da_col + jnp.where(lane_l == h, jnp.sum(qq, axis=1, keepdims=True), 0.0)
                    rowt_ref[h:h + 1, :] = jnp.sum(qq, axis=0, keepdims=True)
                ch = _dot(cg, _mx(ht))
                dyea = dy * ea
                dyea_m = _mx(dyea)
                xw_m = _mx(xdt * dte)
                dc_g = dc_g + _dot_nt(dyea_m, _mx(ht))
                db_g = db_g + _dot_nt(xw_m, dhn_m)
                wl = xdt * gmat * dte
                lane_a = dyea * ch - wl
                lane_b = dxdt * xp
                lane_c = jnp.sum(dhn * ht, axis=0, keepdims=True) * cd + jnp.sum(wl, axis=0, keepdims=True)
                for hh in range(2):
                    h = 2 * j + hh
                    mine = lo if hh == 0 else jnp.logical_not(lo)
                    da_col = da_col + jnp.where(
                        lane_l == h, jnp.sum(jnp.where(mine, lane_a, 0.0), axis=1, keepdims=True), 0.0)
                    ddt_x = ddt_x + jnp.where(
                        lane_l == h, jnp.sum(jnp.where(mine, lane_b, 0.0), axis=1, keepdims=True), 0.0)
                    mine_1 = (lane_1 < 64) if hh == 0 else (lane_1 >= 64)
                    last = last + jnp.where(
                        lane_1 == h, jnp.sum(jnp.where(mine_1, lane_c, 0.0), axis=1, keepdims=True), 0.0)
                dht_ref[j] = dhn * cd + _dot_tn(cg, dyea_m)
                dxbc_ref[:, sl] = dxdt * dtx + dy * dexp_ref[:, sl]
                ddd_ref[:, sl] += jnp.sum(dy * xp, axis=0, keepdims=True)
            dcb_m = _mx(dcb)
            dxbc_ref[:, 2048 + 128 * g:2176 + 128 * g] = db_g + _dot_tn(dcb_m, cg)
            dxbc_ref[:, 2560 + 128 * g:2688 + 128 * g] = dc_g + _dot(dcb_m, bg)

        da_cs = da_col - rowt_ref[...].T
        da_cs = da_cs + jnp.where(lax.broadcasted_iota(jnp.int32, (ll, 128), 0) == ll - 1, last, 0.0)
        d_dta = _dot((ri <= lax.broadcasted_iota(jnp.int32, (ll, ll), 1)).astype(F32), da_cs, _HI)
        ddt = d_dta * a_neg + ddt_x
        dda_ref[...] += jnp.sum(d_dta * dt, axis=0, keepdims=True)
        ddt_raw = ddt * _sigmoid(dt_ref[:, 0:128] + dtb_ref[...])
        ddtb_ref[...] += jnp.sum(ddt_raw, axis=0, keepdims=True)
        ddt_ref[:, 0:128] = ddt_raw.astype(ddt_ref.dtype)
        ddt_ref[:, 128:DT[1]] = jnp.zeros((ll, DT[1] - 128), ddt_ref.dtype)

    rev = lambda c: nc - 1 - c
    vec = lambda w: pl.BlockSpec((1, w), lambda c: (0, 0))
    row = lambda w: pl.BlockSpec((ll, w), lambda c: (rev(c), 0))
    return _pcall(
        body, ride, (xbc, proj, proj, yraw, hprev, dyssd, dproj, dtb, alog, dexp, ng, e64), grid=(nc,),
        in_specs=[row(3072),
                  pl.BlockSpec((ll, DT[1]), lambda c: (rev(c), DT[0] // DT[1])),
                  pl.BlockSpec((ll, Z[1]), lambda c: (rev(c), Z[0] // Z[1])),
                  row(2048),
                  pl.BlockSpec((1, 16, 128, 128), lambda c: (rev(c), 0, 0, 0)),
                  row(2048),
                  pl.BlockSpec(memory_space=pl.ANY),
                  vec(128), vec(128), vec(2048), vec(2048),
                  pl.BlockSpec(e64.shape, lambda c: (0, 0))],
        out_specs=[pl.BlockSpec((ll, Z[1]), lambda c: (rev(c), Z[0] // Z[1])),
                   row(DT[1]),
                   row(3072), vec(2048), vec(128), vec(2048), vec(128)],
        out_shape=[jax.ShapeDtypeStruct(dproj.shape, dproj.dtype), jax.ShapeDtypeStruct((s, DT[1]), dproj.dtype),
                   jax.ShapeDtypeStruct((s, 3072), F32), jax.ShapeDtypeStruct((1, 2048), F32),
                   jax.ShapeDtypeStruct((1, 128), F32), jax.ShapeDtypeStruct((1, 2048), F32),
                   jax.ShapeDtypeStruct((1, 128), F32)],
        scratch_shapes=[pltpu.VMEM((16, 128, 128), F32), pltpu.VMEM((128, ll), F32),
                        pltpu.VMEM((ll, 2048), F32), pltpu.VMEM((ll, 2048), F32), pltpu.VMEM((ll, 2048), F32),
                        pltpu.VMEM((128, ll), F32)],
        aliases={6: 0}, sem=("arbitrary",), name="ssd_bwd")


def _put_block(src, dproj, blk, name):
    s = src.shape[0]
    off, width = blk
    cb = off // width
    ts = _tile(s, 1024)

    def body(s_ref, dproj_hbm, o_ref):
        del dproj_hbm
        o_ref[...] = s_ref[...]

    return pl.pallas_call(
        body, grid=(s // ts,),
        in_specs=[pl.BlockSpec((ts, width), lambda i: (i, 0)), pl.BlockSpec(memory_space=pl.ANY)],
        out_specs=pl.BlockSpec((ts, width), lambda i: (i, cb)),
        out_shape=jax.ShapeDtypeStruct(dproj.shape, dproj.dtype),
        input_output_aliases={1: 0},
        compiler_params=_cparams(("parallel",)), name=name)(src, dproj)


LRU_G = 384


def _lru_gates(xl_ref, wa_ref, wx_ref, ba_ref, bx_ref, lam_ref, g):
    sl = slice(LRU_G * g, LRU_G * g + LRU_G)
    xg = xl_ref[:, sl]
    xm = _mx(xg)
    pa = _dot(xm, wa_ref[g]) + ba_ref[:, sl]
    r = jnp.where(pa < -12.0, jnp.exp(pa), _sigmoid(pa))
    ig = _sigmoid(_dot(xm, wx_ref[g]) + bx_ref[:, sl])
    sp = _softplus(-lam_ref[:, sl])
    log_a = (-LRU_C * r) * sp
    a = jnp.exp(log_a)
    mult = jnp.sqrt(_one_minus_sq(log_a, a))
    return sl, xg, r, ig, sp, a, mult


def _lru_fwd(xl, proj, wa, wx, ba, bx, lam):
    s = xl.shape[0]
    ts = _tile(s, 512)
    w = LRU_W

    def body(xl_ref, lg_ref, wa_ref, wx_ref, ba_ref, bx_ref, lam_ref, y_ref, hs_ref, a_ref, u_ref, carry_ref):
        @pl.when(pl.program_id(0) == 0)
        def _():
            carry_ref[...] = jnp.zeros_like(carry_ref)

        for g in range(4):
            sl, xg, _, ig, _, a, mult = _lru_gates(xl_ref, wa_ref, wx_ref, ba_ref, bx_ref, lam_ref, g)
            a_ref[:, sl] = a
            u_ref[:, sl] = mult * (ig * xg)

        def step(t, h):
            h = a_ref[pl.ds(t, 1), :] * h + u_ref[pl.ds(t, 1), :]
            hs_ref[pl.ds(t, 1), :] = h
            return h

        carry_ref[0:1, :] = lax.fori_loop(0, ts, step, carry_ref[0:1, :], unroll=8)
        y_ref[...] = (hs_ref[...] * _silu(lg_ref[...])).astype(y_ref.dtype)

    row = pl.BlockSpec((ts, w), lambda i: (i, 0))
    vec = pl.BlockSpec((1, w), lambda i: (0, 0))
    wsp = pl.BlockSpec((4, LRU_G, LRU_G), lambda i: (0, 0, 0))
    return pl.pallas_call(
        body, grid=(s // ts,),
        in_specs=[row, pl.BlockSpec((ts, w), lambda i: (i, LG[0] // w)), wsp, wsp, vec, vec, vec],
        out_specs=[row, row],
        out_shape=[jax.ShapeDtypeStruct((s, w), _MXU), jax.ShapeDtypeStruct((s, w), F32)],
        scratch_shapes=[pltpu.VMEM((ts, w), F32), pltpu.VMEM((ts, w), F32), pltpu.VMEM((8, w), F32)],
        compiler_params=_cparams(("arbitrary",)), name="lru_fwd")(xl, proj, wa, wx, ba, bx, lam)


def _lru_bwd(xl, proj, hs, dy, dproj, wa, wx, ba, bx, lam, ride=None):
    s = xl.shape[0]
    ts = _tile(s, 256)
    nt = s // ts
    w = LRU_W
    hb = ts // 8

    def body(xl_ref, lg_ref, hs_ref, hprev_ref, dy_ref, dproj_hbm, wa_ref, wx_ref, ba_ref, bx_ref, lam_ref,
             dlg_ref, dxl_ref, dwa_ref, dwx_ref, dba_ref, dbx_ref, dlam_ref,
             a_ref, dh_ref, ext_ref, carry_ref, r_ref, ig_ref, mult_ref):
        del dproj_hbm
        i = pl.program_id(0)

        @pl.when(i == 0)
        def _():
            carry_ref[...] = jnp.zeros_like(carry_ref)
            for ref in (dwa_ref, dwx_ref, dba_ref, dbx_ref, dlam_ref):
                ref[...] = jnp.zeros_like(ref)

        lg = lg_ref[...]
        dyv = dy_ref[...]
        dh_ref[...] = dyv * _silu(lg)
        dlg_ref[...] = (dyv * hs_ref[...] * _dsilu(lg)).astype(dlg_ref.dtype)
        for g in range(4):
            sl, _, r, ig, _, a, mult = _lru_gates(xl_ref, wa_ref, wx_ref, ba_ref, bx_ref, lam_ref, g)
            a_ref[:, sl] = a
            r_ref[:, sl] = r
            ig_ref[:, sl] = ig
            mult_ref[:, sl] = mult

        def step(k, carry):
            t = ts - 1 - k
            dh = dh_ref[pl.ds(t, 1), :] + carry
            dh_ref[pl.ds(t, 1), :] = dh
            return a_ref[pl.ds(t, 1), :] * dh

        carry_ref[0:1, :] = lax.fori_loop(0, ts, step, carry_ref[0:1, :], unroll=8)

        ext_ref[0:8, :] = jnp.where(i == nt - 1, 0.0, 1.0) * hprev_ref[...]
        ext_ref[8:8 + ts, :] = hs_ref[...]
        for g in range(4):
            sl = slice(LRU_G * g, LRU_G * g + LRU_G)
            xg, r, ig, a, mult = xl_ref[:, sl], r_ref[:, sl], ig_ref[:, sl], a_ref[:, sl], mult_ref[:, sl]
            sp = _softplus(-lam_ref[:, sl])
            dh = dh_ref[:, sl]
            da = dh * ext_ref[7:7 + ts, sl]
            dmult = dh * ig * xg
            di = dh * mult * xg
            dxl = dh * mult * ig
            dlog_a = da * a - dmult * (a * a) / mult
            dlam_ref[:, sl] += jnp.sum(dlog_a * r, axis=0, keepdims=True) * (LRU_C * _sigmoid(-lam_ref[:, sl]))
            dpa = dlog_a * (-LRU_C * sp) * r * (1.0 - r)
            dpx = di * ig * (1.0 - ig)
            dba_ref[:, sl] += jnp.sum(dpa, axis=0, keepdims=True)
            dbx_ref[:, sl] += jnp.sum(dpx, axis=0, keepdims=True)
            dpa_m, dpx_m, xm = _mx(dpa), _mx(dpx), _mx(xg)
            dxl_ref[:, sl] = dxl + _dot_nt(dpa_m, wa_ref[g]) + _dot_nt(dpx_m, wx_ref[g])
            dwa_ref[g] += _dot_tn(xm, dpa_m)
            dwx_ref[g] += _dot_tn(xm, dpx_m)

    rev = lambda i: nt - 1 - i
    row = pl.BlockSpec((ts, w), lambda i: (rev(i), 0))
    vec = pl.BlockSpec((1, w), lambda i: (0, 0))
    wsp = pl.BlockSpec((4, LRU_G, LRU_G), lambda i: (0, 0, 0))
    lgs = pl.BlockSpec((ts, w), lambda i: (rev(i), LG[0] // w))
    return _pcall(
        body, ride, (xl, proj, hs, hs, dy, dproj, wa, wx, ba, bx, lam), grid=(nt,),
        in_specs=[row, lgs, row, pl.BlockSpec((8, w), lambda i: (jnp.maximum(rev(i) * hb - 1, 0), 0)), row,
                  pl.BlockSpec(memory_space=pl.ANY), wsp, wsp, vec, vec, vec],
        out_specs=[lgs, row, wsp, wsp, vec, vec, vec],
        out_shape=[jax.ShapeDtypeStruct(dproj.shape, dproj.dtype), jax.ShapeDtypeStruct((s, w), F32),
                   jax.ShapeDtypeStruct((4, LRU_G, LRU_G), F32), jax.ShapeDtypeStruct((4, LRU_G, LRU_G), F32),
                   jax.ShapeDtypeStruct((1, w), F32), jax.ShapeDtypeStruct((1, w), F32),
                   jax.ShapeDtypeStruct((1, w), F32)],
        scratch_shapes=[pltpu.VMEM((ts, w), F32), pltpu.VMEM((ts, w), F32), pltpu.VMEM((ts + 8, w), F32),
                        pltpu.VMEM((8, w), F32), pltpu.VMEM((ts, w), F32), pltpu.VMEM((ts, w), F32),
                        pltpu.VMEM((ts, w), F32)],
        aliases={5: 0}, sem=("arbitrary",), name="lru_bwd")


def _mem_kv_fwd(mem, g, wkv):
    m = mem.shape[0]

    def body(mem_ref, g_ref, w_ref, k_ref, v_ref, mn_ref):
        mv = mem_ref[...]
        r = lax.rsqrt(jnp.mean(mv * mv, axis=-1, keepdims=True) + EPS)
        mn = _mx(mv * r * g_ref[...])
        mn_ref[...] = mn
        kv = _dot(mn, w_ref[...])
        k_ref[...] = kv[:, 0:D].astype(k_ref.dtype)
        v_ref[...] = kv[:, D:2 * D].astype(v_ref.dtype)

    sh = jax.ShapeDtypeStruct((m, D), _MXU)
    return pl.pallas_call(body, out_shape=[sh, sh, sh], compiler_params=_cparams(None), name="mem_kv_fwd")(mem, g, wkv)


def _mem_kv_bwd(mem, g, mn, wkv, dk, dv):
    m = mem.shape[0]

    def body(mem_ref, g_ref, mn_ref, w_ref, dk_ref, dv_ref, dw_ref, dg_ref):
        dkv = _mx(jnp.concatenate([dk_ref[...], dv_ref[...]], axis=1))
        dw_ref[...] = _dot_tn(mn_ref[...], dkv).astype(dw_ref.dtype)
        dmn = _dot_nt(dkv, w_ref[...])
        mv = mem_ref[...]
        r = lax.rsqrt(jnp.mean(mv * mv, axis=-1, keepdims=True) + EPS)
        dg_ref[...] = jnp.sum(dmn * mv * r, axis=0, keepdims=True)

    del m
    return pl.pallas_call(
        body, out_shape=[jax.ShapeDtypeStruct((D, 2 * D), _MXU), jax.ShapeDtypeStruct((1, D), F32)],
        compiler_params=_cparams(None), name="mem_kv_bwd")(mem, g, mn, wkv, dk, dv)


def _attn_probs(q_ref, k_ref, hd):
    sl = slice(MEM_HD * hd, MEM_HD * hd + MEM_HD)
    qh = _mx(q_ref[:, sl])
    sc = _dot_nt(qh, k_ref[:, sl]) * (MEM_HD ** -0.5)
    e = jnp.exp(sc - jnp.max(sc, axis=-1, keepdims=True))
    return sl, qh, e / jnp.sum(e, axis=-1, keepdims=True)


def _attn_fwd(proj, k, v):
    s = proj.shape[0]
    m = k.shape[0]
    ts = _tile(s, 1024)

    def body(q_ref, k_ref, v_ref, y_ref):
        for hd in range(MEM_HEADS):
            sl, _, p = _attn_probs(q_ref, k_ref, hd)
            y_ref[:, sl] = _dot(_mx(p), v_ref[:, sl]).astype(y_ref.dtype)

    kvs = pl.BlockSpec((m, D), lambda i: (0, 0))
    return pl.pallas_call(
        body, grid=(s // ts,),
        in_specs=[pl.BlockSpec((ts, D), lambda i: (i, Q[0] // D)), kvs, kvs],
        out_specs=pl.BlockSpec((ts, D), lambda i: (i, 0)),
        out_shape=jax.ShapeDtypeStruct((s, D), _MXU),
        compiler_params=_cparams(("parallel",)), name="attn_fwd")(proj, k, v)


def _attn_bwd(proj, k, v, dy, dproj):
    s = proj.shape[0]
    m = k.shape[0]
    ts = _tile(s, 1024)

    def body(q_ref, k_ref, v_ref, dy_ref, dproj_hbm, dq_ref, dk_ref, dv_ref):
        del dproj_hbm

        @pl.when(pl.program_id(0) == 0)
        def _():
            dk_ref[...] = jnp.zeros_like(dk_ref)
            dv_ref[...] = jnp.zeros_like(dv_ref)

        for hd in range(MEM_HEADS):
            sl, qh, p = _attn_probs(q_ref, k_ref, hd)
            dyh = _mx(dy_ref[:, sl])
            dp = _dot_nt(dyh, v_ref[:, sl])
            ds = _mx(p * (dp - jnp.sum(dp * p, axis=-1, keepdims=True)) * (MEM_HD ** -0.5))
            dq_ref[:, sl] = _dot(ds, k_ref[:, sl]).astype(dq_ref.dtype)
            dk_ref[:, sl] += _dot_tn(ds, qh)
            dv_ref[:, sl] += _dot_tn(_mx(p), dyh)

    kvs = pl.BlockSpec((m, D), lambda i: (0, 0))
    qs = pl.BlockSpec((ts, D), lambda i: (i, Q[0] // D))
    return pl.pallas_call(
        body, grid=(s // ts,),
        in_specs=[qs, kvs, kvs, pl.BlockSpec((ts, D), lambda i: (i, 0)), pl.BlockSpec(memory_space=pl.ANY)],
        out_specs=[qs, kvs, kvs],
        out_shape=[jax.ShapeDtypeStruct(dproj.shape, dproj.dtype), jax.ShapeDtypeStruct((m, D), F32),
                   jax.ShapeDtypeStruct((m, D), F32)],
        input_output_aliases={4: 0},
        compiler_params=_cparams(("arbitrary",)), name="attn_bwd")(proj, k, v, dy, dproj)


def _merge_fb(x, target, yssd, ylru, ymem, proj, wbs, wbl, wbm, wo, fg):
    s = x.shape[0]
    ts = _tile(s, 256)

    def body(x_ref, t_ref, ys_ref, yl_ref, ym_ref, gl_ref, wbs_ref, wbl_ref, wbm_ref, wo_ref, fg_ref,
             dgl_ref, dx2_ref, dx2m_ref, mg_ref, db0_ref, db1_ref, db2_ref, loss_ref, dfg_ref):
        @pl.when(pl.program_id(0) == 0)
        def _():
            loss_ref[...] = jnp.zeros_like(loss_ref)
            dfg_ref[...] = jnp.zeros_like(dfg_ref)

        bs = (_dot(ys_ref[...], wbs_ref[...]), _dot(yl_ref[...], wbl_ref[...]), _dot(ym_ref[...], wbm_ref[...]))
        gates = [_sigmoid(gl_ref[:, D * n:D * n + D]) for n in range(3)]
        merged = gates[0] * bs[0] + gates[1] * bs[1] + gates[2] * bs[2]
        mg = _mx(merged)
        mg_ref[...] = mg
        x2 = x_ref[...] + _dot(mg, wo_ref[...])
        r = lax.rsqrt(jnp.mean(x2 * x2, axis=-1, keepdims=True) + EPS)
        xhat = x2 * r
        err = xhat * fg_ref[...] - t_ref[...]
        loss_ref[...] += jnp.sum(err * err, axis=0, keepdims=True) * (0.5 / D)
        dy = err * (1.0 / D)
        dfg_ref[...] += jnp.sum(dy * xhat, axis=0, keepdims=True)
        dxh = dy * fg_ref[...]
        dx2 = r * (dxh - xhat * jnp.mean(dxh * xhat, axis=-1, keepdims=True))
        dx2_ref[...] = dx2
        dx2m = _mx(dx2)
        dx2m_ref[...] = dx2m
        dmg = _dot_nt(dx2m, wo_ref[...])
        for n, db_ref in enumerate((db0_ref, db1_ref, db2_ref)):
            gt = gates[n]
            dgl_ref[:, D * n:D * n + D] = (dmg * bs[n] * gt * (1.0 - gt)).astype(dgl_ref.dtype)
            db_ref[...] = (dmg * gt).astype(db_ref.dtype)

    row = lambda w: pl.BlockSpec((ts, w), lambda i: (i, 0))
    full = lambda a: pl.BlockSpec(a.shape, lambda i: (0, 0))
    vec = pl.BlockSpec((1, D), lambda i: (0, 0))
    gls = pl.BlockSpec((ts, GL[1]), lambda i: (i, GL[0] // GL[1]))
    act = jax.ShapeDtypeStruct((s, D), _MXU)
    return pl.pallas_call(
        body, grid=(s // ts,),
        in_specs=[row(D), row(D), row(SSD_W), row(LRU_W), row(D), gls, full(wbs), full(wbl), full(wbm), full(wo), vec],
        out_specs=[gls, row(D), row(D), row(D), row(D), row(D), row(D), vec, vec],
        out_shape=[jax.ShapeDtypeStruct((s, NP), _MXU), jax.ShapeDtypeStruct((s, D), F32), act, act, act, act, act,
                   jax.ShapeDtypeStruct((1, D), F32), jax.ShapeDtypeStruct((1, D), F32)],
        compiler_params=_cparams(("arbitrary",)), name="merge_fwd_bwd")(
            x, target, yssd, ylru, ymem, proj, wbs, wbl, wbm, wo, fg)


def _adamw(w, g, m, v, name):
    rows, cols = w.shape
    tr = _tile(rows, 512, 8)

    def body(w_ref, g_ref, m_ref, v_ref, d_ref, mo_ref, vo_ref):
        gv = g_ref[...]
        mn = ADAM_B1 * m_ref[...] + (1.0 - ADAM_B1) * gv
        vn = ADAM_B2 * v_ref[...] + (1.0 - ADAM_B2) * (gv * gv)
        m_hat = mn / (1.0 - ADAM_B1 ** ADAM_STEP)
        v_hat = vn / (1.0 - ADAM_B2 ** ADAM_STEP)
        d_ref[...] = -ADAM_LR * (m_hat / (jnp.sqrt(v_hat) + ADAM_EPS) + ADAM_WD * w_ref[...])
        mo_ref[...] = mn
        vo_ref[...] = vn

    blk = pl.BlockSpec((tr, cols), lambda i: (i, 0))
    sh = jax.ShapeDtypeStruct((rows, cols), F32)
    return pl.pallas_call(
        body, grid=(rows // tr,), in_specs=[blk] * 4, out_specs=[blk] * 3, out_shape=[sh] * 3,
        compiler_params=_cparams(("parallel",)), name=name)(w, g, m, v)


def _mesh_pos():
    x, y, c = lax.axis_index("x"), lax.axis_index("y"), lax.axis_index("c")
    chips = [(1 - x, y), (x, 1 - y), (1 - x, 1 - y)]
    return x, y, c, 2 * x + y, chips


def _hbm():
    return pl.BlockSpec(memory_space=pl.ANY)


def _remote(src, dst, send_sem, recv_sem, dev):
    return pltpu.make_async_remote_copy(src_ref=src, dst_ref=dst, send_sem=send_sem, recv_sem=recv_sem,
                                        device_id=dev, device_id_type=MESH)


def _sems(n):
    return [pltpu.SemaphoreType.DMA((n,)), pltpu.SemaphoreType.DMA((n,))]


class _Exchange:
    def __init__(self, inputs, out_shape, n_sem, start, finish, aliases=None):
        self.inputs, self.out_shape, self.n_sem = list(inputs), list(out_shape), n_sem
        self.start, self.finish, self.aliases = start, finish, dict(aliases or {})


def _run_exchange(ex, name):
    n_in, n_out = len(ex.inputs), len(ex.out_shape)

    def body(*refs):
        srcs, outs = refs[:n_in], refs[n_in:n_in + n_out]
        send_sems, recv_sems = refs[n_in + n_out:]
        ex.start(srcs, outs, send_sems, recv_sems)
        ex.finish(srcs, outs, send_sems, recv_sems)

    return pl.pallas_call(
        body, in_specs=[_hbm()] * n_in, out_specs=[_hbm()] * n_out, out_shape=ex.out_shape,
        input_output_aliases=ex.aliases, scratch_shapes=_sems(ex.n_sem), name=name)(*ex.inputs)


def _pcall(body, ride, args, *, grid, in_specs, out_specs, out_shape, scratch_shapes, sem, name, aliases=None):
    in_specs, out_specs, out_shape = list(in_specs), list(out_specs), list(out_shape)
    scratch_shapes, aliases = list(scratch_shapes), dict(aliases or {})
    if ride is None:
        outs = pl.pallas_call(
            body, grid=grid, in_specs=in_specs, out_specs=out_specs, out_shape=out_shape, scratch_shapes=scratch_shapes,
            input_output_aliases=aliases, compiler_params=_cparams(sem), name=name)(*args)
        return outs, None
    n_in, n_out, n_scr = len(in_specs), len(out_shape), len(scratch_shapes)
    e_in, e_out = len(ride.inputs), len(ride.out_shape)

    def carried(*refs):
        cut = [n_in, e_in, n_out, e_out, n_scr]
        parts, p = [], 0
        for c in cut:
            parts.append(refs[p:p + c])
            p += c
        ins, e_ins, outs, e_outs, scr = parts
        send_sems, recv_sems = refs[p], refs[p + 1]
        first = last = None
        for d, size in enumerate(grid):
            i = pl.program_id(d)
            first = (i == 0) if first is None else jnp.logical_and(first, i == 0)
            last = (i == size - 1) if last is None else jnp.logical_and(last, i == size - 1)

        @pl.when(first)
        def _():
            ride.start(e_ins, e_outs, send_sems, recv_sems)

        body(*ins, *outs, *scr)

        @pl.when(last)
        def _():
            ride.finish(e_ins, e_outs, send_sems, recv_sems)

    for k, v in ride.aliases.items():
        aliases[n_in + k] = n_out + v
    res = pl.pallas_call(
        carried, grid=grid, in_specs=in_specs + [_hbm()] * e_in, out_specs=out_specs + [_hbm()] * e_out,
        out_shape=out_shape + ride.out_shape, scratch_shapes=scratch_shapes + _sems(ride.n_sem),
        input_output_aliases=aliases, compiler_params=_cparams(("arbitrary",) * len(grid)),
        name=name)(*args, *ride.inputs)
    return res[:n_out], res[n_out:]


def _gather_shards(arrs, split, relay):
    n = len(arrs)
    n_sem = sum(6 if sp else 3 for sp in split)

    def rows(i, which):
        if not split[i]:
            return pl.ds(0, arrs[i].shape[0])
        half = arrs[i].shape[0] // 2
        return pl.ds(which * half, half)

    def sends(srcs, outs, send_sems, recv_sems):
        _, _, c, me, chips = _mesh_pos()
        return [_remote(srcs[i].at[rows(i, c)], outs[i].at[me, rows(i, c)], send_sems.at[3 * i + j],
                        recv_sems.at[3 * i + j], (cx, cy, c))
                for i in range(n) for j, (cx, cy) in enumerate(chips) if not (relay and split[i] and j == 2)]

    def start(srcs, outs, send_sems, recv_sems):
        for cp in sends(srcs, outs, send_sems, recv_sems):
            cp.start()

    def finish(srcs, outs, send_sems, recv_sems):
        x, y, c, _, chips = _mesh_pos()
        sib = (x, y, 1 - c)
        first = ((x + 1 - c) % 2, (y + c) % 2)
        other = ((x + c) % 2, (y + 1 - c) % 2)
        started, k = [], 3 * n
        for i in range(n):
            sem = lambda j, i=i: (send_sems.at[3 * i + j], recv_sems.at[3 * i + j])
            if not split[i]:
                for j, (cx, cy) in enumerate(chips):
                    slot = outs[i].at[2 * cx + cy]
                    _remote(slot, slot, *sem(j), (cx, cy, c)).wait_recv()
                continue
            to_sib = lambda j, k=k: (send_sems.at[k + j], recv_sems.at[k + j])
            slot = lambda chip, which, i=i: outs[i].at[2 * chip[0] + chip[1], rows(i, which)]
            got = slot(first, c)
            _remote(got, got, *sem(c), (*first, c)).wait_recv()
            if relay:
                started.append(_remote(got, got, *sem(2), (*other, c)))
                started[-1].start()
            started.append(_remote(got, got, *to_sib(c), sib))
            started[-1].start()
            for chip, j_in, j_sib in ((other, 1 - c, 1 - c), (chips[2], 2, 2)):
                got = slot(chip, c)
                _remote(got, got, *sem(j_in), (*chip, c)).wait_recv()
                started.append(_remote(got, got, *to_sib(j_sib), sib))
                started[-1].start()
            for chip, j_sib in ((first, c), (other, 1 - c), (chips[2], 2)):
                theirs = slot(chip, 1 - c)
                _remote(theirs, theirs, *to_sib(j_sib), sib).wait_recv()
            k += 3
        for cp in sends(srcs, outs, send_sems, recv_sems) + started:
            cp.wait_send()

    return _Exchange(arrs, [jax.ShapeDtypeStruct((NSHARD,) + a.shape, a.dtype) for a in arrs], n_sem, start, finish)


def _with_own_slot(arrs, got):
    own_slot = jnp.arange(NSHARD, dtype=jnp.int32)[:, None, None] == 2 * lax.axis_index("x") + lax.axis_index("y")
    return [jnp.where(own_slot, a[None], g) for a, g in zip(arrs, got)]


def _swap_halves(arrs):
    n = len(arrs)

    def copies(srcs, outs, send_sems, recv_sems):
        x, y, c, _, _ = _mesh_pos()
        cps = []
        for i in range(n):
            half = arrs[i].shape[1] // 2
            cps.append(_remote(srcs[i].at[:, pl.ds((1 - c) * half, half)], outs[i], send_sems.at[i], recv_sems.at[i],
                               (x, y, 1 - c)))
        return cps

    def start(*refs):
        for cp in copies(*refs):
            cp.start()

    def finish(*refs):
        for cp in copies(*refs):
            cp.wait()

    shapes = [jax.ShapeDtypeStruct((NSHARD, a.shape[1] // 2, a.shape[2]), a.dtype) for a in arrs]
    return _Exchange(arrs, shapes, n, start, finish)


def _scatter_chips(arrs):
    n = len(arrs)

    def copies(srcs, outs, send_sems, recv_sems):
        _, _, c, me, chips = _mesh_pos()
        own = [pltpu.make_async_copy(srcs[i].at[me], outs[i].at[me], send_sems.at[3 * n + i]) for i in range(n)]
        cps = [_remote(srcs[i].at[2 * cx + cy], outs[i].at[me], send_sems.at[3 * i + j], recv_sems.at[3 * i + j],
                       (cx, cy, c)) for i in range(n) for j, (cx, cy) in enumerate(chips)]
        return own, cps

    def start(*refs):
        own, cps = copies(*refs)
        for cp in own + cps:
            cp.start()

    def finish(srcs, outs, send_sems, recv_sems):
        _, _, c, _, chips = _mesh_pos()
        for i in range(n):
            for j, (cx, cy) in enumerate(chips):
                slot = outs[i].at[2 * cx + cy]
                _remote(slot, slot, send_sems.at[3 * i + j], recv_sems.at[3 * i + j], (cx, cy, c)).wait_recv()
        own, cps = copies(srcs, outs, send_sems, recv_sems)
        for cp in cps:
            cp.wait_send()
        for cp in own:
            cp.wait()

    return _Exchange(arrs, [jax.ShapeDtypeStruct(a.shape, a.dtype) for a in arrs], 4 * n, start, finish)


def _share_halves(arrs):
    n = len(arrs)

    def copies(outs, send_sems, recv_sems):
        x, y, c, _, _ = _mesh_pos()
        return [_remote(outs[i].at[c], outs[i].at[c], send_sems.at[i], recv_sems.at[i], (x, y, 1 - c))
                for i in range(n)]

    def start(srcs, outs, send_sems, recv_sems):
        del srcs
        for cp in copies(outs, send_sems, recv_sems):
            cp.start()

    def finish(srcs, outs, send_sems, recv_sems):
        del srcs
        x, y, c, _, _ = _mesh_pos()
        for i in range(n):
            theirs = outs[i].at[1 - c]
            _remote(theirs, theirs, send_sems.at[i], recv_sems.at[i], (x, y, 1 - c)).wait_recv()
        for cp in copies(outs, send_sems, recv_sems):
            cp.wait_send()

    return _Exchange(arrs, [jax.ShapeDtypeStruct(a.shape, a.dtype) for a in arrs], n, start, finish,
                     aliases={i: i for i in range(n)})


def _gather_small(full):
    _, width = full.shape

    def copies(srcs, outs, send_sems, recv_sems):
        _, _, c, me, chips = _mesh_pos()
        mine = srcs[0].at[pl.ds(0, SMALL_ROWS)]
        own = pltpu.make_async_copy(mine, outs[0].at[me], send_sems.at[3])
        return own, [_remote(mine, outs[0].at[me], send_sems.at[j], recv_sems.at[j], (cx, cy, c))
                     for j, (cx, cy) in enumerate(chips)]

    def start(*refs):
        own, cps = copies(*refs)
        for cp in [own] + cps:
            cp.start()

    def finish(srcs, outs, send_sems, recv_sems):
        _, _, c, _, chips = _mesh_pos()
        for j, (cx, cy) in enumerate(chips):
            slot = outs[0].at[2 * cx + cy]
            _remote(slot, slot, send_sems.at[j], recv_sems.at[j], (cx, cy, c)).wait_recv()
        own, cps = copies(srcs, outs, send_sems, recv_sems)
        for cp in cps:
            cp.wait_send()
        own.wait()

    return _Exchange([full], [jax.ShapeDtypeStruct((NSHARD, SMALL_ROWS, width), full.dtype)], 4, start, finish)


def _add_sibling(mine, recv, c, name):
    _, half, width = recv.shape
    tr = _tile(half, 256, 8)
    nb = half // tr

    def body(c_ref, a_ref, b_ref, o_ref):
        del c_ref
        o_ref[...] = (a_ref[...].astype(F32) + b_ref[...].astype(F32)).astype(o_ref.dtype)

    grid_spec = pltpu.PrefetchScalarGridSpec(
        num_scalar_prefetch=1, grid=(NSHARD, nb),
        in_specs=[pl.BlockSpec((1, tr, width), lambda j, r, c_ref: (j, c_ref[0] * nb + r, 0)),
                  pl.BlockSpec((1, tr, width), lambda j, r, c_ref: (j, r, 0))],
        out_specs=pl.BlockSpec((1, tr, width), lambda j, r, c_ref: (j, r, 0)))
    return pl.pallas_call(
        body, grid_spec=grid_spec, out_shape=jax.ShapeDtypeStruct(recv.shape, recv.dtype),
        compiler_params=_cparams(("parallel", "parallel")), name=name)(c, mine, recv)


def _sum_chips(parts, c, name):
    _, half, width = parts.shape
    tr = _tile(half, 256, 8)

    def body(c_ref, p_ref, o_ref):
        del c_ref
        p = [p_ref[j].astype(F32) for j in range(NSHARD)]
        o_ref[0] = ((p[0] + p[1]) + p[2]) + p[3]

    grid_spec = pltpu.PrefetchScalarGridSpec(
        num_scalar_prefetch=1, grid=(half // tr,),
        in_specs=[pl.BlockSpec((NSHARD, tr, width), lambda r, c_ref: (0, r, 0))],
        out_specs=pl.BlockSpec((1, tr, width), lambda r, c_ref: (c_ref[0], r, 0)))
    return pl.pallas_call(
        body, grid_spec=grid_spec, out_shape=jax.ShapeDtypeStruct((2, half, width), F32),
        compiler_params=_cparams(("parallel",)), name=name)(c, parts)


def _unpack(flat, names, shapes):
    out, off = {}, 0
    for n in names:
        sz = _size(shapes[n])
        out[n] = flat[off:off + sz].reshape(shapes[n])
        off += sz
    return out


W_IN_COLS = 3080
W_IN_PAD = 3136


def _reorder_w_in_t(w):
    return jnp.concatenate([w[2048:5120], w[9248:12320], w[0:2048], w[8224:9248], w[5152:6688], w[6688:8224],
                            w[5120:5152], jnp.zeros((NP - 12320, D), w.dtype)], axis=0)


def _restore_w_in_t(g):
    return jnp.concatenate([g[6144:8192], g[0:3072], g[12288:12320], g[9216:10752], g[10752:12288], g[8192:9216],
                            g[3072:6144]], axis=0)


def _lru_group_weights(w):
    w4 = w.reshape(4, 4, 96, 96)
    eye = jnp.eye(4, dtype=w.dtype)
    return (w4[:, :, None, :, :] * eye[None, :, :, None, None]).transpose(0, 1, 3, 2, 4).reshape(4, LRU_G, LRU_G)


def _lru_group_blocks(g):
    g5 = g.reshape(4, 4, 96, 4, 96)
    return jnp.stack([g5[:, a, :, a, :] for a in range(4)], axis=1).reshape(16, 96, 96)


def _spread(a):
    return a.transpose(1, 0, 2).reshape(a.shape[1], NSHARD * a.shape[2])


def _split(a):
    return a.reshape(a.shape[0], NSHARD, a.shape[1] // NSHARD).transpose(1, 0, 2)


class _Reduction:
    def __init__(self, dist, parts, names):
        self.c, self.parts, self.names = dist.c, parts, names

    def swap(self):
        return _swap_halves(self.parts)

    def scatter(self, recv):
        return _scatter_chips([_add_sibling(p, r, self.c, "add_sibling_" + n)
                               for p, r, n in zip(self.parts, recv, self.names)])

    def share(self, landed):
        return _share_halves([_sum_chips(a, self.c, "sum_chips_" + n) for a, n in zip(landed, self.names)])

    def done(self, shared):
        return [a.reshape(2 * a.shape[1], a.shape[2]) for a in shared]


class _Dist:
    def __init__(self, w_in_shard, late_shards):
        self.c = lax.axis_index("c").astype(jnp.int32).reshape(1)
        self.w_in_shard = [w_in_shard]
        self.late_shards = late_shards

    def w_in_ride(self):
        return _gather_shards(self.w_in_shard, [True], relay=True)

    def w_in_arrived(self, got):
        (g_in,) = _with_own_slot(self.w_in_shard, got)
        return _reorder_w_in_t(g_in[:, 0:W_IN_COLS].reshape(NSHARD * W_IN_COLS, D))

    def weights_ride(self):
        return _gather_shards(self.late_shards, [True, True, False], relay=False)

    def weights_arrived(self, got):
        g_kv, g_rows, g_small = _with_own_slot(self.late_shards, got)
        out = {"w_kv": _spread(g_kv)}
        for n, lo_, hi_ in ROW_PIECES:
            out[n] = g_rows[:, lo_:hi_].reshape(NSHARD * (hi_ - lo_), D)
        out["ssd_conv_w"] = _spread(g_small[:, :, 0:768])
        out["ssd_norm_g"] = _spread(g_small[:, :, 768:896])
        out["lru_conv_w"] = _spread(g_small[:, :, 896:1280])
        return out

    def early_parts(self, grads):
        rows = jnp.concatenate([grads[n].reshape(NSHARD, hi_ - lo_, D) for n, lo_, hi_ in ROW_PIECES], axis=1)
        return [_split(grads["w_kv"]), rows]

    def late_parts(self, grads):
        rows = _restore_w_in_t(grads["w_in_rt"]).reshape(NSHARD, W_IN_COLS, D)
        return [jnp.pad(rows, ((0, 0), (0, W_IN_PAD - W_IN_COLS), (0, 0)))]


def _local_grads(x, mem, target, wts, dist=None):
    pad128 = lambda a: jnp.pad(a, ((0, 0), (0, 128 - a.shape[1])))

    if dist is None:
        (h,), _ = _norm_fwd(x, wts["norm_g"])
        w_in_rt = wts["w_in_rt"]
        proj = _mm(h, w_in_rt, F32, "in_proj", tb=True, tn=NP_TILE)
    else:
        (h,), arrived = _norm_fwd(x, wts["norm_g"], ride=dist.w_in_ride())
        w_in_rt = dist.w_in_arrived(arrived)
        proj, arrived = _mm(h, w_in_rt, F32, "in_proj", tb=True, tn=NP_TILE, ride=dist.weights_ride())
        wts = dict(wts, **dist.weights_arrived(arrived))
    wbs, wbl, wbm, wo, wkv = wts["w_br_ssd"], wts["w_br_lru"], wts["w_br_mem"], wts["w_out"], wts["w_kv"]
    wa, wx = _mx(_lru_group_weights(wts["lru_w_a"])), _mx(_lru_group_weights(wts["lru_w_x"]))
    ba, bx = wts["lru_b_a"].reshape(1, LRU_W), wts["lru_b_x"].reshape(1, LRU_W)
    dtb, alog = pad128(wts["ssd_dt_bias"]), pad128(wts["ssd_a_log"])
    dexp = jnp.repeat(wts["ssd_d"], 64, axis=1)
    ng = wts["ssd_norm_g"].reshape(1, SSD_W)
    xbc = _conv_fwd(proj, XBC, wts["ssd_conv_w"], wts["ssd_conv_b"], True, "ssd_conv_fwd")
    yssd, yraw, hprev = _ssd_fwd(xbc, proj, dtb, alog, dexp, ng)
    xl = _conv_fwd(proj, LX, wts["lru_conv_w"], wts["lru_conv_b"], False, "lru_conv_fwd")
    ylru, hs = _lru_fwd(xl, proj, wa, wx, ba, bx, wts["lru_lambda"])
    kk, vv, mn = _mem_kv_fwd(mem, wts["mem_norm_g"], wkv)
    ymem = _attn_fwd(proj, kk, vv)

    dproj, dx2, dx2m, merged, db0, db1, db2, loss_vec, dfg = _merge_fb(
        x, target, yssd, ylru, ymem, proj, wbs, wbl, wbm, wo, wts["final_g"].reshape(1, D))
    grads = {"final_g": dfg.reshape(D)}
    grads["w_out"] = _mm(merged, dx2m, _MXU, "dw_out", ta=True)
    grads["w_br_ssd"] = _mm(yssd, db0, _MXU, "dw_br_ssd", ta=True)
    grads["w_br_lru"] = _mm(ylru, db1, _MXU, "dw_br_lru", ta=True)
    grads["w_br_mem"] = _mm(ymem, db2, _MXU, "dw_br_mem", ta=True)
    dyssd = _mm(db0, wbs, F32, "dy_ssd", tb=True)
    dylru = _mm(db1, wbl, F32, "dy_lru", tb=True)
    dymem = _mm(db2, wbm, F32, "dy_mem", tb=True)

    dproj, dk, dv = _attn_bwd(proj, kk, vv, dymem, dproj)
    grads["w_kv"], grads["mem_norm_g"] = _mem_kv_bwd(mem, wts["mem_norm_g"], mn, wkv, dk, dv)

    early = None if dist is None else _Reduction(dist, dist.early_parts(grads), ["w_kv", "rows"])

    (dproj, dxl, dwa, dwx, dba, dbx, dlam), got = _lru_bwd(
        xl, proj, hs, dylru, dproj, wa, wx, ba, bx, wts["lru_lambda"], ride=early and early.swap())
    grads["lru_w_a"] = _lru_group_blocks(dwa)[None]
    grads["lru_w_x"] = _lru_group_blocks(dwx)[None]
    grads["lru_b_a"], grads["lru_b_x"] = dba.reshape(1, 16, 96), dbx.reshape(1, 16, 96)
    grads["lru_lambda"] = dlam
    (grads["lru_conv_w"], grads["lru_conv_b"]), _ = _conv_bwd_w(
        proj, LX, wts["lru_conv_w"], wts["lru_conv_b"], dxl, False, "lru_conv_bwd_w")
    dproj = _conv_bwd_x(dxl, wts["lru_conv_w"], dproj, LX, "lru_conv_bwd_x")

    (dproj, ddt, dxbc, dng, dda, ddd, ddtb), got = _ssd_bwd(
        xbc, proj, yraw, hprev, dyssd, dproj, dtb, alog, dexp, ng, ride=early and early.scatter(got))
    dproj = _put_block(ddt, dproj, DT, "put_ddt")
    grads["ssd_norm_g"] = dng.reshape(4, 512)
    grads["ssd_dt_bias"] = ddtb[:, 0:32]
    grads["ssd_a_log"] = (dda * -jnp.exp(alog))[:, 0:32]
    grads["ssd_d"] = ddd.reshape(32, 64).sum(axis=1)[None, :]
    (dpre, grads["ssd_conv_w"], grads["ssd_conv_b"]), got = _conv_bwd_w(
        proj, XBC, wts["ssd_conv_w"], wts["ssd_conv_b"], dxbc, True, "ssd_conv_bwd_w", ride=early and early.share(got))
    reduced = {} if dist is None else dict(zip(["w_kv", "rows"], early.done(got)))
    dproj = _conv_bwd_x(dpre, wts["ssd_conv_w"], dproj, XBC, "ssd_conv_bwd_x")

    grads["w_in_rt"] = _mm(dproj, h, _MXU, "dw_in", ta=True, tm=NP_TILE, tn=1024)
    if dist is None:
        dh = _mm(dproj, w_in_rt, F32, "dh", tn=1024, tk=NP_TILE)
        (grad_x, grads["norm_g"]), _ = _norm_bwd(x, wts["norm_g"], dh, dx2)
    else:
        late = _Reduction(dist, dist.late_parts(grads), ["w_in"])
        got = _run_exchange(late.swap(), "swap_halves_w_in")
        dh, got = _mm(dproj, w_in_rt, F32, "dh", tn=1024, tk=NP_TILE, ride=late.scatter(got))
        (grad_x, grads["norm_g"]), _ = _norm_bwd(x, wts["norm_g"], dh, dx2)
        reduced["w_in"] = late.done(_run_exchange(late.share(got), "share_halves_w_in"))[0]
    return jnp.sum(loss_vec), grad_x, grads, reduced


def kernel(x, mem, norm_g, w_in, ssd_conv_w, ssd_conv_b, ssd_dt_bias, ssd_a_log, ssd_d, ssd_norm_g, lru_conv_w, lru_conv_b, lru_w_a, lru_b_a, lru_w_x, lru_b_x, lru_lambda, mem_norm_g, w_kv, w_br_ssd, w_br_lru, w_br_mem, w_out, final_g, loss_target, m_norm_g, m_w_in, m_ssd_conv_w, m_ssd_conv_b, m_ssd_dt_bias, m_ssd_a_log, m_ssd_d, m_ssd_norm_g, m_lru_conv_w, m_lru_conv_b, m_lru_w_a, m_lru_b_a, m_lru_w_x, m_lru_b_x, m_lru_lambda, m_mem_norm_g, m_w_kv, m_w_br_ssd, m_w_br_lru, m_w_br_mem, m_w_out, m_final_g, v_norm_g, v_w_in, v_ssd_conv_w, v_ssd_conv_b, v_ssd_dt_bias, v_ssd_a_log, v_ssd_d, v_ssd_norm_g, v_lru_conv_w, v_lru_conv_b, v_lru_w_a, v_lru_b_a, v_lru_w_x, v_lru_b_x, v_lru_lambda, v_mem_norm_g, v_w_kv, v_w_br_ssd, v_w_br_lru, v_w_br_mem, v_w_out, v_final_g):
    given = dict(locals())

    rows_w = jnp.concatenate([w_br_ssd[0], w_br_lru[0], w_br_mem[0], w_out[0]], axis=0)
    small_w = jnp.concatenate([ssd_conv_w[0], ssd_norm_g[0], lru_conv_w[0]], axis=1)
    w_in_t = jnp.pad(_mx(w_in[0].T), ((0, W_IN_PAD - W_IN_COLS), (0, 0)))
    dist = _Dist(w_in_t, [_mx(w_kv[0]), _mx(rows_w), small_w])
    wts = {n: given[n] for n in REPL}
    wts["lru_w_a"], wts["lru_w_x"] = lru_w_a[0], lru_w_x[0]

    loss_part, grad_x, grads, reduced = _local_grads(x[0], mem[0], loss_target[0], wts, dist)
    loss = lax.psum(loss_part, ("x", "y", "c"))

    repl_flat = jnp.concatenate([grads[n].reshape(-1) for n in REPL])
    repl_flat = jnp.pad(repl_flat, (0, NSHARD * SMALL_Q - repl_flat.shape[0])).reshape(NSHARD, SMALL_Q)
    shard_small = jnp.concatenate([_split(grads[n]).reshape(NSHARD, -1) for n in SMALL_SHARDED], axis=1)
    p_small = jnp.concatenate(
        [repl_flat, shard_small, jnp.zeros((NSHARD, SMALL_BUF_ROWS * PACK_W - SMALL_Q - 5120), F32)], axis=1)
    small = _Reduction(dist, [p_small.reshape(NSHARD, SMALL_BUF_ROWS, PACK_W)], ["small"])
    got = _run_exchange(small.swap(), "swap_halves_small")
    got = _run_exchange(small.scatter(got), "scatter_chips_small")
    got = _run_exchange(small.share(got), "share_halves_small")
    r_small = small.done(got)[0]
    repl_all = _run_exchange(_gather_small(r_small), "gather_small")[0].reshape(-1)

    g_shard = {"w_kv": reduced["w_kv"]}
    for n, lo_, hi_ in ROW_PIECES:
        g_shard[n] = reduced["rows"][lo_:hi_]
    g_shard.update(_unpack(r_small.reshape(-1)[SMALL_Q:], SMALL_SHARDED, SHARD_SHAPE))
    g_repl = _unpack(repl_all, REPL, REPL_SHAPE)

    out_g, out_d, out_m, out_v = {}, {}, {}, {}
    for n in WEIGHTS:
        w_full = given[n]
        if n == "w_in":
            g2 = reduced["w_in"][0:W_IN_COLS]
            d, mo, vo = _adamw(w_in[0].T, g2, m_w_in[0].T, v_w_in[0].T, "adamw_w_in")
            out_g[n], out_d[n], out_m[n], out_v[n] = [a.T[None] for a in (g2, d, mo, vo)]
            continue
        g = (g_shard[n] if n in SHARDED else g_repl[n]).reshape(w_full.shape)
        cols = w_full.shape[-1]
        as2d = lambda a: a.reshape(-1, cols)
        d, mo, vo = _adamw(as2d(w_full), as2d(g), as2d(given["m_" + n]), as2d(given["v_" + n]), "adamw_" + n)
        out_g[n] = g
        out_d[n], out_m[n], out_v[n] = d.reshape(w_full.shape), mo.reshape(w_full.shape), vo.reshape(w_full.shape)

    return (loss, grad_x[None], *[out_g[n] for n in WEIGHTS], *[out_d[n] for n in WEIGHTS],
            *[out_m[n] for n in WEIGHTS], *[out_v[n] for n in WEIGHTS])
```

```python
import jax
import jax.numpy as jnp
from jax import lax
from jax.experimental import pallas as pl
from jax.experimental.pallas import tpu as pltpu

F32 = jnp.float32
_MXU = jnp.bfloat16
_HI = lax.Precision.HIGHEST
MESH = pl.DeviceIdType.MESH

D = 1024
EPS = 1e-6
MEM_HEADS = 4
MEM_HD = 256
LRU_C = 8.0
SSD_L = 128
SSD_W = 2048
LRU_W = 1536
NSHARD = 4

XBC = (0, 3072)
GL = (3072, 3072)
Z = (6144, 2048)
Q = (8192, 1024)
LG = (9216, 1536)
LX = (10752, 1536)
DT = (12288, 256)
NP = 12544
NP_TILE = 1792

ADAM_LR = 0.001
ADAM_B1 = 0.9
ADAM_B2 = 0.999
ADAM_EPS = 1e-08
ADAM_WD = 0.01
ADAM_STEP = 10

VMEM_LIMIT = 56 * 1024 * 1024

SHARDED = ("w_in", "ssd_conv_w", "ssd_norm_g", "lru_conv_w", "w_kv", "w_br_ssd", "w_br_lru", "w_br_mem", "w_out")
SHARD_SHAPE = {"w_in": (1024, 3080), "ssd_conv_w": (4, 768), "ssd_norm_g": (4, 128), "lru_conv_w": (4, 384),
               "w_kv": (1024, 512), "w_br_ssd": (512, 1024), "w_br_lru": (384, 1024), "w_br_mem": (256, 1024),
               "w_out": (256, 1024)}
REPL = ("norm_g", "ssd_conv_b", "ssd_dt_bias", "ssd_a_log", "ssd_d", "lru_conv_b", "lru_w_a", "lru_b_a",
        "lru_w_x", "lru_b_x", "lru_lambda", "mem_norm_g", "final_g")
REPL_SHAPE = {"norm_g": (1, 1024), "ssd_conv_b": (1, 3072), "ssd_dt_bias": (1, 32), "ssd_a_log": (1, 32),
              "ssd_d": (1, 32), "lru_conv_b": (1, 1536), "lru_w_a": (1, 16, 96, 96), "lru_b_a": (1, 16, 96),
              "lru_w_x": (1, 16, 96, 96), "lru_b_x": (1, 16, 96), "lru_lambda": (1, 1536),
              "mem_norm_g": (1, 1024), "final_g": (1024,)}
WEIGHTS = ("norm_g", "w_in", "ssd_conv_w", "ssd_conv_b", "ssd_dt_bias", "ssd_a_log", "ssd_d", "ssd_norm_g",
           "lru_conv_w", "lru_conv_b", "lru_w_a", "lru_b_a", "lru_w_x", "lru_b_x", "lru_lambda", "mem_norm_g",
           "w_kv", "w_br_ssd", "w_br_lru", "w_br_mem", "w_out", "final_g")

ROW_PIECES = (("w_br_ssd", 0, 512), ("w_br_lru", 512, 896), ("w_br_mem", 896, 1152), ("w_out", 1152, 1408))
SMALL_SHARDED = ("ssd_conv_w", "ssd_norm_g", "lru_conv_w")
PACK_W = 512
SMALL_ROWS = 152
SMALL_Q = SMALL_ROWS * PACK_W
SMALL_BUF_ROWS = 176


def _size(shape):
    n = 1
    for s in shape:
        n *= s
    return n


def _sigmoid(x):
    return 0.5 * jnp.tanh(0.5 * x) + 0.5


def _silu(x):
    return x * _sigmoid(x)


def _dsilu(x):
    s = _sigmoid(x)
    return s * (1.0 + x * (1.0 - s))


def _softplus(x):
    return jnp.maximum(x, 0.0) + jnp.log(1.0 + jnp.exp(-jnp.abs(x)))


def _one_minus_sq(log_a, a):
    x = 2.0 * log_a
    series = -x * (1.0 + x * (0.5 + x * (1.0 / 6.0 + x * (1.0 / 24.0))))
    return jnp.where(x > -0.03, series, 1.0 - a * a)


def _dot(a, b, precision=None):
    return jnp.dot(a, b, preferred_element_type=F32, precision=precision)


def _dot_nt(a, b):
    return lax.dot_general(a, b, (((1,), (1,)), ((), ())), preferred_element_type=F32)


def _dot_tn(a, b):
    return lax.dot_general(a, b, (((0,), (0,)), ((), ())), preferred_element_type=F32)


def _mx(a):
    return a.astype(_MXU)


def _cparams(sem):
    return pltpu.CompilerParams(dimension_semantics=sem, vmem_limit_bytes=VMEM_LIMIT)


def _tile(n, want, mult=128):
    if n <= want:
        return n
    for t in range(want - want % mult, 0, -mult):
        if n % t == 0:
            return t
    raise ValueError((n, want, mult))


def _mm(a, b, out_dtype, name, ta=False, tb=False, tm=1024, tn=1280, tk=1024, ride=None):
    k, m = a.shape if ta else a.shape[::-1]
    k2, n = b.shape[::-1] if tb else b.shape
    assert k == k2
    tm, tn, tk = _tile(m, tm), _tile(n, tn), _tile(k, tk)
    nk = k // tk
    contract = (((0 if ta else 1,), (1 if tb else 0,)), ((), ()))

    def body(a_ref, b_ref, o_ref, acc_ref):
        kk = pl.program_id(2)

        @pl.when(kk == 0)
        def _():
            acc_ref[...] = jnp.zeros_like(acc_ref)

        acc_ref[...] += lax.dot_general(a_ref[...], b_ref[...], contract, preferred_element_type=F32)

        @pl.when(kk == nk - 1)
        def _():
            o_ref[...] = acc_ref[...].astype(o_ref.dtype)

    a_spec = pl.BlockSpec((tk, tm), lambda i, j, kk: (kk, i)) if ta else pl.BlockSpec((tm, tk), lambda i, j, kk: (i, kk))
    b_spec = pl.BlockSpec((tn, tk), lambda i, j, kk: (j, kk)) if tb else pl.BlockSpec((tk, tn), lambda i, j, kk: (kk, j))
    outs, carried = _pcall(
        body, ride, (a, b), grid=(m // tm, n // tn, nk),
        in_specs=[a_spec, b_spec],
        out_specs=[pl.BlockSpec((tm, tn), lambda i, j, kk: (i, j))],
        out_shape=[jax.ShapeDtypeStruct((m, n), out_dtype)],
        scratch_shapes=[pltpu.VMEM((tm, tn), F32)],
        sem=("parallel", "parallel", "arbitrary"), name=name)
    return outs[0] if ride is None else (outs[0], carried)


def _norm_fwd(x, g, ride=None):
    s = x.shape[0]
    ts = _tile(s, 512)

    def body(x_ref, g_ref, h_ref):
        xv = x_ref[...]
        r = lax.rsqrt(jnp.mean(xv * xv, axis=-1, keepdims=True) + EPS)
        h_ref[...] = (xv * r * g_ref[...]).astype(h_ref.dtype)

    return _pcall(
        body, ride, (x, g), grid=(s // ts,),
        in_specs=[pl.BlockSpec((ts, D), lambda i: (i, 0)), pl.BlockSpec((1, D), lambda i: (0, 0))],
        out_specs=[pl.BlockSpec((ts, D), lambda i: (i, 0))],
        out_shape=[jax.ShapeDtypeStruct((s, D), _MXU)], scratch_shapes=[], sem=("parallel",), name="norm_fwd")


def _norm_bwd(x, g, dh, dx2, ride=None):
    s = x.shape[0]
    ts = _tile(s, 1024)

    def body(x_ref, g_ref, dh_ref, dx2_ref, gx_ref, dg_ref):
        @pl.when(pl.program_id(0) == 0)
        def _():
            dg_ref[...] = jnp.zeros_like(dg_ref)

        xv = x_ref[...]
        r = lax.rsqrt(jnp.mean(xv * xv, axis=-1, keepdims=True) + EPS)
        xhat = xv * r
        dh_v = dh_ref[...]
        dg_ref[...] += jnp.sum(dh_v * xhat, axis=0, keepdims=True)
        dxh = dh_v * g_ref[...]
        gx_ref[...] = dx2_ref[...] + r * (dxh - xhat * jnp.mean(dxh * xhat, axis=-1, keepdims=True))

    row = pl.BlockSpec((ts, D), lambda i: (i, 0))
    vec = pl.BlockSpec((1, D), lambda i: (0, 0))
    return _pcall(
        body, ride, (x, g, dh, dx2), grid=(s // ts,), in_specs=[row, vec, row, row], out_specs=[row, vec],
        out_shape=[jax.ShapeDtypeStruct((s, D), F32), jax.ShapeDtypeStruct((1, D), F32)],
        scratch_shapes=[], sem=("arbitrary",), name="norm_bwd")


CONV_TS = 512
CONV_RB = 16
CONV_LC = 256


def _fold8(v):
    acc = v[0:8]
    for r0 in range(8, v.shape[0], 8):
        acc = acc + v[r0:r0 + 8]
    return acc


def _conv_fwd(src, blk, w, b, act, name):
    s = src.shape[0]
    off, width = blk
    cb = off // width
    ts = _tile(s, CONV_TS)

    def body(x_ref, w_ref, b_ref, o_ref, ext_ref):
        @pl.when(pl.program_id(0) == 0)
        def _():
            ext_ref[0:8, :] = jnp.zeros((8, width), F32)

        ext_ref[8:8 + ts, :] = x_ref[...]
        for l0 in range(0, width, CONV_LC):
            ls = slice(l0, l0 + CONV_LC)
            taps = [w_ref[k:k + 1, ls] for k in range(4)]
            bias = b_ref[:, ls]
            for r0 in range(0, ts, CONV_RB):
                pre = bias
                for k in range(4):
                    pre = pre + taps[k] * ext_ref[5 + k + r0:5 + k + r0 + CONV_RB, ls]
                o_ref[r0:r0 + CONV_RB, ls] = _silu(pre) if act else pre
        ext_ref[0:8, :] = x_ref[ts - 8:ts, :]

    return pl.pallas_call(
        body, grid=(s // ts,),
        in_specs=[pl.BlockSpec((ts, width), lambda i: (i, cb)), pl.BlockSpec((4, width), lambda i: (0, 0)),
                  pl.BlockSpec((1, width), lambda i: (0, 0))],
        out_specs=pl.BlockSpec((ts, width), lambda i: (i, 0)),
        out_shape=jax.ShapeDtypeStruct((s, width), F32),
        scratch_shapes=[pltpu.VMEM((ts + 8, width), F32)],
        compiler_params=_cparams(("arbitrary",)), name=name)(src, w, b)


def _conv_bwd_w(src, blk, w, b, dout, act, name, ride=None):
    s = src.shape[0]
    off, width = blk
    cb = off // width
    ts = _tile(s, CONV_TS)

    def body(x_ref, w_ref, b_ref, do_ref, *rest):
        if act:
            dpre_ref, dw_ref, db_ref, ext_ref = rest
        else:
            dw_ref, db_ref, ext_ref = rest

        @pl.when(pl.program_id(0) == 0)
        def _():
            ext_ref[0:8, :] = jnp.zeros((8, width), F32)
            dw_ref[...] = jnp.zeros_like(dw_ref)
            db_ref[...] = jnp.zeros_like(db_ref)

        ext_ref[8:8 + ts, :] = x_ref[...]
        for l0 in range(0, width, CONV_LC):
            ls = slice(l0, l0 + CONV_LC)
            taps = [w_ref[k:k + 1, ls] for k in range(4)]
            bias = b_ref[:, ls]
            acc_b = jnp.zeros((8, CONV_LC), F32)
            acc_w = [jnp.zeros((8, CONV_LC), F32) for _ in range(4)]
            for r0 in range(0, ts, CONV_RB):
                xs = [ext_ref[5 + k + r0:5 + k + r0 + CONV_RB, ls] for k in range(4)]
                dpre = do_ref[r0:r0 + CONV_RB, ls]
                if act:
                    pre = bias
                    for k in range(4):
                        pre = pre + taps[k] * xs[k]
                    dpre = dpre * _dsilu(pre)
                    dpre_ref[r0:r0 + CONV_RB, ls] = dpre
                acc_b = acc_b + _fold8(dpre)
                for k in range(4):
                    acc_w[k] = acc_w[k] + _fold8(dpre * xs[k])
            db_ref[:, ls] += jnp.sum(acc_b, axis=0, keepdims=True)
            for k in range(4):
                dw_ref[k:k + 1, ls] += jnp.sum(acc_w[k], axis=0, keepdims=True)
        ext_ref[0:8, :] = x_ref[ts - 8:ts, :]

    row = pl.BlockSpec((ts, width), lambda i: (i, 0))
    outs = [pl.BlockSpec((4, width), lambda i: (0, 0)), pl.BlockSpec((1, width), lambda i: (0, 0))]
    shapes = [jax.ShapeDtypeStruct((4, width), F32), jax.ShapeDtypeStruct((1, width), F32)]
    if act:
        outs = [row] + outs
        shapes = [jax.ShapeDtypeStruct((s, width), F32)] + shapes
    return _pcall(
        body, ride, (src, w, b, dout), grid=(s // ts,),
        in_specs=[pl.BlockSpec((ts, width), lambda i: (i, cb)), pl.BlockSpec((4, width), lambda i: (0, 0)),
                  pl.BlockSpec((1, width), lambda i: (0, 0)), row],
        out_specs=outs, out_shape=shapes,
        scratch_shapes=[pltpu.VMEM((ts + 8, width), F32)], sem=("arbitrary",), name=name)


def _conv_bwd_x(dpre, w, dproj, blk, name):
    s = dpre.shape[0]
    off, width = blk
    cb = off // width
    ts = _tile(s, CONV_TS)
    nt = s // ts

    def body(dp_ref, w_ref, dproj_hbm, o_ref, ext_ref):
        del dproj_hbm

        @pl.when(pl.program_id(0) == 0)
        def _():
            ext_ref[ts:ts + 8, :] = jnp.zeros((8, width), F32)

        ext_ref[0:ts, :] = dp_ref[...]
        for l0 in range(0, width, CONV_LC):
            ls = slice(l0, l0 + CONV_LC)
            taps = [w_ref[k:k + 1, ls] for k in range(4)]
            for r0 in range(0, ts, CONV_RB):
                acc = taps[0] * ext_ref[3 + r0:3 + r0 + CONV_RB, ls]
                for k in range(1, 4):
                    acc = acc + taps[k] * ext_ref[3 - k + r0:3 - k + r0 + CONV_RB, ls]
                o_ref[r0:r0 + CONV_RB, ls] = acc.astype(o_ref.dtype)
        ext_ref[ts:ts + 8, :] = dp_ref[0:8, :]

    return pl.pallas_call(
        body, grid=(nt,),
        in_specs=[pl.BlockSpec((ts, width), lambda i: (nt - 1 - i, 0)), pl.BlockSpec((4, width), lambda i: (0, 0)),
                  pl.BlockSpec(memory_space=pl.ANY)],
        out_specs=pl.BlockSpec((ts, width), lambda i: (nt - 1 - i, cb)),
        out_shape=jax.ShapeDtypeStruct(dproj.shape, dproj.dtype),
        scratch_shapes=[pltpu.VMEM((ts + 8, width), F32)],
        input_output_aliases={2: 0},
        compiler_params=_cparams(("arbitrary",)), name=name)(dpre, w, dproj)


def _ssd_decay(a_cs, acst_ref, h, causal, lane_l):
    col = jnp.sum(jnp.where(lane_l == h, a_cs, 0.0), axis=1, keepdims=True)
    row = acst_ref[h:h + 1, :]
    return jnp.where(causal, jnp.exp(jnp.minimum(col - row, 0.0)), 0.0)


def _split3(x):
    hi = x.astype(jnp.bfloat16)
    rest = x - hi.astype(F32)
    mid = rest.astype(jnp.bfloat16)
    return jnp.concatenate([hi, mid, (rest - mid.astype(F32)).astype(jnp.bfloat16)], axis=1)


def _spread_matrix():
    col = jnp.arange(128, dtype=jnp.int32)[:, None]
    e64 = (col == jnp.arange(SSD_W, dtype=jnp.int32)[None, :] // 64).astype(jnp.bfloat16)
    return jnp.tile(e64, (3, 1))


def _ssd_common(dt_ref, dtb_ref, alog_ref, e64_ref, acst_ref, dtx_ref, acx_ref):
    ll = SSD_L
    dt = _softplus(dt_ref[:, 0:128] + dtb_ref[...])
    a_neg = -jnp.exp(alog_ref[...])
    ri = lax.broadcasted_iota(jnp.int32, (ll, ll), 0)
    ci = lax.broadcasted_iota(jnp.int32, (ll, ll), 1)
    causal = ri >= ci
    a_cs = _dot(causal.astype(F32), dt * a_neg, _HI)
    acst_ref[...] = a_cs.T
    both = _dot(jnp.concatenate([_split3(dt), _split3(a_cs)], axis=0), e64_ref[...])
    dtx_ref[...] = both[0:ll]
    acx_ref[...] = both[ll:2 * ll]
    lane_l = lax.broadcasted_iota(jnp.int32, (ll, 128), 1)
    return dt, a_neg, a_cs, causal, ri, lane_l, lane_l < 64


def _ssd_fwd(xbc, proj, dtb, alog, dexp, ng):
    s = xbc.shape[0]
    ll = SSD_L
    nc = s // ll
    e64 = _spread_matrix()

    def body(xbc_ref, dt_ref, z_ref, dtb_ref, alog_ref, dexp_ref, ng_ref, e64_ref,
             yssd_ref, yraw_ref, hprev_ref, ht_ref, acst_ref, dtx_ref, acx_ref):
        @pl.when(pl.program_id(0) == 0)
        def _():
            ht_ref[...] = jnp.zeros_like(ht_ref)

        hprev_ref[0] = ht_ref[...]
        _, _, a_cs, causal, _, lane_l, lo = _ssd_common(dt_ref, dtb_ref, alog_ref, e64_ref, acst_ref, dtx_ref, acx_ref)
        for g in range(4):
            bg = _mx(xbc_ref[:, 2048 + 128 * g:2176 + 128 * g])
            cg = _mx(xbc_ref[:, 2560 + 128 * g:2688 + 128 * g])
            cbm = _dot_nt(cg, bg)
            for jj in range(4):
                j = 4 * g + jj
                sl = slice(128 * j, 128 * j + 128)
                xp = xbc_ref[:, sl]
                acx = acx_ref[:, sl]
                a_last = acx_ref[ll - 1:ll, sl]
                xdt = xp * dtx_ref[:, sl]
                acc = None
                for hh in range(2):
                    dec = _ssd_decay(a_cs, acst_ref, 2 * j + hh, causal, lane_l)
                    xm = jnp.where(lo if hh == 0 else jnp.logical_not(lo), xdt, 0.0)
                    t = _dot(_mx(dec * cbm), _mx(xm))
                    acc = t if acc is None else acc + t
                ht = ht_ref[j]
                y = acc + _dot(cg, _mx(ht)) * jnp.exp(acx) + xp * dexp_ref[:, sl]
                yraw_ref[:, sl] = y
                st = _dot_tn(bg, _mx(xdt * jnp.exp(a_last - acx)))
                ht_ref[j] = ht * jnp.exp(a_last) + st
        for g in range(4):
            sl = slice(512 * g, 512 * g + 512)
            yg = yraw_ref[:, sl] * _silu(z_ref[:, sl])
            r = lax.rsqrt(jnp.mean(yg * yg, axis=-1, keepdims=True) + EPS)
            yssd_ref[:, sl] = (yg * r * ng_ref[:, sl]).astype(yssd_ref.dtype)

    vec = lambda w: pl.BlockSpec((1, w), lambda c: (0, 0))
    return pl.pallas_call(
        body, grid=(nc,),
        in_specs=[pl.BlockSpec((ll, 3072), lambda c: (c, 0)),
                  pl.BlockSpec((ll, DT[1]), lambda c: (c, DT[0] // DT[1])),
                  pl.BlockSpec((ll, Z[1]), lambda c: (c, Z[0] // Z[1])),
                  vec(128), vec(128), vec(2048), vec(2048),
                  pl.BlockSpec(e64.shape, lambda c: (0, 0))],
        out_specs=[pl.BlockSpec((ll, 2048), lambda c: (c, 0)), pl.BlockSpec((ll, 2048), lambda c: (c, 0)),
                   pl.BlockSpec((1, 16, 128, 128), lambda c: (c, 0, 0, 0))],
        out_shape=[jax.ShapeDtypeStruct((s, 2048), _MXU), jax.ShapeDtypeStruct((s, 2048), F32),
                   jax.ShapeDtypeStruct((nc, 16, 128, 128), F32)],
        scratch_shapes=[pltpu.VMEM((16, 128, 128), F32), pltpu.VMEM((128, ll), F32),
                        pltpu.VMEM((ll, 2048), F32), pltpu.VMEM((ll, 2048), F32)],
        compiler_params=_cparams(("arbitrary",)), name="ssd_fwd")(xbc, proj, proj, dtb, alog, dexp, ng, e64)


def _ssd_bwd(xbc, proj, yraw, hprev, dyssd, dproj, dtb, alog, dexp, ng, ride=None):
    s = xbc.shape[0]
    ll = SSD_L
    nc = s // ll
    e64 = _spread_matrix()

    def body(xbc_ref, dt_ref, z_ref, yraw_ref, hprev_ref, dy_ref, dproj_hbm, dtb_ref, alog_ref, dexp_ref, ng_ref,
             e64_ref,
             dz_ref, ddt_ref, dxbc_ref, dng_ref, dda_ref, ddd_ref, ddtb_ref,
             dht_ref, acst_ref, dtx_ref, acx_ref, dyr_ref, rowt_ref):
        del dproj_hbm

        @pl.when(pl.program_id(0) == 0)
        def _():
            dht_ref[...] = jnp.zeros_like(dht_ref)
            dng_ref[...] = jnp.zeros_like(dng_ref)
            dda_ref[...] = jnp.zeros_like(dda_ref)
            ddd_ref[...] = jnp.zeros_like(ddd_ref)
            ddtb_ref[...] = jnp.zeros_like(ddtb_ref)
            rowt_ref[...] = jnp.zeros_like(rowt_ref)

        for g in range(4):
            sl = slice(512 * g, 512 * g + 512)
            zz = z_ref[:, sl]
            yr = yraw_ref[:, sl]
            sz = _silu(zz)
            yg = yr * sz
            r = lax.rsqrt(jnp.mean(yg * yg, axis=-1, keepdims=True) + EPS)
            yhat = yg * r
            dyv = dy_ref[:, sl]
            dng_ref[:, sl] += jnp.sum(dyv * yhat, axis=0, keepdims=True)
            dyh = dyv * ng_ref[:, sl]
            dyg = r * (dyh - yhat * jnp.mean(dyh * yhat, axis=-1, keepdims=True))
            dz_ref[:, sl] = (dyg * yr * _dsilu(zz)).astype(dz_ref.dtype)
            dyr_ref[:, sl] = dyg * sz

        dt, a_neg, a_cs, causal, ri, lane_l, lo = _ssd_common(dt_ref, dtb_ref, alog_ref, e64_ref,
                                                              acst_ref, dtx_ref, acx_ref)
        lane_1 = lax.broadcasted_iota(jnp.int32, (1, 128), 1)
        da_col = jnp.zeros((ll, 128), F32)
        ddt_x = jnp.zeros((ll, 128), F32)
        last = jnp.zeros((1, 128), F32)
        for g in range(4):
            bg = _mx(xbc_ref[:, 2048 + 128 * g:2176 + 128 * g])
            cg = _mx(xbc_ref[:, 2560 + 128 * g:2688 + 128 * g])
            cbm = _dot_nt(cg, bg)
            dcb = jnp.zeros((ll, ll), F32)
            db_g = jnp.zeros((ll, 128), F32)
            dc_g = jnp.zeros((ll, 128), F32)
            for jj in range(4):
                j = 4 * g + jj
                sl = slice(128 * j, 128 * j + 128)
                xp = xbc_ref[:, sl]
                dtx = dtx_ref[:, sl]
                acx = acx_ref[:, sl]
                a_last = acx_ref[ll - 1:ll, sl]
                ea = jnp.exp(acx)
                dte = jnp.exp(a_last - acx)
                cd = jnp.exp(a_last)
                xdt = xp * dtx
                xdt_m = _mx(xdt)
                dy = dyr_ref[:, sl]
                ht = hprev_ref[0, j]
                dhn = dht_ref[j]
                dhn_m = _mx(dhn)
                gmat = _dot(bg, dhn_m)
                dxdt = gmat * dte
                for hh in range(2):
                    h = 2 * j + hh
                    dec = _ssd_decay(a_cs, acst_ref, h, causal, lane_l)
                    mm = dec * cbm
                    dym = _mx(jnp.where(lo if hh == 0 else jnp.logical_not(lo), dy, 0.0))
                    dxdt = dxdt + _dot_tn(_mx(mm), dym)
                    dm = _dot_nt(dym, xdt_m)
                    dcb = dcb + dm * dec
                    qq = dm * mm
                    da_col = da_col + jnp.where(lane_l == h, jnp.sum(qq, axis=1, keepdims=True), 0.0)
                    rowt_ref[h:h + 1, :] = jnp.sum(qq, axis=0, keepdims=True)
                ch = _dot(cg, _mx(ht))
                dyea = dy * ea
                dyea_m = _mx(dyea)
                xw_m = _mx(xdt * dte)
                dc_g = dc_g + _dot_nt(dyea_m, _mx(ht))
                db_g = db_g + _dot_nt(xw_m, dhn_m)
                wl = xdt * gmat * dte
                lane_a = dyea * ch - wl
                lane_b = dxdt * xp
                lane_c = jnp.sum(dhn * ht, axis=0, keepdims=True) * cd + jnp.sum(wl, axis=0, keepdims=True)
                for hh in range(2):
                    h = 2 * j + hh
                    mine = lo if hh == 0 else jnp.logical_not(lo)
                    da_col = da_col + jnp.where(
                        lane_l == h, jnp.sum(jnp.where(mine, lane_a, 0.0), axis=1, keepdims=True), 0.0)
                    ddt_x = ddt_x + jnp.where(
                        lane_l == h, jnp.sum(jnp.where(mine, lane_b, 0.0), axis=1, keepdims=True), 0.0)
                    mine_1 = (lane_1 < 64) if hh == 0 else (lane_1 >= 64)
                    last = last + jnp.where(
                        lane_1 == h, jnp.sum(jnp.where(mine_1, lane_c, 0.0), axis=1, keepdims=True), 0.0)
                dht_ref[j] = dhn * cd + _dot_tn(cg, dyea_m)
                dxbc_ref[:, sl] = dxdt * dtx + dy * dexp_ref[:, sl]
                ddd_ref[:, sl] += jnp.sum(dy * xp, axis=0, keepdims=True)
            dcb_m = _mx(dcb)
            dxbc_ref[:, 2048 + 128 * g:2176 + 128 * g] = db_g + _dot_tn(dcb_m, cg)
            dxbc_ref[:, 2560 + 128 * g:2688 + 128 * g] = dc_g + _dot(dcb_m, bg)

        da_cs = da_col - rowt_ref[...].T
        da_cs = da_cs + jnp.where(lax.broadcasted_iota(jnp.int32, (ll, 128), 0) == ll - 1, last, 0.0)
        d_dta = _dot((ri <= lax.broadcasted_iota(jnp.int32, (ll, ll), 1)).astype(F32), da_cs, _HI)
        ddt = d_dta * a_neg + ddt_x
        dda_ref[...] += jnp.sum(d_dta * dt, axis=0, keepdims=True)
        ddt_raw = ddt * _sigmoid(dt_ref[:, 0:128] + dtb_ref[...])
        ddtb_ref[...] += jnp.sum(ddt_raw, axis=0, keepdims=True)
        ddt_ref[:, 0:128] = ddt_raw.astype(ddt_ref.dtype)
        ddt_ref[:, 128:DT[1]] = jnp.zeros((ll, DT[1] - 128), ddt_ref.dtype)

    rev = lambda c: nc - 1 - c
    vec = lambda w: pl.BlockSpec((1, w), lambda c: (0, 0))
    row = lambda w: pl.BlockSpec((ll, w), lambda c: (rev(c), 0))
    return _pcall(
        body, ride, (xbc, proj, proj, yraw, hprev, dyssd, dproj, dtb, alog, dexp, ng, e64), grid=(nc,),
        in_specs=[row(3072),
                  pl.BlockSpec((ll, DT[1]), lambda c: (rev(c), DT[0] // DT[1])),
                  pl.BlockSpec((ll, Z[1]), lambda c: (rev(c), Z[0] // Z[1])),
                  row(2048),
                  pl.BlockSpec((1, 16, 128, 128), lambda c: (rev(c), 0, 0, 0)),
                  row(2048),
                  pl.BlockSpec(memory_space=pl.ANY),
                  vec(128), vec(128), vec(2048), vec(2048),
                  pl.BlockSpec(e64.shape, lambda c: (0, 0))],
        out_specs=[pl.BlockSpec((ll, Z[1]), lambda c: (rev(c), Z[0] // Z[1])),
                   row(DT[1]),
                   row(3072), vec(2048), vec(128), vec(2048), vec(128)],
        out_shape=[jax.ShapeDtypeStruct(dproj.shape, dproj.dtype), jax.ShapeDtypeStruct((s, DT[1]), dproj.dtype),
                   jax.ShapeDtypeStruct((s, 3072), F32), jax.ShapeDtypeStruct((1, 2048), F32),
                   jax.ShapeDtypeStruct((1, 128), F32), jax.ShapeDtypeStruct((1, 2048), F32),
                   jax.ShapeDtypeStruct((1, 128), F32)],
        scratch_shapes=[pltpu.VMEM((16, 128, 128), F32), pltpu.VMEM((128, ll), F32),
                        pltpu.VMEM((ll, 2048), F32), pltpu.VMEM((ll, 2048), F32), pltpu.VMEM((ll, 2048), F32),
                        pltpu.VMEM((128, ll), F32)],
        aliases={6: 0}, sem=("arbitrary",), name="ssd_bwd")


def _put_block(src, dproj, blk, name):
    s = src.shape[0]
    off, width = blk
    cb = off // width
    ts = _tile(s, 1024)

    def body(s_ref, dproj_hbm, o_ref):
        del dproj_hbm
        o_ref[...] = s_ref[...]

    return pl.pallas_call(
        body, grid=(s // ts,),
        in_specs=[pl.BlockSpec((ts, width), lambda i: (i, 0)), pl.BlockSpec(memory_space=pl.ANY)],
        out_specs=pl.BlockSpec((ts, width), lambda i: (i, cb)),
        out_shape=jax.ShapeDtypeStruct(dproj.shape, dproj.dtype),
        input_output_aliases={1: 0},
        compiler_params=_cparams(("parallel",)), name=name)(src, dproj)


LRU_G = 384


def _lru_gates(xl_ref, wa_ref, wx_ref, ba_ref, bx_ref, lam_ref, g):
    sl = slice(LRU_G * g, LRU_G * g + LRU_G)
    xg = xl_ref[:, sl]
    xm = _mx(xg)
    pa = _dot(xm, wa_ref[g]) + ba_ref[:, sl]
    r = jnp.where(pa < -12.0, jnp.exp(pa), _sigmoid(pa))
    ig = _sigmoid(_dot(xm, wx_ref[g]) + bx_ref[:, sl])
    sp = _softplus(-lam_ref[:, sl])
    log_a = (-LRU_C * r) * sp
    a = jnp.exp(log_a)
    mult = jnp.sqrt(_one_minus_sq(log_a, a))
    return sl, xg, r, ig, sp, a, mult


def _lru_fwd(xl, proj, wa, wx, ba, bx, lam):
    s = xl.shape[0]
    ts = _tile(s, 512)
    w = LRU_W

    def body(xl_ref, lg_ref, wa_ref, wx_ref, ba_ref, bx_ref, lam_ref, y_ref, hs_ref, a_ref, u_ref, carry_ref):
        @pl.when(pl.program_id(0) == 0)
        def _():
            carry_ref[...] = jnp.zeros_like(carry_ref)

        for g in range(4):
            sl, xg, _, ig, _, a, mult = _lru_gates(xl_ref, wa_ref, wx_ref, ba_ref, bx_ref, lam_ref, g)
            a_ref[:, sl] = a
            u_ref[:, sl] = mult * (ig * xg)

        def step(t, h):
            h = a_ref[pl.ds(t, 1), :] * h + u_ref[pl.ds(t, 1), :]
            hs_ref[pl.ds(t, 1), :] = h
            return h

        carry_ref[0:1, :] = lax.fori_loop(0, ts, step, carry_ref[0:1, :], unroll=8)
        y_ref[...] = (hs_ref[...] * _silu(lg_ref[...])).astype(y_ref.dtype)

    row = pl.BlockSpec((ts, w), lambda i: (i, 0))
    vec = pl.BlockSpec((1, w), lambda i: (0, 0))
    wsp = pl.BlockSpec((4, LRU_G, LRU_G), lambda i: (0, 0, 0))
    return pl.pallas_call(
        body, grid=(s // ts,),
        in_specs=[row, pl.BlockSpec((ts, w), lambda i: (i, LG[0] // w)), wsp, wsp, vec, vec, vec],
        out_specs=[row, row],
        out_shape=[jax.ShapeDtypeStruct((s, w), _MXU), jax.ShapeDtypeStruct((s, w), F32)],
        scratch_shapes=[pltpu.VMEM((ts, w), F32), pltpu.VMEM((ts, w), F32), pltpu.VMEM((8, w), F32)],
        compiler_params=_cparams(("arbitrary",)), name="lru_fwd")(xl, proj, wa, wx, ba, bx, lam)


def _lru_bwd(xl, proj, hs, dy, dproj, wa, wx, ba, bx, lam, ride=None):
    s = xl.shape[0]
    ts = _tile(s, 256)
    nt = s // ts
    w = LRU_W
    hb = ts // 8

    def body(xl_ref, lg_ref, hs_ref, hprev_ref, dy_ref, dproj_hbm, wa_ref, wx_ref, ba_ref, bx_ref, lam_ref,
             dlg_ref, dxl_ref, dwa_ref, dwx_ref, dba_ref, dbx_ref, dlam_ref,
             a_ref, dh_ref, ext_ref, carry_ref, r_ref, ig_ref, mult_ref):
        del dproj_hbm
        i = pl.program_id(0)

        @pl.when(i == 0)
        def _():
            carry_ref[...] = jnp.zeros_like(carry_ref)
            for ref in (dwa_ref, dwx_ref, dba_ref, dbx_ref, dlam_ref):
                ref[...] = jnp.zeros_like(ref)

        lg = lg_ref[...]
        dyv = dy_ref[...]
        dh_ref[...] = dyv * _silu(lg)
        dlg_ref[...] = (dyv * hs_ref[...] * _dsilu(lg)).astype(dlg_ref.dtype)
        for g in range(4):
            sl, _, r, ig, _, a, mult = _lru_gates(xl_ref, wa_ref, wx_ref, ba_ref, bx_ref, lam_ref, g)
            a_ref[:, sl] = a
            r_ref[:, sl] = r
            ig_ref[:, sl] = ig
            mult_ref[:, sl] = mult

        def step(k, carry):
            t = ts - 1 - k
            dh = dh_ref[pl.ds(t, 1), :] + carry
            dh_ref[pl.ds(t, 1), :] = dh
            return a_ref[pl.ds(t, 1), :] * dh

        carry_ref[0:1, :] = lax.fori_loop(0, ts, step, carry_ref[0:1, :], unroll=8)

        ext_ref[0:8, :] = jnp.where(i == nt - 1, 0.0, 1.0) * hprev_ref[...]
        ext_ref[8:8 + ts, :] = hs_ref[...]
        for g in range(4):
            sl = slice(LRU_G * g, LRU_G * g + LRU_G)
            xg, r, ig, a, mult = xl_ref[:, sl], r_ref[:, sl], ig_ref[:, sl], a_ref[:, sl], mult_ref[:, sl]
            sp = _softplus(-lam_ref[:, sl])
            dh = dh_ref[:, sl]
            da = dh * ext_ref[7:7 + ts, sl]
            dmult = dh * ig * xg
            di = dh * mult * xg
            dxl = dh * mult * ig
            dlog_a = da * a - dmult * (a * a) / mult
            dlam_ref[:, sl] += jnp.sum(dlog_a * r, axis=0, keepdims=True) * (LRU_C * _sigmoid(-lam_ref[:, sl]))
            dpa = dlog_a * (-LRU_C * sp) * r * (1.0 - r)
            dpx = di * ig * (1.0 - ig)
            dba_ref[:, sl] += jnp.sum(dpa, axis=0, keepdims=True)
            dbx_ref[:, sl] += jnp.sum(dpx, axis=0, keepdims=True)
            dpa_m, dpx_m, xm = _mx(dpa), _mx(dpx), _mx(xg)
            dxl_ref[:, sl] = dxl + _dot_nt(dpa_m, wa_ref[g]) + _dot_nt(dpx_m, wx_ref[g])
            dwa_ref[g] += _dot_tn(xm, dpa_m)
            dwx_ref[g] += _dot_tn(xm, dpx_m)

    rev = lambda i: nt - 1 - i
    row = pl.BlockSpec((ts, w), lambda i: (rev(i), 0))
    vec = pl.BlockSpec((1, w), lambda i: (0, 0))
    wsp = pl.BlockSpec((4, LRU_G, LRU_G), lambda i: (0, 0, 0))
    lgs = pl.BlockSpec((ts, w), lambda i: (rev(i), LG[0] // w))
    return _pcall(
        body, ride, (xl, proj, hs, hs, dy, dproj, wa, wx, ba, bx, lam), grid=(nt,),
        in_specs=[row, lgs, row, pl.BlockSpec((8, w), lambda i: (jnp.maximum(rev(i) * hb - 1, 0), 0)), row,
                  pl.BlockSpec(memory_space=pl.ANY), wsp, wsp, vec, vec, vec],
        out_specs=[lgs, row, wsp, wsp, vec, vec, vec],
        out_shape=[jax.ShapeDtypeStruct(dproj.shape, dproj.dtype), jax.ShapeDtypeStruct((s, w), F32),
                   jax.ShapeDtypeStruct((4, LRU_G, LRU_G), F32), jax.ShapeDtypeStruct((4, LRU_G, LRU_G), F32),
                   jax.ShapeDtypeStruct((1, w), F32), jax.ShapeDtypeStruct((1, w), F32),
                   jax.ShapeDtypeStruct((1, w), F32)],
        scratch_shapes=[pltpu.VMEM((ts, w), F32), pltpu.VMEM((ts, w), F32), pltpu.VMEM((ts + 8, w), F32),
                        pltpu.VMEM((8, w), F32), pltpu.VMEM((ts, w), F32), pltpu.VMEM((ts, w), F32),
                        pltpu.VMEM((ts, w), F32)],
        aliases={5: 0}, sem=("arbitrary",), name="lru_bwd")


def _mem_kv_fwd(mem, g, wkv):
    m = mem.shape[0]

    def body(mem_ref, g_ref, w_ref, k_ref, v_ref, mn_ref):
        mv = mem_ref[...]
        r = lax.rsqrt(jnp.mean(mv * mv, axis=-1, keepdims=True) + EPS)
        mn = _mx(mv * r * g_ref[...])
        mn_ref[...] = mn
        kv = _dot(mn, w_ref[...])
        k_ref[...] = kv[:, 0:D].astype(k_ref.dtype)
        v_ref[...] = kv[:, D:2 * D].astype(v_ref.dtype)

    sh = jax.ShapeDtypeStruct((m, D), _MXU)
    return pl.pallas_call(body, out_shape=[sh, sh, sh], compiler_params=_cparams(None), name="mem_kv_fwd")(mem, g, wkv)


def _mem_kv_bwd(mem, g, mn, wkv, dk, dv):
    m = mem.shape[0]

    def body(mem_ref, g_ref, mn_ref, w_ref, dk_ref, dv_ref, dw_ref, dg_ref):
        dkv = _mx(jnp.concatenate([dk_ref[...], dv_ref[...]], axis=1))
        dw_ref[...] = _dot_tn(mn_ref[...], dkv).astype(dw_ref.dtype)
        dmn = _dot_nt(dkv, w_ref[...])
        mv = mem_ref[...]
        r = lax.rsqrt(jnp.mean(mv * mv, axis=-1, keepdims=True) + EPS)
        dg_ref[...] = jnp.sum(dmn * mv * r, axis=0, keepdims=True)

    del m
    return pl.pallas_call(
        body, out_shape=[jax.ShapeDtypeStruct((D, 2 * D), _MXU), jax.ShapeDtypeStruct((1, D), F32)],
        compiler_params=_cparams(None), name="mem_kv_bwd")(mem, g, mn, wkv, dk, dv)


def _attn_probs(q_ref, k_ref, hd):
    sl = slice(MEM_HD * hd, MEM_HD * hd + MEM_HD)
    qh = _mx(q_ref[:, sl])
    sc = _dot_nt(qh, k_ref[:, sl]) * (MEM_HD ** -0.5)
    e = jnp.exp(sc - jnp.max(sc, axis=-1, keepdims=True))
    return sl, qh, e / jnp.sum(e, axis=-1, keepdims=True)


def _attn_fwd(proj, k, v):
    s = proj.shape[0]
    m = k.shape[0]
    ts = _tile(s, 1024)

    def body(q_ref, k_ref, v_ref, y_ref):
        for hd in range(MEM_HEADS):
            sl, _, p = _attn_probs(q_ref, k_ref, hd)
            y_ref[:, sl] = _dot(_mx(p), v_ref[:, sl]).astype(y_ref.dtype)

    kvs = pl.BlockSpec((m, D), lambda i: (0, 0))
    return pl.pallas_call(
        body, grid=(s // ts,),
        in_specs=[pl.BlockSpec((ts, D), lambda i: (i, Q[0] // D)), kvs, kvs],
        out_specs=pl.BlockSpec((ts, D), lambda i: (i, 0)),
        out_shape=jax.ShapeDtypeStruct((s, D), _MXU),
        compiler_params=_cparams(("parallel",)), name="attn_fwd")(proj, k, v)


def _attn_bwd(proj, k, v, dy, dproj):
    s = proj.shape[0]
    m = k.shape[0]
    ts = _tile(s, 1024)

    def body(q_ref, k_ref, v_ref, dy_ref, dproj_hbm, dq_ref, dk_ref, dv_ref):
        del dproj_hbm

        @pl.when(pl.program_id(0) == 0)
        def _():
            dk_ref[...] = jnp.zeros_like(dk_ref)
            dv_ref[...] = jnp.zeros_like(dv_ref)

        for hd in range(MEM_HEADS):
            sl, qh, p = _attn_probs(q_ref, k_ref, hd)
            dyh = _mx(dy_ref[:, sl])
            dp = _dot_nt(dyh, v_ref[:, sl])
            ds = _mx(p * (dp - jnp.sum(dp * p, axis=-1, keepdims=True)) * (MEM_HD ** -0.5))
            dq_ref[:, sl] = _dot(ds, k_ref[:, sl]).astype(dq_ref.dtype)
            dk_ref[:, sl] += _dot_tn(ds, qh)
            dv_ref[:, sl] += _dot_tn(_mx(p), dyh)

    kvs = pl.BlockSpec((m, D), lambda i: (0, 0))
    qs = pl.BlockSpec((ts, D), lambda i: (i, Q[0] // D))
    return pl.pallas_call(
        body, grid=(s // ts,),
        in_specs=[qs, kvs, kvs, pl.BlockSpec((ts, D), lambda i: (i, 0)), pl.BlockSpec(memory_space=pl.ANY)],
        out_specs=[qs, kvs, kvs],
        out_shape=[jax.ShapeDtypeStruct(dproj.shape, dproj.dtype), jax.ShapeDtypeStruct((m, D), F32),
                   jax.ShapeDtypeStruct((m, D), F32)],
        input_output_aliases={4: 0},
        compiler_params=_cparams(("arbitrary",)), name="attn_bwd")(proj, k, v, dy, dproj)


def _merge_fb(x, target, yssd, ylru, ymem, proj, wbs, wbl, wbm, wo, fg):
    s = x.shape[0]
    ts = _tile(s, 256)

    def body(x_ref, t_ref, ys_ref, yl_ref, ym_ref, gl_ref, wbs_ref, wbl_ref, wbm_ref, wo_ref, fg_ref,
             dgl_ref, dx2_ref, dx2m_ref, mg_ref, db0_ref, db1_ref, db2_ref, loss_ref, dfg_ref):
        @pl.when(pl.program_id(0) == 0)
        def _():
            loss_ref[...] = jnp.zeros_like(loss_ref)
            dfg_ref[...] = jnp.zeros_like(dfg_ref)

        bs = (_dot(ys_ref[...], wbs_ref[...]), _dot(yl_ref[...], wbl_ref[...]), _dot(ym_ref[...], wbm_ref[...]))
        gates = [_sigmoid(gl_ref[:, D * n:D * n + D]) for n in range(3)]
        merged = gates[0] * bs[0] + gates[1] * bs[1] + gates[2] * bs[2]
        mg = _mx(merged)
        mg_ref[...] = mg
        x2 = x_ref[...] + _dot(mg, wo_ref[...])
        r = lax.rsqrt(jnp.mean(x2 * x2, axis=-1, keepdims=True) + EPS)
        xhat = x2 * r
        err = xhat * fg_ref[...] - t_ref[...]
        loss_ref[...] += jnp.sum(err * err, axis=0, keepdims=True) * (0.5 / D)
        dy = err * (1.0 / D)
        dfg_ref[...] += jnp.sum(dy * xhat, axis=0, keepdims=True)
        dxh = dy * fg_ref[...]
        dx2 = r * (dxh - xhat * jnp.mean(dxh * xhat, axis=-1, keepdims=True))
        dx2_ref[...] = dx2
        dx2m = _mx(dx2)
        dx2m_ref[...] = dx2m
        dmg = _dot_nt(dx2m, wo_ref[...])
        for n, db_ref in enumerate((db0_ref, db1_ref, db2_ref)):
            gt = gates[n]
            dgl_ref[:, D * n:D * n + D] = (dmg * bs[n] * gt * (1.0 - gt)).astype(dgl_ref.dtype)
            db_ref[...] = (dmg * gt).astype(db_ref.dtype)

    row = lambda w: pl.BlockSpec((ts, w), lambda i: (i, 0))
    full = lambda a: pl.BlockSpec(a.shape, lambda i: (0, 0))
    vec = pl.BlockSpec((1, D), lambda i: (0, 0))
    gls = pl.BlockSpec((ts, GL[1]), lambda i: (i, GL[0] // GL[1]))
    act = jax.ShapeDtypeStruct((s, D), _MXU)
    return pl.pallas_call(
        body, grid=(s // ts,),
        in_specs=[row(D), row(D), row(SSD_W), row(LRU_W), row(D), gls, full(wbs), full(wbl), full(wbm), full(wo), vec],
        out_specs=[gls, row(D), row(D), row(D), row(D), row(D), row(D), vec, vec],
        out_shape=[jax.ShapeDtypeStruct((s, NP), _MXU), jax.ShapeDtypeStruct((s, D), F32), act, act, act, act, act,
                   jax.ShapeDtypeStruct((1, D), F32), jax.ShapeDtypeStruct((1, D), F32)],
        compiler_params=_cparams(("arbitrary",)), name="merge_fwd_bwd")(
            x, target, yssd, ylru, ymem, proj, wbs, wbl, wbm, wo, fg)


def _adamw(w, g, m, v, name):
    rows, cols = w.shape
    tr = _tile(rows, 512, 8)

    def body(w_ref, g_ref, m_ref, v_ref, d_ref, mo_ref, vo_ref):
        gv = g_ref[...]
        mn = ADAM_B1 * m_ref[...] + (1.0 - ADAM_B1) * gv
        vn = ADAM_B2 * v_ref[...] + (1.0 - ADAM_B2) * (gv * gv)
        m_hat = mn / (1.0 - ADAM_B1 ** ADAM_STEP)
        v_hat = vn / (1.0 - ADAM_B2 ** ADAM_STEP)
        d_ref[...] = -ADAM_LR * (m_hat / (jnp.sqrt(v_hat) + ADAM_EPS) + ADAM_WD * w_ref[...])
        mo_ref[...] = mn
        vo_ref[...] = vn

    blk = pl.BlockSpec((tr, cols), lambda i: (i, 0))
    sh = jax.ShapeDtypeStruct((rows, cols), F32)
    return pl.pallas_call(
        body, grid=(rows // tr,), in_specs=[blk] * 4, out_specs=[blk] * 3, out_shape=[sh] * 3,
        compiler_params=_cparams(("parallel",)), name=name)(w, g, m, v)


def _mesh_pos():
    x, y, c = lax.axis_index("x"), lax.axis_index("y"), lax.axis_index("c")
    chips = [(1 - x, y), (x, 1 - y), (1 - x, 1 - y)]
    return x, y, c, 2 * x + y, chips


def _hbm():
    return pl.BlockSpec(memory_space=pl.ANY)


def _remote(src, dst, send_sem, recv_sem, dev):
    return pltpu.make_async_remote_copy(src_ref=src, dst_ref=dst, send_sem=send_sem, recv_sem=recv_sem,
                                        device_id=dev, device_id_type=MESH)


def _sems(n):
    return [pltpu.SemaphoreType.DMA((n,)), pltpu.SemaphoreType.DMA((n,))]


class _Exchange:
    def __init__(self, inputs, out_shape, n_sem, start, finish, aliases=None):
        self.inputs, self.out_shape, self.n_sem = list(inputs), list(out_shape), n_sem
        self.start, self.finish, self.aliases = start, finish, dict(aliases or {})


def _run_exchange(ex, name):
    n_in, n_out = len(ex.inputs), len(ex.out_shape)

    def body(*refs):
        srcs, outs = refs[:n_in], refs[n_in:n_in + n_out]
        send_sems, recv_sems = refs[n_in + n_out:]
        ex.start(srcs, outs, send_sems, recv_sems)
        ex.finish(srcs, outs, send_sems, recv_sems)

    return pl.pallas_call(
        body, in_specs=[_hbm()] * n_in, out_specs=[_hbm()] * n_out, out_shape=ex.out_shape,
        input_output_aliases=ex.aliases, scratch_shapes=_sems(ex.n_sem), name=name)(*ex.inputs)


def _pcall(body, ride, args, *, grid, in_specs, out_specs, out_shape, scratch_shapes, sem, name, aliases=None):
    in_specs, out_specs, out_shape = list(in_specs), list(out_specs), list(out_shape)
    scratch_shapes, aliases = list(scratch_shapes), dict(aliases or {})
    if ride is None:
        outs = pl.pallas_call(
            body, grid=grid, in_specs=in_specs, out_specs=out_specs, out_shape=out_shape, scratch_shapes=scratch_shapes,
            input_output_aliases=aliases, compiler_params=_cparams(sem), name=name)(*args)
        return outs, None
    n_in, n_out, n_scr = len(in_specs), len(out_shape), len(scratch_shapes)
    e_in, e_out = len(ride.inputs), len(ride.out_shape)

    def carried(*refs):
        cut = [n_in, e_in, n_out, e_out, n_scr]
        parts, p = [], 0
        for c in cut:
            parts.append(refs[p:p + c])
            p += c
        ins, e_ins, outs, e_outs, scr = parts
        send_sems, recv_sems = refs[p], refs[p + 1]
        first = last = None
        for d, size in enumerate(grid):
            i = pl.program_id(d)
            first = (i == 0) if first is None else jnp.logical_and(first, i == 0)
            last = (i == size - 1) if last is None else jnp.logical_and(last, i == size - 1)

        @pl.when(first)
        def _():
            ride.start(e_ins, e_outs, send_sems, recv_sems)

        body(*ins, *outs, *scr)

        @pl.when(last)
        def _():
            ride.finish(e_ins, e_outs, send_sems, recv_sems)

    for k, v in ride.aliases.items():
        aliases[n_in + k] = n_out + v
    res = pl.pallas_call(
        carried, grid=grid, in_specs=in_specs + [_hbm()] * e_in, out_specs=out_specs + [_hbm()] * e_out,
        out_shape=out_shape + ride.out_shape, scratch_shapes=scratch_shapes + _sems(ride.n_sem),
        input_output_aliases=aliases, compiler_params=_cparams(("arbitrary",) * len(grid)),
        name=name)(*args, *ride.inputs)
    return res[:n_out], res[n_out:]


def _gather_shards(arrs, split, relay):
    n = len(arrs)
    n_sem = sum(6 if sp else 3 for sp in split)

    def rows(i, which):
        if not split[i]:
            return pl.ds(0, arrs[i].shape[0])
        half = arrs[i].shape[0] // 2
        return pl.ds(which * half, half)

    def sends(srcs, outs, send_sems, recv_sems):
        _, _, c, me, chips = _mesh_pos()
        return [_remote(srcs[i].at[rows(i, c)], outs[i].at[me, rows(i, c)], send_sems.at[3 * i + j],
                        recv_sems.at[3 * i + j], (cx, cy, c))
                for i in range(n) for j, (cx, cy) in enumerate(chips) if not (relay and split[i] and j == 2)]

    def start(srcs, outs, send_sems, recv_sems):
        for cp in sends(srcs, outs, send_sems, recv_sems):
            cp.start()

    def finish(srcs, outs, send_sems, recv_sems):
        x, y, c, _, chips = _mesh_pos()
        sib = (x, y, 1 - c)
        first = ((x + 1 - c) % 2, (y + c) % 2)
        other = ((x + c) % 2, (y + 1 - c) % 2)
        started, k = [], 3 * n
        for i in range(n):
            sem = lambda j, i=i: (send_sems.at[3 * i + j], recv_sems.at[3 * i + j])
            if not split[i]:
                for j, (cx, cy) in enumerate(chips):
                    slot = outs[i].at[2 * cx + cy]
                    _remote(slot, slot, *sem(j), (cx, cy, c)).wait_recv()
                continue
            to_sib = lambda j, k=k: (send_sems.at[k + j], recv_sems.at[k + j])
            slot = lambda chip, which, i=i: outs[i].at[2 * chip[0] + chip[1], rows(i, which)]
            got = slot(first, c)
            _remote(got, got, *sem(c), (*first, c)).wait_recv()
            if relay:
                started.append(_remote(got, got, *sem(2), (*other, c)))
                started[-1].start()
            started.append(_remote(got, got, *to_sib(c), sib))
            started[-1].start()
            for chip, j_in, j_sib in ((other, 1 - c, 1 - c), (chips[2], 2, 2)):
                got = slot(chip, c)
                _remote(got, got, *sem(j_in), (*chip, c)).wait_recv()
                started.append(_remote(got, got, *to_sib(j_sib), sib))
                started[-1].start()
            for chip, j_sib in ((first, c), (other, 1 - c), (chips[2], 2)):
                theirs = slot(chip, 1 - c)
                _remote(theirs, theirs, *to_sib(j_sib), sib).wait_recv()
            k += 3
        for cp in sends(srcs, outs, send_sems, recv_sems) + started:
            cp.wait_send()

    return _Exchange(arrs, [jax.ShapeDtypeStruct((NSHARD,) + a.shape, a.dtype) for a in arrs], n_sem, start, finish)


def _with_own_slot(arrs, got):
    own_slot = jnp.arange(NSHARD, dtype=jnp.int32)[:, None, None] == 2 * lax.axis_index("x") + lax.axis_index("y")
    return [jnp.where(own_slot, a[None], g) for a, g in zip(arrs, got)]


def _swap_halves(arrs):
    n = len(arrs)

    def copies(srcs, outs, send_sems, recv_sems):
        x, y, c, _, _ = _mesh_pos()
        cps = []
        for i in range(n):
            half = arrs[i].shape[1] // 2
            cps.append(_remote(srcs[i].at[:, pl.ds((1 - c) * half, half)], outs[i], send_sems.at[i], recv_sems.at[i],
                               (x, y, 1 - c)))
        return cps

    def start(*refs):
        for cp in copies(*refs):
            cp.start()

    def finish(*refs):
        for cp in copies(*refs):
            cp.wait()

    shapes = [jax.ShapeDtypeStruct((NSHARD, a.shape[1] // 2, a.shape[2]), a.dtype) for a in arrs]
    return _Exchange(arrs, shapes, n, start, finish)


def _scatter_chips(arrs):
    n = len(arrs)

    def copies(srcs, outs, send_sems, recv_sems):
        _, _, c, me, chips = _mesh_pos()
        own = [pltpu.make_async_copy(srcs[i].at[me], outs[i].at[me], send_sems.at[3 * n + i]) for i in range(n)]
        cps = [_remote(srcs[i].at[2 * cx + cy], outs[i].at[me], send_sems.at[3 * i + j], recv_sems.at[3 * i + j],
                       (cx, cy, c)) for i in range(n) for j, (cx, cy) in enumerate(chips)]
        return own, cps

    def start(*refs):
        own, cps = copies(*refs)
        for cp in own + cps:
            cp.start()

    def finish(srcs, outs, send_sems, recv_sems):
        _, _, c, _, chips = _mesh_pos()
        for i in range(n):
            for j, (cx, cy) in enumerate(chips):
                slot = outs[i].at[2 * cx + cy]
                _remote(slot, slot, send_sems.at[3 * i + j], recv_sems.at[3 * i + j], (cx, cy, c)).wait_recv()
        own, cps = copies(srcs, outs, send_sems, recv_sems)
        for cp in cps:
            cp.wait_send()
        for cp in own:
            cp.wait()

    return _Exchange(arrs, [jax.ShapeDtypeStruct(a.shape, a.dtype) for a in arrs], 4 * n, start, finish)


def _share_halves(arrs):
    n = len(arrs)

    def copies(outs, send_sems, recv_sems):
        x, y, c, _, _ = _mesh_pos()
        return [_remote(outs[i].at[c], outs[i].at[c], send_sems.at[i], recv_sems.at[i], (x, y, 1 - c))
                for i in range(n)]

    def start(srcs, outs, send_sems, recv_sems):
        del srcs
        for cp in copies(outs, send_sems, recv_sems):
            cp.start()

    def finish(srcs, outs, send_sems, recv_sems):
        del srcs
        x, y, c, _, _ = _mesh_pos()
        for i in range(n):
            theirs = outs[i].at[1 - c]
            _remote(theirs, theirs, send_sems.at[i], recv_sems.at[i], (x, y, 1 - c)).wait_recv()
        for cp in copies(outs, send_sems, recv_sems):
            cp.wait_send()

    return _Exchange(arrs, [jax.ShapeDtypeStruct(a.shape, a.dtype) for a in arrs], n, start, finish,
                     aliases={i: i for i in range(n)})


def _gather_small(full):
    _, width = full.shape

    def copies(srcs, outs, send_sems, recv_sems):
        _, _, c, me, chips = _mesh_pos()
        mine = srcs[0].at[pl.ds(0, SMALL_ROWS)]
        own = pltpu.make_async_copy(mine, outs[0].at[me], send_sems.at[3])
        return own, [_remote(mine, outs[0].at[me], send_sems.at[j], recv_sems.at[j], (cx, cy, c))
                     for j, (cx, cy) in enumerate(chips)]

    def start(*refs):
        own, cps = copies(*refs)
        for cp in [own] + cps:
            cp.start()

    def finish(srcs, outs, send_sems, recv_sems):
        _, _, c, _, chips = _mesh_pos()
        for j, (cx, cy) in enumerate(chips):
            slot = outs[0].at[2 * cx + cy]
            _remote(slot, slot, send_sems.at[j], recv_sems.at[j], (cx, cy, c)).wait_recv()
        own, cps = copies(srcs, outs, send_sems, recv_sems)
        for cp in cps:
            cp.wait_send()
        own.wait()

    return _Exchange([full], [jax.ShapeDtypeStruct((NSHARD, SMALL_ROWS, width), full.dtype)], 4, start, finish)


def _add_sibling(mine, recv, c, name):
    _, half, width = recv.shape
    tr = _tile(half, 1024, 16)
    nb = half // tr

    def body(c_ref, a_ref, b_ref, o_ref):
        del c_ref
        o_ref[...] = (a_ref[...].astype(F32) + b_ref[...].astype(F32)).astype(o_ref.dtype)

    grid_spec = pltpu.PrefetchScalarGridSpec(
        num_scalar_prefetch=1, grid=(NSHARD, nb),
        in_specs=[pl.BlockSpec((1, tr, width), lambda j, r, c_ref: (j, c_ref[0] * nb + r, 0)),
                  pl.BlockSpec((1, tr, width), lambda j, r, c_ref: (j, r, 0))],
        out_specs=pl.BlockSpec((1, tr, width), lambda j, r, c_ref: (j, r, 0)))
    return pl.pallas_call(
        body, grid_spec=grid_spec, out_shape=jax.ShapeDtypeStruct(recv.shape, recv.dtype),
        compiler_params=_cparams(("parallel", "parallel")), name=name)(c, mine, recv)


def _sum_chips(parts, c, name):
    _, half, width = parts.shape
    tr = _tile(half, 1024, 16)

    def body(c_ref, p_ref, o_ref):
        del c_ref
        p = [p_ref[j].astype(F32) for j in range(NSHARD)]
        o_ref[0] = ((p[0] + p[1]) + p[2]) + p[3]

    grid_spec = pltpu.PrefetchScalarGridSpec(
        num_scalar_prefetch=1, grid=(half // tr,),
        in_specs=[pl.BlockSpec((NSHARD, tr, width), lambda r, c_ref: (0, r, 0))],
        out_specs=pl.BlockSpec((1, tr, width), lambda r, c_ref: (c_ref[0], r, 0)))
    return pl.pallas_call(
        body, grid_spec=grid_spec, out_shape=jax.ShapeDtypeStruct((2, half, width), F32),
        compiler_params=_cparams(("parallel",)), name=name)(c, parts)


def _unpack(flat, names, shapes):
    out, off = {}, 0
    for n in names:
        sz = _size(shapes[n])
        out[n] = flat[off:off + sz].reshape(shapes[n])
        off += sz
    return out


W_IN_COLS = 3080
W_IN_PAD = 3136


def _reorder_w_in_t(w):
    return jnp.concatenate([w[2048:5120], w[9248:12320], w[0:2048], w[8224:9248], w[5152:6688], w[6688:8224],
                            w[5120:5152], jnp.zeros((NP - 12320, D), w.dtype)], axis=0)


def _restore_w_in_t(g):
    return jnp.concatenate([g[6144:8192], g[0:3072], g[12288:12320], g[9216:10752], g[10752:12288], g[8192:9216],
                            g[3072:6144]], axis=0)


def _lru_group_weights(w):
    w4 = w.reshape(4, 4, 96, 96)
    eye = jnp.eye(4, dtype=w.dtype)
    return (w4[:, :, None, :, :] * eye[None, :, :, None, None]).transpose(0, 1, 3, 2, 4).reshape(4, LRU_G, LRU_G)


def _lru_group_blocks(g):
    g5 = g.reshape(4, 4, 96, 4, 96)
    return jnp.stack([g5[:, a, :, a, :] for a in range(4)], axis=1).reshape(16, 96, 96)


def _spread(a):
    return a.transpose(1, 0, 2).reshape(a.shape[1], NSHARD * a.shape[2])


def _split(a):
    return a.reshape(a.shape[0], NSHARD, a.shape[1] // NSHARD).transpose(1, 0, 2)


class _Reduction:
    def __init__(self, dist, parts, names):
        self.c, self.parts, self.names = dist.c, parts, names

    def swap(self):
        return _swap_halves(self.parts)

    def scatter(self, recv):
        return _scatter_chips([_add_sibling(p, r, self.c, "add_sibling_" + n)
                               for p, r, n in zip(self.parts, recv, self.names)])

    def share(self, landed):
        return _share_halves([_sum_chips(a, self.c, "sum_chips_" + n) for a, n in zip(landed, self.names)])

    def done(self, shared):
        return [a.reshape(2 * a.shape[1], a.shape[2]) for a in shared]


class _Dist:
    def __init__(self, w_in_shard, late_shards):
        self.c = lax.axis_index("c").astype(jnp.int32).reshape(1)
        self.w_in_shard = [w_in_shard]
        self.late_shards = late_shards

    def w_in_ride(self):
        return _gather_shards(self.w_in_shard, [True], relay=True)

    def w_in_arrived(self, got):
        (g_in,) = _with_own_slot(self.w_in_shard, got)
        return _reorder_w_in_t(g_in[:, 0:W_IN_COLS].reshape(NSHARD * W_IN_COLS, D))

    def weights_ride(self):
        return _gather_shards(self.late_shards, [True, True, False], relay=False)

    def weights_arrived(self, got):
        g_kv, g_rows, g_small = _with_own_slot(self.late_shards, got)
        out = {"w_kv": _spread(g_kv)}
        for n, lo_, hi_ in ROW_PIECES:
            out[n] = g_rows[:, lo_:hi_].reshape(NSHARD * (hi_ - lo_), D)
        out["ssd_conv_w"] = _spread(g_small[:, :, 0:768])
        out["ssd_norm_g"] = _spread(g_small[:, :, 768:896])
        out["lru_conv_w"] = _spread(g_small[:, :, 896:1280])
        return out

    def early_parts(self, grads):
        rows = jnp.concatenate([grads[n].reshape(NSHARD, hi_ - lo_, D) for n, lo_, hi_ in ROW_PIECES], axis=1)
        return [_split(grads["w_kv"]), rows]

    def late_parts(self, grads):
        rows = _restore_w_in_t(grads["w_in_rt"]).reshape(NSHARD, W_IN_COLS, D)
        return [jnp.pad(rows, ((0, 0), (0, W_IN_PAD - W_IN_COLS), (0, 0)))]


def _local_grads(x, mem, target, wts, dist=None):
    pad128 = lambda a: jnp.pad(a, ((0, 0), (0, 128 - a.shape[1])))

    if dist is None:
        (h,), _ = _norm_fwd(x, wts["norm_g"])
        w_in_rt = wts["w_in_rt"]
        proj = _mm(h, w_in_rt, F32, "in_proj", tb=True, tn=NP_TILE)
    else:
        (h,), arrived = _norm_fwd(x, wts["norm_g"], ride=dist.w_in_ride())
        w_in_rt = dist.w_in_arrived(arrived)
        proj, arrived = _mm(h, w_in_rt, F32, "in_proj", tb=True, tn=NP_TILE, ride=dist.weights_ride())
        wts = dict(wts, **dist.weights_arrived(arrived))
    wbs, wbl, wbm, wo, wkv = wts["w_br_ssd"], wts["w_br_lru"], wts["w_br_mem"], wts["w_out"], wts["w_kv"]
    wa, wx = _mx(_lru_group_weights(wts["lru_w_a"])), _mx(_lru_group_weights(wts["lru_w_x"]))
    ba, bx = wts["lru_b_a"].reshape(1, LRU_W), wts["lru_b_x"].reshape(1, LRU_W)
    dtb, alog = pad128(wts["ssd_dt_bias"]), pad128(wts["ssd_a_log"])
    dexp = jnp.repeat(wts["ssd_d"], 64, axis=1)
    ng = wts["ssd_norm_g"].reshape(1, SSD_W)
    xbc = _conv_fwd(proj, XBC, wts["ssd_conv_w"], wts["ssd_conv_b"], True, "ssd_conv_fwd")
    yssd, yraw, hprev = _ssd_fwd(xbc, proj, dtb, alog, dexp, ng)
    xl = _conv_fwd(proj, LX, wts["lru_conv_w"], wts["lru_conv_b"], False, "lru_conv_fwd")
    ylru, hs = _lru_fwd(xl, proj, wa, wx, ba, bx, wts["lru_lambda"])
    kk, vv, mn = _mem_kv_fwd(mem, wts["mem_norm_g"], wkv)
    ymem = _attn_fwd(proj, kk, vv)

    dproj, dx2, dx2m, merged, db0, db1, db2, loss_vec, dfg = _merge_fb(
        x, target, yssd, ylru, ymem, proj, wbs, wbl, wbm, wo, wts["final_g"].reshape(1, D))
    grads = {"final_g": dfg.reshape(D)}
    grads["w_out"] = _mm(merged, dx2m, _MXU, "dw_out", ta=True)
    grads["w_br_ssd"] = _mm(yssd, db0, _MXU, "dw_br_ssd", ta=True)
    grads["w_br_lru"] = _mm(ylru, db1, _MXU, "dw_br_lru", ta=True)
    grads["w_br_mem"] = _mm(ymem, db2, _MXU, "dw_br_mem", ta=True)
    dyssd = _mm(db0, wbs, F32, "dy_ssd", tb=True)
    dylru = _mm(db1, wbl, F32, "dy_lru", tb=True)
    dymem = _mm(db2, wbm, F32, "dy_mem", tb=True)

    dproj, dk, dv = _attn_bwd(proj, kk, vv, dymem, dproj)
    grads["w_kv"], grads["mem_norm_g"] = _mem_kv_bwd(mem, wts["mem_norm_g"], mn, wkv, dk, dv)

    early = None if dist is None else _Reduction(dist, dist.early_parts(grads), ["w_kv", "rows"])

    (dproj, dxl, dwa, dwx, dba, dbx, dlam), got = _lru_bwd(
        xl, proj, hs, dylru, dproj, wa, wx, ba, bx, wts["lru_lambda"], ride=early and early.swap())
    grads["lru_w_a"] = _lru_group_blocks(dwa)[None]
    grads["lru_w_x"] = _lru_group_blocks(dwx)[None]
    grads["lru_b_a"], grads["lru_b_x"] = dba.reshape(1, 16, 96), dbx.reshape(1, 16, 96)
    grads["lru_lambda"] = dlam
    (grads["lru_conv_w"], grads["lru_conv_b"]), _ = _conv_bwd_w(
        proj, LX, wts["lru_conv_w"], wts["lru_conv_b"], dxl, False, "lru_conv_bwd_w")
    dproj = _conv_bwd_x(dxl, wts["lru_conv_w"], dproj, LX, "lru_conv_bwd_x")

    (dproj, ddt, dxbc, dng, dda, ddd, ddtb), got = _ssd_bwd(
        xbc, proj, yraw, hprev, dyssd, dproj, dtb, alog, dexp, ng, ride=early and early.scatter(got))
    dproj = _put_block(ddt, dproj, DT, "put_ddt")
    grads["ssd_norm_g"] = dng.reshape(4, 512)
    grads["ssd_dt_bias"] = ddtb[:, 0:32]
    grads["ssd_a_log"] = (dda * -jnp.exp(alog))[:, 0:32]
    grads["ssd_d"] = ddd.reshape(32, 64).sum(axis=1)[None, :]
    (dpre, grads["ssd_conv_w"], grads["ssd_conv_b"]), got = _conv_bwd_w(
        proj, XBC, wts["ssd_conv_w"], wts["ssd_conv_b"], dxbc, True, "ssd_conv_bwd_w", ride=early and early.share(got))
    reduced = {} if dist is None else dict(zip(["w_kv", "rows"], early.done(got)))
    dproj = _conv_bwd_x(dpre, wts["ssd_conv_w"], dproj, XBC, "ssd_conv_bwd_x")

    grads["w_in_rt"] = _mm(dproj, h, _MXU, "dw_in", ta=True, tm=NP_TILE, tn=1024)
    if dist is None:
        dh = _mm(dproj, w_in_rt, F32, "dh", tn=1024, tk=NP_TILE)
        (grad_x, grads["norm_g"]), _ = _norm_bwd(x, wts["norm_g"], dh, dx2)
    else:
        late = _Reduction(dist, dist.late_parts(grads), ["w_in"])
        got = _run_exchange(late.swap(), "swap_halves_w_in")
        dh, got = _mm(dproj, w_in_rt, F32, "dh", tn=1024, tk=NP_TILE, ride=late.scatter(got))
        (grad_x, grads["norm_g"]), _ = _norm_bwd(x, wts["norm_g"], dh, dx2)
        reduced["w_in"] = late.done(_run_exchange(late.share(got), "share_halves_w_in"))[0]
    return jnp.sum(loss_vec), grad_x, grads, reduced


def kernel(x, mem, norm_g, w_in, ssd_conv_w, ssd_conv_b, ssd_dt_bias, ssd_a_log, ssd_d, ssd_norm_g, lru_conv_w, lru_conv_b, lru_w_a, lru_b_a, lru_w_x, lru_b_x, lru_lambda, mem_norm_g, w_kv, w_br_ssd, w_br_lru, w_br_mem, w_out, final_g, loss_target, m_norm_g, m_w_in, m_ssd_conv_w, m_ssd_conv_b, m_ssd_dt_bias, m_ssd_a_log, m_ssd_d, m_ssd_norm_g, m_lru_conv_w, m_lru_conv_b, m_lru_w_a, m_lru_b_a, m_lru_w_x, m_lru_b_x, m_lru_lambda, m_mem_norm_g, m_w_kv, m_w_br_ssd, m_w_br_lru, m_w_br_mem, m_w_out, m_final_g, v_norm_g, v_w_in, v_ssd_conv_w, v_ssd_conv_b, v_ssd_dt_bias, v_ssd_a_log, v_ssd_d, v_ssd_norm_g, v_lru_conv_w, v_lru_conv_b, v_lru_w_a, v_lru_b_a, v_lru_w_x, v_lru_b_x, v_lru_lambda, v_mem_norm_g, v_w_kv, v_w_br_ssd, v_w_br_lru, v_w_br_mem, v_w_out, v_final_g):
    given = dict(locals())

    rows_w = jnp.concatenate([w_br_ssd[0], w_br_lru[0], w_br_mem[0], w_out[0]], axis=0)
    small_w = jnp.concatenate([ssd_conv_w[0], ssd_norm_g[0], lru_conv_w[0]], axis=1)
    w_in_t = jnp.pad(_mx(w_in[0].T), ((0, W_IN_PAD - W_IN_COLS), (0, 0)))
    dist = _Dist(w_in_t, [_mx(w_kv[0]), _mx(rows_w), small_w])
    wts = {n: given[n] for n in REPL}
    wts["lru_w_a"], wts["lru_w_x"] = lru_w_a[0], lru_w_x[0]

    loss_part, grad_x, grads, reduced = _local_grads(x[0], mem[0], loss_target[0], wts, dist)
    loss = lax.psum(loss_part, ("x", "y", "c"))

    repl_flat = jnp.concatenate([grads[n].reshape(-1) for n in REPL])
    repl_flat = jnp.pad(repl_flat, (0, NSHARD * SMALL_Q - repl_flat.shape[0])).reshape(NSHARD, SMALL_Q)
    shard_small = jnp.concatenate([_split(grads[n]).reshape(NSHARD, -1) for n in SMALL_SHARDED], axis=1)
    p_small = jnp.concatenate(
        [repl_flat, shard_small, jnp.zeros((NSHARD, SMALL_BUF_ROWS * PACK_W - SMALL_Q - 5120), F32)], axis=1)
    small = _Reduction(dist, [p_small.reshape(NSHARD, SMALL_BUF_ROWS, PACK_W)], ["small"])
    got = _run_exchange(small.swap(), "swap_halves_small")
    got = _run_exchange(small.scatter(got), "scatter_chips_small")
    got = _run_exchange(small.share(got), "share_halves_small")
    r_small = small.done(got)[0]
    repl_all = _run_exchange(_gather_small(r_small), "gather_small")[0].reshape(-1)

    g_shard = {"w_kv": reduced["w_kv"]}
    for n, lo_, hi_ in ROW_PIECES:
        g_shard[n] = reduced["rows"][lo_:hi_]
    g_shard.update(_unpack(r_small.reshape(-1)[SMALL_Q:], SMALL_SHARDED, SHARD_SHAPE))
    g_repl = _unpack(repl_all, REPL, REPL_SHAPE)

    out_g, out_d, out_m, out_v = {}, {}, {}, {}
    for n in WEIGHTS:
        w_full = given[n]
        if n == "w_in":
            g2 = reduced["w_in"][0:W_IN_COLS]
            d, mo, vo = _adamw(w_in[0].T, g2, m_w_in[0].T, v_w_in[0].T, "adamw_w_in")
            out_g[n], out_d[n], out_m[n], out_v[n] = [a.T[None] for a in (g2, d, mo, vo)]
            continue
        g = (g_shard[n] if n in SHARDED else g_repl[n]).reshape(w_full.shape)
        cols = w_full.shape[-1]
        as2d = lambda a: a.reshape(-1, cols)
        d, mo, vo = _adamw(as2d(w_full), as2d(g), as2d(given["m_" + n]), as2d(given["v_" + n]), "adamw_" + n)
        out_g[n] = g
        out_d[n], out_m[n], out_v[n] = d.reshape(w_full.shape), mo.reshape(w_full.shape), vo.reshape(w_full.shape)

    return (loss, grad_x[None], *[out_g[n] for n in WEIGHTS], *[out_d[n] for n in WEIGHTS],
            *[out_m[n] for n in WEIGHTS], *[out_v[n] for n in WEIGHTS])
```

```python
import jax
import jax.numpy as jnp
from jax import lax
from jax.experimental import pallas as pl
from jax.experimental.pallas import tpu as pltpu

F32 = jnp.float32
_MXU = jnp.bfloat16
_HI = lax.Precision.HIGHEST
MESH = pl.DeviceIdType.MESH

D = 1024
EPS = 1e-6
MEM_HEADS = 4
MEM_HD = 256
LRU_C = 8.0
SSD_L = 128
SSD_W = 2048
LRU_W = 1536
NSHARD = 4

XBC = (0, 3072)
GL = (3072, 3072)
Z = (6144, 2048)
Q = (8192, 1024)
LG = (9216, 1536)
LX = (10752, 1536)
DT = (12288, 256)
NP = 12544
NP_TILE = 1792

ADAM_LR = 0.001
ADAM_B1 = 0.9
ADAM_B2 = 0.999
ADAM_EPS = 1e-08
ADAM_WD = 0.01
ADAM_STEP = 10

VMEM_LIMIT = 56 * 1024 * 1024

SHARDED = ("w_in", "ssd_conv_w", "ssd_norm_g", "lru_conv_w", "w_kv", "w_br_ssd", "w_br_lru", "w_br_mem", "w_out")
SHARD_SHAPE = {"w_in": (1024, 3080), "ssd_conv_w": (4, 768), "ssd_norm_g": (4, 128), "lru_conv_w": (4, 384),
               "w_kv": (1024, 512), "w_br_ssd": (512, 1024), "w_br_lru": (384, 1024), "w_br_mem": (256, 1024),
               "w_out": (256, 1024)}
REPL = ("norm_g", "ssd_conv_b", "ssd_dt_bias", "ssd_a_log", "ssd_d", "lru_conv_b", "lru_w_a", "lru_b_a",
        "lru_w_x", "lru_b_x", "lru_lambda", "mem_norm_g", "final_g")
REPL_SHAPE = {"norm_g": (1, 1024), "ssd_conv_b": (1, 3072), "ssd_dt_bias": (1, 32), "ssd_a_log": (1, 32),
              "ssd_d": (1, 32), "lru_conv_b": (1, 1536), "lru_w_a": (1, 16, 96, 96), "lru_b_a": (1, 16, 96),
              "lru_w_x": (1, 16, 96, 96), "lru_b_x": (1, 16, 96), "lru_lambda": (1, 1536),
              "mem_norm_g": (1, 1024), "final_g": (1024,)}
WEIGHTS = ("norm_g", "w_in", "ssd_conv_w", "ssd_conv_b", "ssd_dt_bias", "ssd_a_log", "ssd_d", "ssd_norm_g",
           "lru_conv_w", "lru_conv_b", "lru_w_a", "lru_b_a", "lru_w_x", "lru_b_x", "lru_lambda", "mem_norm_g",
           "w_kv", "w_br_ssd", "w_br_lru", "w_br_mem", "w_out", "final_g")

ROW_PIECES = (("w_br_ssd", 0, 512), ("w_br_lru", 512, 896), ("w_br_mem", 896, 1152), ("w_out", 1152, 1408))
SMALL_SHARDED = ("ssd_conv_w", "ssd_norm_g", "lru_conv_w")
PACK_W = 512
SMALL_ROWS = 152
SMALL_Q = SMALL_ROWS * PACK_W
SMALL_BUF_ROWS = 176


def _size(shape):
    n = 1
    for s in shape:
        n *= s
    return n


def _sigmoid(x):
    return 0.5 * jnp.tanh(0.5 * x) + 0.5


def _silu(x):
    return x * _sigmoid(x)


def _dsilu(x):
    s = _sigmoid(x)
    return s * (1.0 + x * (1.0 - s))


def _softplus(x):
    return jnp.maximum(x, 0.0) + jnp.log(1.0 + jnp.exp(-jnp.abs(x)))


def _one_minus_sq(log_a, a):
    x = 2.0 * log_a
    series = -x * (1.0 + x * (0.5 + x * (1.0 / 6.0 + x * (1.0 / 24.0))))
    return jnp.where(x > -0.03, series, 1.0 - a * a)


def _dot(a, b, precision=None):
    return jnp.dot(a, b, preferred_element_type=F32, precision=precision)


def _dot_nt(a, b):
    return lax.dot_general(a, b, (((1,), (1,)), ((), ())), preferred_element_type=F32)


def _dot_tn(a, b):
    return lax.dot_general(a, b, (((0,), (0,)), ((), ())), preferred_element_type=F32)


def _mx(a):
    return a.astype(_MXU)


def _cparams(sem):
    return pltpu.CompilerParams(dimension_semantics=sem, vmem_limit_bytes=VMEM_LIMIT)


def _tile(n, want, mult=128):
    if n <= want:
        return n
    for t in range(want - want % mult, 0, -mult):
        if n % t == 0:
            return t
    raise ValueError((n, want, mult))


def _mm(a, b, out_dtype, name, ta=False, tb=False, tm=1024, tn=1280, tk=1024, ride=None):
    k, m = a.shape if ta else a.shape[::-1]
    k2, n = b.shape[::-1] if tb else b.shape
    assert k == k2
    tm, tn, tk = _tile(m, tm), _tile(n, tn), _tile(k, tk)
    nk = k // tk
    contract = (((0 if ta else 1,), (1 if tb else 0,)), ((), ()))

    def body(a_ref, b_ref, o_ref, acc_ref):
        kk = pl.program_id(2)

        @pl.when(kk == 0)
        def _():
            acc_ref[...] = jnp.zeros_like(acc_ref)

        acc_ref[...] += lax.dot_general(a_ref[...], b_ref[...], contract, preferred_element_type=F32)

        @pl.when(kk == nk - 1)
        def _():
            o_ref[...] = acc_ref[...].astype(o_ref.dtype)

    a_spec = pl.BlockSpec((tk, tm), lambda i, j, kk: (kk, i)) if ta else pl.BlockSpec((tm, tk), lambda i, j, kk: (i, kk))
    b_spec = pl.BlockSpec((tn, tk), lambda i, j, kk: (j, kk)) if tb else pl.BlockSpec((tk, tn), lambda i, j, kk: (kk, j))
    outs, carried = _pcall(
        body, ride, (a, b), grid=(m // tm, n // tn, nk),
        in_specs=[a_spec, b_spec],
        out_specs=[pl.BlockSpec((tm, tn), lambda i, j, kk: (i, j))],
        out_shape=[jax.ShapeDtypeStruct((m, n), out_dtype)],
        scratch_shapes=[pltpu.VMEM((tm, tn), F32)],
        sem=("parallel", "parallel", "arbitrary"), name=name)
    return outs[0] if ride is None else (outs[0], carried)


def _norm_fwd(x, g, ride=None):
    s = x.shape[0]
    ts = _tile(s, 512)

    def body(x_ref, g_ref, h_ref):
        xv = x_ref[...]
        r = lax.rsqrt(jnp.mean(xv * xv, axis=-1, keepdims=True) + EPS)
        h_ref[...] = (xv * r * g_ref[...]).astype(h_ref.dtype)

    return _pcall(
        body, ride, (x, g), grid=(s // ts,),
        in_specs=[pl.BlockSpec((ts, D), lambda i: (i, 0)), pl.BlockSpec((1, D), lambda i: (0, 0))],
        out_specs=[pl.BlockSpec((ts, D), lambda i: (i, 0))],
        out_shape=[jax.ShapeDtypeStruct((s, D), _MXU)], scratch_shapes=[], sem=("parallel",), name="norm_fwd")


def _norm_bwd(x, g, dh, dx2, ride=None):
    s = x.shape[0]
    ts = _tile(s, 1024)

    def body(x_ref, g_ref, dh_ref, dx2_ref, gx_ref, dg_ref):
        @pl.when(pl.program_id(0) == 0)
        def _():
            dg_ref[...] = jnp.zeros_like(dg_ref)

        xv = x_ref[...]
        r = lax.rsqrt(jnp.mean(xv * xv, axis=-1, keepdims=True) + EPS)
        xhat = xv * r
        dh_v = dh_ref[...]
        dg_ref[...] += jnp.sum(dh_v * xhat, axis=0, keepdims=True)
        dxh = dh_v * g_ref[...]
        gx_ref[...] = dx2_ref[...] + r * (dxh - xhat * jnp.mean(dxh * xhat, axis=-1, keepdims=True))

    row = pl.BlockSpec((ts, D), lambda i: (i, 0))
    vec = pl.BlockSpec((1, D), lambda i: (0, 0))
    return _pcall(
        body, ride, (x, g, dh, dx2), grid=(s // ts,), in_specs=[row, vec, row, row], out_specs=[row, vec],
        out_shape=[jax.ShapeDtypeStruct((s, D), F32), jax.ShapeDtypeStruct((1, D), F32)],
        scratch_shapes=[], sem=("arbitrary",), name="norm_bwd")


CONV_TS = 512
CONV_RB = 16
CONV_LC = 256


def _fold8(v):
    acc = v[0:8]
    for r0 in range(8, v.shape[0], 8):
        acc = acc + v[r0:r0 + 8]
    return acc


def _conv_fwd(src, blk, w, b, act, name):
    s = src.shape[0]
    off, width = blk
    cb = off // width
    ts = _tile(s, CONV_TS)

    def body(x_ref, w_ref, b_ref, o_ref, ext_ref):
        @pl.when(pl.program_id(0) == 0)
        def _():
            ext_ref[0:8, :] = jnp.zeros((8, width), F32)

        ext_ref[8:8 + ts, :] = x_ref[...]
        for l0 in range(0, width, CONV_LC):
            ls = slice(l0, l0 + CONV_LC)
            taps = [w_ref[k:k + 1, ls] for k in range(4)]
            bias = b_ref[:, ls]
            for r0 in range(0, ts, CONV_RB):
                pre = bias
                for k in range(4):
                    pre = pre + taps[k] * ext_ref[5 + k + r0:5 + k + r0 + CONV_RB, ls]
                o_ref[r0:r0 + CONV_RB, ls] = _silu(pre) if act else pre
        ext_ref[0:8, :] = x_ref[ts - 8:ts, :]

    return pl.pallas_call(
        body, grid=(s // ts,),
        in_specs=[pl.BlockSpec((ts, width), lambda i: (i, cb)), pl.BlockSpec((4, width), lambda i: (0, 0)),
                  pl.BlockSpec((1, width), lambda i: (0, 0))],
        out_specs=pl.BlockSpec((ts, width), lambda i: (i, 0)),
        out_shape=jax.ShapeDtypeStruct((s, width), F32),
        scratch_shapes=[pltpu.VMEM((ts + 8, width), F32)],
        compiler_params=_cparams(("arbitrary",)), name=name)(src, w, b)


def _conv_bwd_w(src, blk, w, b, dout, act, name, ride=None):
    s = src.shape[0]
    off, width = blk
    cb = off // width
    ts = _tile(s, CONV_TS)

    def body(x_ref, w_ref, b_ref, do_ref, *rest):
        if act:
            dpre_ref, dw_ref, db_ref, ext_ref = rest
        else:
            dw_ref, db_ref, ext_ref = rest

        @pl.when(pl.program_id(0) == 0)
        def _():
            ext_ref[0:8, :] = jnp.zeros((8, width), F32)
            dw_ref[...] = jnp.zeros_like(dw_ref)
            db_ref[...] = jnp.zeros_like(db_ref)

        ext_ref[8:8 + ts, :] = x_ref[...]
        for l0 in range(0, width, CONV_LC):
            ls = slice(l0, l0 + CONV_LC)
            taps = [w_ref[k:k + 1, ls] for k in range(4)]
            bias = b_ref[:, ls]
            acc_b = jnp.zeros((8, CONV_LC), F32)
            acc_w = [jnp.zeros((8, CONV_LC), F32) for _ in range(4)]
            for r0 in range(0, ts, CONV_RB):
                xs = [ext_ref[5 + k + r0:5 + k + r0 + CONV_RB, ls] for k in range(4)]
                dpre = do_ref[r0:r0 + CONV_RB, ls]
                if act:
                    pre = bias
                    for k in range(4):
                        pre = pre + taps[k] * xs[k]
                    dpre = dpre * _dsilu(pre)
                    dpre_ref[r0:r0 + CONV_RB, ls] = dpre
                acc_b = acc_b + _fold8(dpre)
                for k in range(4):
                    acc_w[k] = acc_w[k] + _fold8(dpre * xs[k])
            db_ref[:, ls] += jnp.sum(acc_b, axis=0, keepdims=True)
            for k in range(4):
                dw_ref[k:k + 1, ls] += jnp.sum(acc_w[k], axis=0, keepdims=True)
        ext_ref[0:8, :] = x_ref[ts - 8:ts, :]

    row = pl.BlockSpec((ts, width), lambda i: (i, 0))
    outs = [pl.BlockSpec((4, width), lambda i: (0, 0)), pl.BlockSpec((1, width), lambda i: (0, 0))]
    shapes = [jax.ShapeDtypeStruct((4, width), F32), jax.ShapeDtypeStruct((1, width), F32)]
    if act:
        outs = [row] + outs
        shapes = [jax.ShapeDtypeStruct((s, width), F32)] + shapes
    return _pcall(
        body, ride, (src, w, b, dout), grid=(s // ts,),
        in_specs=[pl.BlockSpec((ts, width), lambda i: (i, cb)), pl.BlockSpec((4, width), lambda i: (0, 0)),
                  pl.BlockSpec((1, width), lambda i: (0, 0)), row],
        out_specs=outs, out_shape=shapes,
        scratch_shapes=[pltpu.VMEM((ts + 8, width), F32)], sem=("arbitrary",), name=name)


def _conv_bwd_x(dpre, w, dproj, blk, name):
    s = dpre.shape[0]
    off, width = blk
    cb = off // width
    ts = _tile(s, CONV_TS)
    nt = s // ts

    def body(dp_ref, w_ref, dproj_hbm, o_ref, ext_ref):
        del dproj_hbm

        @pl.when(pl.program_id(0) == 0)
        def _():
            ext_ref[ts:ts + 8, :] = jnp.zeros((8, width), F32)

        ext_ref[0:ts, :] = dp_ref[...]
        for l0 in range(0, width, CONV_LC):
            ls = slice(l0, l0 + CONV_LC)
            taps = [w_ref[k:k + 1, ls] for k in range(4)]
            for r0 in range(0, ts, CONV_RB):
                acc = taps[0] * ext_ref[3 + r0:3 + r0 + CONV_RB, ls]
                for k in range(1, 4):
                    acc = acc + taps[k] * ext_ref[3 - k + r0:3 - k + r0 + CONV_RB, ls]
                o_ref[r0:r0 + CONV_RB, ls] = acc.astype(o_ref.dtype)
        ext_ref[ts:ts + 8, :] = dp_ref[0:8, :]

    return pl.pallas_call(
        body, grid=(nt,),
        in_specs=[pl.BlockSpec((ts, width), lambda i: (nt - 1 - i, 0)), pl.BlockSpec((4, width), lambda i: (0, 0)),
                  pl.BlockSpec(memory_space=pl.ANY)],
        out_specs=pl.BlockSpec((ts, width), lambda i: (nt - 1 - i, cb)),
        out_shape=jax.ShapeDtypeStruct(dproj.shape, dproj.dtype),
        scratch_shapes=[pltpu.VMEM((ts + 8, width), F32)],
        input_output_aliases={2: 0},
        compiler_params=_cparams(("arbitrary",)), name=name)(dpre, w, dproj)


def _ssd_decay(a_cs, acst_ref, h, causal, lane_l):
    col = jnp.sum(jnp.where(lane_l == h, a_cs, 0.0), axis=1, keepdims=True)
    row = acst_ref[h:h + 1, :]
    return jnp.where(causal, jnp.exp(jnp.minimum(col - row, 0.0)), 0.0)


def _split3(x):
    hi = x.astype(jnp.bfloat16)
    rest = x - hi.astype(F32)
    mid = rest.astype(jnp.bfloat16)
    return jnp.concatenate([hi, mid, (rest - mid.astype(F32)).astype(jnp.bfloat16)], axis=1)


def _spread_matrix():
    col = jnp.arange(128, dtype=jnp.int32)[:, None]
    e64 = (col == jnp.arange(SSD_W, dtype=jnp.int32)[None, :] // 64).astype(jnp.bfloat16)
    return jnp.tile(e64, (3, 1))


def _ssd_common(dt_ref, dtb_ref, alog_ref, e64_ref, acst_ref, dtx_ref, acx_ref):
    ll = SSD_L
    dt = _softplus(dt_ref[:, 0:128] + dtb_ref[...])
    a_neg = -jnp.exp(alog_ref[...])
    ri = lax.broadcasted_iota(jnp.int32, (ll, ll), 0)
    ci = lax.broadcasted_iota(jnp.int32, (ll, ll), 1)
    causal = ri >= ci
    a_cs = _dot(causal.astype(F32), dt * a_neg, _HI)
    acst_ref[...] = a_cs.T
    both = _dot(jnp.concatenate([_split3(dt), _split3(a_cs)], axis=0), e64_ref[...])
    dtx_ref[...] = both[0:ll]
    acx_ref[...] = both[ll:2 * ll]
    lane_l = lax.broadcasted_iota(jnp.int32, (ll, 128), 1)
    return dt, a_neg, a_cs, causal, ri, lane_l, lane_l < 64


def _ssd_fwd(xbc, proj, dtb, alog, dexp, ng):
    s = xbc.shape[0]
    ll = SSD_L
    nc = s // ll
    e64 = _spread_matrix()

    def body(xbc_ref, dt_ref, z_ref, dtb_ref, alog_ref, dexp_ref, ng_ref, e64_ref,
             yssd_ref, yraw_ref, hprev_ref, ht_ref, acst_ref, dtx_ref, acx_ref):
        @pl.when(pl.program_id(0) == 0)
        def _():
            ht_ref[...] = jnp.zeros_like(ht_ref)

        hprev_ref[0] = ht_ref[...]
        _, _, a_cs, causal, _, lane_l, lo = _ssd_common(dt_ref, dtb_ref, alog_ref, e64_ref, acst_ref, dtx_ref, acx_ref)
        for g in range(4):
            bg = _mx(xbc_ref[:, 2048 + 128 * g:2176 + 128 * g])
            cg = _mx(xbc_ref[:, 2560 + 128 * g:2688 + 128 * g])
            cbm = _dot_nt(cg, bg)
            for jj in range(4):
                j = 4 * g + jj
                sl = slice(128 * j, 128 * j + 128)
                xp = xbc_ref[:, sl]
                acx = acx_ref[:, sl]
                a_last = acx_ref[ll - 1:ll, sl]
                xdt = xp * dtx_ref[:, sl]
                acc = None
                for hh in range(2):
                    dec = _ssd_decay(a_cs, acst_ref, 2 * j + hh, causal, lane_l)
                    xm = jnp.where(lo if hh == 0 else jnp.logical_not(lo), xdt, 0.0)
                    t = _dot(_mx(dec * cbm), _mx(xm))
                    acc = t if acc is None else acc + t
                ht = ht_ref[j]
                y = acc + _dot(cg, _mx(ht)) * jnp.exp(acx) + xp * dexp_ref[:, sl]
                yraw_ref[:, sl] = y
                st = _dot_tn(bg, _mx(xdt * jnp.exp(a_last - acx)))
                ht_ref[j] = ht * jnp.exp(a_last) + st
        for g in range(4):
            sl = slice(512 * g, 512 * g + 512)
            yg = yraw_ref[:, sl] * _silu(z_ref[:, sl])
            r = lax.rsqrt(jnp.mean(yg * yg, axis=-1, keepdims=True) + EPS)
            yssd_ref[:, sl] = (yg * r * ng_ref[:, sl]).astype(yssd_ref.dtype)

    vec = lambda w: pl.BlockSpec((1, w), lambda c: (0, 0))
    return pl.pallas_call(
        body, grid=(nc,),
        in_specs=[pl.BlockSpec((ll, 3072), lambda c: (c, 0)),
                  pl.BlockSpec((ll, DT[1]), lambda c: (c, DT[0] // DT[1])),
                  pl.BlockSpec((ll, Z[1]), lambda c: (c, Z[0] // Z[1])),
                  vec(128), vec(128), vec(2048), vec(2048),
                  pl.BlockSpec(e64.shape, lambda c: (0, 0))],
        out_specs=[pl.BlockSpec((ll, 2048), lambda c: (c, 0)), pl.BlockSpec((ll, 2048), lambda c: (c, 0)),
                   pl.BlockSpec((1, 16, 128, 128), lambda c: (c, 0, 0, 0))],
        out_shape=[jax.ShapeDtypeStruct((s, 2048), _MXU), jax.ShapeDtypeStruct((s, 2048), F32),
                   jax.ShapeDtypeStruct((nc, 16, 128, 128), F32)],
        scratch_shapes=[pltpu.VMEM((16, 128, 128), F32), pltpu.VMEM((128, ll), F32),
                        pltpu.VMEM((ll, 2048), F32), pltpu.VMEM((ll, 2048), F32)],
        compiler_params=_cparams(("arbitrary",)), name="ssd_fwd")(xbc, proj, proj, dtb, alog, dexp, ng, e64)


def _ssd_bwd(xbc, proj, yraw, hprev, dyssd, dproj, dtb, alog, dexp, ng, ride=None):
    s = xbc.shape[0]
    ll = SSD_L
    nc = s // ll
    e64 = _spread_matrix()

    def body(xbc_ref, dt_ref, z_ref, yraw_ref, hprev_ref, dy_ref, dproj_hbm, dtb_ref, alog_ref, dexp_ref, ng_ref,
             e64_ref,
             dz_ref, ddt_ref, dxbc_ref, dng_ref, dda_ref, ddd_ref, ddtb_ref,
             dht_ref, acst_ref, dtx_ref, acx_ref, dyr_ref, rowt_ref):
        del dproj_hbm

        @pl.when(pl.program_id(0) == 0)
        def _():
            dht_ref[...] = jnp.zeros_like(dht_ref)
            dng_ref[...] = jnp.zeros_like(dng_ref)
            dda_ref[...] = jnp.zeros_like(dda_ref)
            ddd_ref[...] = jnp.zeros_like(ddd_ref)
            ddtb_ref[...] = jnp.zeros_like(ddtb_ref)
            rowt_ref[...] = jnp.zeros_like(rowt_ref)

        for g in range(4):
            sl = slice(512 * g, 512 * g + 512)
            zz = z_ref[:, sl]
            yr = yraw_ref[:, sl]
            sz = _silu(zz)
            yg = yr * sz
            r = lax.rsqrt(jnp.mean(yg * yg, axis=-1, keepdims=True) + EPS)
            yhat = yg * r
            dyv = dy_ref[:, sl]
            dng_ref[:, sl] += jnp.sum(dyv * yhat, axis=0, keepdims=True)
            dyh = dyv * ng_ref[:, sl]
            dyg = r * (dyh - yhat * jnp.mean(dyh * yhat, axis=-1, keepdims=True))
            dz_ref[:, sl] = (dyg * yr * _dsilu(zz)).astype(dz_ref.dtype)
            dyr_ref[:, sl] = dyg * sz

        dt, a_neg, a_cs, causal, ri, lane_l, lo = _ssd_common(dt_ref, dtb_ref, alog_ref, e64_ref,
                                                              acst_ref, dtx_ref, acx_ref)
        lane_1 = lax.broadcasted_iota(jnp.int32, (1, 128), 1)
        da_col = jnp.zeros((ll, 128), F32)
        ddt_x = jnp.zeros((ll, 128), F32)
        last = jnp.zeros((1, 128), F32)
        for g in range(4):
            bg = _mx(xbc_ref[:, 2048 + 128 * g:2176 + 128 * g])
            cg = _mx(xbc_ref[:, 2560 + 128 * g:2688 + 128 * g])
            cbm = _dot_nt(cg, bg)
            dcb = jnp.zeros((ll, ll), F32)
            db_g = jnp.zeros((ll, 128), F32)
            dc_g = jnp.zeros((ll, 128), F32)
            for jj in range(4):
                j = 4 * g + jj
                sl = slice(128 * j, 128 * j + 128)
                xp = xbc_ref[:, sl]
                dtx = dtx_ref[:, sl]
                acx = acx_ref[:, sl]
                a_last = acx_ref[ll - 1:ll, sl]
                ea = jnp.exp(acx)
                dte = jnp.exp(a_last - acx)
                cd = jnp.exp(a_last)
                xdt = xp * dtx
                xdt_m = _mx(xdt)
                dy = dyr_ref[:, sl]
                ht = hprev_ref[0, j]
                dhn = dht_ref[j]
                dhn_m = _mx(dhn)
                gmat = _dot(bg, dhn_m)
                dxdt = gmat * dte
                for hh in range(2):
                    h = 2 * j + hh
                    dec = _ssd_decay(a_cs, acst_ref, h, causal, lane_l)
                    mm = dec * cbm
                    dym = _mx(jnp.where(lo if hh == 0 else jnp.logical_not(lo), dy, 0.0))
                    dxdt = dxdt + _dot_tn(_mx(mm), dym)
                    dm = _dot_nt(dym, xdt_m)
                    dcb = dcb + dm * dec
                    qq = dm * mm
                    da_col = da_col + jnp.where(lane_l == h, jnp.sum(qq, axis=1, keepdims=True), 0.0)
                    rowt_ref[h:h + 1, :] = jnp.sum(qq, axis=0, keepdims=True)
                ch = _dot(cg, _mx(ht))
                dyea = dy * ea
                dyea_m = _mx(dyea)
                xw_m = _mx(xdt * dte)
                dc_g = dc_g + _dot_nt(dyea_m, _mx(ht))
                db_g = db_g + _dot_nt(xw_m, dhn_m)
                wl = xdt * gmat * dte
                lane_a = dyea * ch - wl
                lane_b = dxdt * xp
                lane_c = jnp.sum(dhn * ht, axis=0, keepdims=True) * cd + jnp.sum(wl, axis=0, keepdims=True)
                for hh in range(2):
                    h = 2 * j + hh
                    mine = lo if hh == 0 else jnp.logical_not(lo)
                    da_col = da_col + jnp.where(
                        lane_l == h, jnp.sum(jnp.where(mine, lane_a, 0.0), axis=1, keepdims=True), 0.0)
                    ddt_x = ddt_x + jnp.where(
                        lane_l == h, jnp.sum(jnp.where(mine, lane_b, 0.0), axis=1, keepdims=True), 0.0)
                    mine_1 = (lane_1 < 64) if hh == 0 else (lane_1 >= 64)
                    last = last + jnp.where(
                        lane_1 == h, jnp.sum(jnp.where(mine_1, lane_c, 0.0), axis=1, keepdims=True), 0.0)
                dht_ref[j] = dhn * cd + _dot_tn(cg, dyea_m)
                dxbc_ref[:, sl] = dxdt * dtx + dy * dexp_ref[:, sl]
                ddd_ref[:, sl] += jnp.sum(dy * xp, axis=0, keepdims=True)
            dcb_m = _mx(dcb)
            dxbc_ref[:, 2048 + 128 * g:2176 + 128 * g] = db_g + _dot_tn(dcb_m, cg)
            dxbc_ref[:, 2560 + 128 * g:2688 + 128 * g] = dc_g + _dot(dcb_m, bg)

        da_cs = da_col - rowt_ref[...].T
        da_cs = da_cs + jnp.where(lax.broadcasted_iota(jnp.int32, (ll, 128), 0) == ll - 1, last, 0.0)
        d_dta = _dot((ri <= lax.broadcasted_iota(jnp.int32, (ll, ll), 1)).astype(F32), da_cs, _HI)
        ddt = d_dta * a_neg + ddt_x
        dda_ref[...] += jnp.sum(d_dta * dt, axis=0, keepdims=True)
        ddt_raw = ddt * _sigmoid(dt_ref[:, 0:128] + dtb_ref[...])
        ddtb_ref[...] += jnp.sum(ddt_raw, axis=0, keepdims=True)
        ddt_ref[:, 0:128] = ddt_raw.astype(ddt_ref.dtype)
        ddt_ref[:, 128:DT[1]] = jnp.zeros((ll, DT[1] - 128), ddt_ref.dtype)

    rev = lambda c: nc - 1 - c
    vec = lambda w: pl.BlockSpec((1, w), lambda c: (0, 0))
    row = lambda w: pl.BlockSpec((ll, w), lambda c: (rev(c), 0))
    return _pcall(
        body, ride, (xbc, proj, proj, yraw, hprev, dyssd, dproj, dtb, alog, dexp, ng, e64), grid=(nc,),
        in_specs=[row(3072),
                  pl.BlockSpec((ll, DT[1]), lambda c: (rev(c), DT[0] // DT[1])),
                  pl.BlockSpec((ll, Z[1]), lambda c: (rev(c), Z[0] // Z[1])),
                  row(2048),
                  pl.BlockSpec((1, 16, 128, 128), lambda c: (rev(c), 0, 0, 0)),
                  row(2048),
                  pl.BlockSpec(memory_space=pl.ANY),
                  vec(128), vec(128), vec(2048), vec(2048),
                  pl.BlockSpec(e64.shape, lambda c: (0, 0))],
        out_specs=[pl.BlockSpec((ll, Z[1]), lambda c: (rev(c), Z[0] // Z[1])),
                   row(DT[1]),
                   row(3072), vec(2048), vec(128), vec(2048), vec(128)],
        out_shape=[jax.ShapeDtypeStruct(dproj.shape, dproj.dtype), jax.ShapeDtypeStruct((s, DT[1]), dproj.dtype),
                   jax.ShapeDtypeStruct((s, 3072), F32), jax.ShapeDtypeStruct((1, 2048), F32),
                   jax.ShapeDtypeStruct((1, 128), F32), jax.ShapeDtypeStruct((1, 2048), F32),
                   jax.ShapeDtypeStruct((1, 128), F32)],
        scratch_shapes=[pltpu.VMEM((16, 128, 128), F32), pltpu.VMEM((128, ll), F32),
                        pltpu.VMEM((ll, 2048), F32), pltpu.VMEM((ll, 2048), F32), pltpu.VMEM((ll, 2048), F32),
                        pltpu.VMEM((128, ll), F32)],
        aliases={6: 0}, sem=("arbitrary",), name="ssd_bwd")


def _put_block(src, dproj, blk, name):
    s = src.shape[0]
    off, width = blk
    cb = off // width
    ts = _tile(s, 1024)

    def body(s_ref, dproj_hbm, o_ref):
        del dproj_hbm
        o_ref[...] = s_ref[...]

    return pl.pallas_call(
        body, grid=(s // ts,),
        in_specs=[pl.BlockSpec((ts, width), lambda i: (i, 0)), pl.BlockSpec(memory_space=pl.ANY)],
        out_specs=pl.BlockSpec((ts, width), lambda i: (i, cb)),
        out_shape=jax.ShapeDtypeStruct(dproj.shape, dproj.dtype),
        input_output_aliases={1: 0},
        compiler_params=_cparams(("parallel",)), name=name)(src, dproj)


LRU_G = 384


def _lru_gates(xl_ref, wa_ref, wx_ref, ba_ref, bx_ref, lam_ref, g):
    sl = slice(LRU_G * g, LRU_G * g + LRU_G)
    xg = xl_ref[:, sl]
    xm = _mx(xg)
    pa = _dot(xm, wa_ref[g]) + ba_ref[:, sl]
    r = jnp.where(pa < -12.0, jnp.exp(pa), _sigmoid(pa))
    ig = _sigmoid(_dot(xm, wx_ref[g]) + bx_ref[:, sl])
    sp = _softplus(-lam_ref[:, sl])
    log_a = (-LRU_C * r) * sp
    a = jnp.exp(log_a)
    mult = jnp.sqrt(_one_minus_sq(log_a, a))
    return sl, xg, r, ig, sp, a, mult


def _lru_fwd(xl, proj, wa, wx, ba, bx, lam):
    s = xl.shape[0]
    ts = _tile(s, 512)
    w = LRU_W

    def body(xl_ref, lg_ref, wa_ref, wx_ref, ba_ref, bx_ref, lam_ref, y_ref, hs_ref, a_ref, u_ref, carry_ref):
        @pl.when(pl.program_id(0) == 0)
        def _():
            carry_ref[...] = jnp.zeros_like(carry_ref)

        for g in range(4):
            sl, xg, _, ig, _, a, mult = _lru_gates(xl_ref, wa_ref, wx_ref, ba_ref, bx_ref, lam_ref, g)
            a_ref[:, sl] = a
            u_ref[:, sl] = mult * (ig * xg)

        def step(t, h):
            h = a_ref[pl.ds(t, 1), :] * h + u_ref[pl.ds(t, 1), :]
            hs_ref[pl.ds(t, 1), :] = h
            return h

        carry_ref[0:1, :] = lax.fori_loop(0, ts, step, carry_ref[0:1, :], unroll=8)
        y_ref[...] = (hs_ref[...] * _silu(lg_ref[...])).astype(y_ref.dtype)

    row = pl.BlockSpec((ts, w), lambda i: (i, 0))
    vec = pl.BlockSpec((1, w), lambda i: (0, 0))
    wsp = pl.BlockSpec((4, LRU_G, LRU_G), lambda i: (0, 0, 0))
    return pl.pallas_call(
        body, grid=(s // ts,),
        in_specs=[row, pl.BlockSpec((ts, w), lambda i: (i, LG[0] // w)), wsp, wsp, vec, vec, vec],
        out_specs=[row, row],
        out_shape=[jax.ShapeDtypeStruct((s, w), _MXU), jax.ShapeDtypeStruct((s, w), F32)],
        scratch_shapes=[pltpu.VMEM((ts, w), F32), pltpu.VMEM((ts, w), F32), pltpu.VMEM((8, w), F32)],
        compiler_params=_cparams(("arbitrary",)), name="lru_fwd")(xl, proj, wa, wx, ba, bx, lam)


def _lru_bwd(xl, proj, hs, dy, dproj, wa, wx, ba, bx, lam, ride=None):
    s = xl.shape[0]
    ts = _tile(s, 256)
    nt = s // ts
    w = LRU_W
    hb = ts // 8

    def body(xl_ref, lg_ref, hs_ref, hprev_ref, dy_ref, dproj_hbm, wa_ref, wx_ref, ba_ref, bx_ref, lam_ref,
             dlg_ref, dxl_ref, dwa_ref, dwx_ref, dba_ref, dbx_ref, dlam_ref,
             a_ref, dh_ref, ext_ref, carry_ref, r_ref, ig_ref, mult_ref):
        del dproj_hbm
        i = pl.program_id(0)

        @pl.when(i == 0)
        def _():
            carry_ref[...] = jnp.zeros_like(carry_ref)
            for ref in (dwa_ref, dwx_ref, dba_ref, dbx_ref, dlam_ref):
                ref[...] = jnp.zeros_like(ref)

        lg = lg_ref[...]
        dyv = dy_ref[...]
        dh_ref[...] = dyv * _silu(lg)
        dlg_ref[...] = (dyv * hs_ref[...] * _dsilu(lg)).astype(dlg_ref.dtype)
        for g in range(4):
            sl, _, r, ig, _, a, mult = _lru_gates(xl_ref, wa_ref, wx_ref, ba_ref, bx_ref, lam_ref, g)
            a_ref[:, sl] = a
            r_ref[:, sl] = r
            ig_ref[:, sl] = ig
            mult_ref[:, sl] = mult

        def step(k, carry):
            t = ts - 1 - k
            dh = dh_ref[pl.ds(t, 1), :] + carry
            dh_ref[pl.ds(t, 1), :] = dh
            return a_ref[pl.ds(t, 1), :] * dh

        carry_ref[0:1, :] = lax.fori_loop(0, ts, step, carry_ref[0:1, :], unroll=8)

        ext_ref[0:8, :] = jnp.where(i == nt - 1, 0.0, 1.0) * hprev_ref[...]
        ext_ref[8:8 + ts, :] = hs_ref[...]
        for g in range(4):
            sl = slice(LRU_G * g, LRU_G * g + LRU_G)
            xg, r, ig, a, mult = xl_ref[:, sl], r_ref[:, sl], ig_ref[:, sl], a_ref[:, sl], mult_ref[:, sl]
            sp = _softplus(-lam_ref[:, sl])
            dh = dh_ref[:, sl]
            da = dh * ext_ref[7:7 + ts, sl]
            dmult = dh * ig * xg
            di = dh * mult * xg
            dxl = dh * mult * ig
            dlog_a = da * a - dmult * (a * a) * lax.rsqrt(mult * mult)
            dlam_ref[:, sl] += jnp.sum(dlog_a * r, axis=0, keepdims=True) * (LRU_C * _sigmoid(-lam_ref[:, sl]))
            dpa = dlog_a * (-LRU_C * sp) * r * (1.0 - r)
            dpx = di * ig * (1.0 - ig)
            dba_ref[:, sl] += jnp.sum(dpa, axis=0, keepdims=True)
            dbx_ref[:, sl] += jnp.sum(dpx, axis=0, keepdims=True)
            dpa_m, dpx_m, xm = _mx(dpa), _mx(dpx), _mx(xg)
            dxl_ref[:, sl] = dxl + _dot_nt(dpa_m, wa_ref[g]) + _dot_nt(dpx_m, wx_ref[g])
            dwa_ref[g] += _dot_tn(xm, dpa_m)
            dwx_ref[g] += _dot_tn(xm, dpx_m)

    rev = lambda i: nt - 1 - i
    row = pl.BlockSpec((ts, w), lambda i: (rev(i), 0))
    vec = pl.BlockSpec((1, w), lambda i: (0, 0))
    wsp = pl.BlockSpec((4, LRU_G, LRU_G), lambda i: (0, 0, 0))
    lgs = pl.BlockSpec((ts, w), lambda i: (rev(i), LG[0] // w))
    return _pcall(
        body, ride, (xl, proj, hs, hs, dy, dproj, wa, wx, ba, bx, lam), grid=(nt,),
        in_specs=[row, lgs, row, pl.BlockSpec((8, w), lambda i: (jnp.maximum(rev(i) * hb - 1, 0), 0)), row,
                  pl.BlockSpec(memory_space=pl.ANY), wsp, wsp, vec, vec, vec],
        out_specs=[lgs, row, wsp, wsp, vec, vec, vec],
        out_shape=[jax.ShapeDtypeStruct(dproj.shape, dproj.dtype), jax.ShapeDtypeStruct((s, w), F32),
                   jax.ShapeDtypeStruct((4, LRU_G, LRU_G), F32), jax.ShapeDtypeStruct((4, LRU_G, LRU_G), F32),
                   jax.ShapeDtypeStruct((1, w), F32), jax.ShapeDtypeStruct((1, w), F32),
                   jax.ShapeDtypeStruct((1, w), F32)],
        scratch_shapes=[pltpu.VMEM((ts, w), F32), pltpu.VMEM((ts, w), F32), pltpu.VMEM((ts + 8, w), F32),
                        pltpu.VMEM((8, w), F32), pltpu.VMEM((ts, w), F32), pltpu.VMEM((ts, w), F32),
                        pltpu.VMEM((ts, w), F32)],
        aliases={5: 0}, sem=("arbitrary",), name="lru_bwd")


def _mem_kv_fwd(mem, g, wkv):
    m = mem.shape[0]

    def body(mem_ref, g_ref, w_ref, k_ref, v_ref, mn_ref):
        mv = mem_ref[...]
        r = lax.rsqrt(jnp.mean(mv * mv, axis=-1, keepdims=True) + EPS)
        mn = _mx(mv * r * g_ref[...])
        mn_ref[...] = mn
        kv = _dot(mn, w_ref[...])
        k_ref[...] = kv[:, 0:D].astype(k_ref.dtype)
        v_ref[...] = kv[:, D:2 * D].astype(v_ref.dtype)

    sh = jax.ShapeDtypeStruct((m, D), _MXU)
    return pl.pallas_call(body, out_shape=[sh, sh, sh], compiler_params=_cparams(None), name="mem_kv_fwd")(mem, g, wkv)


def _mem_kv_bwd(mem, g, mn, wkv, dk, dv):
    m = mem.shape[0]

    def body(mem_ref, g_ref, mn_ref, w_ref, dk_ref, dv_ref, dw_ref, dg_ref):
        dkv = _mx(jnp.concatenate([dk_ref[...], dv_ref[...]], axis=1))
        dw_ref[...] = _dot_tn(mn_ref[...], dkv).astype(dw_ref.dtype)
        dmn = _dot_nt(dkv, w_ref[...])
        mv = mem_ref[...]
        r = lax.rsqrt(jnp.mean(mv * mv, axis=-1, keepdims=True) + EPS)
        dg_ref[...] = jnp.sum(dmn * mv * r, axis=0, keepdims=True)

    del m
    return pl.pallas_call(
        body, out_shape=[jax.ShapeDtypeStruct((D, 2 * D), _MXU), jax.ShapeDtypeStruct((1, D), F32)],
        compiler_params=_cparams(None), name="mem_kv_bwd")(mem, g, mn, wkv, dk, dv)


def _attn_probs(q_ref, k_ref, hd):
    sl = slice(MEM_HD * hd, MEM_HD * hd + MEM_HD)
    qh = _mx(q_ref[:, sl])
    sc = _dot_nt(qh, k_ref[:, sl]) * (MEM_HD ** -0.5)
    e = jnp.exp(sc - jnp.max(sc, axis=-1, keepdims=True))
    return sl, qh, e / jnp.sum(e, axis=-1, keepdims=True)


def _attn_fwd(proj, k, v):
    s = proj.shape[0]
    m = k.shape[0]
    ts = _tile(s, 1024)

    def body(q_ref, k_ref, v_ref, y_ref):
        for hd in range(MEM_HEADS):
            sl, _, p = _attn_probs(q_ref, k_ref, hd)
            y_ref[:, sl] = _dot(_mx(p), v_ref[:, sl]).astype(y_ref.dtype)

    kvs = pl.BlockSpec((m, D), lambda i: (0, 0))
    return pl.pallas_call(
        body, grid=(s // ts,),
        in_specs=[pl.BlockSpec((ts, D), lambda i: (i, Q[0] // D)), kvs, kvs],
        out_specs=pl.BlockSpec((ts, D), lambda i: (i, 0)),
        out_shape=jax.ShapeDtypeStruct((s, D), _MXU),
        compiler_params=_cparams(("parallel",)), name="attn_fwd")(proj, k, v)


def _attn_bwd(proj, k, v, dy, dproj):
    s = proj.shape[0]
    m = k.shape[0]
    ts = _tile(s, 1024)

    def body(q_ref, k_ref, v_ref, dy_ref, dproj_hbm, dq_ref, dk_ref, dv_ref):
        del dproj_hbm

        @pl.when(pl.program_id(0) == 0)
        def _():
            dk_ref[...] = jnp.zeros_like(dk_ref)
            dv_ref[...] = jnp.zeros_like(dv_ref)

        for hd in range(MEM_HEADS):
            sl, qh, p = _attn_probs(q_ref, k_ref, hd)
            dyh = _mx(dy_ref[:, sl])
            dp = _dot_nt(dyh, v_ref[:, sl])
            ds = _mx(p * (dp - jnp.sum(dp * p, axis=-1, keepdims=True)) * (MEM_HD ** -0.5))
            dq_ref[:, sl] = _dot(ds, k_ref[:, sl]).astype(dq_ref.dtype)
            dk_ref[:, sl] += _dot_tn(ds, qh)
            dv_ref[:, sl] += _dot_tn(_mx(p), dyh)

    kvs = pl.BlockSpec((m, D), lambda i: (0, 0))
    qs = pl.BlockSpec((ts, D), lambda i: (i, Q[0] // D))
    return pl.pallas_call(
        body, grid=(s // ts,),
        in_specs=[qs, kvs, kvs, pl.BlockSpec((ts, D), lambda i: (i, 0)), pl.BlockSpec(memory_space=pl.ANY)],
        out_specs=[qs, kvs, kvs],
        out_shape=[jax.ShapeDtypeStruct(dproj.shape, dproj.dtype), jax.ShapeDtypeStruct((m, D), F32),
                   jax.ShapeDtypeStruct((m, D), F32)],
        input_output_aliases={4: 0},
        compiler_params=_cparams(("arbitrary",)), name="attn_bwd")(proj, k, v, dy, dproj)


def _merge_fb(x, target, yssd, ylru, ymem, proj, wbs, wbl, wbm, wo, fg):
    s = x.shape[0]
    ts = _tile(s, 256)

    def body(x_ref, t_ref, ys_ref, yl_ref, ym_ref, gl_ref, wbs_ref, wbl_ref, wbm_ref, wo_ref, fg_ref,
             dgl_ref, dx2_ref, dx2m_ref, mg_ref, db0_ref, db1_ref, db2_ref, loss_ref, dfg_ref):
        @pl.when(pl.program_id(0) == 0)
        def _():
            loss_ref[...] = jnp.zeros_like(loss_ref)
            dfg_ref[...] = jnp.zeros_like(dfg_ref)

        bs = (_dot(ys_ref[...], wbs_ref[...]), _dot(yl_ref[...], wbl_ref[...]), _dot(ym_ref[...], wbm_ref[...]))
        gates = [_sigmoid(gl_ref[:, D * n:D * n + D]) for n in range(3)]
        merged = gates[0] * bs[0] + gates[1] * bs[1] + gates[2] * bs[2]
        mg = _mx(merged)
        mg_ref[...] = mg
        x2 = x_ref[...] + _dot(mg, wo_ref[...])
        r = lax.rsqrt(jnp.mean(x2 * x2, axis=-1, keepdims=True) + EPS)
        xhat = x2 * r
        err = xhat * fg_ref[...] - t_ref[...]
        loss_ref[...] += jnp.sum(err * err, axis=0, keepdims=True) * (0.5 / D)
        dy = err * (1.0 / D)
        dfg_ref[...] += jnp.sum(dy * xhat, axis=0, keepdims=True)
        dxh = dy * fg_ref[...]
        dx2 = r * (dxh - xhat * jnp.mean(dxh * xhat, axis=-1, keepdims=True))
        dx2_ref[...] = dx2
        dx2m = _mx(dx2)
        dx2m_ref[...] = dx2m
        dmg = _dot_nt(dx2m, wo_ref[...])
        for n, db_ref in enumerate((db0_ref, db1_ref, db2_ref)):
            gt = gates[n]
            dgl_ref[:, D * n:D * n + D] = (dmg * bs[n] * gt * (1.0 - gt)).astype(dgl_ref.dtype)
            db_ref[...] = (dmg * gt).astype(db_ref.dtype)

    row = lambda w: pl.BlockSpec((ts, w), lambda i: (i, 0))
    full = lambda a: pl.BlockSpec(a.shape, lambda i: (0, 0))
    vec = pl.BlockSpec((1, D), lambda i: (0, 0))
    gls = pl.BlockSpec((ts, GL[1]), lambda i: (i, GL[0] // GL[1]))
    act = jax.ShapeDtypeStruct((s, D), _MXU)
    return pl.pallas_call(
        body, grid=(s // ts,),
        in_specs=[row(D), row(D), row(SSD_W), row(LRU_W), row(D), gls, full(wbs), full(wbl), full(wbm), full(wo), vec],
        out_specs=[gls, row(D), row(D), row(D), row(D), row(D), row(D), vec, vec],
        out_shape=[jax.ShapeDtypeStruct((s, NP), _MXU), jax.ShapeDtypeStruct((s, D), F32), act, act, act, act, act,
                   jax.ShapeDtypeStruct((1, D), F32), jax.ShapeDtypeStruct((1, D), F32)],
        compiler_params=_cparams(("arbitrary",)), name="merge_fwd_bwd")(
            x, target, yssd, ylru, ymem, proj, wbs, wbl, wbm, wo, fg)


def _adamw(w, g, m, v, name):
    rows, cols = w.shape
    tr = _tile(rows, 512, 8)

    def body(w_ref, g_ref, m_ref, v_ref, d_ref, mo_ref, vo_ref):
        gv = g_ref[...]
        mn = ADAM_B1 * m_ref[...] + (1.0 - ADAM_B1) * gv
        vn = ADAM_B2 * v_ref[...] + (1.0 - ADAM_B2) * (gv * gv)
        m_hat = mn / (1.0 - ADAM_B1 ** ADAM_STEP)
        v_hat = vn / (1.0 - ADAM_B2 ** ADAM_STEP)
        d_ref[...] = -ADAM_LR * (m_hat / (jnp.sqrt(v_hat) + ADAM_EPS) + ADAM_WD * w_ref[...])
        mo_ref[...] = mn
        vo_ref[...] = vn

    blk = pl.BlockSpec((tr, cols), lambda i: (i, 0))
    sh = jax.ShapeDtypeStruct((rows, cols), F32)
    return pl.pallas_call(
        body, grid=(rows // tr,), in_specs=[blk] * 4, out_specs=[blk] * 3, out_shape=[sh] * 3,
        compiler_params=_cparams(("parallel",)), name=name)(w, g, m, v)


def _mesh_pos():
    x, y, c = lax.axis_index("x"), lax.axis_index("y"), lax.axis_index("c")
    chips = [(1 - x, y), (x, 1 - y), (1 - x, 1 - y)]
    return x, y, c, 2 * x + y, chips


def _hbm():
    return pl.BlockSpec(memory_space=pl.ANY)


def _remote(src, dst, send_sem, recv_sem, dev):
    return pltpu.make_async_remote_copy(src_ref=src, dst_ref=dst, send_sem=send_sem, recv_sem=recv_sem,
                                        device_id=dev, device_id_type=MESH)


def _sems(n):
    return [pltpu.SemaphoreType.DMA((n,)), pltpu.SemaphoreType.DMA((n,))]


class _Exchange:
    def __init__(self, inputs, out_shape, n_sem, start, finish, aliases=None):
        self.inputs, self.out_shape, self.n_sem = list(inputs), list(out_shape), n_sem
        self.start, self.finish, self.aliases = start, finish, dict(aliases or {})


def _run_exchange(ex, name):
    n_in, n_out = len(ex.inputs), len(ex.out_shape)

    def body(*refs):
        srcs, outs = refs[:n_in], refs[n_in:n_in + n_out]
        send_sems, recv_sems = refs[n_in + n_out:]
        ex.start(srcs, outs, send_sems, recv_sems)
        ex.finish(srcs, outs, send_sems, recv_sems)

    return pl.pallas_call(
        body, in_specs=[_hbm()] * n_in, out_specs=[_hbm()] * n_out, out_shape=ex.out_shape,
        input_output_aliases=ex.aliases, scratch_shapes=_sems(ex.n_sem), name=name)(*ex.inputs)


def _pcall(body, ride, args, *, grid, in_specs, out_specs, out_shape, scratch_shapes, sem, name, aliases=None):
    in_specs, out_specs, out_shape = list(in_specs), list(out_specs), list(out_shape)
    scratch_shapes, aliases = list(scratch_shapes), dict(aliases or {})
    if ride is None:
        outs = pl.pallas_call(
            body, grid=grid, in_specs=in_specs, out_specs=out_specs, out_shape=out_shape, scratch_shapes=scratch_shapes,
            input_output_aliases=aliases, compiler_params=_cparams(sem), name=name)(*args)
        return outs, None
    n_in, n_out, n_scr = len(in_specs), len(out_shape), len(scratch_shapes)
    e_in, e_out = len(ride.inputs), len(ride.out_shape)

    def carried(*refs):
        cut = [n_in, e_in, n_out, e_out, n_scr]
        parts, p = [], 0
        for c in cut:
            parts.append(refs[p:p + c])
            p += c
        ins, e_ins, outs, e_outs, scr = parts
        send_sems, recv_sems = refs[p], refs[p + 1]
        first = last = None
        for d, size in enumerate(grid):
            i = pl.program_id(d)
            first = (i == 0) if first is None else jnp.logical_and(first, i == 0)
            last = (i == size - 1) if last is None else jnp.logical_and(last, i == size - 1)

        @pl.when(first)
        def _():
            ride.start(e_ins, e_outs, send_sems, recv_sems)

        body(*ins, *outs, *scr)

        @pl.when(last)
        def _():
            ride.finish(e_ins, e_outs, send_sems, recv_sems)

    for k, v in ride.aliases.items():
        aliases[n_in + k] = n_out + v
    res = pl.pallas_call(
        carried, grid=grid, in_specs=in_specs + [_hbm()] * e_in, out_specs=out_specs + [_hbm()] * e_out,
        out_shape=out_shape + ride.out_shape, scratch_shapes=scratch_shapes + _sems(ride.n_sem),
        input_output_aliases=aliases, compiler_params=_cparams(("arbitrary",) * len(grid)),
        name=name)(*args, *ride.inputs)
    return res[:n_out], res[n_out:]


def _gather_shards(arrs, split, relay):
    n = len(arrs)
    n_sem = sum(6 if sp else 3 for sp in split)

    def rows(i, which):
        if not split[i]:
            return pl.ds(0, arrs[i].shape[0])
        half = arrs[i].shape[0] // 2
        return pl.ds(which * half, half)

    def sends(srcs, outs, send_sems, recv_sems):
        _, _, c, me, chips = _mesh_pos()
        return [_remote(srcs[i].at[rows(i, c)], outs[i].at[me, rows(i, c)], send_sems.at[3 * i + j],
                        recv_sems.at[3 * i + j], (cx, cy, c))
                for i in range(n) for j, (cx, cy) in enumerate(chips) if not (relay and split[i] and j == 2)]

    def start(srcs, outs, send_sems, recv_sems):
        for cp in sends(srcs, outs, send_sems, recv_sems):
            cp.start()

    def finish(srcs, outs, send_sems, recv_sems):
        x, y, c, _, chips = _mesh_pos()
        sib = (x, y, 1 - c)
        first = ((x + 1 - c) % 2, (y + c) % 2)
        other = ((x + c) % 2, (y + 1 - c) % 2)
        started, k = [], 3 * n
        for i in range(n):
            sem = lambda j, i=i: (send_sems.at[3 * i + j], recv_sems.at[3 * i + j])
            if not split[i]:
                for j, (cx, cy) in enumerate(chips):
                    slot = outs[i].at[2 * cx + cy]
                    _remote(slot, slot, *sem(j), (cx, cy, c)).wait_recv()
                continue
            to_sib = lambda j, k=k: (send_sems.at[k + j], recv_sems.at[k + j])
            slot = lambda chip, which, i=i: outs[i].at[2 * chip[0] + chip[1], rows(i, which)]
            got = slot(first, c)
            _remote(got, got, *sem(c), (*first, c)).wait_recv()
            if relay:
                started.append(_remote(got, got, *sem(2), (*other, c)))
                started[-1].start()
            started.append(_remote(got, got, *to_sib(c), sib))
            started[-1].start()
            for chip, j_in, j_sib in ((other, 1 - c, 1 - c), (chips[2], 2, 2)):
                got = slot(chip, c)
                _remote(got, got, *sem(j_in), (*chip, c)).wait_recv()
                started.append(_remote(got, got, *to_sib(j_sib), sib))
                started[-1].start()
            for chip, j_sib in ((first, c), (other, 1 - c), (chips[2], 2)):
                theirs = slot(chip, 1 - c)
                _remote(theirs, theirs, *to_sib(j_sib), sib).wait_recv()
            k += 3
        for cp in sends(srcs, outs, send_sems, recv_sems) + started:
            cp.wait_send()

    return _Exchange(arrs, [jax.ShapeDtypeStruct((NSHARD,) + a.shape, a.dtype) for a in arrs], n_sem, start, finish)


def _with_own_slot(arrs, got):
    own_slot = jnp.arange(NSHARD, dtype=jnp.int32)[:, None, None] == 2 * lax.axis_index("x") + lax.axis_index("y")
    return [jnp.where(own_slot, a[None], g) for a, g in zip(arrs, got)]


def _swap_halves(arrs):
    n = len(arrs)

    def copies(srcs, outs, send_sems, recv_sems):
        x, y, c, _, _ = _mesh_pos()
        cps = []
        for i in range(n):
            half = arrs[i].shape[1] // 2
            cps.append(_remote(srcs[i].at[:, pl.ds((1 - c) * half, half)], outs[i], send_sems.at[i], recv_sems.at[i],
                               (x, y, 1 - c)))
        return cps

    def start(*refs):
        for cp in copies(*refs):
            cp.start()

    def finish(*refs):
        for cp in copies(*refs):
            cp.wait()

    shapes = [jax.ShapeDtypeStruct((NSHARD, a.shape[1] // 2, a.shape[2]), a.dtype) for a in arrs]
    return _Exchange(arrs, shapes, n, start, finish)


def _scatter_chips(arrs):
    n = len(arrs)

    def copies(srcs, outs, send_sems, recv_sems):
        _, _, c, me, chips = _mesh_pos()
        own = [pltpu.make_async_copy(srcs[i].at[me], outs[i].at[me], send_sems.at[3 * n + i]) for i in range(n)]
        cps = [_remote(srcs[i].at[2 * cx + cy], outs[i].at[me], send_sems.at[3 * i + j], recv_sems.at[3 * i + j],
                       (cx, cy, c)) for i in range(n) for j, (cx, cy) in enumerate(chips)]
        return own, cps

    def start(*refs):
        own, cps = copies(*refs)
        for cp in own + cps:
            cp.start()

    def finish(srcs, outs, send_sems, recv_sems):
        _, _, c, _, chips = _mesh_pos()
        for i in range(n):
            for j, (cx, cy) in enumerate(chips):
                slot = outs[i].at[2 * cx + cy]
                _remote(slot, slot, send_sems.at[3 * i + j], recv_sems.at[3 * i + j], (cx, cy, c)).wait_recv()
        own, cps = copies(srcs, outs, send_sems, recv_sems)
        for cp in cps:
            cp.wait_send()
        for cp in own:
            cp.wait()

    return _Exchange(arrs, [jax.ShapeDtypeStruct(a.shape, a.dtype) for a in arrs], 4 * n, start, finish)


def _share_halves(arrs):
    n = len(arrs)

    def copies(outs, send_sems, recv_sems):
        x, y, c, _, _ = _mesh_pos()
        return [_remote(outs[i].at[c], outs[i].at[c], send_sems.at[i], recv_sems.at[i], (x, y, 1 - c))
                for i in range(n)]

    def start(srcs, outs, send_sems, recv_sems):
        del srcs
        for cp in copies(outs, send_sems, recv_sems):
            cp.start()

    def finish(srcs, outs, send_sems, recv_sems):
        del srcs
        x, y, c, _, _ = _mesh_pos()
        for i in range(n):
            theirs = outs[i].at[1 - c]
            _remote(theirs, theirs, send_sems.at[i], recv_sems.at[i], (x, y, 1 - c)).wait_recv()
        for cp in copies(outs, send_sems, recv_sems):
            cp.wait_send()

    return _Exchange(arrs, [jax.ShapeDtypeStruct(a.shape, a.dtype) for a in arrs], n, start, finish,
                     aliases={i: i for i in range(n)})


def _gather_small(full):
    _, width = full.shape

    def copies(srcs, outs, send_sems, recv_sems):
        _, _, c, me, chips = _mesh_pos()
        mine = srcs[0].at[pl.ds(0, SMALL_ROWS)]
        own = pltpu.make_async_copy(mine, outs[0].at[me], send_sems.at[3])
        return own, [_remote(mine, outs[0].at[me], send_sems.at[j], recv_sems.at[j], (cx, cy, c))
                     for j, (cx, cy) in enumerate(chips)]

    def start(*refs):
        own, cps = copies(*refs)
        for cp in [own] + cps:
            cp.start()

    def finish(srcs, outs, send_sems, recv_sems):
        _, _, c, _, chips = _mesh_pos()
        for j, (cx, cy) in enumerate(chips):
            slot = outs[0].at[2 * cx + cy]
            _remote(slot, slot, send_sems.at[j], recv_sems.at[j], (cx, cy, c)).wait_recv()
        own, cps = copies(srcs, outs, send_sems, recv_sems)
        for cp in cps:
            cp.wait_send()
        own.wait()

    return _Exchange([full], [jax.ShapeDtypeStruct((NSHARD, SMALL_ROWS, width), full.dtype)], 4, start, finish)


def _add_sibling(mine, recv, c, name):
    _, half, width = recv.shape
    tr = _tile(half, 1024, 16)
    nb = half // tr

    def body(c_ref, a_ref, b_ref, o_ref):
        del c_ref
        o_ref[...] = (a_ref[...].astype(F32) + b_ref[...].astype(F32)).astype(o_ref.dtype)

    grid_spec = pltpu.PrefetchScalarGridSpec(
        num_scalar_prefetch=1, grid=(NSHARD, nb),
        in_specs=[pl.BlockSpec((1, tr, width), lambda j, r, c_ref: (j, c_ref[0] * nb + r, 0)),
                  pl.BlockSpec((1, tr, width), lambda j, r, c_ref: (j, r, 0))],
        out_specs=pl.BlockSpec((1, tr, width), lambda j, r, c_ref: (j, r, 0)))
    return pl.pallas_call(
        body, grid_spec=grid_spec, out_shape=jax.ShapeDtypeStruct(recv.shape, recv.dtype),
        compiler_params=_cparams(("parallel", "parallel")), name=name)(c, mine, recv)


def _sum_chips(parts, c, name):
    _, half, width = parts.shape
    tr = _tile(half, 1024, 16)

    def body(c_ref, p_ref, o_ref):
        del c_ref
        p = [p_ref[j].astype(F32) for j in range(NSHARD)]
        o_ref[0] = ((p[0] + p[1]) + p[2]) + p[3]

    grid_spec = pltpu.PrefetchScalarGridSpec(
        num_scalar_prefetch=1, grid=(half // tr,),
        in_specs=[pl.BlockSpec((NSHARD, tr, width), lambda r, c_ref: (0, r, 0))],
        out_specs=pl.BlockSpec((1, tr, width), lambda r, c_ref: (c_ref[0], r, 0)))
    return pl.pallas_call(
        body, grid_spec=grid_spec, out_shape=jax.ShapeDtypeStruct((2, half, width), F32),
        compiler_params=_cparams(("parallel",)), name=name)(c, parts)


def _unpack(flat, names, shapes):
    out, off = {}, 0
    for n in names:
        sz = _size(shapes[n])
        out[n] = flat[off:off + sz].reshape(shapes[n])
        off += sz
    return out


W_IN_COLS = 3080
W_IN_PAD = 3136


def _reorder_w_in_t(w):
    return jnp.concatenate([w[2048:5120], w[9248:12320], w[0:2048], w[8224:9248], w[5152:6688], w[6688:8224],
                            w[5120:5152], jnp.zeros((NP - 12320, D), w.dtype)], axis=0)


def _restore_w_in_t(g):
    return jnp.concatenate([g[6144:8192], g[0:3072], g[12288:12320], g[9216:10752], g[10752:12288], g[8192:9216],
                            g[3072:6144]], axis=0)


def _lru_group_weights(w):
    w4 = w.reshape(4, 4, 96, 96)
    eye = jnp.eye(4, dtype=w.dtype)
    return (w4[:, :, None, :, :] * eye[None, :, :, None, None]).transpose(0, 1, 3, 2, 4).reshape(4, LRU_G, LRU_G)


def _lru_group_blocks(g):
    g5 = g.reshape(4, 4, 96, 4, 96)
    return jnp.stack([g5[:, a, :, a, :] for a in range(4)], axis=1).reshape(16, 96, 96)


def _spread(a):
    return a.transpose(1, 0, 2).reshape(a.shape[1], NSHARD * a.shape[2])


def _split(a):
    return a.reshape(a.shape[0], NSHARD, a.shape[1] // NSHARD).transpose(1, 0, 2)


class _Reduction:
    def __init__(self, dist, parts, names):
        self.c, self.parts, self.names = dist.c, parts, names

    def swap(self):
        return _swap_halves(self.parts)

    def scatter(self, recv):
        return _scatter_chips([_add_sibling(p, r, self.c, "add_sibling_" + n)
                               for p, r, n in zip(self.parts, recv, self.names)])

    def share(self, landed):
        return _share_halves([_sum_chips(a, self.c, "sum_chips_" + n) for a, n in zip(landed, self.names)])

    def done(self, shared):
        return [a.reshape(2 * a.shape[1], a.shape[2]) for a in shared]


class _Dist:
    def __init__(self, w_in_shard, late_shards):
        self.c = lax.axis_index("c").astype(jnp.int32).reshape(1)
        self.w_in_shard = [w_in_shard]
        self.late_shards = late_shards

    def w_in_ride(self):
        return _gather_shards(self.w_in_shard, [True], relay=True)

    def w_in_arrived(self, got):
        (g_in,) = _with_own_slot(self.w_in_shard, got)
        return _reorder_w_in_t(g_in[:, 0:W_IN_COLS].reshape(NSHARD * W_IN_COLS, D))

    def weights_ride(self):
        return _gather_shards(self.late_shards, [True, True, False], relay=False)

    def weights_arrived(self, got):
        g_kv, g_rows, g_small = _with_own_slot(self.late_shards, got)
        out = {"w_kv": _spread(g_kv)}
        for n, lo_, hi_ in ROW_PIECES:
            out[n] = g_rows[:, lo_:hi_].reshape(NSHARD * (hi_ - lo_), D)
        out["ssd_conv_w"] = _spread(g_small[:, :, 0:768])
        out["ssd_norm_g"] = _spread(g_small[:, :, 768:896])
        out["lru_conv_w"] = _spread(g_small[:, :, 896:1280])
        return out

    def early_parts(self, grads):
        rows = jnp.concatenate([grads[n].reshape(NSHARD, hi_ - lo_, D) for n, lo_, hi_ in ROW_PIECES], axis=1)
        return [_split(grads["w_kv"]), rows]

    def late_parts(self, grads):
        rows = _restore_w_in_t(grads["w_in_rt"]).reshape(NSHARD, W_IN_COLS, D)
        return [jnp.pad(rows, ((0, 0), (0, W_IN_PAD - W_IN_COLS), (0, 0)))]


def _local_grads(x, mem, target, wts, dist=None):
    pad128 = lambda a: jnp.pad(a, ((0, 0), (0, 128 - a.shape[1])))

    if dist is None:
        (h,), _ = _norm_fwd(x, wts["norm_g"])
        w_in_rt = wts["w_in_rt"]
        proj = _mm(h, w_in_rt, F32, "in_proj", tb=True, tn=NP_TILE)
    else:
        (h,), arrived = _norm_fwd(x, wts["norm_g"], ride=dist.w_in_ride())
        w_in_rt = dist.w_in_arrived(arrived)
        proj, arrived = _mm(h, w_in_rt, F32, "in_proj", tb=True, tn=NP_TILE, ride=dist.weights_ride())
        wts = dict(wts, **dist.weights_arrived(arrived))
    wbs, wbl, wbm, wo, wkv = wts["w_br_ssd"], wts["w_br_lru"], wts["w_br_mem"], wts["w_out"], wts["w_kv"]
    wa, wx = _mx(_lru_group_weights(wts["lru_w_a"])), _mx(_lru_group_weights(wts["lru_w_x"]))
    ba, bx = wts["lru_b_a"].reshape(1, LRU_W), wts["lru_b_x"].reshape(1, LRU_W)
    dtb, alog = pad128(wts["ssd_dt_bias"]), pad128(wts["ssd_a_log"])
    dexp = jnp.repeat(wts["ssd_d"], 64, axis=1)
    ng = wts["ssd_norm_g"].reshape(1, SSD_W)
    xbc = _conv_fwd(proj, XBC, wts["ssd_conv_w"], wts["ssd_conv_b"], True, "ssd_conv_fwd")
    yssd, yraw, hprev = _ssd_fwd(xbc, proj, dtb, alog, dexp, ng)
    xl = _conv_fwd(proj, LX, wts["lru_conv_w"], wts["lru_conv_b"], False, "lru_conv_fwd")
    ylru, hs = _lru_fwd(xl, proj, wa, wx, ba, bx, wts["lru_lambda"])
    kk, vv, mn = _mem_kv_fwd(mem, wts["mem_norm_g"], wkv)
    ymem = _attn_fwd(proj, kk, vv)

    dproj, dx2, dx2m, merged, db0, db1, db2, loss_vec, dfg = _merge_fb(
        x, target, yssd, ylru, ymem, proj, wbs, wbl, wbm, wo, wts["final_g"].reshape(1, D))
    grads = {"final_g": dfg.reshape(D)}
    grads["w_out"] = _mm(merged, dx2m, _MXU, "dw_out", ta=True)
    grads["w_br_ssd"] = _mm(yssd, db0, _MXU, "dw_br_ssd", ta=True)
    grads["w_br_lru"] = _mm(ylru, db1, _MXU, "dw_br_lru", ta=True)
    grads["w_br_mem"] = _mm(ymem, db2, _MXU, "dw_br_mem", ta=True)
    dyssd = _mm(db0, wbs, F32, "dy_ssd", tb=True)
    dylru = _mm(db1, wbl, F32, "dy_lru", tb=True)
    dymem = _mm(db2, wbm, F32, "dy_mem", tb=True)

    dproj, dk, dv = _attn_bwd(proj, kk, vv, dymem, dproj)
    grads["w_kv"], grads["mem_norm_g"] = _mem_kv_bwd(mem, wts["mem_norm_g"], mn, wkv, dk, dv)

    early = None if dist is None else _Reduction(dist, dist.early_parts(grads), ["w_kv", "rows"])

    (dproj, dxl, dwa, dwx, dba, dbx, dlam), got = _lru_bwd(
        xl, proj, hs, dylru, dproj, wa, wx, ba, bx, wts["lru_lambda"], ride=early and early.swap())
    grads["lru_w_a"] = _lru_group_blocks(dwa)[None]
    grads["lru_w_x"] = _lru_group_blocks(dwx)[None]
    grads["lru_b_a"], grads["lru_b_x"] = dba.reshape(1, 16, 96), dbx.reshape(1, 16, 96)
    grads["lru_lambda"] = dlam
    (grads["lru_conv_w"], grads["lru_conv_b"]), _ = _conv_bwd_w(
        proj, LX, wts["lru_conv_w"], wts["lru_conv_b"], dxl, False, "lru_conv_bwd_w")
    dproj = _conv_bwd_x(dxl, wts["lru_conv_w"], dproj, LX, "lru_conv_bwd_x")

    (dproj, ddt, dxbc, dng, dda, ddd, ddtb), got = _ssd_bwd(
        xbc, proj, yraw, hprev, dyssd, dproj, dtb, alog, dexp, ng, ride=early and early.scatter(got))
    dproj = _put_block(ddt, dproj, DT, "put_ddt")
    grads["ssd_norm_g"] = dng.reshape(4, 512)
    grads["ssd_dt_bias"] = ddtb[:, 0:32]
    grads["ssd_a_log"] = (dda * -jnp.exp(alog))[:, 0:32]
    grads["ssd_d"] = ddd.reshape(32, 64).sum(axis=1)[None, :]
    (dpre, grads["ssd_conv_w"], grads["ssd_conv_b"]), got = _conv_bwd_w(
        proj, XBC, wts["ssd_conv_w"], wts["ssd_conv_b"], dxbc, True, "ssd_conv_bwd_w", ride=early and early.share(got))
    reduced = {} if dist is None else dict(zip(["w_kv", "rows"], early.done(got)))
    dproj = _conv_bwd_x(dpre, wts["ssd_conv_w"], dproj, XBC, "ssd_conv_bwd_x")

    grads["w_in_rt"] = _mm(dproj, h, _MXU, "dw_in", ta=True, tm=NP_TILE, tn=1024)
    if dist is None:
        dh = _mm(dproj, w_in_rt, F32, "dh", tn=1024, tk=NP_TILE)
        (grad_x, grads["norm_g"]), _ = _norm_bwd(x, wts["norm_g"], dh, dx2)
    else:
        late = _Reduction(dist, dist.late_parts(grads), ["w_in"])
        got = _run_exchange(late.swap(), "swap_halves_w_in")
        dh, got = _mm(dproj, w_in_rt, F32, "dh", tn=1024, tk=NP_TILE, ride=late.scatter(got))
        (grad_x, grads["norm_g"]), _ = _norm_bwd(x, wts["norm_g"], dh, dx2)
        reduced["w_in"] = late.done(_run_exchange(late.share(got), "share_halves_w_in"))[0]
    return jnp.sum(loss_vec), grad_x, grads, reduced


def kernel(x, mem, norm_g, w_in, ssd_conv_w, ssd_conv_b, ssd_dt_bias, ssd_a_log, ssd_d, ssd_norm_g, lru_conv_w, lru_conv_b, lru_w_a, lru_b_a, lru_w_x, lru_b_x, lru_lambda, mem_norm_g, w_kv, w_br_ssd, w_br_lru, w_br_mem, w_out, final_g, loss_target, m_norm_g, m_w_in, m_ssd_conv_w, m_ssd_conv_b, m_ssd_dt_bias, m_ssd_a_log, m_ssd_d, m_ssd_norm_g, m_lru_conv_w, m_lru_conv_b, m_lru_w_a, m_lru_b_a, m_lru_w_x, m_lru_b_x, m_lru_lambda, m_mem_norm_g, m_w_kv, m_w_br_ssd, m_w_br_lru, m_w_br_mem, m_w_out, m_final_g, v_norm_g, v_w_in, v_ssd_conv_w, v_ssd_conv_b, v_ssd_dt_bias, v_ssd_a_log, v_ssd_d, v_ssd_norm_g, v_lru_conv_w, v_lru_conv_b, v_lru_w_a, v_lru_b_a, v_lru_w_x, v_lru_b_x, v_lru_lambda, v_mem_norm_g, v_w_kv, v_w_br_ssd, v_w_br_lru, v_w_br_mem, v_w_out, v_final_g):
    given = dict(locals())

    rows_w = jnp.concatenate([w_br_ssd[0], w_br_lru[0], w_br_mem[0], w_out[0]], axis=0)
    small_w = jnp.concatenate([ssd_conv_w[0], ssd_norm_g[0], lru_conv_w[0]], axis=1)
    w_in_t = jnp.pad(_mx(w_in[0].T), ((0, W_IN_PAD - W_IN_COLS), (0, 0)))
    dist = _Dist(w_in_t, [_mx(w_kv[0]), _mx(rows_w), small_w])
    wts = {n: given[n] for n in REPL}
    wts["lru_w_a"], wts["lru_w_x"] = lru_w_a[0], lru_w_x[0]

    loss_part, grad_x, grads, reduced = _local_grads(x[0], mem[0], loss_target[0], wts, dist)
    loss = lax.psum(loss_part, ("x", "y", "c"))

    repl_flat = jnp.concatenate([grads[n].reshape(-1) for n in REPL])
    repl_flat = jnp.pad(repl_flat, (0, NSHARD * SMALL_Q - repl_flat.shape[0])).reshape(NSHARD, SMALL_Q)
    shard_small = jnp.concatenate([_split(grads[n]).reshape(NSHARD, -1) for n in SMALL_SHARDED], axis=1)
    p_small = jnp.concatenate(
        [repl_flat, shard_small, jnp.zeros((NSHARD, SMALL_BUF_ROWS * PACK_W - SMALL_Q - 5120), F32)], axis=1)
    small = _Reduction(dist, [p_small.reshape(NSHARD, SMALL_BUF_ROWS, PACK_W)], ["small"])
    got = _run_exchange(small.swap(), "swap_halves_small")
    got = _run_exchange(small.scatter(got), "scatter_chips_small")
    got = _run_exchange(small.share(got), "share_halves_small")
    r_small = small.done(got)[0]
    repl_all = _run_exchange(_gather_small(r_small), "gather_small")[0].reshape(-1)

    g_shard = {"w_kv": reduced["w_kv"]}
    for n, lo_, hi_ in ROW_PIECES:
        g_shard[n] = reduced["rows"][lo_:hi_]
    g_shard.update(_unpack(r_small.reshape(-1)[SMALL_Q:], SMALL_SHARDED, SHARD_SHAPE))
    g_repl = _unpack(repl_all, REPL, REPL_SHAPE)

    out_g, out_d, out_m, out_v = {}, {}, {}, {}
    for n in WEIGHTS:
        w_full = given[n]
        if n == "w_in":
            g2 = reduced["w_in"][0:W_IN_COLS]
            d, mo, vo = _adamw(w_in[0].T, g2, m_w_in[0].T, v_w_in[0].T, "adamw_w_in")
            out_g[n], out_d[n], out_m[n], out_v[n] = [a.T[None] for a in (g2, d, mo, vo)]
            continue
        g = (g_shard[n] if n in SHARDED else g_repl[n]).reshape(w_full.shape)
        cols = w_full.shape[-1]
        as2d = lambda a: a.reshape(-1, cols)
        d, mo, vo = _adamw(as2d(w_full), as2d(g), as2d(given["m_" + n]), as2d(given["v_" + n]), "adamw_" + n)
        out_g[n] = g
        out_d[n], out_m[n], out_v[n] = d.reshape(w_full.shape), mo.reshape(w_full.shape), vo.reshape(w_full.shape)

    return (loss, grad_x[None], *[out_g[n] for n in WEIGHTS], *[out_d[n] for n in WEIGHTS],
            *[out_m[n] for n in WEIGHTS], *[out_v[n] for n in WEIGHTS])
```
